```python
import jax, jax.numpy as jnp
from jax import lax
import numpy as np

D_MODEL = 1024
BATCH = 8
SEQ = 4096
DEPTH = 4

N_META = 16
BLOCK = 128
WINDOW = 128
ROPE_THETA = 10000.0
EPS = 1e-6
NEG = -1e30
SWA_HEADS = 8
SWA_KV_HEADS = 2
SWA_HEAD_DIM = 64
SWA_GROUP = SWA_HEADS // SWA_KV_HEADS
MLA_HEADS = 8
MLA_Q_RANK = 256
MLA_KV_RANK = 128
MLA_NOPE_DIM = 64
MLA_ROPE_DIM = 32
MLA_V_DIM = 64
MLA_QK_DIM = MLA_NOPE_DIM + MLA_ROPE_DIM
SWA_Q_W = SWA_HEADS * SWA_HEAD_DIM
SWA_KV_W = SWA_KV_HEADS * SWA_HEAD_DIM
MLA_OUT_W = MLA_HEADS * MLA_V_DIM
MIX_W = SWA_Q_W + MLA_OUT_W
IN_W = SWA_Q_W + 2 * SWA_KV_W + MLA_Q_RANK + MLA_KV_RANK + MLA_ROPE_DIM
D_FF = -(-8 * D_MODEL // (3 * 256)) * 256

kernel_name = "hymba_swa_sink_mla_hybrid"


def rmsnorm(x, g):
    xf = x.astype(jnp.float32)
    y = xf * lax.rsqrt(jnp.mean(xf * xf, axis=-1, keepdims=True) + EPS)
    return (y * g.astype(jnp.float32)).astype(x.dtype)


def rope(x, pos):
    d = x.shape[-1]
    inv = ROPE_THETA ** (-jnp.arange(0, d, 2, dtype=jnp.float32) / d)
    ang = pos[:, None] * inv[None, :]
    cos = jnp.cos(ang)[:, None, :]
    sin = jnp.sin(ang)[:, None, :]
    xf = x.astype(jnp.float32)
    x1, x2 = xf[..., : d // 2], xf[..., d // 2:]
    return jnp.concatenate([x1 * cos - x2 * sin, x2 * cos + x1 * sin], -1).astype(x.dtype)


def swa_sink_attention(q, k, v, sinks, key_valid):
    B, T, _, D = q.shape
    nb = T // BLOCK
    qb = q.reshape(B, nb, BLOCK, SWA_KV_HEADS, SWA_GROUP, D)
    kb = k.reshape(B, nb, BLOCK, SWA_KV_HEADS, D)
    vb = v.reshape(B, nb, BLOCK, SWA_KV_HEADS, D)
    prev = lambda a: jnp.concatenate([jnp.zeros_like(a[:, :1]), a[:, :-1]], axis=1)
    kw = jnp.concatenate([prev(kb), kb], axis=2)
    vw = jnp.concatenate([prev(vb), vb], axis=2)
    s = jnp.einsum("bnqhgd,bnkhd->bhgnqk", qb, kw,
                   preferred_element_type=jnp.float32) * (D ** -0.5)
    qpos = jnp.arange(T).reshape(nb, BLOCK)
    kpos = jnp.concatenate([qpos - BLOCK, qpos], axis=1)
    kv_ok = key_valid.reshape(nb, BLOCK)
    kv_ok = jnp.concatenate(
        [jnp.concatenate([jnp.zeros((1, BLOCK), bool), kv_ok[:-1]], 0), kv_ok], 1)
    diff = qpos[:, :, None] - kpos[:, None, :]
    mask = (diff >= 0) & (diff < WINDOW) & kv_ok[:, None, :]
    s = jnp.where(mask, s, NEG)
    sink = jnp.broadcast_to(
        sinks.astype(jnp.float32).reshape(SWA_KV_HEADS, SWA_GROUP)[None, :, :, None, None, None],
        s.shape[:-1] + (1,))
    p = jax.nn.softmax(jnp.concatenate([s, sink], axis=-1), axis=-1)[..., :-1]
    o = jnp.einsum("bhgnqk,bnkhd->bnqhgd", p.astype(v.dtype), vw)
    return o.reshape(B, T, SWA_HEADS * D)


def causal_block_attention(q, k, v, key_valid):
    B, T, H, dqk = q.shape
    nb = T // BLOCK
    scale = dqk ** -0.5
    qb = jnp.moveaxis(q.reshape(B, nb, BLOCK, H, dqk), 1, 0)
    kpos = jnp.arange(T)

    def one_block(args):
        qblk, i = args
        s = jnp.einsum("bqhd,bkhd->bhqk", qblk, k,
                       preferred_element_type=jnp.float32) * scale
        qpos = i * BLOCK + jnp.arange(BLOCK)
        mask = (kpos[None, :] <= qpos[:, None]) & key_valid[None, :]
        p = jax.nn.softmax(jnp.where(mask, s, NEG), axis=-1)
        return jnp.einsum("bhqk,bkhd->bqhd", p.astype(v.dtype), v)

    o = lax.map(one_block, (qb, jnp.arange(nb)))
    return jnp.moveaxis(o, 0, 1).reshape(B, T, H * v.shape[-1])


def _fwd_setup_inputs(seed: int = 0) -> dict:
    key = jax.random.key(seed)
    ks = jax.random.split(key, 20)
    f32 = jnp.float32
    nrm = lambda k, shape, scale: jax.random.normal(k, shape, f32) * scale
    gain = lambda k, shape: 1.0 + 0.02 * jax.random.normal(k, shape, f32)
    return {
        "x": nrm(ks[0], (BATCH, SEQ, D_MODEL), 1.0),
        "meta_tokens": nrm(ks[1], (N_META, D_MODEL), 1.0),
        "attn_norm": gain(ks[2], (DEPTH, D_MODEL)),
        "w_in": nrm(ks[3], (DEPTH, D_MODEL, IN_W), D_MODEL ** -0.5),
        "q_norm": gain(ks[4], (DEPTH, MLA_Q_RANK)),
        "w_q_up": nrm(ks[5], (DEPTH, MLA_Q_RANK, MLA_HEADS * MLA_QK_DIM), MLA_Q_RANK ** -0.5),
        "kv_norm": gain(ks[6], (DEPTH, MLA_KV_RANK)),
        "w_kv_up": nrm(ks[7], (DEPTH, MLA_KV_RANK, MLA_HEADS * (MLA_NOPE_DIM + MLA_V_DIM)),
                        MLA_KV_RANK ** -0.5),
        "sinks": nrm(ks[8], (DEPTH, SWA_HEADS), 1.0),
        "out_norm_swa": gain(ks[9], (DEPTH, SWA_Q_W)),
        "out_norm_mla": gain(ks[10], (DEPTH, MLA_OUT_W)),
        "w_o": nrm(ks[11], (DEPTH, MIX_W, D_MODEL), MIX_W ** -0.5),
        "ffn_norm": gain(ks[12], (DEPTH, D_MODEL)),
        "w_gate": nrm(ks[13], (DEPTH, D_MODEL, D_FF), D_MODEL ** -0.5),
        "w_up": nrm(ks[14], (DEPTH, D_MODEL, D_FF), D_MODEL ** -0.5),
        "w_down": nrm(ks[15], (DEPTH, D_FF, D_MODEL), D_FF ** -0.5),
        "final_norm": gain(ks[16], (D_MODEL,)),
    }


def _fwd_reference(x, meta_tokens, attn_norm, w_in, q_norm, w_q_up, kv_norm, w_kv_up, sinks,
              out_norm_swa, out_norm_mla, w_o, ffn_norm, w_gate, w_up, w_down, final_norm):
    B, S, D = x.shape
    front = (-N_META) % BLOCK
    back = (-S) % BLOCK
    T = front + N_META + S + back
    h = jnp.concatenate([
        jnp.zeros((B, front, D), x.dtype),
        jnp.broadcast_to(meta_tokens.astype(x.dtype)[None], (B, N_META, D)),
        x,
        jnp.zeros((B, back, D), x.dtype)], axis=1)
    idx = jnp.arange(T)
    key_valid = (idx >= front) & (idx < front + N_META + S)
    pos = (idx - front).astype(jnp.float32)

    o1 = SWA_Q_W
    o2 = o1 + SWA_KV_W
    o3 = o2 + SWA_KV_W
    o4 = o3 + MLA_Q_RANK
    o5 = o4 + MLA_KV_RANK
    for l in range(DEPTH):
        u = rmsnorm(h, attn_norm[l])
        proj = u @ w_in[l]
        q_a = rope(proj[..., :o1].reshape(B, T, SWA_HEADS, SWA_HEAD_DIM), pos)
        k_a = rope(proj[..., o1:o2].reshape(B, T, SWA_KV_HEADS, SWA_HEAD_DIM), pos)
        v_a = proj[..., o2:o3].reshape(B, T, SWA_KV_HEADS, SWA_HEAD_DIM)
        out_a = swa_sink_attention(q_a, k_a, v_a, sinks[l], key_valid)
        q_b = (rmsnorm(proj[..., o3:o4], q_norm[l]) @ w_q_up[l]).reshape(
            B, T, MLA_HEADS, MLA_QK_DIM)
        kv_b = (rmsnorm(proj[..., o4:o5], kv_norm[l]) @ w_kv_up[l]).reshape(
            B, T, MLA_HEADS, MLA_NOPE_DIM + MLA_V_DIM)
        k_rope = rope(proj[..., o5:][:, :, None, :], pos)
        q_full = jnp.concatenate(
            [q_b[..., :MLA_NOPE_DIM], rope(q_b[..., MLA_NOPE_DIM:], pos)], axis=-1)
        k_full = jnp.concatenate(
            [kv_b[..., :MLA_NOPE_DIM],
             jnp.broadcast_to(k_rope, (B, T, MLA_HEADS, MLA_ROPE_DIM))], axis=-1)
        v_b = kv_b[..., MLA_NOPE_DIM:]
        out_b = causal_block_attention(q_full, k_full, v_b, key_valid)
        mix = jnp.concatenate([rmsnorm(out_a, out_norm_swa[l]),
                               rmsnorm(out_b, out_norm_mla[l])], axis=-1)
        h = h + mix @ w_o[l]
        u = rmsnorm(h, ffn_norm[l])
        h = h + (jax.nn.silu(u @ w_gate[l]) * (u @ w_up[l])) @ w_down[l]

    h = rmsnorm(h, final_norm)
    start = front + N_META
    return h[:, start:start + S]


import jax as _jax
import jax.numpy as _jnp

TWIN_FORMAT = 'train_step'
FWD_PARAMS = ['x', 'meta_tokens', 'attn_norm', 'w_in', 'q_norm', 'w_q_up', 'kv_norm', 'w_kv_up', 'sinks', 'out_norm_swa', 'out_norm_mla', 'w_o', 'ffn_norm', 'w_gate', 'w_up', 'w_down', 'final_norm']
TWIN_WEIGHTS = ['meta_tokens', 'attn_norm', 'w_in', 'q_norm', 'w_q_up', 'kv_norm', 'w_kv_up', 'sinks', 'out_norm_swa', 'out_norm_mla', 'w_o', 'ffn_norm', 'w_gate', 'w_up', 'w_down', 'final_norm']
TWIN_DIFF_INPUT = 'x'
TWIN_INPUTS = ['x', 'meta_tokens', 'attn_norm', 'w_in', 'q_norm', 'w_q_up', 'kv_norm', 'w_kv_up', 'sinks', 'out_norm_swa', 'out_norm_mla', 'w_o', 'ffn_norm', 'w_gate', 'w_up', 'w_down', 'final_norm', 'loss_target', 'm_meta_tokens', 'm_attn_norm', 'm_w_in', 'm_q_norm', 'm_w_q_up', 'm_kv_norm', 'm_w_kv_up', 'm_sinks', 'm_out_norm_swa', 'm_out_norm_mla', 'm_w_o', 'm_ffn_norm', 'm_w_gate', 'm_w_up', 'm_w_down', 'm_final_norm', 'v_meta_tokens', 'v_attn_norm', 'v_w_in', 'v_q_norm', 'v_w_q_up', 'v_kv_norm', 'v_w_kv_up', 'v_sinks', 'v_out_norm_swa', 'v_out_norm_mla', 'v_w_o', 'v_ffn_norm', 'v_w_gate', 'v_w_up', 'v_w_down', 'v_final_norm']
TWIN_OUTPUTS = ['loss', 'grad_x', 'grad_meta_tokens', 'grad_attn_norm', 'grad_w_in', 'grad_q_norm', 'grad_w_q_up', 'grad_kv_norm', 'grad_w_kv_up', 'grad_sinks', 'grad_out_norm_swa', 'grad_out_norm_mla', 'grad_w_o', 'grad_ffn_norm', 'grad_w_gate', 'grad_w_up', 'grad_w_down', 'grad_final_norm', 'delta_meta_tokens', 'delta_attn_norm', 'delta_w_in', 'delta_q_norm', 'delta_w_q_up', 'delta_kv_norm', 'delta_w_kv_up', 'delta_sinks', 'delta_out_norm_swa', 'delta_out_norm_mla', 'delta_w_o', 'delta_ffn_norm', 'delta_w_gate', 'delta_w_up', 'delta_w_down', 'delta_final_norm', 'new_m_meta_tokens', 'new_m_attn_norm', 'new_m_w_in', 'new_m_q_norm', 'new_m_w_q_up', 'new_m_kv_norm', 'new_m_w_kv_up', 'new_m_sinks', 'new_m_out_norm_swa', 'new_m_out_norm_mla', 'new_m_w_o', 'new_m_ffn_norm', 'new_m_w_gate', 'new_m_w_up', 'new_m_w_down', 'new_m_final_norm', 'new_v_meta_tokens', 'new_v_attn_norm', 'new_v_w_in', 'new_v_q_norm', 'new_v_w_q_up', 'new_v_kv_norm', 'new_v_w_kv_up', 'new_v_sinks', 'new_v_out_norm_swa', 'new_v_out_norm_mla', 'new_v_w_o', 'new_v_ffn_norm', 'new_v_w_gate', 'new_v_w_up', 'new_v_w_down', 'new_v_final_norm']
TWIN_LEAF_KINDS = {'loss': 'loss', 'grad_x': 'grad_x', 'grad_meta_tokens': 'grad_w', 'grad_attn_norm': 'grad_w', 'grad_w_in': 'grad_w', 'grad_q_norm': 'grad_w', 'grad_w_q_up': 'grad_w', 'grad_kv_norm': 'grad_w', 'grad_w_kv_up': 'grad_w', 'grad_sinks': 'grad_w', 'grad_out_norm_swa': 'grad_w', 'grad_out_norm_mla': 'grad_w', 'grad_w_o': 'grad_w', 'grad_ffn_norm': 'grad_w', 'grad_w_gate': 'grad_w', 'grad_w_up': 'grad_w', 'grad_w_down': 'grad_w', 'grad_final_norm': 'grad_w', 'delta_meta_tokens': 'delta_w', 'delta_attn_norm': 'delta_w', 'delta_w_in': 'delta_w', 'delta_q_norm': 'delta_w', 'delta_w_q_up': 'delta_w', 'delta_kv_norm': 'delta_w', 'delta_w_kv_up': 'delta_w', 'delta_sinks': 'delta_w', 'delta_out_norm_swa': 'delta_w', 'delta_out_norm_mla': 'delta_w', 'delta_w_o': 'delta_w', 'delta_ffn_norm': 'delta_w', 'delta_w_gate': 'delta_w', 'delta_w_up': 'delta_w', 'delta_w_down': 'delta_w', 'delta_final_norm': 'delta_w', 'new_m_meta_tokens': 'new_m', 'new_m_attn_norm': 'new_m', 'new_m_w_in': 'new_m', 'new_m_q_norm': 'new_m', 'new_m_w_q_up': 'new_m', 'new_m_kv_norm': 'new_m', 'new_m_w_kv_up': 'new_m', 'new_m_sinks': 'new_m', 'new_m_out_norm_swa': 'new_m', 'new_m_out_norm_mla': 'new_m', 'new_m_w_o': 'new_m', 'new_m_ffn_norm': 'new_m', 'new_m_w_gate': 'new_m', 'new_m_w_up': 'new_m', 'new_m_w_down': 'new_m', 'new_m_final_norm': 'new_m', 'new_v_meta_tokens': 'new_v', 'new_v_attn_norm': 'new_v', 'new_v_w_in': 'new_v', 'new_v_q_norm': 'new_v', 'new_v_w_q_up': 'new_v', 'new_v_kv_norm': 'new_v', 'new_v_w_kv_up': 'new_v', 'new_v_sinks': 'new_v', 'new_v_out_norm_swa': 'new_v', 'new_v_out_norm_mla': 'new_v', 'new_v_w_o': 'new_v', 'new_v_ffn_norm': 'new_v', 'new_v_w_gate': 'new_v', 'new_v_w_up': 'new_v', 'new_v_w_down': 'new_v', 'new_v_final_norm': 'new_v'}


def _forward(args):
    return _fwd_reference(*[args[k] for k in FWD_PARAMS])


def _output_shape():
    out = _jax.eval_shape(lambda: _forward(_fwd_setup_inputs(0)))
    return out.shape, out.dtype

N_MICROBATCH = 1
ADAM_LR = 0.001
ADAM_B1 = 0.9
ADAM_B2 = 0.999
ADAM_EPS = 1e-08
ADAM_WD = 0.01
ADAM_STEP = 10
PER_EXAMPLE_BATCH_AXIS = {'x': 0, 'loss_target': 0}
SHARED_INPUTS = []
_WEIGHT_DTYPES = {'meta_tokens': _jnp.float32, 'attn_norm': _jnp.float32, 'w_in': _jnp.float32, 'q_norm': _jnp.float32, 'w_q_up': _jnp.float32, 'kv_norm': _jnp.float32, 'w_kv_up': _jnp.float32, 'sinks': _jnp.float32, 'out_norm_swa': _jnp.float32, 'out_norm_mla': _jnp.float32, 'w_o': _jnp.float32, 'ffn_norm': _jnp.float32, 'w_gate': _jnp.float32, 'w_up': _jnp.float32, 'w_down': _jnp.float32, 'final_norm': _jnp.float32}
MOMENT_SCALE = {'meta_tokens': 3.831638e-02, 'attn_norm': 1.687014e-01, 'w_in': 1.552999e-01, 'q_norm': 9.556580e-02, 'w_q_up': 5.098123e-02, 'kv_norm': 3.412830e-01, 'w_kv_up': 1.297236e-01, 'sinks': 4.369703e-02, 'out_norm_swa': 1.425530e-01, 'out_norm_mla': 1.716204e-01, 'w_o': 1.519608e-01, 'ffn_norm': 7.699937e-02, 'w_gate': 3.356081e-02, 'w_up': 3.348955e-02, 'w_down': 5.563437e-02, 'final_norm': 3.125626e+01}


def _to_microbatches(a, axis):
    t = _jnp.moveaxis(a, axis, 0)
    t = t.reshape((N_MICROBATCH, t.shape[0] // N_MICROBATCH) + t.shape[1:])
    return _jnp.moveaxis(t, 1, axis + 1)


def setup_inputs(seed: int = 0) -> dict:
    inp = _fwd_setup_inputs(seed)
    key = _jax.random.fold_in(_jax.random.key(seed), 7919)
    shape, _ = _output_shape()
    out = dict(inp)
    out["loss_target"] = _jax.random.normal(_jax.random.fold_in(key, 0), shape, _jnp.float32)
    for i, name in enumerate(TWIN_WEIGHTS):
        w = inp[name].astype(_jnp.float32)
        if MOMENT_SCALE is None:
            s = _jnp.sqrt(_jnp.mean(_jnp.square(w)) + 1e-30)
        else:
            s = MOMENT_SCALE[name]
        km, kv = _jax.random.split(_jax.random.fold_in(key, i + 1))
        out[name] = w
        out["m_" + name] = s * _jax.random.normal(km, w.shape, _jnp.float32)
        out["v_" + name] = (s * s) * _jax.random.uniform(kv, w.shape, _jnp.float32, 0.5, 1.5)
    if N_MICROBATCH > 1:
        for name, axis in PER_EXAMPLE_BATCH_AXIS.items():
            out[name] = _to_microbatches(out[name], axis)
    return {'x': out['x'], 'meta_tokens': out['meta_tokens'], 'attn_norm': out['attn_norm'], 'w_in': out['w_in'], 'q_norm': out['q_norm'], 'w_q_up': out['w_q_up'], 'kv_norm': out['kv_norm'], 'w_kv_up': out['w_kv_up'], 'sinks': out['sinks'], 'out_norm_swa': out['out_norm_swa'], 'out_norm_mla': out['out_norm_mla'], 'w_o': out['w_o'], 'ffn_norm': out['ffn_norm'], 'w_gate': out['w_gate'], 'w_up': out['w_up'], 'w_down': out['w_down'], 'final_norm': out['final_norm'], 'loss_target': out['loss_target'], 'm_meta_tokens': out['m_meta_tokens'], 'm_attn_norm': out['m_attn_norm'], 'm_w_in': out['m_w_in'], 'm_q_norm': out['m_q_norm'], 'm_w_q_up': out['m_w_q_up'], 'm_kv_norm': out['m_kv_norm'], 'm_w_kv_up': out['m_w_kv_up'], 'm_sinks': out['m_sinks'], 'm_out_norm_swa': out['m_out_norm_swa'], 'm_out_norm_mla': out['m_out_norm_mla'], 'm_w_o': out['m_w_o'], 'm_ffn_norm': out['m_ffn_norm'], 'm_w_gate': out['m_w_gate'], 'm_w_up': out['m_w_up'], 'm_w_down': out['m_w_down'], 'm_final_norm': out['m_final_norm'], 'v_meta_tokens': out['v_meta_tokens'], 'v_attn_norm': out['v_attn_norm'], 'v_w_in': out['v_w_in'], 'v_q_norm': out['v_q_norm'], 'v_w_q_up': out['v_w_q_up'], 'v_kv_norm': out['v_kv_norm'], 'v_w_kv_up': out['v_w_kv_up'], 'v_sinks': out['v_sinks'], 'v_out_norm_swa': out['v_out_norm_swa'], 'v_out_norm_mla': out['v_out_norm_mla'], 'v_w_o': out['v_w_o'], 'v_ffn_norm': out['v_ffn_norm'], 'v_w_gate': out['v_w_gate'], 'v_w_up': out['v_w_up'], 'v_w_down': out['v_w_down'], 'v_final_norm': out['v_final_norm']}


def _loss(weights, diff, rest, loss_target):
    with _jax.named_scope("forward"):
        args = {**rest, TWIN_DIFF_INPUT: diff, **{k: w.astype(_WEIGHT_DTYPES[k]) for k, w in weights.items()}}
        y = _forward(args)
    with _jax.named_scope("loss_head"):
        err = _jnp.square(y.astype(_jnp.float32) - loss_target)
        return 0.5 * _jnp.sum(_jnp.mean(err, axis=-1)) if err.ndim else 0.5 * err


def _adamw(w, g, m, v):
    m = ADAM_B1 * m + (1.0 - ADAM_B1) * g
    v = ADAM_B2 * v + (1.0 - ADAM_B2) * _jnp.square(g)
    m_hat = m / (1.0 - ADAM_B1 ** ADAM_STEP)
    v_hat = v / (1.0 - ADAM_B2 ** ADAM_STEP)
    delta = -ADAM_LR * (m_hat / (_jnp.sqrt(v_hat) + ADAM_EPS) + ADAM_WD * w)
    return delta, m, v


def reference(x, meta_tokens, attn_norm, w_in, q_norm, w_q_up, kv_norm, w_kv_up, sinks, out_norm_swa, out_norm_mla, w_o, ffn_norm, w_gate, w_up, w_down, final_norm, loss_target, m_meta_tokens, m_attn_norm, m_w_in, m_q_norm, m_w_q_up, m_kv_norm, m_w_kv_up, m_sinks, m_out_norm_swa, m_out_norm_mla, m_w_o, m_ffn_norm, m_w_gate, m_w_up, m_w_down, m_final_norm, v_meta_tokens, v_attn_norm, v_w_in, v_q_norm, v_w_q_up, v_kv_norm, v_w_kv_up, v_sinks, v_out_norm_swa, v_out_norm_mla, v_w_o, v_ffn_norm, v_w_gate, v_w_up, v_w_down, v_final_norm):
    given = dict(x=x, meta_tokens=meta_tokens, attn_norm=attn_norm, w_in=w_in, q_norm=q_norm, w_q_up=w_q_up, kv_norm=kv_norm, w_kv_up=w_kv_up, sinks=sinks, out_norm_swa=out_norm_swa, out_norm_mla=out_norm_mla, w_o=w_o, ffn_norm=ffn_norm, w_gate=w_gate, w_up=w_up, w_down=w_down, final_norm=final_norm, loss_target=loss_target, m_meta_tokens=m_meta_tokens, m_attn_norm=m_attn_norm, m_w_in=m_w_in, m_q_norm=m_q_norm, m_w_q_up=m_w_q_up, m_kv_norm=m_kv_norm, m_w_kv_up=m_w_kv_up, m_sinks=m_sinks, m_out_norm_swa=m_out_norm_swa, m_out_norm_mla=m_out_norm_mla, m_w_o=m_w_o, m_ffn_norm=m_ffn_norm, m_w_gate=m_w_gate, m_w_up=m_w_up, m_w_down=m_w_down, m_final_norm=m_final_norm, v_meta_tokens=v_meta_tokens, v_attn_norm=v_attn_norm, v_w_in=v_w_in, v_q_norm=v_q_norm, v_w_q_up=v_w_q_up, v_kv_norm=v_kv_norm, v_w_kv_up=v_w_kv_up, v_sinks=v_sinks, v_out_norm_swa=v_out_norm_swa, v_out_norm_mla=v_out_norm_mla, v_w_o=v_w_o, v_ffn_norm=v_ffn_norm, v_w_gate=v_w_gate, v_w_up=v_w_up, v_w_down=v_w_down, v_final_norm=v_final_norm)
    weights = {n: given[n] for n in TWIN_WEIGHTS}
    shared = {n: given[n] for n in SHARED_INPUTS}
    per_example = {n: given[n] for n in ['x']}
    grad_fn = _jax.value_and_grad(_loss, argnums=(0, 1))

    def one_microbatch(ex, loss_target):
        ex = dict(ex)
        diff = ex.pop(TWIN_DIFF_INPUT)
        return grad_fn(weights, diff, {**shared, **ex}, loss_target)

    if N_MICROBATCH == 1:
        loss, (grad_w, grad_x) = one_microbatch(per_example, given["loss_target"])
    else:
        def body(carry, xs):
            loss_sum, grad_sum = carry
            l_k, (gw_k, gx_k) = one_microbatch(xs[0], xs[1])
            with _jax.named_scope("update"):
                return (loss_sum + l_k, _jax.tree.map(_jnp.add, grad_sum, gw_k)), gx_k

        init = (_jnp.zeros((), _jnp.float32), _jax.tree.map(_jnp.zeros_like, weights))
        (loss, grad_w), grad_x = _jax.lax.scan(body, init, (per_example, given["loss_target"]))
    with _jax.named_scope("update"):
        delta_w, new_m, new_v = {}, {}, {}
        for n in TWIN_WEIGHTS:
            delta_w[n], new_m[n], new_v[n] = _adamw(weights[n], grad_w[n], given["m_" + n], given["v_" + n])
    return (loss, grad_x, *[grad_w[n] for n in TWIN_WEIGHTS], *[delta_w[n] for n in TWIN_WEIGHTS],
            *[new_m[n] for n in TWIN_WEIGHTS], *[new_v[n] for n in TWIN_WEIGHTS])
```

```python
import jax
import jax.numpy as jnp
from jax import lax
from jax.experimental import pallas as pl
from jax.experimental.pallas import tpu as pltpu

F32 = jnp.float32
BF16 = jnp.bfloat16
MXU_DTYPE = BF16

D_MODEL = 1024
DEPTH = 4
N_META = 16
BLOCK = 128
WINDOW = 128
ROPE_THETA = 10000.0
EPS = 1e-6
NEG = -1e30
SWA_HEADS = 8
SWA_KV_HEADS = 2
SWA_HEAD_DIM = 64
MLA_HEADS = 8
MLA_Q_RANK = 256
MLA_KV_RANK = 128
MLA_NOPE_DIM = 64
MLA_ROPE_DIM = 32
MLA_V_DIM = 64
MLA_QK_DIM = MLA_NOPE_DIM + MLA_ROPE_DIM
D_FF = 2816
FRONT = (-N_META) % BLOCK
LANES = 128
SLOT_W = 8 * LANES
C_QA, C_KA, C_VA, C_QL, C_KL, C_KR, IN_WP = 0, 512, 640, 768, 1024, 1152, 1280
KR_LANE = 64
IN_W = 1184

ADAM_LR, ADAM_B1, ADAM_B2, ADAM_EPS, ADAM_WD, ADAM_STEP = 0.001, 0.9, 0.999, 1e-08, 0.01, 10

VMEM_LIMIT = 48 * 1024 * 1024
MESH = pl.DeviceIdType.MESH


def _tile(n, prefs):
    for t in prefs:
        if n % t == 0:
            return t
    return n


def _params(sem):
    return pltpu.CompilerParams(dimension_semantics=sem, vmem_limit_bytes=VMEM_LIMIT)


_DIMS = {"nn": (((1,), (0,)), ((), ())), "nt": (((1,), (1,)), ((), ())), "tn": (((0,), (0,)), ((), ()))}


def _mm(name, a, b, mode, out_dtype=F32, res=None):
    if mode == "nn":
        (M, K), (_, N) = a.shape, b.shape
    elif mode == "nt":
        (M, K), (N, _) = a.shape, b.shape
    else:
        (K, M), (_, N) = a.shape, b.shape
    lane_tiles = (1408, 1024, 640, 768, 512, 384, 256, 128)
    row_tiles = (528, 512, 384, 256, 128)
    bm = _tile(M, lane_tiles if mode == "tn" else row_tiles)
    bn = _tile(N, lane_tiles)
    bk = _tile(K, row_tiles if mode == "tn" else lane_tiles)
    nk = K // bk
    if mode == "tn":
        a_spec = pl.BlockSpec((bk, bm), lambda i, j, k: (k, i))
    else:
        a_spec = pl.BlockSpec((bm, bk), lambda i, j, k: (i, k))
    if mode == "nt":
        b_spec = pl.BlockSpec((bn, bk), lambda i, j, k: (j, k))
    else:
        b_spec = pl.BlockSpec((bk, bn), lambda i, j, k: (k, j))
    o_spec = pl.BlockSpec((bm, bn), lambda i, j, k: (i, j))
    in_specs = [a_spec, b_spec]
    args = [a, b]
    if res is not None:
        in_specs.append(o_spec)
        args.append(res)
    dims = _DIMS[mode]

    def kern(a_ref, b_ref, *rest):
        if res is not None:
            r_ref, o_ref = rest[0], rest[1]
            scr = rest[2:]
        else:
            r_ref, o_ref = None, rest[0]
            scr = rest[1:]
        p = lax.dot_general(a_ref[...].astype(MXU_DTYPE), b_ref[...].astype(MXU_DTYPE), dims,
                            preferred_element_type=F32)

        def finish(val):
            if r_ref is not None:
                val = val + r_ref[...]
            o_ref[...] = val.astype(o_ref.dtype)

        if nk == 1:
            finish(p)
        else:
            acc = scr[0]
            k = pl.program_id(2)

            @pl.when(k == 0)
            def _():
                acc[...] = p

            @pl.when(k > 0)
            def _():
                acc[...] += p

            @pl.when(k == nk - 1)
            def _():
                finish(acc[...])

    return pl.pallas_call(
        kern, name=name,
        out_shape=jax.ShapeDtypeStruct((M, N), out_dtype),
        grid=(M // bm, N // bn, nk),
        in_specs=in_specs, out_specs=o_spec,
        scratch_shapes=[pltpu.VMEM((bm, bn), F32)] if nk > 1 else [],
        compiler_params=_params(("parallel", "parallel", "arbitrary")),
    )(*args)


def _rowmap(name, body, rows, vecs, outs, accs=(), tr_prefs=(384, 256, 128)):
    R = rows[0].shape[0]
    tr = _tile(R, tr_prefs)
    n_r, n_v, n_o, n_a = len(rows), len(vecs), len(outs), len(accs)

    def kern(*refs):
        ins = [r[...] for r in refs[:n_r + n_v]]
        o_refs = refs[n_r + n_v:n_r + n_v + n_o]
        a_refs = refs[n_r + n_v + n_o:]
        res = body(*ins)
        for o_ref, val in zip(o_refs, res[:n_o]):
            o_ref[...] = val.astype(o_ref.dtype)
        if n_a:
            first = pl.program_id(0) == 0

            @pl.when(first)
            def _():
                for a_ref, val in zip(a_refs, res[n_o:]):
                    a_ref[...] = val

            @pl.when(jnp.logical_not(first))
            def _():
                for a_ref, val in zip(a_refs, res[n_o:]):
                    a_ref[...] += val

    in_specs = [pl.BlockSpec((tr, r.shape[1]), lambda i: (i, 0)) for r in rows]
    in_specs += [pl.BlockSpec((1, v.shape[1]), lambda i: (0, 0)) for v in vecs]
    out_specs = [pl.BlockSpec((tr, c), lambda i: (i, 0)) for c, _ in outs]
    out_specs += [pl.BlockSpec((1, c), lambda i: (0, 0)) for c in accs]
    out_shape = [jax.ShapeDtypeStruct((R, c), dt) for c, dt in outs]
    out_shape += [jax.ShapeDtypeStruct((1, c), F32) for c in accs]
    return pl.pallas_call(
        kern, name=name, out_shape=out_shape, grid=(R // tr,),
        in_specs=in_specs, out_specs=out_specs,
        compiler_params=_params(("arbitrary",) if n_a else ("parallel",)),
    )(*rows, *vecs)


def _lane(shape):
    return lax.broadcasted_iota(jnp.int32, shape, 1)


def _rot_swa(x):
    lane = _lane(x.shape)
    return jnp.where((lane & 63) < 32, -pltpu.roll(x, 96, 1), pltpu.roll(x, 32, 1))


def _rot_mla(x):
    lane = _lane(x.shape)
    lo = jnp.where(lane >= KR_LANE, -pltpu.roll(x, 112, 1), 0.0)
    hi = jnp.where(lane < KR_LANE + MLA_ROPE_DIM, pltpu.roll(x, 16, 1), 0.0)
    return jnp.where(lane < KR_LANE + 16, lo, hi)


def _rope(x, cos, sin, rot):
    return x * cos + rot(x) * sin


def _rope_t(g, cos, sin, rot):
    return g * cos - rot(g * sin)


def _low(x):
    return jnp.where(_lane(x.shape) < 64, x, 0.0)


def _blk(x, j):
    return x[:, j * LANES:(j + 1) * LANES]


def _rms_r(x, width):
    return lax.rsqrt(jnp.sum(x * x, axis=-1, keepdims=True) * (1.0 / width) + EPS)


def _rms_bwd(x, g, dy, width):
    r = _rms_r(x, width)
    gdy = dy * g
    dot = jnp.sum(gdy * x, axis=-1, keepdims=True)
    dx = r * gdy - x * (r * r * r * (1.0 / width) * dot)
    return dx, dy * x * r


def _colsum(x):
    return jnp.sum(x, axis=0, keepdims=True)


def _rmsnorm_fwd(name, x, g):
    def body(xv, gv):
        return (xv * _rms_r(xv, D_MODEL) * gv,)
    return _rowmap(name, body, [x], [g], [(D_MODEL, BF16)])[0]


def _rmsnorm_bwd(name, x, g, dy, dres):
    def body(xv, dyv, dr, gv):
        dx, dg = _rms_bwd(xv, gv, dyv, D_MODEL)
        return dx + dr, _colsum(dg)
    return _rowmap(name, body, [x, dy, dres], [g], [(D_MODEL, F32)], [D_MODEL])


def _prep1(proj, qn_g, kv_g, cosa, sina, cosm, sinm):
    def body(p, ca, sa, cm, sm, gq, gk):
        qa = []
        for j in range(4):
            xr = _rope(_blk(p, j), ca, sa, _rot_swa)
            qa += [_low(xr), _low(pltpu.roll(xr, 64, 1))]
        kr_ = _rope(_blk(p, C_KA // LANES), ca, sa, _rot_swa)
        ka = [_low(kr_), _low(pltpu.roll(kr_, 64, 1))]
        vv = _blk(p, C_VA // LANES)
        va = [_low(vv), _low(pltpu.roll(vv, 64, 1))]
        ql = p[:, C_QL:C_QL + MLA_Q_RANK]
        qn = ql * _rms_r(ql, MLA_Q_RANK) * gq
        kl = p[:, C_KL:C_KL + MLA_KV_RANK]
        cn = kl * _rms_r(kl, MLA_KV_RANK) * gk
        kr = _rope(_blk(p, C_KR // LANES), cm, sm, _rot_mla)
        return (jnp.concatenate(qa, 1), jnp.concatenate(ka, 1), jnp.concatenate(va, 1), qn, cn, kr)
    return _rowmap("prep1", body, [proj, cosa, sina, cosm, sinm], [qn_g, kv_g],
                   [(SLOT_W, BF16), (2 * LANES, BF16), (2 * LANES, BF16),
                    (MLA_Q_RANK, BF16), (MLA_KV_RANK, BF16), (LANES, F32)])


def _prep2(qb, kvb, kr, cosm, sinm):
    def body(q, kv, krv, cm, sm):
        qs, ks, vs = [], [], []
        for h in range(MLA_HEADS):
            qs.append(_rope(_blk(q, h), cm, sm, _rot_mla))
            kvh = _blk(kv, h)
            ks.append(_low(kvh) + krv)
            vs.append(_low(pltpu.roll(kvh, 64, 1)))
        return jnp.concatenate(qs, 1), jnp.concatenate(ks, 1), jnp.concatenate(vs, 1)
    return _rowmap("prep2", body, [qb, kvb, kr, cosm, sinm], [],
                   [(SLOT_W, BF16), (SLOT_W, BF16), (SLOT_W, BF16)])


def _compact(slots):
    return jnp.concatenate(
        [_blk(slots, 2 * j) + pltpu.roll(_blk(slots, 2 * j + 1), 64, 1) for j in range(4)], 1)


def _expand(nat):
    out = []
    for j in range(4):
        b = _blk(nat, j)
        out += [_low(b), _low(pltpu.roll(b, 64, 1))]
    return jnp.concatenate(out, 1)


def _merge_fwd(oa, ob, ga, gb):
    def body(a, b, gav, gbv):
        xa, xb = _compact(a), _compact(b)
        return (jnp.concatenate([xa * _rms_r(xa, 512) * gav, xb * _rms_r(xb, 512) * gbv], 1),)
    return _rowmap("merge_fwd", body, [oa, ob], [ga, gb], [(D_MODEL, BF16)])[0]


def _merge_bwd(dmix, oa, ob, lse_a, ga, gb, sink_slots):
    def body(dm, a, b, lse, gav, gbv, sk):
        outs = []
        accs = []
        for o, g, lo in ((a, gav, 0), (b, gbv, 512)):
            x = _compact(o)
            dx, dg = _rms_bwd(x, g, dm[:, lo:lo + 512], 512)
            do = _expand(dx)
            delta = jnp.concatenate(
                [jnp.broadcast_to(jnp.sum(_blk(do, h) * _blk(o, h), axis=-1, keepdims=True),
                                  (do.shape[0], LANES)) for h in range(8)], 1)
            outs += [do, delta]
            accs.append(_colsum(dg))
        dsink = _colsum(-jnp.exp(sk - lse) * outs[1])
        return (*outs, *accs, dsink)
    return _rowmap("merge_bwd", body, [dmix, oa, ob, lse_a], [ga, gb, sink_slots],
                   [(SLOT_W, BF16), (SLOT_W, F32), (SLOT_W, BF16), (SLOT_W, F32)],
                   [512, 512, SLOT_W])


def _prep2_bwd(dq, dk, dv, cosm, sinm):
    def body(dqv, dkv, dvv, cm, sm):
        dqb, dkvb = [], []
        krsum = jnp.zeros((dqv.shape[0], LANES), F32)
        for h in range(MLA_HEADS):
            dqb.append(_rope_t(_blk(dqv, h), cm, sm, _rot_mla))
            dkh = _blk(dkv, h)
            dkvb.append(_low(dkh) + pltpu.roll(_blk(dvv, h), 64, 1))
            krsum = krsum + dkh
        lane = _lane(krsum.shape)
        dkr = jnp.where((lane >= KR_LANE) & (lane < KR_LANE + MLA_ROPE_DIM),
                        _rope_t(krsum, cm, sm, _rot_mla), 0.0)
        return jnp.concatenate(dqb, 1), jnp.concatenate(dkvb, 1), dkr
    return _rowmap("prep2_bwd", body, [dq, dk, dv, cosm, sinm], [],
                   [(SLOT_W, BF16), (SLOT_W, BF16), (LANES, F32)])


def _prep1_bwd(proj, dqa, dka, dva, dqn, dcn, dkr, cosa, sina, qn_g, kv_g):
    def body(p, dq, dk, dv, dqnv, dcnv, dkrv, ca, sa, gq, gk):
        cols = []
        for j in range(4):
            nat = _blk(dq, 2 * j) + pltpu.roll(_blk(dq, 2 * j + 1), 64, 1)
            cols.append(_rope_t(nat, ca, sa, _rot_swa))
        grp = lambda d, g: sum(_blk(d, 4 * g + i) for i in range(4))
        cols.append(_rope_t(grp(dk, 0) + pltpu.roll(grp(dk, 1), 64, 1), ca, sa, _rot_swa))
        cols.append(grp(dv, 0) + pltpu.roll(grp(dv, 1), 64, 1))
        dql, dgq = _rms_bwd(p[:, C_QL:C_QL + MLA_Q_RANK], gq, dqnv, MLA_Q_RANK)
        dkl, dgk = _rms_bwd(p[:, C_KL:C_KL + MLA_KV_RANK], gk, dcnv, MLA_KV_RANK)
        cols += [dql, dkl, dkrv]
        return jnp.concatenate(cols, 1), _colsum(dgq), _colsum(dgk)
    return _rowmap("prep1_bwd", body, [proj, dqa, dka, dva, dqn, dcn, dkr, cosa, sina], [qn_g, kv_g],
                   [(IN_WP, BF16)], [MLA_Q_RANK, MLA_KV_RANK], tr_prefs=(192, 128))


def _sigmoid(x):
    return 1.0 / (1.0 + jnp.exp(-x))


def _swiglu_fwd(a, b):
    def body(av, bv):
        return (av * _sigmoid(av) * bv,)
    return _rowmap("swiglu_fwd", body, [a, b], [], [(D_FF, BF16)])[0]


def _swiglu_bwd(a, b, dh):
    def body(av, bv, d):
        s = _sigmoid(av)
        return d * bv * (s * (1.0 + av * (1.0 - s))), d * (av * s)
    return _rowmap("swiglu_bwd", body, [a, b, dh], [], [(D_FF, BF16), (D_FF, BF16)])


def _loss_head(h, target, g):
    T = h.shape[0]
    nb = T // BLOCK

    def kern(h_ref, t_ref, g_ref, dh_ref, dg_ref, loss_ref, acc):
        i = pl.program_id(0)

        @pl.when(i == 0)
        def _():
            dh_ref[...] = jnp.zeros_like(dh_ref)
            dg_ref[...] = jnp.zeros_like(dg_ref)
            acc[...] = jnp.zeros_like(acc)

        @pl.when(i > 0)
        def _():
            x = h_ref[...]
            gv = g_ref[...]
            e = x * _rms_r(x, D_MODEL) * gv - t_ref[...]
            acc[...] += _colsum(e * e)
            dx, dg = _rms_bwd(x, gv, e * (1.0 / D_MODEL), D_MODEL)
            dh_ref[...] = dx
            dg_ref[...] += _colsum(dg)

        @pl.when(i == nb - 1)
        def _():
            tot = jnp.sum(acc[...], axis=-1, keepdims=True) * (0.5 / D_MODEL)
            loss_ref[...] = jnp.broadcast_to(tot, loss_ref.shape)

    return pl.pallas_call(
        kern, name="loss_head",
        out_shape=[jax.ShapeDtypeStruct((T, D_MODEL), F32), jax.ShapeDtypeStruct((1, D_MODEL), F32),
                   jax.ShapeDtypeStruct((1, LANES), F32)],
        grid=(nb,),
        in_specs=[pl.BlockSpec((BLOCK, D_MODEL), lambda i: (i, 0)),
                  pl.BlockSpec((BLOCK, D_MODEL), lambda i: (jnp.maximum(i - 1, 0), 0)),
                  pl.BlockSpec((1, D_MODEL), lambda i: (0, 0))],
        out_specs=[pl.BlockSpec((BLOCK, D_MODEL), lambda i: (i, 0)),
                   pl.BlockSpec((1, D_MODEL), lambda i: (0, 0)),
                   pl.BlockSpec((1, LANES), lambda i: (0, 0))],
        scratch_shapes=[pltpu.VMEM((1, D_MODEL), F32)],
        compiler_params=_params(("arbitrary",)),
    )(h, target, g)


def _attn_cfg(T, causal):
    t = _tile(T, (384, 256, 128)) if causal else BLOCK
    n = T // t
    return t, n, (n if causal else 2)


def _scores(q, k, scale, i, kb, t, causal):
    s = lax.dot_general(q, k, _DIMS["nt"], preferred_element_type=F32) * scale
    qpos = i * t + lax.broadcasted_iota(jnp.int32, (t, 1), 0)
    kpos = kb * t + lax.broadcasted_iota(jnp.int32, (t, t), 1)
    lo = jnp.full((t, 1), FRONT, jnp.int32) if causal else jnp.maximum(qpos - (WINDOW - 1), FRONT)
    return jnp.where(kpos >= lo, jnp.where(kpos <= qpos, s, NEG), NEG)


def _flash_fwd(name, q, k, v, sinks, group, causal, scale):
    T = q.shape[0]
    H = q.shape[1] // LANES
    t, n, nkw = _attn_cfg(T, causal)
    kb_of = (lambda i, j: jnp.minimum(j, i)) if causal else (lambda i, j: jnp.maximum(i - 1 + j, 0))

    def kern(sink_ref, q_ref, k_ref, v_ref, o_ref, lse_ref, m_s, l_s, acc):
        h, i, j = pl.program_id(0), pl.program_id(1), pl.program_id(2)

        @pl.when(j == 0)
        def _():
            m_s[...] = jnp.full(m_s.shape, sink_ref[h], F32)
            l_s[...] = jnp.ones_like(l_s)
            acc[...] = jnp.zeros_like(acc)

        active = (j <= i) if causal else (i - 1 + j >= 0)

        @pl.when(active)
        def _():
            s = _scores(q_ref[...], k_ref[...], scale, i, kb_of(i, j), t, causal)
            m_old = m_s[...]
            m_new = jnp.maximum(m_old, jnp.max(s, axis=-1, keepdims=True))
            alpha = jnp.exp(m_old - m_new)
            p = jnp.exp(s - m_new)
            l_s[...] = alpha * l_s[...] + jnp.sum(p, axis=-1, keepdims=True)
            acc[...] = alpha * acc[...] + jnp.dot(p.astype(MXU_DTYPE), v_ref[...],
                                                  preferred_element_type=F32)
            m_s[...] = m_new

        @pl.when(j == nkw - 1)
        def _():
            l = l_s[...]
            o_ref[...] = acc[...] / l
            lse_ref[...] = jnp.broadcast_to(m_s[...] + jnp.log(l), lse_ref.shape)

    q_spec = pl.BlockSpec((t, LANES), lambda h, i, j: (i, h))
    kv_spec = pl.BlockSpec((t, LANES), lambda h, i, j: (kb_of(i, j), h // group))
    return pl.pallas_call(
        kern, name=name,
        out_shape=[jax.ShapeDtypeStruct((T, H * LANES), F32)] * 2,
        grid=(H, n, nkw),
        in_specs=[pl.BlockSpec(memory_space=pltpu.SMEM), q_spec, kv_spec, kv_spec],
        out_specs=[q_spec, q_spec],
        scratch_shapes=[pltpu.VMEM((t, 1), F32), pltpu.VMEM((t, 1), F32), pltpu.VMEM((t, LANES), F32)],
        compiler_params=_params(("parallel", "parallel", "arbitrary")),
    )(sinks, q, k, v)


def _flash_dq(name, q, k, v, do, lse, delta, group, causal, scale):
    T = q.shape[0]
    H = q.shape[1] // LANES
    t, n, nkw = _attn_cfg(T, causal)
    kb_of = (lambda i, j: jnp.minimum(j, i)) if causal else (lambda i, j: jnp.maximum(i - 1 + j, 0))

    def kern(q_ref, k_ref, v_ref, do_ref, lse_ref, dl_ref, dq_ref, acc):
        i, j = pl.program_id(1), pl.program_id(2)

        @pl.when(j == 0)
        def _():
            acc[...] = jnp.zeros_like(acc)

        active = (j <= i) if causal else (i - 1 + j >= 0)

        @pl.when(active)
        def _():
            kk = k_ref[...]
            s = _scores(q_ref[...], kk, scale, i, kb_of(i, j), t, causal)
            p = jnp.exp(s - lse_ref[:, 0:1])
            dp = lax.dot_general(do_ref[...], v_ref[...], _DIMS["nt"], preferred_element_type=F32)
            ds = (p * (dp - dl_ref[:, 0:1]) * scale).astype(MXU_DTYPE)
            acc[...] += jnp.dot(ds, kk, preferred_element_type=F32)

        @pl.when(j == nkw - 1)
        def _():
            dq_ref[...] = acc[...]

    q_spec = pl.BlockSpec((t, LANES), lambda h, i, j: (i, h))
    kv_spec = pl.BlockSpec((t, LANES), lambda h, i, j: (kb_of(i, j), h // group))
    return pl.pallas_call(
        kern, name=name,
        out_shape=jax.ShapeDtypeStruct((T, H * LANES), F32),
        grid=(H, n, nkw),
        in_specs=[q_spec, kv_spec, kv_spec, q_spec, q_spec, q_spec],
        out_specs=q_spec,
        scratch_shapes=[pltpu.VMEM((t, LANES), F32)],
        compiler_params=_params(("parallel", "parallel", "arbitrary")),
    )(q, k, v, do, lse, delta)


def _flash_dkv(name, q, k, v, do, lse, delta, group, causal, scale):
    T = q.shape[0]
    H = q.shape[1] // LANES
    t, n, nqw = _attn_cfg(T, causal)
    qb_of = (lambda kk, j: jnp.maximum(j, kk)) if causal else (lambda kk, j: jnp.minimum(kk + j, n - 1))

    def kern(q_ref, k_ref, v_ref, do_ref, lse_ref, dl_ref, dk_ref, dv_ref, dk_acc, dv_acc):
        kk, j = pl.program_id(1), pl.program_id(2)

        @pl.when(j == 0)
        def _():
            dk_acc[...] = jnp.zeros_like(dk_acc)
            dv_acc[...] = jnp.zeros_like(dv_acc)

        active = (j >= kk) if causal else (kk + j <= n - 1)

        @pl.when(active)
        def _():
            qq = q_ref[...]
            dd = do_ref[...]
            s = _scores(qq, k_ref[...], scale, qb_of(kk, j), kk, t, causal)
            p = jnp.exp(s - lse_ref[:, 0:1])
            dv_acc[...] += lax.dot_general(p.astype(MXU_DTYPE), dd, _DIMS["tn"],
                                           preferred_element_type=F32)
            dp = lax.dot_general(dd, v_ref[...], _DIMS["nt"], preferred_element_type=F32)
            ds = (p * (dp - dl_ref[:, 0:1]) * scale).astype(MXU_DTYPE)
            dk_acc[...] += lax.dot_general(ds, qq, _DIMS["tn"], preferred_element_type=F32)

        @pl.when(j == nqw - 1)
        def _():
            dk_ref[...] = dk_acc[...]
            dv_ref[...] = dv_acc[...]

    q_spec = pl.BlockSpec((t, LANES), lambda h, kk, j: (qb_of(kk, j), h))
    kv_spec = pl.BlockSpec((t, LANES), lambda h, kk, j: (kk, h // group))
    o_spec = pl.BlockSpec((t, LANES), lambda h, kk, j: (kk, h))
    return pl.pallas_call(
        kern, name=name,
        out_shape=[jax.ShapeDtypeStruct((T, H * LANES), F32)] * 2,
        grid=(H, n, nqw),
        in_specs=[q_spec, kv_spec, kv_spec, q_spec, q_spec, q_spec],
        out_specs=[o_spec, o_spec],
        scratch_shapes=[pltpu.VMEM((t, LANES), F32), pltpu.VMEM((t, LANES), F32)],
        compiler_params=_params(("parallel", "parallel", "arbitrary")),
    )(q, k, v, do, lse, delta)


def _ew(name, fn, ins, out_dtypes):
    shape = ins[0].shape
    flat = [a.reshape(-1, shape[-1]) for a in ins]
    R, C = flat[0].shape
    tr = _tile(R, (512, 256, 128, 64, 32, 16, 8))
    n_in = len(ins)

    def kern(*refs):
        res = fn(*[r[...] for r in refs[:n_in]])
        for o_ref, val in zip(refs[n_in:], res):
            o_ref[...] = val.astype(o_ref.dtype)

    spec = pl.BlockSpec((tr, C), lambda i: (i, 0))
    outs = pl.pallas_call(
        kern, name=name,
        out_shape=[jax.ShapeDtypeStruct((R, C), dt) for dt in out_dtypes],
        grid=(R // tr,), in_specs=[spec] * n_in, out_specs=[spec] * len(out_dtypes),
        compiler_params=_params(("parallel",)),
    )(*flat)
    return [o.reshape(shape) for o in outs]


def _adamw(name, w, g, m, v):
    c1 = 1.0 - ADAM_B1 ** ADAM_STEP
    c2 = 1.0 - ADAM_B2 ** ADAM_STEP

    def fn(wv, gv, mv, vv):
        mn = ADAM_B1 * mv + (1.0 - ADAM_B1) * gv
        vn = ADAM_B2 * vv + (1.0 - ADAM_B2) * (gv * gv)
        delta = -ADAM_LR * ((mn / c1) / (jnp.sqrt(vn / c2) + ADAM_EPS) + ADAM_WD * wv)
        return delta, mn, vn

    return _ew(name, fn, [w, g, m, v], [F32, F32, F32])


_ANY = pl.BlockSpec(memory_space=pl.ANY)


def _where_am_i():
    x, y, c = lax.axis_index("x"), lax.axis_index("y"), lax.axis_index("c")
    chips = [(1 - x, y), (x, 1 - y), (1 - x, 1 - y)]
    return x, y, c, chips


def _gather_weights(shards, meta):
    arrs = list(shards) + [meta]
    n = len(arrs)
    per = [2] * len(shards) + [8]

    def body(*refs):
        ins, outs = refs[:n], refs[n:2 * n]
        send1, recv1, send2, recv2, lsem = refs[2 * n:]
        x, y, c, chips = _where_am_i()
        me = 2 * x + y

        def half(ref, k, cc):
            return ref.at[pl.ds(per[k] * cc, per[k])]

        local = [pltpu.make_async_copy(ins[k], outs[k].at[me], lsem.at[k]) for k in range(n)]
        for cp in local:
            cp.start()
        first = []
        for k in range(n):
            for j, (cx, cy) in enumerate(chips):
                first.append(pltpu.make_async_remote_copy(
                    src_ref=half(ins[k], k, c), dst_ref=half(outs[k].at[me], k, c),
                    send_sem=send1.at[k, j], recv_sem=recv1.at[k, j],
                    device_id=(cx, cy, c), device_id_type=MESH))
        for cp in first:
            cp.start()
        passed = []
        for k in range(n):
            for j, (cx, cy) in enumerate(chips):
                landed = half(outs[k].at[2 * cx + cy], k, c)
                pltpu.make_async_remote_copy(
                    src_ref=landed, dst_ref=landed, send_sem=send1.at[k, j], recv_sem=recv1.at[k, j],
                    device_id=(cx, cy, c), device_id_type=MESH).wait_recv()
                fwd = pltpu.make_async_remote_copy(
                    src_ref=landed, dst_ref=landed, send_sem=send2.at[k, j], recv_sem=recv2.at[k, j],
                    device_id=(x, y, 1 - c), device_id_type=MESH)
                fwd.start()
                passed.append(fwd)
        for k in range(n):
            for j, (cx, cy) in enumerate(chips):
                other = half(outs[k].at[2 * cx + cy], k, 1 - c)
                pltpu.make_async_remote_copy(
                    src_ref=other, dst_ref=other, send_sem=send2.at[k, j], recv_sem=recv2.at[k, j],
                    device_id=(x, y, 1 - c), device_id_type=MESH).wait_recv()
        for cp in first + passed:
            cp.wait_send()
        for cp in local:
            cp.wait()

    return pl.pallas_call(
        body, name="gather_weights",
        out_shape=[jax.ShapeDtypeStruct((4,) + a.shape, a.dtype) for a in arrs],
        in_specs=[_ANY] * n, out_specs=[_ANY] * n,
        scratch_shapes=[pltpu.SemaphoreType.DMA((n, 3))] * 4 + [pltpu.SemaphoreType.DMA((n,))],
    )(*arrs)


def _pair_split(grads):
    n = len(grads)

    def body(*refs):
        ins, own, got = refs[:n], refs[n:2 * n], refs[2 * n:3 * n]
        send, recv, lsem = refs[3 * n:]
        x, y, c, _ = _where_am_i()
        cps = []
        for k in range(n):
            cps.append(pltpu.make_async_copy(ins[k].at[:, pl.ds(2 * c, 2)], own[k], lsem.at[k]))
            cps.append(pltpu.make_async_remote_copy(
                src_ref=ins[k].at[:, pl.ds(2 * (1 - c), 2)], dst_ref=got[k],
                send_sem=send.at[k], recv_sem=recv.at[k],
                device_id=(x, y, 1 - c), device_id_type=MESH))
        for cp in cps:
            cp.start()
        for cp in cps:
            cp.wait()

    half = [jax.ShapeDtypeStruct((4, 2) + g.shape[2:], g.dtype) for g in grads]
    res = pl.pallas_call(
        body, name="reduce_pair",
        out_shape=half + half,
        in_specs=[_ANY] * n, out_specs=[_ANY] * (2 * n),
        scratch_shapes=[pltpu.SemaphoreType.DMA((n,))] * 3,
    )(*grads)
    return res[:n], res[n:]


def _chip_scatter(parts):
    n = len(parts)

    def body(*refs):
        ins, outs = refs[:n], refs[n:2 * n]
        send, recv, lsem = refs[2 * n:]
        x, y, c, chips = _where_am_i()
        me = 2 * x + y
        cps = []
        for k in range(n):
            cps.append(pltpu.make_async_copy(ins[k].at[me], outs[k].at[me], lsem.at[k]))
            for j, (cx, cy) in enumerate(chips):
                cps.append(pltpu.make_async_remote_copy(
                    src_ref=ins[k].at[2 * cx + cy], dst_ref=outs[k].at[me],
                    send_sem=send.at[k, j], recv_sem=recv.at[k, j],
                    device_id=(cx, cy, c), device_id_type=MESH))
        for cp in cps:
            cp.start()
        for k in range(n):
            for j, (cx, cy) in enumerate(chips):
                slot = outs[k].at[2 * cx + cy]
                pltpu.make_async_remote_copy(
                    src_ref=slot, dst_ref=slot, send_sem=send.at[k, j], recv_sem=recv.at[k, j],
                    device_id=(cx, cy, c), device_id_type=MESH).wait_recv()
        for k in range(n):
            cps[4 * k].wait()
            for j in range(3):
                cps[4 * k + 1 + j].wait_send()

    return pl.pallas_call(
        body, name="reduce_chips",
        out_shape=[jax.ShapeDtypeStruct(p.shape, p.dtype) for p in parts],
        in_specs=[_ANY] * n, out_specs=[_ANY] * n,
        scratch_shapes=[pltpu.SemaphoreType.DMA((n, 3))] * 2 + [pltpu.SemaphoreType.DMA((n,))],
    )(*parts)


def _pair_join(halves):
    n = len(halves)

    def body(*refs):
        ins, outs = refs[:n], refs[n:2 * n]
        send, recv, lsem = refs[2 * n:]
        x, y, c, _ = _where_am_i()
        cps = []
        for k in range(n):
            cps.append(pltpu.make_async_copy(ins[k], outs[k].at[pl.ds(2 * c, 2)], lsem.at[k]))
            cps.append(pltpu.make_async_remote_copy(
                src_ref=ins[k], dst_ref=outs[k].at[pl.ds(2 * c, 2)],
                send_sem=send.at[k], recv_sem=recv.at[k],
                device_id=(x, y, 1 - c), device_id_type=MESH))
        for cp in cps:
            cp.start()
        for k in range(n):
            cps[2 * k].wait()
            cps[2 * k + 1].wait_send()
            theirs = outs[k].at[pl.ds(2 * (1 - c), 2)]
            pltpu.make_async_remote_copy(
                src_ref=theirs, dst_ref=theirs, send_sem=send.at[k], recv_sem=recv.at[k],
                device_id=(x, y, 1 - c), device_id_type=MESH).wait_recv()

    return pl.pallas_call(
        body, name="reduce_join",
        out_shape=[jax.ShapeDtypeStruct((4,) + h.shape[1:], h.dtype) for h in halves],
        in_specs=[_ANY] * n, out_specs=[_ANY] * n,
        scratch_shapes=[pltpu.SemaphoreType.DMA((n,))] * 3,
    )(*halves)


def _allreduce_small(buf):
    R = buf.shape[0]

    def body(in_ref, out_ref, land, send, recv):
        x, y, c, _ = _where_am_i()
        me = 4 * x + 2 * y + c
        land[me] = in_ref[...]
        cps = []
        for k in range(1, 8):
            px, py, pc = x ^ (k >> 2), y ^ ((k >> 1) & 1), c ^ (k & 1)
            cps.append(pltpu.make_async_remote_copy(
                src_ref=in_ref, dst_ref=land.at[me], send_sem=send.at[k - 1], recv_sem=recv.at[k - 1],
                device_id=(px, py, pc), device_id_type=MESH))
        for cp in cps:
            cp.start()
        for k in range(1, 8):
            px, py, pc = x ^ (k >> 2), y ^ ((k >> 1) & 1), c ^ (k & 1)
            slot = land.at[4 * px + 2 * py + pc]
            pltpu.make_async_remote_copy(
                src_ref=slot, dst_ref=slot, send_sem=send.at[k - 1], recv_sem=recv.at[k - 1],
                device_id=(px, py, pc), device_id_type=MESH).wait_recv()
        for cp in cps:
            cp.wait_send()
        tot = land[0]
        for d in range(1, 8):
            tot = tot + land[d]
        out_ref[...] = tot

    vm = pl.BlockSpec(memory_space=pltpu.VMEM)
    return pl.pallas_call(
        body, name="allreduce_small",
        out_shape=jax.ShapeDtypeStruct(buf.shape, F32),
        in_specs=[vm], out_specs=vm,
        scratch_shapes=[pltpu.VMEM((8, R, LANES), F32), pltpu.SemaphoreType.DMA((7,)),
                        pltpu.SemaphoreType.DMA((7,))],
    )(buf)


def _rope_tables(T):
    pos = (jnp.arange(T) - FRONT).astype(F32)
    lane = jnp.arange(LANES)
    inv_a = ROPE_THETA ** (-(2 * ((lane % 64) % 32)).astype(F32) / SWA_HEAD_DIM)
    ang_a = pos[:, None] * inv_a[None, :]
    cosa, sina = jnp.cos(ang_a), jnp.sin(ang_a)
    inv_m = ROPE_THETA ** (-(2 * ((lane - KR_LANE) % 16)).astype(F32) / MLA_ROPE_DIM)
    ang_m = pos[:, None] * inv_m[None, :]
    on = ((lane >= KR_LANE) & (lane < KR_LANE + MLA_ROPE_DIM))[None, :]
    cosm = jnp.where(on, jnp.cos(ang_m), 1.0)
    sinm = jnp.where(on, jnp.sin(ang_m), 0.0)
    return cosa, sina, cosm, sinm


def _cols_from_chips(g):
    return jnp.concatenate([g[j] for j in range(4)], axis=-1)


def _rows_from_chips(g):
    return jnp.concatenate([g[j] for j in range(4)], axis=-2)


def _cols_to_chips(w):
    L, r, c4 = w.shape
    return jnp.moveaxis(w.reshape(L, r, 4, c4 // 4), 2, 0)


def _rows_to_chips(w):
    L, r4, c = w.shape
    return jnp.moveaxis(w.reshape(L, 4, r4 // 4, c), 1, 0)


def _local_step(x2, target, meta_full, natural, p):
    T = BLOCK + x2.shape[0]
    L = DEPTH
    W_in, W_qup, W_kvup, W_o, W_gate, W_up, W_down = natural
    zpad = lambda n: jnp.zeros((L, D_MODEL, n), W_in.dtype)
    W_in = jnp.concatenate([W_in[..., :C_KR], zpad(KR_LANE), W_in[..., C_KR:IN_W],
                            zpad(LANES - KR_LANE - MLA_ROPE_DIM)], axis=-1)
    W_qup = W_qup.reshape(L, MLA_Q_RANK, MLA_HEADS, MLA_QK_DIM)
    W_qup = jnp.pad(W_qup, ((0, 0), (0, 0), (0, 0), (0, LANES - MLA_QK_DIM))).reshape(L, MLA_Q_RANK, SLOT_W)
    attn_norm, q_norm, kv_norm, sinks = p["attn_norm"], p["q_norm"], p["kv_norm"], p["sinks"]
    out_norm_swa, out_norm_mla, ffn_norm, final_norm = (
        p["out_norm_swa"], p["out_norm_mla"], p["ffn_norm"], p["final_norm"])

    cosa, sina, cosm, sinm = _rope_tables(T)
    no_sink = jnp.full((MLA_HEADS,), NEG, F32)
    scale_a, scale_b = SWA_HEAD_DIM ** -0.5, MLA_QK_DIM ** -0.5
    row = lambda v: v.reshape(1, -1)

    h = jnp.concatenate([jnp.zeros((FRONT, D_MODEL), F32), meta_full, x2], axis=0)
    saved = []
    for l in range(L):
        u = _rmsnorm_fwd("attn_norm", h, row(attn_norm[l]))
        proj = _mm("in_proj", u, W_in[l], "nn")
        qa, ka, va, qn, cn, kr = _prep1(proj, row(q_norm[l]), row(kv_norm[l]), cosa, sina, cosm, sinm)
        qb = _mm("q_up", qn, W_qup[l], "nn")
        kvb = _mm("kv_up", cn, W_kvup[l], "nn")
        qs, ks, vs = _prep2(qb, kvb, kr, cosm, sinm)
        oa, lse_a = _flash_fwd("swa_fwd", qa, ka, va, sinks[l], 4, False, scale_a)
        ob, lse_b = _flash_fwd("mla_fwd", qs, ks, vs, no_sink, 1, True, scale_b)
        mix = _merge_fwd(oa, ob, row(out_norm_swa[l]), row(out_norm_mla[l]))
        h1 = _mm("o_proj", mix, W_o[l], "nn", res=h)
        u2 = _rmsnorm_fwd("ffn_norm", h1, row(ffn_norm[l]))
        a = _mm("gate_proj", u2, W_gate[l], "nn")
        b = _mm("up_proj", u2, W_up[l], "nn")
        hm = _swiglu_fwd(a, b)
        h2 = _mm("down_proj", hm, W_down[l], "nn", res=h1)
        saved.append((h, u, proj, qa, ka, va, qn, cn, qs, ks, vs, oa, lse_a, ob, lse_b, mix, h1, u2, a, b, hm))
        h = h2

    dh, d_final, loss_row = _loss_head(h, target, row(final_norm))

    gw = {k: [None] * L for k in ("in", "qup", "kvup", "o", "gate", "up", "down")}
    gs = {k: [None] * L for k in ("attn", "qn", "kvn", "sink", "ga", "gb", "ffn")}
    for l in reversed(range(L)):
        (h0, u, proj, qa, ka, va, qn, cn, qs, ks, vs, oa, lse_a, ob, lse_b, mix, h1, u2, a, b, hm) = saved[l]
        gw["down"][l] = _mm("down_dw", hm, dh, "tn")
        dhm = _mm("down_dx", dh, W_down[l], "nt")
        da, db = _swiglu_bwd(a, b, dhm)
        gw["gate"][l] = _mm("gate_dw", u2, da, "tn")
        gw["up"][l] = _mm("up_dw", u2, db, "tn")
        du2 = _mm("gate_dx", da, W_gate[l], "nt")
        du2 = _mm("up_dx", db, W_up[l], "nt", res=du2)
        dh1, gs["ffn"][l] = _rmsnorm_bwd("ffn_norm_bwd", h1, row(ffn_norm[l]), du2, dh)
        gw["o"][l] = _mm("o_dw", mix, dh1, "tn")
        dmix = _mm("o_dx", dh1, W_o[l], "nt")
        sink_slots = jnp.repeat(sinks[l], LANES).reshape(1, SLOT_W)
        doa, dla, dob, dlb, gs["ga"][l], gs["gb"][l], dsink = _merge_bwd(
            dmix, oa, ob, lse_a, row(out_norm_swa[l]), row(out_norm_mla[l]), sink_slots)
        gs["sink"][l] = dsink.reshape(SWA_HEADS, LANES)[:, 0]
        dqs = _flash_dq("mla_dq", qs, ks, vs, dob, lse_b, dlb, 1, True, scale_b)
        dks, dvs = _flash_dkv("mla_dkv", qs, ks, vs, dob, lse_b, dlb, 1, True, scale_b)
        dqa = _flash_dq("swa_dq", qa, ka, va, doa, lse_a, dla, 4, False, scale_a)
        dka, dva = _flash_dkv("swa_dkv", qa, ka, va, doa, lse_a, dla, 4, False, scale_a)
        dqb, dkvb, dkr = _prep2_bwd(dqs, dks, dvs, cosm, sinm)
        gw["qup"][l] = _mm("q_up_dw", qn, dqb, "tn")
        gw["kvup"][l] = _mm("kv_up_dw", cn, dkvb, "tn")
        dqn = _mm("q_up_dx", dqb, W_qup[l], "nt")
        dcn = _mm("kv_up_dx", dkvb, W_kvup[l], "nt")
        dproj, gs["qn"][l], gs["kvn"][l] = _prep1_bwd(
            proj, dqa, dka, dva, dqn, dcn, dkr, cosa, sina, row(q_norm[l]), row(kv_norm[l]))
        gw["in"][l] = _mm("in_dw", u, dproj, "tn")
        du = _mm("in_dx", dproj, W_in[l], "nt")
        dh, gs["attn"][l] = _rmsnorm_bwd("attn_norm_bwd", h0, row(attn_norm[l]), du, dh1)

    st = lambda k: jnp.stack(gw[k])
    d_in = st("in")
    d_in = jnp.concatenate([d_in[..., :C_KR], d_in[..., C_KR + KR_LANE:C_KR + KR_LANE + MLA_ROPE_DIM]], axis=-1)
    d_qup = st("qup").reshape(L, MLA_Q_RANK, MLA_HEADS, LANES)[..., :MLA_QK_DIM].reshape(L, MLA_Q_RANK, -1)
    d_nat = [d_in, d_qup, st("kvup"), st("o"), st("gate"), st("up"), st("down")]
    return loss_row, dh, d_nat, gs, d_final


def kernel(x, meta_tokens, attn_norm, w_in, q_norm, w_q_up, kv_norm, w_kv_up, sinks, out_norm_swa, out_norm_mla, w_o, ffn_norm, w_gate, w_up, w_down, final_norm, loss_target, m_meta_tokens, m_attn_norm, m_w_in, m_q_norm, m_w_q_up, m_kv_norm, m_w_kv_up, m_sinks, m_out_norm_swa, m_out_norm_mla, m_w_o, m_ffn_norm, m_w_gate, m_w_up, m_w_down, m_final_norm, v_meta_tokens, v_attn_norm, v_w_in, v_q_norm, v_w_q_up, v_kv_norm, v_w_kv_up, v_sinks, v_out_norm_swa, v_out_norm_mla, v_w_o, v_ffn_norm, v_w_gate, v_w_up, v_w_down, v_final_norm):
    assert x.shape[0] == 1 and x.shape[1] % BLOCK == 0
    big = [w_in, w_q_up, w_kv_up, w_o, w_gate, w_up, w_down]

    gathered = _gather_weights([w.astype(BF16) for w in big], meta_tokens)
    g_in, g_qup, g_kvup, g_o, g_gate, g_up, g_down, g_meta = gathered
    natural = [_cols_from_chips(g_in), _cols_from_chips(g_qup), _cols_from_chips(g_kvup), _rows_from_chips(g_o),
               _cols_from_chips(g_gate), _cols_from_chips(g_up), _rows_from_chips(g_down)]
    meta_full = jnp.concatenate([g_meta[j] for j in range(4)], axis=-1)
    small_p = dict(attn_norm=attn_norm, q_norm=q_norm, kv_norm=kv_norm, sinks=sinks, out_norm_swa=out_norm_swa,
                   out_norm_mla=out_norm_mla, ffn_norm=ffn_norm, final_norm=final_norm)
    loss_row, dh, d_nat, gs, d_final = _local_step(x[0], loss_target[0], meta_full, natural, small_p)
    grad_x = dh[BLOCK:][None]

    split = [_cols_to_chips, _cols_to_chips, _cols_to_chips, _rows_to_chips, _cols_to_chips, _cols_to_chips,
             _rows_to_chips]
    full = [f(d) for f, d in zip(split, d_nat)]

    own, got = _pair_split(full)
    chip_sum = [_ew("pair_add", lambda p, q: (p + q,), [o, g], [F32])[0] for o, g in zip(own, got)]
    landed = _chip_scatter(chip_sum)
    halves = []
    for t in landed:
        parts = [t[j] for j in range(4)]
        halves.append(_ew("chip_add", lambda p0, p1, p2, p3: (((p0 + p1) + p2) + p3,), parts, [F32])[0])
    g_big = _pair_join(halves)

    small = [jnp.stack(gs["attn"]).reshape(-1), jnp.stack(gs["qn"]).reshape(-1), jnp.stack(gs["kvn"]).reshape(-1),
             jnp.stack(gs["sink"]).reshape(-1), jnp.stack(gs["ga"]).reshape(-1), jnp.stack(gs["gb"]).reshape(-1),
             jnp.stack(gs["ffn"]).reshape(-1), d_final.reshape(-1)]
    sizes = [s.shape[0] for s in small]
    flat = jnp.concatenate(small + [dh[FRONT:BLOCK].reshape(-1), loss_row[0, :1]])
    n_flat = flat.shape[0]
    rows_needed = -(-n_flat // (8 * LANES)) * 8
    flat = jnp.pad(flat, (0, rows_needed * LANES - n_flat)).reshape(rows_needed, LANES)
    tot = _allreduce_small(flat).reshape(-1)
    n_small = sum(sizes)
    loss = tot[n_small + N_META * D_MODEL]
    g_meta_full = tot[n_small:n_small + N_META * D_MODEL].reshape(N_META, D_MODEL)
    chip = 2 * lax.axis_index("x") + lax.axis_index("y")
    g_meta_mine = lax.dynamic_slice_in_dim(g_meta_full, chip * (D_MODEL // 4), D_MODEL // 4, axis=1)

    small_w = [attn_norm, q_norm, kv_norm, sinks, out_norm_swa, out_norm_mla, ffn_norm, final_norm]
    small_m = [m_attn_norm, m_q_norm, m_kv_norm, m_sinks, m_out_norm_swa, m_out_norm_mla, m_ffn_norm, m_final_norm]
    small_v = [v_attn_norm, v_q_norm, v_kv_norm, v_sinks, v_out_norm_swa, v_out_norm_mla, v_ffn_norm, v_final_norm]
    n_rows = -(-n_small // (8 * LANES)) * 8

    def pack(arrs):
        f = jnp.concatenate([a.reshape(-1) for a in arrs])
        return jnp.pad(f, (0, n_rows * LANES - n_small), constant_values=1.0).reshape(n_rows, LANES)

    g_small_pack = jnp.pad(tot[:n_small], (0, n_rows * LANES - n_small)).reshape(n_rows, LANES)
    upd_small = _adamw("adam_small", pack(small_w), g_small_pack, pack(small_m), pack(small_v))

    def unpack(p):
        f = p.reshape(-1)
        out, off = [], 0
        for a, n in zip(small_w, sizes):
            out.append(f[off:off + n].reshape(a.shape))
            off += n
        return out

    g_small = unpack(g_small_pack)
    d_small, m_small, v_small = [unpack(p) for p in upd_small]
    d_meta, nm_meta, nv_meta = _adamw("adam_meta", meta_tokens, g_meta_mine, m_meta_tokens, v_meta_tokens)

    big_m = [m_w_in, m_w_q_up, m_w_kv_up, m_w_o, m_w_gate, m_w_up, m_w_down]
    big_v = [v_w_in, v_w_q_up, v_w_kv_up, v_w_o, v_w_gate, v_w_up, v_w_down]
    upd_big = [_adamw("adam_big", w, g, m, v) for w, g, m, v in zip(big, g_big, big_m, big_v)]

    names = ["meta_tokens", "attn_norm", "w_in", "q_norm", "w_q_up", "kv_norm", "w_kv_up", "sinks",
             "out_norm_swa", "out_norm_mla", "w_o", "ffn_norm", "w_gate", "w_up", "w_down", "final_norm"]
    small_idx = {"attn_norm": 0, "q_norm": 1, "kv_norm": 2, "sinks": 3, "out_norm_swa": 4,
                 "out_norm_mla": 5, "ffn_norm": 6, "final_norm": 7}
    big_idx = {"w_in": 0, "w_q_up": 1, "w_kv_up": 2, "w_o": 3, "w_gate": 4, "w_up": 5, "w_down": 6}
    grads, deltas, new_m, new_v = [], [], [], []
    for nme in names:
        if nme == "meta_tokens":
            quad = (g_meta_mine, d_meta, nm_meta, nv_meta)
        elif nme in small_idx:
            i = small_idx[nme]
            quad = (g_small[i], d_small[i], m_small[i], v_small[i])
        else:
            i = big_idx[nme]
            quad = (g_big[i], *upd_big[i])
        grads.append(quad[0]); deltas.append(quad[1]); new_m.append(quad[2]); new_v.append(quad[3])
    return (loss, grad_x, *grads, *deltas, *new_m, *new_v)
```

```python
import jax
import jax.numpy as jnp
from jax import lax
from jax.experimental import pallas as pl
from jax.experimental.pallas import tpu as pltpu

F32 = jnp.float32
BF16 = jnp.bfloat16
MXU_DTYPE = BF16

D_MODEL = 1024
DEPTH = 4
N_META = 16
BLOCK = 128
WINDOW = 128
ROPE_THETA = 10000.0
EPS = 1e-6
NEG = -1e30
SWA_HEADS = 8
SWA_KV_HEADS = 2
SWA_HEAD_DIM = 64
MLA_HEADS = 8
MLA_Q_RANK = 256
MLA_KV_RANK = 128
MLA_NOPE_DIM = 64
MLA_ROPE_DIM = 32
MLA_V_DIM = 64
MLA_QK_DIM = MLA_NOPE_DIM + MLA_ROPE_DIM
D_FF = 2816
FRONT = (-N_META) % BLOCK
LANES = 128
SLOT_W = 8 * LANES
C_QA, C_KA, C_VA, C_QL, C_KL, C_KR, IN_WP = 0, 512, 640, 768, 1024, 1152, 1280
KR_LANE = 64
IN_W = 1184

ADAM_LR, ADAM_B1, ADAM_B2, ADAM_EPS, ADAM_WD, ADAM_STEP = 0.001, 0.9, 0.999, 1e-08, 0.01, 10

VMEM_LIMIT = 48 * 1024 * 1024
MESH = pl.DeviceIdType.MESH


def _tile(n, prefs):
    for t in prefs:
        if n % t == 0:
            return t
    return n


def _params(sem):
    return pltpu.CompilerParams(dimension_semantics=sem, vmem_limit_bytes=VMEM_LIMIT)


_DIMS = {"nn": (((1,), (0,)), ((), ())), "nt": (((1,), (1,)), ((), ())), "tn": (((0,), (0,)), ((), ()))}


def _mm(name, a, b, mode, out_dtype=F32, res=None):
    if mode == "nn":
        (M, K), (_, N) = a.shape, b.shape
    elif mode == "nt":
        (M, K), (N, _) = a.shape, b.shape
    else:
        (K, M), (_, N) = a.shape, b.shape
    lane_tiles = (1408, 1024, 640, 768, 512, 384, 256, 128)
    row_tiles = (528, 512, 384, 256, 128)
    bm = _tile(M, lane_tiles if mode == "tn" else row_tiles)
    bn = _tile(N, lane_tiles)
    bk = _tile(K, row_tiles if mode == "tn" else lane_tiles)
    nk = K // bk
    if mode == "tn":
        a_spec = pl.BlockSpec((bk, bm), lambda i, j, k: (k, i))
    else:
        a_spec = pl.BlockSpec((bm, bk), lambda i, j, k: (i, k))
    if mode == "nt":
        b_spec = pl.BlockSpec((bn, bk), lambda i, j, k: (j, k))
    else:
        b_spec = pl.BlockSpec((bk, bn), lambda i, j, k: (k, j))
    o_spec = pl.BlockSpec((bm, bn), lambda i, j, k: (i, j))
    in_specs = [a_spec, b_spec]
    args = [a, b]
    if res is not None:
        in_specs.append(o_spec)
        args.append(res)
    dims = _DIMS[mode]

    def kern(a_ref, b_ref, *rest):
        if res is not None:
            r_ref, o_ref = rest[0], rest[1]
            scr = rest[2:]
        else:
            r_ref, o_ref = None, rest[0]
            scr = rest[1:]
        p = lax.dot_general(a_ref[...].astype(MXU_DTYPE), b_ref[...].astype(MXU_DTYPE), dims,
                            preferred_element_type=F32)

        def finish(val):
            if r_ref is not None:
                val = val + r_ref[...]
            o_ref[...] = val.astype(o_ref.dtype)

        if nk == 1:
            finish(p)
        else:
            acc = scr[0]
            k = pl.program_id(2)

            @pl.when(k == 0)
            def _():
                acc[...] = p

            @pl.when(k > 0)
            def _():
                acc[...] += p

            @pl.when(k == nk - 1)
            def _():
                finish(acc[...])

    return pl.pallas_call(
        kern, name=name,
        out_shape=jax.ShapeDtypeStruct((M, N), out_dtype),
        grid=(M // bm, N // bn, nk),
        in_specs=in_specs, out_specs=o_spec,
        scratch_shapes=[pltpu.VMEM((bm, bn), F32)] if nk > 1 else [],
        compiler_params=_params(("parallel", "parallel", "arbitrary")),
    )(*args)


def _rowmap(name, body, rows, vecs, outs, accs=(), tr_prefs=(384, 256, 128)):
    R = rows[0].shape[0]
    tr = _tile(R, tr_prefs)
    n_r, n_v, n_o, n_a = len(rows), len(vecs), len(outs), len(accs)

    def kern(*refs):
        ins = [r[...] for r in refs[:n_r + n_v]]
        o_refs = refs[n_r + n_v:n_r + n_v + n_o]
        a_refs = refs[n_r + n_v + n_o:]
        res = body(*ins)
        for o_ref, val in zip(o_refs, res[:n_o]):
            o_ref[...] = val.astype(o_ref.dtype)
        if n_a:
            first = pl.program_id(0) == 0

            @pl.when(first)
            def _():
                for a_ref, val in zip(a_refs, res[n_o:]):
                    a_ref[...] = val

            @pl.when(jnp.logical_not(first))
            def _():
                for a_ref, val in zip(a_refs, res[n_o:]):
                    a_ref[...] += val

    in_specs = [pl.BlockSpec((tr, r.shape[1]), lambda i: (i, 0)) for r in rows]
    in_specs += [pl.BlockSpec((1, v.shape[1]), lambda i: (0, 0)) for v in vecs]
    out_specs = [pl.BlockSpec((tr, c), lambda i: (i, 0)) for c, _ in outs]
    out_specs += [pl.BlockSpec((1, c), lambda i: (0, 0)) for c in accs]
    out_shape = [jax.ShapeDtypeStruct((R, c), dt) for c, dt in outs]
    out_shape += [jax.ShapeDtypeStruct((1, c), F32) for c in accs]
    return pl.pallas_call(
        kern, name=name, out_shape=out_shape, grid=(R // tr,),
        in_specs=in_specs, out_specs=out_specs,
        compiler_params=_params(("arbitrary",) if n_a else ("parallel",)),
    )(*rows, *vecs)


def _lane(shape):
    return lax.broadcasted_iota(jnp.int32, shape, 1)


def _rot_swa(x):
    lane = _lane(x.shape)
    return jnp.where((lane & 63) < 32, -pltpu.roll(x, 96, 1), pltpu.roll(x, 32, 1))


def _rot_mla(x):
    lane = _lane(x.shape)
    lo = jnp.where(lane >= KR_LANE, -pltpu.roll(x, 112, 1), 0.0)
    hi = jnp.where(lane < KR_LANE + MLA_ROPE_DIM, pltpu.roll(x, 16, 1), 0.0)
    return jnp.where(lane < KR_LANE + 16, lo, hi)


def _rope(x, cos, sin, rot):
    return x * cos + rot(x) * sin


def _rope_t(g, cos, sin, rot):
    return g * cos - rot(g * sin)


def _low(x):
    return jnp.where(_lane(x.shape) < 64, x, 0.0)


def _blk(x, j):
    return x[:, j * LANES:(j + 1) * LANES]


def _rms_r(x, width):
    return lax.rsqrt(jnp.sum(x * x, axis=-1, keepdims=True) * (1.0 / width) + EPS)


def _rms_bwd(x, g, dy, width):
    r = _rms_r(x, width)
    gdy = dy * g
    dot = jnp.sum(gdy * x, axis=-1, keepdims=True)
    dx = r * gdy - x * (r * r * r * (1.0 / width) * dot)
    return dx, dy * x * r


def _colsum(x):
    return jnp.sum(x, axis=0, keepdims=True)


def _rmsnorm_fwd(name, x, g):
    def body(xv, gv):
        return (xv * _rms_r(xv, D_MODEL) * gv,)
    return _rowmap(name, body, [x], [g], [(D_MODEL, BF16)])[0]


def _rmsnorm_bwd(name, x, g, dy, dres):
    def body(xv, dyv, dr, gv):
        dx, dg = _rms_bwd(xv, gv, dyv, D_MODEL)
        return dx + dr, _colsum(dg)
    return _rowmap(name, body, [x, dy, dres], [g], [(D_MODEL, F32)], [D_MODEL])


def _prep1(proj, qn_g, kv_g, cosa, sina, cosm, sinm):
    def body(p, ca, sa, cm, sm, gq, gk):
        qa = []
        for j in range(4):
            xr = _rope(_blk(p, j), ca, sa, _rot_swa)
            qa += [_low(xr), _low(pltpu.roll(xr, 64, 1))]
        kr_ = _rope(_blk(p, C_KA // LANES), ca, sa, _rot_swa)
        ka = [_low(kr_), _low(pltpu.roll(kr_, 64, 1))]
        vv = _blk(p, C_VA // LANES)
        va = [_low(vv), _low(pltpu.roll(vv, 64, 1))]
        ql = p[:, C_QL:C_QL + MLA_Q_RANK]
        qn = ql * _rms_r(ql, MLA_Q_RANK) * gq
        kl = p[:, C_KL:C_KL + MLA_KV_RANK]
        cn = kl * _rms_r(kl, MLA_KV_RANK) * gk
        kr = _rope(_blk(p, C_KR // LANES), cm, sm, _rot_mla)
        return (jnp.concatenate(qa, 1), jnp.concatenate(ka, 1), jnp.concatenate(va, 1), qn, cn, kr)
    return _rowmap("prep1", body, [proj, cosa, sina, cosm, sinm], [qn_g, kv_g],
                   [(SLOT_W, BF16), (2 * LANES, BF16), (2 * LANES, BF16),
                    (MLA_Q_RANK, BF16), (MLA_KV_RANK, BF16), (LANES, F32)])


def _prep2(qb, kvb, kr, cosm, sinm):
    def body(q, kv, krv, cm, sm):
        qs, ks, vs = [], [], []
        for h in range(MLA_HEADS):
            qs.append(_rope(_blk(q, h), cm, sm, _rot_mla))
            kvh = _blk(kv, h)
            ks.append(_low(kvh) + krv)
            vs.append(_low(pltpu.roll(kvh, 64, 1)))
        return jnp.concatenate(qs, 1), jnp.concatenate(ks, 1), jnp.concatenate(vs, 1)
    return _rowmap("prep2", body, [qb, kvb, kr, cosm, sinm], [],
                   [(SLOT_W, BF16), (SLOT_W, BF16), (SLOT_W, BF16)])


def _compact(slots):
    return jnp.concatenate(
        [_blk(slots, 2 * j) + pltpu.roll(_blk(slots, 2 * j + 1), 64, 1) for j in range(4)], 1)


def _expand(nat):
    out = []
    for j in range(4):
        b = _blk(nat, j)
        out += [_low(b), _low(pltpu.roll(b, 64, 1))]
    return jnp.concatenate(out, 1)


def _merge_fwd(oa, ob, ga, gb):
    def body(a, b, gav, gbv):
        xa, xb = _compact(a), _compact(b)
        return (jnp.concatenate([xa * _rms_r(xa, 512) * gav, xb * _rms_r(xb, 512) * gbv], 1),)
    return _rowmap("merge_fwd", body, [oa, ob], [ga, gb], [(D_MODEL, BF16)])[0]


def _merge_bwd(dmix, oa, ob, lse_a, ga, gb, sink_slots):
    def body(dm, a, b, lse, gav, gbv, sk):
        outs = []
        accs = []
        for o, g, lo in ((a, gav, 0), (b, gbv, 512)):
            x = _compact(o)
            dx, dg = _rms_bwd(x, g, dm[:, lo:lo + 512], 512)
            do = _expand(dx)
            delta = jnp.concatenate(
                [jnp.broadcast_to(jnp.sum(_blk(do, h) * _blk(o, h), axis=-1, keepdims=True),
                                  (do.shape[0], LANES)) for h in range(8)], 1)
            outs += [do, delta]
            accs.append(_colsum(dg))
        dsink = _colsum(-jnp.exp(sk - lse) * outs[1])
        return (*outs, *accs, dsink)
    return _rowmap("merge_bwd", body, [dmix, oa, ob, lse_a], [ga, gb, sink_slots],
                   [(SLOT_W, BF16), (SLOT_W, F32), (SLOT_W, BF16), (SLOT_W, F32)],
                   [512, 512, SLOT_W])


def _prep2_bwd(dq, dk, dv, cosm, sinm):
    def body(dqv, dkv, dvv, cm, sm):
        dqb, dkvb = [], []
        krsum = jnp.zeros((dqv.shape[0], LANES), F32)
        for h in range(MLA_HEADS):
            dqb.append(_rope_t(_blk(dqv, h), cm, sm, _rot_mla))
            dkh = _blk(dkv, h)
            dkvb.append(_low(dkh) + pltpu.roll(_blk(dvv, h), 64, 1))
            krsum = krsum + dkh
        lane = _lane(krsum.shape)
        dkr = jnp.where((lane >= KR_LANE) & (lane < KR_LANE + MLA_ROPE_DIM),
                        _rope_t(krsum, cm, sm, _rot_mla), 0.0)
        return jnp.concatenate(dqb, 1), jnp.concatenate(dkvb, 1), dkr
    return _rowmap("prep2_bwd", body, [dq, dk, dv, cosm, sinm], [],
                   [(SLOT_W, BF16), (SLOT_W, BF16), (LANES, F32)])


def _prep1_bwd(proj, dqa, dka, dva, dqn, dcn, dkr, cosa, sina, qn_g, kv_g):
    def body(p, dq, dk, dv, dqnv, dcnv, dkrv, ca, sa, gq, gk):
        cols = []
        for j in range(4):
            nat = _blk(dq, 2 * j) + pltpu.roll(_blk(dq, 2 * j + 1), 64, 1)
            cols.append(_rope_t(nat, ca, sa, _rot_swa))
        grp = lambda d, g: sum(_blk(d, 4 * g + i) for i in range(4))
        cols.append(_rope_t(grp(dk, 0) + pltpu.roll(grp(dk, 1), 64, 1), ca, sa, _rot_swa))
        cols.append(grp(dv, 0) + pltpu.roll(grp(dv, 1), 64, 1))
        dql, dgq = _rms_bwd(p[:, C_QL:C_QL + MLA_Q_RANK], gq, dqnv, MLA_Q_RANK)
        dkl, dgk = _rms_bwd(p[:, C_KL:C_KL + MLA_KV_RANK], gk, dcnv, MLA_KV_RANK)
        cols += [dql, dkl, dkrv]
        return jnp.concatenate(cols, 1), _colsum(dgq), _colsum(dgk)
    return _rowmap("prep1_bwd", body, [proj, dqa, dka, dva, dqn, dcn, dkr, cosa, sina], [qn_g, kv_g],
                   [(IN_WP, BF16)], [MLA_Q_RANK, MLA_KV_RANK], tr_prefs=(192, 128))


def _sigmoid(x):
    return 1.0 / (1.0 + jnp.exp(-x))


def _swiglu_fwd(a, b):
    def body(av, bv):
        return (av * _sigmoid(av) * bv,)
    return _rowmap("swiglu_fwd", body, [a, b], [], [(D_FF, BF16)])[0]


def _swiglu_bwd(a, b, dh):
    def body(av, bv, d):
        s = _sigmoid(av)
        return d * bv * (s * (1.0 + av * (1.0 - s))), d * (av * s)
    return _rowmap("swiglu_bwd", body, [a, b, dh], [], [(D_FF, BF16), (D_FF, BF16)])


def _loss_head(h, target, g):
    T = h.shape[0]
    nb = T // BLOCK

    def kern(h_ref, t_ref, g_ref, dh_ref, dg_ref, loss_ref, acc):
        i = pl.program_id(0)

        @pl.when(i == 0)
        def _():
            dh_ref[...] = jnp.zeros_like(dh_ref)
            dg_ref[...] = jnp.zeros_like(dg_ref)
            acc[...] = jnp.zeros_like(acc)

        @pl.when(i > 0)
        def _():
            x = h_ref[...]
            gv = g_ref[...]
            e = x * _rms_r(x, D_MODEL) * gv - t_ref[...]
            acc[...] += _colsum(e * e)
            dx, dg = _rms_bwd(x, gv, e * (1.0 / D_MODEL), D_MODEL)
            dh_ref[...] = dx
            dg_ref[...] += _colsum(dg)

        @pl.when(i == nb - 1)
        def _():
            tot = jnp.sum(acc[...], axis=-1, keepdims=True) * (0.5 / D_MODEL)
            loss_ref[...] = jnp.broadcast_to(tot, loss_ref.shape)

    return pl.pallas_call(
        kern, name="loss_head",
        out_shape=[jax.ShapeDtypeStruct((T, D_MODEL), F32), jax.ShapeDtypeStruct((1, D_MODEL), F32),
                   jax.ShapeDtypeStruct((1, LANES), F32)],
        grid=(nb,),
        in_specs=[pl.BlockSpec((BLOCK, D_MODEL), lambda i: (i, 0)),
                  pl.BlockSpec((BLOCK, D_MODEL), lambda i: (jnp.maximum(i - 1, 0), 0)),
                  pl.BlockSpec((1, D_MODEL), lambda i: (0, 0))],
        out_specs=[pl.BlockSpec((BLOCK, D_MODEL), lambda i: (i, 0)),
                   pl.BlockSpec((1, D_MODEL), lambda i: (0, 0)),
                   pl.BlockSpec((1, LANES), lambda i: (0, 0))],
        scratch_shapes=[pltpu.VMEM((1, D_MODEL), F32)],
        compiler_params=_params(("arbitrary",)),
    )(h, target, g)


def _attn_plan(T, causal):
    tq = _tile(T, (384, 256, 128))
    ck = min(2 * tq, T) if causal else min(tq + WINDOW, T)
    return tq, ck, (-(-T // ck) if causal else 1)


def _chunk(i, c, T, tq, ck, causal):
    if causal:
        return pl.multiple_of(jnp.minimum(c * ck, T - ck), LANES), jnp.maximum(c * ck, FRONT)
    start = pl.multiple_of(jnp.clip(i * tq - WINDOW, 0, T - ck), LANES)
    return start, jnp.int32(FRONT)


def _n_chunks(i, tq, ck, causal):
    return ((i + 1) * tq + ck - 1) // ck if causal else 1


def _masked_scores(q, k, scale, i, start, lo, tq, ck, causal):
    s = lax.dot_general(q, k, _DIMS["nt"], preferred_element_type=F32) * scale
    qpos = i * tq + lax.broadcasted_iota(jnp.int32, (tq, 1), 0)
    kpos = start + lax.broadcasted_iota(jnp.int32, (tq, ck), 1)
    low = lo if causal else jnp.maximum(qpos - (WINDOW - 1), lo)
    return jnp.where(kpos >= low, jnp.where(kpos <= qpos, s, NEG), NEG)


def _attn_fwd(name, q, k, v, sinks, group, causal, scale):
    T = q.shape[0]
    H = q.shape[1] // LANES
    tq, ck, slots = _attn_plan(T, causal)
    nq = T // tq

    def kern(sink_ref, q_ref, k_ref, v_ref, o_ref, lse_ref, s_scr):
        sink = sink_ref[pl.program_id(0)]

        def q_tile(i, carry):
            rows = pl.ds(pl.multiple_of(i * tq, tq), tq)
            qq = q_ref[rows, :]
            n = _n_chunks(i, tq, ck, causal)

            def score(c, m):
                start, lo = _chunk(i, c, T, tq, ck, causal)
                s = _masked_scores(qq, k_ref[pl.ds(start, ck), :], scale, i, start, lo, tq, ck, causal)
                s_scr[c] = s
                return jnp.maximum(m, jnp.max(s, axis=-1, keepdims=True))

            m = lax.fori_loop(0, n, score, jnp.full((tq, 1), sink, F32))

            def weigh(c, carry):
                l, acc = carry
                start, _ = _chunk(i, c, T, tq, ck, causal)
                p = jnp.exp(s_scr[c] - m)
                acc = acc + jnp.dot(p.astype(MXU_DTYPE), v_ref[pl.ds(start, ck), :],
                                    preferred_element_type=F32)
                return l + jnp.sum(p, axis=-1, keepdims=True), acc

            l, acc = lax.fori_loop(0, n, weigh, (jnp.exp(sink - m), jnp.zeros((tq, LANES), F32)))
            o_ref[rows, :] = acc / l
            lse_ref[rows, :] = jnp.broadcast_to(m + jnp.log(l), (tq, LANES))
            return carry

        lax.fori_loop(0, nq, q_tile, 0)

    q_spec = pl.BlockSpec((T, LANES), lambda h: (0, h))
    kv_spec = pl.BlockSpec((T, LANES), lambda h: (0, h // group))
    return pl.pallas_call(
        kern, name=name,
        out_shape=[jax.ShapeDtypeStruct((T, H * LANES), F32)] * 2,
        grid=(H,),
        in_specs=[pl.BlockSpec(memory_space=pltpu.SMEM), q_spec, kv_spec, kv_spec],
        out_specs=[q_spec, q_spec],
        scratch_shapes=[pltpu.VMEM((slots, tq, ck), F32)],
        compiler_params=_params(("parallel",)),
    )(sinks, q, k, v)


def _attn_bwd(name, q, k, v, do, lse, delta, group, causal, scale):
    T = q.shape[0]
    H = q.shape[1] // LANES
    tq, ck, _ = _attn_plan(T, causal)
    nq = T // tq

    def kern(q_ref, k_ref, v_ref, do_ref, lse_ref, dl_ref, dq_ref, dk_ref, dv_ref):
        dk_ref[...] = jnp.zeros_like(dk_ref)
        dv_ref[...] = jnp.zeros_like(dv_ref)

        def q_tile(i, carry):
            rows = pl.ds(pl.multiple_of(i * tq, tq), tq)
            qq, dd = q_ref[rows, :], do_ref[rows, :]
            lse_c, dl_c = lse_ref[rows, :][:, 0:1], dl_ref[rows, :][:, 0:1]

            def chunk(c, dq):
                start, lo = _chunk(i, c, T, tq, ck, causal)
                keys = pl.ds(start, ck)
                kk, vv = k_ref[keys, :], v_ref[keys, :]
                s = _masked_scores(qq, kk, scale, i, start, lo, tq, ck, causal)
                p = jnp.exp(s - lse_c)
                dv_ref[keys, :] += lax.dot_general(p.astype(MXU_DTYPE), dd, _DIMS["tn"],
                                                   preferred_element_type=F32)
                dp = lax.dot_general(dd, vv, _DIMS["nt"], preferred_element_type=F32)
                ds = (p * (dp - dl_c) * scale).astype(MXU_DTYPE)
                dk_ref[keys, :] += lax.dot_general(ds, qq, _DIMS["tn"], preferred_element_type=F32)
                return dq + jnp.dot(ds, kk, preferred_element_type=F32)

            dq_ref[rows, :] = lax.fori_loop(0, _n_chunks(i, tq, ck, causal), chunk,
                                            jnp.zeros((tq, LANES), F32))
            return carry

        lax.fori_loop(0, nq, q_tile, 0)

    q_spec = pl.BlockSpec((T, LANES), lambda h: (0, h))
    kv_spec = pl.BlockSpec((T, LANES), lambda h: (0, h // group))
    return pl.pallas_call(
        kern, name=name,
        out_shape=[jax.ShapeDtypeStruct((T, H * LANES), F32)] * 3,
        grid=(H,),
        in_specs=[q_spec, kv_spec, kv_spec, q_spec, q_spec, q_spec],
        out_specs=[q_spec, q_spec, q_spec],
        compiler_params=_params(("parallel",)),
    )(q, k, v, do, lse, delta)


def _ew(name, fn, ins, out_dtypes):
    shape = ins[0].shape
    flat = [a.reshape(-1, shape[-1]) for a in ins]
    R, C = flat[0].shape
    tr = _tile(R, (512, 256, 128, 64, 32, 16, 8))
    n_in = len(ins)

    def kern(*refs):
        res = fn(*[r[...] for r in refs[:n_in]])
        for o_ref, val in zip(refs[n_in:], res):
            o_ref[...] = val.astype(o_ref.dtype)

    spec = pl.BlockSpec((tr, C), lambda i: (i, 0))
    outs = pl.pallas_call(
        kern, name=name,
        out_shape=[jax.ShapeDtypeStruct((R, C), dt) for dt in out_dtypes],
        grid=(R // tr,), in_specs=[spec] * n_in, out_specs=[spec] * len(out_dtypes),
        compiler_params=_params(("parallel",)),
    )(*flat)
    return [o.reshape(shape) for o in outs]


def _adamw(name, w, g, m, v):
    c1 = 1.0 - ADAM_B1 ** ADAM_STEP
    c2 = 1.0 - ADAM_B2 ** ADAM_STEP

    def fn(wv, gv, mv, vv):
        mn = ADAM_B1 * mv + (1.0 - ADAM_B1) * gv
        vn = ADAM_B2 * vv + (1.0 - ADAM_B2) * (gv * gv)
        delta = -ADAM_LR * ((mn / c1) / (jnp.sqrt(vn / c2) + ADAM_EPS) + ADAM_WD * wv)
        return delta, mn, vn

    return _ew(name, fn, [w, g, m, v], [F32, F32, F32])


_ANY = pl.BlockSpec(memory_space=pl.ANY)


def _where_am_i():
    x, y, c = lax.axis_index("x"), lax.axis_index("y"), lax.axis_index("c")
    chips = [(1 - x, y), (x, 1 - y), (1 - x, 1 - y)]
    return x, y, c, chips


def _gather_weights(shards, meta):
    arrs = list(shards) + [meta]
    n = len(arrs)
    per = [2] * len(shards) + [8]

    def body(*refs):
        ins, outs = refs[:n], refs[n:2 * n]
        send1, recv1, send2, recv2, lsem = refs[2 * n:]
        x, y, c, chips = _where_am_i()
        me = 2 * x + y

        def half(ref, k, cc):
            return ref.at[pl.ds(per[k] * cc, per[k])]

        local = [pltpu.make_async_copy(ins[k], outs[k].at[me], lsem.at[k]) for k in range(n)]
        for cp in local:
            cp.start()
        first = []
        for k in range(n):
            for j, (cx, cy) in enumerate(chips):
                first.append(pltpu.make_async_remote_copy(
                    src_ref=half(ins[k], k, c), dst_ref=half(outs[k].at[me], k, c),
                    send_sem=send1.at[k, j], recv_sem=recv1.at[k, j],
                    device_id=(cx, cy, c), device_id_type=MESH))
        for cp in first:
            cp.start()
        passed = []
        for k in range(n):
            for j, (cx, cy) in enumerate(chips):
                landed = half(outs[k].at[2 * cx + cy], k, c)
                pltpu.make_async_remote_copy(
                    src_ref=landed, dst_ref=landed, send_sem=send1.at[k, j], recv_sem=recv1.at[k, j],
                    device_id=(cx, cy, c), device_id_type=MESH).wait_recv()
                fwd = pltpu.make_async_remote_copy(
                    src_ref=landed, dst_ref=landed, send_sem=send2.at[k, j], recv_sem=recv2.at[k, j],
                    device_id=(x, y, 1 - c), device_id_type=MESH)
                fwd.start()
                passed.append(fwd)
        for k in range(n):
            for j, (cx, cy) in enumerate(chips):
                other = half(outs[k].at[2 * cx + cy], k, 1 - c)
                pltpu.make_async_remote_copy(
                    src_ref=other, dst_ref=other, send_sem=send2.at[k, j], recv_sem=recv2.at[k, j],
                    device_id=(x, y, 1 - c), device_id_type=MESH).wait_recv()
        for cp in first + passed:
            cp.wait_send()
        for cp in local:
            cp.wait()

    return pl.pallas_call(
        body, name="gather_weights",
        out_shape=[jax.ShapeDtypeStruct((4,) + a.shape, a.dtype) for a in arrs],
        in_specs=[_ANY] * n, out_specs=[_ANY] * n,
        scratch_shapes=[pltpu.SemaphoreType.DMA((n, 3))] * 4 + [pltpu.SemaphoreType.DMA((n,))],
    )(*arrs)


def _row_chunks(r):
    n = 4 if r % 64 == 0 else 1
    return [(q * (r // n), r // n) for q in range(n)]


def _pair_exchange(grads):
    n = len(grads)

    def body(*refs):
        ins, got = refs[:n], refs[n:2 * n]
        send, recv = refs[2 * n:]
        x, y, c, _ = _where_am_i()
        sib = (x, y, 1 - c)
        for k in range(n):
            for ch in range(4):
                for l in range(2):
                    pltpu.make_async_remote_copy(
                        src_ref=ins[k].at[ch, 2 * (1 - c) + l], dst_ref=got[k].at[ch, l],
                        send_sem=send.at[k], recv_sem=recv.at[k], device_id=sib, device_id_type=MESH).start()
        for k in range(n):
            pltpu.make_async_remote_copy(
                src_ref=got[k], dst_ref=got[k], send_sem=send.at[k], recv_sem=recv.at[k],
                device_id=sib, device_id_type=MESH).wait()

    return pl.pallas_call(
        body, name="reduce_pair",
        out_shape=[jax.ShapeDtypeStruct((4, 2) + g.shape[2:], g.dtype) for g in grads],
        in_specs=[_ANY] * n, out_specs=[_ANY] * n,
        scratch_shapes=[pltpu.SemaphoreType.DMA((n,))] * 2,
    )(*grads)


def _pair_add(full, got, c):
    _, _, r, cols = full.shape
    tr = _tile(r, (512, 256, 352, 128))

    def kern(c_ref, a_ref, b_ref, o32_ref, o16_ref):
        tot = a_ref[...] + b_ref[...].astype(F32)
        o32_ref[...] = tot
        o16_ref[...] = tot.astype(o16_ref.dtype)

    blk = (None, None, tr, cols)
    mine = pl.BlockSpec(blk, lambda ch, l, i, cr: (ch, 2 * cr[0] + l, i, 0))
    same = pl.BlockSpec(blk, lambda ch, l, i, cr: (ch, l, i, 0))
    return pl.pallas_call(
        kern, name="pair_add",
        out_shape=[jax.ShapeDtypeStruct(got.shape, F32), jax.ShapeDtypeStruct(got.shape, got.dtype)],
        grid_spec=pltpu.PrefetchScalarGridSpec(
            num_scalar_prefetch=1, grid=(4, 2, r // tr), in_specs=[mine, same], out_specs=[same, same]),
        compiler_params=_params(("parallel", "parallel", "parallel")),
    )(c.reshape(1), full, got)


def _chip_scatter(parts):
    n = len(parts)

    def body(*refs):
        ins, outs = refs[:n], refs[n:2 * n]
        send, recv, lsem = refs[2 * n:]
        x, y, c, chips = _where_am_i()
        me = 2 * x + y
        local = [pltpu.make_async_copy(ins[k].at[me], outs[k].at[me], lsem.at[k]) for k in range(n)]
        for cp in local:
            cp.start()
        for k in range(n):
            for j, (cx, cy) in enumerate(chips):
                for l in range(2):
                    pltpu.make_async_remote_copy(
                        src_ref=ins[k].at[2 * cx + cy, l], dst_ref=outs[k].at[me, l],
                        send_sem=send.at[k, j], recv_sem=recv.at[k, j],
                        device_id=(cx, cy, c), device_id_type=MESH).start()
        for k in range(n):
            for j, (cx, cy) in enumerate(chips):
                slot = outs[k].at[2 * cx + cy]
                pltpu.make_async_remote_copy(
                    src_ref=slot, dst_ref=slot, send_sem=send.at[k, j], recv_sem=recv.at[k, j],
                    device_id=(cx, cy, c), device_id_type=MESH).wait()
        for cp in local:
            cp.wait()

    return pl.pallas_call(
        body, name="reduce_chips",
        out_shape=[jax.ShapeDtypeStruct(p.shape, p.dtype) for p in parts],
        in_specs=[_ANY] * n, out_specs=[_ANY] * n,
        scratch_shapes=[pltpu.SemaphoreType.DMA((n, 3))] * 2 + [pltpu.SemaphoreType.DMA((n,))],
    )(*parts)


def _chip_add(landed, mine, me):
    _, _, r, cols = landed.shape
    tr = _tile(r, (512, 256, 352, 128))

    def kern(me_ref, land_ref, own_ref, o_ref):
        own = own_ref[...]
        tot = None
        for j in range(4):
            term = jnp.where(me_ref[0] == j, own, land_ref[j].astype(F32))
            tot = term if tot is None else tot + term
        o_ref[...] = tot

    return pl.pallas_call(
        kern, name="chip_add",
        out_shape=jax.ShapeDtypeStruct(landed.shape[1:], F32),
        grid_spec=pltpu.PrefetchScalarGridSpec(
            num_scalar_prefetch=1, grid=(2, r // tr),
            in_specs=[pl.BlockSpec((4, None, tr, cols), lambda l, i, mr: (0, l, i, 0)),
                      pl.BlockSpec((None, None, tr, cols), lambda l, i, mr: (mr[0], l, i, 0))],
            out_specs=pl.BlockSpec((None, tr, cols), lambda l, i, mr: (l, i, 0))),
        compiler_params=_params(("parallel", "parallel")),
    )(me.reshape(1), landed, mine)


def _pair_join(halves):
    n = len(halves)

    def body(*refs):
        ins, outs = refs[:n], refs[n:2 * n]
        send, recv, lsem = refs[2 * n:]
        x, y, c, _ = _where_am_i()
        sib = (x, y, 1 - c)
        local = [pltpu.make_async_copy(ins[k], outs[k].at[pl.ds(2 * c, 2)], lsem.at[k]) for k in range(n)]
        for cp in local:
            cp.start()
        for k in range(n):
            for l in range(2):
                for r0, rn in _row_chunks(ins[k].shape[1]):
                    pltpu.make_async_remote_copy(
                        src_ref=ins[k].at[l, pl.ds(r0, rn)], dst_ref=outs[k].at[2 * c + l, pl.ds(r0, rn)],
                        send_sem=send.at[k], recv_sem=recv.at[k], device_id=sib, device_id_type=MESH).start()
        for k in range(n):
            theirs = outs[k].at[pl.ds(2 * (1 - c), 2)]
            pltpu.make_async_remote_copy(
                src_ref=theirs, dst_ref=theirs, send_sem=send.at[k], recv_sem=recv.at[k],
                device_id=sib, device_id_type=MESH).wait()
        for cp in local:
            cp.wait()

    return pl.pallas_call(
        body, name="reduce_join",
        out_shape=[jax.ShapeDtypeStruct((4,) + h.shape[1:], h.dtype) for h in halves],
        in_specs=[_ANY] * n, out_specs=[_ANY] * n,
        scratch_shapes=[pltpu.SemaphoreType.DMA((n,))] * 3,
    )(*halves)


def _allreduce_small(buf):
    R = buf.shape[0]

    def body(in_ref, out_ref, land, send, recv):
        x, y, c, _ = _where_am_i()
        me = 4 * x + 2 * y + c
        land[me] = in_ref[...]
        cps = []
        for k in range(1, 8):
            px, py, pc = x ^ (k >> 2), y ^ ((k >> 1) & 1), c ^ (k & 1)
            cps.append(pltpu.make_async_remote_copy(
                src_ref=in_ref, dst_ref=land.at[me], send_sem=send.at[k - 1], recv_sem=recv.at[k - 1],
                device_id=(px, py, pc), device_id_type=MESH))
        for cp in cps:
            cp.start()
        for k in range(1, 8):
            px, py, pc = x ^ (k >> 2), y ^ ((k >> 1) & 1), c ^ (k & 1)
            slot = land.at[4 * px + 2 * py + pc]
            pltpu.make_async_remote_copy(
                src_ref=slot, dst_ref=slot, send_sem=send.at[k - 1], recv_sem=recv.at[k - 1],
                device_id=(px, py, pc), device_id_type=MESH).wait_recv()
        for cp in cps:
            cp.wait_send()
        tot = land[0]
        for d in range(1, 8):
            tot = tot + land[d]
        out_ref[...] = tot

    vm = pl.BlockSpec(memory_space=pltpu.VMEM)
    return pl.pallas_call(
        body, name="allreduce_small",
        out_shape=jax.ShapeDtypeStruct(buf.shape, F32),
        in_specs=[vm], out_specs=vm,
        scratch_shapes=[pltpu.VMEM((8, R, LANES), F32), pltpu.SemaphoreType.DMA((7,)),
                        pltpu.SemaphoreType.DMA((7,))],
    )(buf)


def _rope_tables(T):
    pos = (jnp.arange(T) - FRONT).astype(F32)
    lane = jnp.arange(LANES)
    inv_a = ROPE_THETA ** (-(2 * ((lane % 64) % 32)).astype(F32) / SWA_HEAD_DIM)
    ang_a = pos[:, None] * inv_a[None, :]
    cosa, sina = jnp.cos(ang_a), jnp.sin(ang_a)
    inv_m = ROPE_THETA ** (-(2 * ((lane - KR_LANE) % 16)).astype(F32) / MLA_ROPE_DIM)
    ang_m = pos[:, None] * inv_m[None, :]
    on = ((lane >= KR_LANE) & (lane < KR_LANE + MLA_ROPE_DIM))[None, :]
    cosm = jnp.where(on, jnp.cos(ang_m), 1.0)
    sinm = jnp.where(on, jnp.sin(ang_m), 0.0)
    return cosa, sina, cosm, sinm


def _cols_from_chips(g):
    return jnp.concatenate([g[j] for j in range(4)], axis=-1)


def _rows_from_chips(g):
    return jnp.concatenate([g[j] for j in range(4)], axis=-2)


def _cols_to_chips(w):
    L, r, c4 = w.shape
    return jnp.moveaxis(w.reshape(L, r, 4, c4 // 4), 2, 0)


def _rows_to_chips(w):
    L, r4, c = w.shape
    return jnp.moveaxis(w.reshape(L, 4, r4 // 4, c), 1, 0)


def _local_step(x2, target, meta_full, natural, p):
    T = BLOCK + x2.shape[0]
    L = DEPTH
    W_in, W_qup, W_kvup, W_o, W_gate, W_up, W_down = natural
    zpad = lambda n: jnp.zeros((L, D_MODEL, n), W_in.dtype)
    W_in = jnp.concatenate([W_in[..., :C_KR], zpad(KR_LANE), W_in[..., C_KR:IN_W],
                            zpad(LANES - KR_LANE - MLA_ROPE_DIM)], axis=-1)
    W_qup = W_qup.reshape(L, MLA_Q_RANK, MLA_HEADS, MLA_QK_DIM)
    W_qup = jnp.pad(W_qup, ((0, 0), (0, 0), (0, 0), (0, LANES - MLA_QK_DIM))).reshape(L, MLA_Q_RANK, SLOT_W)
    attn_norm, q_norm, kv_norm, sinks = p["attn_norm"], p["q_norm"], p["kv_norm"], p["sinks"]
    out_norm_swa, out_norm_mla, ffn_norm, final_norm = (
        p["out_norm_swa"], p["out_norm_mla"], p["ffn_norm"], p["final_norm"])

    cosa, sina, cosm, sinm = _rope_tables(T)
    no_sink = jnp.full((MLA_HEADS,), NEG, F32)
    scale_a, scale_b = SWA_HEAD_DIM ** -0.5, MLA_QK_DIM ** -0.5
    row = lambda v: v.reshape(1, -1)

    h = jnp.concatenate([jnp.zeros((FRONT, D_MODEL), F32), meta_full, x2], axis=0)
    saved = []
    for l in range(L):
        u = _rmsnorm_fwd("attn_norm", h, row(attn_norm[l]))
        proj = _mm("in_proj", u, W_in[l], "nn")
        qa, ka, va, qn, cn, kr = _prep1(proj, row(q_norm[l]), row(kv_norm[l]), cosa, sina, cosm, sinm)
        qb = _mm("q_up", qn, W_qup[l], "nn")
        kvb = _mm("kv_up", cn, W_kvup[l], "nn")
        qs, ks, vs = _prep2(qb, kvb, kr, cosm, sinm)
        oa, lse_a = _attn_fwd("swa_fwd", qa, ka, va, sinks[l], 4, False, scale_a)
        ob, lse_b = _attn_fwd("mla_fwd", qs, ks, vs, no_sink, 1, True, scale_b)
        mix = _merge_fwd(oa, ob, row(out_norm_swa[l]), row(out_norm_mla[l]))
        h1 = _mm("o_proj", mix, W_o[l], "nn", res=h)
        u2 = _rmsnorm_fwd("ffn_norm", h1, row(ffn_norm[l]))
        a = _mm("gate_proj", u2, W_gate[l], "nn")
        b = _mm("up_proj", u2, W_up[l], "nn")
        hm = _swiglu_fwd(a, b)
        h2 = _mm("down_proj", hm, W_down[l], "nn", res=h1)
        saved.append((h, u, proj, qa, ka, va, qn, cn, qs, ks, vs, oa, lse_a, ob, lse_b, mix, h1, u2, a, b, hm))
        h = h2

    dh, d_final, loss_row = _loss_head(h, target, row(final_norm))

    gw = {k: [None] * L for k in ("in", "qup", "kvup", "o", "gate", "up", "down")}
    gs = {k: [None] * L for k in ("attn", "qn", "kvn", "sink", "ga", "gb", "ffn")}
    for l in reversed(range(L)):
        (h0, u, proj, qa, ka, va, qn, cn, qs, ks, vs, oa, lse_a, ob, lse_b, mix, h1, u2, a, b, hm) = saved[l]
        gw["down"][l] = _mm("down_dw", hm, dh, "tn")
        dhm = _mm("down_dx", dh, W_down[l], "nt")
        da, db = _swiglu_bwd(a, b, dhm)
        gw["gate"][l] = _mm("gate_dw", u2, da, "tn")
        gw["up"][l] = _mm("up_dw", u2, db, "tn")
        du2 = _mm("gate_dx", da, W_gate[l], "nt")
        du2 = _mm("up_dx", db, W_up[l], "nt", res=du2)
        dh1, gs["ffn"][l] = _rmsnorm_bwd("ffn_norm_bwd", h1, row(ffn_norm[l]), du2, dh)
        gw["o"][l] = _mm("o_dw", mix, dh1, "tn")
        dmix = _mm("o_dx", dh1, W_o[l], "nt")
        sink_slots = jnp.repeat(sinks[l], LANES).reshape(1, SLOT_W)
        doa, dla, dob, dlb, gs["ga"][l], gs["gb"][l], dsink = _merge_bwd(
            dmix, oa, ob, lse_a, row(out_norm_swa[l]), row(out_norm_mla[l]), sink_slots)
        gs["sink"][l] = dsink.reshape(SWA_HEADS, LANES)[:, 0]
        dqs, dks, dvs = _attn_bwd("mla_bwd", qs, ks, vs, dob, lse_b, dlb, 1, True, scale_b)
        dqa, dka, dva = _attn_bwd("swa_bwd", qa, ka, va, doa, lse_a, dla, 4, False, scale_a)
        dqb, dkvb, dkr = _prep2_bwd(dqs, dks, dvs, cosm, sinm)
        gw["qup"][l] = _mm("q_up_dw", qn, dqb, "tn")
        gw["kvup"][l] = _mm("kv_up_dw", cn, dkvb, "tn")
        dqn = _mm("q_up_dx", dqb, W_qup[l], "nt")
        dcn = _mm("kv_up_dx", dkvb, W_kvup[l], "nt")
        dproj, gs["qn"][l], gs["kvn"][l] = _prep1_bwd(
            proj, dqa, dka, dva, dqn, dcn, dkr, cosa, sina, row(q_norm[l]), row(kv_norm[l]))
        gw["in"][l] = _mm("in_dw", u, dproj, "tn")
        du = _mm("in_dx", dproj, W_in[l], "nt")
        dh, gs["attn"][l] = _rmsnorm_bwd("attn_norm_bwd", h0, row(attn_norm[l]), du, dh1)

    st = lambda k: jnp.stack(gw[k])
    d_in = st("in")
    d_in = jnp.concatenate([d_in[..., :C_KR], d_in[..., C_KR + KR_LANE:C_KR + KR_LANE + MLA_ROPE_DIM]], axis=-1)
    d_qup = st("qup").reshape(L, MLA_Q_RANK, MLA_HEADS, LANES)[..., :MLA_QK_DIM].reshape(L, MLA_Q_RANK, -1)
    d_nat = [d_in, d_qup, st("kvup"), st("o"), st("gate"), st("up"), st("down")]
    return loss_row, dh, d_nat, gs, d_final


def kernel(x, meta_tokens, attn_norm, w_in, q_norm, w_q_up, kv_norm, w_kv_up, sinks, out_norm_swa, out_norm_mla, w_o, ffn_norm, w_gate, w_up, w_down, final_norm, loss_target, m_meta_tokens, m_attn_norm, m_w_in, m_q_norm, m_w_q_up, m_kv_norm, m_w_kv_up, m_sinks, m_out_norm_swa, m_out_norm_mla, m_w_o, m_ffn_norm, m_w_gate, m_w_up, m_w_down, m_final_norm, v_meta_tokens, v_attn_norm, v_w_in, v_q_norm, v_w_q_up, v_kv_norm, v_w_kv_up, v_sinks, v_out_norm_swa, v_out_norm_mla, v_w_o, v_ffn_norm, v_w_gate, v_w_up, v_w_down, v_final_norm):
    assert x.shape[0] == 1 and x.shape[1] % BLOCK == 0
    big = [w_in, w_q_up, w_kv_up, w_o, w_gate, w_up, w_down]

    gathered = _gather_weights([w.astype(BF16) for w in big], meta_tokens)
    g_in, g_qup, g_kvup, g_o, g_gate, g_up, g_down, g_meta = gathered
    natural = [_cols_from_chips(g_in), _cols_from_chips(g_qup), _cols_from_chips(g_kvup), _rows_from_chips(g_o),
               _cols_from_chips(g_gate), _cols_from_chips(g_up), _rows_from_chips(g_down)]
    meta_full = jnp.concatenate([g_meta[j] for j in range(4)], axis=-1)
    small_p = dict(attn_norm=attn_norm, q_norm=q_norm, kv_norm=kv_norm, sinks=sinks, out_norm_swa=out_norm_swa,
                   out_norm_mla=out_norm_mla, ffn_norm=ffn_norm, final_norm=final_norm)
    loss_row, dh, d_nat, gs, d_final = _local_step(x[0], loss_target[0], meta_full, natural, small_p)
    grad_x = dh[BLOCK:][None]

    split = [_cols_to_chips, _cols_to_chips, _cols_to_chips, _rows_to_chips, _cols_to_chips, _cols_to_chips,
             _rows_to_chips]
    full = [f(d) for f, d in zip(split, d_nat)]

    c_idx = lax.axis_index("c").astype(jnp.int32)
    chip = (2 * lax.axis_index("x") + lax.axis_index("y")).astype(jnp.int32)
    got = _pair_exchange([f.astype(BF16) for f in full])
    sums = [_pair_add(f, g, c_idx) for f, g in zip(full, got)]
    landed = _chip_scatter([s16 for _, s16 in sums])
    halves = [_chip_add(t, s32, chip) for t, (s32, _) in zip(landed, sums)]
    g_big = _pair_join(halves)

    small = [jnp.stack(gs["attn"]).reshape(-1), jnp.stack(gs["qn"]).reshape(-1), jnp.stack(gs["kvn"]).reshape(-1),
             jnp.stack(gs["sink"]).reshape(-1), jnp.stack(gs["ga"]).reshape(-1), jnp.stack(gs["gb"]).reshape(-1),
             jnp.stack(gs["ffn"]).reshape(-1), d_final.reshape(-1)]
    sizes = [s.shape[0] for s in small]
    flat = jnp.concatenate(small + [dh[FRONT:BLOCK].reshape(-1), loss_row[0, :1]])
    n_flat = flat.shape[0]
    rows_needed = -(-n_flat // (8 * LANES)) * 8
    flat = jnp.pad(flat, (0, rows_needed * LANES - n_flat)).reshape(rows_needed, LANES)
    tot = _allreduce_small(flat).reshape(-1)
    n_small = sum(sizes)
    loss = tot[n_small + N_META * D_MODEL]
    g_meta_full = tot[n_small:n_small + N_META * D_MODEL].reshape(N_META, D_MODEL)
    g_meta_mine = lax.dynamic_slice_in_dim(g_meta_full, chip * (D_MODEL // 4), D_MODEL // 4, axis=1)

    small_w = [attn_norm, q_norm, kv_norm, sinks, out_norm_swa, out_norm_mla, ffn_norm, final_norm]
    small_m = [m_attn_norm, m_q_norm, m_kv_norm, m_sinks, m_out_norm_swa, m_out_norm_mla, m_ffn_norm, m_final_norm]
    small_v = [v_attn_norm, v_q_norm, v_kv_norm, v_sinks, v_out_norm_swa, v_out_norm_mla, v_ffn_norm, v_final_norm]
    n_rows = -(-n_small // (8 * LANES)) * 8

    def pack(arrs):
        f = jnp.concatenate([a.reshape(-1) for a in arrs])
        return jnp.pad(f, (0, n_rows * LANES - n_small), constant_values=1.0).reshape(n_rows, LANES)

    g_small_pack = jnp.pad(tot[:n_small], (0, n_rows * LANES - n_small)).reshape(n_rows, LANES)
    upd_small = _adamw("adam_small", pack(small_w), g_small_pack, pack(small_m), pack(small_v))

    def unpack(p):
        f = p.reshape(-1)
        out, off = [], 0
        for a, n in zip(small_w, sizes):
            out.append(f[off:off + n].reshape(a.shape))
            off += n
        return out

    g_small = unpack(g_small_pack)
    d_small, m_small, v_small = [unpack(p) for p in upd_small]
    d_meta, nm_meta, nv_meta = _adamw("adam_meta", meta_tokens, g_meta_mine, m_meta_tokens, v_meta_tokens)

    big_m = [m_w_in, m_w_q_up, m_w_kv_up, m_w_o, m_w_gate, m_w_up, m_w_down]
    big_v = [v_w_in, v_w_q_up, v_w_kv_up, v_w_o, v_w_gate, v_w_up, v_w_down]
    upd_big = [_adamw("adam_big", w, g, m, v) for w, g, m, v in zip(big, g_big, big_m, big_v)]

    names = ["meta_tokens", "attn_norm", "w_in", "q_norm", "w_q_up", "kv_norm", "w_kv_up", "sinks",
             "out_norm_swa", "out_norm_mla", "w_o", "ffn_norm", "w_gate", "w_up", "w_down", "final_norm"]
    small_idx = {"attn_norm": 0, "q_norm": 1, "kv_norm": 2, "sinks": 3, "out_norm_swa": 4,
                 "out_norm_mla": 5, "ffn_norm": 6, "final_norm": 7}
    big_idx = {"w_in": 0, "w_q_up": 1, "w_kv_up": 2, "w_o": 3, "w_gate": 4, "w_up": 5, "w_down": 6}
    grads, deltas, new_m, new_v = [], [], [], []
    for nme in names:
        if nme == "meta_tokens":
            quad = (g_meta_mine, d_meta, nm_meta, nv_meta)
        elif nme in small_idx:
            i = small_idx[nme]
            quad = (g_small[i], d_small[i], m_small[i], v_small[i])
        else:
            i = big_idx[nme]
            quad = (g_big[i], *upd_big[i])
        grads.append(quad[0]); deltas.append(quad[1]); new_m.append(quad[2]); new_v.append(quad[3])
    return (loss, grad_x, *grads, *deltas, *new_m, *new_v)
```

```python
import jax
import jax.numpy as jnp
from jax import lax
from jax.experimental import pallas as pl
from jax.experimental.pallas import tpu as pltpu

F32 = jnp.float32
BF16 = jnp.bfloat16
MXU_DTYPE = BF16

D_MODEL = 1024
DEPTH = 4
N_META = 16
BLOCK = 128
WINDOW = 128
ROPE_THETA = 10000.0
EPS = 1e-6
NEG = -1e30
SWA_HEADS = 8
SWA_KV_HEADS = 2
SWA_HEAD_DIM = 64
MLA_HEADS = 8
MLA_Q_RANK = 256
MLA_KV_RANK = 128
MLA_NOPE_DIM = 64
MLA_ROPE_DIM = 32
MLA_V_DIM = 64
MLA_QK_DIM = MLA_NOPE_DIM + MLA_ROPE_DIM
D_FF = 2816
FRONT = (-N_META) % BLOCK
LANES = 128
SLOT_W = 8 * LANES
C_QA, C_KA, C_VA, C_QL, C_KL, C_KR, IN_WP = 0, 512, 640, 768, 1024, 1152, 1280
KR_LANE = 64
IN_W = 1184

ADAM_LR, ADAM_B1, ADAM_B2, ADAM_EPS, ADAM_WD, ADAM_STEP = 0.001, 0.9, 0.999, 1e-08, 0.01, 10

VMEM_LIMIT = 48 * 1024 * 1024
MESH = pl.DeviceIdType.MESH


def _tile(n, prefs):
    for t in prefs:
        if n % t == 0:
            return t
    return n


def _params(sem):
    return pltpu.CompilerParams(dimension_semantics=sem, vmem_limit_bytes=VMEM_LIMIT)


_DIMS = {"nn": (((1,), (0,)), ((), ())), "nt": (((1,), (1,)), ((), ())), "tn": (((0,), (0,)), ((), ()))}


def _mm(name, a, b, mode, out_dtype=F32, res=None):
    if mode == "nn":
        (M, K), (_, N) = a.shape, b.shape
    elif mode == "nt":
        (M, K), (N, _) = a.shape, b.shape
    else:
        (K, M), (_, N) = a.shape, b.shape
    lane_tiles = (1408, 1024, 640, 768, 512, 384, 256, 128)
    row_tiles = (528, 512, 384, 256, 128)
    bm = _tile(M, lane_tiles if mode == "tn" else row_tiles)
    bn = _tile(N, lane_tiles)
    bk = _tile(K, row_tiles if mode == "tn" else lane_tiles)
    nk = K // bk
    if mode == "tn":
        a_spec = pl.BlockSpec((bk, bm), lambda i, j, k: (k, i))
    else:
        a_spec = pl.BlockSpec((bm, bk), lambda i, j, k: (i, k))
    if mode == "nt":
        b_spec = pl.BlockSpec((bn, bk), lambda i, j, k: (j, k))
    else:
        b_spec = pl.BlockSpec((bk, bn), lambda i, j, k: (k, j))
    o_spec = pl.BlockSpec((bm, bn), lambda i, j, k: (i, j))
    in_specs = [a_spec, b_spec]
    args = [a, b]
    if res is not None:
        in_specs.append(o_spec)
        args.append(res)
    dims = _DIMS[mode]

    def kern(a_ref, b_ref, *rest):
        if res is not None:
            r_ref, o_ref = rest[0], rest[1]
            scr = rest[2:]
        else:
            r_ref, o_ref = None, rest[0]
            scr = rest[1:]
        p = lax.dot_general(a_ref[...].astype(MXU_DTYPE), b_ref[...].astype(MXU_DTYPE), dims,
                            preferred_element_type=F32)

        def finish(val):
            if r_ref is not None:
                val = val + r_ref[...]
            o_ref[...] = val.astype(o_ref.dtype)

        if nk == 1:
            finish(p)
        else:
            acc = scr[0]
            k = pl.program_id(2)

            @pl.when(k == 0)
            def _():
                acc[...] = p

            @pl.when(k > 0)
            def _():
                acc[...] += p

            @pl.when(k == nk - 1)
            def _():
                finish(acc[...])

    return pl.pallas_call(
        kern, name=name,
        out_shape=jax.ShapeDtypeStruct((M, N), out_dtype),
        grid=(M // bm, N // bn, nk),
        in_specs=in_specs, out_specs=o_spec,
        scratch_shapes=[pltpu.VMEM((bm, bn), F32)] if nk > 1 else [],
        compiler_params=_params(("parallel", "parallel", "arbitrary")),
    )(*args)


def _rowmap(name, body, rows, vecs, outs, accs=(), tr_prefs=(384, 256, 128)):
    R = rows[0].shape[0]
    tr = _tile(R, tr_prefs)
    n_r, n_v, n_o, n_a = len(rows), len(vecs), len(outs), len(accs)

    def kern(*refs):
        ins = [r[...] for r in refs[:n_r + n_v]]
        o_refs = refs[n_r + n_v:n_r + n_v + n_o]
        a_refs = refs[n_r + n_v + n_o:]
        res = body(*ins)
        for o_ref, val in zip(o_refs, res[:n_o]):
            o_ref[...] = val.astype(o_ref.dtype)
        if n_a:
            first = pl.program_id(0) == 0

            @pl.when(first)
            def _():
                for a_ref, val in zip(a_refs, res[n_o:]):
                    a_ref[...] = val

            @pl.when(jnp.logical_not(first))
            def _():
                for a_ref, val in zip(a_refs, res[n_o:]):
                    a_ref[...] += val

    in_specs = [pl.BlockSpec((tr, r.shape[1]), lambda i: (i, 0)) for r in rows]
    in_specs += [pl.BlockSpec((1, v.shape[1]), lambda i: (0, 0)) for v in vecs]
    out_specs = [pl.BlockSpec((tr, c), lambda i: (i, 0)) for c, _ in outs]
    out_specs += [pl.BlockSpec((1, c), lambda i: (0, 0)) for c in accs]
    out_shape = [jax.ShapeDtypeStruct((R, c), dt) for c, dt in outs]
    out_shape += [jax.ShapeDtypeStruct((1, c), F32) for c in accs]
    return pl.pallas_call(
        kern, name=name, out_shape=out_shape, grid=(R // tr,),
        in_specs=in_specs, out_specs=out_specs,
        compiler_params=_params(("arbitrary",) if n_a else ("parallel",)),
    )(*rows, *vecs)


def _lane(shape):
    return lax.broadcasted_iota(jnp.int32, shape, 1)


def _rot_swa(x):
    lane = _lane(x.shape)
    return jnp.where((lane & 63) < 32, -pltpu.roll(x, 96, 1), pltpu.roll(x, 32, 1))


def _rot_mla(x):
    lane = _lane(x.shape)
    lo = jnp.where(lane >= KR_LANE, -pltpu.roll(x, 112, 1), 0.0)
    hi = jnp.where(lane < KR_LANE + MLA_ROPE_DIM, pltpu.roll(x, 16, 1), 0.0)
    return jnp.where(lane < KR_LANE + 16, lo, hi)


def _rope(x, cos, sin, rot):
    return x * cos + rot(x) * sin


def _rope_t(g, cos, sin, rot):
    return g * cos - rot(g * sin)


def _low(x):
    return jnp.where(_lane(x.shape) < 64, x, 0.0)


def _blk(x, j):
    return x[:, j * LANES:(j + 1) * LANES]


def _rms_r(x, width):
    return lax.rsqrt(jnp.sum(x * x, axis=-1, keepdims=True) * (1.0 / width) + EPS)


def _rms_bwd(x, g, dy, width):
    r = _rms_r(x, width)
    gdy = dy * g
    dot = jnp.sum(gdy * x, axis=-1, keepdims=True)
    dx = r * gdy - x * (r * r * r * (1.0 / width) * dot)
    return dx, dy * x * r


def _colsum(x):
    return jnp.sum(x, axis=0, keepdims=True)


def _rmsnorm_fwd(name, x, g):
    def body(xv, gv):
        return (xv * _rms_r(xv, D_MODEL) * gv,)
    return _rowmap(name, body, [x], [g], [(D_MODEL, BF16)])[0]


def _rmsnorm_bwd(name, x, g, dy, dres):
    def body(xv, dyv, dr, gv):
        dx, dg = _rms_bwd(xv, gv, dyv, D_MODEL)
        return dx + dr, _colsum(dg)
    return _rowmap(name, body, [x, dy, dres], [g], [(D_MODEL, F32)], [D_MODEL])


def _prep1(proj, qn_g, kv_g, cosa, sina, cosm, sinm):
    def body(p, ca, sa, cm, sm, gq, gk):
        qa = []
        for j in range(4):
            xr = _rope(_blk(p, j), ca, sa, _rot_swa)
            qa += [_low(xr), _low(pltpu.roll(xr, 64, 1))]
        kr_ = _rope(_blk(p, C_KA // LANES), ca, sa, _rot_swa)
        ka = [_low(kr_), _low(pltpu.roll(kr_, 64, 1))]
        vv = _blk(p, C_VA // LANES)
        va = [_low(vv), _low(pltpu.roll(vv, 64, 1))]
        ql = p[:, C_QL:C_QL + MLA_Q_RANK]
        qn = ql * _rms_r(ql, MLA_Q_RANK) * gq
        kl = p[:, C_KL:C_KL + MLA_KV_RANK]
        cn = kl * _rms_r(kl, MLA_KV_RANK) * gk
        kr = _rope(_blk(p, C_KR // LANES), cm, sm, _rot_mla)
        return (jnp.concatenate(qa, 1), jnp.concatenate(ka, 1), jnp.concatenate(va, 1), qn, cn, kr)
    return _rowmap("prep1", body, [proj, cosa, sina, cosm, sinm], [qn_g, kv_g],
                   [(SLOT_W, BF16), (2 * LANES, BF16), (2 * LANES, BF16),
                    (MLA_Q_RANK, BF16), (MLA_KV_RANK, BF16), (LANES, F32)])


def _prep2(qb, kvb, kr, cosm, sinm):
    def body(q, kv, krv, cm, sm):
        qs, ks, vs = [], [], []
        for h in range(MLA_HEADS):
            qs.append(_rope(_blk(q, h), cm, sm, _rot_mla))
            kvh = _blk(kv, h)
            ks.append(_low(kvh) + krv)
            vs.append(_low(pltpu.roll(kvh, 64, 1)))
        return jnp.concatenate(qs, 1), jnp.concatenate(ks, 1), jnp.concatenate(vs, 1)
    return _rowmap("prep2", body, [qb, kvb, kr, cosm, sinm], [],
                   [(SLOT_W, BF16), (SLOT_W, BF16), (SLOT_W, BF16)])


def _compact(slots):
    return jnp.concatenate(
        [_blk(slots, 2 * j) + pltpu.roll(_blk(slots, 2 * j + 1), 64, 1) for j in range(4)], 1)


def _expand(nat):
    out = []
    for j in range(4):
        b = _blk(nat, j)
        out += [_low(b), _low(pltpu.roll(b, 64, 1))]
    return jnp.concatenate(out, 1)


def _merge_fwd(oa, ob, ga, gb):
    def body(a, b, gav, gbv):
        xa, xb = _compact(a), _compact(b)
        return (jnp.concatenate([xa * _rms_r(xa, 512) * gav, xb * _rms_r(xb, 512) * gbv], 1),)
    return _rowmap("merge_fwd", body, [oa, ob], [ga, gb], [(D_MODEL, BF16)])[0]


def _merge_bwd(dmix, oa, ob, lse_a, ga, gb, sink_slots):
    def body(dm, a, b, lse, gav, gbv, sk):
        outs = []
        accs = []
        for o, g, lo in ((a, gav, 0), (b, gbv, 512)):
            x = _compact(o)
            dx, dg = _rms_bwd(x, g, dm[:, lo:lo + 512], 512)
            do = _expand(dx)
            delta = jnp.concatenate(
                [jnp.broadcast_to(jnp.sum(_blk(do, h) * _blk(o, h), axis=-1, keepdims=True),
                                  (do.shape[0], LANES)) for h in range(8)], 1)
            outs += [do, delta]
            accs.append(_colsum(dg))
        dsink = _colsum(-jnp.exp(sk - lse) * outs[1])
        return (*outs, *accs, dsink)
    return _rowmap("merge_bwd", body, [dmix, oa, ob, lse_a], [ga, gb, sink_slots],
                   [(SLOT_W, BF16), (SLOT_W, F32), (SLOT_W, BF16), (SLOT_W, F32)],
                   [512, 512, SLOT_W])


def _prep2_bwd(dq, dk, dv, cosm, sinm):
    def body(dqv, dkv, dvv, cm, sm):
        dqb, dkvb = [], []
        krsum = jnp.zeros((dqv.shape[0], LANES), F32)
        for h in range(MLA_HEADS):
            dqb.append(_rope_t(_blk(dqv, h), cm, sm, _rot_mla))
            dkh = _blk(dkv, h)
            dkvb.append(_low(dkh) + pltpu.roll(_blk(dvv, h), 64, 1))
            krsum = krsum + dkh
        lane = _lane(krsum.shape)
        dkr = jnp.where((lane >= KR_LANE) & (lane < KR_LANE + MLA_ROPE_DIM),
                        _rope_t(krsum, cm, sm, _rot_mla), 0.0)
        return jnp.concatenate(dqb, 1), jnp.concatenate(dkvb, 1), dkr
    return _rowmap("prep2_bwd", body, [dq, dk, dv, cosm, sinm], [],
                   [(SLOT_W, BF16), (SLOT_W, BF16), (LANES, F32)])


def _prep1_bwd(proj, dqa, dka, dva, dqn, dcn, dkr, cosa, sina, qn_g, kv_g):
    def body(p, dq, dk, dv, dqnv, dcnv, dkrv, ca, sa, gq, gk):
        cols = []
        for j in range(4):
            nat = _blk(dq, 2 * j) + pltpu.roll(_blk(dq, 2 * j + 1), 64, 1)
            cols.append(_rope_t(nat, ca, sa, _rot_swa))
        grp = lambda d, g: sum(_blk(d, 4 * g + i) for i in range(4))
        cols.append(_rope_t(grp(dk, 0) + pltpu.roll(grp(dk, 1), 64, 1), ca, sa, _rot_swa))
        cols.append(grp(dv, 0) + pltpu.roll(grp(dv, 1), 64, 1))
        dql, dgq = _rms_bwd(p[:, C_QL:C_QL + MLA_Q_RANK], gq, dqnv, MLA_Q_RANK)
        dkl, dgk = _rms_bwd(p[:, C_KL:C_KL + MLA_KV_RANK], gk, dcnv, MLA_KV_RANK)
        cols += [dql, dkl, dkrv]
        return jnp.concatenate(cols, 1), _colsum(dgq), _colsum(dgk)
    return _rowmap("prep1_bwd", body, [proj, dqa, dka, dva, dqn, dcn, dkr, cosa, sina], [qn_g, kv_g],
                   [(IN_WP, BF16)], [MLA_Q_RANK, MLA_KV_RANK], tr_prefs=(192, 128))


def _sigmoid(x):
    return 1.0 / (1.0 + jnp.exp(-x))


def _swiglu_fwd(a, b):
    def body(av, bv):
        return (av * _sigmoid(av) * bv,)
    return _rowmap("swiglu_fwd", body, [a, b], [], [(D_FF, BF16)])[0]


def _swiglu_bwd(a, b, dh):
    def body(av, bv, d):
        s = _sigmoid(av)
        return d * bv * (s * (1.0 + av * (1.0 - s))), d * (av * s)
    return _rowmap("swiglu_bwd", body, [a, b, dh], [], [(D_FF, BF16), (D_FF, BF16)])


def _loss_head(h, target, g):
    T = h.shape[0]
    nb = T // BLOCK

    def kern(h_ref, t_ref, g_ref, dh_ref, dg_ref, loss_ref, acc):
        i = pl.program_id(0)

        @pl.when(i == 0)
        def _():
            dh_ref[...] = jnp.zeros_like(dh_ref)
            dg_ref[...] = jnp.zeros_like(dg_ref)
            acc[...] = jnp.zeros_like(acc)

        @pl.when(i > 0)
        def _():
            x = h_ref[...]
            gv = g_ref[...]
            e = x * _rms_r(x, D_MODEL) * gv - t_ref[...]
            acc[...] += _colsum(e * e)
            dx, dg = _rms_bwd(x, gv, e * (1.0 / D_MODEL), D_MODEL)
            dh_ref[...] = dx
            dg_ref[...] += _colsum(dg)

        @pl.when(i == nb - 1)
        def _():
            tot = jnp.sum(acc[...], axis=-1, keepdims=True) * (0.5 / D_MODEL)
            loss_ref[...] = jnp.broadcast_to(tot, loss_ref.shape)

    return pl.pallas_call(
        kern, name="loss_head",
        out_shape=[jax.ShapeDtypeStruct((T, D_MODEL), F32), jax.ShapeDtypeStruct((1, D_MODEL), F32),
                   jax.ShapeDtypeStruct((1, LANES), F32)],
        grid=(nb,),
        in_specs=[pl.BlockSpec((BLOCK, D_MODEL), lambda i: (i, 0)),
                  pl.BlockSpec((BLOCK, D_MODEL), lambda i: (jnp.maximum(i - 1, 0), 0)),
                  pl.BlockSpec((1, D_MODEL), lambda i: (0, 0))],
        out_specs=[pl.BlockSpec((BLOCK, D_MODEL), lambda i: (i, 0)),
                   pl.BlockSpec((1, D_MODEL), lambda i: (0, 0)),
                   pl.BlockSpec((1, LANES), lambda i: (0, 0))],
        scratch_shapes=[pltpu.VMEM((1, D_MODEL), F32)],
        compiler_params=_params(("arbitrary",)),
    )(h, target, g)


def _attn_plan(T, causal):
    tq = _tile(T, (384, 256, 128))
    ck = min(2 * tq, T) if causal else min(tq + WINDOW, T)
    return tq, ck, (-(-T // ck) if causal else 1)


def _chunk(i, c, T, tq, ck, causal):
    if causal:
        return pl.multiple_of(jnp.minimum(c * ck, T - ck), LANES), jnp.maximum(c * ck, FRONT)
    start = pl.multiple_of(jnp.clip(i * tq - WINDOW, 0, T - ck), LANES)
    return start, jnp.int32(FRONT)


def _n_chunks(i, tq, ck, causal):
    return ((i + 1) * tq + ck - 1) // ck if causal else 1


def _masked_scores(q, k, scale, i, start, lo, tq, ck, causal):
    s = lax.dot_general(q, k, _DIMS["nt"], preferred_element_type=F32) * scale
    qpos = i * tq + lax.broadcasted_iota(jnp.int32, (tq, 1), 0)
    kpos = start + lax.broadcasted_iota(jnp.int32, (tq, ck), 1)
    low = lo if causal else jnp.maximum(qpos - (WINDOW - 1), lo)
    return jnp.where(kpos >= low, jnp.where(kpos <= qpos, s, NEG), NEG)


def _attn_fwd(name, q, k, v, sinks, group, causal, scale):
    T = q.shape[0]
    H = q.shape[1] // LANES
    tq, ck, slots = _attn_plan(T, causal)
    nq = T // tq

    def kern(sink_ref, q_ref, k_ref, v_ref, o_ref, lse_ref, s_scr):
        sink = sink_ref[pl.program_id(0)]

        def q_tile(i, carry):
            rows = pl.ds(pl.multiple_of(i * tq, tq), tq)
            qq = q_ref[rows, :]
            n = _n_chunks(i, tq, ck, causal)

            def score(c, m):
                start, lo = _chunk(i, c, T, tq, ck, causal)
                s = _masked_scores(qq, k_ref[pl.ds(start, ck), :], scale, i, start, lo, tq, ck, causal)
                s_scr[c] = s
                return jnp.maximum(m, jnp.max(s, axis=-1, keepdims=True))

            m = lax.fori_loop(0, n, score, jnp.full((tq, 1), sink, F32))

            def weigh(c, carry):
                l, acc = carry
                start, _ = _chunk(i, c, T, tq, ck, causal)
                p = jnp.exp(s_scr[c] - m)
                acc = acc + jnp.dot(p.astype(MXU_DTYPE), v_ref[pl.ds(start, ck), :],
                                    preferred_element_type=F32)
                return l + jnp.sum(p, axis=-1, keepdims=True), acc

            l, acc = lax.fori_loop(0, n, weigh, (jnp.exp(sink - m), jnp.zeros((tq, LANES), F32)))
            o_ref[rows, :] = acc / l
            lse_ref[rows, :] = jnp.broadcast_to(m + jnp.log(l), (tq, LANES))
            return carry

        lax.fori_loop(0, nq, q_tile, 0)

    q_spec = pl.BlockSpec((T, LANES), lambda h: (0, h))
    kv_spec = pl.BlockSpec((T, LANES), lambda h: (0, h // group))
    return pl.pallas_call(
        kern, name=name,
        out_shape=[jax.ShapeDtypeStruct((T, H * LANES), F32)] * 2,
        grid=(H,),
        in_specs=[pl.BlockSpec(memory_space=pltpu.SMEM), q_spec, kv_spec, kv_spec],
        out_specs=[q_spec, q_spec],
        scratch_shapes=[pltpu.VMEM((slots, tq, ck), F32)],
        compiler_params=_params(("parallel",)),
    )(sinks, q, k, v)


def _attn_bwd(name, q, k, v, do, lse, delta, group, causal, scale):
    T = q.shape[0]
    H = q.shape[1] // LANES
    tq, ck, _ = _attn_plan(T, causal)
    nq = T // tq

    def kern(q_ref, k_ref, v_ref, do_ref, lse_ref, dl_ref, dq_ref, dk_ref, dv_ref):
        dk_ref[...] = jnp.zeros_like(dk_ref)
        dv_ref[...] = jnp.zeros_like(dv_ref)

        def q_tile(i, carry):
            rows = pl.ds(pl.multiple_of(i * tq, tq), tq)
            qq, dd = q_ref[rows, :], do_ref[rows, :]
            lse_c, dl_c = lse_ref[rows, :][:, 0:1], dl_ref[rows, :][:, 0:1]

            def chunk(c, dq):
                start, lo = _chunk(i, c, T, tq, ck, causal)
                keys = pl.ds(start, ck)
                kk, vv = k_ref[keys, :], v_ref[keys, :]
                s = _masked_scores(qq, kk, scale, i, start, lo, tq, ck, causal)
                p = jnp.exp(s - lse_c)
                dv_ref[keys, :] += lax.dot_general(p.astype(MXU_DTYPE), dd, _DIMS["tn"],
                                                   preferred_element_type=F32)
                dp = lax.dot_general(dd, vv, _DIMS["nt"], preferred_element_type=F32)
                ds = (p * (dp - dl_c) * scale).astype(MXU_DTYPE)
                dk_ref[keys, :] += lax.dot_general(ds, qq, _DIMS["tn"], preferred_element_type=F32)
                return dq + jnp.dot(ds, kk, preferred_element_type=F32)

            dq_ref[rows, :] = lax.fori_loop(0, _n_chunks(i, tq, ck, causal), chunk,
                                            jnp.zeros((tq, LANES), F32))
            return carry

        lax.fori_loop(0, nq, q_tile, 0)

    q_spec = pl.BlockSpec((T, LANES), lambda h: (0, h))
    kv_spec = pl.BlockSpec((T, LANES), lambda h: (0, h // group))
    return pl.pallas_call(
        kern, name=name,
        out_shape=[jax.ShapeDtypeStruct((T, H * LANES), F32)] * 3,
        grid=(H,),
        in_specs=[q_spec, kv_spec, kv_spec, q_spec, q_spec, q_spec],
        out_specs=[q_spec, q_spec, q_spec],
        compiler_params=_params(("parallel",)),
    )(q, k, v, do, lse, delta)


def _ew(name, fn, ins, out_dtypes):
    shape = ins[0].shape
    flat = [a.reshape(-1, shape[-1]) for a in ins]
    R, C = flat[0].shape
    tr = _tile(R, (512, 256, 128, 64, 32, 16, 8))
    n_in = len(ins)

    def kern(*refs):
        res = fn(*[r[...] for r in refs[:n_in]])
        for o_ref, val in zip(refs[n_in:], res):
            o_ref[...] = val.astype(o_ref.dtype)

    spec = pl.BlockSpec((tr, C), lambda i: (i, 0))
    outs = pl.pallas_call(
        kern, name=name,
        out_shape=[jax.ShapeDtypeStruct((R, C), dt) for dt in out_dtypes],
        grid=(R // tr,), in_specs=[spec] * n_in, out_specs=[spec] * len(out_dtypes),
        compiler_params=_params(("parallel",)),
    )(*flat)
    return [o.reshape(shape) for o in outs]


def _adamw(name, w, g, m, v):
    c1 = 1.0 - ADAM_B1 ** ADAM_STEP
    c2 = 1.0 - ADAM_B2 ** ADAM_STEP

    def fn(wv, gv, mv, vv):
        mn = ADAM_B1 * mv + (1.0 - ADAM_B1) * gv
        vn = ADAM_B2 * vv + (1.0 - ADAM_B2) * (gv * gv)
        delta = -ADAM_LR * ((mn / c1) / (jnp.sqrt(vn / c2) + ADAM_EPS) + ADAM_WD * wv)
        return delta, mn, vn

    return _ew(name, fn, [w, g, m, v], [F32, F32, F32])


_ANY = pl.BlockSpec(memory_space=pl.ANY)


def _where_am_i():
    x, y, c = lax.axis_index("x"), lax.axis_index("y"), lax.axis_index("c")
    chips = [(1 - x, y), (x, 1 - y), (1 - x, 1 - y)]
    return x, y, c, chips


def _gather_weights(shards, meta):
    arrs = list(shards) + [meta]
    n = len(arrs)
    per = [2] * len(shards) + [8]

    def body(*refs):
        ins, outs = refs[:n], refs[n:2 * n]
        send1, recv1, send2, recv2 = refs[2 * n:]
        x, y, c, chips = _where_am_i()
        me = 2 * x + y

        def half(ref, k, cc):
            return ref.at[pl.ds(per[k] * cc, per[k])]

        first = []
        for k in range(n):
            for j, (cx, cy) in enumerate(chips):
                first.append(pltpu.make_async_remote_copy(
                    src_ref=half(ins[k], k, c), dst_ref=half(outs[k].at[me], k, c),
                    send_sem=send1.at[k, j], recv_sem=recv1.at[k, j],
                    device_id=(cx, cy, c), device_id_type=MESH))
        for cp in first:
            cp.start()
        passed = []
        for k in range(n):
            for j, (cx, cy) in enumerate(chips):
                landed = half(outs[k].at[2 * cx + cy], k, c)
                pltpu.make_async_remote_copy(
                    src_ref=landed, dst_ref=landed, send_sem=send1.at[k, j], recv_sem=recv1.at[k, j],
                    device_id=(cx, cy, c), device_id_type=MESH).wait_recv()
                fwd = pltpu.make_async_remote_copy(
                    src_ref=landed, dst_ref=landed, send_sem=send2.at[k, j], recv_sem=recv2.at[k, j],
                    device_id=(x, y, 1 - c), device_id_type=MESH)
                fwd.start()
                passed.append(fwd)
        for k in range(n):
            for j, (cx, cy) in enumerate(chips):
                other = half(outs[k].at[2 * cx + cy], k, 1 - c)
                pltpu.make_async_remote_copy(
                    src_ref=other, dst_ref=other, send_sem=send2.at[k, j], recv_sem=recv2.at[k, j],
                    device_id=(x, y, 1 - c), device_id_type=MESH).wait_recv()
        for cp in first + passed:
            cp.wait_send()

    return pl.pallas_call(
        body, name="gather_weights",
        out_shape=[jax.ShapeDtypeStruct((4,) + a.shape, a.dtype) for a in arrs],
        in_specs=[_ANY] * n, out_specs=[_ANY] * n,
        scratch_shapes=[pltpu.SemaphoreType.DMA((n, 3))] * 4,
    )(*arrs)


def _row_chunks(r):
    n = 4 if r % 64 == 0 else 1
    return [(q * (r // n), r // n) for q in range(n)]


def _pair_exchange(grads):
    n = len(grads)

    def body(*refs):
        ins, got = refs[:n], refs[n:2 * n]
        send, recv = refs[2 * n:]
        x, y, c, _ = _where_am_i()
        sib = (x, y, 1 - c)
        for k in range(n):
            for ch in range(4):
                for l in range(2):
                    pltpu.make_async_remote_copy(
                        src_ref=ins[k].at[ch, 2 * (1 - c) + l], dst_ref=got[k].at[ch, l],
                        send_sem=send.at[k], recv_sem=recv.at[k], device_id=sib, device_id_type=MESH).start()
        for k in range(n):
            pltpu.make_async_remote_copy(
                src_ref=got[k], dst_ref=got[k], send_sem=send.at[k], recv_sem=recv.at[k],
                device_id=sib, device_id_type=MESH).wait()

    return pl.pallas_call(
        body, name="reduce_pair",
        out_shape=[jax.ShapeDtypeStruct((4, 2) + g.shape[2:], g.dtype) for g in grads],
        in_specs=[_ANY] * n, out_specs=[_ANY] * n,
        scratch_shapes=[pltpu.SemaphoreType.DMA((n,))] * 2,
    )(*grads)


def _pair_add(full, got, c):
    _, _, r, cols = full.shape
    tr = _tile(r, (512, 256, 352, 128))

    def kern(c_ref, a_ref, b_ref, o32_ref, o16_ref):
        tot = a_ref[...] + b_ref[...].astype(F32)
        o32_ref[...] = tot
        o16_ref[...] = tot.astype(o16_ref.dtype)

    blk = (None, None, tr, cols)
    mine = pl.BlockSpec(blk, lambda ch, l, i, cr: (ch, 2 * cr[0] + l, i, 0))
    same = pl.BlockSpec(blk, lambda ch, l, i, cr: (ch, l, i, 0))
    return pl.pallas_call(
        kern, name="pair_add",
        out_shape=[jax.ShapeDtypeStruct(got.shape, F32), jax.ShapeDtypeStruct(got.shape, got.dtype)],
        grid_spec=pltpu.PrefetchScalarGridSpec(
            num_scalar_prefetch=1, grid=(4, 2, r // tr), in_specs=[mine, same], out_specs=[same, same]),
        compiler_params=_params(("parallel", "parallel", "parallel")),
    )(c.reshape(1), full, got)


def _chip_scatter(parts):
    n = len(parts)

    def body(*refs):
        ins, outs = refs[:n], refs[n:2 * n]
        send, recv = refs[2 * n:]
        x, y, c, chips = _where_am_i()
        me = 2 * x + y
        for k in range(n):
            for j, (cx, cy) in enumerate(chips):
                for l in range(2):
                    pltpu.make_async_remote_copy(
                        src_ref=ins[k].at[2 * cx + cy, l], dst_ref=outs[k].at[me, l],
                        send_sem=send.at[k, j], recv_sem=recv.at[k, j],
                        device_id=(cx, cy, c), device_id_type=MESH).start()
        for k in range(n):
            for j, (cx, cy) in enumerate(chips):
                slot = outs[k].at[2 * cx + cy]
                pltpu.make_async_remote_copy(
                    src_ref=slot, dst_ref=slot, send_sem=send.at[k, j], recv_sem=recv.at[k, j],
                    device_id=(cx, cy, c), device_id_type=MESH).wait()

    return pl.pallas_call(
        body, name="reduce_chips",
        out_shape=[jax.ShapeDtypeStruct(p.shape, p.dtype) for p in parts],
        in_specs=[_ANY] * n, out_specs=[_ANY] * n,
        scratch_shapes=[pltpu.SemaphoreType.DMA((n, 3))] * 2,
    )(*parts)


def _chip_add(landed, mine, me, c):
    _, _, r, cols = landed.shape
    tr = _tile(r, (512, 256, 352, 128))

    def kern(me_ref, c_ref, land_ref, own_ref, o_ref):
        own = own_ref[...]
        tot = None
        for j in range(4):
            term = jnp.where(me_ref[0] == j, own, land_ref[j].astype(F32))
            tot = term if tot is None else tot + term
        o_ref[...] = tot

    return pl.pallas_call(
        kern, name="chip_add",
        out_shape=jax.ShapeDtypeStruct((4, r, cols), F32),
        grid_spec=pltpu.PrefetchScalarGridSpec(
            num_scalar_prefetch=2, grid=(2, r // tr),
            in_specs=[pl.BlockSpec((4, None, tr, cols), lambda l, i, mr, cr: (0, l, i, 0)),
                      pl.BlockSpec((None, None, tr, cols), lambda l, i, mr, cr: (mr[0], l, i, 0))],
            out_specs=pl.BlockSpec((None, tr, cols), lambda l, i, mr, cr: (2 * cr[0] + l, i, 0))),
        compiler_params=_params(("parallel", "parallel")),
    )(me.reshape(1), c.reshape(1), landed, mine)


def _pair_join(sums):
    n = len(sums)

    def body(*refs):
        bufs = refs[n:2 * n]
        send, recv = refs[2 * n:]
        x, y, c, _ = _where_am_i()
        sib = (x, y, 1 - c)
        for k in range(n):
            for l in range(2):
                for r0, rn in _row_chunks(bufs[k].shape[1]):
                    piece = bufs[k].at[2 * c + l, pl.ds(r0, rn)]
                    pltpu.make_async_remote_copy(
                        src_ref=piece, dst_ref=piece, send_sem=send.at[k], recv_sem=recv.at[k],
                        device_id=sib, device_id_type=MESH).start()
        for k in range(n):
            theirs = bufs[k].at[pl.ds(2 * (1 - c), 2)]
            pltpu.make_async_remote_copy(
                src_ref=theirs, dst_ref=theirs, send_sem=send.at[k], recv_sem=recv.at[k],
                device_id=sib, device_id_type=MESH).wait()

    return pl.pallas_call(
        body, name="reduce_join",
        out_shape=[jax.ShapeDtypeStruct(s.shape, s.dtype) for s in sums],
        in_specs=[_ANY] * n, out_specs=[_ANY] * n,
        input_output_aliases={k: k for k in range(n)},
        scratch_shapes=[pltpu.SemaphoreType.DMA((n,))] * 2,
    )(*sums)


def _allreduce_small(buf):
    R = buf.shape[0]

    def body(in_ref, out_ref, land, send, recv):
        x, y, c, _ = _where_am_i()
        me = 4 * x + 2 * y + c
        land[me] = in_ref[...]
        cps = []
        for k in range(1, 8):
            px, py, pc = x ^ (k >> 2), y ^ ((k >> 1) & 1), c ^ (k & 1)
            cps.append(pltpu.make_async_remote_copy(
                src_ref=in_ref, dst_ref=land.at[me], send_sem=send.at[k - 1], recv_sem=recv.at[k - 1],
                device_id=(px, py, pc), device_id_type=MESH))
        for cp in cps:
            cp.start()
        for k in range(1, 8):
            px, py, pc = x ^ (k >> 2), y ^ ((k >> 1) & 1), c ^ (k & 1)
            slot = land.at[4 * px + 2 * py + pc]
            pltpu.make_async_remote_copy(
                src_ref=slot, dst_ref=slot, send_sem=send.at[k - 1], recv_sem=recv.at[k - 1],
                device_id=(px, py, pc), device_id_type=MESH).wait_recv()
        for cp in cps:
            cp.wait_send()
        tot = land[0]
        for d in range(1, 8):
            tot = tot + land[d]
        out_ref[...] = tot

    vm = pl.BlockSpec(memory_space=pltpu.VMEM)
    return pl.pallas_call(
        body, name="allreduce_small",
        out_shape=jax.ShapeDtypeStruct(buf.shape, F32),
        in_specs=[vm], out_specs=vm,
        scratch_shapes=[pltpu.VMEM((8, R, LANES), F32), pltpu.SemaphoreType.DMA((7,)),
                        pltpu.SemaphoreType.DMA((7,))],
    )(buf)


def _rope_tables(T):
    pos = (jnp.arange(T) - FRONT).astype(F32)
    lane = jnp.arange(LANES)
    inv_a = ROPE_THETA ** (-(2 * ((lane % 64) % 32)).astype(F32) / SWA_HEAD_DIM)
    ang_a = pos[:, None] * inv_a[None, :]
    cosa, sina = jnp.cos(ang_a), jnp.sin(ang_a)
    inv_m = ROPE_THETA ** (-(2 * ((lane - KR_LANE) % 16)).astype(F32) / MLA_ROPE_DIM)
    ang_m = pos[:, None] * inv_m[None, :]
    on = ((lane >= KR_LANE) & (lane < KR_LANE + MLA_ROPE_DIM))[None, :]
    cosm = jnp.where(on, jnp.cos(ang_m), 1.0)
    sinm = jnp.where(on, jnp.sin(ang_m), 0.0)
    return cosa, sina, cosm, sinm


def _cols_from_chips(g):
    return jnp.concatenate([g[j] for j in range(4)], axis=-1)


def _rows_from_chips(g):
    return jnp.concatenate([g[j] for j in range(4)], axis=-2)


def _cols_to_chips(w):
    L, r, c4 = w.shape
    return jnp.moveaxis(w.reshape(L, r, 4, c4 // 4), 2, 0)


def _rows_to_chips(w):
    L, r4, c = w.shape
    return jnp.moveaxis(w.reshape(L, 4, r4 // 4, c), 1, 0)


def _local_step(x2, target, meta_full, natural, p):
    T = BLOCK + x2.shape[0]
    L = DEPTH
    W_in, W_qup, W_kvup, W_o, W_gate, W_up, W_down = natural
    zpad = lambda n: jnp.zeros((L, D_MODEL, n), W_in.dtype)
    W_in = jnp.concatenate([W_in[..., :C_KR], zpad(KR_LANE), W_in[..., C_KR:IN_W],
                            zpad(LANES - KR_LANE - MLA_ROPE_DIM)], axis=-1)
    W_qup = W_qup.reshape(L, MLA_Q_RANK, MLA_HEADS, MLA_QK_DIM)
    W_qup = jnp.pad(W_qup, ((0, 0), (0, 0), (0, 0), (0, LANES - MLA_QK_DIM))).reshape(L, MLA_Q_RANK, SLOT_W)
    attn_norm, q_norm, kv_norm, sinks = p["attn_norm"], p["q_norm"], p["kv_norm"], p["sinks"]
    out_norm_swa, out_norm_mla, ffn_norm, final_norm = (
        p["out_norm_swa"], p["out_norm_mla"], p["ffn_norm"], p["final_norm"])

    cosa, sina, cosm, sinm = _rope_tables(T)
    no_sink = jnp.full((MLA_HEADS,), NEG, F32)
    scale_a, scale_b = SWA_HEAD_DIM ** -0.5, MLA_QK_DIM ** -0.5
    row = lambda v: v.reshape(1, -1)

    h = jnp.concatenate([jnp.zeros((FRONT, D_MODEL), F32), meta_full, x2], axis=0)
    saved = []
    for l in range(L):
        u = _rmsnorm_fwd("attn_norm", h, row(attn_norm[l]))
        proj = _mm("in_proj", u, W_in[l], "nn")
        qa, ka, va, qn, cn, kr = _prep1(proj, row(q_norm[l]), row(kv_norm[l]), cosa, sina, cosm, sinm)
        qb = _mm("q_up", qn, W_qup[l], "nn")
        kvb = _mm("kv_up", cn, W_kvup[l], "nn")
        qs, ks, vs = _prep2(qb, kvb, kr, cosm, sinm)
        oa, lse_a = _attn_fwd("swa_fwd", qa, ka, va, sinks[l], 4, False, scale_a)
        ob, lse_b = _attn_fwd("mla_fwd", qs, ks, vs, no_sink, 1, True, scale_b)
        mix = _merge_fwd(oa, ob, row(out_norm_swa[l]), row(out_norm_mla[l]))
        h1 = _mm("o_proj", mix, W_o[l], "nn", res=h)
        u2 = _rmsnorm_fwd("ffn_norm", h1, row(ffn_norm[l]))
        a = _mm("gate_proj", u2, W_gate[l], "nn")
        b = _mm("up_proj", u2, W_up[l], "nn")
        hm = _swiglu_fwd(a, b)
        h2 = _mm("down_proj", hm, W_down[l], "nn", res=h1)
        saved.append((h, u, proj, qa, ka, va, qn, cn, qs, ks, vs, oa, lse_a, ob, lse_b, mix, h1, u2, a, b, hm))
        h = h2

    dh, d_final, loss_row = _loss_head(h, target, row(final_norm))

    gw = {k: [None] * L for k in ("in", "qup", "kvup", "o", "gate", "up", "down")}
    gs = {k: [None] * L for k in ("attn", "qn", "kvn", "sink", "ga", "gb", "ffn")}
    for l in reversed(range(L)):
        (h0, u, proj, qa, ka, va, qn, cn, qs, ks, vs, oa, lse_a, ob, lse_b, mix, h1, u2, a, b, hm) = saved[l]
        gw["down"][l] = _mm("down_dw", hm, dh, "tn")
        dhm = _mm("down_dx", dh, W_down[l], "nt")
        da, db = _swiglu_bwd(a, b, dhm)
        gw["gate"][l] = _mm("gate_dw", u2, da, "tn")
        gw["up"][l] = _mm("up_dw", u2, db, "tn")
        du2 = _mm("gate_dx", da, W_gate[l], "nt")
        du2 = _mm("up_dx", db, W_up[l], "nt", res=du2)
        dh1, gs["ffn"][l] = _rmsnorm_bwd("ffn_norm_bwd", h1, row(ffn_norm[l]), du2, dh)
        gw["o"][l] = _mm("o_dw", mix, dh1, "tn")
        dmix = _mm("o_dx", dh1, W_o[l], "nt")
        sink_slots = jnp.repeat(sinks[l], LANES).reshape(1, SLOT_W)
        doa, dla, dob, dlb, gs["ga"][l], gs["gb"][l], dsink = _merge_bwd(
            dmix, oa, ob, lse_a, row(out_norm_swa[l]), row(out_norm_mla[l]), sink_slots)
        gs["sink"][l] = dsink.reshape(SWA_HEADS, LANES)[:, 0]
        dqs, dks, dvs = _attn_bwd("mla_bwd", qs, ks, vs, dob, lse_b, dlb, 1, True, scale_b)
        dqa, dka, dva = _attn_bwd("swa_bwd", qa, ka, va, doa, lse_a, dla, 4, False, scale_a)
        dqb, dkvb, dkr = _prep2_bwd(dqs, dks, dvs, cosm, sinm)
        gw["qup"][l] = _mm("q_up_dw", qn, dqb, "tn")
        gw["kvup"][l] = _mm("kv_up_dw", cn, dkvb, "tn")
        dqn = _mm("q_up_dx", dqb, W_qup[l], "nt")
        dcn = _mm("kv_up_dx", dkvb, W_kvup[l], "nt")
        dproj, gs["qn"][l], gs["kvn"][l] = _prep1_bwd(
            proj, dqa, dka, dva, dqn, dcn, dkr, cosa, sina, row(q_norm[l]), row(kv_norm[l]))
        gw["in"][l] = _mm("in_dw", u, dproj, "tn")
        du = _mm("in_dx", dproj, W_in[l], "nt")
        dh, gs["attn"][l] = _rmsnorm_bwd("attn_norm_bwd", h0, row(attn_norm[l]), du, dh1)

    st = lambda k: jnp.stack(gw[k])
    d_in = st("in")
    d_in = jnp.concatenate([d_in[..., :C_KR], d_in[..., C_KR + KR_LANE:C_KR + KR_LANE + MLA_ROPE_DIM]], axis=-1)
    d_qup = st("qup").reshape(L, MLA_Q_RANK, MLA_HEADS, LANES)[..., :MLA_QK_DIM].reshape(L, MLA_Q_RANK, -1)
    d_nat = [d_in, d_qup, st("kvup"), st("o"), st("gate"), st("up"), st("down")]
    return loss_row, dh, d_nat, gs, d_final


def kernel(x, meta_tokens, attn_norm, w_in, q_norm, w_q_up, kv_norm, w_kv_up, sinks, out_norm_swa, out_norm_mla, w_o, ffn_norm, w_gate, w_up, w_down, final_norm, loss_target, m_meta_tokens, m_attn_norm, m_w_in, m_q_norm, m_w_q_up, m_kv_norm, m_w_kv_up, m_sinks, m_out_norm_swa, m_out_norm_mla, m_w_o, m_ffn_norm, m_w_gate, m_w_up, m_w_down, m_final_norm, v_meta_tokens, v_attn_norm, v_w_in, v_q_norm, v_w_q_up, v_kv_norm, v_w_kv_up, v_sinks, v_out_norm_swa, v_out_norm_mla, v_w_o, v_ffn_norm, v_w_gate, v_w_up, v_w_down, v_final_norm):
    assert x.shape[0] == 1 and x.shape[1] % BLOCK == 0
    big = [w_in, w_q_up, w_kv_up, w_o, w_gate, w_up, w_down]

    c_idx = lax.axis_index("c").astype(jnp.int32)
    chip = (2 * lax.axis_index("x") + lax.axis_index("y")).astype(jnp.int32)
    mine = [w.astype(BF16) for w in big] + [meta_tokens]
    gathered = _gather_weights(mine[:-1], mine[-1])
    parts = [[jnp.where(chip == j, own, g[j]) for j in range(4)] for own, g in zip(mine, gathered)]
    join = [_cols_from_chips, _cols_from_chips, _cols_from_chips, _rows_from_chips, _cols_from_chips,
            _cols_from_chips, _rows_from_chips]
    natural = [f(p) for f, p in zip(join, parts[:-1])]
    meta_full = jnp.concatenate(parts[-1], axis=-1)
    small_p = dict(attn_norm=attn_norm, q_norm=q_norm, kv_norm=kv_norm, sinks=sinks, out_norm_swa=out_norm_swa,
                   out_norm_mla=out_norm_mla, ffn_norm=ffn_norm, final_norm=final_norm)
    loss_row, dh, d_nat, gs, d_final = _local_step(x[0], loss_target[0], meta_full, natural, small_p)
    grad_x = dh[BLOCK:][None]

    split = [_cols_to_chips, _cols_to_chips, _cols_to_chips, _rows_to_chips, _cols_to_chips, _cols_to_chips,
             _rows_to_chips]
    full = [f(d) for f, d in zip(split, d_nat)]

    got = _pair_exchange([f.astype(BF16) for f in full])
    sums = [_pair_add(f, g, c_idx) for f, g in zip(full, got)]
    landed = _chip_scatter([s16 for _, s16 in sums])
    g_big = _pair_join([_chip_add(t, s32, chip, c_idx) for t, (s32, _) in zip(landed, sums)])

    small = [jnp.stack(gs["attn"]).reshape(-1), jnp.stack(gs["qn"]).reshape(-1), jnp.stack(gs["kvn"]).reshape(-1),
             jnp.stack(gs["sink"]).reshape(-1), jnp.stack(gs["ga"]).reshape(-1), jnp.stack(gs["gb"]).reshape(-1),
             jnp.stack(gs["ffn"]).reshape(-1), d_final.reshape(-1)]
    sizes = [s.shape[0] for s in small]
    flat = jnp.concatenate(small + [dh[FRONT:BLOCK].reshape(-1), loss_row[0, :1]])
    n_flat = flat.shape[0]
    rows_needed = -(-n_flat // (8 * LANES)) * 8
    flat = jnp.pad(flat, (0, rows_needed * LANES - n_flat)).reshape(rows_needed, LANES)
    tot = _allreduce_small(flat).reshape(-1)
    n_small = sum(sizes)
    loss = tot[n_small + N_META * D_MODEL]
    g_meta_full = tot[n_small:n_small + N_META * D_MODEL].reshape(N_META, D_MODEL)
    g_meta_mine = lax.dynamic_slice_in_dim(g_meta_full, chip * (D_MODEL // 4), D_MODEL // 4, axis=1)

    small_w = [attn_norm, q_norm, kv_norm, sinks, out_norm_swa, out_norm_mla, ffn_norm, final_norm]
    small_m = [m_attn_norm, m_q_norm, m_kv_norm, m_sinks, m_out_norm_swa, m_out_norm_mla, m_ffn_norm, m_final_norm]
    small_v = [v_attn_norm, v_q_norm, v_kv_norm, v_sinks, v_out_norm_swa, v_out_norm_mla, v_ffn_norm, v_final_norm]
    n_rows = -(-n_small // (8 * LANES)) * 8

    def pack(arrs):
        f = jnp.concatenate([a.reshape(-1) for a in arrs])
        return jnp.pad(f, (0, n_rows * LANES - n_small), constant_values=1.0).reshape(n_rows, LANES)

    g_small_pack = jnp.pad(tot[:n_small], (0, n_rows * LANES - n_small)).reshape(n_rows, LANES)
    upd_small = _adamw("adam_small", pack(small_w), g_small_pack, pack(small_m), pack(small_v))

    def unpack(p):
        f = p.reshape(-1)
        out, off = [], 0
        for a, n in zip(small_w, sizes):
            out.append(f[off:off + n].reshape(a.shape))
            off += n
        return out

    g_small = unpack(g_small_pack)
    d_small, m_small, v_small = [unpack(p) for p in upd_small]
    d_meta, nm_meta, nv_meta = _adamw("adam_meta", meta_tokens, g_meta_mine, m_meta_tokens, v_meta_tokens)

    big_m = [m_w_in, m_w_q_up, m_w_kv_up, m_w_o, m_w_gate, m_w_up, m_w_down]
    big_v = [v_w_in, v_w_q_up, v_w_kv_up, v_w_o, v_w_gate, v_w_up, v_w_down]
    upd_big = [_adamw("adam_big", w, g, m, v) for w, g, m, v in zip(big, g_big, big_m, big_v)]

    names = ["meta_tokens", "attn_norm", "w_in", "q_norm", "w_q_up", "kv_norm", "w_kv_up", "sinks",
             "out_norm_swa", "out_norm_mla", "w_o", "ffn_norm", "w_gate", "w_up", "w_down", "final_norm"]
    small_idx = {"attn_norm": 0, "q_norm": 1, "kv_norm": 2, "sinks": 3, "out_norm_swa": 4,
                 "out_norm_mla": 5, "ffn_norm": 6, "final_norm": 7}
    big_idx = {"w_in": 0, "w_q_up": 1, "w_kv_up": 2, "w_o": 3, "w_gate": 4, "w_up": 5, "w_down": 6}
    grads, deltas, new_m, new_v = [], [], [], []
    for nme in names:
        if nme == "meta_tokens":
            quad = (g_meta_mine, d_meta, nm_meta, nv_meta)
        elif nme in small_idx:
            i = small_idx[nme]
            quad = (g_small[i], d_small[i], m_small[i], v_small[i])
        else:
            i = big_idx[nme]
            quad = (g_big[i], *upd_big[i])
        grads.append(quad[0]); deltas.append(quad[1]); new_m.append(quad[2]); new_v.append(quad[3])
    return (loss, grad_x, *grads, *deltas, *new_m, *new_v)
```

```python
import jax
import jax.numpy as jnp
from jax import lax
from jax.experimental import pallas as pl
from jax.experimental.pallas import tpu as pltpu

F32 = jnp.float32
BF16 = jnp.bfloat16
MXU_DTYPE = BF16

D_MODEL = 1024
DEPTH = 4
N_META = 16
BLOCK = 128
WINDOW = 128
ROPE_THETA = 10000.0
EPS = 1e-6
NEG = -1e30
SWA_HEADS = 8
SWA_KV_HEADS = 2
SWA_HEAD_DIM = 64
MLA_HEADS = 8
MLA_Q_RANK = 256
MLA_KV_RANK = 128
MLA_NOPE_DIM = 64
MLA_ROPE_DIM = 32
MLA_V_DIM = 64
MLA_QK_DIM = MLA_NOPE_DIM + MLA_ROPE_DIM
D_FF = 2816
FRONT = (-N_META) % BLOCK
LANES = 128
SLOT_W = 8 * LANES
C_QA, C_KA, C_VA, C_QL, C_KL, C_KR, IN_WP = 0, 512, 640, 768, 1024, 1152, 1280
KR_LANE = 64
IN_W = 1184

ADAM_LR, ADAM_B1, ADAM_B2, ADAM_EPS, ADAM_WD, ADAM_STEP = 0.001, 0.9, 0.999, 1e-08, 0.01, 10

VMEM_LIMIT = 48 * 1024 * 1024
MESH = pl.DeviceIdType.MESH


def _tile(n, prefs):
    for t in prefs:
        if n % t == 0:
            return t
    return n


def _params(sem):
    return pltpu.CompilerParams(dimension_semantics=sem, vmem_limit_bytes=VMEM_LIMIT)


_DIMS = {"nn": (((1,), (0,)), ((), ())), "nt": (((1,), (1,)), ((), ())), "tn": (((0,), (0,)), ((), ()))}


def _mm(name, a, b, mode, out_dtype=F32, res=None):
    if mode == "nn":
        (M, K), (_, N) = a.shape, b.shape
    elif mode == "nt":
        (M, K), (N, _) = a.shape, b.shape
    else:
        (K, M), (_, N) = a.shape, b.shape
    lane_tiles = (1408, 1024, 640, 768, 512, 384, 256, 128)
    row_tiles = (528, 512, 384, 256, 128)
    bm = _tile(M, lane_tiles if mode == "tn" else row_tiles)
    bn = _tile(N, lane_tiles)
    bk = _tile(K, row_tiles if mode == "tn" else lane_tiles)
    nk = K // bk
    if mode == "tn":
        a_spec = pl.BlockSpec((bk, bm), lambda i, j, k: (k, i))
    else:
        a_spec = pl.BlockSpec((bm, bk), lambda i, j, k: (i, k))
    if mode == "nt":
        b_spec = pl.BlockSpec((bn, bk), lambda i, j, k: (j, k))
    else:
        b_spec = pl.BlockSpec((bk, bn), lambda i, j, k: (k, j))
    o_spec = pl.BlockSpec((bm, bn), lambda i, j, k: (i, j))
    in_specs = [a_spec, b_spec]
    args = [a, b]
    if res is not None:
        in_specs.append(o_spec)
        args.append(res)
    dims = _DIMS[mode]

    def kern(a_ref, b_ref, *rest):
        if res is not None:
            r_ref, o_ref = rest[0], rest[1]
            scr = rest[2:]
        else:
            r_ref, o_ref = None, rest[0]
            scr = rest[1:]
        p = lax.dot_general(a_ref[...].astype(MXU_DTYPE), b_ref[...].astype(MXU_DTYPE), dims,
                            preferred_element_type=F32)

        def finish(val):
            if r_ref is not None:
                val = val + r_ref[...]
            o_ref[...] = val.astype(o_ref.dtype)

        if nk == 1:
            finish(p)
        else:
            acc = scr[0]
            k = pl.program_id(2)

            @pl.when(k == 0)
            def _():
                acc[...] = p

            @pl.when(k > 0)
            def _():
                acc[...] += p

            @pl.when(k == nk - 1)
            def _():
                finish(acc[...])

    return pl.pallas_call(
        kern, name=name,
        out_shape=jax.ShapeDtypeStruct((M, N), out_dtype),
        grid=(M // bm, N // bn, nk),
        in_specs=in_specs, out_specs=o_spec,
        scratch_shapes=[pltpu.VMEM((bm, bn), F32)] if nk > 1 else [],
        compiler_params=_params(("parallel", "parallel", "arbitrary")),
    )(*args)


def _rowmap(name, body, rows, vecs, outs, accs=(), tr_prefs=(384, 256, 128)):
    R = rows[0].shape[0]
    tr = _tile(R, tr_prefs)
    n_r, n_v, n_o, n_a = len(rows), len(vecs), len(outs), len(accs)

    def kern(*refs):
        ins = [r[...] for r in refs[:n_r + n_v]]
        o_refs = refs[n_r + n_v:n_r + n_v + n_o]
        a_refs = refs[n_r + n_v + n_o:]
        res = body(*ins)
        for o_ref, val in zip(o_refs, res[:n_o]):
            o_ref[...] = val.astype(o_ref.dtype)
        if n_a:
            first = pl.program_id(0) == 0

            @pl.when(first)
            def _():
                for a_ref, val in zip(a_refs, res[n_o:]):
                    a_ref[...] = val

            @pl.when(jnp.logical_not(first))
            def _():
                for a_ref, val in zip(a_refs, res[n_o:]):
                    a_ref[...] += val

    in_specs = [pl.BlockSpec((tr, r.shape[1]), lambda i: (i, 0)) for r in rows]
    in_specs += [pl.BlockSpec((1, v.shape[1]), lambda i: (0, 0)) for v in vecs]
    out_specs = [pl.BlockSpec((tr, c), lambda i: (i, 0)) for c, _ in outs]
    out_specs += [pl.BlockSpec((1, c), lambda i: (0, 0)) for c in accs]
    out_shape = [jax.ShapeDtypeStruct((R, c), dt) for c, dt in outs]
    out_shape += [jax.ShapeDtypeStruct((1, c), F32) for c in accs]
    return pl.pallas_call(
        kern, name=name, out_shape=out_shape, grid=(R // tr,),
        in_specs=in_specs, out_specs=out_specs,
        compiler_params=_params(("arbitrary",) if n_a else ("parallel",)),
    )(*rows, *vecs)


def _lane(shape):
    return lax.broadcasted_iota(jnp.int32, shape, 1)


def _rot_swa(x):
    lane = _lane(x.shape)
    return jnp.where((lane & 63) < 32, -pltpu.roll(x, 96, 1), pltpu.roll(x, 32, 1))


def _rot_mla(x):
    lane = _lane(x.shape)
    lo = jnp.where(lane >= KR_LANE, -pltpu.roll(x, 112, 1), 0.0)
    hi = jnp.where(lane < KR_LANE + MLA_ROPE_DIM, pltpu.roll(x, 16, 1), 0.0)
    return jnp.where(lane < KR_LANE + 16, lo, hi)


def _rope(x, cos, sin, rot):
    return x * cos + rot(x) * sin


def _rope_t(g, cos, sin, rot):
    return g * cos - rot(g * sin)


def _low(x):
    return jnp.where(_lane(x.shape) < 64, x, 0.0)


def _blk(x, j):
    return x[:, j * LANES:(j + 1) * LANES]


def _rms_r(x, width):
    return lax.rsqrt(jnp.sum(x * x, axis=-1, keepdims=True) * (1.0 / width) + EPS)


def _rms_bwd(x, g, dy, width):
    r = _rms_r(x, width)
    gdy = dy * g
    dot = jnp.sum(gdy * x, axis=-1, keepdims=True)
    dx = r * gdy - x * (r * r * r * (1.0 / width) * dot)
    return dx, dy * x * r


def _colsum(x):
    return jnp.sum(x, axis=0, keepdims=True)


def _rmsnorm_fwd(name, x, g):
    def body(xv, gv):
        return (xv * _rms_r(xv, D_MODEL) * gv,)
    return _rowmap(name, body, [x], [g], [(D_MODEL, BF16)])[0]


def _rmsnorm_bwd(name, x, g, dy, dres):
    def body(xv, dyv, dr, gv):
        dx, dg = _rms_bwd(xv, gv, dyv, D_MODEL)
        return dx + dr, _colsum(dg)
    return _rowmap(name, body, [x, dy, dres], [g], [(D_MODEL, F32)], [D_MODEL])


def _prep1(proj, qn_g, kv_g, cosa, sina, cosm, sinm):
    def body(p, ca, sa, cm, sm, gq, gk):
        qa = []
        for j in range(4):
            xr = _rope(_blk(p, j), ca, sa, _rot_swa)
            qa += [_low(xr), _low(pltpu.roll(xr, 64, 1))]
        kr_ = _rope(_blk(p, C_KA // LANES), ca, sa, _rot_swa)
        ka = [_low(kr_), _low(pltpu.roll(kr_, 64, 1))]
        vv = _blk(p, C_VA // LANES)
        va = [_low(vv), _low(pltpu.roll(vv, 64, 1))]
        ql = p[:, C_QL:C_QL + MLA_Q_RANK]
        qn = ql * _rms_r(ql, MLA_Q_RANK) * gq
        kl = p[:, C_KL:C_KL + MLA_KV_RANK]
        cn = kl * _rms_r(kl, MLA_KV_RANK) * gk
        kr = _rope(_blk(p, C_KR // LANES), cm, sm, _rot_mla)
        return (jnp.concatenate(qa, 1), jnp.concatenate(ka, 1), jnp.concatenate(va, 1), qn, cn, kr)
    return _rowmap("prep1", body, [proj, cosa, sina, cosm, sinm], [qn_g, kv_g],
                   [(SLOT_W, BF16), (2 * LANES, BF16), (2 * LANES, BF16),
                    (MLA_Q_RANK, BF16), (MLA_KV_RANK, BF16), (LANES, F32)])


def _prep2(qb, kvb, kr, cosm, sinm):
    def body(q, kv, krv, cm, sm):
        qs, ks, vs = [], [], []
        for h in range(MLA_HEADS):
            qs.append(_rope(_blk(q, h), cm, sm, _rot_mla))
            kvh = _blk(kv, h)
            ks.append(_low(kvh) + krv)
            vs.append(_low(pltpu.roll(kvh, 64, 1)))
        return jnp.concatenate(qs, 1), jnp.concatenate(ks, 1), jnp.concatenate(vs, 1)
    return _rowmap("prep2", body, [qb, kvb, kr, cosm, sinm], [],
                   [(SLOT_W, BF16), (SLOT_W, BF16), (SLOT_W, BF16)])


def _compact(slots):
    return jnp.concatenate(
        [_blk(slots, 2 * j) + pltpu.roll(_blk(slots, 2 * j + 1), 64, 1) for j in range(4)], 1)


def _expand(nat):
    out = []
    for j in range(4):
        b = _blk(nat, j)
        out += [_low(b), _low(pltpu.roll(b, 64, 1))]
    return jnp.concatenate(out, 1)


def _merge_fwd(oa, ob, ga, gb):
    def body(a, b, gav, gbv):
        xa, xb = _compact(a), _compact(b)
        return (jnp.concatenate([xa * _rms_r(xa, 512) * gav, xb * _rms_r(xb, 512) * gbv], 1),)
    return _rowmap("merge_fwd", body, [oa, ob], [ga, gb], [(D_MODEL, BF16)])[0]


def _merge_bwd(dmix, oa, ob, lse_a, ga, gb, sink_slots):
    def body(dm, a, b, lse, gav, gbv, sk):
        outs = []
        accs = []
        for o, g, lo in ((a, gav, 0), (b, gbv, 512)):
            x = _compact(o)
            dx, dg = _rms_bwd(x, g, dm[:, lo:lo + 512], 512)
            do = _expand(dx)
            delta = jnp.concatenate(
                [jnp.broadcast_to(jnp.sum(_blk(do, h) * _blk(o, h), axis=-1, keepdims=True),
                                  (do.shape[0], LANES)) for h in range(8)], 1)
            outs += [do, delta]
            accs.append(_colsum(dg))
        dsink = _colsum(-jnp.exp(sk - lse) * outs[1])
        return (*outs, *accs, dsink)
    return _rowmap("merge_bwd", body, [dmix, oa, ob, lse_a], [ga, gb, sink_slots],
                   [(SLOT_W, BF16), (SLOT_W, F32), (SLOT_W, BF16), (SLOT_W, F32)],
                   [512, 512, SLOT_W])


def _prep2_bwd(dq, dk, dv, cosm, sinm):
    def body(dqv, dkv, dvv, cm, sm):
        dqb, dkvb = [], []
        krsum = jnp.zeros((dqv.shape[0], LANES), F32)
        for h in range(MLA_HEADS):
            dqb.append(_rope_t(_blk(dqv, h), cm, sm, _rot_mla))
            dkh = _blk(dkv, h)
            dkvb.append(_low(dkh) + pltpu.roll(_blk(dvv, h), 64, 1))
            krsum = krsum + dkh
        lane = _lane(krsum.shape)
        dkr = jnp.where((lane >= KR_LANE) & (lane < KR_LANE + MLA_ROPE_DIM),
                        _rope_t(krsum, cm, sm, _rot_mla), 0.0)
        return jnp.concatenate(dqb, 1), jnp.concatenate(dkvb, 1), dkr
    return _rowmap("prep2_bwd", body, [dq, dk, dv, cosm, sinm], [],
                   [(SLOT_W, BF16), (SLOT_W, BF16), (LANES, F32)])


def _prep1_bwd(proj, dqa, dka, dva, dqn, dcn, dkr, cosa, sina, qn_g, kv_g):
    def body(p, dq, dk, dv, dqnv, dcnv, dkrv, ca, sa, gq, gk):
        cols = []
        for j in range(4):
            nat = _blk(dq, 2 * j) + pltpu.roll(_blk(dq, 2 * j + 1), 64, 1)
            cols.append(_rope_t(nat, ca, sa, _rot_swa))
        grp = lambda d, g: sum(_blk(d, 4 * g + i) for i in range(4))
        cols.append(_rope_t(grp(dk, 0) + pltpu.roll(grp(dk, 1), 64, 1), ca, sa, _rot_swa))
        cols.append(grp(dv, 0) + pltpu.roll(grp(dv, 1), 64, 1))
        dql, dgq = _rms_bwd(p[:, C_QL:C_QL + MLA_Q_RANK], gq, dqnv, MLA_Q_RANK)
        dkl, dgk = _rms_bwd(p[:, C_KL:C_KL + MLA_KV_RANK], gk, dcnv, MLA_KV_RANK)
        cols += [dql, dkl, dkrv]
        return jnp.concatenate(cols, 1), _colsum(dgq), _colsum(dgk)
    return _rowmap("prep1_bwd", body, [proj, dqa, dka, dva, dqn, dcn, dkr, cosa, sina], [qn_g, kv_g],
                   [(IN_WP, BF16)], [MLA_Q_RANK, MLA_KV_RANK], tr_prefs=(192, 128))


def _sigmoid(x):
    return 1.0 / (1.0 + jnp.exp(-x))


def _swiglu_fwd(a, b):
    def body(av, bv):
        return (av * _sigmoid(av) * bv,)
    return _rowmap("swiglu_fwd", body, [a, b], [], [(D_FF, BF16)])[0]


def _swiglu_bwd(a, b, dh):
    def body(av, bv, d):
        s = _sigmoid(av)
        return d * bv * (s * (1.0 + av * (1.0 - s))), d * (av * s)
    return _rowmap("swiglu_bwd", body, [a, b, dh], [], [(D_FF, BF16), (D_FF, BF16)])


def _loss_head(h, target, g):
    T = h.shape[0]
    nb = T // BLOCK

    def kern(h_ref, t_ref, g_ref, dh_ref, dg_ref, loss_ref, acc):
        i = pl.program_id(0)

        @pl.when(i == 0)
        def _():
            dh_ref[...] = jnp.zeros_like(dh_ref)
            dg_ref[...] = jnp.zeros_like(dg_ref)
            acc[...] = jnp.zeros_like(acc)

        @pl.when(i > 0)
        def _():
            x = h_ref[...]
            gv = g_ref[...]
            e = x * _rms_r(x, D_MODEL) * gv - t_ref[...]
            acc[...] += _colsum(e * e)
            dx, dg = _rms_bwd(x, gv, e * (1.0 / D_MODEL), D_MODEL)
            dh_ref[...] = dx
            dg_ref[...] += _colsum(dg)

        @pl.when(i == nb - 1)
        def _():
            tot = jnp.sum(acc[...], axis=-1, keepdims=True) * (0.5 / D_MODEL)
            loss_ref[...] = jnp.broadcast_to(tot, loss_ref.shape)

    return pl.pallas_call(
        kern, name="loss_head",
        out_shape=[jax.ShapeDtypeStruct((T, D_MODEL), F32), jax.ShapeDtypeStruct((1, D_MODEL), F32),
                   jax.ShapeDtypeStruct((1, LANES), F32)],
        grid=(nb,),
        in_specs=[pl.BlockSpec((BLOCK, D_MODEL), lambda i: (i, 0)),
                  pl.BlockSpec((BLOCK, D_MODEL), lambda i: (jnp.maximum(i - 1, 0), 0)),
                  pl.BlockSpec((1, D_MODEL), lambda i: (0, 0))],
        out_specs=[pl.BlockSpec((BLOCK, D_MODEL), lambda i: (i, 0)),
                   pl.BlockSpec((1, D_MODEL), lambda i: (0, 0)),
                   pl.BlockSpec((1, LANES), lambda i: (0, 0))],
        scratch_shapes=[pltpu.VMEM((1, D_MODEL), F32)],
        compiler_params=_params(("arbitrary",)),
    )(h, target, g)


LOG2E = 1.4426950408889634


def _attn_plan(T, causal):
    tq = _tile(T, (384, 256, 128))
    ck = min(2 * tq, T) if causal else min(tq + WINDOW, T)
    return tq, ck, (-(-T // ck) if causal else 1)


def _chunk(i, c, T, tq, ck, causal):
    if causal:
        return pl.multiple_of(jnp.minimum(c * ck, T - ck), LANES), c * ck
    return pl.multiple_of(jnp.clip(i * tq - WINDOW, 0, T - ck), LANES), 0


def _n_chunks(i, tq, ck, causal):
    return ((i + 1) * tq + ck - 1) // ck if causal else 1


def _mask(s, i, start, first, tq, ck, causal):
    qpos = i * tq + lax.broadcasted_iota(jnp.int32, (tq, 1), 0)
    kpos = start + lax.broadcasted_iota(jnp.int32, (tq, ck), 1)
    low = jnp.maximum(jnp.where(qpos < FRONT, 0, FRONT), first)
    if not causal:
        low = jnp.maximum(low, qpos - (WINDOW - 1))
    return jnp.where(kpos >= low, jnp.where(kpos <= qpos, s, NEG), NEG)


def _chunk_loop(n, body, init, causal):
    carry = body(0, init, True)
    if not causal:
        return carry
    carry = lax.fori_loop(1, n - 1, lambda c, cr: body(c, cr, False), carry)
    return lax.cond(n > 1, lambda cr: body(n - 1, cr, True), lambda cr: cr, carry)


def _attn_fwd(name, q, k, v, sinks, group, causal, scale):
    T = q.shape[0]
    H = q.shape[1] // LANES
    tq, ck, slots = _attn_plan(T, causal)
    nq = T // tq
    c2 = scale * LOG2E

    def kern(sink_ref, q_ref, k_ref, v_ref, o_ref, lse_ref, s_scr):
        sink = sink_ref[pl.program_id(0)]

        def q_tile(i, carry):
            rows = pl.ds(pl.multiple_of(i * tq, tq), tq)
            qq = q_ref[rows, :]
            n = _n_chunks(i, tq, ck, causal)

            def score(c, m, masked):
                start, first = _chunk(i, c, T, tq, ck, causal)
                s = lax.dot_general(qq, k_ref[pl.ds(start, ck), :], _DIMS["nt"], preferred_element_type=F32)
                if masked:
                    s = _mask(s, i, start, first, tq, ck, causal)
                s_scr[c] = s
                return jnp.maximum(m, jnp.max(s, axis=-1, keepdims=True))

            m = _chunk_loop(n, score, jnp.full((tq, 1), sink * (1.0 / scale), F32), causal)
            m2 = m * c2

            def weigh(c, carry, masked):
                l, acc = carry
                start, _ = _chunk(i, c, T, tq, ck, causal)
                p = jnp.exp2(s_scr[c] * c2 - m2)
                acc = acc + jnp.dot(p.astype(MXU_DTYPE), v_ref[pl.ds(start, ck), :],
                                    preferred_element_type=F32)
                return l + jnp.sum(p, axis=-1, keepdims=True), acc

            init = (jnp.exp2(sink * LOG2E - m2), jnp.zeros((tq, LANES), F32))
            l, acc = lax.fori_loop(0, n, lambda c, cr: weigh(c, cr, False), init)
            o_ref[rows, :] = acc / l
            lse_ref[rows, :] = jnp.broadcast_to(m * scale + jnp.log(l), (tq, LANES))
            return carry

        lax.fori_loop(0, nq, q_tile, 0)

    q_spec = pl.BlockSpec((T, LANES), lambda h: (0, h))
    kv_spec = pl.BlockSpec((T, LANES), lambda h: (0, h // group))
    return pl.pallas_call(
        kern, name=name,
        out_shape=[jax.ShapeDtypeStruct((T, H * LANES), F32)] * 2,
        grid=(H,),
        in_specs=[pl.BlockSpec(memory_space=pltpu.SMEM), q_spec, kv_spec, kv_spec],
        out_specs=[q_spec, q_spec],
        scratch_shapes=[pltpu.VMEM((slots, tq, ck), F32)],
        compiler_params=_params(("parallel",)),
    )(sinks, q, k, v)


def _attn_bwd(name, q, k, v, do, lse, delta, group, causal, scale):
    T = q.shape[0]
    H = q.shape[1] // LANES
    tq, ck, _ = _attn_plan(T, causal)
    nq = T // tq
    c2 = scale * LOG2E

    def kern(q_ref, k_ref, v_ref, do_ref, lse_ref, dl_ref, dq_ref, dk_ref, dv_ref):
        dk_ref[...] = jnp.zeros_like(dk_ref)
        dv_ref[...] = jnp.zeros_like(dv_ref)

        def q_tile(i, carry):
            rows = pl.ds(pl.multiple_of(i * tq, tq), tq)
            qq, dd = q_ref[rows, :], do_ref[rows, :]
            lse2, dl_c = lse_ref[rows, :][:, 0:1] * LOG2E, dl_ref[rows, :][:, 0:1]

            def chunk(c, dq, masked):
                start, first = _chunk(i, c, T, tq, ck, causal)
                keys = pl.ds(start, ck)
                kk, vv = k_ref[keys, :], v_ref[keys, :]
                s = lax.dot_general(qq, kk, _DIMS["nt"], preferred_element_type=F32)
                if masked:
                    s = _mask(s, i, start, first, tq, ck, causal)
                p = jnp.exp2(s * c2 - lse2)
                dv_ref[keys, :] += lax.dot_general(p.astype(MXU_DTYPE), dd, _DIMS["tn"],
                                                   preferred_element_type=F32)
                dp = lax.dot_general(dd, vv, _DIMS["nt"], preferred_element_type=F32)
                ds = (p * (dp - dl_c)).astype(MXU_DTYPE)
                dk_ref[keys, :] += lax.dot_general(ds, qq, _DIMS["tn"], preferred_element_type=F32) * scale
                return dq + jnp.dot(ds, kk, preferred_element_type=F32)

            dq = _chunk_loop(_n_chunks(i, tq, ck, causal), chunk, jnp.zeros((tq, LANES), F32), causal)
            dq_ref[rows, :] = dq * scale
            return carry

        lax.fori_loop(0, nq, q_tile, 0)

    q_spec = pl.BlockSpec((T, LANES), lambda h: (0, h))
    kv_spec = pl.BlockSpec((T, LANES), lambda h: (0, h // group))
    return pl.pallas_call(
        kern, name=name,
        out_shape=[jax.ShapeDtypeStruct((T, H * LANES), F32)] * 3,
        grid=(H,),
        in_specs=[q_spec, kv_spec, kv_spec, q_spec, q_spec, q_spec],
        out_specs=[q_spec, q_spec, q_spec],
        compiler_params=_params(("parallel",)),
    )(q, k, v, do, lse, delta)


def _ew(name, fn, ins, out_dtypes):
    shape = ins[0].shape
    flat = [a.reshape(-1, shape[-1]) for a in ins]
    R, C = flat[0].shape
    tr = _tile(R, (512, 256, 128, 64, 32, 16, 8))
    n_in = len(ins)

    def kern(*refs):
        res = fn(*[r[...] for r in refs[:n_in]])
        for o_ref, val in zip(refs[n_in:], res):
            o_ref[...] = val.astype(o_ref.dtype)

    spec = pl.BlockSpec((tr, C), lambda i: (i, 0))
    outs = pl.pallas_call(
        kern, name=name,
        out_shape=[jax.ShapeDtypeStruct((R, C), dt) for dt in out_dtypes],
        grid=(R // tr,), in_specs=[spec] * n_in, out_specs=[spec] * len(out_dtypes),
        compiler_params=_params(("parallel",)),
    )(*flat)
    return [o.reshape(shape) for o in outs]


def _adamw(name, w, g, m, v):
    c1 = 1.0 - ADAM_B1 ** ADAM_STEP
    c2 = 1.0 - ADAM_B2 ** ADAM_STEP

    def fn(wv, gv, mv, vv):
        mn = ADAM_B1 * mv + (1.0 - ADAM_B1) * gv
        vn = ADAM_B2 * vv + (1.0 - ADAM_B2) * (gv * gv)
        delta = -ADAM_LR * ((mn / c1) / (jnp.sqrt(vn / c2) + ADAM_EPS) + ADAM_WD * wv)
        return delta, mn, vn

    return _ew(name, fn, [w, g, m, v], [F32, F32, F32])


_ANY = pl.BlockSpec(memory_space=pl.ANY)


def _where_am_i():
    x, y, c = lax.axis_index("x"), lax.axis_index("y"), lax.axis_index("c")
    chips = [(1 - x, y), (x, 1 - y), (1 - x, 1 - y)]
    return x, y, c, chips


def _gather_weights(shards, meta):
    arrs = list(shards) + [meta]
    n = len(arrs)
    per = [2] * len(shards) + [8]

    def body(*refs):
        ins, outs = refs[:n], refs[n:2 * n]
        send1, recv1, send2, recv2 = refs[2 * n:]
        x, y, c, chips = _where_am_i()
        me = 2 * x + y

        def half(ref, k, cc):
            return ref.at[pl.ds(per[k] * cc, per[k])]

        first = []
        for k in range(n):
            for j, (cx, cy) in enumerate(chips):
                first.append(pltpu.make_async_remote_copy(
                    src_ref=half(ins[k], k, c), dst_ref=half(outs[k].at[me], k, c),
                    send_sem=send1.at[k, j], recv_sem=recv1.at[k, j],
                    device_id=(cx, cy, c), device_id_type=MESH))
        for cp in first:
            cp.start()
        passed = []
        for k in range(n):
            for j, (cx, cy) in enumerate(chips):
                landed = half(outs[k].at[2 * cx + cy], k, c)
                pltpu.make_async_remote_copy(
                    src_ref=landed, dst_ref=landed, send_sem=send1.at[k, j], recv_sem=recv1.at[k, j],
                    device_id=(cx, cy, c), device_id_type=MESH).wait_recv()
                fwd = pltpu.make_async_remote_copy(
                    src_ref=landed, dst_ref=landed, send_sem=send2.at[k, j], recv_sem=recv2.at[k, j],
                    device_id=(x, y, 1 - c), device_id_type=MESH)
                fwd.start()
                passed.append(fwd)
        for k in range(n):
            for j, (cx, cy) in enumerate(chips):
                other = half(outs[k].at[2 * cx + cy], k, 1 - c)
                pltpu.make_async_remote_copy(
                    src_ref=other, dst_ref=other, send_sem=send2.at[k, j], recv_sem=recv2.at[k, j],
                    device_id=(x, y, 1 - c), device_id_type=MESH).wait_recv()
        for cp in first + passed:
            cp.wait_send()

    return pl.pallas_call(
        body, name="gather_weights",
        out_shape=[jax.ShapeDtypeStruct((4,) + a.shape, a.dtype) for a in arrs],
        in_specs=[_ANY] * n, out_specs=[_ANY] * n,
        scratch_shapes=[pltpu.SemaphoreType.DMA((n, 3))] * 4,
    )(*arrs)


def _row_chunks(r):
    n = 4 if r % 64 == 0 else 1
    return [(q * (r // n), r // n) for q in range(n)]


def _pair_exchange(grads):
    n = len(grads)

    def body(*refs):
        ins, got = refs[:n], refs[n:2 * n]
        send, recv = refs[2 * n:]
        x, y, c, _ = _where_am_i()
        sib = (x, y, 1 - c)
        for k in range(n):
            for ch in range(4):
                for l in range(2):
                    pltpu.make_async_remote_copy(
                        src_ref=ins[k].at[ch, 2 * (1 - c) + l], dst_ref=got[k].at[ch, l],
                        send_sem=send.at[k], recv_sem=recv.at[k], device_id=sib, device_id_type=MESH).start()
        for k in range(n):
            pltpu.make_async_remote_copy(
                src_ref=got[k], dst_ref=got[k], send_sem=send.at[k], recv_sem=recv.at[k],
                device_id=sib, device_id_type=MESH).wait()

    return pl.pallas_call(
        body, name="reduce_pair",
        out_shape=[jax.ShapeDtypeStruct((4, 2) + g.shape[2:], g.dtype) for g in grads],
        in_specs=[_ANY] * n, out_specs=[_ANY] * n,
        scratch_shapes=[pltpu.SemaphoreType.DMA((n,))] * 2,
    )(*grads)


def _pair_add(full, got, c):
    _, _, r, cols = full.shape
    tr = _tile(r, (512, 256, 352, 128))

    def kern(c_ref, a_ref, b_ref, o32_ref, o16_ref):
        tot = a_ref[...] + b_ref[...].astype(F32)
        o32_ref[...] = tot
        o16_ref[...] = tot.astype(o16_ref.dtype)

    blk = (None, None, tr, cols)
    mine = pl.BlockSpec(blk, lambda ch, l, i, cr: (ch, 2 * cr[0] + l, i, 0))
    same = pl.BlockSpec(blk, lambda ch, l, i, cr: (ch, l, i, 0))
    return pl.pallas_call(
        kern, name="pair_add",
        out_shape=[jax.ShapeDtypeStruct(got.shape, F32), jax.ShapeDtypeStruct(got.shape, got.dtype)],
        grid_spec=pltpu.PrefetchScalarGridSpec(
            num_scalar_prefetch=1, grid=(4, 2, r // tr), in_specs=[mine, same], out_specs=[same, same]),
        compiler_params=_params(("parallel", "parallel", "parallel")),
    )(c.reshape(1), full, got)


def _chip_scatter(parts):
    n = len(parts)

    def body(*refs):
        ins, outs = refs[:n], refs[n:2 * n]
        send, recv = refs[2 * n:]
        x, y, c, chips = _where_am_i()
        me = 2 * x + y
        for k in range(n):
            for j, (cx, cy) in enumerate(chips):
                for l in range(2):
                    pltpu.make_async_remote_copy(
                        src_ref=ins[k].at[2 * cx + cy, l], dst_ref=outs[k].at[me, l],
                        send_sem=send.at[k, j], recv_sem=recv.at[k, j],
                        device_id=(cx, cy, c), device_id_type=MESH).start()
        for k in range(n):
            for j, (cx, cy) in enumerate(chips):
                slot = outs[k].at[2 * cx + cy]
                pltpu.make_async_remote_copy(
                    src_ref=slot, dst_ref=slot, send_sem=send.at[k, j], recv_sem=recv.at[k, j],
                    device_id=(cx, cy, c), device_id_type=MESH).wait()

    return pl.pallas_call(
        body, name="reduce_chips",
        out_shape=[jax.ShapeDtypeStruct(p.shape, p.dtype) for p in parts],
        in_specs=[_ANY] * n, out_specs=[_ANY] * n,
        scratch_shapes=[pltpu.SemaphoreType.DMA((n, 3))] * 2,
    )(*parts)


def _chip_add(landed, mine, me, c):
    _, _, r, cols = landed.shape
    tr = _tile(r, (512, 256, 352, 128))

    def kern(me_ref, c_ref, land_ref, own_ref, o_ref):
        own = own_ref[...]
        tot = None
        for j in range(4):
            term = jnp.where(me_ref[0] == j, own, land_ref[j].astype(F32))
            tot = term if tot is None else tot + term
        o_ref[...] = tot

    return pl.pallas_call(
        kern, name="chip_add",
        out_shape=jax.ShapeDtypeStruct((4, r, cols), F32),
        grid_spec=pltpu.PrefetchScalarGridSpec(
            num_scalar_prefetch=2, grid=(2, r // tr),
            in_specs=[pl.BlockSpec((4, None, tr, cols), lambda l, i, mr, cr: (0, l, i, 0)),
                      pl.BlockSpec((None, None, tr, cols), lambda l, i, mr, cr: (mr[0], l, i, 0))],
            out_specs=pl.BlockSpec((None, tr, cols), lambda l, i, mr, cr: (2 * cr[0] + l, i, 0))),
        compiler_params=_params(("parallel", "parallel")),
    )(me.reshape(1), c.reshape(1), landed, mine)


def _pair_join(sums):
    n = len(sums)

    def body(*refs):
        bufs = refs[n:2 * n]
        send, recv = refs[2 * n:]
        x, y, c, _ = _where_am_i()
        sib = (x, y, 1 - c)
        for k in range(n):
            for l in range(2):
                for r0, rn in _row_chunks(bufs[k].shape[1]):
                    piece = bufs[k].at[2 * c + l, pl.ds(r0, rn)]
                    pltpu.make_async_remote_copy(
                        src_ref=piece, dst_ref=piece, send_sem=send.at[k], recv_sem=recv.at[k],
                        device_id=sib, device_id_type=MESH).start()
        for k in range(n):
            theirs = bufs[k].at[pl.ds(2 * (1 - c), 2)]
            pltpu.make_async_remote_copy(
                src_ref=theirs, dst_ref=theirs, send_sem=send.at[k], recv_sem=recv.at[k],
                device_id=sib, device_id_type=MESH).wait()

    return pl.pallas_call(
        body, name="reduce_join",
        out_shape=[jax.ShapeDtypeStruct(s.shape, s.dtype) for s in sums],
        in_specs=[_ANY] * n, out_specs=[_ANY] * n,
        input_output_aliases={k: k for k in range(n)},
        scratch_shapes=[pltpu.SemaphoreType.DMA((n,))] * 2,
    )(*sums)


def _allreduce_small(buf):
    R = buf.shape[0]

    def body(in_ref, out_ref, land, send, recv):
        x, y, c, _ = _where_am_i()
        me = 4 * x + 2 * y + c
        land[me] = in_ref[...]
        cps = []
        for k in range(1, 8):
            px, py, pc = x ^ (k >> 2), y ^ ((k >> 1) & 1), c ^ (k & 1)
            cps.append(pltpu.make_async_remote_copy(
                src_ref=in_ref, dst_ref=land.at[me], send_sem=send.at[k - 1], recv_sem=recv.at[k - 1],
                device_id=(px, py, pc), device_id_type=MESH))
        for cp in cps:
            cp.start()
        for k in range(1, 8):
            px, py, pc = x ^ (k >> 2), y ^ ((k >> 1) & 1), c ^ (k & 1)
            slot = land.at[4 * px + 2 * py + pc]
            pltpu.make_async_remote_copy(
                src_ref=slot, dst_ref=slot, send_sem=send.at[k - 1], recv_sem=recv.at[k - 1],
                device_id=(px, py, pc), device_id_type=MESH).wait_recv()
        for cp in cps:
            cp.wait_send()
        tot = land[0]
        for d in range(1, 8):
            tot = tot + land[d]
        out_ref[...] = tot

    vm = pl.BlockSpec(memory_space=pltpu.VMEM)
    return pl.pallas_call(
        body, name="allreduce_small",
        out_shape=jax.ShapeDtypeStruct(buf.shape, F32),
        in_specs=[vm], out_specs=vm,
        scratch_shapes=[pltpu.VMEM((8, R, LANES), F32), pltpu.SemaphoreType.DMA((7,)),
                        pltpu.SemaphoreType.DMA((7,))],
    )(buf)


def _rope_tables(T):
    pos = (jnp.arange(T) - FRONT).astype(F32)
    lane = jnp.arange(LANES)
    inv_a = ROPE_THETA ** (-(2 * ((lane % 64) % 32)).astype(F32) / SWA_HEAD_DIM)
    ang_a = pos[:, None] * inv_a[None, :]
    cosa, sina = jnp.cos(ang_a), jnp.sin(ang_a)
    inv_m = ROPE_THETA ** (-(2 * ((lane - KR_LANE) % 16)).astype(F32) / MLA_ROPE_DIM)
    ang_m = pos[:, None] * inv_m[None, :]
    on = ((lane >= KR_LANE) & (lane < KR_LANE + MLA_ROPE_DIM))[None, :]
    cosm = jnp.where(on, jnp.cos(ang_m), 1.0)
    sinm = jnp.where(on, jnp.sin(ang_m), 0.0)
    return cosa, sina, cosm, sinm


def _cols_from_chips(g):
    return jnp.concatenate([g[j] for j in range(4)], axis=-1)


def _rows_from_chips(g):
    return jnp.concatenate([g[j] for j in range(4)], axis=-2)


def _cols_to_chips(w):
    L, r, c4 = w.shape
    return jnp.moveaxis(w.reshape(L, r, 4, c4 // 4), 2, 0)


def _rows_to_chips(w):
    L, r4, c = w.shape
    return jnp.moveaxis(w.reshape(L, 4, r4 // 4, c), 1, 0)


def _local_step(x2, target, meta_full, natural, p):
    T = BLOCK + x2.shape[0]
    L = DEPTH
    W_in, W_qup, W_kvup, W_o, W_gate, W_up, W_down = natural
    zpad = lambda n: jnp.zeros((L, D_MODEL, n), W_in.dtype)
    W_in = jnp.concatenate([W_in[..., :C_KR], zpad(KR_LANE), W_in[..., C_KR:IN_W],
                            zpad(LANES - KR_LANE - MLA_ROPE_DIM)], axis=-1)
    W_qup = W_qup.reshape(L, MLA_Q_RANK, MLA_HEADS, MLA_QK_DIM)
    W_qup = jnp.pad(W_qup, ((0, 0), (0, 0), (0, 0), (0, LANES - MLA_QK_DIM))).reshape(L, MLA_Q_RANK, SLOT_W)
    attn_norm, q_norm, kv_norm, sinks = p["attn_norm"], p["q_norm"], p["kv_norm"], p["sinks"]
    out_norm_swa, out_norm_mla, ffn_norm, final_norm = (
        p["out_norm_swa"], p["out_norm_mla"], p["ffn_norm"], p["final_norm"])

    cosa, sina, cosm, sinm = _rope_tables(T)
    no_sink = jnp.full((MLA_HEADS,), NEG, F32)
    scale_a, scale_b = SWA_HEAD_DIM ** -0.5, MLA_QK_DIM ** -0.5
    row = lambda v: v.reshape(1, -1)

    h = jnp.concatenate([jnp.zeros((FRONT, D_MODEL), F32), meta_full, x2], axis=0)
    saved = []
    for l in range(L):
        u = _rmsnorm_fwd("attn_norm", h, row(attn_norm[l]))
        proj = _mm("in_proj", u, W_in[l], "nn")
        qa, ka, va, qn, cn, kr = _prep1(proj, row(q_norm[l]), row(kv_norm[l]), cosa, sina, cosm, sinm)
        qb = _mm("q_up", qn, W_qup[l], "nn")
        kvb = _mm("kv_up", cn, W_kvup[l], "nn")
        qs, ks, vs = _prep2(qb, kvb, kr, cosm, sinm)
        oa, lse_a = _attn_fwd("swa_fwd", qa, ka, va, sinks[l], 4, False, scale_a)
        ob, lse_b = _attn_fwd("mla_fwd", qs, ks, vs, no_sink, 1, True, scale_b)
        mix = _merge_fwd(oa, ob, row(out_norm_swa[l]), row(out_norm_mla[l]))
        h1 = _mm("o_proj", mix, W_o[l], "nn", res=h)
        u2 = _rmsnorm_fwd("ffn_norm", h1, row(ffn_norm[l]))
        a = _mm("gate_proj", u2, W_gate[l], "nn")
        b = _mm("up_proj", u2, W_up[l], "nn")
        hm = _swiglu_fwd(a, b)
        h2 = _mm("down_proj", hm, W_down[l], "nn", res=h1)
        saved.append((h, u, proj, qa, ka, va, qn, cn, qs, ks, vs, oa, lse_a, ob, lse_b, mix, h1, u2, a, b, hm))
        h = h2

    dh, d_final, loss_row = _loss_head(h, target, row(final_norm))

    gw = {k: [None] * L for k in ("in", "qup", "kvup", "o", "gate", "up", "down")}
    gs = {k: [None] * L for k in ("attn", "qn", "kvn", "sink", "ga", "gb", "ffn")}
    for l in reversed(range(L)):
        (h0, u, proj, qa, ka, va, qn, cn, qs, ks, vs, oa, lse_a, ob, lse_b, mix, h1, u2, a, b, hm) = saved[l]
        gw["down"][l] = _mm("down_dw", hm, dh, "tn")
        dhm = _mm("down_dx", dh, W_down[l], "nt")
        da, db = _swiglu_bwd(a, b, dhm)
        gw["gate"][l] = _mm("gate_dw", u2, da, "tn")
        gw["up"][l] = _mm("up_dw", u2, db, "tn")
        du2 = _mm("gate_dx", da, W_gate[l], "nt")
        du2 = _mm("up_dx", db, W_up[l], "nt", res=du2)
        dh1, gs["ffn"][l] = _rmsnorm_bwd("ffn_norm_bwd", h1, row(ffn_norm[l]), du2, dh)
        gw["o"][l] = _mm("o_dw", mix, dh1, "tn")
        dmix = _mm("o_dx", dh1, W_o[l], "nt")
        sink_slots = jnp.repeat(sinks[l], LANES).reshape(1, SLOT_W)
        doa, dla, dob, dlb, gs["ga"][l], gs["gb"][l], dsink = _merge_bwd(
            dmix, oa, ob, lse_a, row(out_norm_swa[l]), row(out_norm_mla[l]), sink_slots)
        gs["sink"][l] = dsink.reshape(SWA_HEADS, LANES)[:, 0]
        dqs, dks, dvs = _attn_bwd("mla_bwd", qs, ks, vs, dob, lse_b, dlb, 1, True, scale_b)
        dqa, dka, dva = _attn_bwd("swa_bwd", qa, ka, va, doa, lse_a, dla, 4, False, scale_a)
        dqb, dkvb, dkr = _prep2_bwd(dqs, dks, dvs, cosm, sinm)
        gw["qup"][l] = _mm("q_up_dw", qn, dqb, "tn")
        gw["kvup"][l] = _mm("kv_up_dw", cn, dkvb, "tn")
        dqn = _mm("q_up_dx", dqb, W_qup[l], "nt")
        dcn = _mm("kv_up_dx", dkvb, W_kvup[l], "nt")
        dproj, gs["qn"][l], gs["kvn"][l] = _prep1_bwd(
            proj, dqa, dka, dva, dqn, dcn, dkr, cosa, sina, row(q_norm[l]), row(kv_norm[l]))
        gw["in"][l] = _mm("in_dw", u, dproj, "tn")
        du = _mm("in_dx", dproj, W_in[l], "nt")
        dh, gs["attn"][l] = _rmsnorm_bwd("attn_norm_bwd", h0, row(attn_norm[l]), du, dh1)

    st = lambda k: jnp.stack(gw[k])
    d_in = st("in")
    d_in = jnp.concatenate([d_in[..., :C_KR], d_in[..., C_KR + KR_LANE:C_KR + KR_LANE + MLA_ROPE_DIM]], axis=-1)
    d_qup = st("qup").reshape(L, MLA_Q_RANK, MLA_HEADS, LANES)[..., :MLA_QK_DIM].reshape(L, MLA_Q_RANK, -1)
    d_nat = [d_in, d_qup, st("kvup"), st("o"), st("gate"), st("up"), st("down")]
    return loss_row, dh, d_nat, gs, d_final


def kernel(x, meta_tokens, attn_norm, w_in, q_norm, w_q_up, kv_norm, w_kv_up, sinks, out_norm_swa, out_norm_mla, w_o, ffn_norm, w_gate, w_up, w_down, final_norm, loss_target, m_meta_tokens, m_attn_norm, m_w_in, m_q_norm, m_w_q_up, m_kv_norm, m_w_kv_up, m_sinks, m_out_norm_swa, m_out_norm_mla, m_w_o, m_ffn_norm, m_w_gate, m_w_up, m_w_down, m_final_norm, v_meta_tokens, v_attn_norm, v_w_in, v_q_norm, v_w_q_up, v_kv_norm, v_w_kv_up, v_sinks, v_out_norm_swa, v_out_norm_mla, v_w_o, v_ffn_norm, v_w_gate, v_w_up, v_w_down, v_final_norm):
    assert x.shape[0] == 1 and x.shape[1] % BLOCK == 0
    big = [w_in, w_q_up, w_kv_up, w_o, w_gate, w_up, w_down]

    c_idx = lax.axis_index("c").astype(jnp.int32)
    chip = (2 * lax.axis_index("x") + lax.axis_index("y")).astype(jnp.int32)
    mine = [w.astype(BF16) for w in big] + [meta_tokens]
    gathered = _gather_weights(mine[:-1], mine[-1])
    parts = [[jnp.where(chip == j, own, g[j]) for j in range(4)] for own, g in zip(mine, gathered)]
    join = [_cols_from_chips, _cols_from_chips, _cols_from_chips, _rows_from_chips, _cols_from_chips,
            _cols_from_chips, _rows_from_chips]
    natural = [f(p) for f, p in zip(join, parts[:-1])]
    meta_full = jnp.concatenate(parts[-1], axis=-1)
    small_p = dict(attn_norm=attn_norm, q_norm=q_norm, kv_norm=kv_norm, sinks=sinks, out_norm_swa=out_norm_swa,
                   out_norm_mla=out_norm_mla, ffn_norm=ffn_norm, final_norm=final_norm)
    loss_row, dh, d_nat, gs, d_final = _local_step(x[0], loss_target[0], meta_full, natural, small_p)
    grad_x = dh[BLOCK:][None]

    split = [_cols_to_chips, _cols_to_chips, _cols_to_chips, _rows_to_chips, _cols_to_chips, _cols_to_chips,
             _rows_to_chips]
    full = [f(d) for f, d in zip(split, d_nat)]

    got = _pair_exchange([f.astype(BF16) for f in full])
    sums = [_pair_add(f, g, c_idx) for f, g in zip(full, got)]
    landed = _chip_scatter([s16 for _, s16 in sums])
    g_big = _pair_join([_chip_add(t, s32, chip, c_idx) for t, (s32, _) in zip(landed, sums)])

    small = [jnp.stack(gs["attn"]).reshape(-1), jnp.stack(gs["qn"]).reshape(-1), jnp.stack(gs["kvn"]).reshape(-1),
             jnp.stack(gs["sink"]).reshape(-1), jnp.stack(gs["ga"]).reshape(-1), jnp.stack(gs["gb"]).reshape(-1),
             jnp.stack(gs["ffn"]).reshape(-1), d_final.reshape(-1)]
    sizes = [s.shape[0] for s in small]
    flat = jnp.concatenate(small + [dh[FRONT:BLOCK].reshape(-1), loss_row[0, :1]])
    n_flat = flat.shape[0]
    rows_needed = -(-n_flat // (8 * LANES)) * 8
    flat = jnp.pad(flat, (0, rows_needed * LANES - n_flat)).reshape(rows_needed, LANES)
    tot = _allreduce_small(flat).reshape(-1)
    n_small = sum(sizes)
    loss = tot[n_small + N_META * D_MODEL]
    g_meta_full = tot[n_small:n_small + N_META * D_MODEL].reshape(N_META, D_MODEL)
    g_meta_mine = lax.dynamic_slice_in_dim(g_meta_full, chip * (D_MODEL // 4), D_MODEL // 4, axis=1)

    small_w = [attn_norm, q_norm, kv_norm, sinks, out_norm_swa, out_norm_mla, ffn_norm, final_norm]
    small_m = [m_attn_norm, m_q_norm, m_kv_norm, m_sinks, m_out_norm_swa, m_out_norm_mla, m_ffn_norm, m_final_norm]
    small_v = [v_attn_norm, v_q_norm, v_kv_norm, v_sinks, v_out_norm_swa, v_out_norm_mla, v_ffn_norm, v_final_norm]
    n_rows = -(-n_small // (8 * LANES)) * 8

    def pack(arrs):
        f = jnp.concatenate([a.reshape(-1) for a in arrs])
        return jnp.pad(f, (0, n_rows * LANES - n_small), constant_values=1.0).reshape(n_rows, LANES)

    g_small_pack = jnp.pad(tot[:n_small], (0, n_rows * LANES - n_small)).reshape(n_rows, LANES)
    upd_small = _adamw("adam_small", pack(small_w), g_small_pack, pack(small_m), pack(small_v))

    def unpack(p):
        f = p.reshape(-1)
        out, off = [], 0
        for a, n in zip(small_w, sizes):
            out.append(f[off:off + n].reshape(a.shape))
            off += n
        return out

    g_small = unpack(g_small_pack)
    d_small, m_small, v_small = [unpack(p) for p in upd_small]
    d_meta, nm_meta, nv_meta = _adamw("adam_meta", meta_tokens, g_meta_mine, m_meta_tokens, v_meta_tokens)

    big_m = [m_w_in, m_w_q_up, m_w_kv_up, m_w_o, m_w_gate, m_w_up, m_w_down]
    big_v = [v_w_in, v_w_q_up, v_w_kv_up, v_w_o, v_w_gate, v_w_up, v_w_down]
    upd_big = [_adamw("adam_big", w, g, m, v) for w, g, m, v in zip(big, g_big, big_m, big_v)]

    names = ["meta_tokens", "attn_norm", "w_in", "q_norm", "w_q_up", "kv_norm", "w_kv_up", "sinks",
             "out_norm_swa", "out_norm_mla", "w_o", "ffn_norm", "w_gate", "w_up", "w_down", "final_norm"]
    small_idx = {"attn_norm": 0, "q_norm": 1, "kv_norm": 2, "sinks": 3, "out_norm_swa": 4,
                 "out_norm_mla": 5, "ffn_norm": 6, "final_norm": 7}
    big_idx = {"w_in": 0, "w_q_up": 1, "w_kv_up": 2, "w_o": 3, "w_gate": 4, "w_up": 5, "w_down": 6}
    grads, deltas, new_m, new_v = [], [], [], []
    for nme in names:
        if nme == "meta_tokens":
            quad = (g_meta_mine, d_meta, nm_meta, nv_meta)
        elif nme in small_idx:
            i = small_idx[nme]
            quad = (g_small[i], d_small[i], m_small[i], v_small[i])
        else:
            i = big_idx[nme]
            quad = (g_big[i], *upd_big[i])
        grads.append(quad[0]); deltas.append(quad[1]); new_m.append(quad[2]); new_v.append(quad[3])
    return (loss, grad_x, *grads, *deltas, *new_m, *new_v)
```

```python
import jax
import jax.numpy as jnp
from jax import lax
from jax.experimental import pallas as pl
from jax.experimental.pallas import tpu as pltpu

F32 = jnp.float32
BF16 = jnp.bfloat16
MXU_DTYPE = BF16

D_MODEL = 1024
DEPTH = 4
N_META = 16
BLOCK = 128
WINDOW = 128
ROPE_THETA = 10000.0
EPS = 1e-6
NEG = -1e30
SWA_HEADS = 8
SWA_KV_HEADS = 2
SWA_HEAD_DIM = 64
MLA_HEADS = 8
MLA_Q_RANK = 256
MLA_KV_RANK = 128
MLA_NOPE_DIM = 64
MLA_ROPE_DIM = 32
MLA_V_DIM = 64
MLA_QK_DIM = MLA_NOPE_DIM + MLA_ROPE_DIM
D_FF = 2816
FRONT = (-N_META) % BLOCK
LANES = 128
SLOT_W = 8 * LANES
C_QA, C_KA, C_VA, C_QL, C_KL, C_KR, IN_WP = 0, 512, 640, 768, 1024, 1152, 1280
KR_LANE = 64
IN_W = 1184

ADAM_LR, ADAM_B1, ADAM_B2, ADAM_EPS, ADAM_WD, ADAM_STEP = 0.001, 0.9, 0.999, 1e-08, 0.01, 10

VMEM_LIMIT = 48 * 1024 * 1024
MESH = pl.DeviceIdType.MESH


def _tile(n, prefs):
    for t in prefs:
        if n % t == 0:
            return t
    return n


def _params(sem):
    return pltpu.CompilerParams(dimension_semantics=sem, vmem_limit_bytes=VMEM_LIMIT)


_DIMS = {"nn": (((1,), (0,)), ((), ())), "nt": (((1,), (1,)), ((), ())), "tn": (((0,), (0,)), ((), ()))}


def _mm(name, a, b, mode, out_dtype=F32, res=None):
    if mode == "nn":
        (M, K), (_, N) = a.shape, b.shape
    elif mode == "nt":
        (M, K), (N, _) = a.shape, b.shape
    else:
        (K, M), (_, N) = a.shape, b.shape
    lane_tiles = (1408, 1024, 640, 768, 512, 384, 256, 128)
    row_tiles = (528, 512, 384, 256, 128)
    bm = _tile(M, lane_tiles if mode == "tn" else row_tiles)
    bn = _tile(N, lane_tiles)
    bk = _tile(K, (1056,) + row_tiles if mode == "tn" else lane_tiles)
    nk = K // bk
    if mode == "tn":
        a_spec = pl.BlockSpec((bk, bm), lambda i, j, k: (k, i))
    else:
        a_spec = pl.BlockSpec((bm, bk), lambda i, j, k: (i, k))
    if mode == "nt":
        b_spec = pl.BlockSpec((bn, bk), lambda i, j, k: (j, k))
    else:
        b_spec = pl.BlockSpec((bk, bn), lambda i, j, k: (k, j))
    o_spec = pl.BlockSpec((bm, bn), lambda i, j, k: (i, j))
    in_specs = [a_spec, b_spec]
    args = [a, b]
    if res is not None:
        in_specs.append(o_spec)
        args.append(res)
    dims = _DIMS[mode]

    def kern(a_ref, b_ref, *rest):
        if res is not None:
            r_ref, o_ref = rest[0], rest[1]
            scr = rest[2:]
        else:
            r_ref, o_ref = None, rest[0]
            scr = rest[1:]
        p = lax.dot_general(a_ref[...].astype(MXU_DTYPE), b_ref[...].astype(MXU_DTYPE), dims,
                            preferred_element_type=F32)

        def finish(val):
            if r_ref is not None:
                val = val + r_ref[...]
            o_ref[...] = val.astype(o_ref.dtype)

        if nk == 1:
            finish(p)
        else:
            acc = scr[0]
            k = pl.program_id(2)

            @pl.when(k == 0)
            def _():
                acc[...] = p

            @pl.when(k > 0)
            def _():
                acc[...] += p

            @pl.when(k == nk - 1)
            def _():
                finish(acc[...])

    return pl.pallas_call(
        kern, name=name,
        out_shape=jax.ShapeDtypeStruct((M, N), out_dtype),
        grid=(M // bm, N // bn, nk),
        in_specs=in_specs, out_specs=o_spec,
        scratch_shapes=[pltpu.VMEM((bm, bn), F32)] if nk > 1 else [],
        compiler_params=_params(("parallel", "parallel", "arbitrary")),
    )(*args)


def _rowmap(name, body, rows, vecs, outs, accs=(), tr_prefs=(384, 256, 128)):
    R = rows[0].shape[0]
    tr = _tile(R, tr_prefs)
    n_r, n_v, n_o, n_a = len(rows), len(vecs), len(outs), len(accs)

    def kern(*refs):
        ins = [r[...] for r in refs[:n_r + n_v]]
        o_refs = refs[n_r + n_v:n_r + n_v + n_o]
        a_refs = refs[n_r + n_v + n_o:]
        res = body(*ins)
        for o_ref, val in zip(o_refs, res[:n_o]):
            o_ref[...] = val.astype(o_ref.dtype)
        if n_a:
            first = pl.program_id(0) == 0

            @pl.when(first)
            def _():
                for a_ref, val in zip(a_refs, res[n_o:]):
                    a_ref[...] = val

            @pl.when(jnp.logical_not(first))
            def _():
                for a_ref, val in zip(a_refs, res[n_o:]):
                    a_ref[...] += val

    in_specs = [pl.BlockSpec((tr, r.shape[1]), lambda i: (i, 0)) for r in rows]
    in_specs += [pl.BlockSpec((1, v.shape[1]), lambda i: (0, 0)) for v in vecs]
    out_specs = [pl.BlockSpec((tr, c), lambda i: (i, 0)) for c, _ in outs]
    out_specs += [pl.BlockSpec((1, c), lambda i: (0, 0)) for c in accs]
    out_shape = [jax.ShapeDtypeStruct((R, c), dt) for c, dt in outs]
    out_shape += [jax.ShapeDtypeStruct((1, c), F32) for c in accs]
    return pl.pallas_call(
        kern, name=name, out_shape=out_shape, grid=(R // tr,),
        in_specs=in_specs, out_specs=out_specs,
        compiler_params=_params(("arbitrary",) if n_a else ("parallel",)),
    )(*rows, *vecs)


def _lane(shape):
    return lax.broadcasted_iota(jnp.int32, shape, 1)


def _rot_swa(x):
    lane = _lane(x.shape)
    return jnp.where((lane & 63) < 32, -pltpu.roll(x, 96, 1), pltpu.roll(x, 32, 1))


def _rot_mla(x):
    lane = _lane(x.shape)
    lo = jnp.where(lane >= KR_LANE, -pltpu.roll(x, 112, 1), 0.0)
    hi = jnp.where(lane < KR_LANE + MLA_ROPE_DIM, pltpu.roll(x, 16, 1), 0.0)
    return jnp.where(lane < KR_LANE + 16, lo, hi)


def _rope(x, cos, sin, rot):
    return x * cos + rot(x) * sin


def _rope_t(g, cos, sin, rot):
    return g * cos - rot(g * sin)


def _low(x):
    return jnp.where(_lane(x.shape) < 64, x, 0.0)


def _blk(x, j):
    return x[:, j * LANES:(j + 1) * LANES]


def _rms_r(x, width):
    return lax.rsqrt(jnp.sum(x * x, axis=-1, keepdims=True) * (1.0 / width) + EPS)


def _rms_bwd(x, g, dy, width):
    r = _rms_r(x, width)
    gdy = dy * g
    dot = jnp.sum(gdy * x, axis=-1, keepdims=True)
    dx = r * gdy - x * (r * r * r * (1.0 / width) * dot)
    return dx, dy * x * r


def _colsum(x):
    return jnp.sum(x, axis=0, keepdims=True)


def _rmsnorm_fwd(name, x, g):
    def body(xv, gv):
        return (xv * _rms_r(xv, D_MODEL) * gv,)
    return _rowmap(name, body, [x], [g], [(D_MODEL, BF16)])[0]


def _rmsnorm_bwd(name, x, g, dy, dres):
    def body(xv, dyv, dr, gv):
        dx, dg = _rms_bwd(xv, gv, dyv, D_MODEL)
        return dx + dr, _colsum(dg)
    return _rowmap(name, body, [x, dy, dres], [g], [(D_MODEL, F32)], [D_MODEL])


def _prep1(proj, qn_g, kv_g, cosa, sina, cosm, sinm):
    def body(p, ca, sa, cm, sm, gq, gk):
        qa = []
        for j in range(4):
            xr = _rope(_blk(p, j), ca, sa, _rot_swa)
            qa += [_low(xr), _low(pltpu.roll(xr, 64, 1))]
        kr_ = _rope(_blk(p, C_KA // LANES), ca, sa, _rot_swa)
        ka = [_low(kr_), _low(pltpu.roll(kr_, 64, 1))]
        vv = _blk(p, C_VA // LANES)
        va = [_low(vv), _low(pltpu.roll(vv, 64, 1))]
        ql = p[:, C_QL:C_QL + MLA_Q_RANK]
        qn = ql * _rms_r(ql, MLA_Q_RANK) * gq
        kl = p[:, C_KL:C_KL + MLA_KV_RANK]
        cn = kl * _rms_r(kl, MLA_KV_RANK) * gk
        kr = _rope(_blk(p, C_KR // LANES), cm, sm, _rot_mla)
        return (jnp.concatenate(qa, 1), jnp.concatenate(ka, 1), jnp.concatenate(va, 1), qn, cn, kr)
    return _rowmap("prep1", body, [proj, cosa, sina, cosm, sinm], [qn_g, kv_g],
                   [(SLOT_W, BF16), (2 * LANES, BF16), (2 * LANES, BF16),
                    (MLA_Q_RANK, BF16), (MLA_KV_RANK, BF16), (LANES, F32)])


def _prep2(qb, kvb, kr, cosm, sinm):
    def body(q, kv, krv, cm, sm):
        qs, ks, vs = [], [], []
        for h in range(MLA_HEADS):
            qs.append(_rope(_blk(q, h), cm, sm, _rot_mla))
            kvh = _blk(kv, h)
            ks.append(_low(kvh) + krv)
            vs.append(_low(pltpu.roll(kvh, 64, 1)))
        return jnp.concatenate(qs, 1), jnp.concatenate(ks, 1), jnp.concatenate(vs, 1)
    return _rowmap("prep2", body, [qb, kvb, kr, cosm, sinm], [],
                   [(SLOT_W, BF16), (SLOT_W, BF16), (SLOT_W, BF16)])


def _compact(slots):
    return jnp.concatenate(
        [_blk(slots, 2 * j) + pltpu.roll(_blk(slots, 2 * j + 1), 64, 1) for j in range(4)], 1)


def _expand(nat):
    out = []
    for j in range(4):
        b = _blk(nat, j)
        out += [_low(b), _low(pltpu.roll(b, 64, 1))]
    return jnp.concatenate(out, 1)


def _merge_fwd(oa, ob, ga, gb):
    def body(a, b, gav, gbv):
        xa, xb = _compact(a), _compact(b)
        return (jnp.concatenate([xa * _rms_r(xa, 512) * gav, xb * _rms_r(xb, 512) * gbv], 1),)
    return _rowmap("merge_fwd", body, [oa, ob], [ga, gb], [(D_MODEL, BF16)])[0]


def _merge_bwd(dmix, oa, ob, lse_a, ga, gb, sink_slots):
    def body(dm, a, b, lse, gav, gbv, sk):
        outs = []
        accs = []
        for o, g, lo in ((a, gav, 0), (b, gbv, 512)):
            x = _compact(o)
            dx, dg = _rms_bwd(x, g, dm[:, lo:lo + 512], 512)
            do = _expand(dx)
            delta = jnp.concatenate(
                [jnp.broadcast_to(jnp.sum(_blk(do, h) * _blk(o, h), axis=-1, keepdims=True),
                                  (do.shape[0], LANES)) for h in range(8)], 1)
            outs += [do, delta]
            accs.append(_colsum(dg))
        dsink = _colsum(-jnp.exp(sk - lse) * outs[1])
        return (*outs, *accs, dsink)
    return _rowmap("merge_bwd", body, [dmix, oa, ob, lse_a], [ga, gb, sink_slots],
                   [(SLOT_W, BF16), (SLOT_W, F32), (SLOT_W, BF16), (SLOT_W, F32)],
                   [512, 512, SLOT_W])


def _prep2_bwd(dq, dk, dv, cosm, sinm):
    def body(dqv, dkv, dvv, cm, sm):
        dqb, dkvb = [], []
        krsum = jnp.zeros((dqv.shape[0], LANES), F32)
        for h in range(MLA_HEADS):
            dqb.append(_rope_t(_blk(dqv, h), cm, sm, _rot_mla))
            dkh = _blk(dkv, h)
            dkvb.append(_low(dkh) + pltpu.roll(_blk(dvv, h), 64, 1))
            krsum = krsum + dkh
        lane = _lane(krsum.shape)
        dkr = jnp.where((lane >= KR_LANE) & (lane < KR_LANE + MLA_ROPE_DIM),
                        _rope_t(krsum, cm, sm, _rot_mla), 0.0)
        return jnp.concatenate(dqb, 1), jnp.concatenate(dkvb, 1), dkr
    return _rowmap("prep2_bwd", body, [dq, dk, dv, cosm, sinm], [],
                   [(SLOT_W, BF16), (SLOT_W, BF16), (LANES, F32)])


def _prep1_bwd(proj, dqa, dka, dva, dqn, dcn, dkr, cosa, sina, qn_g, kv_g):
    def body(p, dq, dk, dv, dqnv, dcnv, dkrv, ca, sa, gq, gk):
        cols = []
        for j in range(4):
            nat = _blk(dq, 2 * j) + pltpu.roll(_blk(dq, 2 * j + 1), 64, 1)
            cols.append(_rope_t(nat, ca, sa, _rot_swa))
        grp = lambda d, g: sum(_blk(d, 4 * g + i) for i in range(4))
        cols.append(_rope_t(grp(dk, 0) + pltpu.roll(grp(dk, 1), 64, 1), ca, sa, _rot_swa))
        cols.append(grp(dv, 0) + pltpu.roll(grp(dv, 1), 64, 1))
        dql, dgq = _rms_bwd(p[:, C_QL:C_QL + MLA_Q_RANK], gq, dqnv, MLA_Q_RANK)
        dkl, dgk = _rms_bwd(p[:, C_KL:C_KL + MLA_KV_RANK], gk, dcnv, MLA_KV_RANK)
        cols += [dql, dkl, dkrv]
        return jnp.concatenate(cols, 1), _colsum(dgq), _colsum(dgk)
    return _rowmap("prep1_bwd", body, [proj, dqa, dka, dva, dqn, dcn, dkr, cosa, sina], [qn_g, kv_g],
                   [(IN_WP, BF16)], [MLA_Q_RANK, MLA_KV_RANK], tr_prefs=(192, 128))


def _sigmoid(x):
    return 1.0 / (1.0 + jnp.exp(-x))


def _swiglu_fwd(a, b):
    def body(av, bv):
        return (av * _sigmoid(av) * bv,)
    return _rowmap("swiglu_fwd", body, [a, b], [], [(D_FF, BF16)])[0]


def _swiglu_bwd(a, b, dh):
    def body(av, bv, d):
        s = _sigmoid(av)
        return d * bv * (s * (1.0 + av * (1.0 - s))), d * (av * s)
    return _rowmap("swiglu_bwd", body, [a, b, dh], [], [(D_FF, BF16), (D_FF, BF16)])


def _loss_head(h, target, g):
    T = h.shape[0]
    nb = T // BLOCK

    def kern(h_ref, t_ref, g_ref, dh_ref, dg_ref, loss_ref, acc):
        i = pl.program_id(0)

        @pl.when(i == 0)
        def _():
            dh_ref[...] = jnp.zeros_like(dh_ref)
            dg_ref[...] = jnp.zeros_like(dg_ref)
            acc[...] = jnp.zeros_like(acc)

        @pl.when(i > 0)
        def _():
            x = h_ref[...]
            gv = g_ref[...]
            e = x * _rms_r(x, D_MODEL) * gv - t_ref[...]
            acc[...] += _colsum(e * e)
            dx, dg = _rms_bwd(x, gv, e * (1.0 / D_MODEL), D_MODEL)
            dh_ref[...] = dx
            dg_ref[...] += _colsum(dg)

        @pl.when(i == nb - 1)
        def _():
            tot = jnp.sum(acc[...], axis=-1, keepdims=True) * (0.5 / D_MODEL)
            loss_ref[...] = jnp.broadcast_to(tot, loss_ref.shape)

    return pl.pallas_call(
        kern, name="loss_head",
        out_shape=[jax.ShapeDtypeStruct((T, D_MODEL), F32), jax.ShapeDtypeStruct((1, D_MODEL), F32),
                   jax.ShapeDtypeStruct((1, LANES), F32)],
        grid=(nb,),
        in_specs=[pl.BlockSpec((BLOCK, D_MODEL), lambda i: (i, 0)),
                  pl.BlockSpec((BLOCK, D_MODEL), lambda i: (jnp.maximum(i - 1, 0), 0)),
                  pl.BlockSpec((1, D_MODEL), lambda i: (0, 0))],
        out_specs=[pl.BlockSpec((BLOCK, D_MODEL), lambda i: (i, 0)),
                   pl.BlockSpec((1, D_MODEL), lambda i: (0, 0)),
                   pl.BlockSpec((1, LANES), lambda i: (0, 0))],
        scratch_shapes=[pltpu.VMEM((1, D_MODEL), F32)],
        compiler_params=_params(("arbitrary",)),
    )(h, target, g)


LOG2E = 1.4426950408889634


def _attn_plan(T, causal):
    tq = _tile(T, (384, 256, 128))
    ck = min(2 * tq, T) if causal else min(tq + WINDOW, T)
    return tq, ck, (-(-T // ck) if causal else 1)


def _chunk(i, c, T, tq, ck, causal):
    if causal:
        return pl.multiple_of(jnp.minimum(c * ck, T - ck), LANES), c * ck
    return pl.multiple_of(jnp.clip(i * tq - WINDOW, 0, T - ck), LANES), 0


def _n_chunks(i, tq, ck, causal):
    return ((i + 1) * tq + ck - 1) // ck if causal else 1


def _mask(s, i, start, first, tq, ck, causal):
    qpos = i * tq + lax.broadcasted_iota(jnp.int32, (tq, 1), 0)
    kpos = start + lax.broadcasted_iota(jnp.int32, (tq, ck), 1)
    low = jnp.maximum(jnp.where(qpos < FRONT, 0, FRONT), first)
    if not causal:
        low = jnp.maximum(low, qpos - (WINDOW - 1))
    return jnp.where(kpos >= low, jnp.where(kpos <= qpos, s, NEG), NEG)


def _chunk_loop(n, body, init, causal):
    carry = body(0, init, True)
    if not causal:
        return carry
    carry = lax.fori_loop(1, n - 1, lambda c, cr: body(c, cr, False), carry)
    return lax.cond(n > 1, lambda cr: body(n - 1, cr, True), lambda cr: cr, carry)


HEADS_PER_STEP = 2


def _head_cols(group):
    q_cols = lambda hh: slice(hh * LANES, (hh + 1) * LANES)
    if group == 1:
        return q_cols, q_cols, HEADS_PER_STEP * LANES
    assert group % HEADS_PER_STEP == 0
    return q_cols, (lambda hh: slice(0, LANES)), LANES


def _whole(T, width, index):
    return pl.BlockSpec((T, width), index, pipeline_mode=pl.Buffered(1))


def _attn_fwd(name, q, k, v, sinks, group, causal, scale):
    T = q.shape[0]
    H = q.shape[1] // LANES
    HP = HEADS_PER_STEP
    tq, ck, slots = _attn_plan(T, causal)
    nq = T // tq
    c2 = scale * LOG2E
    q_cols, k_cols, kw = _head_cols(group)

    def kern(sink_ref, q_ref, k_ref, v_ref, o_ref, lse_ref, s_scr):
        sink = [sink_ref[pl.program_id(0) * HP + hh] for hh in range(HP)]

        def q_tile(i, carry):
            rows = pl.ds(pl.multiple_of(i * tq, tq), tq)
            qq = [q_ref[rows, q_cols(hh)] for hh in range(HP)]
            n = _n_chunks(i, tq, ck, causal)

            def score(c, m, masked):
                start, first = _chunk(i, c, T, tq, ck, causal)
                out = []
                for hh in range(HP):
                    s = lax.dot_general(qq[hh], k_ref[pl.ds(start, ck), k_cols(hh)], _DIMS["nt"],
                                        preferred_element_type=F32)
                    if masked:
                        s = _mask(s, i, start, first, tq, ck, causal)
                    s_scr[hh, c] = s
                    out.append(jnp.maximum(m[hh], jnp.max(s, axis=-1, keepdims=True)))
                return tuple(out)

            m = _chunk_loop(n, score, tuple(jnp.full((tq, 1), sk * (1.0 / scale), F32) for sk in sink), causal)
            m2 = [mh * c2 for mh in m]

            def weigh(c, carry):
                start, _ = _chunk(i, c, T, tq, ck, causal)
                out = []
                for hh in range(HP):
                    l, acc = carry[hh]
                    p = jnp.exp2(s_scr[hh, c] * c2 - m2[hh])
                    acc = acc + jnp.dot(p.astype(MXU_DTYPE), v_ref[pl.ds(start, ck), k_cols(hh)],
                                        preferred_element_type=F32)
                    out.append((l + jnp.sum(p, axis=-1, keepdims=True), acc))
                return tuple(out)

            init = tuple((jnp.exp2(sink[hh] * LOG2E - m2[hh]), jnp.zeros((tq, LANES), F32)) for hh in range(HP))
            res = lax.fori_loop(0, n, weigh, init)
            for hh in range(HP):
                l, acc = res[hh]
                o_ref[rows, q_cols(hh)] = acc / l
                lse_ref[rows, q_cols(hh)] = jnp.broadcast_to(m[hh] * scale + jnp.log(l), (tq, LANES))
            return carry

        lax.fori_loop(0, nq, q_tile, 0)

    q_spec = _whole(T, HP * LANES, lambda g: (0, g))
    kv_spec = _whole(T, kw, (lambda g: (0, g)) if group == 1 else (lambda g: (0, (g * HP) // group)))
    return pl.pallas_call(
        kern, name=name,
        out_shape=[jax.ShapeDtypeStruct((T, H * LANES), F32)] * 2,
        grid=(H // HP,),
        in_specs=[pl.BlockSpec(memory_space=pltpu.SMEM), q_spec, kv_spec, kv_spec],
        out_specs=[q_spec, q_spec],
        scratch_shapes=[pltpu.VMEM((HP, slots, tq, ck), F32)],
        compiler_params=_params(("parallel",)),
    )(sinks, q, k, v)


def _attn_bwd(name, q, k, v, do, lse, delta, group, causal, scale):
    T = q.shape[0]
    H = q.shape[1] // LANES
    HP = HEADS_PER_STEP
    tq, ck, _ = _attn_plan(T, causal)
    nq = T // tq
    c2 = scale * LOG2E
    q_cols, k_cols, kw = _head_cols(group)

    def kern(q_ref, k_ref, v_ref, do_ref, lse_ref, dl_ref, dq_ref, dk_ref, dv_ref):
        dk_ref[...] = jnp.zeros_like(dk_ref)
        dv_ref[...] = jnp.zeros_like(dv_ref)

        def q_tile(i, carry):
            rows = pl.ds(pl.multiple_of(i * tq, tq), tq)
            qq = [q_ref[rows, q_cols(hh)] for hh in range(HP)]
            dd = [do_ref[rows, q_cols(hh)] for hh in range(HP)]
            lse2 = [lse_ref[rows, q_cols(hh)][:, 0:1] * LOG2E for hh in range(HP)]
            dl_c = [dl_ref[rows, q_cols(hh)][:, 0:1] for hh in range(HP)]

            def chunk(c, dq, masked):
                start, first = _chunk(i, c, T, tq, ck, causal)
                keys = pl.ds(start, ck)
                out = []
                for hh in range(HP):
                    kk, vv = k_ref[keys, k_cols(hh)], v_ref[keys, k_cols(hh)]
                    s = lax.dot_general(qq[hh], kk, _DIMS["nt"], preferred_element_type=F32)
                    if masked:
                        s = _mask(s, i, start, first, tq, ck, causal)
                    p = jnp.exp2(s * c2 - lse2[hh])
                    dv_ref[keys, q_cols(hh)] += lax.dot_general(p.astype(MXU_DTYPE), dd[hh], _DIMS["tn"],
                                                                preferred_element_type=F32)
                    dp = lax.dot_general(dd[hh], vv, _DIMS["nt"], preferred_element_type=F32)
                    ds = (p * (dp - dl_c[hh])).astype(MXU_DTYPE)
                    dk_ref[keys, q_cols(hh)] += lax.dot_general(ds, qq[hh], _DIMS["tn"],
                                                                preferred_element_type=F32) * scale
                    out.append(dq[hh] + jnp.dot(ds, kk, preferred_element_type=F32))
                return tuple(out)

            dq = _chunk_loop(_n_chunks(i, tq, ck, causal), chunk,
                             tuple(jnp.zeros((tq, LANES), F32) for _ in range(HP)), causal)
            for hh in range(HP):
                dq_ref[rows, q_cols(hh)] = dq[hh] * scale
            return carry

        lax.fori_loop(0, nq, q_tile, 0)

    q_spec = _whole(T, HP * LANES, lambda g: (0, g))
    kv_spec = _whole(T, kw, (lambda g: (0, g)) if group == 1 else (lambda g: (0, (g * HP) // group)))
    return pl.pallas_call(
        kern, name=name,
        out_shape=[jax.ShapeDtypeStruct((T, H * LANES), F32)] * 3,
        grid=(H // HP,),
        in_specs=[q_spec, kv_spec, kv_spec, q_spec, q_spec, q_spec],
        out_specs=[q_spec, q_spec, q_spec],
        compiler_params=_params(("parallel",)),
    )(q, k, v, do, lse, delta)


def _ew(name, fn, ins, out_dtypes):
    shape = ins[0].shape
    flat = [a.reshape(-1, shape[-1]) for a in ins]
    R, C = flat[0].shape
    tr = _tile(R, (512, 256, 128, 64, 32, 16, 8))
    n_in = len(ins)

    def kern(*refs):
        res = fn(*[r[...] for r in refs[:n_in]])
        for o_ref, val in zip(refs[n_in:], res):
            o_ref[...] = val.astype(o_ref.dtype)

    spec = pl.BlockSpec((tr, C), lambda i: (i, 0))
    outs = pl.pallas_call(
        kern, name=name,
        out_shape=[jax.ShapeDtypeStruct((R, C), dt) for dt in out_dtypes],
        grid=(R // tr,), in_specs=[spec] * n_in, out_specs=[spec] * len(out_dtypes),
        compiler_params=_params(("parallel",)),
    )(*flat)
    return [o.reshape(shape) for o in outs]


def _adamw(name, w, g, m, v):
    c1 = 1.0 - ADAM_B1 ** ADAM_STEP
    c2 = 1.0 - ADAM_B2 ** ADAM_STEP

    def fn(wv, gv, mv, vv):
        mn = ADAM_B1 * mv + (1.0 - ADAM_B1) * gv
        vn = ADAM_B2 * vv + (1.0 - ADAM_B2) * (gv * gv)
        delta = -ADAM_LR * ((mn / c1) / (jnp.sqrt(vn / c2) + ADAM_EPS) + ADAM_WD * wv)
        return delta, mn, vn

    return _ew(name, fn, [w, g, m, v], [F32, F32, F32])


_ANY = pl.BlockSpec(memory_space=pl.ANY)


def _where_am_i():
    x, y, c = lax.axis_index("x"), lax.axis_index("y"), lax.axis_index("c")
    chips = [(1 - x, y), (x, 1 - y), (1 - x, 1 - y)]
    return x, y, c, chips


def _gather_weights(shards, meta):
    arrs = list(shards) + [meta]
    n = len(arrs)
    per = [2] * len(shards) + [8]

    def body(*refs):
        ins, outs = refs[:n], refs[n:2 * n]
        send1, recv1, send2, recv2 = refs[2 * n:]
        x, y, c, chips = _where_am_i()
        me = 2 * x + y

        def half(ref, k, cc):
            return ref.at[pl.ds(per[k] * cc, per[k])]

        first = []
        for k in range(n):
            for j, (cx, cy) in enumerate(chips):
                first.append(pltpu.make_async_remote_copy(
                    src_ref=half(ins[k], k, c), dst_ref=half(outs[k].at[me], k, c),
                    send_sem=send1.at[k, j], recv_sem=recv1.at[k, j],
                    device_id=(cx, cy, c), device_id_type=MESH))
        for cp in first:
            cp.start()
        passed = []
        for k in range(n):
            for j, (cx, cy) in enumerate(chips):
                landed = half(outs[k].at[2 * cx + cy], k, c)
                pltpu.make_async_remote_copy(
                    src_ref=landed, dst_ref=landed, send_sem=send1.at[k, j], recv_sem=recv1.at[k, j],
                    device_id=(cx, cy, c), device_id_type=MESH).wait_recv()
                fwd = pltpu.make_async_remote_copy(
                    src_ref=landed, dst_ref=landed, send_sem=send2.at[k, j], recv_sem=recv2.at[k, j],
                    device_id=(x, y, 1 - c), device_id_type=MESH)
                fwd.start()
                passed.append(fwd)
        for k in range(n):
            for j, (cx, cy) in enumerate(chips):
                other = half(outs[k].at[2 * cx + cy], k, 1 - c)
                pltpu.make_async_remote_copy(
                    src_ref=other, dst_ref=other, send_sem=send2.at[k, j], recv_sem=recv2.at[k, j],
                    device_id=(x, y, 1 - c), device_id_type=MESH).wait_recv()
        for cp in first + passed:
            cp.wait_send()

    return pl.pallas_call(
        body, name="gather_weights",
        out_shape=[jax.ShapeDtypeStruct((4,) + a.shape, a.dtype) for a in arrs],
        in_specs=[_ANY] * n, out_specs=[_ANY] * n,
        scratch_shapes=[pltpu.SemaphoreType.DMA((n, 3))] * 4,
    )(*arrs)


def _row_chunks(r):
    n = 4 if r % 64 == 0 else 1
    return [(q * (r // n), r // n) for q in range(n)]


def _pair_exchange(grads):
    n = len(grads)

    def body(*refs):
        ins, got = refs[:n], refs[n:2 * n]
        send, recv = refs[2 * n:]
        x, y, c, _ = _where_am_i()
        sib = (x, y, 1 - c)
        for k in range(n):
            for ch in range(4):
                for l in range(2):
                    pltpu.make_async_remote_copy(
                        src_ref=ins[k].at[ch, 2 * (1 - c) + l], dst_ref=got[k].at[ch, l],
                        send_sem=send.at[k], recv_sem=recv.at[k], device_id=sib, device_id_type=MESH).start()
        for k in range(n):
            pltpu.make_async_remote_copy(
                src_ref=got[k], dst_ref=got[k], send_sem=send.at[k], recv_sem=recv.at[k],
                device_id=sib, device_id_type=MESH).wait()

    return pl.pallas_call(
        body, name="reduce_pair",
        out_shape=[jax.ShapeDtypeStruct((4, 2) + g.shape[2:], g.dtype) for g in grads],
        in_specs=[_ANY] * n, out_specs=[_ANY] * n,
        scratch_shapes=[pltpu.SemaphoreType.DMA((n,))] * 2,
    )(*grads)


def _pair_add(full, got, c):
    _, _, r, cols = full.shape
    tr = _tile(r, (512, 256, 352, 128))

    def kern(c_ref, a_ref, b_ref, o32_ref, o16_ref):
        tot = a_ref[...] + b_ref[...].astype(F32)
        o32_ref[...] = tot
        o16_ref[...] = tot.astype(o16_ref.dtype)

    blk = (None, None, tr, cols)
    mine = pl.BlockSpec(blk, lambda ch, l, i, cr: (ch, 2 * cr[0] + l, i, 0))
    same = pl.BlockSpec(blk, lambda ch, l, i, cr: (ch, l, i, 0))
    return pl.pallas_call(
        kern, name="pair_add",
        out_shape=[jax.ShapeDtypeStruct(got.shape, F32), jax.ShapeDtypeStruct(got.shape, got.dtype)],
        grid_spec=pltpu.PrefetchScalarGridSpec(
            num_scalar_prefetch=1, grid=(4, 2, r // tr), in_specs=[mine, same], out_specs=[same, same]),
        compiler_params=_params(("parallel", "parallel", "parallel")),
    )(c.reshape(1), full, got)


def _chip_scatter(parts):
    n = len(parts)

    def body(*refs):
        ins, outs = refs[:n], refs[n:2 * n]
        send, recv = refs[2 * n:]
        x, y, c, chips = _where_am_i()
        me = 2 * x + y
        for k in range(n):
            for j, (cx, cy) in enumerate(chips):
                for l in range(2):
                    pltpu.make_async_remote_copy(
                        src_ref=ins[k].at[2 * cx + cy, l], dst_ref=outs[k].at[me, l],
                        send_sem=send.at[k, j], recv_sem=recv.at[k, j],
                        device_id=(cx, cy, c), device_id_type=MESH).start()
        for k in range(n):
            for j, (cx, cy) in enumerate(chips):
                slot = outs[k].at[2 * cx + cy]
                pltpu.make_async_remote_copy(
                    src_ref=slot, dst_ref=slot, send_sem=send.at[k, j], recv_sem=recv.at[k, j],
                    device_id=(cx, cy, c), device_id_type=MESH).wait()

    return pl.pallas_call(
        body, name="reduce_chips",
        out_shape=[jax.ShapeDtypeStruct(p.shape, p.dtype) for p in parts],
        in_specs=[_ANY] * n, out_specs=[_ANY] * n,
        scratch_shapes=[pltpu.SemaphoreType.DMA((n, 3))] * 2,
    )(*parts)


def _chip_add(landed, mine, me, c):
    _, _, r, cols = landed.shape
    tr = _tile(r, (512, 256, 352, 128))

    def kern(me_ref, c_ref, land_ref, own_ref, o_ref):
        own = own_ref[...]
        tot = None
        for j in range(4):
            term = jnp.where(me_ref[0] == j, own, land_ref[j].astype(F32))
            tot = term if tot is None else tot + term
        o_ref[...] = tot

    return pl.pallas_call(
        kern, name="chip_add",
        out_shape=jax.ShapeDtypeStruct((4, r, cols), F32),
        grid_spec=pltpu.PrefetchScalarGridSpec(
            num_scalar_prefetch=2, grid=(2, r // tr),
            in_specs=[pl.BlockSpec((4, None, tr, cols), lambda l, i, mr, cr: (0, l, i, 0)),
                      pl.BlockSpec((None, None, tr, cols), lambda l, i, mr, cr: (mr[0], l, i, 0))],
            out_specs=pl.BlockSpec((None, tr, cols), lambda l, i, mr, cr: (2 * cr[0] + l, i, 0))),
        compiler_params=_params(("parallel", "parallel")),
    )(me.reshape(1), c.reshape(1), landed, mine)


def _pair_join(sums):
    n = len(sums)

    def body(*refs):
        bufs = refs[n:2 * n]
        send, recv = refs[2 * n:]
        x, y, c, _ = _where_am_i()
        sib = (x, y, 1 - c)
        for k in range(n):
            for l in range(2):
                for r0, rn in _row_chunks(bufs[k].shape[1]):
                    piece = bufs[k].at[2 * c + l, pl.ds(r0, rn)]
                    pltpu.make_async_remote_copy(
                        src_ref=piece, dst_ref=piece, send_sem=send.at[k], recv_sem=recv.at[k],
                        device_id=sib, device_id_type=MESH).start()
        for k in range(n):
            theirs = bufs[k].at[pl.ds(2 * (1 - c), 2)]
            pltpu.make_async_remote_copy(
                src_ref=theirs, dst_ref=theirs, send_sem=send.at[k], recv_sem=recv.at[k],
                device_id=sib, device_id_type=MESH).wait()

    return pl.pallas_call(
        body, name="reduce_join",
        out_shape=[jax.ShapeDtypeStruct(s.shape, s.dtype) for s in sums],
        in_specs=[_ANY] * n, out_specs=[_ANY] * n,
        input_output_aliases={k: k for k in range(n)},
        scratch_shapes=[pltpu.SemaphoreType.DMA((n,))] * 2,
    )(*sums)


def _allreduce_small(buf):
    R = buf.shape[0]

    def body(in_ref, out_ref, land, send, recv):
        x, y, c, _ = _where_am_i()
        me = 4 * x + 2 * y + c
        land[me] = in_ref[...]
        cps = []
        for k in range(1, 8):
            px, py, pc = x ^ (k >> 2), y ^ ((k >> 1) & 1), c ^ (k & 1)
            cps.append(pltpu.make_async_remote_copy(
                src_ref=in_ref, dst_ref=land.at[me], send_sem=send.at[k - 1], recv_sem=recv.at[k - 1],
                device_id=(px, py, pc), device_id_type=MESH))
        for cp in cps:
            cp.start()
        for k in range(1, 8):
            px, py, pc = x ^ (k >> 2), y ^ ((k >> 1) & 1), c ^ (k & 1)
            slot = land.at[4 * px + 2 * py + pc]
            pltpu.make_async_remote_copy(
                src_ref=slot, dst_ref=slot, send_sem=send.at[k - 1], recv_sem=recv.at[k - 1],
                device_id=(px, py, pc), device_id_type=MESH).wait_recv()
        for cp in cps:
            cp.wait_send()
        tot = land[0]
        for d in range(1, 8):
            tot = tot + land[d]
        out_ref[...] = tot

    vm = pl.BlockSpec(memory_space=pltpu.VMEM)
    return pl.pallas_call(
        body, name="allreduce_small",
        out_shape=jax.ShapeDtypeStruct(buf.shape, F32),
        in_specs=[vm], out_specs=vm,
        scratch_shapes=[pltpu.VMEM((8, R, LANES), F32), pltpu.SemaphoreType.DMA((7,)),
                        pltpu.SemaphoreType.DMA((7,))],
    )(buf)


def _rope_tables(T):
    pos = (jnp.arange(T) - FRONT).astype(F32)
    lane = jnp.arange(LANES)
    inv_a = ROPE_THETA ** (-(2 * ((lane % 64) % 32)).astype(F32) / SWA_HEAD_DIM)
    ang_a = pos[:, None] * inv_a[None, :]
    cosa, sina = jnp.cos(ang_a), jnp.sin(ang_a)
    inv_m = ROPE_THETA ** (-(2 * ((lane - KR_LANE) % 16)).astype(F32) / MLA_ROPE_DIM)
    ang_m = pos[:, None] * inv_m[None, :]
    on = ((lane >= KR_LANE) & (lane < KR_LANE + MLA_ROPE_DIM))[None, :]
    cosm = jnp.where(on, jnp.cos(ang_m), 1.0)
    sinm = jnp.where(on, jnp.sin(ang_m), 0.0)
    return cosa, sina, cosm, sinm


def _cols_from_chips(g):
    return jnp.concatenate([g[j] for j in range(4)], axis=-1)


def _rows_from_chips(g):
    return jnp.concatenate([g[j] for j in range(4)], axis=-2)


def _cols_to_chips(w):
    L, r, c4 = w.shape
    return jnp.moveaxis(w.reshape(L, r, 4, c4 // 4), 2, 0)


def _rows_to_chips(w):
    L, r4, c = w.shape
    return jnp.moveaxis(w.reshape(L, 4, r4 // 4, c), 1, 0)


def _local_step(x2, target, meta_full, natural, p):
    T = BLOCK + x2.shape[0]
    L = DEPTH
    W_in, W_qup, W_kvup, W_o, W_gate, W_up, W_down = natural
    zpad = lambda n: jnp.zeros((L, D_MODEL, n), W_in.dtype)
    W_in = jnp.concatenate([W_in[..., :C_KR], zpad(KR_LANE), W_in[..., C_KR:IN_W],
                            zpad(LANES - KR_LANE - MLA_ROPE_DIM)], axis=-1)
    W_qup = W_qup.reshape(L, MLA_Q_RANK, MLA_HEADS, MLA_QK_DIM)
    W_qup = jnp.pad(W_qup, ((0, 0), (0, 0), (0, 0), (0, LANES - MLA_QK_DIM))).reshape(L, MLA_Q_RANK, SLOT_W)
    attn_norm, q_norm, kv_norm, sinks = p["attn_norm"], p["q_norm"], p["kv_norm"], p["sinks"]
    out_norm_swa, out_norm_mla, ffn_norm, final_norm = (
        p["out_norm_swa"], p["out_norm_mla"], p["ffn_norm"], p["final_norm"])

    cosa, sina, cosm, sinm = _rope_tables(T)
    no_sink = jnp.full((MLA_HEADS,), NEG, F32)
    scale_a, scale_b = SWA_HEAD_DIM ** -0.5, MLA_QK_DIM ** -0.5
    row = lambda v: v.reshape(1, -1)

    h = jnp.concatenate([jnp.zeros((FRONT, D_MODEL), F32), meta_full, x2], axis=0)
    saved = []
    for l in range(L):
        u = _rmsnorm_fwd("attn_norm", h, row(attn_norm[l]))
        proj = _mm("in_proj", u, W_in[l], "nn")
        qa, ka, va, qn, cn, kr = _prep1(proj, row(q_norm[l]), row(kv_norm[l]), cosa, sina, cosm, sinm)
        qb = _mm("q_up", qn, W_qup[l], "nn")
        kvb = _mm("kv_up", cn, W_kvup[l], "nn")
        qs, ks, vs = _prep2(qb, kvb, kr, cosm, sinm)
        oa, lse_a = _attn_fwd("swa_fwd", qa, ka, va, sinks[l], 4, False, scale_a)
        ob, lse_b = _attn_fwd("mla_fwd", qs, ks, vs, no_sink, 1, True, scale_b)
        mix = _merge_fwd(oa, ob, row(out_norm_swa[l]), row(out_norm_mla[l]))
        h1 = _mm("o_proj", mix, W_o[l], "nn", res=h)
        u2 = _rmsnorm_fwd("ffn_norm", h1, row(ffn_norm[l]))
        a = _mm("gate_proj", u2, W_gate[l], "nn")
        b = _mm("up_proj", u2, W_up[l], "nn")
        hm = _swiglu_fwd(a, b)
        h2 = _mm("down_proj", hm, W_down[l], "nn", res=h1)
        saved.append((h, u, proj, qa, ka, va, qn, cn, qs, ks, vs, oa, lse_a, ob, lse_b, mix, h1, u2, a, b, hm))
        h = h2

    dh, d_final, loss_row = _loss_head(h, target, row(final_norm))

    gw = {k: [None] * L for k in ("in", "qup", "kvup", "o", "gate", "up", "down")}
    gs = {k: [None] * L for k in ("attn", "qn", "kvn", "sink", "ga", "gb", "ffn")}
    for l in reversed(range(L)):
        (h0, u, proj, qa, ka, va, qn, cn, qs, ks, vs, oa, lse_a, ob, lse_b, mix, h1, u2, a, b, hm) = saved[l]
        gw["down"][l] = _mm("down_dw", hm, dh, "tn")
        dhm = _mm("down_dx", dh, W_down[l], "nt")
        da, db = _swiglu_bwd(a, b, dhm)
        gw["gate"][l] = _mm("gate_dw", u2, da, "tn")
        gw["up"][l] = _mm("up_dw", u2, db, "tn")
        du2 = _mm("gate_dx", da, W_gate[l], "nt")
        du2 = _mm("up_dx", db, W_up[l], "nt", res=du2)
        dh1, gs["ffn"][l] = _rmsnorm_bwd("ffn_norm_bwd", h1, row(ffn_norm[l]), du2, dh)
        gw["o"][l] = _mm("o_dw", mix, dh1, "tn")
        dmix = _mm("o_dx", dh1, W_o[l], "nt")
        sink_slots = jnp.repeat(sinks[l], LANES).reshape(1, SLOT_W)
        doa, dla, dob, dlb, gs["ga"][l], gs["gb"][l], dsink = _merge_bwd(
            dmix, oa, ob, lse_a, row(out_norm_swa[l]), row(out_norm_mla[l]), sink_slots)
        gs["sink"][l] = dsink.reshape(SWA_HEADS, LANES)[:, 0]
        dqs, dks, dvs = _attn_bwd("mla_bwd", qs, ks, vs, dob, lse_b, dlb, 1, True, scale_b)
        dqa, dka, dva = _attn_bwd("swa_bwd", qa, ka, va, doa, lse_a, dla, 4, False, scale_a)
        dqb, dkvb, dkr = _prep2_bwd(dqs, dks, dvs, cosm, sinm)
        gw["qup"][l] = _mm("q_up_dw", qn, dqb, "tn")
        gw["kvup"][l] = _mm("kv_up_dw", cn, dkvb, "tn")
        dqn = _mm("q_up_dx", dqb, W_qup[l], "nt")
        dcn = _mm("kv_up_dx", dkvb, W_kvup[l], "nt")
        dproj, gs["qn"][l], gs["kvn"][l] = _prep1_bwd(
            proj, dqa, dka, dva, dqn, dcn, dkr, cosa, sina, row(q_norm[l]), row(kv_norm[l]))
        gw["in"][l] = _mm("in_dw", u, dproj, "tn")
        du = _mm("in_dx", dproj, W_in[l], "nt")
        dh, gs["attn"][l] = _rmsnorm_bwd("attn_norm_bwd", h0, row(attn_norm[l]), du, dh1)

    st = lambda k: jnp.stack(gw[k])
    d_in = st("in")
    d_in = jnp.concatenate([d_in[..., :C_KR], d_in[..., C_KR + KR_LANE:C_KR + KR_LANE + MLA_ROPE_DIM]], axis=-1)
    d_qup = st("qup").reshape(L, MLA_Q_RANK, MLA_HEADS, LANES)[..., :MLA_QK_DIM].reshape(L, MLA_Q_RANK, -1)
    d_nat = [d_in, d_qup, st("kvup"), st("o"), st("gate"), st("up"), st("down")]
    return loss_row, dh, d_nat, gs, d_final


def kernel(x, meta_tokens, attn_norm, w_in, q_norm, w_q_up, kv_norm, w_kv_up, sinks, out_norm_swa, out_norm_mla, w_o, ffn_norm, w_gate, w_up, w_down, final_norm, loss_target, m_meta_tokens, m_attn_norm, m_w_in, m_q_norm, m_w_q_up, m_kv_norm, m_w_kv_up, m_sinks, m_out_norm_swa, m_out_norm_mla, m_w_o, m_ffn_norm, m_w_gate, m_w_up, m_w_down, m_final_norm, v_meta_tokens, v_attn_norm, v_w_in, v_q_norm, v_w_q_up, v_kv_norm, v_w_kv_up, v_sinks, v_out_norm_swa, v_out_norm_mla, v_w_o, v_ffn_norm, v_w_gate, v_w_up, v_w_down, v_final_norm):
    assert x.shape[0] == 1 and x.shape[1] % BLOCK == 0
    big = [w_in, w_q_up, w_kv_up, w_o, w_gate, w_up, w_down]

    c_idx = lax.axis_index("c").astype(jnp.int32)
    chip = (2 * lax.axis_index("x") + lax.axis_index("y")).astype(jnp.int32)
    mine = [w.astype(BF16) for w in big] + [meta_tokens]
    gathered = _gather_weights(mine[:-1], mine[-1])
    parts = [[jnp.where(chip == j, own, g[j]) for j in range(4)] for own, g in zip(mine, gathered)]
    join = [_cols_from_chips, _cols_from_chips, _cols_from_chips, _rows_from_chips, _cols_from_chips,
            _cols_from_chips, _rows_from_chips]
    natural = [f(p) for f, p in zip(join, parts[:-1])]
    meta_full = jnp.concatenate(parts[-1], axis=-1)
    small_p = dict(attn_norm=attn_norm, q_norm=q_norm, kv_norm=kv_norm, sinks=sinks, out_norm_swa=out_norm_swa,
                   out_norm_mla=out_norm_mla, ffn_norm=ffn_norm, final_norm=final_norm)
    loss_row, dh, d_nat, gs, d_final = _local_step(x[0], loss_target[0], meta_full, natural, small_p)
    grad_x = dh[BLOCK:][None]

    split = [_cols_to_chips, _cols_to_chips, _cols_to_chips, _rows_to_chips, _cols_to_chips, _cols_to_chips,
             _rows_to_chips]
    full = [f(d) for f, d in zip(split, d_nat)]

    got = _pair_exchange([f.astype(BF16) for f in full])
    sums = [_pair_add(f, g, c_idx) for f, g in zip(full, got)]
    landed = _chip_scatter([s16 for _, s16 in sums])
    g_big = _pair_join([_chip_add(t, s32, chip, c_idx) for t, (s32, _) in zip(landed, sums)])

    small = [jnp.stack(gs["attn"]).reshape(-1), jnp.stack(gs["qn"]).reshape(-1), jnp.stack(gs["kvn"]).reshape(-1),
             jnp.stack(gs["sink"]).reshape(-1), jnp.stack(gs["ga"]).reshape(-1), jnp.stack(gs["gb"]).reshape(-1),
             jnp.stack(gs["ffn"]).reshape(-1), d_final.reshape(-1)]
    sizes = [s.shape[0] for s in small]
    flat = jnp.concatenate(small + [dh[FRONT:BLOCK].reshape(-1), loss_row[0, :1]])
    n_flat = flat.shape[0]
    rows_needed = -(-n_flat // (8 * LANES)) * 8
    flat = jnp.pad(flat, (0, rows_needed * LANES - n_flat)).reshape(rows_needed, LANES)
    tot = _allreduce_small(flat).reshape(-1)
    n_small = sum(sizes)
    loss = tot[n_small + N_META * D_MODEL]
    g_meta_full = tot[n_small:n_small + N_META * D_MODEL].reshape(N_META, D_MODEL)
    g_meta_mine = lax.dynamic_slice_in_dim(g_meta_full, chip * (D_MODEL // 4), D_MODEL // 4, axis=1)

    small_w = [attn_norm, q_norm, kv_norm, sinks, out_norm_swa, out_norm_mla, ffn_norm, final_norm]
    small_m = [m_attn_norm, m_q_norm, m_kv_norm, m_sinks, m_out_norm_swa, m_out_norm_mla, m_ffn_norm, m_final_norm]
    small_v = [v_attn_norm, v_q_norm, v_kv_norm, v_sinks, v_out_norm_swa, v_out_norm_mla, v_ffn_norm, v_final_norm]
    n_rows = -(-n_small // (8 * LANES)) * 8

    def pack(arrs):
        f = jnp.concatenate([a.reshape(-1) for a in arrs])
        return jnp.pad(f, (0, n_rows * LANES - n_small), constant_values=1.0).reshape(n_rows, LANES)

    g_small_pack = jnp.pad(tot[:n_small], (0, n_rows * LANES - n_small)).reshape(n_rows, LANES)
    upd_small = _adamw("adam_small", pack(small_w), g_small_pack, pack(small_m), pack(small_v))

    def unpack(p):
        f = p.reshape(-1)
        out, off = [], 0
        for a, n in zip(small_w, sizes):
            out.append(f[off:off + n].reshape(a.shape))
            off += n
        return out

    g_small = unpack(g_small_pack)
    d_small, m_small, v_small = [unpack(p) for p in upd_small]
    d_meta, nm_meta, nv_meta = _adamw("adam_meta", meta_tokens, g_meta_mine, m_meta_tokens, v_meta_tokens)

    big_m = [m_w_in, m_w_q_up, m_w_kv_up, m_w_o, m_w_gate, m_w_up, m_w_down]
    big_v = [v_w_in, v_w_q_up, v_w_kv_up, v_w_o, v_w_gate, v_w_up, v_w_down]
    upd_big = [_adamw("adam_big", w, g, m, v) for w, g, m, v in zip(big, g_big, big_m, big_v)]

    names = ["meta_tokens", "attn_norm", "w_in", "q_norm", "w_q_up", "kv_norm", "w_kv_up", "sinks",
             "out_norm_swa", "out_norm_mla", "w_o", "ffn_norm", "w_gate", "w_up", "w_down", "final_norm"]
    small_idx = {"attn_norm": 0, "q_norm": 1, "kv_norm": 2, "sinks": 3, "out_norm_swa": 4,
                 "out_norm_mla": 5, "ffn_norm": 6, "final_norm": 7}
    big_idx = {"w_in": 0, "w_q_up": 1, "w_kv_up": 2, "w_o": 3, "w_gate": 4, "w_up": 5, "w_down": 6}
    grads, deltas, new_m, new_v = [], [], [], []
    for nme in names:
        if nme == "meta_tokens":
            quad = (g_meta_mine, d_meta, nm_meta, nv_meta)
        elif nme in small_idx:
            i = small_idx[nme]
            quad = (g_small[i], d_small[i], m_small[i], v_small[i])
        else:
            i = big_idx[nme]
            quad = (g_big[i], *upd_big[i])
        grads.append(quad[0]); deltas.append(quad[1]); new_m.append(quad[2]); new_v.append(quad[3])
    return (loss, grad_x, *grads, *deltas, *new_m, *new_v)
```

```python
import jax
import jax.numpy as jnp
from jax import lax
from jax.experimental import pallas as pl
from jax.experimental.pallas import tpu as pltpu

F32 = jnp.float32
BF16 = jnp.bfloat16
MXU_DTYPE = BF16

D_MODEL = 1024
DEPTH = 4
N_META = 16
BLOCK = 128
WINDOW = 128
ROPE_THETA = 10000.0
EPS = 1e-6
NEG = -1e30
SWA_HEADS = 8
SWA_KV_HEADS = 2
SWA_HEAD_DIM = 64
MLA_HEADS = 8
MLA_Q_RANK = 256
MLA_KV_RANK = 128
MLA_NOPE_DIM = 64
MLA_ROPE_DIM = 32
MLA_V_DIM = 64
MLA_QK_DIM = MLA_NOPE_DIM + MLA_ROPE_DIM
D_FF = 2816
FRONT = (-N_META) % BLOCK
LANES = 128
SLOT_W = 8 * LANES
C_QA, C_KA, C_VA, C_QL, C_KL, C_KR, IN_WP = 0, 512, 640, 768, 1024, 1152, 1280
KR_LANE = 64
IN_W = 1184

ADAM_LR, ADAM_B1, ADAM_B2, ADAM_EPS, ADAM_WD, ADAM_STEP = 0.001, 0.9, 0.999, 1e-08, 0.01, 10

VMEM_LIMIT = 48 * 1024 * 1024
MESH = pl.DeviceIdType.MESH


def _tile(n, prefs):
    for t in prefs:
        if n % t == 0:
            return t
    return n


def _params(sem):
    return pltpu.CompilerParams(dimension_semantics=sem, vmem_limit_bytes=VMEM_LIMIT)


_DIMS = {"nn": (((1,), (0,)), ((), ())), "nt": (((1,), (1,)), ((), ())), "tn": (((0,), (0,)), ((), ()))}


def _mm(name, a, b, mode, out_dtype=F32, res=None):
    if mode == "nn":
        (M, K), (_, N) = a.shape, b.shape
    elif mode == "nt":
        (M, K), (N, _) = a.shape, b.shape
    else:
        (K, M), (_, N) = a.shape, b.shape
    lane_tiles = (1408, 1024, 640, 768, 512, 384, 256, 128)
    row_tiles = (528, 512, 384, 256, 128)
    bm = _tile(M, lane_tiles if mode == "tn" else row_tiles)
    bn = _tile(N, lane_tiles)
    bk = _tile(K, (1056,) + row_tiles if mode == "tn" else lane_tiles)
    nk = K // bk
    if mode == "tn":
        a_spec = pl.BlockSpec((bk, bm), lambda i, j, k: (k, i))
    else:
        a_spec = pl.BlockSpec((bm, bk), lambda i, j, k: (i, k))
    if mode == "nt":
        b_spec = pl.BlockSpec((bn, bk), lambda i, j, k: (j, k))
    else:
        b_spec = pl.BlockSpec((bk, bn), lambda i, j, k: (k, j))
    o_spec = pl.BlockSpec((bm, bn), lambda i, j, k: (i, j))
    in_specs = [a_spec, b_spec]
    args = [a, b]
    if res is not None:
        in_specs.append(o_spec)
        args.append(res)
    dims = _DIMS[mode]

    def kern(a_ref, b_ref, *rest):
        if res is not None:
            r_ref, o_ref = rest[0], rest[1]
            scr = rest[2:]
        else:
            r_ref, o_ref = None, rest[0]
            scr = rest[1:]
        p = lax.dot_general(a_ref[...].astype(MXU_DTYPE), b_ref[...].astype(MXU_DTYPE), dims,
                            preferred_element_type=F32)

        def finish(val):
            if r_ref is not None:
                val = val + r_ref[...]
            o_ref[...] = val.astype(o_ref.dtype)

        if nk == 1:
            finish(p)
        else:
            acc = scr[0]
            k = pl.program_id(2)

            @pl.when(k == 0)
            def _():
                acc[...] = p

            @pl.when(k > 0)
            def _():
                acc[...] += p

            @pl.when(k == nk - 1)
            def _():
                finish(acc[...])

    return pl.pallas_call(
        kern, name=name,
        out_shape=jax.ShapeDtypeStruct((M, N), out_dtype),
        grid=(M // bm, N // bn, nk),
        in_specs=in_specs, out_specs=o_spec,
        scratch_shapes=[pltpu.VMEM((bm, bn), F32)] if nk > 1 else [],
        compiler_params=_params(("parallel", "parallel", "arbitrary")),
    )(*args)


CHIPS = 4


def _mmc(name, mode, a, a_blk, a_idx, b, b_blk, b_idx, out_shape, o_blk, o_idx, nm, nk, sum_chips,
         res=None, acc_into=None):
    n_red = (CHIPS if sum_chips else 1) * nk
    jk = (lambda jp, r: (r // nk, r % nk)) if sum_chips else (lambda jp, r: (jp, r))
    wrap = lambda idx: (lambda i, jp, r: idx(i, *jk(jp, r)))
    o_map = (lambda i, jp, r: o_idx(i, 0, 0)) if sum_chips else (lambda i, jp, r: o_idx(i, jp, 0))
    in_specs = [pl.BlockSpec(a_blk, wrap(a_idx)), pl.BlockSpec(b_blk, wrap(b_idx))]
    args = [a, b]
    o_spec = pl.BlockSpec(o_blk, o_map)
    if res is not None:
        in_specs.append(o_spec)
        args.append(res)
    aliases = {}
    if acc_into is not None:
        aliases = {len(args): 0}
        in_specs.append(pl.BlockSpec(memory_space=pl.ANY))
        args.append(acc_into)
    dims = _DIMS[mode]
    acc_shape = tuple(d for d in o_blk if d is not None)

    def kern(a_ref, b_ref, *rest):
        rest = list(rest)
        r_ref = rest.pop(0) if res is not None else None
        if acc_into is not None:
            rest.pop(0)
        o_ref = rest.pop(0)
        p = lax.dot_general(a_ref[...].astype(MXU_DTYPE), b_ref[...].astype(MXU_DTYPE), dims,
                            preferred_element_type=F32)

        def finish(val):
            if r_ref is not None:
                val = val + r_ref[...]
            o_ref[...] = val.astype(o_ref.dtype)

        if n_red == 1:
            finish(p)
        else:
            acc = rest[0]
            r = pl.program_id(2)

            @pl.when(r == 0)
            def _():
                acc[...] = p

            @pl.when(r > 0)
            def _():
                acc[...] += p

            @pl.when(r == n_red - 1)
            def _():
                finish(acc[...])

    return pl.pallas_call(
        kern, name=name, out_shape=out_shape,
        grid=(nm, 1 if sum_chips else CHIPS, n_red),
        in_specs=in_specs, out_specs=o_spec,
        scratch_shapes=[pltpu.VMEM(acc_shape, F32)] if n_red > 1 else [],
        input_output_aliases=aliases,
        compiler_params=_params(("parallel", "parallel", "arbitrary")),
    )(*args)


def _row_tiling(T):
    bm = _tile(T, (1056, 768, 512, 384, 256, 128))
    return bm, T // bm


def _mm_fan(name, a, w, l, out_dtype=F32):
    (T, K), c = a.shape, w.shape[-1]
    bm, nm = _row_tiling(T)
    return _mmc(name, "nn", a, (bm, K), lambda i, j, k: (i, 0), w, (None, None, K, c), lambda i, j, k: (j, l, 0, 0),
                jax.ShapeDtypeStruct((CHIPS, T, c), out_dtype), (None, bm, c), lambda i, j, k: (j, i, 0), nm, 1, False)


def _mm_fan_t(name, g, w, l, res=None):
    (_, T, c), K = g.shape, w.shape[-2]
    bm, nm = _row_tiling(T)
    return _mmc(name, "nt", g, (None, bm, c), lambda i, j, k: (j, i, 0), w, (None, None, K, c),
                lambda i, j, k: (j, l, 0, 0), jax.ShapeDtypeStruct((T, K), F32), (bm, K), lambda i, j, k: (i, 0),
                nm, 1, True, res=res)


def _mm_fan_dw(name, a, g, into, l):
    (T, K), c = a.shape, g.shape[-1]
    bk = _tile(T, (1056, 528, 512, 384, 256, 128))
    return _mmc(name, "tn", a, (bk, K), lambda i, j, k: (k, 0), g, (None, bk, c), lambda i, j, k: (j, k, 0),
                jax.ShapeDtypeStruct(into.shape, F32), (None, None, K, c), lambda i, j, k: (j, l, 0, 0),
                1, T // bk, False, acc_into=into)


def _mm_join(name, a, w, l, res=None):
    c, N = w.shape[-2:]
    if a.ndim == 3:
        T = a.shape[1]
        bm, nm = _row_tiling(T)
        a_blk, a_idx = (None, bm, c), (lambda i, j, k: (j, i, 0))
    else:
        T = a.shape[0]
        bm, nm = _row_tiling(T)
        a_blk, a_idx = (bm, c), (lambda i, j, k: (i, j))
    return _mmc(name, "nn", a, a_blk, a_idx, w, (None, None, c, N), lambda i, j, k: (j, l, 0, 0),
                jax.ShapeDtypeStruct((T, N), F32), (bm, N), lambda i, j, k: (i, 0), nm, 1, True, res=res)


def _mm_join_t(name, g, w, l, split):
    (T, N), c = g.shape, w.shape[-2]
    bm, nm = _row_tiling(T)
    if split:
        shape, o_blk, o_idx = (CHIPS, T, c), (None, bm, c), (lambda i, j, k: (j, i, 0))
    else:
        shape, o_blk, o_idx = (T, CHIPS * c), (bm, c), (lambda i, j, k: (i, j))
    return _mmc(name, "nt", g, (bm, N), lambda i, j, k: (i, 0), w, (None, None, c, N), lambda i, j, k: (j, l, 0, 0),
                jax.ShapeDtypeStruct(shape, F32), o_blk, o_idx, nm, 1, False)


def _mm_join_dw(name, a, g, into, l):
    c, N = into.shape[-2:]
    T = g.shape[0]
    bk = _tile(T, (1056, 528, 512, 384, 256, 128))
    if a.ndim == 3:
        a_blk, a_idx = (None, bk, c), (lambda i, j, k: (j, k, 0))
    else:
        a_blk, a_idx = (bk, c), (lambda i, j, k: (k, j))
    return _mmc(name, "tn", a, a_blk, a_idx, g, (bk, N), lambda i, j, k: (k, 0),
                jax.ShapeDtypeStruct(into.shape, F32), (None, None, c, N), lambda i, j, k: (j, l, 0, 0),
                1, T // bk, False, acc_into=into)


def _mm_cols(name, a, w, l):
    (T, K), c = a.shape, w.shape[-1]
    bm, nm = _row_tiling(T)
    return _mmc(name, "nn", a, (bm, K), lambda i, j, k: (i, 0), w, (None, None, K, c), lambda i, j, k: (j, l, 0, 0),
                jax.ShapeDtypeStruct((T, CHIPS * c), F32), (bm, c), lambda i, j, k: (i, j), nm, 1, False)


def _mm_cols_t(name, g, w, l):
    (T, _), (K, c) = g.shape, w.shape[-2:]
    bm, nm = _row_tiling(T)
    return _mmc(name, "nt", g, (bm, c), lambda i, j, k: (i, j), w, (None, None, K, c), lambda i, j, k: (j, l, 0, 0),
                jax.ShapeDtypeStruct((T, K), F32), (bm, K), lambda i, j, k: (i, 0), nm, 1, True)


def _mm_cols_dw(name, a, g, into, l):
    (T, K), c = a.shape, into.shape[-1]
    bk = _tile(T, (1056, 528, 512, 384, 256, 128))
    return _mmc(name, "tn", a, (bk, K), lambda i, j, k: (k, 0), g, (bk, c), lambda i, j, k: (k, j),
                jax.ShapeDtypeStruct(into.shape, F32), (None, None, K, c), lambda i, j, k: (j, l, 0, 0),
                1, T // bk, False, acc_into=into)


def _rowmap(name, body, rows, vecs, outs, accs=(), tr_prefs=(384, 256, 128)):
    R = rows[0].shape[0]
    tr = _tile(R, tr_prefs)
    n_r, n_v, n_o, n_a = len(rows), len(vecs), len(outs), len(accs)

    def kern(*refs):
        ins = [r[...] for r in refs[:n_r + n_v]]
        o_refs = refs[n_r + n_v:n_r + n_v + n_o]
        a_refs = refs[n_r + n_v + n_o:]
        res = body(*ins)
        for o_ref, val in zip(o_refs, res[:n_o]):
            o_ref[...] = val.astype(o_ref.dtype)
        if n_a:
            first = pl.program_id(0) == 0

            @pl.when(first)
            def _():
                for a_ref, val in zip(a_refs, res[n_o:]):
                    a_ref[...] = val

            @pl.when(jnp.logical_not(first))
            def _():
                for a_ref, val in zip(a_refs, res[n_o:]):
                    a_ref[...] += val

    in_specs = [pl.BlockSpec((tr, r.shape[1]), lambda i: (i, 0)) for r in rows]
    in_specs += [pl.BlockSpec((1, v.shape[1]), lambda i: (0, 0)) for v in vecs]
    out_specs = [pl.BlockSpec((tr, c), lambda i: (i, 0)) for c, _ in outs]
    out_specs += [pl.BlockSpec((1, c), lambda i: (0, 0)) for c in accs]
    out_shape = [jax.ShapeDtypeStruct((R, c), dt) for c, dt in outs]
    out_shape += [jax.ShapeDtypeStruct((1, c), F32) for c in accs]
    return pl.pallas_call(
        kern, name=name, out_shape=out_shape, grid=(R // tr,),
        in_specs=in_specs, out_specs=out_specs,
        compiler_params=_params(("arbitrary",) if n_a else ("parallel",)),
    )(*rows, *vecs)


def _lane(shape):
    return lax.broadcasted_iota(jnp.int32, shape, 1)


def _rot_swa(x):
    lane = _lane(x.shape)
    return jnp.where((lane & 63) < 32, -pltpu.roll(x, 96, 1), pltpu.roll(x, 32, 1))


def _rot_mla(x):
    lane = _lane(x.shape)
    lo = jnp.where(lane >= KR_LANE, -pltpu.roll(x, 112, 1), 0.0)
    hi = jnp.where(lane < KR_LANE + MLA_ROPE_DIM, pltpu.roll(x, 16, 1), 0.0)
    return jnp.where(lane < KR_LANE + 16, lo, hi)


def _rope(x, cos, sin, rot):
    return x * cos + rot(x) * sin


def _rope_t(g, cos, sin, rot):
    return g * cos - rot(g * sin)


def _low(x):
    return jnp.where(_lane(x.shape) < 64, x, 0.0)


def _blk(x, j):
    return x[:, j * LANES:(j + 1) * LANES]


def _rms_r(x, width):
    return lax.rsqrt(jnp.sum(x * x, axis=-1, keepdims=True) * (1.0 / width) + EPS)


def _rms_bwd(x, g, dy, width):
    r = _rms_r(x, width)
    gdy = dy * g
    dot = jnp.sum(gdy * x, axis=-1, keepdims=True)
    dx = r * gdy - x * (r * r * r * (1.0 / width) * dot)
    return dx, dy * x * r


def _colsum(x):
    return jnp.sum(x, axis=0, keepdims=True)


def _rmsnorm_fwd(name, x, g):
    def body(xv, gv):
        return (xv * _rms_r(xv, D_MODEL) * gv,)
    return _rowmap(name, body, [x], [g], [(D_MODEL, BF16)])[0]


def _rmsnorm_bwd(name, x, g, dy, dres):
    def body(xv, dyv, dr, gv):
        dx, dg = _rms_bwd(xv, gv, dyv, D_MODEL)
        return dx + dr, _colsum(dg)
    return _rowmap(name, body, [x, dy, dres], [g], [(D_MODEL, F32)], [D_MODEL])


def _prep1(proj, qn_g, kv_g, cosa, sina, cosm, sinm):
    def body(p, ca, sa, cm, sm, gq, gk):
        qa = []
        for j in range(4):
            xr = _rope(_blk(p, j), ca, sa, _rot_swa)
            qa += [_low(xr), _low(pltpu.roll(xr, 64, 1))]
        kr_ = _rope(_blk(p, C_KA // LANES), ca, sa, _rot_swa)
        ka = [_low(kr_), _low(pltpu.roll(kr_, 64, 1))]
        vv = _blk(p, C_VA // LANES)
        va = [_low(vv), _low(pltpu.roll(vv, 64, 1))]
        ql = p[:, C_QL:C_QL + MLA_Q_RANK]
        qn = ql * _rms_r(ql, MLA_Q_RANK) * gq
        kl = p[:, C_KL:C_KL + MLA_KV_RANK]
        cn = kl * _rms_r(kl, MLA_KV_RANK) * gk
        kr = _rope(_blk(p, C_KR // LANES), cm, sm, _rot_mla)
        return (jnp.concatenate(qa, 1), jnp.concatenate(ka, 1), jnp.concatenate(va, 1), qn, cn, kr)
    return _rowmap("prep1", body, [proj, cosa, sina, cosm, sinm], [qn_g, kv_g],
                   [(SLOT_W, BF16), (2 * LANES, BF16), (2 * LANES, BF16),
                    (MLA_Q_RANK, BF16), (MLA_KV_RANK, BF16), (LANES, F32)])


def _prep2(qb, kvb, kr, cosm, sinm):
    def body(q, kv, krv, cm, sm):
        qs, ks, vs = [], [], []
        for h in range(MLA_HEADS):
            qs.append(_rope(_blk(q, h), cm, sm, _rot_mla))
            kvh = _blk(kv, h)
            ks.append(_low(kvh) + krv)
            vs.append(_low(pltpu.roll(kvh, 64, 1)))
        return jnp.concatenate(qs, 1), jnp.concatenate(ks, 1), jnp.concatenate(vs, 1)
    return _rowmap("prep2", body, [qb, kvb, kr, cosm, sinm], [],
                   [(SLOT_W, BF16), (SLOT_W, BF16), (SLOT_W, BF16)])


def _compact(slots):
    return jnp.concatenate(
        [_blk(slots, 2 * j) + pltpu.roll(_blk(slots, 2 * j + 1), 64, 1) for j in range(4)], 1)


def _expand(nat):
    out = []
    for j in range(4):
        b = _blk(nat, j)
        out += [_low(b), _low(pltpu.roll(b, 64, 1))]
    return jnp.concatenate(out, 1)


def _merge_fwd(oa, ob, ga, gb):
    def body(a, b, gav, gbv):
        xa, xb = _compact(a), _compact(b)
        return (jnp.concatenate([xa * _rms_r(xa, 512) * gav, xb * _rms_r(xb, 512) * gbv], 1),)
    return _rowmap("merge_fwd", body, [oa, ob], [ga, gb], [(D_MODEL, BF16)])[0]


def _merge_bwd(dmix, oa, ob, lse_a, ga, gb, sink_slots):
    def body(dm, a, b, lse, gav, gbv, sk):
        outs = []
        accs = []
        for o, g, lo in ((a, gav, 0), (b, gbv, 512)):
            x = _compact(o)
            dx, dg = _rms_bwd(x, g, dm[:, lo:lo + 512], 512)
            do = _expand(dx)
            delta = jnp.concatenate(
                [jnp.broadcast_to(jnp.sum(_blk(do, h) * _blk(o, h), axis=-1, keepdims=True),
                                  (do.shape[0], LANES)) for h in range(8)], 1)
            outs += [do, delta]
            accs.append(_colsum(dg))
        dsink = _colsum(-jnp.exp(sk - lse) * outs[1])
        return (*outs, *accs, dsink)
    return _rowmap("merge_bwd", body, [dmix, oa, ob, lse_a], [ga, gb, sink_slots],
                   [(SLOT_W, BF16), (SLOT_W, F32), (SLOT_W, BF16), (SLOT_W, F32)],
                   [512, 512, SLOT_W])


def _prep2_bwd(dq, dk, dv, cosm, sinm):
    def body(dqv, dkv, dvv, cm, sm):
        dqb, dkvb = [], []
        krsum = jnp.zeros((dqv.shape[0], LANES), F32)
        for h in range(MLA_HEADS):
            dqb.append(_rope_t(_blk(dqv, h), cm, sm, _rot_mla))
            dkh = _blk(dkv, h)
            dkvb.append(_low(dkh) + pltpu.roll(_blk(dvv, h), 64, 1))
            krsum = krsum + dkh
        lane = _lane(krsum.shape)
        dkr = jnp.where((lane >= KR_LANE) & (lane < KR_LANE + MLA_ROPE_DIM),
                        _rope_t(krsum, cm, sm, _rot_mla), 0.0)
        return jnp.concatenate(dqb, 1), jnp.concatenate(dkvb, 1), dkr
    return _rowmap("prep2_bwd", body, [dq, dk, dv, cosm, sinm], [],
                   [(SLOT_W, BF16), (SLOT_W, BF16), (LANES, F32)])


def _prep1_bwd(proj, dqa, dka, dva, dqn, dcn, dkr, cosa, sina, qn_g, kv_g):
    def body(p, dq, dk, dv, dqnv, dcnv, dkrv, ca, sa, gq, gk):
        cols = []
        for j in range(4):
            nat = _blk(dq, 2 * j) + pltpu.roll(_blk(dq, 2 * j + 1), 64, 1)
            cols.append(_rope_t(nat, ca, sa, _rot_swa))
        grp = lambda d, g: sum(_blk(d, 4 * g + i) for i in range(4))
        cols.append(_rope_t(grp(dk, 0) + pltpu.roll(grp(dk, 1), 64, 1), ca, sa, _rot_swa))
        cols.append(grp(dv, 0) + pltpu.roll(grp(dv, 1), 64, 1))
        dql, dgq = _rms_bwd(p[:, C_QL:C_QL + MLA_Q_RANK], gq, dqnv, MLA_Q_RANK)
        dkl, dgk = _rms_bwd(p[:, C_KL:C_KL + MLA_KV_RANK], gk, dcnv, MLA_KV_RANK)
        cols += [dql, dkl, dkrv]
        return jnp.concatenate(cols, 1), _colsum(dgq), _colsum(dgk)
    return _rowmap("prep1_bwd", body, [proj, dqa, dka, dva, dqn, dcn, dkr, cosa, sina], [qn_g, kv_g],
                   [(IN_WP, BF16)], [MLA_Q_RANK, MLA_KV_RANK], tr_prefs=(192, 128))


def _sigmoid(x):
    return 1.0 / (1.0 + jnp.exp(-x))


def _swiglu_fwd(a, b):
    def body(av, bv):
        return (av * _sigmoid(av) * bv,)
    return _rowmap("swiglu_fwd", body, [a, b], [], [(a.shape[1], BF16)])[0]


def _swiglu_bwd(a, b, dh):
    def body(av, bv, d):
        s = _sigmoid(av)
        return d * bv * (s * (1.0 + av * (1.0 - s))), d * (av * s)
    return _rowmap("swiglu_bwd", body, [a, b, dh], [], [(a.shape[1], BF16)] * 2)


def _loss_head(h, target, g):
    T = h.shape[0]
    nb = T // BLOCK

    def kern(h_ref, t_ref, g_ref, dh_ref, dg_ref, loss_ref, acc):
        i = pl.program_id(0)

        @pl.when(i == 0)
        def _():
            dh_ref[...] = jnp.zeros_like(dh_ref)
            dg_ref[...] = jnp.zeros_like(dg_ref)
            acc[...] = jnp.zeros_like(acc)

        @pl.when(i > 0)
        def _():
            x = h_ref[...]
            gv = g_ref[...]
            e = x * _rms_r(x, D_MODEL) * gv - t_ref[...]
            acc[...] += _colsum(e * e)
            dx, dg = _rms_bwd(x, gv, e * (1.0 / D_MODEL), D_MODEL)
            dh_ref[...] = dx
            dg_ref[...] += _colsum(dg)

        @pl.when(i == nb - 1)
        def _():
            tot = jnp.sum(acc[...], axis=-1, keepdims=True) * (0.5 / D_MODEL)
            loss_ref[...] = jnp.broadcast_to(tot, loss_ref.shape)

    return pl.pallas_call(
        kern, name="loss_head",
        out_shape=[jax.ShapeDtypeStruct((T, D_MODEL), F32), jax.ShapeDtypeStruct((1, D_MODEL), F32),
                   jax.ShapeDtypeStruct((1, LANES), F32)],
        grid=(nb,),
        in_specs=[pl.BlockSpec((BLOCK, D_MODEL), lambda i: (i, 0)),
                  pl.BlockSpec((BLOCK, D_MODEL), lambda i: (jnp.maximum(i - 1, 0), 0)),
                  pl.BlockSpec((1, D_MODEL), lambda i: (0, 0))],
        out_specs=[pl.BlockSpec((BLOCK, D_MODEL), lambda i: (i, 0)),
                   pl.BlockSpec((1, D_MODEL), lambda i: (0, 0)),
                   pl.BlockSpec((1, LANES), lambda i: (0, 0))],
        scratch_shapes=[pltpu.VMEM((1, D_MODEL), F32)],
        compiler_params=_params(("arbitrary",)),
    )(h, target, g)


LOG2E = 1.4426950408889634


def _attn_plan(T, causal):
    tq = _tile(T, (384, 256, 128))
    ck = min(2 * tq, T) if causal else min(tq + WINDOW, T)
    return tq, ck, (-(-T // ck) if causal else 1)


def _chunk(i, c, T, tq, ck, causal):
    if causal:
        return pl.multiple_of(jnp.minimum(c * ck, T - ck), LANES), c * ck
    return pl.multiple_of(jnp.clip(i * tq - WINDOW, 0, T - ck), LANES), 0


def _n_chunks(i, tq, ck, causal):
    return ((i + 1) * tq + ck - 1) // ck if causal else 1


def _mask(s, i, start, first, tq, ck, causal):
    qpos = i * tq + lax.broadcasted_iota(jnp.int32, (tq, 1), 0)
    kpos = start + lax.broadcasted_iota(jnp.int32, (tq, ck), 1)
    low = jnp.maximum(jnp.where(qpos < FRONT, 0, FRONT), first)
    if not causal:
        low = jnp.maximum(low, qpos - (WINDOW - 1))
    return jnp.where(kpos >= low, jnp.where(kpos <= qpos, s, NEG), NEG)


def _chunk_loop(n, body, init, causal):
    carry = body(0, init, True)
    if not causal:
        return carry
    carry = lax.fori_loop(1, n - 1, lambda c, cr: body(c, cr, False), carry)
    return lax.cond(n > 1, lambda cr: body(n - 1, cr, True), lambda cr: cr, carry)


FWD_HEADS_PER_STEP = 2
BWD_HEADS_PER_STEP = 1


def _head_cols(group, hp):
    q_cols = lambda hh: slice(hh * LANES, (hh + 1) * LANES)
    if group == 1:
        return q_cols, q_cols, hp * LANES
    assert group % hp == 0
    return q_cols, (lambda hh: slice(0, LANES)), LANES


def _whole(T, width, index, single):
    if single:
        return pl.BlockSpec((T, width), index, pipeline_mode=pl.Buffered(1))
    return pl.BlockSpec((T, width), index)


def _attn_fwd(name, q, k, v, sinks, group, causal, scale):
    T = q.shape[0]
    H = q.shape[1] // LANES
    HP = FWD_HEADS_PER_STEP
    tq, ck, slots = _attn_plan(T, causal)
    nq = T // tq
    c2 = scale * LOG2E
    q_cols, k_cols, kw = _head_cols(group, HP)

    def kern(sink_ref, q_ref, k_ref, v_ref, o_ref, lse_ref, s_scr):
        sink = [sink_ref[pl.program_id(0) * HP + hh] for hh in range(HP)]

        def q_tile(i, carry):
            rows = pl.ds(pl.multiple_of(i * tq, tq), tq)
            qq = [q_ref[rows, q_cols(hh)] for hh in range(HP)]
            n = _n_chunks(i, tq, ck, causal)

            def score(c, m, masked):
                start, first = _chunk(i, c, T, tq, ck, causal)
                out = []
                for hh in range(HP):
                    s = lax.dot_general(qq[hh], k_ref[pl.ds(start, ck), k_cols(hh)], _DIMS["nt"],
                                        preferred_element_type=F32)
                    if masked:
                        s = _mask(s, i, start, first, tq, ck, causal)
                    s_scr[hh, c] = s
                    out.append(jnp.maximum(m[hh], jnp.max(s, axis=-1, keepdims=True)))
                return tuple(out)

            m = _chunk_loop(n, score, tuple(jnp.full((tq, 1), sk * (1.0 / scale), F32) for sk in sink), causal)
            m2 = [mh * c2 for mh in m]

            def weigh(c, carry):
                start, _ = _chunk(i, c, T, tq, ck, causal)
                out = []
                for hh in range(HP):
                    l, acc = carry[hh]
                    p = jnp.exp2(s_scr[hh, c] * c2 - m2[hh])
                    acc = acc + jnp.dot(p.astype(MXU_DTYPE), v_ref[pl.ds(start, ck), k_cols(hh)],
                                        preferred_element_type=F32)
                    out.append((l + jnp.sum(p, axis=-1, keepdims=True), acc))
                return tuple(out)

            init = tuple((jnp.exp2(sink[hh] * LOG2E - m2[hh]), jnp.zeros((tq, LANES), F32)) for hh in range(HP))
            res = lax.fori_loop(0, n, weigh, init)
            for hh in range(HP):
                l, acc = res[hh]
                o_ref[rows, q_cols(hh)] = acc / l
                lse_ref[rows, q_cols(hh)] = jnp.broadcast_to(m[hh] * scale + jnp.log(l), (tq, LANES))
            return carry

        lax.fori_loop(0, nq, q_tile, 0)

    q_spec = _whole(T, HP * LANES, lambda g: (0, g), True)
    kv_spec = _whole(T, kw, (lambda g: (0, g)) if group == 1 else (lambda g: (0, (g * HP) // group)), True)
    return pl.pallas_call(
        kern, name=name,
        out_shape=[jax.ShapeDtypeStruct((T, H * LANES), F32)] * 2,
        grid=(H // HP,),
        in_specs=[pl.BlockSpec(memory_space=pltpu.SMEM), q_spec, kv_spec, kv_spec],
        out_specs=[q_spec, q_spec],
        scratch_shapes=[pltpu.VMEM((HP, slots, tq, ck), F32)],
        compiler_params=_params(("parallel",)),
    )(sinks, q, k, v)


def _attn_bwd(name, q, k, v, do, lse, delta, group, causal, scale):
    T = q.shape[0]
    H = q.shape[1] // LANES
    HP = BWD_HEADS_PER_STEP
    tq, ck, _ = _attn_plan(T, causal)
    nq = T // tq
    c2 = scale * LOG2E
    q_cols, k_cols, kw = _head_cols(group, HP)

    def kern(q_ref, k_ref, v_ref, do_ref, lse_ref, dl_ref, dq_ref, dk_ref, dv_ref):
        dk_ref[...] = jnp.zeros_like(dk_ref)
        dv_ref[...] = jnp.zeros_like(dv_ref)

        def q_tile(i, carry):
            rows = pl.ds(pl.multiple_of(i * tq, tq), tq)
            qq = [q_ref[rows, q_cols(hh)] for hh in range(HP)]
            dd = [do_ref[rows, q_cols(hh)] for hh in range(HP)]
            lse2 = [lse_ref[rows, q_cols(hh)][:, 0:1] * LOG2E for hh in range(HP)]
            dl_c = [dl_ref[rows, q_cols(hh)][:, 0:1] for hh in range(HP)]

            def chunk(c, dq, masked):
                start, first = _chunk(i, c, T, tq, ck, causal)
                keys = pl.ds(start, ck)
                out = []
                for hh in range(HP):
                    kk, vv = k_ref[keys, k_cols(hh)], v_ref[keys, k_cols(hh)]
                    s = lax.dot_general(qq[hh], kk, _DIMS["nt"], preferred_element_type=F32)
                    if masked:
                        s = _mask(s, i, start, first, tq, ck, causal)
                    p = jnp.exp2(s * c2 - lse2[hh])
                    dv_ref[keys, q_cols(hh)] += lax.dot_general(p.astype(MXU_DTYPE), dd[hh], _DIMS["tn"],
                                                                preferred_element_type=F32)
                    dp = lax.dot_general(dd[hh], vv, _DIMS["nt"], preferred_element_type=F32)
                    ds = (p * (dp - dl_c[hh])).astype(MXU_DTYPE)
                    dk_ref[keys, q_cols(hh)] += lax.dot_general(ds, qq[hh], _DIMS["tn"],
                                                                preferred_element_type=F32) * scale
                    out.append(dq[hh] + jnp.dot(ds, kk, preferred_element_type=F32))
                return tuple(out)

            dq = _chunk_loop(_n_chunks(i, tq, ck, causal), chunk,
                             tuple(jnp.zeros((tq, LANES), F32) for _ in range(HP)), causal)
            for hh in range(HP):
                dq_ref[rows, q_cols(hh)] = dq[hh] * scale
            return carry

        lax.fori_loop(0, nq, q_tile, 0)

    q_spec = _whole(T, HP * LANES, lambda g: (0, g), False)
    kv_spec = _whole(T, kw, (lambda g: (0, g)) if group == 1 else (lambda g: (0, (g * HP) // group)), False)
    return pl.pallas_call(
        kern, name=name,
        out_shape=[jax.ShapeDtypeStruct((T, H * LANES), F32)] * 3,
        grid=(H // HP,),
        in_specs=[q_spec, kv_spec, kv_spec, q_spec, q_spec, q_spec],
        out_specs=[q_spec, q_spec, q_spec],
        compiler_params=_params(("parallel",)),
    )(q, k, v, do, lse, delta)


def _ew(name, fn, ins, out_dtypes):
    shape = ins[0].shape
    flat = [a.reshape(-1, shape[-1]) for a in ins]
    R, C = flat[0].shape
    tr = _tile(R, (512, 256, 128, 64, 32, 16, 8))
    n_in = len(ins)

    def kern(*refs):
        res = fn(*[r[...] for r in refs[:n_in]])
        for o_ref, val in zip(refs[n_in:], res):
            o_ref[...] = val.astype(o_ref.dtype)

    spec = pl.BlockSpec((tr, C), lambda i: (i, 0))
    outs = pl.pallas_call(
        kern, name=name,
        out_shape=[jax.ShapeDtypeStruct((R, C), dt) for dt in out_dtypes],
        grid=(R // tr,), in_specs=[spec] * n_in, out_specs=[spec] * len(out_dtypes),
        compiler_params=_params(("parallel",)),
    )(*flat)
    return [o.reshape(shape) for o in outs]


def _adamw(name, w, g, m, v):
    c1 = 1.0 - ADAM_B1 ** ADAM_STEP
    c2 = 1.0 - ADAM_B2 ** ADAM_STEP

    def fn(wv, gv, mv, vv):
        mn = ADAM_B1 * mv + (1.0 - ADAM_B1) * gv
        vn = ADAM_B2 * vv + (1.0 - ADAM_B2) * (gv * gv)
        delta = -ADAM_LR * ((mn / c1) / (jnp.sqrt(vn / c2) + ADAM_EPS) + ADAM_WD * wv)
        return delta, mn, vn

    return _ew(name, fn, [w, g, m, v], [F32, F32, F32])


_ANY = pl.BlockSpec(memory_space=pl.ANY)


def _where_am_i():
    x, y, c = lax.axis_index("x"), lax.axis_index("y"), lax.axis_index("c")
    chips = [(1 - x, y), (x, 1 - y), (1 - x, 1 - y)]
    return x, y, c, chips


def _row_chunks(r):
    n = 4 if r % 64 == 0 else 1
    return [(q * (r // n), r // n) for q in range(n)]


def _gather_weights(shards, meta):
    arrs = list(shards) + [meta]
    n = len(arrs)
    per = [2] * len(shards) + [8]

    def body(*refs):
        ins, outs = refs[:n], refs[n:2 * n]
        send1, recv1, send2, recv2, lsem = refs[2 * n:]
        x, y, c, chips = _where_am_i()
        me = 2 * x + y

        def half(ref, k, cc):
            return ref.at[pl.ds(per[k] * cc, per[k])]

        for k in range(n - 1):
            for l in range(ins[k].shape[0]):
                for r0, rn in _row_chunks(ins[k].shape[1]):
                    pltpu.make_async_copy(ins[k].at[l, pl.ds(r0, rn)], outs[k].at[me, l, pl.ds(r0, rn)],
                                          lsem.at[k]).start()
        pltpu.make_async_copy(ins[n - 1], outs[n - 1].at[me], lsem.at[n - 1]).start()
        first = []
        for k in range(n):
            for j, (cx, cy) in enumerate(chips):
                first.append(pltpu.make_async_remote_copy(
                    src_ref=half(ins[k], k, c), dst_ref=half(outs[k].at[me], k, c),
                    send_sem=send1.at[k, j], recv_sem=recv1.at[k, j],
                    device_id=(cx, cy, c), device_id_type=MESH))
        for cp in first:
            cp.start()
        passed = []
        for k in range(n):
            for j, (cx, cy) in enumerate(chips):
                landed = half(outs[k].at[2 * cx + cy], k, c)
                pltpu.make_async_remote_copy(
                    src_ref=landed, dst_ref=landed, send_sem=send1.at[k, j], recv_sem=recv1.at[k, j],
                    device_id=(cx, cy, c), device_id_type=MESH).wait_recv()
                fwd = pltpu.make_async_remote_copy(
                    src_ref=landed, dst_ref=landed, send_sem=send2.at[k, j], recv_sem=recv2.at[k, j],
                    device_id=(x, y, 1 - c), device_id_type=MESH)
                fwd.start()
                passed.append(fwd)
        for k in range(n):
            for j, (cx, cy) in enumerate(chips):
                other = half(outs[k].at[2 * cx + cy], k, 1 - c)
                pltpu.make_async_remote_copy(
                    src_ref=other, dst_ref=other, send_sem=send2.at[k, j], recv_sem=recv2.at[k, j],
                    device_id=(x, y, 1 - c), device_id_type=MESH).wait_recv()
        for cp in first + passed:
            cp.wait_send()
        for k in range(n):
            pltpu.make_async_copy(ins[k], outs[k].at[me], lsem.at[k]).wait()

    return pl.pallas_call(
        body, name="gather_weights",
        out_shape=[jax.ShapeDtypeStruct((4,) + a.shape, a.dtype) for a in arrs],
        in_specs=[_ANY] * n, out_specs=[_ANY] * n,
        scratch_shapes=[pltpu.SemaphoreType.DMA((n, 3))] * 4 + [pltpu.SemaphoreType.DMA((n,))],
    )(*arrs)


def _pair_exchange(grads):
    n = len(grads)

    def body(*refs):
        ins, got = refs[:n], refs[n:2 * n]
        send, recv = refs[2 * n:]
        x, y, c, _ = _where_am_i()
        sib = (x, y, 1 - c)
        for k in range(n):
            for ch in range(4):
                for l in range(2):
                    pltpu.make_async_remote_copy(
                        src_ref=ins[k].at[ch, 2 * (1 - c) + l], dst_ref=got[k].at[ch, l],
                        send_sem=send.at[k], recv_sem=recv.at[k], device_id=sib, device_id_type=MESH).start()
        for k in range(n):
            pltpu.make_async_remote_copy(
                src_ref=got[k], dst_ref=got[k], send_sem=send.at[k], recv_sem=recv.at[k],
                device_id=sib, device_id_type=MESH).wait()

    return pl.pallas_call(
        body, name="reduce_pair",
        out_shape=[jax.ShapeDtypeStruct((4, 2) + g.shape[2:], g.dtype) for g in grads],
        in_specs=[_ANY] * n, out_specs=[_ANY] * n,
        scratch_shapes=[pltpu.SemaphoreType.DMA((n,))] * 2,
    )(*grads)


def _pair_add(full, got, c):
    _, _, r, cols = full.shape
    tr = _tile(r, (512, 256, 352, 128))

    def kern(c_ref, a_ref, b_ref, o32_ref, o16_ref):
        tot = a_ref[...] + b_ref[...].astype(F32)
        o32_ref[...] = tot
        o16_ref[...] = tot.astype(o16_ref.dtype)

    blk = (None, None, tr, cols)
    mine = pl.BlockSpec(blk, lambda ch, l, i, cr: (ch, 2 * cr[0] + l, i, 0))
    same = pl.BlockSpec(blk, lambda ch, l, i, cr: (ch, l, i, 0))
    return pl.pallas_call(
        kern, name="pair_add",
        out_shape=[jax.ShapeDtypeStruct(got.shape, F32), jax.ShapeDtypeStruct(got.shape, got.dtype)],
        grid_spec=pltpu.PrefetchScalarGridSpec(
            num_scalar_prefetch=1, grid=(4, 2, r // tr), in_specs=[mine, same], out_specs=[same, same]),
        compiler_params=_params(("parallel", "parallel", "parallel")),
    )(c.reshape(1), full, got)


def _chip_scatter(parts):
    n = len(parts)

    def body(*refs):
        ins, outs = refs[:n], refs[n:2 * n]
        send, recv = refs[2 * n:]
        x, y, c, chips = _where_am_i()
        me = 2 * x + y
        for k in range(n):
            for j, (cx, cy) in enumerate(chips):
                for l in range(2):
                    pltpu.make_async_remote_copy(
                        src_ref=ins[k].at[2 * cx + cy, l], dst_ref=outs[k].at[me, l],
                        send_sem=send.at[k, j], recv_sem=recv.at[k, j],
                        device_id=(cx, cy, c), device_id_type=MESH).start()
        for k in range(n):
            for j, (cx, cy) in enumerate(chips):
                slot = outs[k].at[2 * cx + cy]
                pltpu.make_async_remote_copy(
                    src_ref=slot, dst_ref=slot, send_sem=send.at[k, j], recv_sem=recv.at[k, j],
                    device_id=(cx, cy, c), device_id_type=MESH).wait()

    return pl.pallas_call(
        body, name="reduce_chips",
        out_shape=[jax.ShapeDtypeStruct(p.shape, p.dtype) for p in parts],
        in_specs=[_ANY] * n, out_specs=[_ANY] * n,
        scratch_shapes=[pltpu.SemaphoreType.DMA((n, 3))] * 2,
    )(*parts)


def _chip_add(landed, mine, me, c):
    _, _, r, cols = landed.shape
    tr = _tile(r, (512, 256, 352, 128))

    def kern(me_ref, c_ref, land_ref, own_ref, o_ref):
        own = own_ref[...]
        tot = None
        for j in range(4):
            term = jnp.where(me_ref[0] == j, own, land_ref[j].astype(F32))
            tot = term if tot is None else tot + term
        o_ref[...] = tot

    return pl.pallas_call(
        kern, name="chip_add",
        out_shape=jax.ShapeDtypeStruct((4, r, cols), F32),
        grid_spec=pltpu.PrefetchScalarGridSpec(
            num_scalar_prefetch=2, grid=(2, r // tr),
            in_specs=[pl.BlockSpec((4, None, tr, cols), lambda l, i, mr, cr: (0, l, i, 0)),
                      pl.BlockSpec((None, None, tr, cols), lambda l, i, mr, cr: (mr[0], l, i, 0))],
            out_specs=pl.BlockSpec((None, tr, cols), lambda l, i, mr, cr: (2 * cr[0] + l, i, 0))),
        compiler_params=_params(("parallel", "parallel")),
    )(me.reshape(1), c.reshape(1), landed, mine)


def _pair_join(sums):
    n = len(sums)

    def body(*refs):
        bufs = refs[n:2 * n]
        send, recv = refs[2 * n:]
        x, y, c, _ = _where_am_i()
        sib = (x, y, 1 - c)
        for k in range(n):
            for l in range(2):
                for r0, rn in _row_chunks(bufs[k].shape[1]):
                    piece = bufs[k].at[2 * c + l, pl.ds(r0, rn)]
                    pltpu.make_async_remote_copy(
                        src_ref=piece, dst_ref=piece, send_sem=send.at[k], recv_sem=recv.at[k],
                        device_id=sib, device_id_type=MESH).start()
        for k in range(n):
            theirs = bufs[k].at[pl.ds(2 * (1 - c), 2)]
            pltpu.make_async_remote_copy(
                src_ref=theirs, dst_ref=theirs, send_sem=send.at[k], recv_sem=recv.at[k],
                device_id=sib, device_id_type=MESH).wait()

    return pl.pallas_call(
        body, name="reduce_join",
        out_shape=[jax.ShapeDtypeStruct(s.shape, s.dtype) for s in sums],
        in_specs=[_ANY] * n, out_specs=[_ANY] * n,
        input_output_aliases={k: k for k in range(n)},
        scratch_shapes=[pltpu.SemaphoreType.DMA((n,))] * 2,
    )(*sums)


def _allreduce_small(buf):
    R = buf.shape[0]

    def body(in_ref, out_ref, land, send, recv):
        x, y, c, _ = _where_am_i()
        me = 4 * x + 2 * y + c
        land[me] = in_ref[...]
        cps = []
        for k in range(1, 8):
            px, py, pc = x ^ (k >> 2), y ^ ((k >> 1) & 1), c ^ (k & 1)
            cps.append(pltpu.make_async_remote_copy(
                src_ref=in_ref, dst_ref=land.at[me], send_sem=send.at[k - 1], recv_sem=recv.at[k - 1],
                device_id=(px, py, pc), device_id_type=MESH))
        for cp in cps:
            cp.start()
        for k in range(1, 8):
            px, py, pc = x ^ (k >> 2), y ^ ((k >> 1) & 1), c ^ (k & 1)
            slot = land.at[4 * px + 2 * py + pc]
            pltpu.make_async_remote_copy(
                src_ref=slot, dst_ref=slot, send_sem=send.at[k - 1], recv_sem=recv.at[k - 1],
                device_id=(px, py, pc), device_id_type=MESH).wait_recv()
        for cp in cps:
            cp.wait_send()
        tot = land[0]
        for d in range(1, 8):
            tot = tot + land[d]
        out_ref[...] = tot

    vm = pl.BlockSpec(memory_space=pltpu.VMEM)
    return pl.pallas_call(
        body, name="allreduce_small",
        out_shape=jax.ShapeDtypeStruct(buf.shape, F32),
        in_specs=[vm], out_specs=vm,
        scratch_shapes=[pltpu.VMEM((8, R, LANES), F32), pltpu.SemaphoreType.DMA((7,)),
                        pltpu.SemaphoreType.DMA((7,))],
    )(buf)


def _rope_tables(T):
    pos = (jnp.arange(T) - FRONT).astype(F32)
    lane = jnp.arange(LANES)
    inv_a = ROPE_THETA ** (-(2 * ((lane % 64) % 32)).astype(F32) / SWA_HEAD_DIM)
    ang_a = pos[:, None] * inv_a[None, :]
    cosa, sina = jnp.cos(ang_a), jnp.sin(ang_a)
    inv_m = ROPE_THETA ** (-(2 * ((lane - KR_LANE) % 16)).astype(F32) / MLA_ROPE_DIM)
    ang_m = pos[:, None] * inv_m[None, :]
    on = ((lane >= KR_LANE) & (lane < KR_LANE + MLA_ROPE_DIM))[None, :]
    cosm = jnp.where(on, jnp.cos(ang_m), 1.0)
    sinm = jnp.where(on, jnp.sin(ang_m), 0.0)
    return cosa, sina, cosm, sinm


def _cols_from_chips(g):
    return jnp.concatenate([g[j] for j in range(4)], axis=-1)


def _cols_to_chips(w):
    L, r, c4 = w.shape
    return jnp.moveaxis(w.reshape(L, r, 4, c4 // 4), 2, 0)


def _local_step(x2, target, meta_full, w, p):
    T = BLOCK + x2.shape[0]
    L = DEPTH
    W_in, W_qup, W_kvup, W_o, W_gate, W_up, W_down = w
    zpad = lambda n: jnp.zeros((L, D_MODEL, n), W_in.dtype)
    W_in = jnp.concatenate([W_in[..., :C_KR], zpad(KR_LANE), W_in[..., C_KR:IN_W],
                            zpad(LANES - KR_LANE - MLA_ROPE_DIM)], axis=-1)
    W_qup = W_qup.reshape(L, MLA_Q_RANK, MLA_HEADS, MLA_QK_DIM)
    W_qup = jnp.pad(W_qup, ((0, 0), (0, 0), (0, 0), (0, LANES - MLA_QK_DIM))).reshape(L, MLA_Q_RANK, SLOT_W)
    attn_norm, q_norm, kv_norm, sinks = p["attn_norm"], p["q_norm"], p["kv_norm"], p["sinks"]
    out_norm_swa, out_norm_mla, ffn_norm, final_norm = (
        p["out_norm_swa"], p["out_norm_mla"], p["ffn_norm"], p["final_norm"])

    cosa, sina, cosm, sinm = _rope_tables(T)
    no_sink = jnp.full((MLA_HEADS,), NEG, F32)
    scale_a, scale_b = SWA_HEAD_DIM ** -0.5, MLA_QK_DIM ** -0.5
    row = lambda v: v.reshape(1, -1)
    flat = lambda t: t.reshape(CHIPS * T, -1)
    unflat = lambda t: t.reshape(CHIPS, T, -1)

    h = jnp.concatenate([jnp.zeros((FRONT, D_MODEL), F32), meta_full, x2], axis=0)
    saved = []
    for l in range(L):
        u = _rmsnorm_fwd("attn_norm", h, row(attn_norm[l]))
        proj = _mm("in_proj", u, W_in[l], "nn")
        qa, ka, va, qn, cn, kr = _prep1(proj, row(q_norm[l]), row(kv_norm[l]), cosa, sina, cosm, sinm)
        qb = _mm("q_up", qn, W_qup[l], "nn")
        kvb = _mm_cols("kv_up", cn, W_kvup, l)
        qs, ks, vs = _prep2(qb, kvb, kr, cosm, sinm)
        oa, lse_a = _attn_fwd("swa_fwd", qa, ka, va, sinks[l], 4, False, scale_a)
        ob, lse_b = _attn_fwd("mla_fwd", qs, ks, vs, no_sink, 1, True, scale_b)
        mix = _merge_fwd(oa, ob, row(out_norm_swa[l]), row(out_norm_mla[l]))
        h1 = _mm_join("o_proj", mix, W_o, l, res=h)
        u2 = _rmsnorm_fwd("ffn_norm", h1, row(ffn_norm[l]))
        a = _mm_fan("gate_proj", u2, W_gate, l)
        b = _mm_fan("up_proj", u2, W_up, l)
        hm = unflat(_swiglu_fwd(flat(a), flat(b)))
        h2 = _mm_join("down_proj", hm, W_down, l, res=h1)
        saved.append((h, u, proj, qa, ka, va, qn, cn, qs, ks, vs, oa, lse_a, ob, lse_b, mix, h1, u2, a, b, hm))
        h = h2

    dh, d_final, loss_row = _loss_head(h, target, row(final_norm))

    gw = {k: [None] * L for k in ("in", "qup")}
    stacked = lambda t: lax.empty(t.shape, F32)
    g_kvup, g_o, g_gate, g_up, g_down = (stacked(t) for t in (W_kvup, W_o, W_gate, W_up, W_down))
    gs = {k: [None] * L for k in ("attn", "qn", "kvn", "sink", "ga", "gb", "ffn")}
    for l in reversed(range(L)):
        (h0, u, proj, qa, ka, va, qn, cn, qs, ks, vs, oa, lse_a, ob, lse_b, mix, h1, u2, a, b, hm) = saved[l]
        g_down = _mm_join_dw("down_dw", hm, dh, g_down, l)
        dhm = _mm_join_t("down_dx", dh, W_down, l, True)
        da, db = _swiglu_bwd(flat(a), flat(b), flat(dhm))
        da, db = unflat(da), unflat(db)
        g_gate = _mm_fan_dw("gate_dw", u2, da, g_gate, l)
        g_up = _mm_fan_dw("up_dw", u2, db, g_up, l)
        du2 = _mm_fan_t("gate_dx", da, W_gate, l)
        du2 = _mm_fan_t("up_dx", db, W_up, l, res=du2)
        dh1, gs["ffn"][l] = _rmsnorm_bwd("ffn_norm_bwd", h1, row(ffn_norm[l]), du2, dh)
        g_o = _mm_join_dw("o_dw", mix, dh1, g_o, l)
        dmix = _mm_join_t("o_dx", dh1, W_o, l, False)
        sink_slots = jnp.repeat(sinks[l], LANES).reshape(1, SLOT_W)
        doa, dla, dob, dlb, gs["ga"][l], gs["gb"][l], dsink = _merge_bwd(
            dmix, oa, ob, lse_a, row(out_norm_swa[l]), row(out_norm_mla[l]), sink_slots)
        gs["sink"][l] = dsink.reshape(SWA_HEADS, LANES)[:, 0]
        dqs, dks, dvs = _attn_bwd("mla_bwd", qs, ks, vs, dob, lse_b, dlb, 1, True, scale_b)
        dqa, dka, dva = _attn_bwd("swa_bwd", qa, ka, va, doa, lse_a, dla, 4, False, scale_a)
        dqb, dkvb, dkr = _prep2_bwd(dqs, dks, dvs, cosm, sinm)
        gw["qup"][l] = _mm("q_up_dw", qn, dqb, "tn")
        g_kvup = _mm_cols_dw("kv_up_dw", cn, dkvb, g_kvup, l)
        dqn = _mm("q_up_dx", dqb, W_qup[l], "nt")
        dcn = _mm_cols_t("kv_up_dx", dkvb, W_kvup, l)
        dproj, gs["qn"][l], gs["kvn"][l] = _prep1_bwd(
            proj, dqa, dka, dva, dqn, dcn, dkr, cosa, sina, row(q_norm[l]), row(kv_norm[l]))
        gw["in"][l] = _mm("in_dw", u, dproj, "tn")
        du = _mm("in_dx", dproj, W_in[l], "nt")
        dh, gs["attn"][l] = _rmsnorm_bwd("attn_norm_bwd", h0, row(attn_norm[l]), du, dh1)

    st = lambda k: jnp.stack(gw[k])
    d_in = st("in")
    d_in = jnp.concatenate([d_in[..., :C_KR], d_in[..., C_KR + KR_LANE:C_KR + KR_LANE + MLA_ROPE_DIM]], axis=-1)
    d_qup = st("qup").reshape(L, MLA_Q_RANK, MLA_HEADS, LANES)[..., :MLA_QK_DIM].reshape(L, MLA_Q_RANK, -1)
    grads = [_cols_to_chips(d_in), _cols_to_chips(d_qup), g_kvup, g_o, g_gate, g_up, g_down]
    return loss_row, dh, grads, gs, d_final


def kernel(x, meta_tokens, attn_norm, w_in, q_norm, w_q_up, kv_norm, w_kv_up, sinks, out_norm_swa, out_norm_mla, w_o, ffn_norm, w_gate, w_up, w_down, final_norm, loss_target, m_meta_tokens, m_attn_norm, m_w_in, m_q_norm, m_w_q_up, m_kv_norm, m_w_kv_up, m_sinks, m_out_norm_swa, m_out_norm_mla, m_w_o, m_ffn_norm, m_w_gate, m_w_up, m_w_down, m_final_norm, v_meta_tokens, v_attn_norm, v_w_in, v_q_norm, v_w_q_up, v_kv_norm, v_w_kv_up, v_sinks, v_out_norm_swa, v_out_norm_mla, v_w_o, v_ffn_norm, v_w_gate, v_w_up, v_w_down, v_final_norm):
    assert x.shape[0] == 1 and x.shape[1] % BLOCK == 0
    big = [w_in, w_q_up, w_kv_up, w_o, w_gate, w_up, w_down]

    c_idx = lax.axis_index("c").astype(jnp.int32)
    chip = (2 * lax.axis_index("x") + lax.axis_index("y")).astype(jnp.int32)
    g_in, g_qup, g_kvup, g_o, g_gate, g_up, g_down, g_meta = _gather_weights(
        [w.astype(BF16) for w in big], meta_tokens)
    weights = [_cols_from_chips(g_in), _cols_from_chips(g_qup), g_kvup, g_o, g_gate, g_up, g_down]
    meta_full = jnp.concatenate([g_meta[j] for j in range(4)], axis=-1)
    small_p = dict(attn_norm=attn_norm, q_norm=q_norm, kv_norm=kv_norm, sinks=sinks, out_norm_swa=out_norm_swa,
                   out_norm_mla=out_norm_mla, ffn_norm=ffn_norm, final_norm=final_norm)
    loss_row, dh, full, gs, d_final = _local_step(x[0], loss_target[0], meta_full, weights, small_p)
    grad_x = dh[BLOCK:][None]

    got = _pair_exchange([f.astype(BF16) for f in full])
    sums = [_pair_add(f, g, c_idx) for f, g in zip(full, got)]
    landed = _chip_scatter([s16 for _, s16 in sums])
    g_big = _pair_join([_chip_add(t, s32, chip, c_idx) for t, (s32, _) in zip(landed, sums)])

    small = [jnp.stack(gs["attn"]).reshape(-1), jnp.stack(gs["qn"]).reshape(-1), jnp.stack(gs["kvn"]).reshape(-1),
             jnp.stack(gs["sink"]).reshape(-1), jnp.stack(gs["ga"]).reshape(-1), jnp.stack(gs["gb"]).reshape(-1),
             jnp.stack(gs["ffn"]).reshape(-1), d_final.reshape(-1)]
    sizes = [s.shape[0] for s in small]
    flat = jnp.concatenate(small + [dh[FRONT:BLOCK].reshape(-1), loss_row[0, :1]])
    n_flat = flat.shape[0]
    rows_needed = -(-n_flat // (8 * LANES)) * 8
    flat = jnp.pad(flat, (0, rows_needed * LANES - n_flat)).reshape(rows_needed, LANES)
    tot = _allreduce_small(flat).reshape(-1)
    n_small = sum(sizes)
    loss = tot[n_small + N_META * D_MODEL]
    g_meta_full = tot[n_small:n_small + N_META * D_MODEL].reshape(N_META, D_MODEL)
    g_meta_mine = lax.dynamic_slice_in_dim(g_meta_full, chip * (D_MODEL // 4), D_MODEL // 4, axis=1)

    small_w = [attn_norm, q_norm, kv_norm, sinks, out_norm_swa, out_norm_mla, ffn_norm, final_norm]
    small_m = [m_attn_norm, m_q_norm, m_kv_norm, m_sinks, m_out_norm_swa, m_out_norm_mla, m_ffn_norm, m_final_norm]
    small_v = [v_attn_norm, v_q_norm, v_kv_norm, v_sinks, v_out_norm_swa, v_out_norm_mla, v_ffn_norm, v_final_norm]
    n_rows = -(-n_small // (8 * LANES)) * 8

    def pack(arrs):
        f = jnp.concatenate([a.reshape(-1) for a in arrs])
        return jnp.pad(f, (0, n_rows * LANES - n_small), constant_values=1.0).reshape(n_rows, LANES)

    g_small_pack = jnp.pad(tot[:n_small], (0, n_rows * LANES - n_small)).reshape(n_rows, LANES)
    upd_small = _adamw("adam_small", pack(small_w), g_small_pack, pack(small_m), pack(small_v))

    def unpack(p):
        f = p.reshape(-1)
        out, off = [], 0
        for a, n in zip(small_w, sizes):
            out.append(f[off:off + n].reshape(a.shape))
            off += n
        return out

    g_small = unpack(g_small_pack)
    d_small, m_small, v_small = [unpack(p) for p in upd_small]
    d_meta, nm_meta, nv_meta = _adamw("adam_meta", meta_tokens, g_meta_mine, m_meta_tokens, v_meta_tokens)

    big_m = [m_w_in, m_w_q_up, m_w_kv_up, m_w_o, m_w_gate, m_w_up, m_w_down]
    big_v = [v_w_in, v_w_q_up, v_w_kv_up, v_w_o, v_w_gate, v_w_up, v_w_down]
    upd_big = [_adamw("adam_big", w, g, m, v) for w, g, m, v in zip(big, g_big, big_m, big_v)]

    names = ["meta_tokens", "attn_norm", "w_in", "q_norm", "w_q_up", "kv_norm", "w_kv_up", "sinks",
             "out_norm_swa", "out_norm_mla", "w_o", "ffn_norm", "w_gate", "w_up", "w_down", "final_norm"]
    small_idx = {"attn_norm": 0, "q_norm": 1, "kv_norm": 2, "sinks": 3, "out_norm_swa": 4,
                 "out_norm_mla": 5, "ffn_norm": 6, "final_norm": 7}
    big_idx = {"w_in": 0, "w_q_up": 1, "w_kv_up": 2, "w_o": 3, "w_gate": 4, "w_up": 5, "w_down": 6}
    grads, deltas, new_m, new_v = [], [], [], []
    for nme in names:
        if nme == "meta_tokens":
            quad = (g_meta_mine, d_meta, nm_meta, nv_meta)
        elif nme in small_idx:
            i = small_idx[nme]
            quad = (g_small[i], d_small[i], m_small[i], v_small[i])
        else:
            i = big_idx[nme]
            quad = (g_big[i], *upd_big[i])
        grads.append(quad[0]); deltas.append(quad[1]); new_m.append(quad[2]); new_v.append(quad[3])
    return (loss, grad_x, *grads, *deltas, *new_m, *new_v)
```

```python
import jax
import jax.numpy as jnp
from jax import lax
from jax.experimental import pallas as pl
from jax.experimental.pallas import tpu as pltpu

F32 = jnp.float32
BF16 = jnp.bfloat16
MXU_DTYPE = BF16

D_MODEL = 1024
DEPTH = 4
N_META = 16
BLOCK = 128
WINDOW = 128
ROPE_THETA = 10000.0
EPS = 1e-6
NEG = -1e30
SWA_HEADS = 8
SWA_KV_HEADS = 2
SWA_HEAD_DIM = 64
MLA_HEADS = 8
MLA_Q_RANK = 256
MLA_KV_RANK = 128
MLA_NOPE_DIM = 64
MLA_ROPE_DIM = 32
MLA_V_DIM = 64
MLA_QK_DIM = MLA_NOPE_DIM + MLA_ROPE_DIM
D_FF = 2816
FRONT = (-N_META) % BLOCK
LANES = 128
SLOT_W = 8 * LANES
C_QA, C_KA, C_VA, C_QL, C_KL, C_KR, IN_WP = 0, 512, 640, 768, 1024, 1152, 1280
KR_LANE = 64
IN_W = 1184

ADAM_LR, ADAM_B1, ADAM_B2, ADAM_EPS, ADAM_WD, ADAM_STEP = 0.001, 0.9, 0.999, 1e-08, 0.01, 10

VMEM_LIMIT = 48 * 1024 * 1024
MESH = pl.DeviceIdType.MESH


def _tile(n, prefs):
    for t in prefs:
        if n % t == 0:
            return t
    return n


def _params(sem):
    return pltpu.CompilerParams(dimension_semantics=sem, vmem_limit_bytes=VMEM_LIMIT)


_DIMS = {"nn": (((1,), (0,)), ((), ())), "nt": (((1,), (1,)), ((), ())), "tn": (((0,), (0,)), ((), ()))}


def _mm(name, a, b, mode, out_dtype=F32, res=None):
    if mode == "nn":
        (M, K), (_, N) = a.shape, b.shape
    elif mode == "nt":
        (M, K), (N, _) = a.shape, b.shape
    else:
        (K, M), (_, N) = a.shape, b.shape
    lane_tiles = (1408, 1024, 640, 768, 512, 384, 256, 128)
    row_tiles = (528, 512, 384, 256, 128)
    bm = _tile(M, lane_tiles if mode == "tn" else row_tiles)
    bn = _tile(N, lane_tiles)
    bk = _tile(K, (1056,) + row_tiles if mode == "tn" else lane_tiles)
    nk = K // bk
    if mode == "tn":
        a_spec = pl.BlockSpec((bk, bm), lambda i, j, k: (k, i))
    else:
        a_spec = pl.BlockSpec((bm, bk), lambda i, j, k: (i, k))
    if mode == "nt":
        b_spec = pl.BlockSpec((bn, bk), lambda i, j, k: (j, k))
    else:
        b_spec = pl.BlockSpec((bk, bn), lambda i, j, k: (k, j))
    o_spec = pl.BlockSpec((bm, bn), lambda i, j, k: (i, j))
    in_specs = [a_spec, b_spec]
    args = [a, b]
    if res is not None:
        in_specs.append(o_spec)
        args.append(res)
    dims = _DIMS[mode]

    def kern(a_ref, b_ref, *rest):
        if res is not None:
            r_ref, o_ref = rest[0], rest[1]
            scr = rest[2:]
        else:
            r_ref, o_ref = None, rest[0]
            scr = rest[1:]
        p = lax.dot_general(a_ref[...].astype(MXU_DTYPE), b_ref[...].astype(MXU_DTYPE), dims,
                            preferred_element_type=F32)

        def finish(val):
            if r_ref is not None:
                val = val + r_ref[...]
            o_ref[...] = val.astype(o_ref.dtype)

        if nk == 1:
            finish(p)
        else:
            acc = scr[0]
            k = pl.program_id(2)

            @pl.when(k == 0)
            def _():
                acc[...] = p

            @pl.when(k > 0)
            def _():
                acc[...] += p

            @pl.when(k == nk - 1)
            def _():
                finish(acc[...])

    return pl.pallas_call(
        kern, name=name,
        out_shape=jax.ShapeDtypeStruct((M, N), out_dtype),
        grid=(M // bm, N // bn, nk),
        in_specs=in_specs, out_specs=o_spec,
        scratch_shapes=[pltpu.VMEM((bm, bn), F32)] if nk > 1 else [],
        compiler_params=_params(("parallel", "parallel", "arbitrary")),
    )(*args)


def _rowmap(name, body, rows, vecs, outs, accs=(), tr_prefs=(384, 256, 128)):
    R = rows[0].shape[0]
    tr = _tile(R, tr_prefs)
    n_r, n_v, n_o, n_a = len(rows), len(vecs), len(outs), len(accs)

    def kern(*refs):
        ins = [r[...] for r in refs[:n_r + n_v]]
        o_refs = refs[n_r + n_v:n_r + n_v + n_o]
        a_refs = refs[n_r + n_v + n_o:]
        res = body(*ins)
        for o_ref, val in zip(o_refs, res[:n_o]):
            o_ref[...] = val.astype(o_ref.dtype)
        if n_a:
            first = pl.program_id(0) == 0

            @pl.when(first)
            def _():
                for a_ref, val in zip(a_refs, res[n_o:]):
                    a_ref[...] = val

            @pl.when(jnp.logical_not(first))
            def _():
                for a_ref, val in zip(a_refs, res[n_o:]):
                    a_ref[...] += val

    in_specs = [pl.BlockSpec((tr, r.shape[1]), lambda i: (i, 0)) for r in rows]
    in_specs += [pl.BlockSpec((1, v.shape[1]), lambda i: (0, 0)) for v in vecs]
    out_specs = [pl.BlockSpec((tr, c), lambda i: (i, 0)) for c, _ in outs]
    out_specs += [pl.BlockSpec((1, c), lambda i: (0, 0)) for c in accs]
    out_shape = [jax.ShapeDtypeStruct((R, c), dt) for c, dt in outs]
    out_shape += [jax.ShapeDtypeStruct((1, c), F32) for c in accs]
    return pl.pallas_call(
        kern, name=name, out_shape=out_shape, grid=(R // tr,),
        in_specs=in_specs, out_specs=out_specs,
        compiler_params=_params(("arbitrary",) if n_a else ("parallel",)),
    )(*rows, *vecs)


def _lane(shape):
    return lax.broadcasted_iota(jnp.int32, shape, 1)


def _rot_swa(x):
    lane = _lane(x.shape)
    return jnp.where((lane & 63) < 32, -pltpu.roll(x, 96, 1), pltpu.roll(x, 32, 1))


def _rot_mla(x):
    lane = _lane(x.shape)
    lo = jnp.where(lane >= KR_LANE, -pltpu.roll(x, 112, 1), 0.0)
    hi = jnp.where(lane < KR_LANE + MLA_ROPE_DIM, pltpu.roll(x, 16, 1), 0.0)
    return jnp.where(lane < KR_LANE + 16, lo, hi)


def _rope(x, cos, sin, rot):
    return x * cos + rot(x) * sin


def _rope_t(g, cos, sin, rot):
    return g * cos - rot(g * sin)


def _low(x):
    return jnp.where(_lane(x.shape) < 64, x, 0.0)


def _blk(x, j):
    return x[:, j * LANES:(j + 1) * LANES]


def _rms_r(x, width):
    return lax.rsqrt(jnp.sum(x * x, axis=-1, keepdims=True) * (1.0 / width) + EPS)


def _rms_bwd(x, g, dy, width):
    r = _rms_r(x, width)
    gdy = dy * g
    dot = jnp.sum(gdy * x, axis=-1, keepdims=True)
    dx = r * gdy - x * (r * r * r * (1.0 / width) * dot)
    return dx, dy * x * r


def _colsum(x):
    return jnp.sum(x, axis=0, keepdims=True)


def _rmsnorm_fwd(name, x, g):
    def body(xv, gv):
        return (xv * _rms_r(xv, D_MODEL) * gv,)
    return _rowmap(name, body, [x], [g], [(D_MODEL, BF16)])[0]


def _rmsnorm_bwd(name, x, g, dy, dres):
    def body(xv, dyv, dr, gv):
        dx, dg = _rms_bwd(xv, gv, dyv, D_MODEL)
        return dx + dr, _colsum(dg)
    return _rowmap(name, body, [x, dy, dres], [g], [(D_MODEL, F32)], [D_MODEL])


def _prep1(proj, qn_g, kv_g, cosa, sina, cosm, sinm):
    def body(p, ca, sa, cm, sm, gq, gk):
        qa = []
        for j in range(4):
            xr = _rope(_blk(p, j), ca, sa, _rot_swa)
            qa += [_low(xr), _low(pltpu.roll(xr, 64, 1))]
        kr_ = _rope(_blk(p, C_KA // LANES), ca, sa, _rot_swa)
        ka = [_low(kr_), _low(pltpu.roll(kr_, 64, 1))]
        vv = _blk(p, C_VA // LANES)
        va = [_low(vv), _low(pltpu.roll(vv, 64, 1))]
        ql = p[:, C_QL:C_QL + MLA_Q_RANK]
        qn = ql * _rms_r(ql, MLA_Q_RANK) * gq
        kl = p[:, C_KL:C_KL + MLA_KV_RANK]
        cn = kl * _rms_r(kl, MLA_KV_RANK) * gk
        kr = _rope(_blk(p, C_KR // LANES), cm, sm, _rot_mla)
        return (jnp.concatenate(qa, 1), jnp.concatenate(ka, 1), jnp.concatenate(va, 1), qn, cn, kr)
    return _rowmap("prep1", body, [proj, cosa, sina, cosm, sinm], [qn_g, kv_g],
                   [(SLOT_W, BF16), (2 * LANES, BF16), (2 * LANES, BF16),
                    (MLA_Q_RANK, BF16), (MLA_KV_RANK, BF16), (LANES, F32)])


def _prep2(qb, kvb, kr, cosm, sinm):
    def body(q, kv, krv, cm, sm):
        qs, ks, vs = [], [], []
        for h in range(MLA_HEADS):
            qs.append(_rope(_blk(q, h), cm, sm, _rot_mla))
            kvh = _blk(kv, h)
            ks.append(_low(kvh) + krv)
            vs.append(_low(pltpu.roll(kvh, 64, 1)))
        return jnp.concatenate(qs, 1), jnp.concatenate(ks, 1), jnp.concatenate(vs, 1)
    return _rowmap("prep2", body, [qb, kvb, kr, cosm, sinm], [],
                   [(SLOT_W, BF16), (SLOT_W, BF16), (SLOT_W, BF16)])


def _compact(slots):
    return jnp.concatenate(
        [_blk(slots, 2 * j) + pltpu.roll(_blk(slots, 2 * j + 1), 64, 1) for j in range(4)], 1)


def _expand(nat):
    out = []
    for j in range(4):
        b = _blk(nat, j)
        out += [_low(b), _low(pltpu.roll(b, 64, 1))]
    return jnp.concatenate(out, 1)


def _merge_fwd(oa, ob, ga, gb):
    def body(a, b, gav, gbv):
        xa, xb = _compact(a), _compact(b)
        return (jnp.concatenate([xa * _rms_r(xa, 512) * gav, xb * _rms_r(xb, 512) * gbv], 1),)
    return _rowmap("merge_fwd", body, [oa, ob], [ga, gb], [(D_MODEL, BF16)])[0]


def _merge_bwd(dmix, oa, ob, lse_a, ga, gb, sink_slots):
    def body(dm, a, b, lse, gav, gbv, sk):
        outs = []
        accs = []
        for o, g, lo in ((a, gav, 0), (b, gbv, 512)):
            x = _compact(o)
            dx, dg = _rms_bwd(x, g, dm[:, lo:lo + 512], 512)
            do = _expand(dx)
            delta = jnp.concatenate(
                [jnp.broadcast_to(jnp.sum(_blk(do, h) * _blk(o, h), axis=-1, keepdims=True),
                                  (do.shape[0], LANES)) for h in range(8)], 1)
            outs += [do, delta]
            accs.append(_colsum(dg))
        dsink = _colsum(-jnp.exp(sk - lse) * outs[1])
        return (*outs, *accs, dsink)
    return _rowmap("merge_bwd", body, [dmix, oa, ob, lse_a], [ga, gb, sink_slots],
                   [(SLOT_W, BF16), (SLOT_W, F32), (SLOT_W, BF16), (SLOT_W, F32)],
                   [512, 512, SLOT_W])


def _prep2_bwd(dq, dk, dv, cosm, sinm):
    def body(dqv, dkv, dvv, cm, sm):
        dqb, dkvb = [], []
        krsum = jnp.zeros((dqv.shape[0], LANES), F32)
        for h in range(MLA_HEADS):
            dqb.append(_rope_t(_blk(dqv, h), cm, sm, _rot_mla))
            dkh = _blk(dkv, h)
            dkvb.append(_low(dkh) + pltpu.roll(_blk(dvv, h), 64, 1))
            krsum = krsum + dkh
        lane = _lane(krsum.shape)
        dkr = jnp.where((lane >= KR_LANE) & (lane < KR_LANE + MLA_ROPE_DIM),
                        _rope_t(krsum, cm, sm, _rot_mla), 0.0)
        return jnp.concatenate(dqb, 1), jnp.concatenate(dkvb, 1), dkr
    return _rowmap("prep2_bwd", body, [dq, dk, dv, cosm, sinm], [],
                   [(SLOT_W, BF16), (SLOT_W, BF16), (LANES, F32)])


def _prep1_bwd(proj, dqa, dka, dva, dqn, dcn, dkr, cosa, sina, qn_g, kv_g):
    def body(p, dq, dk, dv, dqnv, dcnv, dkrv, ca, sa, gq, gk):
        cols = []
        for j in range(4):
            nat = _blk(dq, 2 * j) + pltpu.roll(_blk(dq, 2 * j + 1), 64, 1)
            cols.append(_rope_t(nat, ca, sa, _rot_swa))
        grp = lambda d, g: sum(_blk(d, 4 * g + i) for i in range(4))
        cols.append(_rope_t(grp(dk, 0) + pltpu.roll(grp(dk, 1), 64, 1), ca, sa, _rot_swa))
        cols.append(grp(dv, 0) + pltpu.roll(grp(dv, 1), 64, 1))
        dql, dgq = _rms_bwd(p[:, C_QL:C_QL + MLA_Q_RANK], gq, dqnv, MLA_Q_RANK)
        dkl, dgk = _rms_bwd(p[:, C_KL:C_KL + MLA_KV_RANK], gk, dcnv, MLA_KV_RANK)
        cols += [dql, dkl, dkrv]
        return jnp.concatenate(cols, 1), _colsum(dgq), _colsum(dgk)
    return _rowmap("prep1_bwd", body, [proj, dqa, dka, dva, dqn, dcn, dkr, cosa, sina], [qn_g, kv_g],
                   [(IN_WP, BF16)], [MLA_Q_RANK, MLA_KV_RANK], tr_prefs=(192, 128))


def _sigmoid(x):
    return 1.0 / (1.0 + jnp.exp(-x))


def _ffn_tiles(T, F):
    return _tile(T, (528, 512, 384, 256, 128)), _tile(F, (1408, 1024, 512, 256, 128))


def _ffn_in(u, w_gate, w_up):
    (T, D), F = u.shape, w_gate.shape[1]
    bm, bn = _ffn_tiles(T, F)

    def kern(u_ref, wg_ref, wu_ref, a_ref, b_ref, h_ref):
        uv = u_ref[...]
        a = jnp.dot(uv, wg_ref[...], preferred_element_type=F32)
        b = jnp.dot(uv, wu_ref[...], preferred_element_type=F32)
        a_ref[...] = a
        b_ref[...] = b
        h_ref[...] = (a * _sigmoid(a) * b).astype(h_ref.dtype)

    w_spec = pl.BlockSpec((D, bn), lambda i, j: (0, j))
    o_spec = pl.BlockSpec((bm, bn), lambda i, j: (i, j))
    return pl.pallas_call(
        kern, name="ffn_in",
        out_shape=[jax.ShapeDtypeStruct((T, F), F32)] * 2 + [jax.ShapeDtypeStruct((T, F), MXU_DTYPE)],
        grid=(T // bm, F // bn),
        in_specs=[pl.BlockSpec((bm, D), lambda i, j: (i, 0)), w_spec, w_spec],
        out_specs=[o_spec, o_spec, o_spec],
        compiler_params=_params(("parallel", "parallel")),
    )(u, w_gate, w_up)


def _ffn_mid_bwd(dh, w_down, a, b):
    (T, D), F = dh.shape, w_down.shape[0]
    bm, bn = _ffn_tiles(T, F)

    def kern(dh_ref, wd_ref, a_ref, b_ref, da_ref, db_ref):
        d = lax.dot_general(dh_ref[...].astype(MXU_DTYPE), wd_ref[...], _DIMS["nt"], preferred_element_type=F32)
        av, bv = a_ref[...], b_ref[...]
        s = _sigmoid(av)
        da_ref[...] = (d * bv * (s * (1.0 + av * (1.0 - s)))).astype(da_ref.dtype)
        db_ref[...] = (d * (av * s)).astype(db_ref.dtype)

    o_spec = pl.BlockSpec((bm, bn), lambda i, j: (i, j))
    return pl.pallas_call(
        kern, name="ffn_mid_bwd",
        out_shape=[jax.ShapeDtypeStruct((T, F), MXU_DTYPE)] * 2,
        grid=(T // bm, F // bn),
        in_specs=[pl.BlockSpec((bm, D), lambda i, j: (i, 0)), pl.BlockSpec((bn, D), lambda i, j: (j, 0)),
                  o_spec, o_spec],
        out_specs=[o_spec, o_spec],
        compiler_params=_params(("parallel", "parallel")),
    )(dh, w_down, a, b)


def _loss_head(h, target, g):
    T = h.shape[0]
    nb = T // BLOCK

    def kern(h_ref, t_ref, g_ref, dh_ref, dg_ref, loss_ref, acc):
        i = pl.program_id(0)

        @pl.when(i == 0)
        def _():
            dh_ref[...] = jnp.zeros_like(dh_ref)
            dg_ref[...] = jnp.zeros_like(dg_ref)
            acc[...] = jnp.zeros_like(acc)

        @pl.when(i > 0)
        def _():
            x = h_ref[...]
            gv = g_ref[...]
            e = x * _rms_r(x, D_MODEL) * gv - t_ref[...]
            acc[...] += _colsum(e * e)
            dx, dg = _rms_bwd(x, gv, e * (1.0 / D_MODEL), D_MODEL)
            dh_ref[...] = dx
            dg_ref[...] += _colsum(dg)

        @pl.when(i == nb - 1)
        def _():
            tot = jnp.sum(acc[...], axis=-1, keepdims=True) * (0.5 / D_MODEL)
            loss_ref[...] = jnp.broadcast_to(tot, loss_ref.shape)

    return pl.pallas_call(
        kern, name="loss_head",
        out_shape=[jax.ShapeDtypeStruct((T, D_MODEL), F32), jax.ShapeDtypeStruct((1, D_MODEL), F32),
                   jax.ShapeDtypeStruct((1, LANES), F32)],
        grid=(nb,),
        in_specs=[pl.BlockSpec((BLOCK, D_MODEL), lambda i: (i, 0)),
                  pl.BlockSpec((BLOCK, D_MODEL), lambda i: (jnp.maximum(i - 1, 0), 0)),
                  pl.BlockSpec((1, D_MODEL), lambda i: (0, 0))],
        out_specs=[pl.BlockSpec((BLOCK, D_MODEL), lambda i: (i, 0)),
                   pl.BlockSpec((1, D_MODEL), lambda i: (0, 0)),
                   pl.BlockSpec((1, LANES), lambda i: (0, 0))],
        scratch_shapes=[pltpu.VMEM((1, D_MODEL), F32)],
        compiler_params=_params(("arbitrary",)),
    )(h, target, g)


LOG2E = 1.4426950408889634


def _attn_plan(T, causal):
    tq = _tile(T, (384, 256, 128))
    ck = min(2 * tq, T) if causal else min(tq + WINDOW, T)
    return tq, ck, (-(-T // ck) if causal else 1)


def _chunk(i, c, T, tq, ck, causal):
    if causal:
        return pl.multiple_of(jnp.minimum(c * ck, T - ck), LANES), c * ck
    return pl.multiple_of(jnp.clip(i * tq - WINDOW, 0, T - ck), LANES), 0


def _n_chunks(i, tq, ck, causal):
    return ((i + 1) * tq + ck - 1) // ck if causal else 1


def _mask(s, i, start, first, tq, ck, causal):
    qpos = i * tq + lax.broadcasted_iota(jnp.int32, (tq, 1), 0)
    kpos = start + lax.broadcasted_iota(jnp.int32, (tq, ck), 1)
    low = jnp.maximum(jnp.where(qpos < FRONT, 0, FRONT), first)
    if not causal:
        low = jnp.maximum(low, qpos - (WINDOW - 1))
    return jnp.where(kpos >= low, jnp.where(kpos <= qpos, s, NEG), NEG)


def _chunk_loop(n, body, init, causal):
    carry = body(0, init, True)
    if not causal:
        return carry
    carry = lax.fori_loop(1, n - 1, lambda c, cr: body(c, cr, False), carry)
    return lax.cond(n > 1, lambda cr: body(n - 1, cr, True), lambda cr: cr, carry)


FWD_HEADS_PER_STEP = 2
BWD_HEADS_PER_STEP = 1


def _head_cols(group, hp):
    q_cols = lambda hh: slice(hh * LANES, (hh + 1) * LANES)
    if group == 1:
        return q_cols, q_cols, hp * LANES
    assert group % hp == 0
    return q_cols, (lambda hh: slice(0, LANES)), LANES


def _whole(T, width, index, single):
    if single:
        return pl.BlockSpec((T, width), index, pipeline_mode=pl.Buffered(1))
    return pl.BlockSpec((T, width), index)


def _attn_fwd(name, q, k, v, sinks, group, causal, scale):
    T = q.shape[0]
    H = q.shape[1] // LANES
    HP = FWD_HEADS_PER_STEP
    tq, ck, slots = _attn_plan(T, causal)
    nq = T // tq
    c2 = scale * LOG2E
    q_cols, k_cols, kw = _head_cols(group, HP)

    def kern(sink_ref, q_ref, k_ref, v_ref, o_ref, lse_ref, s_scr):
        sink = [sink_ref[pl.program_id(0) * HP + hh] for hh in range(HP)]

        def q_tile(i, carry):
            rows = pl.ds(pl.multiple_of(i * tq, tq), tq)
            qq = [q_ref[rows, q_cols(hh)] for hh in range(HP)]
            n = _n_chunks(i, tq, ck, causal)

            def score(c, m, masked):
                start, first = _chunk(i, c, T, tq, ck, causal)
                out = []
                for hh in range(HP):
                    s = lax.dot_general(qq[hh], k_ref[pl.ds(start, ck), k_cols(hh)], _DIMS["nt"],
                                        preferred_element_type=F32)
                    if masked:
                        s = _mask(s, i, start, first, tq, ck, causal)
                    s_scr[hh, c] = s
                    out.append(jnp.maximum(m[hh], jnp.max(s, axis=-1, keepdims=True)))
                return tuple(out)

            m = _chunk_loop(n, score, tuple(jnp.full((tq, 1), sk * (1.0 / scale), F32) for sk in sink), causal)
            m2 = [mh * c2 for mh in m]

            def weigh(c, carry):
                start, _ = _chunk(i, c, T, tq, ck, causal)
                out = []
                for hh in range(HP):
                    l, acc = carry[hh]
                    p = jnp.exp2(s_scr[hh, c] * c2 - m2[hh])
                    acc = acc + jnp.dot(p.astype(MXU_DTYPE), v_ref[pl.ds(start, ck), k_cols(hh)],
                                        preferred_element_type=F32)
                    out.append((l + jnp.sum(p, axis=-1, keepdims=True), acc))
                return tuple(out)

            init = tuple((jnp.exp2(sink[hh] * LOG2E - m2[hh]), jnp.zeros((tq, LANES), F32)) for hh in range(HP))
            res = lax.fori_loop(0, n, weigh, init)
            for hh in range(HP):
                l, acc = res[hh]
                o_ref[rows, q_cols(hh)] = acc / l
                lse_ref[rows, q_cols(hh)] = jnp.broadcast_to(m[hh] * scale + jnp.log(l), (tq, LANES))
            return carry

        lax.fori_loop(0, nq, q_tile, 0)

    q_spec = _whole(T, HP * LANES, lambda g: (0, g), True)
    kv_spec = _whole(T, kw, (lambda g: (0, g)) if group == 1 else (lambda g: (0, (g * HP) // group)), True)
    return pl.pallas_call(
        kern, name=name,
        out_shape=[jax.ShapeDtypeStruct((T, H * LANES), F32)] * 2,
        grid=(H // HP,),
        in_specs=[pl.BlockSpec(memory_space=pltpu.SMEM), q_spec, kv_spec, kv_spec],
        out_specs=[q_spec, q_spec],
        scratch_shapes=[pltpu.VMEM((HP, slots, tq, ck), F32)],
        compiler_params=_params(("parallel",)),
    )(sinks, q, k, v)


def _attn_bwd(name, q, k, v, do, lse, delta, group, causal, scale):
    T = q.shape[0]
    H = q.shape[1] // LANES
    HP = BWD_HEADS_PER_STEP
    tq, ck, _ = _attn_plan(T, causal)
    nq = T // tq
    c2 = scale * LOG2E
    q_cols, k_cols, kw = _head_cols(group, HP)

    def kern(q_ref, k_ref, v_ref, do_ref, lse_ref, dl_ref, dq_ref, dk_ref, dv_ref):
        dk_ref[...] = jnp.zeros_like(dk_ref)
        dv_ref[...] = jnp.zeros_like(dv_ref)

        def q_tile(i, carry):
            rows = pl.ds(pl.multiple_of(i * tq, tq), tq)
            qq = [q_ref[rows, q_cols(hh)] for hh in range(HP)]
            dd = [do_ref[rows, q_cols(hh)] for hh in range(HP)]
            lse2 = [lse_ref[rows, q_cols(hh)][:, 0:1] * LOG2E for hh in range(HP)]
            dl_c = [dl_ref[rows, q_cols(hh)][:, 0:1] for hh in range(HP)]

            def chunk(c, dq, masked):
                start, first = _chunk(i, c, T, tq, ck, causal)
                keys = pl.ds(start, ck)
                out = []
                for hh in range(HP):
                    kk, vv = k_ref[keys, k_cols(hh)], v_ref[keys, k_cols(hh)]
                    s = lax.dot_general(qq[hh], kk, _DIMS["nt"], preferred_element_type=F32)
                    if masked:
                        s = _mask(s, i, start, first, tq, ck, causal)
                    p = jnp.exp2(s * c2 - lse2[hh])
                    dv_ref[keys, q_cols(hh)] += lax.dot_general(p.astype(MXU_DTYPE), dd[hh], _DIMS["tn"],
                                                                preferred_element_type=F32)
                    dp = lax.dot_general(dd[hh], vv, _DIMS["nt"], preferred_element_type=F32)
                    ds = (p * (dp - dl_c[hh])).astype(MXU_DTYPE)
                    dk_ref[keys, q_cols(hh)] += lax.dot_general(ds, qq[hh], _DIMS["tn"],
                                                                preferred_element_type=F32) * scale
                    out.append(dq[hh] + jnp.dot(ds, kk, preferred_element_type=F32))
                return tuple(out)

            dq = _chunk_loop(_n_chunks(i, tq, ck, causal), chunk,
                             tuple(jnp.zeros((tq, LANES), F32) for _ in range(HP)), causal)
            for hh in range(HP):
                dq_ref[rows, q_cols(hh)] = dq[hh] * scale
            return carry

        lax.fori_loop(0, nq, q_tile, 0)

    q_spec = _whole(T, HP * LANES, lambda g: (0, g), False)
    kv_spec = _whole(T, kw, (lambda g: (0, g)) if group == 1 else (lambda g: (0, (g * HP) // group)), False)
    return pl.pallas_call(
        kern, name=name,
        out_shape=[jax.ShapeDtypeStruct((T, H * LANES), F32)] * 3,
        grid=(H // HP,),
        in_specs=[q_spec, kv_spec, kv_spec, q_spec, q_spec, q_spec],
        out_specs=[q_spec, q_spec, q_spec],
        compiler_params=_params(("parallel",)),
    )(q, k, v, do, lse, delta)


def _ew(name, fn, ins, out_dtypes):
    shape = ins[0].shape
    flat = [a.reshape(-1, shape[-1]) for a in ins]
    R, C = flat[0].shape
    tr = _tile(R, (512, 256, 128, 64, 32, 16, 8))
    n_in = len(ins)

    def kern(*refs):
        res = fn(*[r[...] for r in refs[:n_in]])
        for o_ref, val in zip(refs[n_in:], res):
            o_ref[...] = val.astype(o_ref.dtype)

    spec = pl.BlockSpec((tr, C), lambda i: (i, 0))
    outs = pl.pallas_call(
        kern, name=name,
        out_shape=[jax.ShapeDtypeStruct((R, C), dt) for dt in out_dtypes],
        grid=(R // tr,), in_specs=[spec] * n_in, out_specs=[spec] * len(out_dtypes),
        compiler_params=_params(("parallel",)),
    )(*flat)
    return [o.reshape(shape) for o in outs]


def _adamw(name, w, g, m, v):
    c1 = 1.0 - ADAM_B1 ** ADAM_STEP
    c2 = 1.0 - ADAM_B2 ** ADAM_STEP

    def fn(wv, gv, mv, vv):
        mn = ADAM_B1 * mv + (1.0 - ADAM_B1) * gv
        vn = ADAM_B2 * vv + (1.0 - ADAM_B2) * (gv * gv)
        delta = -ADAM_LR * ((mn / c1) / (jnp.sqrt(vn / c2) + ADAM_EPS) + ADAM_WD * wv)
        return delta, mn, vn

    return _ew(name, fn, [w, g, m, v], [F32, F32, F32])


_ANY = pl.BlockSpec(memory_space=pl.ANY)


def _where_am_i():
    x, y, c = lax.axis_index("x"), lax.axis_index("y"), lax.axis_index("c")
    chips = [(1 - x, y), (x, 1 - y), (1 - x, 1 - y)]
    return x, y, c, chips


def _gather_weights(shards, meta):
    arrs = list(shards) + [meta]
    n = len(arrs)
    per = [2] * len(shards) + [8]

    def body(*refs):
        ins, outs = refs[:n], refs[n:2 * n]
        send1, recv1, send2, recv2 = refs[2 * n:]
        x, y, c, chips = _where_am_i()
        me = 2 * x + y

        def half(ref, k, cc):
            return ref.at[pl.ds(per[k] * cc, per[k])]

        first = []
        for k in range(n):
            for j, (cx, cy) in enumerate(chips):
                first.append(pltpu.make_async_remote_copy(
                    src_ref=half(ins[k], k, c), dst_ref=half(outs[k].at[me], k, c),
                    send_sem=send1.at[k, j], recv_sem=recv1.at[k, j],
                    device_id=(cx, cy, c), device_id_type=MESH))
        for cp in first:
            cp.start()
        passed = []
        for k in range(n):
            for j, (cx, cy) in enumerate(chips):
                landed = half(outs[k].at[2 * cx + cy], k, c)
                pltpu.make_async_remote_copy(
                    src_ref=landed, dst_ref=landed, send_sem=send1.at[k, j], recv_sem=recv1.at[k, j],
                    device_id=(cx, cy, c), device_id_type=MESH).wait_recv()
                fwd = pltpu.make_async_remote_copy(
                    src_ref=landed, dst_ref=landed, send_sem=send2.at[k, j], recv_sem=recv2.at[k, j],
                    device_id=(x, y, 1 - c), device_id_type=MESH)
                fwd.start()
                passed.append(fwd)
        for k in range(n):
            for j, (cx, cy) in enumerate(chips):
                other = half(outs[k].at[2 * cx + cy], k, 1 - c)
                pltpu.make_async_remote_copy(
                    src_ref=other, dst_ref=other, send_sem=send2.at[k, j], recv_sem=recv2.at[k, j],
                    device_id=(x, y, 1 - c), device_id_type=MESH).wait_recv()
        for cp in first + passed:
            cp.wait_send()

    return pl.pallas_call(
        body, name="gather_weights",
        out_shape=[jax.ShapeDtypeStruct((4,) + a.shape, a.dtype) for a in arrs],
        in_specs=[_ANY] * n, out_specs=[_ANY] * n,
        scratch_shapes=[pltpu.SemaphoreType.DMA((n, 3))] * 4,
    )(*arrs)


def _row_chunks(r):
    n = 4 if r % 64 == 0 else 1
    return [(q * (r // n), r // n) for q in range(n)]


def _pair_exchange(grads):
    n = len(grads)

    def body(*refs):
        ins, got = refs[:n], refs[n:2 * n]
        send, recv = refs[2 * n:]
        x, y, c, _ = _where_am_i()
        sib = (x, y, 1 - c)
        for k in range(n):
            for ch in range(4):
                for l in range(2):
                    pltpu.make_async_remote_copy(
                        src_ref=ins[k].at[ch, 2 * (1 - c) + l], dst_ref=got[k].at[ch, l],
                        send_sem=send.at[k], recv_sem=recv.at[k], device_id=sib, device_id_type=MESH).start()
        for k in range(n):
            pltpu.make_async_remote_copy(
                src_ref=got[k], dst_ref=got[k], send_sem=send.at[k], recv_sem=recv.at[k],
                device_id=sib, device_id_type=MESH).wait()

    return pl.pallas_call(
        body, name="reduce_pair",
        out_shape=[jax.ShapeDtypeStruct((4, 2) + g.shape[2:], g.dtype) for g in grads],
        in_specs=[_ANY] * n, out_specs=[_ANY] * n,
        scratch_shapes=[pltpu.SemaphoreType.DMA((n,))] * 2,
    )(*grads)


def _pair_add(full, got, c):
    _, _, r, cols = full.shape
    tr = _tile(r, (512, 256, 352, 128))

    def kern(c_ref, a_ref, b_ref, o32_ref, o16_ref):
        tot = a_ref[...] + b_ref[...].astype(F32)
        o32_ref[...] = tot
        o16_ref[...] = tot.astype(o16_ref.dtype)

    blk = (None, None, tr, cols)
    mine = pl.BlockSpec(blk, lambda ch, l, i, cr: (ch, 2 * cr[0] + l, i, 0))
    same = pl.BlockSpec(blk, lambda ch, l, i, cr: (ch, l, i, 0))
    return pl.pallas_call(
        kern, name="pair_add",
        out_shape=[jax.ShapeDtypeStruct(got.shape, F32), jax.ShapeDtypeStruct(got.shape, got.dtype)],
        grid_spec=pltpu.PrefetchScalarGridSpec(
            num_scalar_prefetch=1, grid=(4, 2, r // tr), in_specs=[mine, same], out_specs=[same, same]),
        compiler_params=_params(("parallel", "parallel", "parallel")),
    )(c.reshape(1), full, got)


def _chip_scatter(parts):
    n = len(parts)

    def body(*refs):
        ins, outs = refs[:n], refs[n:2 * n]
        send, recv = refs[2 * n:]
        x, y, c, chips = _where_am_i()
        me = 2 * x + y
        for k in range(n):
            for j, (cx, cy) in enumerate(chips):
                for l in range(2):
                    pltpu.make_async_remote_copy(
                        src_ref=ins[k].at[2 * cx + cy, l], dst_ref=outs[k].at[me, l],
                        send_sem=send.at[k, j], recv_sem=recv.at[k, j],
                        device_id=(cx, cy, c), device_id_type=MESH).start()
        for k in range(n):
            for j, (cx, cy) in enumerate(chips):
                slot = outs[k].at[2 * cx + cy]
                pltpu.make_async_remote_copy(
                    src_ref=slot, dst_ref=slot, send_sem=send.at[k, j], recv_sem=recv.at[k, j],
                    device_id=(cx, cy, c), device_id_type=MESH).wait()

    return pl.pallas_call(
        body, name="reduce_chips",
        out_shape=[jax.ShapeDtypeStruct(p.shape, p.dtype) for p in parts],
        in_specs=[_ANY] * n, out_specs=[_ANY] * n,
        scratch_shapes=[pltpu.SemaphoreType.DMA((n, 3))] * 2,
    )(*parts)


def _chip_add(landed, mine, me, c):
    _, _, r, cols = landed.shape
    tr = _tile(r, (512, 256, 352, 128))

    def kern(me_ref, c_ref, land_ref, own_ref, o_ref):
        own = own_ref[...]
        tot = None
        for j in range(4):
            term = jnp.where(me_ref[0] == j, own, land_ref[j].astype(F32))
            tot = term if tot is None else tot + term
        o_ref[...] = tot

    return pl.pallas_call(
        kern, name="chip_add",
        out_shape=jax.ShapeDtypeStruct((4, r, cols), F32),
        grid_spec=pltpu.PrefetchScalarGridSpec(
            num_scalar_prefetch=2, grid=(2, r // tr),
            in_specs=[pl.BlockSpec((4, None, tr, cols), lambda l, i, mr, cr: (0, l, i, 0)),
                      pl.BlockSpec((None, None, tr, cols), lambda l, i, mr, cr: (mr[0], l, i, 0))],
            out_specs=pl.BlockSpec((None, tr, cols), lambda l, i, mr, cr: (2 * cr[0] + l, i, 0))),
        compiler_params=_params(("parallel", "parallel")),
    )(me.reshape(1), c.reshape(1), landed, mine)


def _pair_join(sums):
    n = len(sums)

    def body(*refs):
        bufs = refs[n:2 * n]
        send, recv = refs[2 * n:]
        x, y, c, _ = _where_am_i()
        sib = (x, y, 1 - c)
        for k in range(n):
            for l in range(2):
                for r0, rn in _row_chunks(bufs[k].shape[1]):
                    piece = bufs[k].at[2 * c + l, pl.ds(r0, rn)]
                    pltpu.make_async_remote_copy(
                        src_ref=piece, dst_ref=piece, send_sem=send.at[k], recv_sem=recv.at[k],
                        device_id=sib, device_id_type=MESH).start()
        for k in range(n):
            theirs = bufs[k].at[pl.ds(2 * (1 - c), 2)]
            pltpu.make_async_remote_copy(
                src_ref=theirs, dst_ref=theirs, send_sem=send.at[k], recv_sem=recv.at[k],
                device_id=sib, device_id_type=MESH).wait()

    return pl.pallas_call(
        body, name="reduce_join",
        out_shape=[jax.ShapeDtypeStruct(s.shape, s.dtype) for s in sums],
        in_specs=[_ANY] * n, out_specs=[_ANY] * n,
        input_output_aliases={k: k for k in range(n)},
        scratch_shapes=[pltpu.SemaphoreType.DMA((n,))] * 2,
    )(*sums)


def _allreduce_small(buf):
    R = buf.shape[0]

    def body(in_ref, out_ref, land, send, recv):
        x, y, c, _ = _where_am_i()
        me = 4 * x + 2 * y + c
        land[me] = in_ref[...]
        cps = []
        for k in range(1, 8):
            px, py, pc = x ^ (k >> 2), y ^ ((k >> 1) & 1), c ^ (k & 1)
            cps.append(pltpu.make_async_remote_copy(
                src_ref=in_ref, dst_ref=land.at[me], send_sem=send.at[k - 1], recv_sem=recv.at[k - 1],
                device_id=(px, py, pc), device_id_type=MESH))
        for cp in cps:
            cp.start()
        for k in range(1, 8):
            px, py, pc = x ^ (k >> 2), y ^ ((k >> 1) & 1), c ^ (k & 1)
            slot = land.at[4 * px + 2 * py + pc]
            pltpu.make_async_remote_copy(
                src_ref=slot, dst_ref=slot, send_sem=send.at[k - 1], recv_sem=recv.at[k - 1],
                device_id=(px, py, pc), device_id_type=MESH).wait_recv()
        for cp in cps:
            cp.wait_send()
        tot = land[0]
        for d in range(1, 8):
            tot = tot + land[d]
        out_ref[...] = tot

    vm = pl.BlockSpec(memory_space=pltpu.VMEM)
    return pl.pallas_call(
        body, name="allreduce_small",
        out_shape=jax.ShapeDtypeStruct(buf.shape, F32),
        in_specs=[vm], out_specs=vm,
        scratch_shapes=[pltpu.VMEM((8, R, LANES), F32), pltpu.SemaphoreType.DMA((7,)),
                        pltpu.SemaphoreType.DMA((7,))],
    )(buf)


def _rope_tables(T):
    pos = (jnp.arange(T) - FRONT).astype(F32)
    lane = jnp.arange(LANES)
    inv_a = ROPE_THETA ** (-(2 * ((lane % 64) % 32)).astype(F32) / SWA_HEAD_DIM)
    ang_a = pos[:, None] * inv_a[None, :]
    cosa, sina = jnp.cos(ang_a), jnp.sin(ang_a)
    inv_m = ROPE_THETA ** (-(2 * ((lane - KR_LANE) % 16)).astype(F32) / MLA_ROPE_DIM)
    ang_m = pos[:, None] * inv_m[None, :]
    on = ((lane >= KR_LANE) & (lane < KR_LANE + MLA_ROPE_DIM))[None, :]
    cosm = jnp.where(on, jnp.cos(ang_m), 1.0)
    sinm = jnp.where(on, jnp.sin(ang_m), 0.0)
    return cosa, sina, cosm, sinm


def _cols_from_chips(g):
    return jnp.concatenate([g[j] for j in range(4)], axis=-1)


def _rows_from_chips(g):
    return jnp.concatenate([g[j] for j in range(4)], axis=-2)


def _cols_to_chips(w):
    L, r, c4 = w.shape
    return jnp.moveaxis(w.reshape(L, r, 4, c4 // 4), 2, 0)


def _rows_to_chips(w):
    L, r4, c = w.shape
    return jnp.moveaxis(w.reshape(L, 4, r4 // 4, c), 1, 0)


def _local_step(x2, target, meta_full, natural, p):
    T = BLOCK + x2.shape[0]
    L = DEPTH
    W_in, W_qup, W_kvup, W_o, W_gate, W_up, W_down = natural
    zpad = lambda n: jnp.zeros((L, D_MODEL, n), W_in.dtype)
    W_in = jnp.concatenate([W_in[..., :C_KR], zpad(KR_LANE), W_in[..., C_KR:IN_W],
                            zpad(LANES - KR_LANE - MLA_ROPE_DIM)], axis=-1)
    W_qup = W_qup.reshape(L, MLA_Q_RANK, MLA_HEADS, MLA_QK_DIM)
    W_qup = jnp.pad(W_qup, ((0, 0), (0, 0), (0, 0), (0, LANES - MLA_QK_DIM))).reshape(L, MLA_Q_RANK, SLOT_W)
    attn_norm, q_norm, kv_norm, sinks = p["attn_norm"], p["q_norm"], p["kv_norm"], p["sinks"]
    out_norm_swa, out_norm_mla, ffn_norm, final_norm = (
        p["out_norm_swa"], p["out_norm_mla"], p["ffn_norm"], p["final_norm"])

    cosa, sina, cosm, sinm = _rope_tables(T)
    no_sink = jnp.full((MLA_HEADS,), NEG, F32)
    scale_a, scale_b = SWA_HEAD_DIM ** -0.5, MLA_QK_DIM ** -0.5
    row = lambda v: v.reshape(1, -1)

    h = jnp.concatenate([jnp.zeros((FRONT, D_MODEL), F32), meta_full, x2], axis=0)
    saved = []
    for l in range(L):
        u = _rmsnorm_fwd("attn_norm", h, row(attn_norm[l]))
        proj = _mm("in_proj", u, W_in[l], "nn")
        qa, ka, va, qn, cn, kr = _prep1(proj, row(q_norm[l]), row(kv_norm[l]), cosa, sina, cosm, sinm)
        qb = _mm("q_up", qn, W_qup[l], "nn")
        kvb = _mm("kv_up", cn, W_kvup[l], "nn")
        qs, ks, vs = _prep2(qb, kvb, kr, cosm, sinm)
        oa, lse_a = _attn_fwd("swa_fwd", qa, ka, va, sinks[l], 4, False, scale_a)
        ob, lse_b = _attn_fwd("mla_fwd", qs, ks, vs, no_sink, 1, True, scale_b)
        mix = _merge_fwd(oa, ob, row(out_norm_swa[l]), row(out_norm_mla[l]))
        h1 = _mm("o_proj", mix, W_o[l], "nn", res=h)
        u2 = _rmsnorm_fwd("ffn_norm", h1, row(ffn_norm[l]))
        a, b, hm = _ffn_in(u2, W_gate[l], W_up[l])
        h2 = _mm("down_proj", hm, W_down[l], "nn", res=h1)
        saved.append((h, u, proj, qa, ka, va, qn, cn, qs, ks, vs, oa, lse_a, ob, lse_b, mix, h1, u2, a, b, hm))
        h = h2

    dh, d_final, loss_row = _loss_head(h, target, row(final_norm))

    gw = {k: [None] * L for k in ("in", "qup", "kvup", "o", "gate", "up", "down")}
    gs = {k: [None] * L for k in ("attn", "qn", "kvn", "sink", "ga", "gb", "ffn")}
    for l in reversed(range(L)):
        (h0, u, proj, qa, ka, va, qn, cn, qs, ks, vs, oa, lse_a, ob, lse_b, mix, h1, u2, a, b, hm) = saved[l]
        gw["down"][l] = _mm("down_dw", hm, dh, "tn")
        da, db = _ffn_mid_bwd(dh, W_down[l], a, b)
        gw["gate"][l] = _mm("gate_dw", u2, da, "tn")
        gw["up"][l] = _mm("up_dw", u2, db, "tn")
        du2 = _mm("gate_dx", da, W_gate[l], "nt")
        du2 = _mm("up_dx", db, W_up[l], "nt", res=du2)
        dh1, gs["ffn"][l] = _rmsnorm_bwd("ffn_norm_bwd", h1, row(ffn_norm[l]), du2, dh)
        gw["o"][l] = _mm("o_dw", mix, dh1, "tn")
        dmix = _mm("o_dx", dh1, W_o[l], "nt")
        sink_slots = jnp.repeat(sinks[l], LANES).reshape(1, SLOT_W)
        doa, dla, dob, dlb, gs["ga"][l], gs["gb"][l], dsink = _merge_bwd(
            dmix, oa, ob, lse_a, row(out_norm_swa[l]), row(out_norm_mla[l]), sink_slots)
        gs["sink"][l] = dsink.reshape(SWA_HEADS, LANES)[:, 0]
        dqs, dks, dvs = _attn_bwd("mla_bwd", qs, ks, vs, dob, lse_b, dlb, 1, True, scale_b)
        dqa, dka, dva = _attn_bwd("swa_bwd", qa, ka, va, doa, lse_a, dla, 4, False, scale_a)
        dqb, dkvb, dkr = _prep2_bwd(dqs, dks, dvs, cosm, sinm)
        gw["qup"][l] = _mm("q_up_dw", qn, dqb, "tn")
        gw["kvup"][l] = _mm("kv_up_dw", cn, dkvb, "tn")
        dqn = _mm("q_up_dx", dqb, W_qup[l], "nt")
        dcn = _mm("kv_up_dx", dkvb, W_kvup[l], "nt")
        dproj, gs["qn"][l], gs["kvn"][l] = _prep1_bwd(
            proj, dqa, dka, dva, dqn, dcn, dkr, cosa, sina, row(q_norm[l]), row(kv_norm[l]))
        gw["in"][l] = _mm("in_dw", u, dproj, "tn")
        du = _mm("in_dx", dproj, W_in[l], "nt")
        dh, gs["attn"][l] = _rmsnorm_bwd("attn_norm_bwd", h0, row(attn_norm[l]), du, dh1)

    st = lambda k: jnp.stack(gw[k])
    d_in = st("in")
    d_in = jnp.concatenate([d_in[..., :C_KR], d_in[..., C_KR + KR_LANE:C_KR + KR_LANE + MLA_ROPE_DIM]], axis=-1)
    d_qup = st("qup").reshape(L, MLA_Q_RANK, MLA_HEADS, LANES)[..., :MLA_QK_DIM].reshape(L, MLA_Q_RANK, -1)
    d_nat = [d_in, d_qup, st("kvup"), st("o"), st("gate"), st("up"), st("down")]
    return loss_row, dh, d_nat, gs, d_final


def kernel(x, meta_tokens, attn_norm, w_in, q_norm, w_q_up, kv_norm, w_kv_up, sinks, out_norm_swa, out_norm_mla, w_o, ffn_norm, w_gate, w_up, w_down, final_norm, loss_target, m_meta_tokens, m_attn_norm, m_w_in, m_q_norm, m_w_q_up, m_kv_norm, m_w_kv_up, m_sinks, m_out_norm_swa, m_out_norm_mla, m_w_o, m_ffn_norm, m_w_gate, m_w_up, m_w_down, m_final_norm, v_meta_tokens, v_attn_norm, v_w_in, v_q_norm, v_w_q_up, v_kv_norm, v_w_kv_up, v_sinks, v_out_norm_swa, v_out_norm_mla, v_w_o, v_ffn_norm, v_w_gate, v_w_up, v_w_down, v_final_norm):
    assert x.shape[0] == 1 and x.shape[1] % BLOCK == 0
    big = [w_in, w_q_up, w_kv_up, w_o, w_gate, w_up, w_down]

    c_idx = lax.axis_index("c").astype(jnp.int32)
    chip = (2 * lax.axis_index("x") + lax.axis_index("y")).astype(jnp.int32)
    mine = [w.astype(BF16) for w in big] + [meta_tokens]
    gathered = _gather_weights(mine[:-1], mine[-1])
    parts = [[jnp.where(chip == j, own, g[j]) for j in range(4)] for own, g in zip(mine, gathered)]
    join = [_cols_from_chips, _cols_from_chips, _cols_from_chips, _rows_from_chips, _cols_from_chips,
            _cols_from_chips, _rows_from_chips]
    natural = [f(p) for f, p in zip(join, parts[:-1])]
    meta_full = jnp.concatenate(parts[-1], axis=-1)
    small_p = dict(attn_norm=attn_norm, q_norm=q_norm, kv_norm=kv_norm, sinks=sinks, out_norm_swa=out_norm_swa,
                   out_norm_mla=out_norm_mla, ffn_norm=ffn_norm, final_norm=final_norm)
    loss_row, dh, d_nat, gs, d_final = _local_step(x[0], loss_target[0], meta_full, natural, small_p)
    grad_x = dh[BLOCK:][None]

    split = [_cols_to_chips, _cols_to_chips, _cols_to_chips, _rows_to_chips, _cols_to_chips, _cols_to_chips,
             _rows_to_chips]
    full = [f(d) for f, d in zip(split, d_nat)]

    got = _pair_exchange([f.astype(BF16) for f in full])
    sums = [_pair_add(f, g, c_idx) for f, g in zip(full, got)]
    landed = _chip_scatter([s16 for _, s16 in sums])
    g_big = _pair_join([_chip_add(t, s32, chip, c_idx) for t, (s32, _) in zip(landed, sums)])

    small = [jnp.stack(gs["attn"]).reshape(-1), jnp.stack(gs["qn"]).reshape(-1), jnp.stack(gs["kvn"]).reshape(-1),
             jnp.stack(gs["sink"]).reshape(-1), jnp.stack(gs["ga"]).reshape(-1), jnp.stack(gs["gb"]).reshape(-1),
             jnp.stack(gs["ffn"]).reshape(-1), d_final.reshape(-1)]
    sizes = [s.shape[0] for s in small]
    flat = jnp.concatenate(small + [dh[FRONT:BLOCK].reshape(-1), loss_row[0, :1]])
    n_flat = flat.shape[0]
    rows_needed = -(-n_flat // (8 * LANES)) * 8
    flat = jnp.pad(flat, (0, rows_needed * LANES - n_flat)).reshape(rows_needed, LANES)
    tot = _allreduce_small(flat).reshape(-1)
    n_small = sum(sizes)
    loss = tot[n_small + N_META * D_MODEL]
    g_meta_full = tot[n_small:n_small + N_META * D_MODEL].reshape(N_META, D_MODEL)
    g_meta_mine = lax.dynamic_slice_in_dim(g_meta_full, chip * (D_MODEL // 4), D_MODEL // 4, axis=1)

    small_w = [attn_norm, q_norm, kv_norm, sinks, out_norm_swa, out_norm_mla, ffn_norm, final_norm]
    small_m = [m_attn_norm, m_q_norm, m_kv_norm, m_sinks, m_out_norm_swa, m_out_norm_mla, m_ffn_norm, m_final_norm]
    small_v = [v_attn_norm, v_q_norm, v_kv_norm, v_sinks, v_out_norm_swa, v_out_norm_mla, v_ffn_norm, v_final_norm]
    n_rows = -(-n_small // (8 * LANES)) * 8

    def pack(arrs):
        f = jnp.concatenate([a.reshape(-1) for a in arrs])
        return jnp.pad(f, (0, n_rows * LANES - n_small), constant_values=1.0).reshape(n_rows, LANES)

    g_small_pack = jnp.pad(tot[:n_small], (0, n_rows * LANES - n_small)).reshape(n_rows, LANES)
    upd_small = _adamw("adam_small", pack(small_w), g_small_pack, pack(small_m), pack(small_v))

    def unpack(p):
        f = p.reshape(-1)
        out, off = [], 0
        for a, n in zip(small_w, sizes):
            out.append(f[off:off + n].reshape(a.shape))
            off += n
        return out

    g_small = unpack(g_small_pack)
    d_small, m_small, v_small = [unpack(p) for p in upd_small]
    d_meta, nm_meta, nv_meta = _adamw("adam_meta", meta_tokens, g_meta_mine, m_meta_tokens, v_meta_tokens)

    big_m = [m_w_in, m_w_q_up, m_w_kv_up, m_w_o, m_w_gate, m_w_up, m_w_down]
    big_v = [v_w_in, v_w_q_up, v_w_kv_up, v_w_o, v_w_gate, v_w_up, v_w_down]
    upd_big = [_adamw("adam_big", w, g, m, v) for w, g, m, v in zip(big, g_big, big_m, big_v)]

    names = ["meta_tokens", "attn_norm", "w_in", "q_norm", "w_q_up", "kv_norm", "w_kv_up", "sinks",
             "out_norm_swa", "out_norm_mla", "w_o", "ffn_norm", "w_gate", "w_up", "w_down", "final_norm"]
    small_idx = {"attn_norm": 0, "q_norm": 1, "kv_norm": 2, "sinks": 3, "out_norm_swa": 4,
                 "out_norm_mla": 5, "ffn_norm": 6, "final_norm": 7}
    big_idx = {"w_in": 0, "w_q_up": 1, "w_kv_up": 2, "w_o": 3, "w_gate": 4, "w_up": 5, "w_down": 6}
    grads, deltas, new_m, new_v = [], [], [], []
    for nme in names:
        if nme == "meta_tokens":
            quad = (g_meta_mine, d_meta, nm_meta, nv_meta)
        elif nme in small_idx:
            i = small_idx[nme]
            quad = (g_small[i], d_small[i], m_small[i], v_small[i])
        else:
            i = big_idx[nme]
            quad = (g_big[i], *upd_big[i])
        grads.append(quad[0]); deltas.append(quad[1]); new_m.append(quad[2]); new_v.append(quad[3])
    return (loss, grad_x, *grads, *deltas, *new_m, *new_v)
```

```python
import jax
import jax.numpy as jnp
from jax import lax
from jax.experimental import pallas as pl
from jax.experimental.pallas import tpu as pltpu

F32 = jnp.float32
BF16 = jnp.bfloat16
MXU_DTYPE = BF16

D_MODEL = 1024
DEPTH = 4
N_META = 16
BLOCK = 128
WINDOW = 128
ROPE_THETA = 10000.0
EPS = 1e-6
NEG = -1e30
SWA_HEADS = 8
SWA_KV_HEADS = 2
SWA_HEAD_DIM = 64
MLA_HEADS = 8
MLA_Q_RANK = 256
MLA_KV_RANK = 128
MLA_NOPE_DIM = 64
MLA_ROPE_DIM = 32
MLA_V_DIM = 64
MLA_QK_DIM = MLA_NOPE_DIM + MLA_ROPE_DIM
D_FF = 2816
FRONT = (-N_META) % BLOCK
LANES = 128
SLOT_W = 8 * LANES
C_QA, C_KA, C_VA, C_QL, C_KL, C_KR, IN_WP = 0, 512, 640, 768, 1024, 1152, 1280
KR_LANE = 64
IN_W = 1184

ADAM_LR, ADAM_B1, ADAM_B2, ADAM_EPS, ADAM_WD, ADAM_STEP = 0.001, 0.9, 0.999, 1e-08, 0.01, 10

VMEM_LIMIT = 48 * 1024 * 1024
MESH = pl.DeviceIdType.MESH


def _tile(n, prefs):
    for t in prefs:
        if n % t == 0:
            return t
    return n


def _params(sem):
    return pltpu.CompilerParams(dimension_semantics=sem, vmem_limit_bytes=VMEM_LIMIT)


_DIMS = {"nn": (((1,), (0,)), ((), ())), "nt": (((1,), (1,)), ((), ())), "tn": (((0,), (0,)), ((), ()))}


def _mm(name, a, b, mode, out_dtype=F32, res=None):
    if mode == "nn":
        (M, K), (_, N) = a.shape, b.shape
    elif mode == "nt":
        (M, K), (N, _) = a.shape, b.shape
    else:
        (K, M), (_, N) = a.shape, b.shape
    lane_tiles = (1408, 1024, 640, 768, 512, 384, 256, 128)
    row_tiles = (528, 512, 384, 256, 128)
    bm = _tile(M, lane_tiles if mode == "tn" else row_tiles)
    bn = _tile(N, lane_tiles)
    bk = _tile(K, (1056,) + row_tiles if mode == "tn" else lane_tiles)
    nk = K // bk
    if mode == "tn":
        a_spec = pl.BlockSpec((bk, bm), lambda i, j, k: (k, i))
    else:
        a_spec = pl.BlockSpec((bm, bk), lambda i, j, k: (i, k))
    if mode == "nt":
        b_spec = pl.BlockSpec((bn, bk), lambda i, j, k: (j, k))
    else:
        b_spec = pl.BlockSpec((bk, bn), lambda i, j, k: (k, j))
    o_spec = pl.BlockSpec((bm, bn), lambda i, j, k: (i, j))
    in_specs = [a_spec, b_spec]
    args = [a, b]
    if res is not None:
        in_specs.append(o_spec)
        args.append(res)
    dims = _DIMS[mode]

    def kern(a_ref, b_ref, *rest):
        if res is not None:
            r_ref, o_ref = rest[0], rest[1]
            scr = rest[2:]
        else:
            r_ref, o_ref = None, rest[0]
            scr = rest[1:]
        p = lax.dot_general(a_ref[...].astype(MXU_DTYPE), b_ref[...].astype(MXU_DTYPE), dims,
                            preferred_element_type=F32)

        def finish(val):
            if r_ref is not None:
                val = val + r_ref[...]
            o_ref[...] = val.astype(o_ref.dtype)

        if nk == 1:
            finish(p)
        else:
            acc = scr[0]
            k = pl.program_id(2)

            @pl.when(k == 0)
            def _():
                acc[...] = p

            @pl.when(k > 0)
            def _():
                acc[...] += p

            @pl.when(k == nk - 1)
            def _():
                finish(acc[...])

    return pl.pallas_call(
        kern, name=name,
        out_shape=jax.ShapeDtypeStruct((M, N), out_dtype),
        grid=(M // bm, N // bn, nk),
        in_specs=in_specs, out_specs=o_spec,
        scratch_shapes=[pltpu.VMEM((bm, bn), F32)] if nk > 1 else [],
        compiler_params=_params(("parallel", "parallel", "arbitrary")),
    )(*args)


def _rowmap(name, body, rows, vecs, outs, accs=(), tr_prefs=(384, 256, 128)):
    R = rows[0].shape[0]
    tr = _tile(R, tr_prefs)
    n_r, n_v, n_o, n_a = len(rows), len(vecs), len(outs), len(accs)

    def kern(*refs):
        ins = [r[...] for r in refs[:n_r + n_v]]
        o_refs = refs[n_r + n_v:n_r + n_v + n_o]
        a_refs = refs[n_r + n_v + n_o:]
        res = body(*ins)
        for o_ref, val in zip(o_refs, res[:n_o]):
            o_ref[...] = val.astype(o_ref.dtype)
        if n_a:
            first = pl.program_id(0) == 0

            @pl.when(first)
            def _():
                for a_ref, val in zip(a_refs, res[n_o:]):
                    a_ref[...] = val

            @pl.when(jnp.logical_not(first))
            def _():
                for a_ref, val in zip(a_refs, res[n_o:]):
                    a_ref[...] += val

    in_specs = [pl.BlockSpec((tr, r.shape[1]), lambda i: (i, 0)) for r in rows]
    in_specs += [pl.BlockSpec((1, v.shape[1]), lambda i: (0, 0)) for v in vecs]
    out_specs = [pl.BlockSpec((tr, c), lambda i: (i, 0)) for c, _ in outs]
    out_specs += [pl.BlockSpec((1, c), lambda i: (0, 0)) for c in accs]
    out_shape = [jax.ShapeDtypeStruct((R, c), dt) for c, dt in outs]
    out_shape += [jax.ShapeDtypeStruct((1, c), F32) for c in accs]
    return pl.pallas_call(
        kern, name=name, out_shape=out_shape, grid=(R // tr,),
        in_specs=in_specs, out_specs=out_specs,
        compiler_params=_params(("arbitrary",) if n_a else ("parallel",)),
    )(*rows, *vecs)


def _lane(shape):
    return lax.broadcasted_iota(jnp.int32, shape, 1)


def _rot_swa(x):
    lane = _lane(x.shape)
    return jnp.where((lane & 63) < 32, -pltpu.roll(x, 96, 1), pltpu.roll(x, 32, 1))


def _rot_mla(x):
    lane = _lane(x.shape)
    lo = jnp.where(lane >= KR_LANE, -pltpu.roll(x, 112, 1), 0.0)
    hi = jnp.where(lane < KR_LANE + MLA_ROPE_DIM, pltpu.roll(x, 16, 1), 0.0)
    return jnp.where(lane < KR_LANE + 16, lo, hi)


def _rope(x, cos, sin, rot):
    return x * cos + rot(x) * sin


def _rope_t(g, cos, sin, rot):
    return g * cos - rot(g * sin)


def _low(x):
    return jnp.where(_lane(x.shape) < 64, x, 0.0)


def _value_slot(x):
    lane = _lane(x.shape)
    return jnp.where(lane < 64, x, jnp.where(lane == 64, 1.0, 0.0))


def _blk(x, j):
    return x[:, j * LANES:(j + 1) * LANES]


def _rms_r(x, width):
    return lax.rsqrt(jnp.sum(x * x, axis=-1, keepdims=True) * (1.0 / width) + EPS)


def _rms_bwd(x, g, dy, width):
    r = _rms_r(x, width)
    gdy = dy * g
    dot = jnp.sum(gdy * x, axis=-1, keepdims=True)
    dx = r * gdy - x * (r * r * r * (1.0 / width) * dot)
    return dx, dy * x * r


def _colsum(x):
    return jnp.sum(x, axis=0, keepdims=True)


def _rmsnorm_bwd(name, x, g, dy, dres):
    def body(xv, dyv, dr, gv):
        dx, dg = _rms_bwd(xv, gv, dyv, D_MODEL)
        return dx + dr, _colsum(dg)
    return _rowmap(name, body, [x, dy, dres], [g], [(D_MODEL, F32)], [D_MODEL])


def _prep1(proj, qn_g, kv_g, cosa, sina, cosm, sinm):
    def body(p, ca, sa, cm, sm, gq, gk):
        qa = []
        for j in range(4):
            xr = _rope(_blk(p, j), ca, sa, _rot_swa)
            qa += [_low(xr), _low(pltpu.roll(xr, 64, 1))]
        kr_ = _rope(_blk(p, C_KA // LANES), ca, sa, _rot_swa)
        ka = [_low(kr_), _low(pltpu.roll(kr_, 64, 1))]
        vv = _blk(p, C_VA // LANES)
        va = [_value_slot(vv), _value_slot(pltpu.roll(vv, 64, 1))]
        ql = p[:, C_QL:C_QL + MLA_Q_RANK]
        qn = ql * _rms_r(ql, MLA_Q_RANK) * gq
        kl = p[:, C_KL:C_KL + MLA_KV_RANK]
        cn = kl * _rms_r(kl, MLA_KV_RANK) * gk
        kr = _rope(_blk(p, C_KR // LANES), cm, sm, _rot_mla)
        return (jnp.concatenate(qa, 1), jnp.concatenate(ka, 1), jnp.concatenate(va, 1), qn, cn, kr)
    return _rowmap("prep1", body, [proj, cosa, sina, cosm, sinm], [qn_g, kv_g],
                   [(SLOT_W, BF16), (2 * LANES, BF16), (2 * LANES, BF16),
                    (MLA_Q_RANK, BF16), (MLA_KV_RANK, BF16), (LANES, F32)])


def _prep2(qb, kvb, kr, cosm, sinm):
    def body(q, kv, krv, cm, sm):
        qs, ks, vs = [], [], []
        for h in range(MLA_HEADS):
            qs.append(_rope(_blk(q, h), cm, sm, _rot_mla))
            kvh = _blk(kv, h)
            ks.append(_low(kvh) + krv)
            vs.append(_value_slot(pltpu.roll(kvh, 64, 1)))
        return jnp.concatenate(qs, 1), jnp.concatenate(ks, 1), jnp.concatenate(vs, 1)
    return _rowmap("prep2", body, [qb, kvb, kr, cosm, sinm], [],
                   [(SLOT_W, BF16), (SLOT_W, BF16), (SLOT_W, BF16)])


def _compact(slots):
    return jnp.concatenate(
        [_blk(slots, 2 * j) + pltpu.roll(_blk(slots, 2 * j + 1), 64, 1) for j in range(4)], 1)


def _expand(nat):
    out = []
    for j in range(4):
        b = _blk(nat, j)
        out += [_low(b), _low(pltpu.roll(b, 64, 1))]
    return jnp.concatenate(out, 1)


def _merge_fwd(oa, ob, ga, gb):
    def body(a, b, gav, gbv):
        xa, xb = _compact(a), _compact(b)
        return (jnp.concatenate([xa * _rms_r(xa, 512) * gav, xb * _rms_r(xb, 512) * gbv], 1),)
    return _rowmap("merge_fwd", body, [oa, ob], [ga, gb], [(D_MODEL, BF16)])[0]


def _merge_bwd(dmix, oa, ob, lse_a, ga, gb, sink_slots):
    def body(dm, a, b, lse, gav, gbv, sk):
        outs = []
        accs = []
        for o, g, lo in ((a, gav, 0), (b, gbv, 512)):
            x = _compact(o)
            dx, dg = _rms_bwd(x, g, dm[:, lo:lo + 512], 512)
            do = _expand(dx)
            delta = jnp.concatenate(
                [jnp.broadcast_to(jnp.sum(_blk(do, h) * _blk(o, h), axis=-1, keepdims=True),
                                  (do.shape[0], LANES)) for h in range(8)], 1)
            outs += [do, delta]
            accs.append(_colsum(dg))
        dsink = _colsum(-jnp.exp(sk - lse) * outs[1])
        return (*outs, *accs, dsink)
    return _rowmap("merge_bwd", body, [dmix, oa, ob, lse_a], [ga, gb, sink_slots],
                   [(SLOT_W, BF16), (SLOT_W, F32), (SLOT_W, BF16), (SLOT_W, F32)],
                   [512, 512, SLOT_W])


def _prep2_bwd(dq, dk, dv, cosm, sinm):
    def body(dqv, dkv, dvv, cm, sm):
        dqb, dkvb = [], []
        krsum = jnp.zeros((dqv.shape[0], LANES), F32)
        for h in range(MLA_HEADS):
            dqb.append(_rope_t(_blk(dqv, h), cm, sm, _rot_mla))
            dkh = _blk(dkv, h)
            dkvb.append(_low(dkh) + pltpu.roll(_blk(dvv, h), 64, 1))
            krsum = krsum + dkh
        lane = _lane(krsum.shape)
        dkr = jnp.where((lane >= KR_LANE) & (lane < KR_LANE + MLA_ROPE_DIM),
                        _rope_t(krsum, cm, sm, _rot_mla), 0.0)
        return jnp.concatenate(dqb, 1), jnp.concatenate(dkvb, 1), dkr
    return _rowmap("prep2_bwd", body, [dq, dk, dv, cosm, sinm], [],
                   [(SLOT_W, BF16), (SLOT_W, BF16), (LANES, F32)])


def _prep1_bwd(proj, dqa, dka, dva, dqn, dcn, dkr, cosa, sina, qn_g, kv_g):
    def body(p, dq, dk, dv, dqnv, dcnv, dkrv, ca, sa, gq, gk):
        cols = []
        for j in range(4):
            nat = _blk(dq, 2 * j) + pltpu.roll(_blk(dq, 2 * j + 1), 64, 1)
            cols.append(_rope_t(nat, ca, sa, _rot_swa))
        grp = lambda d, g: sum(_blk(d, 4 * g + i) for i in range(4))
        cols.append(_rope_t(grp(dk, 0) + pltpu.roll(grp(dk, 1), 64, 1), ca, sa, _rot_swa))
        cols.append(grp(dv, 0) + pltpu.roll(grp(dv, 1), 64, 1))
        dql, dgq = _rms_bwd(p[:, C_QL:C_QL + MLA_Q_RANK], gq, dqnv, MLA_Q_RANK)
        dkl, dgk = _rms_bwd(p[:, C_KL:C_KL + MLA_KV_RANK], gk, dcnv, MLA_KV_RANK)
        cols += [dql, dkl, dkrv]
        return jnp.concatenate(cols, 1), _colsum(dgq), _colsum(dgk)
    return _rowmap("prep1_bwd", body, [proj, dqa, dka, dva, dqn, dcn, dkr, cosa, sina], [qn_g, kv_g],
                   [(IN_WP, BF16)], [MLA_Q_RANK, MLA_KV_RANK], tr_prefs=(192, 128))


def _sigmoid(x):
    return 1.0 / (1.0 + jnp.exp(-x))


def _ffn_tiles(T, F):
    return _tile(T, (528, 512, 384, 256, 128)), _tile(F, (1408, 1024, 640, 512, 256, 128))


def _norm_proj(name, h, g, weights, swiglu):
    (T, D), F = h.shape, weights[0].shape[1]
    bm, bn = _ffn_tiles(T, F)
    nw = len(weights)

    def kern(h_ref, g_ref, *rest):
        w_refs, u_ref, o_refs, u_scr = rest[:nw], rest[nw], rest[nw + 1:-1], rest[-1]

        @pl.when(pl.program_id(1) == 0)
        def _():
            x = h_ref[...]
            u = (x * _rms_r(x, D) * g_ref[...]).astype(u_scr.dtype)
            u_scr[...] = u
            u_ref[...] = u

        uv = u_scr[...]
        prods = [jnp.dot(uv, w[...], preferred_element_type=F32) for w in w_refs]
        for o_ref, p in zip(o_refs, prods):
            o_ref[...] = p
        if swiglu:
            a, b = prods
            o_refs[nw][...] = (a * _sigmoid(a) * b).astype(o_refs[nw].dtype)

    w_spec = pl.BlockSpec((D, bn), lambda i, j: (0, j))
    row_spec = pl.BlockSpec((bm, D), lambda i, j: (i, 0))
    o_spec = pl.BlockSpec((bm, bn), lambda i, j: (i, j))
    n_out = nw + (1 if swiglu else 0)
    return pl.pallas_call(
        kern, name=name,
        out_shape=[jax.ShapeDtypeStruct((T, D), MXU_DTYPE)] + [jax.ShapeDtypeStruct((T, F), F32)] * nw
        + ([jax.ShapeDtypeStruct((T, F), MXU_DTYPE)] if swiglu else []),
        grid=(T // bm, F // bn),
        in_specs=[row_spec, pl.BlockSpec((1, D), lambda i, j: (0, 0))] + [w_spec] * nw,
        out_specs=[row_spec] + [o_spec] * n_out,
        scratch_shapes=[pltpu.VMEM((bm, D), MXU_DTYPE)],
        compiler_params=_params(("parallel", "arbitrary")),
    )(h, g, *weights)


def _ffn_mid_bwd(dh, w_down, a, b):
    (T, D), F = dh.shape, w_down.shape[0]
    bm, bn = _ffn_tiles(T, F)

    def kern(dh_ref, wd_ref, a_ref, b_ref, da_ref, db_ref):
        d = lax.dot_general(dh_ref[...].astype(MXU_DTYPE), wd_ref[...], _DIMS["nt"], preferred_element_type=F32)
        av, bv = a_ref[...], b_ref[...]
        s = _sigmoid(av)
        da_ref[...] = (d * bv * (s * (1.0 + av * (1.0 - s)))).astype(da_ref.dtype)
        db_ref[...] = (d * (av * s)).astype(db_ref.dtype)

    o_spec = pl.BlockSpec((bm, bn), lambda i, j: (i, j))
    return pl.pallas_call(
        kern, name="ffn_mid_bwd",
        out_shape=[jax.ShapeDtypeStruct((T, F), MXU_DTYPE)] * 2,
        grid=(T // bm, F // bn),
        in_specs=[pl.BlockSpec((bm, D), lambda i, j: (i, 0)), pl.BlockSpec((bn, D), lambda i, j: (j, 0)),
                  o_spec, o_spec],
        out_specs=[o_spec, o_spec],
        compiler_params=_params(("parallel", "parallel")),
    )(dh, w_down, a, b)


def _loss_head(h, target, g):
    T = h.shape[0]
    nb = T // BLOCK

    def kern(h_ref, t_ref, g_ref, dh_ref, dg_ref, loss_ref, acc):
        i = pl.program_id(0)

        @pl.when(i == 0)
        def _():
            dh_ref[...] = jnp.zeros_like(dh_ref)
            dg_ref[...] = jnp.zeros_like(dg_ref)
            acc[...] = jnp.zeros_like(acc)

        @pl.when(i > 0)
        def _():
            x = h_ref[...]
            gv = g_ref[...]
            e = x * _rms_r(x, D_MODEL) * gv - t_ref[...]
            acc[...] += _colsum(e * e)
            dx, dg = _rms_bwd(x, gv, e * (1.0 / D_MODEL), D_MODEL)
            dh_ref[...] = dx
            dg_ref[...] += _colsum(dg)

        @pl.when(i == nb - 1)
        def _():
            tot = jnp.sum(acc[...], axis=-1, keepdims=True) * (0.5 / D_MODEL)
            loss_ref[...] = jnp.broadcast_to(tot, loss_ref.shape)

    return pl.pallas_call(
        kern, name="loss_head",
        out_shape=[jax.ShapeDtypeStruct((T, D_MODEL), F32), jax.ShapeDtypeStruct((1, D_MODEL), F32),
                   jax.ShapeDtypeStruct((1, LANES), F32)],
        grid=(nb,),
        in_specs=[pl.BlockSpec((BLOCK, D_MODEL), lambda i: (i, 0)),
                  pl.BlockSpec((BLOCK, D_MODEL), lambda i: (jnp.maximum(i - 1, 0), 0)),
                  pl.BlockSpec((1, D_MODEL), lambda i: (0, 0))],
        out_specs=[pl.BlockSpec((BLOCK, D_MODEL), lambda i: (i, 0)),
                   pl.BlockSpec((1, D_MODEL), lambda i: (0, 0)),
                   pl.BlockSpec((1, LANES), lambda i: (0, 0))],
        scratch_shapes=[pltpu.VMEM((1, D_MODEL), F32)],
        compiler_params=_params(("arbitrary",)),
    )(h, target, g)


LOG2E = 1.4426950408889634


def _attn_plan(T, causal):
    tq = _tile(T, (384, 256, 128))
    ck = min(2 * tq, T) if causal else min(tq + WINDOW, T)
    return tq, ck, (-(-T // ck) if causal else 1)


def _chunk(i, c, T, tq, ck, causal):
    if causal:
        return pl.multiple_of(jnp.minimum(c * ck, T - ck), LANES), c * ck
    return pl.multiple_of(jnp.clip(i * tq - WINDOW, 0, T - ck), LANES), 0


def _n_chunks(i, tq, ck, causal):
    return ((i + 1) * tq + ck - 1) // ck if causal else 1


def _mask(s, i, start, first, tq, ck, causal):
    qpos = i * tq + lax.broadcasted_iota(jnp.int32, (tq, 1), 0)
    kpos = start + lax.broadcasted_iota(jnp.int32, (tq, ck), 1)
    low = jnp.maximum(jnp.where(qpos < FRONT, 0, FRONT), first)
    if not causal:
        low = jnp.maximum(low, qpos - (WINDOW - 1))
    return jnp.where(kpos >= low, jnp.where(kpos <= qpos, s, NEG), NEG)


def _chunk_loop(n, body, init, causal):
    carry = body(0, init, True)
    if not causal:
        return carry
    carry = lax.fori_loop(1, n - 1, lambda c, cr: body(c, cr, False), carry)
    return lax.cond(n > 1, lambda cr: body(n - 1, cr, True), lambda cr: cr, carry)


FWD_HEADS_PER_STEP = 2
BWD_HEADS_PER_STEP = 1


def _head_cols(group, hp):
    q_cols = lambda hh: slice(hh * LANES, (hh + 1) * LANES)
    if group == 1:
        return q_cols, q_cols, hp * LANES
    assert group % hp == 0
    return q_cols, (lambda hh: slice(0, LANES)), LANES


def _whole(T, width, index, single):
    if single:
        return pl.BlockSpec((T, width), index, pipeline_mode=pl.Buffered(1))
    return pl.BlockSpec((T, width), index)


def _attn_fwd(name, q, k, v, sinks, group, causal, scale):
    T = q.shape[0]
    H = q.shape[1] // LANES
    HP = FWD_HEADS_PER_STEP
    tq, ck, slots = _attn_plan(T, causal)
    nq = T // tq
    c2 = scale * LOG2E
    q_cols, k_cols, kw = _head_cols(group, HP)

    def kern(sink_ref, q_ref, k_ref, v_ref, o_ref, lse_ref, s_scr):
        sink2 = [sink_ref[pl.program_id(0) * HP + hh] * LOG2E for hh in range(HP)]

        def q_tile(i, carry):
            rows = pl.ds(pl.multiple_of(i * tq, tq), tq)
            qq = [q_ref[rows, q_cols(hh)] for hh in range(HP)]
            n = _n_chunks(i, tq, ck, causal)

            def score(c, m2, masked):
                start, first = _chunk(i, c, T, tq, ck, causal)
                out = []
                for hh in range(HP):
                    s = lax.dot_general(qq[hh], k_ref[pl.ds(start, ck), k_cols(hh)], _DIMS["nt"],
                                        preferred_element_type=F32) * c2
                    if masked:
                        s = _mask(s, i, start, first, tq, ck, causal)
                    s_scr[hh, c] = s
                    out.append(jnp.maximum(m2[hh], jnp.max(s, axis=-1, keepdims=True)))
                return tuple(out)

            m2 = _chunk_loop(n, score, tuple(jnp.full((tq, 1), sk, F32) for sk in sink2), causal)

            def weigh(c, acc):
                start, _ = _chunk(i, c, T, tq, ck, causal)
                out = []
                for hh in range(HP):
                    p = jnp.exp2(s_scr[hh, c] - m2[hh])
                    out.append(acc[hh] + jnp.dot(p.astype(MXU_DTYPE), v_ref[pl.ds(start, ck), k_cols(hh)],
                                                 preferred_element_type=F32))
                return tuple(out)

            acc = lax.fori_loop(0, n, weigh, tuple(jnp.zeros((tq, LANES), F32) for _ in range(HP)))
            lane = _lane((tq, LANES))
            for hh in range(HP):
                l = acc[hh][:, 64:65] + jnp.exp2(sink2[hh] - m2[hh])
                o_ref[rows, q_cols(hh)] = jnp.where(lane < 64, acc[hh] / l, 0.0)
                lse_ref[rows, q_cols(hh)] = jnp.broadcast_to(m2[hh] * (1.0 / LOG2E) + jnp.log(l), (tq, LANES))
            return carry

        lax.fori_loop(0, nq, q_tile, 0)

    q_spec = _whole(T, HP * LANES, lambda g: (0, g), True)
    kv_spec = _whole(T, kw, (lambda g: (0, g)) if group == 1 else (lambda g: (0, (g * HP) // group)), True)
    return pl.pallas_call(
        kern, name=name,
        out_shape=[jax.ShapeDtypeStruct((T, H * LANES), F32)] * 2,
        grid=(H // HP,),
        in_specs=[pl.BlockSpec(memory_space=pltpu.SMEM), q_spec, kv_spec, kv_spec],
        out_specs=[q_spec, q_spec],
        scratch_shapes=[pltpu.VMEM((HP, slots, tq, ck), F32)],
        compiler_params=_params(("parallel",)),
    )(sinks, q, k, v)


def _attn_bwd(name, q, k, v, do, lse, delta, group, causal, scale):
    T = q.shape[0]
    H = q.shape[1] // LANES
    HP = BWD_HEADS_PER_STEP
    tq, ck, _ = _attn_plan(T, causal)
    nq = T // tq
    c2 = scale * LOG2E
    q_cols, k_cols, kw = _head_cols(group, HP)

    def kern(q_ref, k_ref, v_ref, do_ref, lse_ref, dl_ref, dq_ref, dk_ref, dv_ref):
        dk_ref[...] = jnp.zeros_like(dk_ref)
        dv_ref[...] = jnp.zeros_like(dv_ref)

        def q_tile(i, carry):
            rows = pl.ds(pl.multiple_of(i * tq, tq), tq)
            qq = [q_ref[rows, q_cols(hh)] for hh in range(HP)]
            dd = [do_ref[rows, q_cols(hh)] for hh in range(HP)]
            lse2 = [lse_ref[rows, q_cols(hh)][:, 0:1] * LOG2E for hh in range(HP)]
            dl_c = [dl_ref[rows, q_cols(hh)][:, 0:1] for hh in range(HP)]

            def chunk(c, dq, masked):
                start, first = _chunk(i, c, T, tq, ck, causal)
                keys = pl.ds(start, ck)
                out = []
                for hh in range(HP):
                    kk, vv = k_ref[keys, k_cols(hh)], v_ref[keys, k_cols(hh)]
                    s = lax.dot_general(qq[hh], kk, _DIMS["nt"], preferred_element_type=F32)
                    if masked:
                        s = _mask(s, i, start, first, tq, ck, causal)
                    p = jnp.exp2(s * c2 - lse2[hh])
                    dv_ref[keys, q_cols(hh)] += lax.dot_general(p.astype(MXU_DTYPE), dd[hh], _DIMS["tn"],
                                                                preferred_element_type=F32)
                    dp = lax.dot_general(dd[hh], vv, _DIMS["nt"], preferred_element_type=F32)
                    ds = (p * (dp - dl_c[hh])).astype(MXU_DTYPE)
                    dk_ref[keys, q_cols(hh)] += lax.dot_general(ds, qq[hh], _DIMS["tn"],
                                                                preferred_element_type=F32) * scale
                    out.append(dq[hh] + jnp.dot(ds, kk, preferred_element_type=F32))
                return tuple(out)

            dq = _chunk_loop(_n_chunks(i, tq, ck, causal), chunk,
                             tuple(jnp.zeros((tq, LANES), F32) for _ in range(HP)), causal)
            for hh in range(HP):
                dq_ref[rows, q_cols(hh)] = dq[hh] * scale
            return carry

        lax.fori_loop(0, nq, q_tile, 0)

    q_spec = _whole(T, HP * LANES, lambda g: (0, g), False)
    kv_spec = _whole(T, kw, (lambda g: (0, g)) if group == 1 else (lambda g: (0, (g * HP) // group)), False)
    return pl.pallas_call(
        kern, name=name,
        out_shape=[jax.ShapeDtypeStruct((T, H * LANES), F32)] * 3,
        grid=(H // HP,),
        in_specs=[q_spec, kv_spec, kv_spec, q_spec, q_spec, q_spec],
        out_specs=[q_spec, q_spec, q_spec],
        compiler_params=_params(("parallel",)),
    )(q, k, v, do, lse, delta)


def _ew(name, fn, ins, out_dtypes):
    shape = ins[0].shape
    flat = [a.reshape(-1, shape[-1]) for a in ins]
    R, C = flat[0].shape
    tr = _tile(R, (512, 256, 128, 64, 32, 16, 8))
    n_in = len(ins)

    def kern(*refs):
        res = fn(*[r[...] for r in refs[:n_in]])
        for o_ref, val in zip(refs[n_in:], res):
            o_ref[...] = val.astype(o_ref.dtype)

    spec = pl.BlockSpec((tr, C), lambda i: (i, 0))
    outs = pl.pallas_call(
        kern, name=name,
        out_shape=[jax.ShapeDtypeStruct((R, C), dt) for dt in out_dtypes],
        grid=(R // tr,), in_specs=[spec] * n_in, out_specs=[spec] * len(out_dtypes),
        compiler_params=_params(("parallel",)),
    )(*flat)
    return [o.reshape(shape) for o in outs]


def _adamw(name, w, g, m, v):
    c1 = 1.0 - ADAM_B1 ** ADAM_STEP
    c2 = 1.0 - ADAM_B2 ** ADAM_STEP

    def fn(wv, gv, mv, vv):
        mn = ADAM_B1 * mv + (1.0 - ADAM_B1) * gv
        vn = ADAM_B2 * vv + (1.0 - ADAM_B2) * (gv * gv)
        delta = -ADAM_LR * ((mn / c1) / (jnp.sqrt(vn / c2) + ADAM_EPS) + ADAM_WD * wv)
        return delta, mn, vn

    return _ew(name, fn, [w, g, m, v], [F32, F32, F32])


_ANY = pl.BlockSpec(memory_space=pl.ANY)


def _where_am_i():
    x, y, c = lax.axis_index("x"), lax.axis_index("y"), lax.axis_index("c")
    chips = [(1 - x, y), (x, 1 - y), (1 - x, 1 - y)]
    return x, y, c, chips


def _gather_weights(shards, meta):
    arrs = list(shards) + [meta]
    n = len(arrs)
    per = [2] * len(shards) + [8]

    def body(*refs):
        ins, outs = refs[:n], refs[n:2 * n]
        send1, recv1, send2, recv2 = refs[2 * n:]
        x, y, c, chips = _where_am_i()
        me = 2 * x + y

        def half(ref, k, cc):
            return ref.at[pl.ds(per[k] * cc, per[k])]

        first = []
        for k in range(n):
            for j, (cx, cy) in enumerate(chips):
                first.append(pltpu.make_async_remote_copy(
                    src_ref=half(ins[k], k, c), dst_ref=half(outs[k].at[me], k, c),
                    send_sem=send1.at[k, j], recv_sem=recv1.at[k, j],
                    device_id=(cx, cy, c), device_id_type=MESH))
        for cp in first:
            cp.start()
        passed = []
        for k in range(n):
            for j, (cx, cy) in enumerate(chips):
                landed = half(outs[k].at[2 * cx + cy], k, c)
                pltpu.make_async_remote_copy(
                    src_ref=landed, dst_ref=landed, send_sem=send1.at[k, j], recv_sem=recv1.at[k, j],
                    device_id=(cx, cy, c), device_id_type=MESH).wait_recv()
                fwd = pltpu.make_async_remote_copy(
                    src_ref=landed, dst_ref=landed, send_sem=send2.at[k, j], recv_sem=recv2.at[k, j],
                    device_id=(x, y, 1 - c), device_id_type=MESH)
                fwd.start()
                passed.append(fwd)
        for k in range(n):
            for j, (cx, cy) in enumerate(chips):
                other = half(outs[k].at[2 * cx + cy], k, 1 - c)
                pltpu.make_async_remote_copy(
                    src_ref=other, dst_ref=other, send_sem=send2.at[k, j], recv_sem=recv2.at[k, j],
                    device_id=(x, y, 1 - c), device_id_type=MESH).wait_recv()
        for cp in first + passed:
            cp.wait_send()

    return pl.pallas_call(
        body, name="gather_weights",
        out_shape=[jax.ShapeDtypeStruct((4,) + a.shape, a.dtype) for a in arrs],
        in_specs=[_ANY] * n, out_specs=[_ANY] * n,
        scratch_shapes=[pltpu.SemaphoreType.DMA((n, 3))] * 4,
    )(*arrs)


def _row_chunks(r):
    n = 4 if r % 64 == 0 else 1
    return [(q * (r // n), r // n) for q in range(n)]


def _pair_exchange(grads):
    n = len(grads)

    def body(*refs):
        ins, got = refs[:n], refs[n:2 * n]
        send, recv = refs[2 * n:]
        x, y, c, _ = _where_am_i()
        sib = (x, y, 1 - c)
        for k in range(n):
            for ch in range(4):
                for l in range(2):
                    pltpu.make_async_remote_copy(
                        src_ref=ins[k].at[ch, 2 * (1 - c) + l], dst_ref=got[k].at[ch, l],
                        send_sem=send.at[k], recv_sem=recv.at[k], device_id=sib, device_id_type=MESH).start()
        for k in range(n):
            pltpu.make_async_remote_copy(
                src_ref=got[k], dst_ref=got[k], send_sem=send.at[k], recv_sem=recv.at[k],
                device_id=sib, device_id_type=MESH).wait()

    return pl.pallas_call(
        body, name="reduce_pair",
        out_shape=[jax.ShapeDtypeStruct((4, 2) + g.shape[2:], g.dtype) for g in grads],
        in_specs=[_ANY] * n, out_specs=[_ANY] * n,
        scratch_shapes=[pltpu.SemaphoreType.DMA((n,))] * 2,
    )(*grads)


def _pair_add(full, got, c):
    _, _, r, cols = full.shape
    tr = _tile(r, (512, 256, 352, 128))

    def kern(c_ref, a_ref, b_ref, o32_ref, o16_ref):
        tot = a_ref[...] + b_ref[...].astype(F32)
        o32_ref[...] = tot
        o16_ref[...] = tot.astype(o16_ref.dtype)

    blk = (None, None, tr, cols)
    mine = pl.BlockSpec(blk, lambda ch, l, i, cr: (ch, 2 * cr[0] + l, i, 0))
    same = pl.BlockSpec(blk, lambda ch, l, i, cr: (ch, l, i, 0))
    return pl.pallas_call(
        kern, name="pair_add",
        out_shape=[jax.ShapeDtypeStruct(got.shape, F32), jax.ShapeDtypeStruct(got.shape, got.dtype)],
        grid_spec=pltpu.PrefetchScalarGridSpec(
            num_scalar_prefetch=1, grid=(4, 2, r // tr), in_specs=[mine, same], out_specs=[same, same]),
        compiler_params=_params(("parallel", "parallel", "parallel")),
    )(c.reshape(1), full, got)


def _chip_scatter(parts):
    n = len(parts)

    def body(*refs):
        ins, outs = refs[:n], refs[n:2 * n]
        send, recv = refs[2 * n:]
        x, y, c, chips = _where_am_i()
        me = 2 * x + y
        for k in range(n):
            for j, (cx, cy) in enumerate(chips):
                for l in range(2):
                    pltpu.make_async_remote_copy(
                        src_ref=ins[k].at[2 * cx + cy, l], dst_ref=outs[k].at[me, l],
                        send_sem=send.at[k, j], recv_sem=recv.at[k, j],
                        device_id=(cx, cy, c), device_id_type=MESH).start()
        for k in range(n):
            for j, (cx, cy) in enumerate(chips):
                slot = outs[k].at[2 * cx + cy]
                pltpu.make_async_remote_copy(
                    src_ref=slot, dst_ref=slot, send_sem=send.at[k, j], recv_sem=recv.at[k, j],
                    device_id=(cx, cy, c), device_id_type=MESH).wait()

    return pl.pallas_call(
        body, name="reduce_chips",
        out_shape=[jax.ShapeDtypeStruct(p.shape, p.dtype) for p in parts],
        in_specs=[_ANY] * n, out_specs=[_ANY] * n,
        scratch_shapes=[pltpu.SemaphoreType.DMA((n, 3))] * 2,
    )(*parts)


def _chip_add(landed, mine, me, c):
    _, _, r, cols = landed.shape
    tr = _tile(r, (512, 256, 352, 128))

    def kern(me_ref, c_ref, land_ref, own_ref, o_ref):
        own = own_ref[...]
        tot = None
        for j in range(4):
            term = jnp.where(me_ref[0] == j, own, land_ref[j].astype(F32))
            tot = term if tot is None else tot + term
        o_ref[...] = tot

    return pl.pallas_call(
        kern, name="chip_add",
        out_shape=jax.ShapeDtypeStruct((4, r, cols), F32),
        grid_spec=pltpu.PrefetchScalarGridSpec(
            num_scalar_prefetch=2, grid=(2, r // tr),
            in_specs=[pl.BlockSpec((4, None, tr, cols), lambda l, i, mr, cr: (0, l, i, 0)),
                      pl.BlockSpec((None, None, tr, cols), lambda l, i, mr, cr: (mr[0], l, i, 0))],
            out_specs=pl.BlockSpec((None, tr, cols), lambda l, i, mr, cr: (2 * cr[0] + l, i, 0))),
        compiler_params=_params(("parallel", "parallel")),
    )(me.reshape(1), c.reshape(1), landed, mine)


def _pair_join(sums):
    n = len(sums)

    def body(*refs):
        bufs = refs[n:2 * n]
        send, recv = refs[2 * n:]
        x, y, c, _ = _where_am_i()
        sib = (x, y, 1 - c)
        for k in range(n):
            for l in range(2):
                for r0, rn in _row_chunks(bufs[k].shape[1]):
                    piece = bufs[k].at[2 * c + l, pl.ds(r0, rn)]
                    pltpu.make_async_remote_copy(
                        src_ref=piece, dst_ref=piece, send_sem=send.at[k], recv_sem=recv.at[k],
                        device_id=sib, device_id_type=MESH).start()
        for k in range(n):
            theirs = bufs[k].at[pl.ds(2 * (1 - c), 2)]
            pltpu.make_async_remote_copy(
                src_ref=theirs, dst_ref=theirs, send_sem=send.at[k], recv_sem=recv.at[k],
                device_id=sib, device_id_type=MESH).wait()

    return pl.pallas_call(
        body, name="reduce_join",
        out_shape=[jax.ShapeDtypeStruct(s.shape, s.dtype) for s in sums],
        in_specs=[_ANY] * n, out_specs=[_ANY] * n,
        input_output_aliases={k: k for k in range(n)},
        scratch_shapes=[pltpu.SemaphoreType.DMA((n,))] * 2,
    )(*sums)


def _allreduce_small(buf):
    R = buf.shape[0]

    def body(in_ref, out_ref, land, send, recv):
        x, y, c, _ = _where_am_i()
        me = 4 * x + 2 * y + c
        land[me] = in_ref[...]
        cps = []
        for k in range(1, 8):
            px, py, pc = x ^ (k >> 2), y ^ ((k >> 1) & 1), c ^ (k & 1)
            cps.append(pltpu.make_async_remote_copy(
                src_ref=in_ref, dst_ref=land.at[me], send_sem=send.at[k - 1], recv_sem=recv.at[k - 1],
                device_id=(px, py, pc), device_id_type=MESH))
        for cp in cps:
            cp.start()
        for k in range(1, 8):
            px, py, pc = x ^ (k >> 2), y ^ ((k >> 1) & 1), c ^ (k & 1)
            slot = land.at[4 * px + 2 * py + pc]
            pltpu.make_async_remote_copy(
                src_ref=slot, dst_ref=slot, send_sem=send.at[k - 1], recv_sem=recv.at[k - 1],
                device_id=(px, py, pc), device_id_type=MESH).wait_recv()
        for cp in cps:
            cp.wait_send()
        tot = land[0]
        for d in range(1, 8):
            tot = tot + land[d]
        out_ref[...] = tot

    vm = pl.BlockSpec(memory_space=pltpu.VMEM)
    return pl.pallas_call(
        body, name="allreduce_small",
        out_shape=jax.ShapeDtypeStruct(buf.shape, F32),
        in_specs=[vm], out_specs=vm,
        scratch_shapes=[pltpu.VMEM((8, R, LANES), F32), pltpu.SemaphoreType.DMA((7,)),
                        pltpu.SemaphoreType.DMA((7,))],
    )(buf)


def _rope_tables(T):
    pos = (jnp.arange(T) - FRONT).astype(F32)
    lane = jnp.arange(LANES)
    inv_a = ROPE_THETA ** (-(2 * ((lane % 64) % 32)).astype(F32) / SWA_HEAD_DIM)
    ang_a = pos[:, None] * inv_a[None, :]
    cosa, sina = jnp.cos(ang_a), jnp.sin(ang_a)
    inv_m = ROPE_THETA ** (-(2 * ((lane - KR_LANE) % 16)).astype(F32) / MLA_ROPE_DIM)
    ang_m = pos[:, None] * inv_m[None, :]
    on = ((lane >= KR_LANE) & (lane < KR_LANE + MLA_ROPE_DIM))[None, :]
    cosm = jnp.where(on, jnp.cos(ang_m), 1.0)
    sinm = jnp.where(on, jnp.sin(ang_m), 0.0)
    return cosa, sina, cosm, sinm


def _cols_from_chips(g):
    return jnp.concatenate([g[j] for j in range(4)], axis=-1)


def _rows_from_chips(g):
    return jnp.concatenate([g[j] for j in range(4)], axis=-2)


def _cols_to_chips(w):
    L, r, c4 = w.shape
    return jnp.moveaxis(w.reshape(L, r, 4, c4 // 4), 2, 0)


def _rows_to_chips(w):
    L, r4, c = w.shape
    return jnp.moveaxis(w.reshape(L, 4, r4 // 4, c), 1, 0)


def _local_step(x2, target, meta_full, natural, p):
    T = BLOCK + x2.shape[0]
    L = DEPTH
    W_in, W_qup, W_kvup, W_o, W_gate, W_up, W_down = natural
    zpad = lambda n: jnp.zeros((L, D_MODEL, n), W_in.dtype)
    W_in = jnp.concatenate([W_in[..., :C_KR], zpad(KR_LANE), W_in[..., C_KR:IN_W],
                            zpad(LANES - KR_LANE - MLA_ROPE_DIM)], axis=-1)
    W_qup = W_qup.reshape(L, MLA_Q_RANK, MLA_HEADS, MLA_QK_DIM)
    W_qup = jnp.pad(W_qup, ((0, 0), (0, 0), (0, 0), (0, LANES - MLA_QK_DIM))).reshape(L, MLA_Q_RANK, SLOT_W)
    attn_norm, q_norm, kv_norm, sinks = p["attn_norm"], p["q_norm"], p["kv_norm"], p["sinks"]
    out_norm_swa, out_norm_mla, ffn_norm, final_norm = (
        p["out_norm_swa"], p["out_norm_mla"], p["ffn_norm"], p["final_norm"])

    cosa, sina, cosm, sinm = _rope_tables(T)
    no_sink = jnp.full((MLA_HEADS,), NEG, F32)
    scale_a, scale_b = SWA_HEAD_DIM ** -0.5, MLA_QK_DIM ** -0.5
    row = lambda v: v.reshape(1, -1)

    h = jnp.concatenate([jnp.zeros((FRONT, D_MODEL), F32), meta_full, x2], axis=0)
    saved = []
    for l in range(L):
        u, proj = _norm_proj("in_proj", h, row(attn_norm[l]), [W_in[l]], False)
        qa, ka, va, qn, cn, kr = _prep1(proj, row(q_norm[l]), row(kv_norm[l]), cosa, sina, cosm, sinm)
        qb = _mm("q_up", qn, W_qup[l], "nn")
        kvb = _mm("kv_up", cn, W_kvup[l], "nn")
        qs, ks, vs = _prep2(qb, kvb, kr, cosm, sinm)
        oa, lse_a = _attn_fwd("swa_fwd", qa, ka, va, sinks[l], 4, False, scale_a)
        ob, lse_b = _attn_fwd("mla_fwd", qs, ks, vs, no_sink, 1, True, scale_b)
        mix = _merge_fwd(oa, ob, row(out_norm_swa[l]), row(out_norm_mla[l]))
        h1 = _mm("o_proj", mix, W_o[l], "nn", res=h)
        u2, a, b, hm = _norm_proj("ffn_in", h1, row(ffn_norm[l]), [W_gate[l], W_up[l]], True)
        h2 = _mm("down_proj", hm, W_down[l], "nn", res=h1)
        saved.append((h, u, proj, qa, ka, va, qn, cn, qs, ks, vs, oa, lse_a, ob, lse_b, mix, h1, u2, a, b, hm))
        h = h2

    dh, d_final, loss_row = _loss_head(h, target, row(final_norm))

    gw = {k: [None] * L for k in ("in", "qup", "kvup", "o", "gate", "up", "down")}
    gs = {k: [None] * L for k in ("attn", "qn", "kvn", "sink", "ga", "gb", "ffn")}
    for l in reversed(range(L)):
        (h0, u, proj, qa, ka, va, qn, cn, qs, ks, vs, oa, lse_a, ob, lse_b, mix, h1, u2, a, b, hm) = saved[l]
        gw["down"][l] = _mm("down_dw", hm, dh, "tn")
        da, db = _ffn_mid_bwd(dh, W_down[l], a, b)
        gw["gate"][l] = _mm("gate_dw", u2, da, "tn")
        gw["up"][l] = _mm("up_dw", u2, db, "tn")
        du2 = _mm("gate_dx", da, W_gate[l], "nt")
        du2 = _mm("up_dx", db, W_up[l], "nt", res=du2)
        dh1, gs["ffn"][l] = _rmsnorm_bwd("ffn_norm_bwd", h1, row(ffn_norm[l]), du2, dh)
        gw["o"][l] = _mm("o_dw", mix, dh1, "tn")
        dmix = _mm("o_dx", dh1, W_o[l], "nt")
        sink_slots = jnp.repeat(sinks[l], LANES).reshape(1, SLOT_W)
        doa, dla, dob, dlb, gs["ga"][l], gs["gb"][l], dsink = _merge_bwd(
            dmix, oa, ob, lse_a, row(out_norm_swa[l]), row(out_norm_mla[l]), sink_slots)
        gs["sink"][l] = dsink.reshape(SWA_HEADS, LANES)[:, 0]
        dqs, dks, dvs = _attn_bwd("mla_bwd", qs, ks, vs, dob, lse_b, dlb, 1, True, scale_b)
        dqa, dka, dva = _attn_bwd("swa_bwd", qa, ka, va, doa, lse_a, dla, 4, False, scale_a)
        dqb, dkvb, dkr = _prep2_bwd(dqs, dks, dvs, cosm, sinm)
        gw["qup"][l] = _mm("q_up_dw", qn, dqb, "tn")
        gw["kvup"][l] = _mm("kv_up_dw", cn, dkvb, "tn")
        dqn = _mm("q_up_dx", dqb, W_qup[l], "nt")
        dcn = _mm("kv_up_dx", dkvb, W_kvup[l], "nt")
        dproj, gs["qn"][l], gs["kvn"][l] = _prep1_bwd(
            proj, dqa, dka, dva, dqn, dcn, dkr, cosa, sina, row(q_norm[l]), row(kv_norm[l]))
        gw["in"][l] = _mm("in_dw", u, dproj, "tn")
        du = _mm("in_dx", dproj, W_in[l], "nt")
        dh, gs["attn"][l] = _rmsnorm_bwd("attn_norm_bwd", h0, row(attn_norm[l]), du, dh1)

    st = lambda k: jnp.stack(gw[k])
    d_in = st("in")
    d_in = jnp.concatenate([d_in[..., :C_KR], d_in[..., C_KR + KR_LANE:C_KR + KR_LANE + MLA_ROPE_DIM]], axis=-1)
    d_qup = st("qup").reshape(L, MLA_Q_RANK, MLA_HEADS, LANES)[..., :MLA_QK_DIM].reshape(L, MLA_Q_RANK, -1)
    d_nat = [d_in, d_qup, st("kvup"), st("o"), st("gate"), st("up"), st("down")]
    return loss_row, dh, d_nat, gs, d_final


def kernel(x, meta_tokens, attn_norm, w_in, q_norm, w_q_up, kv_norm, w_kv_up, sinks, out_norm_swa, out_norm_mla, w_o, ffn_norm, w_gate, w_up, w_down, final_norm, loss_target, m_meta_tokens, m_attn_norm, m_w_in, m_q_norm, m_w_q_up, m_kv_norm, m_w_kv_up, m_sinks, m_out_norm_swa, m_out_norm_mla, m_w_o, m_ffn_norm, m_w_gate, m_w_up, m_w_down, m_final_norm, v_meta_tokens, v_attn_norm, v_w_in, v_q_norm, v_w_q_up, v_kv_norm, v_w_kv_up, v_sinks, v_out_norm_swa, v_out_norm_mla, v_w_o, v_ffn_norm, v_w_gate, v_w_up, v_w_down, v_final_norm):
    assert x.shape[0] == 1 and x.shape[1] % BLOCK == 0
    big = [w_in, w_q_up, w_kv_up, w_o, w_gate, w_up, w_down]

    c_idx = lax.axis_index("c").astype(jnp.int32)
    chip = (2 * lax.axis_index("x") + lax.axis_index("y")).astype(jnp.int32)
    mine = [w.astype(BF16) for w in big] + [meta_tokens]
    gathered = _gather_weights(mine[:-1], mine[-1])
    parts = [[jnp.where(chip == j, own, g[j]) for j in range(4)] for own, g in zip(mine, gathered)]
    join = [_cols_from_chips, _cols_from_chips, _cols_from_chips, _rows_from_chips, _cols_from_chips,
            _cols_from_chips, _rows_from_chips]
    natural = [f(p) for f, p in zip(join, parts[:-1])]
    meta_full = jnp.concatenate(parts[-1], axis=-1)
    small_p = dict(attn_norm=attn_norm, q_norm=q_norm, kv_norm=kv_norm, sinks=sinks, out_norm_swa=out_norm_swa,
                   out_norm_mla=out_norm_mla, ffn_norm=ffn_norm, final_norm=final_norm)
    loss_row, dh, d_nat, gs, d_final = _local_step(x[0], loss_target[0], meta_full, natural, small_p)
    grad_x = dh[BLOCK:][None]

    split = [_cols_to_chips, _cols_to_chips, _cols_to_chips, _rows_to_chips, _cols_to_chips, _cols_to_chips,
             _rows_to_chips]
    full = [f(d) for f, d in zip(split, d_nat)]

    got = _pair_exchange([f.astype(BF16) for f in full])
    sums = [_pair_add(f, g, c_idx) for f, g in zip(full, got)]
    landed = _chip_scatter([s16 for _, s16 in sums])
    g_big = _pair_join([_chip_add(t, s32, chip, c_idx) for t, (s32, _) in zip(landed, sums)])

    small = [jnp.stack(gs["attn"]).reshape(-1), jnp.stack(gs["qn"]).reshape(-1), jnp.stack(gs["kvn"]).reshape(-1),
             jnp.stack(gs["sink"]).reshape(-1), jnp.stack(gs["ga"]).reshape(-1), jnp.stack(gs["gb"]).reshape(-1),
             jnp.stack(gs["ffn"]).reshape(-1), d_final.reshape(-1)]
    sizes = [s.shape[0] for s in small]
    flat = jnp.concatenate(small + [dh[FRONT:BLOCK].reshape(-1), loss_row[0, :1]])
    n_flat = flat.shape[0]
    rows_needed = -(-n_flat // (8 * LANES)) * 8
    flat = jnp.pad(flat, (0, rows_needed * LANES - n_flat)).reshape(rows_needed, LANES)
    tot = _allreduce_small(flat).reshape(-1)
    n_small = sum(sizes)
    loss = tot[n_small + N_META * D_MODEL]
    g_meta_full = tot[n_small:n_small + N_META * D_MODEL].reshape(N_META, D_MODEL)
    g_meta_mine = lax.dynamic_slice_in_dim(g_meta_full, chip * (D_MODEL // 4), D_MODEL // 4, axis=1)

    small_w = [attn_norm, q_norm, kv_norm, sinks, out_norm_swa, out_norm_mla, ffn_norm, final_norm]
    small_m = [m_attn_norm, m_q_norm, m_kv_norm, m_sinks, m_out_norm_swa, m_out_norm_mla, m_ffn_norm, m_final_norm]
    small_v = [v_attn_norm, v_q_norm, v_kv_norm, v_sinks, v_out_norm_swa, v_out_norm_mla, v_ffn_norm, v_final_norm]
    n_rows = -(-n_small // (8 * LANES)) * 8

    def pack(arrs):
        f = jnp.concatenate([a.reshape(-1) for a in arrs])
        return jnp.pad(f, (0, n_rows * LANES - n_small), constant_values=1.0).reshape(n_rows, LANES)

    g_small_pack = jnp.pad(tot[:n_small], (0, n_rows * LANES - n_small)).reshape(n_rows, LANES)
    upd_small = _adamw("adam_small", pack(small_w), g_small_pack, pack(small_m), pack(small_v))

    def unpack(p):
        f = p.reshape(-1)
        out, off = [], 0
        for a, n in zip(small_w, sizes):
            out.append(f[off:off + n].reshape(a.shape))
            off += n
        return out

    g_small = unpack(g_small_pack)
    d_small, m_small, v_small = [unpack(p) for p in upd_small]
    d_meta, nm_meta, nv_meta = _adamw("adam_meta", meta_tokens, g_meta_mine, m_meta_tokens, v_meta_tokens)

    big_m = [m_w_in, m_w_q_up, m_w_kv_up, m_w_o, m_w_gate, m_w_up, m_w_down]
    big_v = [v_w_in, v_w_q_up, v_w_kv_up, v_w_o, v_w_gate, v_w_up, v_w_down]
    upd_big = [_adamw("adam_big", w, g, m, v) for w, g, m, v in zip(big, g_big, big_m, big_v)]

    names = ["meta_tokens", "attn_norm", "w_in", "q_norm", "w_q_up", "kv_norm", "w_kv_up", "sinks",
             "out_norm_swa", "out_norm_mla", "w_o", "ffn_norm", "w_gate", "w_up", "w_down", "final_norm"]
    small_idx = {"attn_norm": 0, "q_norm": 1, "kv_norm": 2, "sinks": 3, "out_norm_swa": 4,
                 "out_norm_mla": 5, "ffn_norm": 6, "final_norm": 7}
    big_idx = {"w_in": 0, "w_q_up": 1, "w_kv_up": 2, "w_o": 3, "w_gate": 4, "w_up": 5, "w_down": 6}
    grads, deltas, new_m, new_v = [], [], [], []
    for nme in names:
        if nme == "meta_tokens":
            quad = (g_meta_mine, d_meta, nm_meta, nv_meta)
        elif nme in small_idx:
            i = small_idx[nme]
            quad = (g_small[i], d_small[i], m_small[i], v_small[i])
        else:
            i = big_idx[nme]
            quad = (g_big[i], *upd_big[i])
        grads.append(quad[0]); deltas.append(quad[1]); new_m.append(quad[2]); new_v.append(quad[3])
    return (loss, grad_x, *grads, *deltas, *new_m, *new_v)
```

```python
import jax
import jax.numpy as jnp
from jax import lax
from jax.experimental import pallas as pl
from jax.experimental.pallas import tpu as pltpu

F32 = jnp.float32
BF16 = jnp.bfloat16
MXU_DTYPE = BF16

D_MODEL = 1024
DEPTH = 4
N_META = 16
BLOCK = 128
WINDOW = 128
ROPE_THETA = 10000.0
EPS = 1e-6
NEG = -1e30
SWA_HEADS = 8
SWA_KV_HEADS = 2
SWA_HEAD_DIM = 64
MLA_HEADS = 8
MLA_Q_RANK = 256
MLA_KV_RANK = 128
MLA_NOPE_DIM = 64
MLA_ROPE_DIM = 32
MLA_V_DIM = 64
MLA_QK_DIM = MLA_NOPE_DIM + MLA_ROPE_DIM
D_FF = 2816
FRONT = (-N_META) % BLOCK
LANES = 128
SLOT_W = 8 * LANES
C_QA, C_KA, C_VA, C_QL, C_KL, C_KR, IN_WP = 0, 512, 640, 768, 1024, 1152, 1280
KR_LANE = 64
IN_W = 1184

ADAM_LR, ADAM_B1, ADAM_B2, ADAM_EPS, ADAM_WD, ADAM_STEP = 0.001, 0.9, 0.999, 1e-08, 0.01, 10

VMEM_LIMIT = 48 * 1024 * 1024
MESH = pl.DeviceIdType.MESH


def _tile(n, prefs):
    for t in prefs:
        if n % t == 0:
            return t
    return n


def _params(sem):
    return pltpu.CompilerParams(dimension_semantics=sem, vmem_limit_bytes=VMEM_LIMIT)


_DIMS = {"nn": (((1,), (0,)), ((), ())), "nt": (((1,), (1,)), ((), ())), "tn": (((0,), (0,)), ((), ()))}


def _mm(name, a, b, mode, out_dtype=F32, res=None):
    if mode == "nn":
        (M, K), (_, N) = a.shape, b.shape
    elif mode == "nt":
        (M, K), (N, _) = a.shape, b.shape
    else:
        (K, M), (_, N) = a.shape, b.shape
    lane_tiles = (1408, 1024, 640, 768, 512, 384, 256, 128)
    row_tiles = (528, 512, 384, 256, 128)
    bm = _tile(M, lane_tiles if mode == "tn" else row_tiles)
    bn = _tile(N, lane_tiles)
    bk = _tile(K, (1056,) + row_tiles if mode == "tn" else lane_tiles)
    nk = K // bk
    if mode == "tn":
        a_spec = pl.BlockSpec((bk, bm), lambda i, j, k: (k, i))
    else:
        a_spec = pl.BlockSpec((bm, bk), lambda i, j, k: (i, k))
    if mode == "nt":
        b_spec = pl.BlockSpec((bn, bk), lambda i, j, k: (j, k))
    else:
        b_spec = pl.BlockSpec((bk, bn), lambda i, j, k: (k, j))
    o_spec = pl.BlockSpec((bm, bn), lambda i, j, k: (i, j))
    in_specs = [a_spec, b_spec]
    args = [a, b]
    if res is not None:
        in_specs.append(o_spec)
        args.append(res)
    dims = _DIMS[mode]

    def kern(a_ref, b_ref, *rest):
        if res is not None:
            r_ref, o_ref = rest[0], rest[1]
            scr = rest[2:]
        else:
            r_ref, o_ref = None, rest[0]
            scr = rest[1:]
        p = lax.dot_general(a_ref[...].astype(MXU_DTYPE), b_ref[...].astype(MXU_DTYPE), dims,
                            preferred_element_type=F32)

        def finish(val):
            if r_ref is not None:
                val = val + r_ref[...]
            o_ref[...] = val.astype(o_ref.dtype)

        if nk == 1:
            finish(p)
        else:
            acc = scr[0]
            k = pl.program_id(2)

            @pl.when(k == 0)
            def _():
                acc[...] = p

            @pl.when(k > 0)
            def _():
                acc[...] += p

            @pl.when(k == nk - 1)
            def _():
                finish(acc[...])

    return pl.pallas_call(
        kern, name=name,
        out_shape=jax.ShapeDtypeStruct((M, N), out_dtype),
        grid=(M // bm, N // bn, nk),
        in_specs=in_specs, out_specs=o_spec,
        scratch_shapes=[pltpu.VMEM((bm, bn), F32)] if nk > 1 else [],
        compiler_params=_params(("parallel", "parallel", "arbitrary")),
    )(*args)


def _rowmap(name, body, rows, vecs, outs, accs=(), tr_prefs=(384, 256, 128)):
    R = rows[0].shape[0]
    tr = _tile(R, tr_prefs)
    n_r, n_v, n_o, n_a = len(rows), len(vecs), len(outs), len(accs)

    def kern(*refs):
        ins = [r[...] for r in refs[:n_r + n_v]]
        o_refs = refs[n_r + n_v:n_r + n_v + n_o]
        a_refs = refs[n_r + n_v + n_o:]
        res = body(*ins)
        for o_ref, val in zip(o_refs, res[:n_o]):
            o_ref[...] = val.astype(o_ref.dtype)
        if n_a:
            first = pl.program_id(0) == 0

            @pl.when(first)
            def _():
                for a_ref, val in zip(a_refs, res[n_o:]):
                    a_ref[...] = val

            @pl.when(jnp.logical_not(first))
            def _():
                for a_ref, val in zip(a_refs, res[n_o:]):
                    a_ref[...] += val

    in_specs = [pl.BlockSpec((tr, r.shape[1]), lambda i: (i, 0)) for r in rows]
    in_specs += [pl.BlockSpec((1, v.shape[1]), lambda i: (0, 0)) for v in vecs]
    out_specs = [pl.BlockSpec((tr, c), lambda i: (i, 0)) for c, _ in outs]
    out_specs += [pl.BlockSpec((1, c), lambda i: (0, 0)) for c in accs]
    out_shape = [jax.ShapeDtypeStruct((R, c), dt) for c, dt in outs]
    out_shape += [jax.ShapeDtypeStruct((1, c), F32) for c in accs]
    return pl.pallas_call(
        kern, name=name, out_shape=out_shape, grid=(R // tr,),
        in_specs=in_specs, out_specs=out_specs,
        compiler_params=_params(("arbitrary",) if n_a else ("parallel",)),
    )(*rows, *vecs)


def _lane(shape):
    return lax.broadcasted_iota(jnp.int32, shape, 1)


def _rot_swa(x):
    lane = _lane(x.shape)
    return jnp.where((lane & 63) < 32, -pltpu.roll(x, 96, 1), pltpu.roll(x, 32, 1))


def _rot_mla(x):
    lane = _lane(x.shape)
    lo = jnp.where(lane >= KR_LANE, -pltpu.roll(x, 112, 1), 0.0)
    hi = jnp.where(lane < KR_LANE + MLA_ROPE_DIM, pltpu.roll(x, 16, 1), 0.0)
    return jnp.where(lane < KR_LANE + 16, lo, hi)


def _rope(x, cos, sin, rot):
    return x * cos + rot(x) * sin


def _rope_t(g, cos, sin, rot):
    return g * cos - rot(g * sin)


def _low(x):
    return jnp.where(_lane(x.shape) < 64, x, 0.0)


def _value_slot(x):
    lane = _lane(x.shape)
    return jnp.where(lane < 64, x, jnp.where(lane == 64, 1.0, 0.0))


def _blk(x, j):
    return x[:, j * LANES:(j + 1) * LANES]


def _rms_r(x, width):
    return lax.rsqrt(jnp.sum(x * x, axis=-1, keepdims=True) * (1.0 / width) + EPS)


def _rms_bwd(x, g, dy, width):
    r = _rms_r(x, width)
    gdy = dy * g
    dot = jnp.sum(gdy * x, axis=-1, keepdims=True)
    dx = r * gdy - x * (r * r * r * (1.0 / width) * dot)
    return dx, dy * x * r


def _colsum(x):
    return jnp.sum(x, axis=0, keepdims=True)


def _rmsnorm_bwd(name, x, g, dy, dres):
    def body(xv, dyv, dr, gv):
        dx, dg = _rms_bwd(xv, gv, dyv, D_MODEL)
        return dx + dr, _colsum(dg)
    return _rowmap(name, body, [x, dy, dres], [g], [(D_MODEL, F32)], [D_MODEL])


def _prep1(proj, qn_g, kv_g, cosa, sina, cosm, sinm):
    def body(p, ca, sa, cm, sm, gq, gk):
        qa = []
        for j in range(4):
            xr = _rope(_blk(p, j), ca, sa, _rot_swa)
            qa += [_low(xr), _low(pltpu.roll(xr, 64, 1))]
        kr_ = _rope(_blk(p, C_KA // LANES), ca, sa, _rot_swa)
        ka = [_low(kr_), _low(pltpu.roll(kr_, 64, 1))]
        vv = _blk(p, C_VA // LANES)
        va = [_value_slot(vv), _value_slot(pltpu.roll(vv, 64, 1))]
        ql = p[:, C_QL:C_QL + MLA_Q_RANK]
        qn = ql * _rms_r(ql, MLA_Q_RANK) * gq
        kl = p[:, C_KL:C_KL + MLA_KV_RANK]
        cn = kl * _rms_r(kl, MLA_KV_RANK) * gk
        kr = _rope(_blk(p, C_KR // LANES), cm, sm, _rot_mla)
        return (jnp.concatenate(qa, 1), jnp.concatenate(ka, 1), jnp.concatenate(va, 1), qn, cn, kr)
    return _rowmap("prep1", body, [proj, cosa, sina, cosm, sinm], [qn_g, kv_g],
                   [(SLOT_W, BF16), (2 * LANES, BF16), (2 * LANES, BF16),
                    (MLA_Q_RANK, BF16), (MLA_KV_RANK, BF16), (LANES, F32)])


def _prep2(qb, kvb, kr, cosm, sinm):
    def body(q, kv, krv, cm, sm):
        qs, ks, vs = [], [], []
        for h in range(MLA_HEADS):
            qs.append(_rope(_blk(q, h), cm, sm, _rot_mla))
            kvh = _blk(kv, h)
            ks.append(_low(kvh) + krv)
            vs.append(_value_slot(pltpu.roll(kvh, 64, 1)))
        return jnp.concatenate(qs, 1), jnp.concatenate(ks, 1), jnp.concatenate(vs, 1)
    return _rowmap("prep2", body, [qb, kvb, kr, cosm, sinm], [],
                   [(SLOT_W, BF16), (SLOT_W, BF16), (SLOT_W, BF16)])


def _compact(slots):
    return jnp.concatenate(
        [_blk(slots, 2 * j) + pltpu.roll(_blk(slots, 2 * j + 1), 64, 1) for j in range(4)], 1)


def _expand(nat):
    out = []
    for j in range(4):
        b = _blk(nat, j)
        out += [_low(b), _low(pltpu.roll(b, 64, 1))]
    return jnp.concatenate(out, 1)


def _merge_fwd(oa, ob, ga, gb):
    def body(a, b, gav, gbv):
        xa, xb = _compact(a), _compact(b)
        return (jnp.concatenate([xa * _rms_r(xa, 512) * gav, xb * _rms_r(xb, 512) * gbv], 1),)
    return _rowmap("merge_fwd", body, [oa, ob], [ga, gb], [(D_MODEL, BF16)])[0]


def _merge_bwd(dmix, oa, ob, lse_a, ga, gb, sink_slots):
    def body(dm, a, b, lse, gav, gbv, sk):
        outs = []
        accs = []
        for o, g, lo in ((a, gav, 0), (b, gbv, 512)):
            x = _compact(o)
            dx, dg = _rms_bwd(x, g, dm[:, lo:lo + 512], 512)
            do = _expand(dx)
            delta = jnp.concatenate(
                [jnp.broadcast_to(jnp.sum(_blk(do, h) * _blk(o, h), axis=-1, keepdims=True),
                                  (do.shape[0], LANES)) for h in range(8)], 1)
            outs += [do, delta]
            accs.append(_colsum(dg))
        dsink = _colsum(-jnp.exp(sk - lse) * outs[1])
        return (*outs, *accs, dsink)
    return _rowmap("merge_bwd", body, [dmix, oa, ob, lse_a], [ga, gb, sink_slots],
                   [(SLOT_W, BF16), (SLOT_W, F32), (SLOT_W, BF16), (SLOT_W, F32)],
                   [512, 512, SLOT_W])


def _prep2_bwd(dq, dk, dv, cosm, sinm):
    def body(dqv, dkv, dvv, cm, sm):
        dqb, dkvb = [], []
        krsum = jnp.zeros((dqv.shape[0], LANES), F32)
        for h in range(MLA_HEADS):
            dqb.append(_rope_t(_blk(dqv, h), cm, sm, _rot_mla))
            dkh = _blk(dkv, h)
            dkvb.append(_low(dkh) + pltpu.roll(_blk(dvv, h), 64, 1))
            krsum = krsum + dkh
        lane = _lane(krsum.shape)
        dkr = jnp.where((lane >= KR_LANE) & (lane < KR_LANE + MLA_ROPE_DIM),
                        _rope_t(krsum, cm, sm, _rot_mla), 0.0)
        return jnp.concatenate(dqb, 1), jnp.concatenate(dkvb, 1), dkr
    return _rowmap("prep2_bwd", body, [dq, dk, dv, cosm, sinm], [],
                   [(SLOT_W, BF16), (SLOT_W, BF16), (LANES, F32)])


def _prep1_bwd(proj, dqa, dka, dva, dqn, dcn, dkr, cosa, sina, qn_g, kv_g):
    def body(p, dq, dk, dv, dqnv, dcnv, dkrv, ca, sa, gq, gk):
        cols = []
        for j in range(4):
            nat = _blk(dq, 2 * j) + pltpu.roll(_blk(dq, 2 * j + 1), 64, 1)
            cols.append(_rope_t(nat, ca, sa, _rot_swa))
        grp = lambda d, g: sum(_blk(d, 4 * g + i) for i in range(4))
        cols.append(_rope_t(grp(dk, 0) + pltpu.roll(grp(dk, 1), 64, 1), ca, sa, _rot_swa))
        cols.append(grp(dv, 0) + pltpu.roll(grp(dv, 1), 64, 1))
        dql, dgq = _rms_bwd(p[:, C_QL:C_QL + MLA_Q_RANK], gq, dqnv, MLA_Q_RANK)
        dkl, dgk = _rms_bwd(p[:, C_KL:C_KL + MLA_KV_RANK], gk, dcnv, MLA_KV_RANK)
        cols += [dql, dkl, dkrv]
        return jnp.concatenate(cols, 1), _colsum(dgq), _colsum(dgk)
    return _rowmap("prep1_bwd", body, [proj, dqa, dka, dva, dqn, dcn, dkr, cosa, sina], [qn_g, kv_g],
                   [(IN_WP, BF16)], [MLA_Q_RANK, MLA_KV_RANK], tr_prefs=(192, 128))


def _sigmoid(x):
    return 1.0 / (1.0 + jnp.exp(-x))


def _ffn_tiles(T, F):
    return _tile(T, (528, 512, 384, 256, 128)), _tile(F, (1408, 1024, 640, 512, 256, 128))


def _norm_proj(name, h, g, weights, swiglu):
    (T, D), F = h.shape, weights[0].shape[1]
    bm, bn = _ffn_tiles(T, F)
    nw = len(weights)

    def kern(h_ref, g_ref, *rest):
        w_refs, u_ref, o_refs, u_scr = rest[:nw], rest[nw], rest[nw + 1:-1], rest[-1]

        @pl.when(pl.program_id(1) == 0)
        def _():
            x = h_ref[...]
            u = (x * _rms_r(x, D) * g_ref[...]).astype(u_scr.dtype)
            u_scr[...] = u
            u_ref[...] = u

        uv = u_scr[...]
        prods = [jnp.dot(uv, w[...], preferred_element_type=F32) for w in w_refs]
        for o_ref, p in zip(o_refs, prods):
            o_ref[...] = p
        if swiglu:
            a, b = prods
            o_refs[nw][...] = (a * _sigmoid(a) * b).astype(o_refs[nw].dtype)

    w_spec = pl.BlockSpec((D, bn), lambda i, j: (0, j))
    row_spec = pl.BlockSpec((bm, D), lambda i, j: (i, 0))
    o_spec = pl.BlockSpec((bm, bn), lambda i, j: (i, j))
    n_out = nw + (1 if swiglu else 0)
    return pl.pallas_call(
        kern, name=name,
        out_shape=[jax.ShapeDtypeStruct((T, D), MXU_DTYPE)] + [jax.ShapeDtypeStruct((T, F), F32)] * nw
        + ([jax.ShapeDtypeStruct((T, F), MXU_DTYPE)] if swiglu else []),
        grid=(T // bm, F // bn),
        in_specs=[row_spec, pl.BlockSpec((1, D), lambda i, j: (0, 0))] + [w_spec] * nw,
        out_specs=[row_spec] + [o_spec] * n_out,
        scratch_shapes=[pltpu.VMEM((bm, D), MXU_DTYPE)],
        compiler_params=_params(("parallel", "arbitrary")),
    )(h, g, *weights)


def _ffn_mid_bwd(dh, w_down, a, b):
    (T, D), F = dh.shape, w_down.shape[0]
    bm, bn = _ffn_tiles(T, F)

    def kern(dh_ref, wd_ref, a_ref, b_ref, da_ref, db_ref):
        d = lax.dot_general(dh_ref[...].astype(MXU_DTYPE), wd_ref[...], _DIMS["nt"], preferred_element_type=F32)
        av, bv = a_ref[...], b_ref[...]
        s = _sigmoid(av)
        da_ref[...] = (d * bv * (s * (1.0 + av * (1.0 - s)))).astype(da_ref.dtype)
        db_ref[...] = (d * (av * s)).astype(db_ref.dtype)

    o_spec = pl.BlockSpec((bm, bn), lambda i, j: (i, j))
    return pl.pallas_call(
        kern, name="ffn_mid_bwd",
        out_shape=[jax.ShapeDtypeStruct((T, F), MXU_DTYPE)] * 2,
        grid=(T // bm, F // bn),
        in_specs=[pl.BlockSpec((bm, D), lambda i, j: (i, 0)), pl.BlockSpec((bn, D), lambda i, j: (j, 0)),
                  o_spec, o_spec],
        out_specs=[o_spec, o_spec],
        compiler_params=_params(("parallel", "parallel")),
    )(dh, w_down, a, b)


def _loss_head(h, target, g):
    T = h.shape[0]
    nb = T // BLOCK

    def kern(h_ref, t_ref, g_ref, dh_ref, dg_ref, loss_ref, acc):
        i = pl.program_id(0)

        @pl.when(i == 0)
        def _():
            dh_ref[...] = jnp.zeros_like(dh_ref)
            dg_ref[...] = jnp.zeros_like(dg_ref)
            acc[...] = jnp.zeros_like(acc)

        @pl.when(i > 0)
        def _():
            x = h_ref[...]
            gv = g_ref[...]
            e = x * _rms_r(x, D_MODEL) * gv - t_ref[...]
            acc[...] += _colsum(e * e)
            dx, dg = _rms_bwd(x, gv, e * (1.0 / D_MODEL), D_MODEL)
            dh_ref[...] = dx
            dg_ref[...] += _colsum(dg)

        @pl.when(i == nb - 1)
        def _():
            tot = jnp.sum(acc[...], axis=-1, keepdims=True) * (0.5 / D_MODEL)
            loss_ref[...] = jnp.broadcast_to(tot, loss_ref.shape)

    return pl.pallas_call(
        kern, name="loss_head",
        out_shape=[jax.ShapeDtypeStruct((T, D_MODEL), F32), jax.ShapeDtypeStruct((1, D_MODEL), F32),
                   jax.ShapeDtypeStruct((1, LANES), F32)],
        grid=(nb,),
        in_specs=[pl.BlockSpec((BLOCK, D_MODEL), lambda i: (i, 0)),
                  pl.BlockSpec((BLOCK, D_MODEL), lambda i: (jnp.maximum(i - 1, 0), 0)),
                  pl.BlockSpec((1, D_MODEL), lambda i: (0, 0))],
        out_specs=[pl.BlockSpec((BLOCK, D_MODEL), lambda i: (i, 0)),
                   pl.BlockSpec((1, D_MODEL), lambda i: (0, 0)),
                   pl.BlockSpec((1, LANES), lambda i: (0, 0))],
        scratch_shapes=[pltpu.VMEM((1, D_MODEL), F32)],
        compiler_params=_params(("arbitrary",)),
    )(h, target, g)


LOG2E = 1.4426950408889634


def _attn_plan(T, causal):
    tq = _tile(T, (384, 256, 128))
    ck = min(2 * tq, T) if causal else min(tq + WINDOW, T)
    return tq, ck, (-(-T // ck) if causal else 1)


def _chunk(i, c, T, tq, ck, causal):
    if causal:
        return pl.multiple_of(jnp.minimum(c * ck, T - ck), LANES), c * ck
    return pl.multiple_of(jnp.clip(i * tq - WINDOW, 0, T - ck), LANES), 0


def _n_chunks(i, tq, ck, causal):
    return ((i + 1) * tq + ck - 1) // ck if causal else 1


def _mask(s, i, start, first, tq, ck, causal):
    qpos = i * tq + lax.broadcasted_iota(jnp.int32, (tq, 1), 0)
    kpos = start + lax.broadcasted_iota(jnp.int32, (tq, ck), 1)
    low = jnp.maximum(jnp.where(qpos < FRONT, 0, FRONT), first)
    if not causal:
        low = jnp.maximum(low, qpos - (WINDOW - 1))
    return jnp.where(kpos >= low, jnp.where(kpos <= qpos, s, NEG), NEG)


def _chunk_loop(n, body, init, causal):
    carry = body(0, init, True)
    if not causal:
        return carry
    carry = lax.fori_loop(1, n - 1, lambda c, cr: body(c, cr, False), carry)
    return lax.cond(n > 1, lambda cr: body(n - 1, cr, True), lambda cr: cr, carry)


FWD_HEADS_PER_STEP = 2
BWD_HEADS_PER_STEP = 1


def _head_cols(group, hp):
    q_cols = lambda hh: slice(hh * LANES, (hh + 1) * LANES)
    if group == 1:
        return q_cols, q_cols, hp * LANES
    assert group % hp == 0
    return q_cols, (lambda hh: slice(0, LANES)), LANES


def _whole(T, width, index, single):
    if single:
        return pl.BlockSpec((T, width), index, pipeline_mode=pl.Buffered(1))
    return pl.BlockSpec((T, width), index)


def _attn_fwd(name, q, k, v, sinks, group, causal, scale):
    T = q.shape[0]
    H = q.shape[1] // LANES
    HP = FWD_HEADS_PER_STEP
    tq, ck, slots = _attn_plan(T, causal)
    nq = T // tq
    c2 = scale * LOG2E
    q_cols, k_cols, kw = _head_cols(group, HP)

    def kern(sink_ref, q_ref, k_ref, v_ref, o_ref, lse_ref, s_scr):
        sink2 = [sink_ref[pl.program_id(0) * HP + hh] * LOG2E for hh in range(HP)]

        def q_tile(i, carry):
            rows = pl.ds(pl.multiple_of(i * tq, tq), tq)
            qq = [q_ref[rows, q_cols(hh)] for hh in range(HP)]
            n = _n_chunks(i, tq, ck, causal)

            def score(c, m2, masked):
                start, first = _chunk(i, c, T, tq, ck, causal)
                out = []
                for hh in range(HP):
                    s = lax.dot_general(qq[hh], k_ref[pl.ds(start, ck), k_cols(hh)], _DIMS["nt"],
                                        preferred_element_type=F32) * c2
                    if masked:
                        s = _mask(s, i, start, first, tq, ck, causal)
                    s_scr[hh, c] = s
                    out.append(jnp.maximum(m2[hh], jnp.max(s, axis=-1, keepdims=True)))
                return tuple(out)

            m2 = _chunk_loop(n, score, tuple(jnp.full((tq, 1), sk, F32) for sk in sink2), causal)

            def weigh(c, acc):
                start, _ = _chunk(i, c, T, tq, ck, causal)
                out = []
                for hh in range(HP):
                    p = jnp.exp2(s_scr[hh, c] - m2[hh])
                    out.append(acc[hh] + jnp.dot(p.astype(MXU_DTYPE), v_ref[pl.ds(start, ck), k_cols(hh)],
                                                 preferred_element_type=F32))
                return tuple(out)

            acc = lax.fori_loop(0, n, weigh, tuple(jnp.zeros((tq, LANES), F32) for _ in range(HP)))
            lane = _lane((tq, LANES))
            for hh in range(HP):
                l = acc[hh][:, 64:65] + jnp.exp2(sink2[hh] - m2[hh])
                o_ref[rows, q_cols(hh)] = jnp.where(lane < 64, acc[hh] / l, 0.0)
                lse_ref[rows, q_cols(hh)] = jnp.broadcast_to(m2[hh] * (1.0 / LOG2E) + jnp.log(l), (tq, LANES))
            return carry

        lax.fori_loop(0, nq, q_tile, 0)

    q_spec = _whole(T, HP * LANES, lambda g: (0, g), True)
    kv_spec = _whole(T, kw, (lambda g: (0, g)) if group == 1 else (lambda g: (0, (g * HP) // group)), True)
    return pl.pallas_call(
        kern, name=name,
        out_shape=[jax.ShapeDtypeStruct((T, H * LANES), F32)] * 2,
        grid=(H // HP,),
        in_specs=[pl.BlockSpec(memory_space=pltpu.SMEM), q_spec, kv_spec, kv_spec],
        out_specs=[q_spec, q_spec],
        scratch_shapes=[pltpu.VMEM((HP, slots, tq, ck), F32)],
        compiler_params=_params(("parallel",)),
    )(sinks, q, k, v)


def _attn_bwd(name, q, k, v, do, lse, delta, group, causal, scale):
    T = q.shape[0]
    H = q.shape[1] // LANES
    HP = BWD_HEADS_PER_STEP
    tq, ck, _ = _attn_plan(T, causal)
    nq = T // tq
    c2 = scale * LOG2E
    q_cols, k_cols, kw = _head_cols(group, HP)

    def kern(q_ref, k_ref, v_ref, do_ref, lse_ref, dl_ref, dq_ref, dk_ref, dv_ref):
        dk_ref[...] = jnp.zeros_like(dk_ref)
        dv_ref[...] = jnp.zeros_like(dv_ref)

        def q_tile(i, carry):
            rows = pl.ds(pl.multiple_of(i * tq, tq), tq)
            qq = [q_ref[rows, q_cols(hh)] for hh in range(HP)]
            dd = [do_ref[rows, q_cols(hh)] for hh in range(HP)]
            lse2 = [lse_ref[rows, q_cols(hh)][:, 0:1] * LOG2E for hh in range(HP)]
            dl_c = [dl_ref[rows, q_cols(hh)][:, 0:1] for hh in range(HP)]

            def chunk(c, dq, masked):
                start, first = _chunk(i, c, T, tq, ck, causal)
                keys = pl.ds(start, ck)
                out = []
                for hh in range(HP):
                    kk, vv = k_ref[keys, k_cols(hh)], v_ref[keys, k_cols(hh)]
                    s = lax.dot_general(qq[hh], kk, _DIMS["nt"], preferred_element_type=F32)
                    if masked:
                        s = _mask(s, i, start, first, tq, ck, causal)
                    p = jnp.exp2(s * c2 - lse2[hh])
                    dv_ref[keys, q_cols(hh)] += lax.dot_general(p.astype(MXU_DTYPE), dd[hh], _DIMS["tn"],
                                                                preferred_element_type=F32)
                    dp = lax.dot_general(dd[hh], vv, _DIMS["nt"], preferred_element_type=F32)
                    ds = (p * (dp - dl_c[hh])).astype(MXU_DTYPE)
                    dk_ref[keys, q_cols(hh)] += lax.dot_general(ds, qq[hh], _DIMS["tn"],
                                                                preferred_element_type=F32) * scale
                    out.append(dq[hh] + jnp.dot(ds, kk, preferred_element_type=F32))
                return tuple(out)

            dq = _chunk_loop(_n_chunks(i, tq, ck, causal), chunk,
                             tuple(jnp.zeros((tq, LANES), F32) for _ in range(HP)), causal)
            for hh in range(HP):
                dq_ref[rows, q_cols(hh)] = dq[hh] * scale
            return carry

        lax.fori_loop(0, nq, q_tile, 0)

    q_spec = _whole(T, HP * LANES, lambda g: (0, g), False)
    kv_spec = _whole(T, kw, (lambda g: (0, g)) if group == 1 else (lambda g: (0, (g * HP) // group)), False)
    return pl.pallas_call(
        kern, name=name,
        out_shape=[jax.ShapeDtypeStruct((T, H * LANES), F32)] * 3,
        grid=(H // HP,),
        in_specs=[q_spec, kv_spec, kv_spec, q_spec, q_spec, q_spec],
        out_specs=[q_spec, q_spec, q_spec],
        compiler_params=_params(("parallel",)),
    )(q, k, v, do, lse, delta)


def _ew(name, fn, ins, out_dtypes):
    shape = ins[0].shape
    flat = [a.reshape(-1, shape[-1]) for a in ins]
    R, C = flat[0].shape
    tr = _tile(R, (512, 256, 128, 64, 32, 16, 8))
    n_in = len(ins)

    def kern(*refs):
        res = fn(*[r[...] for r in refs[:n_in]])
        for o_ref, val in zip(refs[n_in:], res):
            o_ref[...] = val.astype(o_ref.dtype)

    spec = pl.BlockSpec((tr, C), lambda i: (i, 0))
    outs = pl.pallas_call(
        kern, name=name,
        out_shape=[jax.ShapeDtypeStruct((R, C), dt) for dt in out_dtypes],
        grid=(R // tr,), in_specs=[spec] * n_in, out_specs=[spec] * len(out_dtypes),
        compiler_params=_params(("parallel",)),
    )(*flat)
    return [o.reshape(shape) for o in outs]


def _adamw(name, w, g, m, v):
    c1 = 1.0 - ADAM_B1 ** ADAM_STEP
    c2 = 1.0 - ADAM_B2 ** ADAM_STEP

    def fn(wv, gv, mv, vv):
        mn = ADAM_B1 * mv + (1.0 - ADAM_B1) * gv
        vn = ADAM_B2 * vv + (1.0 - ADAM_B2) * (gv * gv)
        delta = -ADAM_LR * ((mn / c1) / (jnp.sqrt(vn / c2) + ADAM_EPS) + ADAM_WD * wv)
        return delta, mn, vn

    return _ew(name, fn, [w, g, m, v], [F32, F32, F32])


_ANY = pl.BlockSpec(memory_space=pl.ANY)


def _where_am_i():
    x, y, c = lax.axis_index("x"), lax.axis_index("y"), lax.axis_index("c")
    chips = [(1 - x, y), (x, 1 - y), (1 - x, 1 - y)]
    return x, y, c, chips


def _gather_weights(shards, meta):
    arrs = list(shards) + [meta]
    n = len(arrs)
    per = [a.shape[0] // 2 for a in arrs]

    def body(*refs):
        ins, outs = refs[:n], refs[n:2 * n]
        send1, recv1, send2, recv2 = refs[2 * n:]
        x, y, c, chips = _where_am_i()
        me = 2 * x + y

        def half(ref, k, cc):
            return ref.at[pl.ds(per[k] * cc, per[k])]

        first = []
        for k in range(n):
            for j, (cx, cy) in enumerate(chips):
                first.append(pltpu.make_async_remote_copy(
                    src_ref=half(ins[k], k, c), dst_ref=half(outs[k].at[me], k, c),
                    send_sem=send1.at[k, j], recv_sem=recv1.at[k, j],
                    device_id=(cx, cy, c), device_id_type=MESH))
        for cp in first:
            cp.start()
        passed = []
        for k in range(n):
            for j, (cx, cy) in enumerate(chips):
                landed = half(outs[k].at[2 * cx + cy], k, c)
                pltpu.make_async_remote_copy(
                    src_ref=landed, dst_ref=landed, send_sem=send1.at[k, j], recv_sem=recv1.at[k, j],
                    device_id=(cx, cy, c), device_id_type=MESH).wait_recv()
                fwd = pltpu.make_async_remote_copy(
                    src_ref=landed, dst_ref=landed, send_sem=send2.at[k, j], recv_sem=recv2.at[k, j],
                    device_id=(x, y, 1 - c), device_id_type=MESH)
                fwd.start()
                passed.append(fwd)
        for k in range(n):
            for j, (cx, cy) in enumerate(chips):
                other = half(outs[k].at[2 * cx + cy], k, 1 - c)
                pltpu.make_async_remote_copy(
                    src_ref=other, dst_ref=other, send_sem=send2.at[k, j], recv_sem=recv2.at[k, j],
                    device_id=(x, y, 1 - c), device_id_type=MESH).wait_recv()
        for cp in first + passed:
            cp.wait_send()

    return pl.pallas_call(
        body, name="gather_weights",
        out_shape=[jax.ShapeDtypeStruct((4,) + a.shape, a.dtype) for a in arrs],
        in_specs=[_ANY] * n, out_specs=[_ANY] * n,
        scratch_shapes=[pltpu.SemaphoreType.DMA((n, 3))] * 4,
    )(*arrs)


_HBM = pl.BlockSpec(memory_space=pltpu.HBM)
_SEM = pl.BlockSpec(memory_space=pltpu.SEMAPHORE)
_FLOWS = pltpu.SideEffectType.DATAFLOW_SIDE_EFFECTING


def _in_hbm(a):
    return pltpu.with_memory_space_constraint(a, pltpu.HBM)


def _gather_start(shards):
    n = len(shards)
    nl = shards[0].shape[0]
    lands = [lax.empty((4,) + sh.shape[1:], sh.dtype) for _ in range(nl) for sh in shards]

    def body(*refs):
        srcs, land = refs[:n], refs[n:n + n * nl]
        send, recv, token = refs[n + n * nl:n + n * nl + 3]
        x, y, c, chips = _where_am_i()
        me = 2 * x + y
        for l in range(nl):
            for k in range(n):
                for cx, cy in chips:
                    pltpu.make_async_remote_copy(
                        src_ref=srcs[k].at[l], dst_ref=land[l * n + k].at[me],
                        send_sem=send.at[l], recv_sem=recv.at[l],
                        device_id=(cx, cy, c), device_id_type=MESH).start()
        token[...] = jnp.zeros_like(token)

    ins = [_in_hbm(a) for a in list(shards) + lands]
    res = pl.pallas_call(
        body, name="gather_start",
        out_shape=(pltpu.SemaphoreType.DMA((nl,)), pltpu.SemaphoreType.DMA((nl,)),
                   jax.ShapeDtypeStruct((8, LANES), F32)) + tuple(pltpu.HBM(a.shape, a.dtype) for a in ins),
        in_specs=[_HBM] * len(ins),
        out_specs=(_SEM, _SEM, pl.BlockSpec(memory_space=pltpu.VMEM)) + (_HBM,) * len(ins),
        input_output_aliases={i: 3 + i for i in range(len(ins))},
        compiler_params=pltpu.CompilerParams(has_side_effects=_FLOWS),
    )(*ins)
    send, recv, token = res[:3]
    thru = res[3:3 + n]
    lands = res[3 + n:]
    return send, recv, token, list(thru), [list(lands[l * n:(l + 1) * n]) for l in range(nl)]


def _gather_wait(name, send, recv, l, thru, land, after):
    n = len(thru)

    def body(*refs):
        srcs, lands = refs[:n], refs[n:2 * n]
        send_sem, recv_sem = refs[2 * n:2 * n + 2]
        x, y, c, chips = _where_am_i()
        for k in range(n):
            for cx, cy in chips:
                copy = pltpu.make_async_remote_copy(
                    src_ref=srcs[k].at[l], dst_ref=lands[k].at[2 * cx + cy],
                    send_sem=send_sem.at[l], recv_sem=recv_sem.at[l],
                    device_id=(cx, cy, c), device_id_type=MESH)
                copy.wait_send()
                copy.wait_recv()

    ins = list(thru) + list(land)
    res = pl.pallas_call(
        body, name=name,
        out_shape=tuple(pltpu.HBM(a.shape, a.dtype) for a in ins),
        in_specs=[_HBM] * len(ins) + [_SEM, _SEM, _ANY],
        out_specs=(_HBM,) * len(ins),
        input_output_aliases={i: i for i in range(len(ins))},
        compiler_params=pltpu.CompilerParams(has_side_effects=_FLOWS),
    )(*ins, send, recv, after)
    return list(res[:n]), list(res[n:])


def _row_chunks(r):
    n = 4 if r % 64 == 0 else 1
    return [(q * (r // n), r // n) for q in range(n)]


def _pair_exchange(grads):
    n = len(grads)

    def body(*refs):
        ins, got = refs[:n], refs[n:2 * n]
        send, recv = refs[2 * n:]
        x, y, c, _ = _where_am_i()
        sib = (x, y, 1 - c)
        for k in range(n):
            for ch in range(4):
                for l in range(2):
                    pltpu.make_async_remote_copy(
                        src_ref=ins[k].at[ch, 2 * (1 - c) + l], dst_ref=got[k].at[ch, l],
                        send_sem=send.at[k], recv_sem=recv.at[k], device_id=sib, device_id_type=MESH).start()
        for k in range(n):
            pltpu.make_async_remote_copy(
                src_ref=got[k], dst_ref=got[k], send_sem=send.at[k], recv_sem=recv.at[k],
                device_id=sib, device_id_type=MESH).wait()

    return pl.pallas_call(
        body, name="reduce_pair",
        out_shape=[jax.ShapeDtypeStruct((4, 2) + g.shape[2:], g.dtype) for g in grads],
        in_specs=[_ANY] * n, out_specs=[_ANY] * n,
        scratch_shapes=[pltpu.SemaphoreType.DMA((n,))] * 2,
    )(*grads)


def _pair_add(full, got, c):
    _, _, r, cols = full.shape
    tr = _tile(r, (512, 256, 352, 128))

    def kern(c_ref, a_ref, b_ref, o32_ref, o16_ref):
        tot = a_ref[...] + b_ref[...].astype(F32)
        o32_ref[...] = tot
        o16_ref[...] = tot.astype(o16_ref.dtype)

    blk = (None, None, tr, cols)
    mine = pl.BlockSpec(blk, lambda ch, l, i, cr: (ch, 2 * cr[0] + l, i, 0))
    same = pl.BlockSpec(blk, lambda ch, l, i, cr: (ch, l, i, 0))
    return pl.pallas_call(
        kern, name="pair_add",
        out_shape=[jax.ShapeDtypeStruct(got.shape, F32), jax.ShapeDtypeStruct(got.shape, got.dtype)],
        grid_spec=pltpu.PrefetchScalarGridSpec(
            num_scalar_prefetch=1, grid=(4, 2, r // tr), in_specs=[mine, same], out_specs=[same, same]),
        compiler_params=_params(("parallel", "parallel", "parallel")),
    )(c.reshape(1), full, got)


def _chip_scatter(parts):
    n = len(parts)

    def body(*refs):
        ins, outs = refs[:n], refs[n:2 * n]
        send, recv = refs[2 * n:]
        x, y, c, chips = _where_am_i()
        me = 2 * x + y
        for k in range(n):
            for j, (cx, cy) in enumerate(chips):
                for l in range(2):
                    pltpu.make_async_remote_copy(
                        src_ref=ins[k].at[2 * cx + cy, l], dst_ref=outs[k].at[me, l],
                        send_sem=send.at[k, j], recv_sem=recv.at[k, j],
                        device_id=(cx, cy, c), device_id_type=MESH).start()
        for k in range(n):
            for j, (cx, cy) in enumerate(chips):
                slot = outs[k].at[2 * cx + cy]
                pltpu.make_async_remote_copy(
                    src_ref=slot, dst_ref=slot, send_sem=send.at[k, j], recv_sem=recv.at[k, j],
                    device_id=(cx, cy, c), device_id_type=MESH).wait()

    return pl.pallas_call(
        body, name="reduce_chips",
        out_shape=[jax.ShapeDtypeStruct(p.shape, p.dtype) for p in parts],
        in_specs=[_ANY] * n, out_specs=[_ANY] * n,
        scratch_shapes=[pltpu.SemaphoreType.DMA((n, 3))] * 2,
    )(*parts)


def _chip_add(landed, mine, me, c):
    _, _, r, cols = landed.shape
    tr = _tile(r, (512, 256, 352, 128))

    def kern(me_ref, c_ref, land_ref, own_ref, o_ref):
        own = own_ref[...]
        tot = None
        for j in range(4):
            term = jnp.where(me_ref[0] == j, own, land_ref[j].astype(F32))
            tot = term if tot is None else tot + term
        o_ref[...] = tot

    return pl.pallas_call(
        kern, name="chip_add",
        out_shape=jax.ShapeDtypeStruct((4, r, cols), F32),
        grid_spec=pltpu.PrefetchScalarGridSpec(
            num_scalar_prefetch=2, grid=(2, r // tr),
            in_specs=[pl.BlockSpec((4, None, tr, cols), lambda l, i, mr, cr: (0, l, i, 0)),
                      pl.BlockSpec((None, None, tr, cols), lambda l, i, mr, cr: (mr[0], l, i, 0))],
            out_specs=pl.BlockSpec((None, tr, cols), lambda l, i, mr, cr: (2 * cr[0] + l, i, 0))),
        compiler_params=_params(("parallel", "parallel")),
    )(me.reshape(1), c.reshape(1), landed, mine)


def _pair_join(sums):
    n = len(sums)

    def body(*refs):
        bufs = refs[n:2 * n]
        send, recv = refs[2 * n:]
        x, y, c, _ = _where_am_i()
        sib = (x, y, 1 - c)
        for k in range(n):
            for l in range(2):
                for r0, rn in _row_chunks(bufs[k].shape[1]):
                    piece = bufs[k].at[2 * c + l, pl.ds(r0, rn)]
                    pltpu.make_async_remote_copy(
                        src_ref=piece, dst_ref=piece, send_sem=send.at[k], recv_sem=recv.at[k],
                        device_id=sib, device_id_type=MESH).start()
        for k in range(n):
            theirs = bufs[k].at[pl.ds(2 * (1 - c), 2)]
            pltpu.make_async_remote_copy(
                src_ref=theirs, dst_ref=theirs, send_sem=send.at[k], recv_sem=recv.at[k],
                device_id=sib, device_id_type=MESH).wait()

    return pl.pallas_call(
        body, name="reduce_join",
        out_shape=[jax.ShapeDtypeStruct(s.shape, s.dtype) for s in sums],
        in_specs=[_ANY] * n, out_specs=[_ANY] * n,
        input_output_aliases={k: k for k in range(n)},
        scratch_shapes=[pltpu.SemaphoreType.DMA((n,))] * 2,
    )(*sums)


def _allreduce_small(buf):
    R = buf.shape[0]

    def body(in_ref, out_ref, land, send, recv):
        x, y, c, _ = _where_am_i()
        me = 4 * x + 2 * y + c
        land[me] = in_ref[...]
        cps = []
        for k in range(1, 8):
            px, py, pc = x ^ (k >> 2), y ^ ((k >> 1) & 1), c ^ (k & 1)
            cps.append(pltpu.make_async_remote_copy(
                src_ref=in_ref, dst_ref=land.at[me], send_sem=send.at[k - 1], recv_sem=recv.at[k - 1],
                device_id=(px, py, pc), device_id_type=MESH))
        for cp in cps:
            cp.start()
        for k in range(1, 8):
            px, py, pc = x ^ (k >> 2), y ^ ((k >> 1) & 1), c ^ (k & 1)
            slot = land.at[4 * px + 2 * py + pc]
            pltpu.make_async_remote_copy(
                src_ref=slot, dst_ref=slot, send_sem=send.at[k - 1], recv_sem=recv.at[k - 1],
                device_id=(px, py, pc), device_id_type=MESH).wait_recv()
        for cp in cps:
            cp.wait_send()
        tot = land[0]
        for d in range(1, 8):
            tot = tot + land[d]
        out_ref[...] = tot

    vm = pl.BlockSpec(memory_space=pltpu.VMEM)
    return pl.pallas_call(
        body, name="allreduce_small",
        out_shape=jax.ShapeDtypeStruct(buf.shape, F32),
        in_specs=[vm], out_specs=vm,
        scratch_shapes=[pltpu.VMEM((8, R, LANES), F32), pltpu.SemaphoreType.DMA((7,)),
                        pltpu.SemaphoreType.DMA((7,))],
    )(buf)


def _rope_tables(T):
    pos = (jnp.arange(T) - FRONT).astype(F32)
    lane = jnp.arange(LANES)
    inv_a = ROPE_THETA ** (-(2 * ((lane % 64) % 32)).astype(F32) / SWA_HEAD_DIM)
    ang_a = pos[:, None] * inv_a[None, :]
    cosa, sina = jnp.cos(ang_a), jnp.sin(ang_a)
    inv_m = ROPE_THETA ** (-(2 * ((lane - KR_LANE) % 16)).astype(F32) / MLA_ROPE_DIM)
    ang_m = pos[:, None] * inv_m[None, :]
    on = ((lane >= KR_LANE) & (lane < KR_LANE + MLA_ROPE_DIM))[None, :]
    cosm = jnp.where(on, jnp.cos(ang_m), 1.0)
    sinm = jnp.where(on, jnp.sin(ang_m), 0.0)
    return cosa, sina, cosm, sinm


def _cols_from_chips(g):
    return jnp.concatenate(g, axis=-1)


def _rows_from_chips(g):
    return jnp.concatenate(g, axis=-2)


def _cols_to_chips(w):
    L, r, c4 = w.shape
    return jnp.moveaxis(w.reshape(L, r, 4, c4 // 4), 2, 0)


def _rows_to_chips(w):
    L, r4, c = w.shape
    return jnp.moveaxis(w.reshape(L, 4, r4 // 4, c), 1, 0)


def _layer_layouts(w_in, w_qup, w_kvup, w_o, w_gate, w_up, w_down):
    zpad = lambda n: jnp.zeros((D_MODEL, n), w_in.dtype)
    w_in = jnp.concatenate([w_in[:, :C_KR], zpad(KR_LANE), w_in[:, C_KR:IN_W],
                            zpad(LANES - KR_LANE - MLA_ROPE_DIM)], axis=-1)
    w_qup = w_qup.reshape(MLA_Q_RANK, MLA_HEADS, MLA_QK_DIM)
    w_qup = jnp.pad(w_qup, ((0, 0), (0, 0), (0, LANES - MLA_QK_DIM))).reshape(MLA_Q_RANK, SLOT_W)
    return w_in, w_qup, w_kvup, w_o, w_gate, w_up, w_down


def _local_step(x2, target, meta_full, layer_weights, p):
    T = BLOCK + x2.shape[0]
    L = DEPTH
    attn_norm, q_norm, kv_norm, sinks = p["attn_norm"], p["q_norm"], p["kv_norm"], p["sinks"]
    out_norm_swa, out_norm_mla, ffn_norm, final_norm = (
        p["out_norm_swa"], p["out_norm_mla"], p["ffn_norm"], p["final_norm"])

    cosa, sina, cosm, sinm = _rope_tables(T)
    no_sink = jnp.full((MLA_HEADS,), NEG, F32)
    scale_a, scale_b = SWA_HEAD_DIM ** -0.5, MLA_QK_DIM ** -0.5
    row = lambda v: v.reshape(1, -1)

    h = jnp.concatenate([jnp.zeros((FRONT, D_MODEL), F32), meta_full, x2], axis=0)
    saved = []
    weights = []
    for l in range(L):
        weights.append(_layer_layouts(*layer_weights(l, h)))
        W_in, W_qup, W_kvup, W_o, W_gate, W_up, W_down = weights[l]
        u, proj = _norm_proj("in_proj", h, row(attn_norm[l]), [W_in], False)
        qa, ka, va, qn, cn, kr = _prep1(proj, row(q_norm[l]), row(kv_norm[l]), cosa, sina, cosm, sinm)
        qb = _mm("q_up", qn, W_qup, "nn")
        kvb = _mm("kv_up", cn, W_kvup, "nn")
        qs, ks, vs = _prep2(qb, kvb, kr, cosm, sinm)
        oa, lse_a = _attn_fwd("swa_fwd", qa, ka, va, sinks[l], 4, False, scale_a)
        ob, lse_b = _attn_fwd("mla_fwd", qs, ks, vs, no_sink, 1, True, scale_b)
        mix = _merge_fwd(oa, ob, row(out_norm_swa[l]), row(out_norm_mla[l]))
        h1 = _mm("o_proj", mix, W_o, "nn", res=h)
        u2, a, b, hm = _norm_proj("ffn_in", h1, row(ffn_norm[l]), [W_gate, W_up], True)
        h2 = _mm("down_proj", hm, W_down, "nn", res=h1)
        saved.append((h, u, proj, qa, ka, va, qn, cn, qs, ks, vs, oa, lse_a, ob, lse_b, mix, h1, u2, a, b, hm))
        h = h2

    dh, d_final, loss_row = _loss_head(h, target, row(final_norm))

    gw = {k: [None] * L for k in ("in", "qup", "kvup", "o", "gate", "up", "down")}
    gs = {k: [None] * L for k in ("attn", "qn", "kvn", "sink", "ga", "gb", "ffn")}
    for l in reversed(range(L)):
        (h0, u, proj, qa, ka, va, qn, cn, qs, ks, vs, oa, lse_a, ob, lse_b, mix, h1, u2, a, b, hm) = saved[l]
        W_in, W_qup, W_kvup, W_o, W_gate, W_up, W_down = weights[l]
        gw["down"][l] = _mm("down_dw", hm, dh, "tn")
        da, db = _ffn_mid_bwd(dh, W_down, a, b)
        gw["gate"][l] = _mm("gate_dw", u2, da, "tn")
        gw["up"][l] = _mm("up_dw", u2, db, "tn")
        du2 = _mm("gate_dx", da, W_gate, "nt")
        du2 = _mm("up_dx", db, W_up, "nt", res=du2)
        dh1, gs["ffn"][l] = _rmsnorm_bwd("ffn_norm_bwd", h1, row(ffn_norm[l]), du2, dh)
        gw["o"][l] = _mm("o_dw", mix, dh1, "tn")
        dmix = _mm("o_dx", dh1, W_o, "nt")
        sink_slots = jnp.repeat(sinks[l], LANES).reshape(1, SLOT_W)
        doa, dla, dob, dlb, gs["ga"][l], gs["gb"][l], dsink = _merge_bwd(
            dmix, oa, ob, lse_a, row(out_norm_swa[l]), row(out_norm_mla[l]), sink_slots)
        gs["sink"][l] = dsink.reshape(SWA_HEADS, LANES)[:, 0]
        dqs, dks, dvs = _attn_bwd("mla_bwd", qs, ks, vs, dob, lse_b, dlb, 1, True, scale_b)
        dqa, dka, dva = _attn_bwd("swa_bwd", qa, ka, va, doa, lse_a, dla, 4, False, scale_a)
        dqb, dkvb, dkr = _prep2_bwd(dqs, dks, dvs, cosm, sinm)
        gw["qup"][l] = _mm("q_up_dw", qn, dqb, "tn")
        gw["kvup"][l] = _mm("kv_up_dw", cn, dkvb, "tn")
        dqn = _mm("q_up_dx", dqb, W_qup, "nt")
        dcn = _mm("kv_up_dx", dkvb, W_kvup, "nt")
        dproj, gs["qn"][l], gs["kvn"][l] = _prep1_bwd(
            proj, dqa, dka, dva, dqn, dcn, dkr, cosa, sina, row(q_norm[l]), row(kv_norm[l]))
        gw["in"][l] = _mm("in_dw", u, dproj, "tn")
        du = _mm("in_dx", dproj, W_in, "nt")
        dh, gs["attn"][l] = _rmsnorm_bwd("attn_norm_bwd", h0, row(attn_norm[l]), du, dh1)

    st = lambda k: jnp.stack(gw[k])
    d_in = st("in")
    d_in = jnp.concatenate([d_in[..., :C_KR], d_in[..., C_KR + KR_LANE:C_KR + KR_LANE + MLA_ROPE_DIM]], axis=-1)
    d_qup = st("qup").reshape(L, MLA_Q_RANK, MLA_HEADS, LANES)[..., :MLA_QK_DIM].reshape(L, MLA_Q_RANK, -1)
    d_nat = [d_in, d_qup, st("kvup"), st("o"), st("gate"), st("up"), st("down")]
    return loss_row, dh, d_nat, gs, d_final


def kernel(x, meta_tokens, attn_norm, w_in, q_norm, w_q_up, kv_norm, w_kv_up, sinks, out_norm_swa, out_norm_mla, w_o, ffn_norm, w_gate, w_up, w_down, final_norm, loss_target, m_meta_tokens, m_attn_norm, m_w_in, m_q_norm, m_w_q_up, m_kv_norm, m_w_kv_up, m_sinks, m_out_norm_swa, m_out_norm_mla, m_w_o, m_ffn_norm, m_w_gate, m_w_up, m_w_down, m_final_norm, v_meta_tokens, v_attn_norm, v_w_in, v_q_norm, v_w_q_up, v_kv_norm, v_w_kv_up, v_sinks, v_out_norm_swa, v_out_norm_mla, v_w_o, v_ffn_norm, v_w_gate, v_w_up, v_w_down, v_final_norm):
    assert x.shape[0] == 1 and x.shape[1] % BLOCK == 0
    big = [w_in, w_q_up, w_kv_up, w_o, w_gate, w_up, w_down]

    c_idx = lax.axis_index("c").astype(jnp.int32)
    chip = (2 * lax.axis_index("x") + lax.axis_index("y")).astype(jnp.int32)
    w16 = [w.astype(BF16) for w in big]
    join = [_cols_from_chips, _cols_from_chips, _cols_from_chips, _rows_from_chips, _cols_from_chips,
            _cols_from_chips, _rows_from_chips]

    def whole(own, landed):
        return [f([jnp.where(chip == j, o, g[j]) for j in range(4)]) for f, o, g in zip(join, own, landed)]

    first = _gather_weights([w[0] for w in w16], meta_tokens)
    meta_full = jnp.concatenate([jnp.where(chip == j, meta_tokens, first[-1][j]) for j in range(4)], axis=-1)
    send, recv, token, thru, lands = _gather_start([w[1:] for w in w16])
    state = {"thru": thru}

    def layer_weights(l, h):
        if l == 0:
            return whole([w[0] for w in w16], first[:-1])
        state["thru"], landed = _gather_wait("gather_wait%d" % l, send, recv, l - 1, state["thru"], lands[l - 1], h)
        return whole([w[l] for w in w16], landed)

    small_p = dict(attn_norm=attn_norm, q_norm=q_norm, kv_norm=kv_norm, sinks=sinks, out_norm_swa=out_norm_swa,
                   out_norm_mla=out_norm_mla, ffn_norm=ffn_norm, final_norm=final_norm)
    loss_row, dh, d_nat, gs, d_final = _local_step(x[0] + token[0, 0], loss_target[0], meta_full, layer_weights, small_p)
    grad_x = dh[BLOCK:][None]

    split = [_cols_to_chips, _cols_to_chips, _cols_to_chips, _rows_to_chips, _cols_to_chips, _cols_to_chips,
             _rows_to_chips]
    full = [f(d) for f, d in zip(split, d_nat)]

    got = _pair_exchange([f.astype(BF16) for f in full])
    sums = [_pair_add(f, g, c_idx) for f, g in zip(full, got)]
    landed = _chip_scatter([s16 for _, s16 in sums])
    g_big = _pair_join([_chip_add(t, s32, chip, c_idx) for t, (s32, _) in zip(landed, sums)])

    small = [jnp.stack(gs["attn"]).reshape(-1), jnp.stack(gs["qn"]).reshape(-1), jnp.stack(gs["kvn"]).reshape(-1),
             jnp.stack(gs["sink"]).reshape(-1), jnp.stack(gs["ga"]).reshape(-1), jnp.stack(gs["gb"]).reshape(-1),
             jnp.stack(gs["ffn"]).reshape(-1), d_final.reshape(-1)]
    sizes = [s.shape[0] for s in small]
    flat = jnp.concatenate(small + [dh[FRONT:BLOCK].reshape(-1), loss_row[0, :1]])
    n_flat = flat.shape[0]
    rows_needed = -(-n_flat // (8 * LANES)) * 8
    flat = jnp.pad(flat, (0, rows_needed * LANES - n_flat)).reshape(rows_needed, LANES)
    tot = _allreduce_small(flat).reshape(-1)
    n_small = sum(sizes)
    loss = tot[n_small + N_META * D_MODEL]
    g_meta_full = tot[n_small:n_small + N_META * D_MODEL].reshape(N_META, D_MODEL)
    g_meta_mine = lax.dynamic_slice_in_dim(g_meta_full, chip * (D_MODEL // 4), D_MODEL // 4, axis=1)

    small_w = [attn_norm, q_norm, kv_norm, sinks, out_norm_swa, out_norm_mla, ffn_norm, final_norm]
    small_m = [m_attn_norm, m_q_norm, m_kv_norm, m_sinks, m_out_norm_swa, m_out_norm_mla, m_ffn_norm, m_final_norm]
    small_v = [v_attn_norm, v_q_norm, v_kv_norm, v_sinks, v_out_norm_swa, v_out_norm_mla, v_ffn_norm, v_final_norm]
    n_rows = -(-n_small // (8 * LANES)) * 8

    def pack(arrs):
        f = jnp.concatenate([a.reshape(-1) for a in arrs])
        return jnp.pad(f, (0, n_rows * LANES - n_small), constant_values=1.0).reshape(n_rows, LANES)

    g_small_pack = jnp.pad(tot[:n_small], (0, n_rows * LANES - n_small)).reshape(n_rows, LANES)
    upd_small = _adamw("adam_small", pack(small_w), g_small_pack, pack(small_m), pack(small_v))

    def unpack(p):
        f = p.reshape(-1)
        out, off = [], 0
        for a, n in zip(small_w, sizes):
            out.append(f[off:off + n].reshape(a.shape))
            off += n
        return out

    g_small = unpack(g_small_pack)
    d_small, m_small, v_small = [unpack(p) for p in upd_small]
    d_meta, nm_meta, nv_meta = _adamw("adam_meta", meta_tokens, g_meta_mine, m_meta_tokens, v_meta_tokens)

    big_m = [m_w_in, m_w_q_up, m_w_kv_up, m_w_o, m_w_gate, m_w_up, m_w_down]
    big_v = [v_w_in, v_w_q_up, v_w_kv_up, v_w_o, v_w_gate, v_w_up, v_w_down]
    upd_big = [_adamw("adam_big", w, g, m, v) for w, g, m, v in zip(big, g_big, big_m, big_v)]

    names = ["meta_tokens", "attn_norm", "w_in", "q_norm", "w_q_up", "kv_norm", "w_kv_up", "sinks",
             "out_norm_swa", "out_norm_mla", "w_o", "ffn_norm", "w_gate", "w_up", "w_down", "final_norm"]
    small_idx = {"attn_norm": 0, "q_norm": 1, "kv_norm": 2, "sinks": 3, "out_norm_swa": 4,
                 "out_norm_mla": 5, "ffn_norm": 6, "final_norm": 7}
    big_idx = {"w_in": 0, "w_q_up": 1, "w_kv_up": 2, "w_o": 3, "w_gate": 4, "w_up": 5, "w_down": 6}
    grads, deltas, new_m, new_v = [], [], [], []
    for nme in names:
        if nme == "meta_tokens":
            quad = (g_meta_mine, d_meta, nm_meta, nv_meta)
        elif nme in small_idx:
            i = small_idx[nme]
            quad = (g_small[i], d_small[i], m_small[i], v_small[i])
        else:
            i = big_idx[nme]
            quad = (g_big[i], *upd_big[i])
        grads.append(quad[0]); deltas.append(quad[1]); new_m.append(quad[2]); new_v.append(quad[3])
    return (loss, grad_x, *grads, *deltas, *new_m, *new_v)
```

```python
import jax
import jax.numpy as jnp
from jax import lax
from jax.experimental import pallas as pl
from jax.experimental.pallas import tpu as pltpu

F32 = jnp.float32
BF16 = jnp.bfloat16
MXU_DTYPE = BF16

D_MODEL = 1024
DEPTH = 4
N_META = 16
BLOCK = 128
WINDOW = 128
ROPE_THETA = 10000.0
EPS = 1e-6
NEG = -1e30
SWA_HEADS = 8
SWA_KV_HEADS = 2
SWA_HEAD_DIM = 64
MLA_HEADS = 8
MLA_Q_RANK = 256
MLA_KV_RANK = 128
MLA_NOPE_DIM = 64
MLA_ROPE_DIM = 32
MLA_V_DIM = 64
MLA_QK_DIM = MLA_NOPE_DIM + MLA_ROPE_DIM
D_FF = 2816
FRONT = (-N_META) % BLOCK
LANES = 128
SLOT_W = 8 * LANES
C_QA, C_KA, C_VA, C_QL, C_KL, C_KR, IN_WP = 0, 512, 640, 768, 1024, 1152, 1280
KR_LANE = 64
IN_W = 1184

ADAM_LR, ADAM_B1, ADAM_B2, ADAM_EPS, ADAM_WD, ADAM_STEP = 0.001, 0.9, 0.999, 1e-08, 0.01, 10

VMEM_LIMIT = 48 * 1024 * 1024
MESH = pl.DeviceIdType.MESH


def _tile(n, prefs):
    for t in prefs:
        if n % t == 0:
            return t
    return n


def _params(sem):
    return pltpu.CompilerParams(dimension_semantics=sem, vmem_limit_bytes=VMEM_LIMIT)


_DIMS = {"nn": (((1,), (0,)), ((), ())), "nt": (((1,), (1,)), ((), ())), "tn": (((0,), (0,)), ((), ()))}


def _mm(name, a, b, mode, out_dtype=F32, res=None):
    if mode == "nn":
        (M, K), (_, N) = a.shape, b.shape
    elif mode == "nt":
        (M, K), (N, _) = a.shape, b.shape
    else:
        (K, M), (_, N) = a.shape, b.shape
    lane_tiles = (1408, 1024, 640, 768, 512, 384, 256, 128)
    row_tiles = (528, 512, 384, 256, 128)
    bm = _tile(M, lane_tiles if mode == "tn" else row_tiles)
    bn = _tile(N, lane_tiles)
    bk = _tile(K, (1056,) + row_tiles if mode == "tn" else lane_tiles)
    nk = K // bk
    if mode == "tn":
        a_spec = pl.BlockSpec((bk, bm), lambda i, j, k: (k, i))
    else:
        a_spec = pl.BlockSpec((bm, bk), lambda i, j, k: (i, k))
    if mode == "nt":
        b_spec = pl.BlockSpec((bn, bk), lambda i, j, k: (j, k))
    else:
        b_spec = pl.BlockSpec((bk, bn), lambda i, j, k: (k, j))
    o_spec = pl.BlockSpec((bm, bn), lambda i, j, k: (i, j))
    in_specs = [a_spec, b_spec]
    args = [a, b]
    if res is not None:
        in_specs.append(o_spec)
        args.append(res)
    dims = _DIMS[mode]

    def kern(a_ref, b_ref, *rest):
        if res is not None:
            r_ref, o_ref = rest[0], rest[1]
            scr = rest[2:]
        else:
            r_ref, o_ref = None, rest[0]
            scr = rest[1:]
        p = lax.dot_general(a_ref[...].astype(MXU_DTYPE), b_ref[...].astype(MXU_DTYPE), dims,
                            preferred_element_type=F32)

        def finish(val):
            if r_ref is not None:
                val = val + r_ref[...]
            o_ref[...] = val.astype(o_ref.dtype)

        if nk == 1:
            finish(p)
        else:
            acc = scr[0]
            k = pl.program_id(2)

            @pl.when(k == 0)
            def _():
                acc[...] = p

            @pl.when(k > 0)
            def _():
                acc[...] += p

            @pl.when(k == nk - 1)
            def _():
                finish(acc[...])

    return pl.pallas_call(
        kern, name=name,
        out_shape=jax.ShapeDtypeStruct((M, N), out_dtype),
        grid=(M // bm, N // bn, nk),
        in_specs=in_specs, out_specs=o_spec,
        scratch_shapes=[pltpu.VMEM((bm, bn), F32)] if nk > 1 else [],
        compiler_params=_params(("parallel", "parallel", "arbitrary")),
    )(*args)


def _rowmap(name, body, rows, vecs, outs, accs=(), tr_prefs=(384, 256, 128)):
    R = rows[0].shape[0]
    tr = _tile(R, tr_prefs)
    n_r, n_v, n_o, n_a = len(rows), len(vecs), len(outs), len(accs)

    def kern(*refs):
        ins = [r[...] for r in refs[:n_r + n_v]]
        o_refs = refs[n_r + n_v:n_r + n_v + n_o]
        a_refs = refs[n_r + n_v + n_o:]
        res = body(*ins)
        for o_ref, val in zip(o_refs, res[:n_o]):
            o_ref[...] = val.astype(o_ref.dtype)
        if n_a:
            first = pl.program_id(0) == 0

            @pl.when(first)
            def _():
                for a_ref, val in zip(a_refs, res[n_o:]):
                    a_ref[...] = val

            @pl.when(jnp.logical_not(first))
            def _():
                for a_ref, val in zip(a_refs, res[n_o:]):
                    a_ref[...] += val

    in_specs = [pl.BlockSpec((tr, r.shape[1]), lambda i: (i, 0)) for r in rows]
    in_specs += [pl.BlockSpec((1, v.shape[1]), lambda i: (0, 0)) for v in vecs]
    out_specs = [pl.BlockSpec((tr, c), lambda i: (i, 0)) for c, _ in outs]
    out_specs += [pl.BlockSpec((1, c), lambda i: (0, 0)) for c in accs]
    out_shape = [jax.ShapeDtypeStruct((R, c), dt) for c, dt in outs]
    out_shape += [jax.ShapeDtypeStruct((1, c), F32) for c in accs]
    return pl.pallas_call(
        kern, name=name, out_shape=out_shape, grid=(R // tr,),
        in_specs=in_specs, out_specs=out_specs,
        compiler_params=_params(("arbitrary",) if n_a else ("parallel",)),
    )(*rows, *vecs)


def _lane(shape):
    return lax.broadcasted_iota(jnp.int32, shape, 1)


def _rot_swa(x):
    lane = _lane(x.shape)
    return jnp.where((lane & 63) < 32, -pltpu.roll(x, 96, 1), pltpu.roll(x, 32, 1))


def _rot_mla(x):
    lane = _lane(x.shape)
    lo = jnp.where(lane >= KR_LANE, -pltpu.roll(x, 112, 1), 0.0)
    hi = jnp.where(lane < KR_LANE + MLA_ROPE_DIM, pltpu.roll(x, 16, 1), 0.0)
    return jnp.where(lane < KR_LANE + 16, lo, hi)


def _rope(x, cos, sin, rot):
    return x * cos + rot(x) * sin


def _rope_t(g, cos, sin, rot):
    return g * cos - rot(g * sin)


def _low(x):
    return jnp.where(_lane(x.shape) < 64, x, 0.0)


def _value_slot(x):
    lane = _lane(x.shape)
    return jnp.where(lane < 64, x, jnp.where(lane == 64, 1.0, 0.0))


def _blk(x, j):
    return x[:, j * LANES:(j + 1) * LANES]


def _rms_r(x, width):
    return lax.rsqrt(jnp.sum(x * x, axis=-1, keepdims=True) * (1.0 / width) + EPS)


def _rms_bwd(x, g, dy, width):
    r = _rms_r(x, width)
    gdy = dy * g
    dot = jnp.sum(gdy * x, axis=-1, keepdims=True)
    dx = r * gdy - x * (r * r * r * (1.0 / width) * dot)
    return dx, dy * x * r


def _colsum(x):
    return jnp.sum(x, axis=0, keepdims=True)


def _rmsnorm_bwd(name, x, g, dy, dres):
    def body(xv, dyv, dr, gv):
        dx, dg = _rms_bwd(xv, gv, dyv, D_MODEL)
        return dx + dr, _colsum(dg)
    return _rowmap(name, body, [x, dy, dres], [g], [(D_MODEL, F32)], [D_MODEL])


def _prep1(proj, qn_g, kv_g, cosa, sina, cosm, sinm):
    def body(p, ca, sa, cm, sm, gq, gk):
        qa = []
        for j in range(4):
            xr = _rope(_blk(p, j), ca, sa, _rot_swa)
            qa += [_low(xr), _low(pltpu.roll(xr, 64, 1))]
        kr_ = _rope(_blk(p, C_KA // LANES), ca, sa, _rot_swa)
        ka = [_low(kr_), _low(pltpu.roll(kr_, 64, 1))]
        vv = _blk(p, C_VA // LANES)
        va = [_value_slot(vv), _value_slot(pltpu.roll(vv, 64, 1))]
        ql = p[:, C_QL:C_QL + MLA_Q_RANK]
        qn = ql * _rms_r(ql, MLA_Q_RANK) * gq
        kl = p[:, C_KL:C_KL + MLA_KV_RANK]
        cn = kl * _rms_r(kl, MLA_KV_RANK) * gk
        kr = _rope(_blk(p, C_KR // LANES), cm, sm, _rot_mla)
        return (jnp.concatenate(qa, 1), jnp.concatenate(ka, 1), jnp.concatenate(va, 1), qn, cn, kr)
    return _rowmap("prep1", body, [proj, cosa, sina, cosm, sinm], [qn_g, kv_g],
                   [(SLOT_W, BF16), (2 * LANES, BF16), (2 * LANES, BF16),
                    (MLA_Q_RANK, BF16), (MLA_KV_RANK, BF16), (LANES, F32)])


def _prep2(qb, kvb, kr, cosm, sinm):
    def body(q, kv, krv, cm, sm):
        qs, ks, vs = [], [], []
        for h in range(MLA_HEADS):
            qs.append(_rope(_blk(q, h), cm, sm, _rot_mla))
            kvh = _blk(kv, h)
            ks.append(_low(kvh) + krv)
            vs.append(_value_slot(pltpu.roll(kvh, 64, 1)))
        return jnp.concatenate(qs, 1), jnp.concatenate(ks, 1), jnp.concatenate(vs, 1)
    return _rowmap("prep2", body, [qb, kvb, kr, cosm, sinm], [],
                   [(SLOT_W, BF16), (SLOT_W, BF16), (SLOT_W, BF16)])


def _compact(slots):
    return jnp.concatenate(
        [_blk(slots, 2 * j) + pltpu.roll(_blk(slots, 2 * j + 1), 64, 1) for j in range(4)], 1)


def _expand(nat):
    out = []
    for j in range(4):
        b = _blk(nat, j)
        out += [_low(b), _low(pltpu.roll(b, 64, 1))]
    return jnp.concatenate(out, 1)


def _merge_fwd(oa, ob, ga, gb):
    def body(a, b, gav, gbv):
        xa, xb = _compact(a), _compact(b)
        return (jnp.concatenate([xa * _rms_r(xa, 512) * gav, xb * _rms_r(xb, 512) * gbv], 1),)
    return _rowmap("merge_fwd", body, [oa, ob], [ga, gb], [(D_MODEL, BF16)])[0]


def _merge_bwd(dmix, oa, ob, lse_a, ga, gb, sink_slots):
    def body(dm, a, b, lse, gav, gbv, sk):
        outs = []
        accs = []
        for o, g, lo in ((a, gav, 0), (b, gbv, 512)):
            x = _compact(o)
            dx, dg = _rms_bwd(x, g, dm[:, lo:lo + 512], 512)
            do = _expand(dx)
            delta = jnp.concatenate(
                [jnp.broadcast_to(jnp.sum(_blk(do, h) * _blk(o, h), axis=-1, keepdims=True),
                                  (do.shape[0], LANES)) for h in range(8)], 1)
            outs += [do, delta]
            accs.append(_colsum(dg))
        dsink = _colsum(-jnp.exp(sk - lse) * outs[1])
        return (*outs, *accs, dsink)
    return _rowmap("merge_bwd", body, [dmix, oa, ob, lse_a], [ga, gb, sink_slots],
                   [(SLOT_W, BF16), (SLOT_W, F32), (SLOT_W, BF16), (SLOT_W, F32)],
                   [512, 512, SLOT_W])


def _prep2_bwd(dq, dk, dv, cosm, sinm):
    def body(dqv, dkv, dvv, cm, sm):
        dqb, dkvb = [], []
        krsum = jnp.zeros((dqv.shape[0], LANES), F32)
        for h in range(MLA_HEADS):
            dqb.append(_rope_t(_blk(dqv, h), cm, sm, _rot_mla))
            dkh = _blk(dkv, h)
            dkvb.append(_low(dkh) + pltpu.roll(_blk(dvv, h), 64, 1))
            krsum = krsum + dkh
        lane = _lane(krsum.shape)
        dkr = jnp.where((lane >= KR_LANE) & (lane < KR_LANE + MLA_ROPE_DIM),
                        _rope_t(krsum, cm, sm, _rot_mla), 0.0)
        return jnp.concatenate(dqb, 1), jnp.concatenate(dkvb, 1), dkr
    return _rowmap("prep2_bwd", body, [dq, dk, dv, cosm, sinm], [],
                   [(SLOT_W, BF16), (SLOT_W, BF16), (LANES, F32)])


def _prep1_bwd(proj, dqa, dka, dva, dqn, dcn, dkr, cosa, sina, qn_g, kv_g):
    def body(p, dq, dk, dv, dqnv, dcnv, dkrv, ca, sa, gq, gk):
        cols = []
        for j in range(4):
            nat = _blk(dq, 2 * j) + pltpu.roll(_blk(dq, 2 * j + 1), 64, 1)
            cols.append(_rope_t(nat, ca, sa, _rot_swa))
        grp = lambda d, g: sum(_blk(d, 4 * g + i) for i in range(4))
        cols.append(_rope_t(grp(dk, 0) + pltpu.roll(grp(dk, 1), 64, 1), ca, sa, _rot_swa))
        cols.append(grp(dv, 0) + pltpu.roll(grp(dv, 1), 64, 1))
        dql, dgq = _rms_bwd(p[:, C_QL:C_QL + MLA_Q_RANK], gq, dqnv, MLA_Q_RANK)
        dkl, dgk = _rms_bwd(p[:, C_KL:C_KL + MLA_KV_RANK], gk, dcnv, MLA_KV_RANK)
        cols += [dql, dkl, dkrv]
        return jnp.concatenate(cols, 1), _colsum(dgq), _colsum(dgk)
    return _rowmap("prep1_bwd", body, [proj, dqa, dka, dva, dqn, dcn, dkr, cosa, sina], [qn_g, kv_g],
                   [(IN_WP, BF16)], [MLA_Q_RANK, MLA_KV_RANK], tr_prefs=(192, 128))


def _sigmoid(x):
    return 1.0 / (1.0 + jnp.exp(-x))


def _ffn_tiles(T, F):
    return _tile(T, (528, 512, 384, 256, 128)), _tile(F, (1408, 1024, 640, 512, 256, 128))


def _norm_proj(name, h, g, weights, swiglu):
    (T, D), F = h.shape, weights[0].shape[1]
    bm, bn = _ffn_tiles(T, F)
    nw = len(weights)

    def kern(h_ref, g_ref, *rest):
        w_refs, u_ref, o_refs, u_scr = rest[:nw], rest[nw], rest[nw + 1:-1], rest[-1]

        @pl.when(pl.program_id(1) == 0)
        def _():
            x = h_ref[...]
            u = (x * _rms_r(x, D) * g_ref[...]).astype(u_scr.dtype)
            u_scr[...] = u
            u_ref[...] = u

        uv = u_scr[...]
        prods = [jnp.dot(uv, w[...], preferred_element_type=F32) for w in w_refs]
        for o_ref, p in zip(o_refs, prods):
            o_ref[...] = p
        if swiglu:
            a, b = prods
            o_refs[nw][...] = (a * _sigmoid(a) * b).astype(o_refs[nw].dtype)

    w_spec = pl.BlockSpec((D, bn), lambda i, j: (0, j))
    row_spec = pl.BlockSpec((bm, D), lambda i, j: (i, 0))
    o_spec = pl.BlockSpec((bm, bn), lambda i, j: (i, j))
    n_out = nw + (1 if swiglu else 0)
    return pl.pallas_call(
        kern, name=name,
        out_shape=[jax.ShapeDtypeStruct((T, D), MXU_DTYPE)] + [jax.ShapeDtypeStruct((T, F), F32)] * nw
        + ([jax.ShapeDtypeStruct((T, F), MXU_DTYPE)] if swiglu else []),
        grid=(T // bm, F // bn),
        in_specs=[row_spec, pl.BlockSpec((1, D), lambda i, j: (0, 0))] + [w_spec] * nw,
        out_specs=[row_spec] + [o_spec] * n_out,
        scratch_shapes=[pltpu.VMEM((bm, D), MXU_DTYPE)],
        compiler_params=_params(("parallel", "arbitrary")),
    )(h, g, *weights)


def _ffn_mid_bwd(dh, w_down, a, b):
    (T, D), F = dh.shape, w_down.shape[0]
    bm, bn = _ffn_tiles(T, F)

    def kern(dh_ref, wd_ref, a_ref, b_ref, da_ref, db_ref):
        d = lax.dot_general(dh_ref[...].astype(MXU_DTYPE), wd_ref[...], _DIMS["nt"], preferred_element_type=F32)
        av, bv = a_ref[...], b_ref[...]
        s = _sigmoid(av)
        da_ref[...] = (d * bv * (s * (1.0 + av * (1.0 - s)))).astype(da_ref.dtype)
        db_ref[...] = (d * (av * s)).astype(db_ref.dtype)

    o_spec = pl.BlockSpec((bm, bn), lambda i, j: (i, j))
    return pl.pallas_call(
        kern, name="ffn_mid_bwd",
        out_shape=[jax.ShapeDtypeStruct((T, F), MXU_DTYPE)] * 2,
        grid=(T // bm, F // bn),
        in_specs=[pl.BlockSpec((bm, D), lambda i, j: (i, 0)), pl.BlockSpec((bn, D), lambda i, j: (j, 0)),
                  o_spec, o_spec],
        out_specs=[o_spec, o_spec],
        compiler_params=_params(("parallel", "parallel")),
    )(dh, w_down, a, b)


def _loss_head(h, target, g):
    T = h.shape[0]
    nb = T // BLOCK

    def kern(h_ref, t_ref, g_ref, dh_ref, dg_ref, loss_ref, acc):
        i = pl.program_id(0)

        @pl.when(i == 0)
        def _():
            dh_ref[...] = jnp.zeros_like(dh_ref)
            dg_ref[...] = jnp.zeros_like(dg_ref)
            acc[...] = jnp.zeros_like(acc)

        @pl.when(i > 0)
        def _():
            x = h_ref[...]
            gv = g_ref[...]
            e = x * _rms_r(x, D_MODEL) * gv - t_ref[...]
            acc[...] += _colsum(e * e)
            dx, dg = _rms_bwd(x, gv, e * (1.0 / D_MODEL), D_MODEL)
            dh_ref[...] = dx
            dg_ref[...] += _colsum(dg)

        @pl.when(i == nb - 1)
        def _():
            tot = jnp.sum(acc[...], axis=-1, keepdims=True) * (0.5 / D_MODEL)
            loss_ref[...] = jnp.broadcast_to(tot, loss_ref.shape)

    return pl.pallas_call(
        kern, name="loss_head",
        out_shape=[jax.ShapeDtypeStruct((T, D_MODEL), F32), jax.ShapeDtypeStruct((1, D_MODEL), F32),
                   jax.ShapeDtypeStruct((1, LANES), F32)],
        grid=(nb,),
        in_specs=[pl.BlockSpec((BLOCK, D_MODEL), lambda i: (i, 0)),
                  pl.BlockSpec((BLOCK, D_MODEL), lambda i: (jnp.maximum(i - 1, 0), 0)),
                  pl.BlockSpec((1, D_MODEL), lambda i: (0, 0))],
        out_specs=[pl.BlockSpec((BLOCK, D_MODEL), lambda i: (i, 0)),
                   pl.BlockSpec((1, D_MODEL), lambda i: (0, 0)),
                   pl.BlockSpec((1, LANES), lambda i: (0, 0))],
        scratch_shapes=[pltpu.VMEM((1, D_MODEL), F32)],
        compiler_params=_params(("arbitrary",)),
    )(h, target, g)


LOG2E = 1.4426950408889634


def _attn_plan(T, causal):
    tq = _tile(T, (384, 256, 128))
    ck = min(2 * tq, T) if causal else min(tq + WINDOW, T)
    return tq, ck, (-(-T // ck) if causal else 1)


def _chunk(i, c, T, tq, ck, causal):
    if causal:
        return pl.multiple_of(jnp.minimum(c * ck, T - ck), LANES), c * ck
    return pl.multiple_of(jnp.clip(i * tq - WINDOW, 0, T - ck), LANES), 0


def _n_chunks(i, tq, ck, causal):
    return ((i + 1) * tq + ck - 1) // ck if causal else 1


def _mask(s, i, start, first, tq, ck, causal):
    qpos = i * tq + lax.broadcasted_iota(jnp.int32, (tq, 1), 0)
    kpos = start + lax.broadcasted_iota(jnp.int32, (tq, ck), 1)
    low = jnp.maximum(jnp.where(qpos < FRONT, 0, FRONT), first)
    if not causal:
        low = jnp.maximum(low, qpos - (WINDOW - 1))
    return jnp.where(kpos >= low, jnp.where(kpos <= qpos, s, NEG), NEG)


def _chunk_loop(n, body, init, several):
    carry = body(0, init, True)
    if not several:
        return carry
    carry = body(n - 1, carry, True)
    return lax.fori_loop(1, n - 1, lambda c, cr: body(c, cr, False), carry)


def _tile_sweep(nq, tq, ck, causal, q_tile):
    one = min(nq, ck // tq) if causal else nq
    lax.fori_loop(0, one, lambda i, cr: q_tile(i, False) or cr, 0)
    if one < nq:
        lax.fori_loop(one, nq, lambda i, cr: q_tile(i, True) or cr, 0)


BWD_HEADS_PER_STEP = 1


def _head_cols(group, hp):
    q_cols = lambda hh: slice(hh * LANES, (hh + 1) * LANES)
    if group == 1:
        return q_cols, q_cols, hp * LANES
    assert group % hp == 0
    return q_cols, (lambda hh: slice(0, LANES)), LANES


def _whole(T, width, index, single):
    if single:
        return pl.BlockSpec((T, width), index, pipeline_mode=pl.Buffered(1))
    return pl.BlockSpec((T, width), index)


def _attn_fwd(name, q, k, v, sinks, group, causal, scale, hp):
    T = q.shape[0]
    H = q.shape[1] // LANES
    HP = hp
    tq, ck, slots = _attn_plan(T, causal)
    nq = T // tq
    c2 = scale * LOG2E
    q_cols, k_cols, kw = _head_cols(group, HP)

    def kern(sink_ref, q_ref, k_ref, v_ref, o_ref, lse_ref, s_scr):
        sink2 = [sink_ref[pl.program_id(0) * HP + hh] * LOG2E for hh in range(HP)]

        def q_tile(i, several):
            rows = pl.ds(pl.multiple_of(i * tq, tq), tq)
            qq = [q_ref[rows, q_cols(hh)] for hh in range(HP)]
            n = _n_chunks(i, tq, ck, causal)

            def score(c, m2, masked):
                start, first = _chunk(i, c, T, tq, ck, causal)
                out = []
                for hh in range(HP):
                    s = lax.dot_general(qq[hh], k_ref[pl.ds(start, ck), k_cols(hh)], _DIMS["nt"],
                                        preferred_element_type=F32) * c2
                    if masked:
                        s = _mask(s, i, start, first, tq, ck, causal)
                    s_scr[hh, c] = s
                    out.append(jnp.maximum(m2[hh], jnp.max(s, axis=-1, keepdims=True)))
                return tuple(out)

            m2 = _chunk_loop(n, score, tuple(jnp.full((tq, 1), sk, F32) for sk in sink2), several)

            def weigh(c, acc):
                start, _ = _chunk(i, c, T, tq, ck, causal)
                out = []
                for hh in range(HP):
                    p = jnp.exp2(s_scr[hh, c] - m2[hh])
                    out.append(acc[hh] + jnp.dot(p.astype(MXU_DTYPE), v_ref[pl.ds(start, ck), k_cols(hh)],
                                                 preferred_element_type=F32))
                return tuple(out)

            acc = lax.fori_loop(0, n, weigh, tuple(jnp.zeros((tq, LANES), F32) for _ in range(HP)))
            lane = _lane((tq, LANES))
            for hh in range(HP):
                l = acc[hh][:, 64:65] + jnp.exp2(sink2[hh] - m2[hh])
                o_ref[rows, q_cols(hh)] = jnp.where(lane < 64, acc[hh] / l, 0.0)
                lse_ref[rows, q_cols(hh)] = jnp.broadcast_to(m2[hh] * (1.0 / LOG2E) + jnp.log(l), (tq, LANES))

        _tile_sweep(nq, tq, ck, causal, q_tile)

    q_spec = _whole(T, HP * LANES, lambda g: (0, g), True)
    kv_spec = _whole(T, kw, (lambda g: (0, g)) if group == 1 else (lambda g: (0, (g * HP) // group)), True)
    return pl.pallas_call(
        kern, name=name,
        out_shape=[jax.ShapeDtypeStruct((T, H * LANES), F32)] * 2,
        grid=(H // HP,),
        in_specs=[pl.BlockSpec(memory_space=pltpu.SMEM), q_spec, kv_spec, kv_spec],
        out_specs=[q_spec, q_spec],
        scratch_shapes=[pltpu.VMEM((HP, slots, tq, ck), F32)],
        compiler_params=_params(("parallel",)),
    )(sinks, q, k, v)


def _attn_bwd(name, q, k, v, do, lse, delta, group, causal, scale):
    T = q.shape[0]
    H = q.shape[1] // LANES
    HP = BWD_HEADS_PER_STEP
    tq, ck, _ = _attn_plan(T, causal)
    nq = T // tq
    c2 = scale * LOG2E
    q_cols, k_cols, kw = _head_cols(group, HP)

    def kern(q_ref, k_ref, v_ref, do_ref, lse_ref, dl_ref, dq_ref, dk_ref, dv_ref):
        dk_ref[...] = jnp.zeros_like(dk_ref)
        dv_ref[...] = jnp.zeros_like(dv_ref)

        def q_tile(i, several):
            rows = pl.ds(pl.multiple_of(i * tq, tq), tq)
            qq = [q_ref[rows, q_cols(hh)] for hh in range(HP)]
            dd = [do_ref[rows, q_cols(hh)] for hh in range(HP)]
            lse2 = [lse_ref[rows, q_cols(hh)][:, 0:1] * LOG2E for hh in range(HP)]
            dl_c = [dl_ref[rows, q_cols(hh)][:, 0:1] for hh in range(HP)]

            def chunk(c, dq, masked):
                start, first = _chunk(i, c, T, tq, ck, causal)
                keys = pl.ds(start, ck)
                out = []
                for hh in range(HP):
                    kk, vv = k_ref[keys, k_cols(hh)], v_ref[keys, k_cols(hh)]
                    s = lax.dot_general(qq[hh], kk, _DIMS["nt"], preferred_element_type=F32)
                    if masked:
                        s = _mask(s, i, start, first, tq, ck, causal)
                    p = jnp.exp2(s * c2 - lse2[hh])
                    dv_ref[keys, q_cols(hh)] += lax.dot_general(p.astype(MXU_DTYPE), dd[hh], _DIMS["tn"],
                                                                preferred_element_type=F32)
                    dp = lax.dot_general(dd[hh], vv, _DIMS["nt"], preferred_element_type=F32)
                    ds = (p * (dp - dl_c[hh])).astype(MXU_DTYPE)
                    dk_ref[keys, q_cols(hh)] += lax.dot_general(ds, qq[hh], _DIMS["tn"],
                                                                preferred_element_type=F32) * scale
                    out.append(dq[hh] + jnp.dot(ds, kk, preferred_element_type=F32))
                return tuple(out)

            dq = _chunk_loop(_n_chunks(i, tq, ck, causal), chunk,
                             tuple(jnp.zeros((tq, LANES), F32) for _ in range(HP)), several)
            for hh in range(HP):
                dq_ref[rows, q_cols(hh)] = dq[hh] * scale

        _tile_sweep(nq, tq, ck, causal, q_tile)

    q_spec = _whole(T, HP * LANES, lambda g: (0, g), False)
    kv_spec = _whole(T, kw, (lambda g: (0, g)) if group == 1 else (lambda g: (0, (g * HP) // group)), False)
    return pl.pallas_call(
        kern, name=name,
        out_shape=[jax.ShapeDtypeStruct((T, H * LANES), F32)] * 3,
        grid=(H // HP,),
        in_specs=[q_spec, kv_spec, kv_spec, q_spec, q_spec, q_spec],
        out_specs=[q_spec, q_spec, q_spec],
        compiler_params=_params(("parallel",)),
    )(q, k, v, do, lse, delta)


def _ew(name, fn, ins, out_dtypes):
    shape = ins[0].shape
    flat = [a.reshape(-1, shape[-1]) for a in ins]
    R, C = flat[0].shape
    tr = _tile(R, (512, 256, 128, 64, 32, 16, 8))
    n_in = len(ins)

    def kern(*refs):
        res = fn(*[r[...] for r in refs[:n_in]])
        for o_ref, val in zip(refs[n_in:], res):
            o_ref[...] = val.astype(o_ref.dtype)

    spec = pl.BlockSpec((tr, C), lambda i: (i, 0))
    outs = pl.pallas_call(
        kern, name=name,
        out_shape=[jax.ShapeDtypeStruct((R, C), dt) for dt in out_dtypes],
        grid=(R // tr,), in_specs=[spec] * n_in, out_specs=[spec] * len(out_dtypes),
        compiler_params=_params(("parallel",)),
    )(*flat)
    return [o.reshape(shape) for o in outs]


def _adamw(name, w, g, m, v):
    c1 = 1.0 - ADAM_B1 ** ADAM_STEP
    c2 = 1.0 - ADAM_B2 ** ADAM_STEP

    def fn(wv, gv, mv, vv):
        mn = ADAM_B1 * mv + (1.0 - ADAM_B1) * gv
        vn = ADAM_B2 * vv + (1.0 - ADAM_B2) * (gv * gv)
        delta = -ADAM_LR * ((mn / c1) / (jnp.sqrt(vn / c2) + ADAM_EPS) + ADAM_WD * wv)
        return delta, mn, vn

    return _ew(name, fn, [w, g, m, v], [F32, F32, F32])


_ANY = pl.BlockSpec(memory_space=pl.ANY)


def _where_am_i():
    x, y, c = lax.axis_index("x"), lax.axis_index("y"), lax.axis_index("c")
    chips = [(1 - x, y), (x, 1 - y), (1 - x, 1 - y)]
    return x, y, c, chips


def _gather_weights(shards, meta):
    arrs = list(shards) + [meta]
    n = len(arrs)
    per = [a.shape[0] // 2 for a in arrs]

    def body(*refs):
        ins, outs = refs[:n], refs[n:2 * n]
        send1, recv1, send2, recv2 = refs[2 * n:]
        x, y, c, chips = _where_am_i()
        me = 2 * x + y

        def half(ref, k, cc):
            return ref.at[pl.ds(per[k] * cc, per[k])]

        first = []
        for k in range(n):
            for j, (cx, cy) in enumerate(chips):
                first.append(pltpu.make_async_remote_copy(
                    src_ref=half(ins[k], k, c), dst_ref=half(outs[k].at[me], k, c),
                    send_sem=send1.at[k, j], recv_sem=recv1.at[k, j],
                    device_id=(cx, cy, c), device_id_type=MESH))
        for cp in first:
            cp.start()
        passed = []
        for k in range(n):
            for j, (cx, cy) in enumerate(chips):
                landed = half(outs[k].at[2 * cx + cy], k, c)
                pltpu.make_async_remote_copy(
                    src_ref=landed, dst_ref=landed, send_sem=send1.at[k, j], recv_sem=recv1.at[k, j],
                    device_id=(cx, cy, c), device_id_type=MESH).wait_recv()
                fwd = pltpu.make_async_remote_copy(
                    src_ref=landed, dst_ref=landed, send_sem=send2.at[k, j], recv_sem=recv2.at[k, j],
                    device_id=(x, y, 1 - c), device_id_type=MESH)
                fwd.start()
                passed.append(fwd)
        for k in range(n):
            for j, (cx, cy) in enumerate(chips):
                other = half(outs[k].at[2 * cx + cy], k, 1 - c)
                pltpu.make_async_remote_copy(
                    src_ref=other, dst_ref=other, send_sem=send2.at[k, j], recv_sem=recv2.at[k, j],
                    device_id=(x, y, 1 - c), device_id_type=MESH).wait_recv()
        for cp in first + passed:
            cp.wait_send()

    return pl.pallas_call(
        body, name="gather_weights",
        out_shape=[jax.ShapeDtypeStruct((4,) + a.shape, a.dtype) for a in arrs],
        in_specs=[_ANY] * n, out_specs=[_ANY] * n,
        scratch_shapes=[pltpu.SemaphoreType.DMA((n, 3))] * 4,
    )(*arrs)


_HBM = pl.BlockSpec(memory_space=pltpu.HBM)
_SEM = pl.BlockSpec(memory_space=pltpu.SEMAPHORE)
_FLOWS = pltpu.SideEffectType.DATAFLOW_SIDE_EFFECTING


def _in_hbm(a):
    return pltpu.with_memory_space_constraint(a, pltpu.HBM)


def _gather_start(shards):
    n = len(shards)
    nl = shards[0].shape[0]
    lands = [lax.empty((4,) + sh.shape[1:], sh.dtype) for _ in range(nl) for sh in shards]

    def body(*refs):
        srcs, land = refs[:n], refs[n:n + n * nl]
        send, recv, token = refs[n + n * nl:n + n * nl + 3]
        x, y, c, chips = _where_am_i()
        me = 2 * x + y
        for l in range(nl):
            for k in range(n):
                for cx, cy in chips:
                    pltpu.make_async_remote_copy(
                        src_ref=srcs[k].at[l], dst_ref=land[l * n + k].at[me],
                        send_sem=send.at[l], recv_sem=recv.at[l],
                        device_id=(cx, cy, c), device_id_type=MESH).start()
        token[...] = jnp.zeros_like(token)

    ins = [_in_hbm(a) for a in list(shards) + lands]
    res = pl.pallas_call(
        body, name="gather_start",
        out_shape=(pltpu.SemaphoreType.DMA((nl,)), pltpu.SemaphoreType.DMA((nl,)),
                   jax.ShapeDtypeStruct((8, LANES), F32)) + tuple(pltpu.HBM(a.shape, a.dtype) for a in ins),
        in_specs=[_HBM] * len(ins),
        out_specs=(_SEM, _SEM, pl.BlockSpec(memory_space=pltpu.VMEM)) + (_HBM,) * len(ins),
        input_output_aliases={i: 3 + i for i in range(len(ins))},
        compiler_params=pltpu.CompilerParams(has_side_effects=_FLOWS),
    )(*ins)
    send, recv, token = res[:3]
    thru = res[3:3 + n]
    lands = res[3 + n:]
    return send, recv, token, list(thru), [list(lands[l * n:(l + 1) * n]) for l in range(nl)]


def _gather_wait(name, send, recv, l, thru, land, after):
    n = len(thru)

    def body(*refs):
        srcs, lands = refs[:n], refs[n:2 * n]
        send_sem, recv_sem = refs[2 * n:2 * n + 2]
        x, y, c, chips = _where_am_i()
        for k in range(n):
            for cx, cy in chips:
                copy = pltpu.make_async_remote_copy(
                    src_ref=srcs[k].at[l], dst_ref=lands[k].at[2 * cx + cy],
                    send_sem=send_sem.at[l], recv_sem=recv_sem.at[l],
                    device_id=(cx, cy, c), device_id_type=MESH)
                copy.wait_send()
                copy.wait_recv()

    ins = list(thru) + list(land)
    res = pl.pallas_call(
        body, name=name,
        out_shape=tuple(pltpu.HBM(a.shape, a.dtype) for a in ins),
        in_specs=[_HBM] * len(ins) + [_SEM, _SEM, _ANY],
        out_specs=(_HBM,) * len(ins),
        input_output_aliases={i: i for i in range(len(ins))},
        compiler_params=pltpu.CompilerParams(has_side_effects=_FLOWS),
    )(*ins, send, recv, after)
    return list(res[:n]), list(res[n:])


def _row_chunks(r):
    n = 4 if r % 64 == 0 else 1
    return [(q * (r // n), r // n) for q in range(n)]


def _pair_exchange(grads):
    n = len(grads)

    def body(*refs):
        ins, got = refs[:n], refs[n:2 * n]
        send, recv = refs[2 * n:]
        x, y, c, _ = _where_am_i()
        sib = (x, y, 1 - c)
        for k in range(n):
            for ch in range(4):
                for l in range(2):
                    pltpu.make_async_remote_copy(
                        src_ref=ins[k].at[ch, 2 * (1 - c) + l], dst_ref=got[k].at[ch, l],
                        send_sem=send.at[k], recv_sem=recv.at[k], device_id=sib, device_id_type=MESH).start()
        for k in range(n):
            pltpu.make_async_remote_copy(
                src_ref=got[k], dst_ref=got[k], send_sem=send.at[k], recv_sem=recv.at[k],
                device_id=sib, device_id_type=MESH).wait()

    return pl.pallas_call(
        body, name="reduce_pair",
        out_shape=[jax.ShapeDtypeStruct((4, 2) + g.shape[2:], g.dtype) for g in grads],
        in_specs=[_ANY] * n, out_specs=[_ANY] * n,
        scratch_shapes=[pltpu.SemaphoreType.DMA((n,))] * 2,
    )(*grads)


def _pair_add(full, got, c):
    _, _, r, cols = full.shape
    tr = _tile(r, (512, 256, 352, 128))

    def kern(c_ref, a_ref, b_ref, o32_ref, o16_ref):
        tot = a_ref[...] + b_ref[...].astype(F32)
        o32_ref[...] = tot
        o16_ref[...] = tot.astype(o16_ref.dtype)

    blk = (None, None, tr, cols)
    mine = pl.BlockSpec(blk, lambda ch, l, i, cr: (ch, 2 * cr[0] + l, i, 0))
    same = pl.BlockSpec(blk, lambda ch, l, i, cr: (ch, l, i, 0))
    return pl.pallas_call(
        kern, name="pair_add",
        out_shape=[jax.ShapeDtypeStruct(got.shape, F32), jax.ShapeDtypeStruct(got.shape, got.dtype)],
        grid_spec=pltpu.PrefetchScalarGridSpec(
            num_scalar_prefetch=1, grid=(4, 2, r // tr), in_specs=[mine, same], out_specs=[same, same]),
        compiler_params=_params(("parallel", "parallel", "parallel")),
    )(c.reshape(1), full, got)


def _chip_scatter(parts):
    n = len(parts)

    def body(*refs):
        ins, outs = refs[:n], refs[n:2 * n]
        send, recv = refs[2 * n:]
        x, y, c, chips = _where_am_i()
        me = 2 * x + y
        for k in range(n):
            for j, (cx, cy) in enumerate(chips):
                for l in range(2):
                    pltpu.make_async_remote_copy(
                        src_ref=ins[k].at[2 * cx + cy, l], dst_ref=outs[k].at[me, l],
                        send_sem=send.at[k, j], recv_sem=recv.at[k, j],
                        device_id=(cx, cy, c), device_id_type=MESH).start()
        for k in range(n):
            for j, (cx, cy) in enumerate(chips):
                slot = outs[k].at[2 * cx + cy]
                pltpu.make_async_remote_copy(
                    src_ref=slot, dst_ref=slot, send_sem=send.at[k, j], recv_sem=recv.at[k, j],
                    device_id=(cx, cy, c), device_id_type=MESH).wait()

    return pl.pallas_call(
        body, name="reduce_chips",
        out_shape=[jax.ShapeDtypeStruct(p.shape, p.dtype) for p in parts],
        in_specs=[_ANY] * n, out_specs=[_ANY] * n,
        scratch_shapes=[pltpu.SemaphoreType.DMA((n, 3))] * 2,
    )(*parts)


def _chip_add(landed, mine, me, c):
    _, _, r, cols = landed.shape
    tr = _tile(r, (512, 256, 352, 128))

    def kern(me_ref, c_ref, land_ref, own_ref, o_ref):
        own = own_ref[...]
        tot = None
        for j in range(4):
            term = jnp.where(me_ref[0] == j, own, land_ref[j].astype(F32))
            tot = term if tot is None else tot + term
        o_ref[...] = tot

    return pl.pallas_call(
        kern, name="chip_add",
        out_shape=jax.ShapeDtypeStruct((4, r, cols), F32),
        grid_spec=pltpu.PrefetchScalarGridSpec(
            num_scalar_prefetch=2, grid=(2, r // tr),
            in_specs=[pl.BlockSpec((4, None, tr, cols), lambda l, i, mr, cr: (0, l, i, 0)),
                      pl.BlockSpec((None, None, tr, cols), lambda l, i, mr, cr: (mr[0], l, i, 0))],
            out_specs=pl.BlockSpec((None, tr, cols), lambda l, i, mr, cr: (2 * cr[0] + l, i, 0))),
        compiler_params=_params(("parallel", "parallel")),
    )(me.reshape(1), c.reshape(1), landed, mine)


def _pair_join(sums):
    n = len(sums)

    def body(*refs):
        bufs = refs[n:2 * n]
        send, recv = refs[2 * n:]
        x, y, c, _ = _where_am_i()
        sib = (x, y, 1 - c)
        for k in range(n):
            for l in range(2):
                for r0, rn in _row_chunks(bufs[k].shape[1]):
                    piece = bufs[k].at[2 * c + l, pl.ds(r0, rn)]
                    pltpu.make_async_remote_copy(
                        src_ref=piece, dst_ref=piece, send_sem=send.at[k], recv_sem=recv.at[k],
                        device_id=sib, device_id_type=MESH).start()
        for k in range(n):
            theirs = bufs[k].at[pl.ds(2 * (1 - c), 2)]
            pltpu.make_async_remote_copy(
                src_ref=theirs, dst_ref=theirs, send_sem=send.at[k], recv_sem=recv.at[k],
                device_id=sib, device_id_type=MESH).wait()

    return pl.pallas_call(
        body, name="reduce_join",
        out_shape=[jax.ShapeDtypeStruct(s.shape, s.dtype) for s in sums],
        in_specs=[_ANY] * n, out_specs=[_ANY] * n,
        input_output_aliases={k: k for k in range(n)},
        scratch_shapes=[pltpu.SemaphoreType.DMA((n,))] * 2,
    )(*sums)


def _allreduce_small(buf):
    R = buf.shape[0]

    def body(in_ref, out_ref, land, send, recv):
        x, y, c, _ = _where_am_i()
        me = 4 * x + 2 * y + c
        land[me] = in_ref[...]
        cps = []
        for k in range(1, 8):
            px, py, pc = x ^ (k >> 2), y ^ ((k >> 1) & 1), c ^ (k & 1)
            cps.append(pltpu.make_async_remote_copy(
                src_ref=in_ref, dst_ref=land.at[me], send_sem=send.at[k - 1], recv_sem=recv.at[k - 1],
                device_id=(px, py, pc), device_id_type=MESH))
        for cp in cps:
            cp.start()
        for k in range(1, 8):
            px, py, pc = x ^ (k >> 2), y ^ ((k >> 1) & 1), c ^ (k & 1)
            slot = land.at[4 * px + 2 * py + pc]
            pltpu.make_async_remote_copy(
                src_ref=slot, dst_ref=slot, send_sem=send.at[k - 1], recv_sem=recv.at[k - 1],
                device_id=(px, py, pc), device_id_type=MESH).wait_recv()
        for cp in cps:
            cp.wait_send()
        tot = land[0]
        for d in range(1, 8):
            tot = tot + land[d]
        out_ref[...] = tot

    vm = pl.BlockSpec(memory_space=pltpu.VMEM)
    return pl.pallas_call(
        body, name="allreduce_small",
        out_shape=jax.ShapeDtypeStruct(buf.shape, F32),
        in_specs=[vm], out_specs=vm,
        scratch_shapes=[pltpu.VMEM((8, R, LANES), F32), pltpu.SemaphoreType.DMA((7,)),
                        pltpu.SemaphoreType.DMA((7,))],
    )(buf)


def _rope_tables(T):
    pos = (jnp.arange(T) - FRONT).astype(F32)
    lane = jnp.arange(LANES)
    inv_a = ROPE_THETA ** (-(2 * ((lane % 64) % 32)).astype(F32) / SWA_HEAD_DIM)
    ang_a = pos[:, None] * inv_a[None, :]
    cosa, sina = jnp.cos(ang_a), jnp.sin(ang_a)
    inv_m = ROPE_THETA ** (-(2 * ((lane - KR_LANE) % 16)).astype(F32) / MLA_ROPE_DIM)
    ang_m = pos[:, None] * inv_m[None, :]
    on = ((lane >= KR_LANE) & (lane < KR_LANE + MLA_ROPE_DIM))[None, :]
    cosm = jnp.where(on, jnp.cos(ang_m), 1.0)
    sinm = jnp.where(on, jnp.sin(ang_m), 0.0)
    return cosa, sina, cosm, sinm


def _cols_from_chips(g):
    return jnp.concatenate(g, axis=-1)


def _rows_from_chips(g):
    return jnp.concatenate(g, axis=-2)


def _cols_to_chips(w):
    L, r, c4 = w.shape
    return jnp.moveaxis(w.reshape(L, r, 4, c4 // 4), 2, 0)


def _rows_to_chips(w):
    L, r4, c = w.shape
    return jnp.moveaxis(w.reshape(L, 4, r4 // 4, c), 1, 0)


def _layer_layouts(w_in, w_qup, w_kvup, w_o, w_gate, w_up, w_down):
    zpad = lambda n: jnp.zeros((D_MODEL, n), w_in.dtype)
    w_in = jnp.concatenate([w_in[:, :C_KR], zpad(KR_LANE), w_in[:, C_KR:IN_W],
                            zpad(LANES - KR_LANE - MLA_ROPE_DIM)], axis=-1)
    w_qup = w_qup.reshape(MLA_Q_RANK, MLA_HEADS, MLA_QK_DIM)
    w_qup = jnp.pad(w_qup, ((0, 0), (0, 0), (0, LANES - MLA_QK_DIM))).reshape(MLA_Q_RANK, SLOT_W)
    return w_in, w_qup, w_kvup, w_o, w_gate, w_up, w_down


def _local_step(x2, target, meta_full, layer_weights, p):
    T = BLOCK + x2.shape[0]
    L = DEPTH
    attn_norm, q_norm, kv_norm, sinks = p["attn_norm"], p["q_norm"], p["kv_norm"], p["sinks"]
    out_norm_swa, out_norm_mla, ffn_norm, final_norm = (
        p["out_norm_swa"], p["out_norm_mla"], p["ffn_norm"], p["final_norm"])

    cosa, sina, cosm, sinm = _rope_tables(T)
    no_sink = jnp.full((MLA_HEADS,), NEG, F32)
    scale_a, scale_b = SWA_HEAD_DIM ** -0.5, MLA_QK_DIM ** -0.5
    row = lambda v: v.reshape(1, -1)

    h = jnp.concatenate([jnp.zeros((FRONT, D_MODEL), F32), meta_full, x2], axis=0)
    saved = []
    weights = []
    for l in range(L):
        weights.append(_layer_layouts(*layer_weights(l, h)))
        W_in, W_qup, W_kvup, W_o, W_gate, W_up, W_down = weights[l]
        u, proj = _norm_proj("in_proj", h, row(attn_norm[l]), [W_in], False)
        qa, ka, va, qn, cn, kr = _prep1(proj, row(q_norm[l]), row(kv_norm[l]), cosa, sina, cosm, sinm)
        qb = _mm("q_up", qn, W_qup, "nn")
        kvb = _mm("kv_up", cn, W_kvup, "nn")
        qs, ks, vs = _prep2(qb, kvb, kr, cosm, sinm)
        oa, lse_a = _attn_fwd("swa_fwd", qa, ka, va, sinks[l], 4, False, scale_a, 4)
        ob, lse_b = _attn_fwd("mla_fwd", qs, ks, vs, no_sink, 1, True, scale_b, 2)
        mix = _merge_fwd(oa, ob, row(out_norm_swa[l]), row(out_norm_mla[l]))
        h1 = _mm("o_proj", mix, W_o, "nn", res=h)
        u2, a, b, hm = _norm_proj("ffn_in", h1, row(ffn_norm[l]), [W_gate, W_up], True)
        h2 = _mm("down_proj", hm, W_down, "nn", res=h1)
        saved.append((h, u, proj, qa, ka, va, qn, cn, qs, ks, vs, oa, lse_a, ob, lse_b, mix, h1, u2, a, b, hm))
        h = h2

    dh, d_final, loss_row = _loss_head(h, target, row(final_norm))

    gw = {k: [None] * L for k in ("in", "qup", "kvup", "o", "gate", "up", "down")}
    gs = {k: [None] * L for k in ("attn", "qn", "kvn", "sink", "ga", "gb", "ffn")}
    for l in reversed(range(L)):
        (h0, u, proj, qa, ka, va, qn, cn, qs, ks, vs, oa, lse_a, ob, lse_b, mix, h1, u2, a, b, hm) = saved[l]
        W_in, W_qup, W_kvup, W_o, W_gate, W_up, W_down = weights[l]
        gw["down"][l] = _mm("down_dw", hm, dh, "tn")
        da, db = _ffn_mid_bwd(dh, W_down, a, b)
        gw["gate"][l] = _mm("gate_dw", u2, da, "tn")
        gw["up"][l] = _mm("up_dw", u2, db, "tn")
        du2 = _mm("gate_dx", da, W_gate, "nt")
        du2 = _mm("up_dx", db, W_up, "nt", res=du2)
        dh1, gs["ffn"][l] = _rmsnorm_bwd("ffn_norm_bwd", h1, row(ffn_norm[l]), du2, dh)
        gw["o"][l] = _mm("o_dw", mix, dh1, "tn")
        dmix = _mm("o_dx", dh1, W_o, "nt")
        sink_slots = jnp.repeat(sinks[l], LANES).reshape(1, SLOT_W)
        doa, dla, dob, dlb, gs["ga"][l], gs["gb"][l], dsink = _merge_bwd(
            dmix, oa, ob, lse_a, row(out_norm_swa[l]), row(out_norm_mla[l]), sink_slots)
        gs["sink"][l] = dsink.reshape(SWA_HEADS, LANES)[:, 0]
        dqs, dks, dvs = _attn_bwd("mla_bwd", qs, ks, vs, dob, lse_b, dlb, 1, True, scale_b)
        dqa, dka, dva = _attn_bwd("swa_bwd", qa, ka, va, doa, lse_a, dla, 4, False, scale_a)
        dqb, dkvb, dkr = _prep2_bwd(dqs, dks, dvs, cosm, sinm)
        gw["qup"][l] = _mm("q_up_dw", qn, dqb, "tn")
        gw["kvup"][l] = _mm("kv_up_dw", cn, dkvb, "tn")
        dqn = _mm("q_up_dx", dqb, W_qup, "nt")
        dcn = _mm("kv_up_dx", dkvb, W_kvup, "nt")
        dproj, gs["qn"][l], gs["kvn"][l] = _prep1_bwd(
            proj, dqa, dka, dva, dqn, dcn, dkr, cosa, sina, row(q_norm[l]), row(kv_norm[l]))
        gw["in"][l] = _mm("in_dw", u, dproj, "tn")
        du = _mm("in_dx", dproj, W_in, "nt")
        dh, gs["attn"][l] = _rmsnorm_bwd("attn_norm_bwd", h0, row(attn_norm[l]), du, dh1)

    st = lambda k: jnp.stack(gw[k])
    d_in = st("in")
    d_in = jnp.concatenate([d_in[..., :C_KR], d_in[..., C_KR + KR_LANE:C_KR + KR_LANE + MLA_ROPE_DIM]], axis=-1)
    d_qup = st("qup").reshape(L, MLA_Q_RANK, MLA_HEADS, LANES)[..., :MLA_QK_DIM].reshape(L, MLA_Q_RANK, -1)
    d_nat = [d_in, d_qup, st("kvup"), st("o"), st("gate"), st("up"), st("down")]
    return loss_row, dh, d_nat, gs, d_final


def kernel(x, meta_tokens, attn_norm, w_in, q_norm, w_q_up, kv_norm, w_kv_up, sinks, out_norm_swa, out_norm_mla, w_o, ffn_norm, w_gate, w_up, w_down, final_norm, loss_target, m_meta_tokens, m_attn_norm, m_w_in, m_q_norm, m_w_q_up, m_kv_norm, m_w_kv_up, m_sinks, m_out_norm_swa, m_out_norm_mla, m_w_o, m_ffn_norm, m_w_gate, m_w_up, m_w_down, m_final_norm, v_meta_tokens, v_attn_norm, v_w_in, v_q_norm, v_w_q_up, v_kv_norm, v_w_kv_up, v_sinks, v_out_norm_swa, v_out_norm_mla, v_w_o, v_ffn_norm, v_w_gate, v_w_up, v_w_down, v_final_norm):
    assert x.shape[0] == 1 and x.shape[1] % BLOCK == 0
    big = [w_in, w_q_up, w_kv_up, w_o, w_gate, w_up, w_down]

    c_idx = lax.axis_index("c").astype(jnp.int32)
    chip = (2 * lax.axis_index("x") + lax.axis_index("y")).astype(jnp.int32)
    w16 = [w.astype(BF16) for w in big]
    join = [_cols_from_chips, _cols_from_chips, _cols_from_chips, _rows_from_chips, _cols_from_chips,
            _cols_from_chips, _rows_from_chips]

    def whole(own, landed):
        return [f([jnp.where(chip == j, o, g[j]) for j in range(4)]) for f, o, g in zip(join, own, landed)]

    first = _gather_weights([w[0] for w in w16], meta_tokens)
    meta_full = jnp.concatenate([jnp.where(chip == j, meta_tokens, first[-1][j]) for j in range(4)], axis=-1)
    send, recv, token, thru, lands = _gather_start([w[1:] for w in w16])
    state = {"thru": thru}

    def layer_weights(l, h):
        if l == 0:
            return whole([w[0] for w in w16], first[:-1])
        state["thru"], landed = _gather_wait("gather_wait%d" % l, send, recv, l - 1, state["thru"], lands[l - 1], h)
        return whole([w[l] for w in w16], landed)

    small_p = dict(attn_norm=attn_norm, q_norm=q_norm, kv_norm=kv_norm, sinks=sinks, out_norm_swa=out_norm_swa,
                   out_norm_mla=out_norm_mla, ffn_norm=ffn_norm, final_norm=final_norm)
    loss_row, dh, d_nat, gs, d_final = _local_step(x[0] + token[0, 0], loss_target[0], meta_full, layer_weights, small_p)
    grad_x = dh[BLOCK:][None]

    split = [_cols_to_chips, _cols_to_chips, _cols_to_chips, _rows_to_chips, _cols_to_chips, _cols_to_chips,
             _rows_to_chips]
    full = [f(d) for f, d in zip(split, d_nat)]

    got = _pair_exchange([f.astype(BF16) for f in full])
    sums = [_pair_add(f, g, c_idx) for f, g in zip(full, got)]
    landed = _chip_scatter([s16 for _, s16 in sums])
    g_big = _pair_join([_chip_add(t, s32, chip, c_idx) for t, (s32, _) in zip(landed, sums)])

    small = [jnp.stack(gs["attn"]).reshape(-1), jnp.stack(gs["qn"]).reshape(-1), jnp.stack(gs["kvn"]).reshape(-1),
             jnp.stack(gs["sink"]).reshape(-1), jnp.stack(gs["ga"]).reshape(-1), jnp.stack(gs["gb"]).reshape(-1),
             jnp.stack(gs["ffn"]).reshape(-1), d_final.reshape(-1)]
    sizes = [s.shape[0] for s in small]
    flat = jnp.concatenate(small + [dh[FRONT:BLOCK].reshape(-1), loss_row[0, :1]])
    n_flat = flat.shape[0]
    rows_needed = -(-n_flat // (8 * LANES)) * 8
    flat = jnp.pad(flat, (0, rows_needed * LANES - n_flat)).reshape(rows_needed, LANES)
    tot = _allreduce_small(flat).reshape(-1)
    n_small = sum(sizes)
    loss = tot[n_small + N_META * D_MODEL]
    g_meta_full = tot[n_small:n_small + N_META * D_MODEL].reshape(N_META, D_MODEL)
    g_meta_mine = lax.dynamic_slice_in_dim(g_meta_full, chip * (D_MODEL // 4), D_MODEL // 4, axis=1)

    small_w = [attn_norm, q_norm, kv_norm, sinks, out_norm_swa, out_norm_mla, ffn_norm, final_norm]
    small_m = [m_attn_norm, m_q_norm, m_kv_norm, m_sinks, m_out_norm_swa, m_out_norm_mla, m_ffn_norm, m_final_norm]
    small_v = [v_attn_norm, v_q_norm, v_kv_norm, v_sinks, v_out_norm_swa, v_out_norm_mla, v_ffn_norm, v_final_norm]
    n_rows = -(-n_small // (8 * LANES)) * 8

    def pack(arrs):
        f = jnp.concatenate([a.reshape(-1) for a in arrs])
        return jnp.pad(f, (0, n_rows * LANES - n_small), constant_values=1.0).reshape(n_rows, LANES)

    g_small_pack = jnp.pad(tot[:n_small], (0, n_rows * LANES - n_small)).reshape(n_rows, LANES)
    upd_small = _adamw("adam_small", pack(small_w), g_small_pack, pack(small_m), pack(small_v))

    def unpack(p):
        f = p.reshape(-1)
        out, off = [], 0
        for a, n in zip(small_w, sizes):
            out.append(f[off:off + n].reshape(a.shape))
            off += n
        return out

    g_small = unpack(g_small_pack)
    d_small, m_small, v_small = [unpack(p) for p in upd_small]
    d_meta, nm_meta, nv_meta = _adamw("adam_meta", meta_tokens, g_meta_mine, m_meta_tokens, v_meta_tokens)

    big_m = [m_w_in, m_w_q_up, m_w_kv_up, m_w_o, m_w_gate, m_w_up, m_w_down]
    big_v = [v_w_in, v_w_q_up, v_w_kv_up, v_w_o, v_w_gate, v_w_up, v_w_down]
    upd_big = [_adamw("adam_big", w, g, m, v) for w, g, m, v in zip(big, g_big, big_m, big_v)]

    names = ["meta_tokens", "attn_norm", "w_in", "q_norm", "w_q_up", "kv_norm", "w_kv_up", "sinks",
             "out_norm_swa", "out_norm_mla", "w_o", "ffn_norm", "w_gate", "w_up", "w_down", "final_norm"]
    small_idx = {"attn_norm": 0, "q_norm": 1, "kv_norm": 2, "sinks": 3, "out_norm_swa": 4,
                 "out_norm_mla": 5, "ffn_norm": 6, "final_norm": 7}
    big_idx = {"w_in": 0, "w_q_up": 1, "w_kv_up": 2, "w_o": 3, "w_gate": 4, "w_up": 5, "w_down": 6}
    grads, deltas, new_m, new_v = [], [], [], []
    for nme in names:
        if nme == "meta_tokens":
            quad = (g_meta_mine, d_meta, nm_meta, nv_meta)
        elif nme in small_idx:
            i = small_idx[nme]
            quad = (g_small[i], d_small[i], m_small[i], v_small[i])
        else:
            i = big_idx[nme]
            quad = (g_big[i], *upd_big[i])
        grads.append(quad[0]); deltas.append(quad[1]); new_m.append(quad[2]); new_v.append(quad[3])
    return (loss, grad_x, *grads, *deltas, *new_m, *new_v)
```

```python
import jax
import jax.numpy as jnp
from jax import lax
from jax.experimental import pallas as pl
from jax.experimental.pallas import tpu as pltpu

F32 = jnp.float32
BF16 = jnp.bfloat16
MXU_DTYPE = BF16

D_MODEL = 1024
DEPTH = 4
N_META = 16
BLOCK = 128
WINDOW = 128
ROPE_THETA = 10000.0
EPS = 1e-6
NEG = -1e30
SWA_HEADS = 8
SWA_KV_HEADS = 2
SWA_HEAD_DIM = 64
MLA_HEADS = 8
MLA_Q_RANK = 256
MLA_KV_RANK = 128
MLA_NOPE_DIM = 64
MLA_ROPE_DIM = 32
MLA_V_DIM = 64
MLA_QK_DIM = MLA_NOPE_DIM + MLA_ROPE_DIM
D_FF = 2816
FRONT = (-N_META) % BLOCK
LANES = 128
SLOT_W = 8 * LANES
C_QA, C_KA, C_VA, C_QL, C_KL, C_KR, IN_WP = 0, 512, 640, 768, 1024, 1152, 1280
KR_LANE = 64
IN_W = 1184

ADAM_LR, ADAM_B1, ADAM_B2, ADAM_EPS, ADAM_WD, ADAM_STEP = 0.001, 0.9, 0.999, 1e-08, 0.01, 10

VMEM_LIMIT = 48 * 1024 * 1024
MESH = pl.DeviceIdType.MESH


def _tile(n, prefs):
    for t in prefs:
        if n % t == 0:
            return t
    return n


def _params(sem):
    return pltpu.CompilerParams(dimension_semantics=sem, vmem_limit_bytes=VMEM_LIMIT)


_DIMS = {"nn": (((1,), (0,)), ((), ())), "nt": (((1,), (1,)), ((), ())), "tn": (((0,), (0,)), ((), ()))}


def _mm(name, a, b, mode, out_dtype=F32, res=None):
    if mode == "nn":
        (M, K), (_, N) = a.shape, b.shape
    elif mode == "nt":
        (M, K), (N, _) = a.shape, b.shape
    else:
        (K, M), (_, N) = a.shape, b.shape
    lane_tiles = (1408, 1024, 640, 768, 512, 384, 256, 128)
    row_tiles = (1056, 528, 512, 384, 256, 128)
    bm = _tile(M, lane_tiles if mode == "tn" else row_tiles)
    bn = _tile(N, lane_tiles)
    bk = _tile(K, row_tiles if mode == "tn" else lane_tiles)
    nk = K // bk
    if mode == "tn":
        a_spec = pl.BlockSpec((bk, bm), lambda i, j, k: (k, i))
    else:
        a_spec = pl.BlockSpec((bm, bk), lambda i, j, k: (i, k))
    if mode == "nt":
        b_spec = pl.BlockSpec((bn, bk), lambda i, j, k: (j, k))
    else:
        b_spec = pl.BlockSpec((bk, bn), lambda i, j, k: (k, j))
    o_spec = pl.BlockSpec((bm, bn), lambda i, j, k: (i, j))
    in_specs = [a_spec, b_spec]
    args = [a, b]
    if res is not None:
        in_specs.append(o_spec)
        args.append(res)
    dims = _DIMS[mode]

    def kern(a_ref, b_ref, *rest):
        if res is not None:
            r_ref, o_ref = rest[0], rest[1]
            scr = rest[2:]
        else:
            r_ref, o_ref = None, rest[0]
            scr = rest[1:]
        p = lax.dot_general(a_ref[...].astype(MXU_DTYPE), b_ref[...].astype(MXU_DTYPE), dims,
                            preferred_element_type=F32)

        def finish(val):
            if r_ref is not None:
                val = val + r_ref[...]
            o_ref[...] = val.astype(o_ref.dtype)

        if nk == 1:
            finish(p)
        else:
            acc = scr[0]
            k = pl.program_id(2)

            @pl.when(k == 0)
            def _():
                acc[...] = p

            @pl.when(k > 0)
            def _():
                acc[...] += p

            @pl.when(k == nk - 1)
            def _():
                finish(acc[...])

    return pl.pallas_call(
        kern, name=name,
        out_shape=jax.ShapeDtypeStruct((M, N), out_dtype),
        grid=(M // bm, N // bn, nk),
        in_specs=in_specs, out_specs=o_spec,
        scratch_shapes=[pltpu.VMEM((bm, bn), F32)] if nk > 1 else [],
        compiler_params=_params(("parallel", "parallel", "arbitrary")),
    )(*args)


def _rowmap(name, body, rows, vecs, outs, accs=(), tr_prefs=(384, 256, 128)):
    R = rows[0].shape[0]
    tr = _tile(R, tr_prefs)
    n_r, n_v, n_o, n_a = len(rows), len(vecs), len(outs), len(accs)

    def kern(*refs):
        ins = [r[...] for r in refs[:n_r + n_v]]
        o_refs = refs[n_r + n_v:n_r + n_v + n_o]
        a_refs = refs[n_r + n_v + n_o:]
        res = body(*ins)
        for o_ref, val in zip(o_refs, res[:n_o]):
            o_ref[...] = val.astype(o_ref.dtype)
        if n_a:
            first = pl.program_id(0) == 0

            @pl.when(first)
            def _():
                for a_ref, val in zip(a_refs, res[n_o:]):
                    a_ref[...] = val

            @pl.when(jnp.logical_not(first))
            def _():
                for a_ref, val in zip(a_refs, res[n_o:]):
                    a_ref[...] += val

    in_specs = [pl.BlockSpec((tr, r.shape[1]), lambda i: (i, 0)) for r in rows]
    in_specs += [pl.BlockSpec((1, v.shape[1]), lambda i: (0, 0)) for v in vecs]
    out_specs = [pl.BlockSpec((tr, c), lambda i: (i, 0)) for c, _ in outs]
    out_specs += [pl.BlockSpec((1, c), lambda i: (0, 0)) for c in accs]
    out_shape = [jax.ShapeDtypeStruct((R, c), dt) for c, dt in outs]
    out_shape += [jax.ShapeDtypeStruct((1, c), F32) for c in accs]
    return pl.pallas_call(
        kern, name=name, out_shape=out_shape, grid=(R // tr,),
        in_specs=in_specs, out_specs=out_specs,
        compiler_params=_params(("arbitrary",) if n_a else ("parallel",)),
    )(*rows, *vecs)


def _lane(shape):
    return lax.broadcasted_iota(jnp.int32, shape, 1)


def _rot_swa(x):
    lane = _lane(x.shape)
    return jnp.where((lane & 63) < 32, -pltpu.roll(x, 96, 1), pltpu.roll(x, 32, 1))


def _rot_mla(x):
    lane = _lane(x.shape)
    lo = jnp.where(lane >= KR_LANE, -pltpu.roll(x, 112, 1), 0.0)
    hi = jnp.where(lane < KR_LANE + MLA_ROPE_DIM, pltpu.roll(x, 16, 1), 0.0)
    return jnp.where(lane < KR_LANE + 16, lo, hi)


def _rope(x, cos, sin, rot):
    return x * cos + rot(x) * sin


def _rope_t(g, cos, sin, rot):
    return g * cos - rot(g * sin)


def _low(x):
    return jnp.where(_lane(x.shape) < 64, x, 0.0)


def _value_slot(x):
    lane = _lane(x.shape)
    return jnp.where(lane < 64, x, jnp.where(lane == 64, 1.0, 0.0))


def _blk(x, j):
    return x[:, j * LANES:(j + 1) * LANES]


def _rms_r(x, width):
    return lax.rsqrt(jnp.sum(x * x, axis=-1, keepdims=True) * (1.0 / width) + EPS)


def _rms_bwd(x, g, dy, width):
    r = _rms_r(x, width)
    gdy = dy * g
    dot = jnp.sum(gdy * x, axis=-1, keepdims=True)
    dx = r * gdy - x * (r * r * r * (1.0 / width) * dot)
    return dx, dy * x * r


def _colsum(x):
    return jnp.sum(x, axis=0, keepdims=True)


def _rmsnorm_bwd(name, x, g, dy, dres):
    def body(xv, dyv, dr, gv):
        dx, dg = _rms_bwd(xv, gv, dyv, D_MODEL)
        return dx + dr, _colsum(dg)
    return _rowmap(name, body, [x, dy, dres], [g], [(D_MODEL, F32)], [D_MODEL])


def _prep1(proj, qn_g, kv_g, cosa, sina, cosm, sinm):
    def body(p, ca, sa, cm, sm, gq, gk):
        qa = []
        for j in range(4):
            xr = _rope(_blk(p, j), ca, sa, _rot_swa)
            qa += [_low(xr), _low(pltpu.roll(xr, 64, 1))]
        kr_ = _rope(_blk(p, C_KA // LANES), ca, sa, _rot_swa)
        ka = [_low(kr_), _low(pltpu.roll(kr_, 64, 1))]
        vv = _blk(p, C_VA // LANES)
        va = [_value_slot(vv), _value_slot(pltpu.roll(vv, 64, 1))]
        ql = p[:, C_QL:C_QL + MLA_Q_RANK]
        qn = ql * _rms_r(ql, MLA_Q_RANK) * gq
        kl = p[:, C_KL:C_KL + MLA_KV_RANK]
        cn = kl * _rms_r(kl, MLA_KV_RANK) * gk
        kr = _rope(_blk(p, C_KR // LANES), cm, sm, _rot_mla)
        return (jnp.concatenate(qa, 1), jnp.concatenate(ka, 1), jnp.concatenate(va, 1), qn, cn, kr)
    return _rowmap("prep1", body, [proj, cosa, sina, cosm, sinm], [qn_g, kv_g],
                   [(SLOT_W, BF16), (2 * LANES, BF16), (2 * LANES, BF16),
                    (MLA_Q_RANK, BF16), (MLA_KV_RANK, BF16), (LANES, F32)])


def _prep2(qb, kvb, kr, cosm, sinm):
    def body(q, kv, krv, cm, sm):
        qs, ks, vs = [], [], []
        for h in range(MLA_HEADS):
            qs.append(_rope(_blk(q, h), cm, sm, _rot_mla))
            kvh = _blk(kv, h)
            ks.append(_low(kvh) + krv)
            vs.append(_value_slot(pltpu.roll(kvh, 64, 1)))
        return jnp.concatenate(qs, 1), jnp.concatenate(ks, 1), jnp.concatenate(vs, 1)
    return _rowmap("prep2", body, [qb, kvb, kr, cosm, sinm], [],
                   [(SLOT_W, BF16), (SLOT_W, BF16), (SLOT_W, BF16)])


def _compact(slots):
    return jnp.concatenate(
        [_blk(slots, 2 * j) + pltpu.roll(_blk(slots, 2 * j + 1), 64, 1) for j in range(4)], 1)


def _expand(nat):
    out = []
    for j in range(4):
        b = _blk(nat, j)
        out += [_low(b), _low(pltpu.roll(b, 64, 1))]
    return jnp.concatenate(out, 1)


def _merge_fwd(oa, ob, ga, gb):
    def body(a, b, gav, gbv):
        xa, xb = _compact(a), _compact(b)
        return (jnp.concatenate([xa * _rms_r(xa, 512) * gav, xb * _rms_r(xb, 512) * gbv], 1),)
    return _rowmap("merge_fwd", body, [oa, ob], [ga, gb], [(D_MODEL, BF16)])[0]


def _merge_bwd(dmix, oa, ob, lse_a, ga, gb, sink_slots):
    def body(dm, a, b, lse, gav, gbv, sk):
        outs = []
        accs = []
        for o, g, lo in ((a, gav, 0), (b, gbv, 512)):
            x = _compact(o)
            dx, dg = _rms_bwd(x, g, dm[:, lo:lo + 512], 512)
            do = _expand(dx)
            delta = jnp.concatenate(
                [jnp.broadcast_to(jnp.sum(_blk(do, h) * _blk(o, h), axis=-1, keepdims=True),
                                  (do.shape[0], LANES)) for h in range(8)], 1)
            outs += [do, delta]
            accs.append(_colsum(dg))
        dsink = _colsum(-jnp.exp(sk - lse) * outs[1])
        return (*outs, *accs, dsink)
    return _rowmap("merge_bwd", body, [dmix, oa, ob, lse_a], [ga, gb, sink_slots],
                   [(SLOT_W, BF16), (SLOT_W, F32), (SLOT_W, BF16), (SLOT_W, F32)],
                   [512, 512, SLOT_W])


def _prep2_bwd(dq, dk, dv, cosm, sinm):
    def body(dqv, dkv, dvv, cm, sm):
        dqv, dkv, dvv = dqv.astype(F32), dkv.astype(F32), dvv.astype(F32)
        dqb, dkvb = [], []
        krsum = jnp.zeros((dqv.shape[0], LANES), F32)
        for h in range(MLA_HEADS):
            dqb.append(_rope_t(_blk(dqv, h), cm, sm, _rot_mla))
            dkh = _blk(dkv, h)
            dkvb.append(_low(dkh) + pltpu.roll(_blk(dvv, h), 64, 1))
            krsum = krsum + dkh
        lane = _lane(krsum.shape)
        dkr = jnp.where((lane >= KR_LANE) & (lane < KR_LANE + MLA_ROPE_DIM),
                        _rope_t(krsum, cm, sm, _rot_mla), 0.0)
        return jnp.concatenate(dqb, 1), jnp.concatenate(dkvb, 1), dkr
    return _rowmap("prep2_bwd", body, [dq, dk, dv, cosm, sinm], [],
                   [(SLOT_W, BF16), (SLOT_W, BF16), (LANES, F32)])


def _prep1_bwd(proj, dqa, dka, dva, dqn, dcn, dkr, cosa, sina, qn_g, kv_g):
    def body(p, dq, dk, dv, dqnv, dcnv, dkrv, ca, sa, gq, gk):
        dq, dk, dv = dq.astype(F32), dk.astype(F32), dv.astype(F32)
        cols = []
        for j in range(4):
            nat = _blk(dq, 2 * j) + pltpu.roll(_blk(dq, 2 * j + 1), 64, 1)
            cols.append(_rope_t(nat, ca, sa, _rot_swa))
        grp = lambda d, g: sum(_blk(d, 4 * g + i) for i in range(4))
        cols.append(_rope_t(grp(dk, 0) + pltpu.roll(grp(dk, 1), 64, 1), ca, sa, _rot_swa))
        cols.append(grp(dv, 0) + pltpu.roll(grp(dv, 1), 64, 1))
        dql, dgq = _rms_bwd(p[:, C_QL:C_QL + MLA_Q_RANK], gq, dqnv, MLA_Q_RANK)
        dkl, dgk = _rms_bwd(p[:, C_KL:C_KL + MLA_KV_RANK], gk, dcnv, MLA_KV_RANK)
        cols += [dql, dkl, dkrv]
        return jnp.concatenate(cols, 1), _colsum(dgq), _colsum(dgk)
    return _rowmap("prep1_bwd", body, [proj, dqa, dka, dva, dqn, dcn, dkr, cosa, sina], [qn_g, kv_g],
                   [(IN_WP, BF16)], [MLA_Q_RANK, MLA_KV_RANK], tr_prefs=(192, 128))


def _sigmoid(x):
    return 1.0 / (1.0 + jnp.exp(-x))


def _ffn_tiles(T, F):
    return _tile(T, (528, 512, 384, 256, 128)), _tile(F, (1408, 1024, 640, 512, 256, 128))


def _norm_proj(name, h, g, weights, swiglu):
    (T, D), F = h.shape, weights[0].shape[1]
    bm, bn = _ffn_tiles(T, F)
    nw = len(weights)

    def kern(h_ref, g_ref, *rest):
        w_refs, u_ref, o_refs, u_scr = rest[:nw], rest[nw], rest[nw + 1:-1], rest[-1]

        @pl.when(pl.program_id(1) == 0)
        def _():
            x = h_ref[...]
            u = (x * _rms_r(x, D) * g_ref[...]).astype(u_scr.dtype)
            u_scr[...] = u
            u_ref[...] = u

        uv = u_scr[...]
        prods = [jnp.dot(uv, w[...], preferred_element_type=F32) for w in w_refs]
        for o_ref, p in zip(o_refs, prods):
            o_ref[...] = p
        if swiglu:
            a, b = prods
            o_refs[nw][...] = (a * _sigmoid(a) * b).astype(o_refs[nw].dtype)

    w_spec = pl.BlockSpec((D, bn), lambda i, j: (0, j))
    row_spec = pl.BlockSpec((bm, D), lambda i, j: (i, 0))
    o_spec = pl.BlockSpec((bm, bn), lambda i, j: (i, j))
    n_out = nw + (1 if swiglu else 0)
    return pl.pallas_call(
        kern, name=name,
        out_shape=[jax.ShapeDtypeStruct((T, D), MXU_DTYPE)] + [jax.ShapeDtypeStruct((T, F), F32)] * nw
        + ([jax.ShapeDtypeStruct((T, F), MXU_DTYPE)] if swiglu else []),
        grid=(T // bm, F // bn),
        in_specs=[row_spec, pl.BlockSpec((1, D), lambda i, j: (0, 0))] + [w_spec] * nw,
        out_specs=[row_spec] + [o_spec] * n_out,
        scratch_shapes=[pltpu.VMEM((bm, D), MXU_DTYPE)],
        compiler_params=_params(("parallel", "arbitrary")),
    )(h, g, *weights)


def _ffn_mid_bwd(dh, w_down, a, b):
    (T, D), F = dh.shape, w_down.shape[0]
    bm, bn = _ffn_tiles(T, F)

    def kern(dh_ref, wd_ref, a_ref, b_ref, da_ref, db_ref):
        d = lax.dot_general(dh_ref[...].astype(MXU_DTYPE), wd_ref[...], _DIMS["nt"], preferred_element_type=F32)
        av, bv = a_ref[...], b_ref[...]
        s = _sigmoid(av)
        da_ref[...] = (d * bv * (s * (1.0 + av * (1.0 - s)))).astype(da_ref.dtype)
        db_ref[...] = (d * (av * s)).astype(db_ref.dtype)

    o_spec = pl.BlockSpec((bm, bn), lambda i, j: (i, j))
    return pl.pallas_call(
        kern, name="ffn_mid_bwd",
        out_shape=[jax.ShapeDtypeStruct((T, F), MXU_DTYPE)] * 2,
        grid=(T // bm, F // bn),
        in_specs=[pl.BlockSpec((bm, D), lambda i, j: (i, 0)), pl.BlockSpec((bn, D), lambda i, j: (j, 0)),
                  o_spec, o_spec],
        out_specs=[o_spec, o_spec],
        compiler_params=_params(("parallel", "parallel")),
    )(dh, w_down, a, b)


def _loss_head(h, target, g):
    T = h.shape[0]
    nb = T // BLOCK

    def kern(h_ref, t_ref, g_ref, dh_ref, dg_ref, loss_ref, acc):
        i = pl.program_id(0)

        @pl.when(i == 0)
        def _():
            dh_ref[...] = jnp.zeros_like(dh_ref)
            dg_ref[...] = jnp.zeros_like(dg_ref)
            acc[...] = jnp.zeros_like(acc)

        @pl.when(i > 0)
        def _():
            x = h_ref[...]
            gv = g_ref[...]
            e = x * _rms_r(x, D_MODEL) * gv - t_ref[...]
            acc[...] += _colsum(e * e)
            dx, dg = _rms_bwd(x, gv, e * (1.0 / D_MODEL), D_MODEL)
            dh_ref[...] = dx
            dg_ref[...] += _colsum(dg)

        @pl.when(i == nb - 1)
        def _():
            tot = jnp.sum(acc[...], axis=-1, keepdims=True) * (0.5 / D_MODEL)
            loss_ref[...] = jnp.broadcast_to(tot, loss_ref.shape)

    return pl.pallas_call(
        kern, name="loss_head",
        out_shape=[jax.ShapeDtypeStruct((T, D_MODEL), F32), jax.ShapeDtypeStruct((1, D_MODEL), F32),
                   jax.ShapeDtypeStruct((1, LANES), F32)],
        grid=(nb,),
        in_specs=[pl.BlockSpec((BLOCK, D_MODEL), lambda i: (i, 0)),
                  pl.BlockSpec((BLOCK, D_MODEL), lambda i: (jnp.maximum(i - 1, 0), 0)),
                  pl.BlockSpec((1, D_MODEL), lambda i: (0, 0))],
        out_specs=[pl.BlockSpec((BLOCK, D_MODEL), lambda i: (i, 0)),
                   pl.BlockSpec((1, D_MODEL), lambda i: (0, 0)),
                   pl.BlockSpec((1, LANES), lambda i: (0, 0))],
        scratch_shapes=[pltpu.VMEM((1, D_MODEL), F32)],
        compiler_params=_params(("arbitrary",)),
    )(h, target, g)


LOG2E = 1.4426950408889634


def _attn_plan(T, causal):
    tq = _tile(T, (384, 256, 128))
    ck = min(2 * tq, T) if causal else min(tq + WINDOW, T)
    return tq, ck, (-(-T // ck) if causal else 1)


def _chunk(i, c, T, tq, ck, causal):
    if causal:
        return pl.multiple_of(jnp.minimum(c * ck, T - ck), LANES), c * ck
    return pl.multiple_of(jnp.clip(i * tq - WINDOW, 0, T - ck), LANES), 0


def _n_chunks(i, tq, ck, causal):
    return ((i + 1) * tq + ck - 1) // ck if causal else 1


def _mask(s, i, start, first, tq, ck, causal):
    qpos = i * tq + lax.broadcasted_iota(jnp.int32, (tq, 1), 0)
    kpos = start + lax.broadcasted_iota(jnp.int32, (tq, ck), 1)
    low = jnp.maximum(jnp.where(qpos < FRONT, 0, FRONT), first)
    if not causal:
        low = jnp.maximum(low, qpos - (WINDOW - 1))
    return jnp.where(kpos >= low, jnp.where(kpos <= qpos, s, NEG), NEG)


def _chunk_loop(n, body, init, several):
    carry = body(0, init, True)
    if not several:
        return carry
    carry = body(n - 1, carry, True)
    return lax.fori_loop(1, n - 1, lambda c, cr: body(c, cr, False), carry)


def _tile_sweep(nq, tq, ck, causal, q_tile):
    one = min(nq, ck // tq) if causal else nq
    lax.fori_loop(0, one, lambda i, cr: q_tile(i, False) or cr, 0)
    if one < nq:
        lax.fori_loop(one, nq, lambda i, cr: q_tile(i, True) or cr, 0)


BWD_HEADS_PER_STEP = 1


def _head_cols(group, hp):
    q_cols = lambda hh: slice(hh * LANES, (hh + 1) * LANES)
    if group == 1:
        return q_cols, q_cols, hp * LANES
    assert group % hp == 0
    return q_cols, (lambda hh: slice(0, LANES)), LANES


def _whole(T, width, index, single):
    if single:
        return pl.BlockSpec((T, width), index, pipeline_mode=pl.Buffered(1))
    return pl.BlockSpec((T, width), index)


def _attn_fwd(name, q, k, v, sinks, group, causal, scale, hp):
    T = q.shape[0]
    H = q.shape[1] // LANES
    HP = hp
    tq, ck, slots = _attn_plan(T, causal)
    nq = T // tq
    c2 = scale * LOG2E
    q_cols, k_cols, kw = _head_cols(group, HP)

    def kern(sink_ref, q_ref, k_ref, v_ref, o_ref, lse_ref, s_scr):
        sink2 = [sink_ref[pl.program_id(0) * HP + hh] * LOG2E for hh in range(HP)]

        def q_tile(i, several):
            rows = pl.ds(pl.multiple_of(i * tq, tq), tq)
            qq = [q_ref[rows, q_cols(hh)] for hh in range(HP)]
            n = _n_chunks(i, tq, ck, causal)

            def score(c, m2, masked):
                start, first = _chunk(i, c, T, tq, ck, causal)
                out = []
                for hh in range(HP):
                    s = lax.dot_general(qq[hh], k_ref[pl.ds(start, ck), k_cols(hh)], _DIMS["nt"],
                                        preferred_element_type=F32) * c2
                    if masked:
                        s = _mask(s, i, start, first, tq, ck, causal)
                    s_scr[hh, c] = s
                    out.append(jnp.maximum(m2[hh], jnp.max(s, axis=-1, keepdims=True)))
                return tuple(out)

            m2 = _chunk_loop(n, score, tuple(jnp.full((tq, 1), sk, F32) for sk in sink2), several)

            def weigh(c, acc):
                start, _ = _chunk(i, c, T, tq, ck, causal)
                out = []
                for hh in range(HP):
                    p = jnp.exp2(s_scr[hh, c] - m2[hh])
                    out.append(acc[hh] + jnp.dot(p.astype(MXU_DTYPE), v_ref[pl.ds(start, ck), k_cols(hh)],
                                                 preferred_element_type=F32))
                return tuple(out)

            acc = lax.fori_loop(0, n, weigh, tuple(jnp.zeros((tq, LANES), F32) for _ in range(HP)))
            lane = _lane((tq, LANES))
            for hh in range(HP):
                l = acc[hh][:, 64:65] + jnp.exp2(sink2[hh] - m2[hh])
                o_ref[rows, q_cols(hh)] = jnp.where(lane < 64, acc[hh] / l, 0.0)
                lse_ref[rows, q_cols(hh)] = jnp.broadcast_to(m2[hh] * (1.0 / LOG2E) + jnp.log(l), (tq, LANES))

        _tile_sweep(nq, tq, ck, causal, q_tile)

    q_spec = _whole(T, HP * LANES, lambda g: (0, g), True)
    kv_spec = _whole(T, kw, (lambda g: (0, g)) if group == 1 else (lambda g: (0, (g * HP) // group)), True)
    return pl.pallas_call(
        kern, name=name,
        out_shape=[jax.ShapeDtypeStruct((T, H * LANES), F32)] * 2,
        grid=(H // HP,),
        in_specs=[pl.BlockSpec(memory_space=pltpu.SMEM), q_spec, kv_spec, kv_spec],
        out_specs=[q_spec, q_spec],
        scratch_shapes=[pltpu.VMEM((HP, slots, tq, ck), F32)],
        compiler_params=_params(("parallel",)),
    )(sinks, q, k, v)


def _attn_bwd(name, q, k, v, do, lse, delta, group, causal, scale):
    T = q.shape[0]
    H = q.shape[1] // LANES
    HP = BWD_HEADS_PER_STEP
    tq, ck, _ = _attn_plan(T, causal)
    nq = T // tq
    c2 = scale * LOG2E
    q_cols, k_cols, kw = _head_cols(group, HP)

    def kern(q_ref, k_ref, v_ref, do_ref, lse_ref, dl_ref, dq_ref, dk_ref, dv_ref, dk_acc, dv_acc):
        dk_acc[...] = jnp.zeros_like(dk_acc)
        dv_acc[...] = jnp.zeros_like(dv_acc)

        def q_tile(i, several):
            rows = pl.ds(pl.multiple_of(i * tq, tq), tq)
            qq = [q_ref[rows, q_cols(hh)] for hh in range(HP)]
            dd = [do_ref[rows, q_cols(hh)] for hh in range(HP)]
            lse2 = [lse_ref[rows, q_cols(hh)][:, 0:1] * LOG2E for hh in range(HP)]
            dl_c = [dl_ref[rows, q_cols(hh)][:, 0:1] for hh in range(HP)]

            def chunk(c, dq, masked):
                start, first = _chunk(i, c, T, tq, ck, causal)
                keys = pl.ds(start, ck)
                out = []
                for hh in range(HP):
                    kk, vv = k_ref[keys, k_cols(hh)], v_ref[keys, k_cols(hh)]
                    s = lax.dot_general(qq[hh], kk, _DIMS["nt"], preferred_element_type=F32)
                    if masked:
                        s = _mask(s, i, start, first, tq, ck, causal)
                    p = jnp.exp2(s * c2 - lse2[hh])
                    dv_acc[keys, q_cols(hh)] += lax.dot_general(p.astype(MXU_DTYPE), dd[hh], _DIMS["tn"],
                                                                preferred_element_type=F32)
                    dp = lax.dot_general(dd[hh], vv, _DIMS["nt"], preferred_element_type=F32)
                    ds = (p * (dp - dl_c[hh])).astype(MXU_DTYPE)
                    dk_acc[keys, q_cols(hh)] += lax.dot_general(ds, qq[hh], _DIMS["tn"],
                                                                preferred_element_type=F32) * scale
                    out.append(dq[hh] + jnp.dot(ds, kk, preferred_element_type=F32))
                return tuple(out)

            dq = _chunk_loop(_n_chunks(i, tq, ck, causal), chunk,
                             tuple(jnp.zeros((tq, LANES), F32) for _ in range(HP)), several)
            for hh in range(HP):
                dq_ref[rows, q_cols(hh)] = (dq[hh] * scale).astype(dq_ref.dtype)

        _tile_sweep(nq, tq, ck, causal, q_tile)
        dk_ref[...] = dk_acc[...].astype(dk_ref.dtype)
        dv_ref[...] = dv_acc[...].astype(dv_ref.dtype)

    q_spec = _whole(T, HP * LANES, lambda g: (0, g), False)
    kv_spec = _whole(T, kw, (lambda g: (0, g)) if group == 1 else (lambda g: (0, (g * HP) // group)), False)
    return pl.pallas_call(
        kern, name=name,
        out_shape=[jax.ShapeDtypeStruct((T, H * LANES), MXU_DTYPE)] * 3,
        grid=(H // HP,),
        in_specs=[q_spec, kv_spec, kv_spec, q_spec, q_spec, q_spec],
        out_specs=[q_spec, q_spec, q_spec],
        scratch_shapes=[pltpu.VMEM((T, HP * LANES), F32)] * 2,
        compiler_params=_params(("parallel",)),
    )(q, k, v, do, lse, delta)


def _ew(name, fn, ins, out_dtypes):
    shape = ins[0].shape
    flat = [a.reshape(-1, shape[-1]) for a in ins]
    R, C = flat[0].shape
    tr = _tile(R, (512, 256, 128, 64, 32, 16, 8))
    n_in = len(ins)

    def kern(*refs):
        res = fn(*[r[...] for r in refs[:n_in]])
        for o_ref, val in zip(refs[n_in:], res):
            o_ref[...] = val.astype(o_ref.dtype)

    spec = pl.BlockSpec((tr, C), lambda i: (i, 0))
    outs = pl.pallas_call(
        kern, name=name,
        out_shape=[jax.ShapeDtypeStruct((R, C), dt) for dt in out_dtypes],
        grid=(R // tr,), in_specs=[spec] * n_in, out_specs=[spec] * len(out_dtypes),
        compiler_params=_params(("parallel",)),
    )(*flat)
    return [o.reshape(shape) for o in outs]


def _adamw(name, w, g, m, v):
    c1 = 1.0 - ADAM_B1 ** ADAM_STEP
    c2 = 1.0 - ADAM_B2 ** ADAM_STEP

    def fn(wv, gv, mv, vv):
        mn = ADAM_B1 * mv + (1.0 - ADAM_B1) * gv
        vn = ADAM_B2 * vv + (1.0 - ADAM_B2) * (gv * gv)
        delta = -ADAM_LR * ((mn / c1) / (jnp.sqrt(vn / c2) + ADAM_EPS) + ADAM_WD * wv)
        return delta, mn, vn

    return _ew(name, fn, [w, g, m, v], [F32, F32, F32])


_ANY = pl.BlockSpec(memory_space=pl.ANY)


def _where_am_i():
    x, y, c = lax.axis_index("x"), lax.axis_index("y"), lax.axis_index("c")
    chips = [(1 - x, y), (x, 1 - y), (1 - x, 1 - y)]
    return x, y, c, chips


def _gather_weights(shards, meta):
    arrs = list(shards) + [meta]
    n = len(arrs)
    per = [a.shape[0] // 2 for a in arrs]

    def body(*refs):
        ins, outs = refs[:n], refs[n:2 * n]
        send1, recv1, send2, recv2 = refs[2 * n:]
        x, y, c, chips = _where_am_i()
        me = 2 * x + y

        def half(ref, k, cc):
            return ref.at[pl.ds(per[k] * cc, per[k])]

        first = []
        for k in range(n):
            for j, (cx, cy) in enumerate(chips):
                first.append(pltpu.make_async_remote_copy(
                    src_ref=half(ins[k], k, c), dst_ref=half(outs[k].at[me], k, c),
                    send_sem=send1.at[k, j], recv_sem=recv1.at[k, j],
                    device_id=(cx, cy, c), device_id_type=MESH))
        for cp in first:
            cp.start()
        passed = []
        for k in range(n):
            for j, (cx, cy) in enumerate(chips):
                landed = half(outs[k].at[2 * cx + cy], k, c)
                pltpu.make_async_remote_copy(
                    src_ref=landed, dst_ref=landed, send_sem=send1.at[k, j], recv_sem=recv1.at[k, j],
                    device_id=(cx, cy, c), device_id_type=MESH).wait_recv()
                fwd = pltpu.make_async_remote_copy(
                    src_ref=landed, dst_ref=landed, send_sem=send2.at[k, j], recv_sem=recv2.at[k, j],
                    device_id=(x, y, 1 - c), device_id_type=MESH)
                fwd.start()
                passed.append(fwd)
        for k in range(n):
            for j, (cx, cy) in enumerate(chips):
                other = half(outs[k].at[2 * cx + cy], k, 1 - c)
                pltpu.make_async_remote_copy(
                    src_ref=other, dst_ref=other, send_sem=send2.at[k, j], recv_sem=recv2.at[k, j],
                    device_id=(x, y, 1 - c), device_id_type=MESH).wait_recv()
        for cp in first + passed:
            cp.wait_send()

    return pl.pallas_call(
        body, name="gather_weights",
        out_shape=[jax.ShapeDtypeStruct((4,) + a.shape, a.dtype) for a in arrs],
        in_specs=[_ANY] * n, out_specs=[_ANY] * n,
        scratch_shapes=[pltpu.SemaphoreType.DMA((n, 3))] * 4,
    )(*arrs)


_HBM = pl.BlockSpec(memory_space=pltpu.HBM)
_SEM = pl.BlockSpec(memory_space=pltpu.SEMAPHORE)
_FLOWS = pltpu.SideEffectType.DATAFLOW_SIDE_EFFECTING


def _in_hbm(a):
    return pltpu.with_memory_space_constraint(a, pltpu.HBM)


def _gather_start(shards):
    n = len(shards)
    nl = shards[0].shape[0]
    lands = [lax.empty((4,) + sh.shape[1:], sh.dtype) for _ in range(nl) for sh in shards]

    def body(*refs):
        srcs, land = refs[:n], refs[n:n + n * nl]
        send, recv, token = refs[n + n * nl:n + n * nl + 3]
        x, y, c, chips = _where_am_i()
        me = 2 * x + y
        for l in range(nl):
            for k in range(n):
                for cx, cy in chips:
                    pltpu.make_async_remote_copy(
                        src_ref=srcs[k].at[l], dst_ref=land[l * n + k].at[me],
                        send_sem=send.at[l], recv_sem=recv.at[l],
                        device_id=(cx, cy, c), device_id_type=MESH).start()
        token[...] = jnp.zeros_like(token)

    ins = [_in_hbm(a) for a in list(shards) + lands]
    res = pl.pallas_call(
        body, name="gather_start",
        out_shape=(pltpu.SemaphoreType.DMA((nl,)), pltpu.SemaphoreType.DMA((nl,)),
                   jax.ShapeDtypeStruct((8, LANES), F32)) + tuple(pltpu.HBM(a.shape, a.dtype) for a in ins),
        in_specs=[_HBM] * len(ins),
        out_specs=(_SEM, _SEM, pl.BlockSpec(memory_space=pltpu.VMEM)) + (_HBM,) * len(ins),
        input_output_aliases={i: 3 + i for i in range(len(ins))},
        compiler_params=pltpu.CompilerParams(has_side_effects=_FLOWS),
    )(*ins)
    send, recv, token = res[:3]
    thru = res[3:3 + n]
    lands = res[3 + n:]
    return send, recv, token, list(thru), [list(lands[l * n:(l + 1) * n]) for l in range(nl)]


def _gather_wait(name, send, recv, l, thru, land, after):
    n = len(thru)

    def body(*refs):
        srcs, lands = refs[:n], refs[n:2 * n]
        send_sem, recv_sem = refs[2 * n:2 * n + 2]
        x, y, c, chips = _where_am_i()
        for k in range(n):
            for cx, cy in chips:
                copy = pltpu.make_async_remote_copy(
                    src_ref=srcs[k].at[l], dst_ref=lands[k].at[2 * cx + cy],
                    send_sem=send_sem.at[l], recv_sem=recv_sem.at[l],
                    device_id=(cx, cy, c), device_id_type=MESH)
                copy.wait_send()
                copy.wait_recv()

    ins = list(thru) + list(land)
    res = pl.pallas_call(
        body, name=name,
        out_shape=tuple(pltpu.HBM(a.shape, a.dtype) for a in ins),
        in_specs=[_HBM] * len(ins) + [_SEM, _SEM, _ANY],
        out_specs=(_HBM,) * len(ins),
        input_output_aliases={i: i for i in range(len(ins))},
        compiler_params=pltpu.CompilerParams(has_side_effects=_FLOWS),
    )(*ins, send, recv, after)
    return list(res[:n]), list(res[n:])


def _row_chunks(r):
    n = 4 if r % 64 == 0 else 1
    return [(q * (r // n), r // n) for q in range(n)]


def _pair_exchange(grads):
    n = len(grads)

    def body(*refs):
        ins, got = refs[:n], refs[n:2 * n]
        send, recv = refs[2 * n:]
        x, y, c, _ = _where_am_i()
        sib = (x, y, 1 - c)
        for k in range(n):
            for ch in range(4):
                for l in range(2):
                    pltpu.make_async_remote_copy(
                        src_ref=ins[k].at[ch, 2 * (1 - c) + l], dst_ref=got[k].at[ch, l],
                        send_sem=send.at[k], recv_sem=recv.at[k], device_id=sib, device_id_type=MESH).start()
        for k in range(n):
            pltpu.make_async_remote_copy(
                src_ref=got[k], dst_ref=got[k], send_sem=send.at[k], recv_sem=recv.at[k],
                device_id=sib, device_id_type=MESH).wait()

    return pl.pallas_call(
        body, name="reduce_pair",
        out_shape=[jax.ShapeDtypeStruct((4, 2) + g.shape[2:], g.dtype) for g in grads],
        in_specs=[_ANY] * n, out_specs=[_ANY] * n,
        scratch_shapes=[pltpu.SemaphoreType.DMA((n,))] * 2,
    )(*grads)


def _pair_add(full, got, c):
    _, _, r, cols = full.shape
    tr = _tile(r, (512, 256, 352, 128))

    def kern(c_ref, a_ref, b_ref, o32_ref, o16_ref):
        tot = a_ref[...] + b_ref[...].astype(F32)
        o32_ref[...] = tot
        o16_ref[...] = tot.astype(o16_ref.dtype)

    blk = (None, None, tr, cols)
    mine = pl.BlockSpec(blk, lambda ch, l, i, cr: (ch, 2 * cr[0] + l, i, 0))
    same = pl.BlockSpec(blk, lambda ch, l, i, cr: (ch, l, i, 0))
    return pl.pallas_call(
        kern, name="pair_add",
        out_shape=[jax.ShapeDtypeStruct(got.shape, F32), jax.ShapeDtypeStruct(got.shape, got.dtype)],
        grid_spec=pltpu.PrefetchScalarGridSpec(
            num_scalar_prefetch=1, grid=(4, 2, r // tr), in_specs=[mine, same], out_specs=[same, same]),
        compiler_params=_params(("parallel", "parallel", "parallel")),
    )(c.reshape(1), full, got)


def _chip_scatter(parts):
    n = len(parts)

    def body(*refs):
        ins, outs = refs[:n], refs[n:2 * n]
        send, recv = refs[2 * n:]
        x, y, c, chips = _where_am_i()
        me = 2 * x + y
        for k in range(n):
            for j, (cx, cy) in enumerate(chips):
                for l in range(2):
                    pltpu.make_async_remote_copy(
                        src_ref=ins[k].at[2 * cx + cy, l], dst_ref=outs[k].at[me, l],
                        send_sem=send.at[k, j], recv_sem=recv.at[k, j],
                        device_id=(cx, cy, c), device_id_type=MESH).start()
        for k in range(n):
            for j, (cx, cy) in enumerate(chips):
                slot = outs[k].at[2 * cx + cy]
                pltpu.make_async_remote_copy(
                    src_ref=slot, dst_ref=slot, send_sem=send.at[k, j], recv_sem=recv.at[k, j],
                    device_id=(cx, cy, c), device_id_type=MESH).wait()

    return pl.pallas_call(
        body, name="reduce_chips",
        out_shape=[jax.ShapeDtypeStruct(p.shape, p.dtype) for p in parts],
        in_specs=[_ANY] * n, out_specs=[_ANY] * n,
        scratch_shapes=[pltpu.SemaphoreType.DMA((n, 3))] * 2,
    )(*parts)


def _chip_add(landed, mine, me, c):
    _, _, r, cols = landed.shape
    tr = _tile(r, (512, 256, 352, 128))

    def kern(me_ref, c_ref, land_ref, own_ref, o_ref):
        own = own_ref[...]
        tot = None
        for j in range(4):
            term = jnp.where(me_ref[0] == j, own, land_ref[j].astype(F32))
            tot = term if tot is None else tot + term
        o_ref[...] = tot

    return pl.pallas_call(
        kern, name="chip_add",
        out_shape=jax.ShapeDtypeStruct((4, r, cols), F32),
        grid_spec=pltpu.PrefetchScalarGridSpec(
            num_scalar_prefetch=2, grid=(2, r // tr),
            in_specs=[pl.BlockSpec((4, None, tr, cols), lambda l, i, mr, cr: (0, l, i, 0)),
                      pl.BlockSpec((None, None, tr, cols), lambda l, i, mr, cr: (mr[0], l, i, 0))],
            out_specs=pl.BlockSpec((None, tr, cols), lambda l, i, mr, cr: (2 * cr[0] + l, i, 0))),
        compiler_params=_params(("parallel", "parallel")),
    )(me.reshape(1), c.reshape(1), landed, mine)


def _pair_join(sums):
    n = len(sums)

    def body(*refs):
        bufs = refs[n:2 * n]
        send, recv = refs[2 * n:]
        x, y, c, _ = _where_am_i()
        sib = (x, y, 1 - c)
        for k in range(n):
            for l in range(2):
                for r0, rn in _row_chunks(bufs[k].shape[1]):
                    piece = bufs[k].at[2 * c + l, pl.ds(r0, rn)]
                    pltpu.make_async_remote_copy(
                        src_ref=piece, dst_ref=piece, send_sem=send.at[k], recv_sem=recv.at[k],
                        device_id=sib, device_id_type=MESH).start()
        for k in range(n):
            theirs = bufs[k].at[pl.ds(2 * (1 - c), 2)]
            pltpu.make_async_remote_copy(
                src_ref=theirs, dst_ref=theirs, send_sem=send.at[k], recv_sem=recv.at[k],
                device_id=sib, device_id_type=MESH).wait()

    return pl.pallas_call(
        body, name="reduce_join",
        out_shape=[jax.ShapeDtypeStruct(s.shape, s.dtype) for s in sums],
        in_specs=[_ANY] * n, out_specs=[_ANY] * n,
        input_output_aliases={k: k for k in range(n)},
        scratch_shapes=[pltpu.SemaphoreType.DMA((n,))] * 2,
    )(*sums)


def _allreduce_small(buf):
    R = buf.shape[0]

    def body(in_ref, out_ref, land, send, recv):
        x, y, c, _ = _where_am_i()
        me = 4 * x + 2 * y + c
        land[me] = in_ref[...]
        cps = []
        for k in range(1, 8):
            px, py, pc = x ^ (k >> 2), y ^ ((k >> 1) & 1), c ^ (k & 1)
            cps.append(pltpu.make_async_remote_copy(
                src_ref=in_ref, dst_ref=land.at[me], send_sem=send.at[k - 1], recv_sem=recv.at[k - 1],
                device_id=(px, py, pc), device_id_type=MESH))
        for cp in cps:
            cp.start()
        for k in range(1, 8):
            px, py, pc = x ^ (k >> 2), y ^ ((k >> 1) & 1), c ^ (k & 1)
            slot = land.at[4 * px + 2 * py + pc]
            pltpu.make_async_remote_copy(
                src_ref=slot, dst_ref=slot, send_sem=send.at[k - 1], recv_sem=recv.at[k - 1],
                device_id=(px, py, pc), device_id_type=MESH).wait_recv()
        for cp in cps:
            cp.wait_send()
        tot = land[0]
        for d in range(1, 8):
            tot = tot + land[d]
        out_ref[...] = tot

    vm = pl.BlockSpec(memory_space=pltpu.VMEM)
    return pl.pallas_call(
        body, name="allreduce_small",
        out_shape=jax.ShapeDtypeStruct(buf.shape, F32),
        in_specs=[vm], out_specs=vm,
        scratch_shapes=[pltpu.VMEM((8, R, LANES), F32), pltpu.SemaphoreType.DMA((7,)),
                        pltpu.SemaphoreType.DMA((7,))],
    )(buf)


def _rope_tables(T):
    pos = (jnp.arange(T) - FRONT).astype(F32)
    lane = jnp.arange(LANES)
    inv_a = ROPE_THETA ** (-(2 * ((lane % 64) % 32)).astype(F32) / SWA_HEAD_DIM)
    ang_a = pos[:, None] * inv_a[None, :]
    cosa, sina = jnp.cos(ang_a), jnp.sin(ang_a)
    inv_m = ROPE_THETA ** (-(2 * ((lane - KR_LANE) % 16)).astype(F32) / MLA_ROPE_DIM)
    ang_m = pos[:, None] * inv_m[None, :]
    on = ((lane >= KR_LANE) & (lane < KR_LANE + MLA_ROPE_DIM))[None, :]
    cosm = jnp.where(on, jnp.cos(ang_m), 1.0)
    sinm = jnp.where(on, jnp.sin(ang_m), 0.0)
    return cosa, sina, cosm, sinm


def _cols_from_chips(g):
    return jnp.concatenate(g, axis=-1)


def _rows_from_chips(g):
    return jnp.concatenate(g, axis=-2)


def _cols_to_chips(w):
    L, r, c4 = w.shape
    return jnp.moveaxis(w.reshape(L, r, 4, c4 // 4), 2, 0)


def _rows_to_chips(w):
    L, r4, c = w.shape
    return jnp.moveaxis(w.reshape(L, 4, r4 // 4, c), 1, 0)


def _layer_layouts(w_in, w_qup, w_kvup, w_o, w_gate, w_up, w_down):
    zpad = lambda n: jnp.zeros((D_MODEL, n), w_in.dtype)
    w_in = jnp.concatenate([w_in[:, :C_KR], zpad(KR_LANE), w_in[:, C_KR:IN_W],
                            zpad(LANES - KR_LANE - MLA_ROPE_DIM)], axis=-1)
    w_qup = w_qup.reshape(MLA_Q_RANK, MLA_HEADS, MLA_QK_DIM)
    w_qup = jnp.pad(w_qup, ((0, 0), (0, 0), (0, LANES - MLA_QK_DIM))).reshape(MLA_Q_RANK, SLOT_W)
    return w_in, w_qup, w_kvup, w_o, w_gate, w_up, w_down


def _local_step(x2, target, meta_full, layer_weights, p):
    T = BLOCK + x2.shape[0]
    L = DEPTH
    attn_norm, q_norm, kv_norm, sinks = p["attn_norm"], p["q_norm"], p["kv_norm"], p["sinks"]
    out_norm_swa, out_norm_mla, ffn_norm, final_norm = (
        p["out_norm_swa"], p["out_norm_mla"], p["ffn_norm"], p["final_norm"])

    cosa, sina, cosm, sinm = _rope_tables(T)
    no_sink = jnp.full((MLA_HEADS,), NEG, F32)
    scale_a, scale_b = SWA_HEAD_DIM ** -0.5, MLA_QK_DIM ** -0.5
    row = lambda v: v.reshape(1, -1)

    h = jnp.concatenate([jnp.zeros((FRONT, D_MODEL), F32), meta_full, x2], axis=0)
    saved = []
    weights = []
    for l in range(L):
        weights.append(_layer_layouts(*layer_weights(l, h)))
        W_in, W_qup, W_kvup, W_o, W_gate, W_up, W_down = weights[l]
        u, proj = _norm_proj("in_proj", h, row(attn_norm[l]), [W_in], False)
        qa, ka, va, qn, cn, kr = _prep1(proj, row(q_norm[l]), row(kv_norm[l]), cosa, sina, cosm, sinm)
        qb = _mm("q_up", qn, W_qup, "nn")
        kvb = _mm("kv_up", cn, W_kvup, "nn")
        qs, ks, vs = _prep2(qb, kvb, kr, cosm, sinm)
        oa, lse_a = _attn_fwd("swa_fwd", qa, ka, va, sinks[l], 4, False, scale_a, 4)
        ob, lse_b = _attn_fwd("mla_fwd", qs, ks, vs, no_sink, 1, True, scale_b, 2)
        mix = _merge_fwd(oa, ob, row(out_norm_swa[l]), row(out_norm_mla[l]))
        h1 = _mm("o_proj", mix, W_o, "nn", res=h)
        u2, a, b, hm = _norm_proj("ffn_in", h1, row(ffn_norm[l]), [W_gate, W_up], True)
        h2 = _mm("down_proj", hm, W_down, "nn", res=h1)
        saved.append((h, u, proj, qa, ka, va, qn, cn, qs, ks, vs, oa, lse_a, ob, lse_b, mix, h1, u2, a, b, hm))
        h = h2

    dh, d_final, loss_row = _loss_head(h, target, row(final_norm))

    gw = {k: [None] * L for k in ("in", "qup", "kvup", "o", "gate", "up", "down")}
    gs = {k: [None] * L for k in ("attn", "qn", "kvn", "sink", "ga", "gb", "ffn")}
    for l in reversed(range(L)):
        (h0, u, proj, qa, ka, va, qn, cn, qs, ks, vs, oa, lse_a, ob, lse_b, mix, h1, u2, a, b, hm) = saved[l]
        W_in, W_qup, W_kvup, W_o, W_gate, W_up, W_down = weights[l]
        gw["down"][l] = _mm("down_dw", hm, dh, "tn")
        da, db = _ffn_mid_bwd(dh, W_down, a, b)
        gw["gate"][l] = _mm("gate_dw", u2, da, "tn")
        gw["up"][l] = _mm("up_dw", u2, db, "tn")
        du2 = _mm("gate_dx", da, W_gate, "nt")
        du2 = _mm("up_dx", db, W_up, "nt", res=du2)
        dh1, gs["ffn"][l] = _rmsnorm_bwd("ffn_norm_bwd", h1, row(ffn_norm[l]), du2, dh)
        gw["o"][l] = _mm("o_dw", mix, dh1, "tn")
        dmix = _mm("o_dx", dh1, W_o, "nt")
        sink_slots = jnp.repeat(sinks[l], LANES).reshape(1, SLOT_W)
        doa, dla, dob, dlb, gs["ga"][l], gs["gb"][l], dsink = _merge_bwd(
            dmix, oa, ob, lse_a, row(out_norm_swa[l]), row(out_norm_mla[l]), sink_slots)
        gs["sink"][l] = dsink.reshape(SWA_HEADS, LANES)[:, 0]
        dqs, dks, dvs = _attn_bwd("mla_bwd", qs, ks, vs, dob, lse_b, dlb, 1, True, scale_b)
        dqa, dka, dva = _attn_bwd("swa_bwd", qa, ka, va, doa, lse_a, dla, 4, False, scale_a)
        dqb, dkvb, dkr = _prep2_bwd(dqs, dks, dvs, cosm, sinm)
        gw["qup"][l] = _mm("q_up_dw", qn, dqb, "tn")
        gw["kvup"][l] = _mm("kv_up_dw", cn, dkvb, "tn")
        dqn = _mm("q_up_dx", dqb, W_qup, "nt")
        dcn = _mm("kv_up_dx", dkvb, W_kvup, "nt")
        dproj, gs["qn"][l], gs["kvn"][l] = _prep1_bwd(
            proj, dqa, dka, dva, dqn, dcn, dkr, cosa, sina, row(q_norm[l]), row(kv_norm[l]))
        gw["in"][l] = _mm("in_dw", u, dproj, "tn")
        du = _mm("in_dx", dproj, W_in, "nt")
        dh, gs["attn"][l] = _rmsnorm_bwd("attn_norm_bwd", h0, row(attn_norm[l]), du, dh1)

    st = lambda k: jnp.stack(gw[k])
    d_in = st("in")
    d_in = jnp.concatenate([d_in[..., :C_KR], d_in[..., C_KR + KR_LANE:C_KR + KR_LANE + MLA_ROPE_DIM]], axis=-1)
    d_qup = st("qup").reshape(L, MLA_Q_RANK, MLA_HEADS, LANES)[..., :MLA_QK_DIM].reshape(L, MLA_Q_RANK, -1)
    d_nat = [d_in, d_qup, st("kvup"), st("o"), st("gate"), st("up"), st("down")]
    return loss_row, dh, d_nat, gs, d_final


def kernel(x, meta_tokens, attn_norm, w_in, q_norm, w_q_up, kv_norm, w_kv_up, sinks, out_norm_swa, out_norm_mla, w_o, ffn_norm, w_gate, w_up, w_down, final_norm, loss_target, m_meta_tokens, m_attn_norm, m_w_in, m_q_norm, m_w_q_up, m_kv_norm, m_w_kv_up, m_sinks, m_out_norm_swa, m_out_norm_mla, m_w_o, m_ffn_norm, m_w_gate, m_w_up, m_w_down, m_final_norm, v_meta_tokens, v_attn_norm, v_w_in, v_q_norm, v_w_q_up, v_kv_norm, v_w_kv_up, v_sinks, v_out_norm_swa, v_out_norm_mla, v_w_o, v_ffn_norm, v_w_gate, v_w_up, v_w_down, v_final_norm):
    assert x.shape[0] == 1 and x.shape[1] % BLOCK == 0
    big = [w_in, w_q_up, w_kv_up, w_o, w_gate, w_up, w_down]

    c_idx = lax.axis_index("c").astype(jnp.int32)
    chip = (2 * lax.axis_index("x") + lax.axis_index("y")).astype(jnp.int32)
    w16 = [w.astype(BF16) for w in big]
    join = [_cols_from_chips, _cols_from_chips, _cols_from_chips, _rows_from_chips, _cols_from_chips,
            _cols_from_chips, _rows_from_chips]

    def whole(own, landed):
        return [f([jnp.where(chip == j, o, g[j]) for j in range(4)]) for f, o, g in zip(join, own, landed)]

    first = _gather_weights([w[0] for w in w16], meta_tokens)
    meta_full = jnp.concatenate([jnp.where(chip == j, meta_tokens, first[-1][j]) for j in range(4)], axis=-1)
    send, recv, token, thru, lands = _gather_start([w[1:] for w in w16])
    state = {"thru": thru}

    def layer_weights(l, h):
        if l == 0:
            return whole([w[0] for w in w16], first[:-1])
        state["thru"], landed = _gather_wait("gather_wait%d" % l, send, recv, l - 1, state["thru"], lands[l - 1], h)
        return whole([w[l] for w in w16], landed)

    small_p = dict(attn_norm=attn_norm, q_norm=q_norm, kv_norm=kv_norm, sinks=sinks, out_norm_swa=out_norm_swa,
                   out_norm_mla=out_norm_mla, ffn_norm=ffn_norm, final_norm=final_norm)
    loss_row, dh, d_nat, gs, d_final = _local_step(x[0] + token[0, 0], loss_target[0], meta_full, layer_weights, small_p)
    grad_x = dh[BLOCK:][None]

    split = [_cols_to_chips, _cols_to_chips, _cols_to_chips, _rows_to_chips, _cols_to_chips, _cols_to_chips,
             _rows_to_chips]
    full = [f(d) for f, d in zip(split, d_nat)]

    got = _pair_exchange([f.astype(BF16) for f in full])
    sums = [_pair_add(f, g, c_idx) for f, g in zip(full, got)]
    landed = _chip_scatter([s16 for _, s16 in sums])
    g_big = _pair_join([_chip_add(t, s32, chip, c_idx) for t, (s32, _) in zip(landed, sums)])

    small = [jnp.stack(gs["attn"]).reshape(-1), jnp.stack(gs["qn"]).reshape(-1), jnp.stack(gs["kvn"]).reshape(-1),
             jnp.stack(gs["sink"]).reshape(-1), jnp.stack(gs["ga"]).reshape(-1), jnp.stack(gs["gb"]).reshape(-1),
             jnp.stack(gs["ffn"]).reshape(-1), d_final.reshape(-1)]
    sizes = [s.shape[0] for s in small]
    flat = jnp.concatenate(small + [dh[FRONT:BLOCK].reshape(-1), loss_row[0, :1]])
    n_flat = flat.shape[0]
    rows_needed = -(-n_flat // (8 * LANES)) * 8
    flat = jnp.pad(flat, (0, rows_needed * LANES - n_flat)).reshape(rows_needed, LANES)
    tot = _allreduce_small(flat).reshape(-1)
    n_small = sum(sizes)
    loss = tot[n_small + N_META * D_MODEL]
    g_meta_full = tot[n_small:n_small + N_META * D_MODEL].reshape(N_META, D_MODEL)
    g_meta_mine = lax.dynamic_slice_in_dim(g_meta_full, chip * (D_MODEL // 4), D_MODEL // 4, axis=1)

    small_w = [attn_norm, q_norm, kv_norm, sinks, out_norm_swa, out_norm_mla, ffn_norm, final_norm]
    small_m = [m_attn_norm, m_q_norm, m_kv_norm, m_sinks, m_out_norm_swa, m_out_norm_mla, m_ffn_norm, m_final_norm]
    small_v = [v_attn_norm, v_q_norm, v_kv_norm, v_sinks, v_out_norm_swa, v_out_norm_mla, v_ffn_norm, v_final_norm]
    n_rows = -(-n_small // (8 * LANES)) * 8

    def pack(arrs):
        f = jnp.concatenate([a.reshape(-1) for a in arrs])
        return jnp.pad(f, (0, n_rows * LANES - n_small), constant_values=1.0).reshape(n_rows, LANES)

    g_small_pack = jnp.pad(tot[:n_small], (0, n_rows * LANES - n_small)).reshape(n_rows, LANES)
    upd_small = _adamw("adam_small", pack(small_w), g_small_pack, pack(small_m), pack(small_v))

    def unpack(p):
        f = p.reshape(-1)
        out, off = [], 0
        for a, n in zip(small_w, sizes):
            out.append(f[off:off + n].reshape(a.shape))
            off += n
        return out

    g_small = unpack(g_small_pack)
    d_small, m_small, v_small = [unpack(p) for p in upd_small]
    d_meta, nm_meta, nv_meta = _adamw("adam_meta", meta_tokens, g_meta_mine, m_meta_tokens, v_meta_tokens)

    big_m = [m_w_in, m_w_q_up, m_w_kv_up, m_w_o, m_w_gate, m_w_up, m_w_down]
    big_v = [v_w_in, v_w_q_up, v_w_kv_up, v_w_o, v_w_gate, v_w_up, v_w_down]
    upd_big = [_adamw("adam_big", w, g, m, v) for w, g, m, v in zip(big, g_big, big_m, big_v)]

    names = ["meta_tokens", "attn_norm", "w_in", "q_norm", "w_q_up", "kv_norm", "w_kv_up", "sinks",
             "out_norm_swa", "out_norm_mla", "w_o", "ffn_norm", "w_gate", "w_up", "w_down", "final_norm"]
    small_idx = {"attn_norm": 0, "q_norm": 1, "kv_norm": 2, "sinks": 3, "out_norm_swa": 4,
                 "out_norm_mla": 5, "ffn_norm": 6, "final_norm": 7}
    big_idx = {"w_in": 0, "w_q_up": 1, "w_kv_up": 2, "w_o": 3, "w_gate": 4, "w_up": 5, "w_down": 6}
    grads, deltas, new_m, new_v = [], [], [], []
    for nme in names:
        if nme == "meta_tokens":
            quad = (g_meta_mine, d_meta, nm_meta, nv_meta)
        elif nme in small_idx:
            i = small_idx[nme]
            quad = (g_small[i], d_small[i], m_small[i], v_small[i])
        else:
            i = big_idx[nme]
            quad = (g_big[i], *upd_big[i])
        grads.append(quad[0]); deltas.append(quad[1]); new_m.append(quad[2]); new_v.append(quad[3])
    return (loss, grad_x, *grads, *deltas, *new_m, *new_v)
```

```python
import jax
import jax.numpy as jnp
from jax import lax
from jax.experimental import pallas as pl
from jax.experimental.pallas import tpu as pltpu

F32 = jnp.float32
BF16 = jnp.bfloat16
MXU_DTYPE = BF16

D_MODEL = 1024
DEPTH = 4
N_META = 16
BLOCK = 128
WINDOW = 128
ROPE_THETA = 10000.0
EPS = 1e-6
NEG = -1e30
SWA_HEADS = 8
SWA_KV_HEADS = 2
SWA_HEAD_DIM = 64
MLA_HEADS = 8
MLA_Q_RANK = 256
MLA_KV_RANK = 128
MLA_NOPE_DIM = 64
MLA_ROPE_DIM = 32
MLA_V_DIM = 64
MLA_QK_DIM = MLA_NOPE_DIM + MLA_ROPE_DIM
D_FF = 2816
FRONT = (-N_META) % BLOCK
LANES = 128
SLOT_W = 8 * LANES
C_QA, C_KA, C_VA, C_QL, C_KL, C_KR, IN_WP = 0, 512, 640, 768, 1024, 1152, 1280
KR_LANE = 64
IN_W = 1184

ADAM_LR, ADAM_B1, ADAM_B2, ADAM_EPS, ADAM_WD, ADAM_STEP = 0.001, 0.9, 0.999, 1e-08, 0.01, 10

VMEM_LIMIT = 48 * 1024 * 1024
MESH = pl.DeviceIdType.MESH


def _tile(n, prefs):
    for t in prefs:
        if n % t == 0:
            return t
    return n


def _params(sem):
    return pltpu.CompilerParams(dimension_semantics=sem, vmem_limit_bytes=VMEM_LIMIT)


_DIMS = {"nn": (((1,), (0,)), ((), ())), "nt": (((1,), (1,)), ((), ())), "tn": (((0,), (0,)), ((), ()))}


def _mm(name, a, b, mode, out_dtype=F32, res=None):
    if mode == "nn":
        (M, K), (_, N) = a.shape, b.shape
    elif mode == "nt":
        (M, K), (N, _) = a.shape, b.shape
    else:
        (K, M), (_, N) = a.shape, b.shape
    lane_tiles = (1408, 1024, 640, 768, 512, 384, 256, 128)
    row_tiles = (1056, 528, 512, 384, 256, 128)
    bm = _tile(M, lane_tiles if mode == "tn" else row_tiles)
    bn = _tile(N, lane_tiles)
    bk = _tile(K, row_tiles if mode == "tn" else lane_tiles)
    nk = K // bk
    if mode == "tn":
        a_spec = pl.BlockSpec((bk, bm), lambda i, j, k: (k, i))
    else:
        a_spec = pl.BlockSpec((bm, bk), lambda i, j, k: (i, k))
    if mode == "nt":
        b_spec = pl.BlockSpec((bn, bk), lambda i, j, k: (j, k))
    else:
        b_spec = pl.BlockSpec((bk, bn), lambda i, j, k: (k, j))
    o_spec = pl.BlockSpec((bm, bn), lambda i, j, k: (i, j))
    in_specs = [a_spec, b_spec]
    args = [a, b]
    if res is not None:
        in_specs.append(o_spec)
        args.append(res)
    dims = _DIMS[mode]

    def kern(a_ref, b_ref, *rest):
        if res is not None:
            r_ref, o_ref = rest[0], rest[1]
            scr = rest[2:]
        else:
            r_ref, o_ref = None, rest[0]
            scr = rest[1:]
        p = lax.dot_general(a_ref[...].astype(MXU_DTYPE), b_ref[...].astype(MXU_DTYPE), dims,
                            preferred_element_type=F32)

        def finish(val):
            if r_ref is not None:
                val = val + r_ref[...]
            o_ref[...] = val.astype(o_ref.dtype)

        if nk == 1:
            finish(p)
        else:
            acc = scr[0]
            k = pl.program_id(2)

            @pl.when(k == 0)
            def _():
                acc[...] = p

            @pl.when(k > 0)
            def _():
                acc[...] += p

            @pl.when(k == nk - 1)
            def _():
                finish(acc[...])

    return pl.pallas_call(
        kern, name=name,
        out_shape=jax.ShapeDtypeStruct((M, N), out_dtype),
        grid=(M // bm, N // bn, nk),
        in_specs=in_specs, out_specs=o_spec,
        scratch_shapes=[pltpu.VMEM((bm, bn), F32)] if nk > 1 else [],
        compiler_params=_params(("parallel", "parallel", "arbitrary")),
    )(*args)


def _rowmap(name, body, rows, vecs, outs, accs=(), tr_prefs=(384, 256, 128)):
    R = rows[0].shape[0]
    tr = _tile(R, tr_prefs)
    n_r, n_v, n_o, n_a = len(rows), len(vecs), len(outs), len(accs)

    def kern(*refs):
        ins = [r[...] for r in refs[:n_r + n_v]]
        o_refs = refs[n_r + n_v:n_r + n_v + n_o]
        a_refs = refs[n_r + n_v + n_o:]
        res = body(*ins)
        for o_ref, val in zip(o_refs, res[:n_o]):
            o_ref[...] = val.astype(o_ref.dtype)
        if n_a:
            first = pl.program_id(0) == 0

            @pl.when(first)
            def _():
                for a_ref, val in zip(a_refs, res[n_o:]):
                    a_ref[...] = val

            @pl.when(jnp.logical_not(first))
            def _():
                for a_ref, val in zip(a_refs, res[n_o:]):
                    a_ref[...] += val

    in_specs = [pl.BlockSpec((tr, r.shape[1]), lambda i: (i, 0)) for r in rows]
    in_specs += [pl.BlockSpec((1, v.shape[1]), lambda i: (0, 0)) for v in vecs]
    out_specs = [pl.BlockSpec((tr, c), lambda i: (i, 0)) for c, _ in outs]
    out_specs += [pl.BlockSpec((1, c), lambda i: (0, 0)) for c in accs]
    out_shape = [jax.ShapeDtypeStruct((R, c), dt) for c, dt in outs]
    out_shape += [jax.ShapeDtypeStruct((1, c), F32) for c in accs]
    return pl.pallas_call(
        kern, name=name, out_shape=out_shape, grid=(R // tr,),
        in_specs=in_specs, out_specs=out_specs,
        compiler_params=_params(("arbitrary",) if n_a else ("parallel",)),
    )(*rows, *vecs)


def _lane(shape):
    return lax.broadcasted_iota(jnp.int32, shape, 1)


def _rot_swa(x):
    lane = _lane(x.shape)
    return jnp.where((lane & 63) < 32, -pltpu.roll(x, 96, 1), pltpu.roll(x, 32, 1))


def _rot_mla(x):
    lane = _lane(x.shape)
    lo = jnp.where(lane >= KR_LANE, -pltpu.roll(x, 112, 1), 0.0)
    hi = jnp.where(lane < KR_LANE + MLA_ROPE_DIM, pltpu.roll(x, 16, 1), 0.0)
    return jnp.where(lane < KR_LANE + 16, lo, hi)


def _rope(x, cos, sin, rot):
    return x * cos + rot(x) * sin


def _rope_t(g, cos, sin, rot):
    return g * cos - rot(g * sin)


def _low(x):
    return jnp.where(_lane(x.shape) < 64, x, 0.0)


def _value_slot(x):
    lane = _lane(x.shape)
    return jnp.where(lane < 64, x, jnp.where(lane == 64, 1.0, 0.0))


def _blk(x, j):
    return x[:, j * LANES:(j + 1) * LANES]


def _rms_r(x, width):
    return lax.rsqrt(jnp.sum(x * x, axis=-1, keepdims=True) * (1.0 / width) + EPS)


def _rms_bwd(x, g, dy, width):
    r = _rms_r(x, width)
    gdy = dy * g
    dot = jnp.sum(gdy * x, axis=-1, keepdims=True)
    dx = r * gdy - x * (r * r * r * (1.0 / width) * dot)
    return dx, dy * x * r


def _colsum(x):
    return jnp.sum(x, axis=0, keepdims=True)


def _rmsnorm_bwd(name, x, g, dy, dres):
    def body(xv, dyv, dr, gv):
        dx, dg = _rms_bwd(xv, gv, dyv, D_MODEL)
        return dx + dr, _colsum(dg)
    return _rowmap(name, body, [x, dy, dres], [g], [(D_MODEL, F32)], [D_MODEL])


def _prep1(proj, qn_g, kv_g, cosa, sina, cosm, sinm):
    def body(p, ca, sa, cm, sm, gq, gk):
        qa = []
        for j in range(4):
            xr = _rope(_blk(p, j), ca, sa, _rot_swa)
            qa += [_low(xr), _low(pltpu.roll(xr, 64, 1))]
        kr_ = _rope(_blk(p, C_KA // LANES), ca, sa, _rot_swa)
        ka = [_low(kr_), _low(pltpu.roll(kr_, 64, 1))]
        vv = _blk(p, C_VA // LANES)
        va = [_value_slot(vv), _value_slot(pltpu.roll(vv, 64, 1))]
        ql = p[:, C_QL:C_QL + MLA_Q_RANK]
        qn = ql * _rms_r(ql, MLA_Q_RANK) * gq
        kl = p[:, C_KL:C_KL + MLA_KV_RANK]
        cn = kl * _rms_r(kl, MLA_KV_RANK) * gk
        kr = _rope(_blk(p, C_KR // LANES), cm, sm, _rot_mla)
        return (jnp.concatenate(qa, 1), jnp.concatenate(ka, 1), jnp.concatenate(va, 1), qn, cn, kr)
    return _rowmap("prep1", body, [proj, cosa, sina, cosm, sinm], [qn_g, kv_g],
                   [(SLOT_W, BF16), (2 * LANES, BF16), (2 * LANES, BF16),
                    (MLA_Q_RANK, BF16), (MLA_KV_RANK, BF16), (LANES, F32)])


def _prep2(qb, kvb, kr, cosm, sinm):
    def body(q, kv, krv, cm, sm):
        qs, ks, vs = [], [], []
        for h in range(MLA_HEADS):
            qs.append(_rope(_blk(q, h), cm, sm, _rot_mla))
            kvh = _blk(kv, h)
            ks.append(_low(kvh) + krv)
            vs.append(_value_slot(pltpu.roll(kvh, 64, 1)))
        return jnp.concatenate(qs, 1), jnp.concatenate(ks, 1), jnp.concatenate(vs, 1)
    return _rowmap("prep2", body, [qb, kvb, kr, cosm, sinm], [],
                   [(SLOT_W, BF16), (SLOT_W, BF16), (SLOT_W, BF16)])


def _compact(slots):
    return jnp.concatenate(
        [_blk(slots, 2 * j) + pltpu.roll(_blk(slots, 2 * j + 1), 64, 1) for j in range(4)], 1)


def _expand(nat):
    out = []
    for j in range(4):
        b = _blk(nat, j)
        out += [_low(b), _low(pltpu.roll(b, 64, 1))]
    return jnp.concatenate(out, 1)


def _merge_fwd(oa, ob, ga, gb):
    def body(a, b, gav, gbv):
        xa, xb = _compact(a), _compact(b)
        return (jnp.concatenate([xa * _rms_r(xa, 512) * gav, xb * _rms_r(xb, 512) * gbv], 1),)
    return _rowmap("merge_fwd", body, [oa, ob], [ga, gb], [(D_MODEL, BF16)])[0]


def _merge_bwd(dmix, oa, ob, lse_a, ga, gb, sink_slots):
    def body(dm, a, b, lse, gav, gbv, sk):
        outs = []
        accs = []
        for o, g, lo in ((a, gav, 0), (b, gbv, 512)):
            x = _compact(o)
            dx, dg = _rms_bwd(x, g, dm[:, lo:lo + 512], 512)
            do = _expand(dx)
            delta = jnp.concatenate(
                [jnp.broadcast_to(jnp.sum(_blk(do, h) * _blk(o, h), axis=-1, keepdims=True),
                                  (do.shape[0], LANES)) for h in range(8)], 1)
            outs += [do, delta]
            accs.append(_colsum(dg))
        dsink = _colsum(-jnp.exp(sk - lse) * outs[1])
        return (*outs, *accs, dsink)
    return _rowmap("merge_bwd", body, [dmix, oa, ob, lse_a], [ga, gb, sink_slots],
                   [(SLOT_W, BF16), (SLOT_W, F32), (SLOT_W, BF16), (SLOT_W, F32)],
                   [512, 512, SLOT_W])


def _prep2_bwd(dq, dk, dv, cosm, sinm):
    def body(dqv, dkv, dvv, cm, sm):
        dqv, dkv, dvv = dqv.astype(F32), dkv.astype(F32), dvv.astype(F32)
        dqb, dkvb = [], []
        krsum = jnp.zeros((dqv.shape[0], LANES), F32)
        for h in range(MLA_HEADS):
            dqb.append(_rope_t(_blk(dqv, h), cm, sm, _rot_mla))
            dkh = _blk(dkv, h)
            dkvb.append(_low(dkh) + pltpu.roll(_blk(dvv, h), 64, 1))
            krsum = krsum + dkh
        lane = _lane(krsum.shape)
        dkr = jnp.where((lane >= KR_LANE) & (lane < KR_LANE + MLA_ROPE_DIM),
                        _rope_t(krsum, cm, sm, _rot_mla), 0.0)
        return jnp.concatenate(dqb, 1), jnp.concatenate(dkvb, 1), dkr
    return _rowmap("prep2_bwd", body, [dq, dk, dv, cosm, sinm], [],
                   [(SLOT_W, BF16), (SLOT_W, BF16), (LANES, F32)])


def _prep1_bwd(proj, dqa, dka, dva, dqn, dcn, dkr, cosa, sina, qn_g, kv_g):
    def body(p, dq, dk, dv, dqnv, dcnv, dkrv, ca, sa, gq, gk):
        dq, dk, dv = dq.astype(F32), dk.astype(F32), dv.astype(F32)
        cols = []
        for j in range(4):
            nat = _blk(dq, 2 * j) + pltpu.roll(_blk(dq, 2 * j + 1), 64, 1)
            cols.append(_rope_t(nat, ca, sa, _rot_swa))
        grp = lambda d, g: sum(_blk(d, 4 * g + i) for i in range(4))
        cols.append(_rope_t(grp(dk, 0) + pltpu.roll(grp(dk, 1), 64, 1), ca, sa, _rot_swa))
        cols.append(grp(dv, 0) + pltpu.roll(grp(dv, 1), 64, 1))
        dql, dgq = _rms_bwd(p[:, C_QL:C_QL + MLA_Q_RANK], gq, dqnv, MLA_Q_RANK)
        dkl, dgk = _rms_bwd(p[:, C_KL:C_KL + MLA_KV_RANK], gk, dcnv, MLA_KV_RANK)
        cols += [dql, dkl, dkrv]
        return jnp.concatenate(cols, 1), _colsum(dgq), _colsum(dgk)
    return _rowmap("prep1_bwd", body, [proj, dqa, dka, dva, dqn, dcn, dkr, cosa, sina], [qn_g, kv_g],
                   [(IN_WP, BF16)], [MLA_Q_RANK, MLA_KV_RANK], tr_prefs=(192, 128))


def _sigmoid(x):
    return 1.0 / (1.0 + jnp.exp(-x))


def _ffn_tiles(T, F):
    return _tile(T, (528, 512, 384, 256, 128)), _tile(F, (1408, 1024, 640, 512, 256, 128))


def _norm_proj(name, h, g, weights, swiglu):
    (T, D), F = h.shape, weights[0].shape[1]
    bm, bn = _ffn_tiles(T, F)
    nw = len(weights)

    def kern(h_ref, g_ref, *rest):
        w_refs, u_ref, o_refs, u_scr = rest[:nw], rest[nw], rest[nw + 1:-1], rest[-1]

        @pl.when(pl.program_id(1) == 0)
        def _():
            x = h_ref[...]
            u = (x * _rms_r(x, D) * g_ref[...]).astype(u_scr.dtype)
            u_scr[...] = u
            u_ref[...] = u

        uv = u_scr[...]
        prods = [jnp.dot(uv, w[...], preferred_element_type=F32) for w in w_refs]
        for o_ref, p in zip(o_refs, prods):
            o_ref[...] = p
        if swiglu:
            a, b = prods
            o_refs[nw][...] = (a * _sigmoid(a) * b).astype(o_refs[nw].dtype)

    w_spec = pl.BlockSpec((D, bn), lambda i, j: (0, j))
    row_spec = pl.BlockSpec((bm, D), lambda i, j: (i, 0))
    o_spec = pl.BlockSpec((bm, bn), lambda i, j: (i, j))
    n_out = nw + (1 if swiglu else 0)
    return pl.pallas_call(
        kern, name=name,
        out_shape=[jax.ShapeDtypeStruct((T, D), MXU_DTYPE)] + [jax.ShapeDtypeStruct((T, F), F32)] * nw
        + ([jax.ShapeDtypeStruct((T, F), MXU_DTYPE)] if swiglu else []),
        grid=(T // bm, F // bn),
        in_specs=[row_spec, pl.BlockSpec((1, D), lambda i, j: (0, 0))] + [w_spec] * nw,
        out_specs=[row_spec] + [o_spec] * n_out,
        scratch_shapes=[pltpu.VMEM((bm, D), MXU_DTYPE)],
        compiler_params=_params(("parallel", "arbitrary")),
    )(h, g, *weights)


def _ffn_mid_bwd(dh, w_down, a, b):
    (T, D), F = dh.shape, w_down.shape[0]
    bm, bn = _ffn_tiles(T, F)

    def kern(dh_ref, wd_ref, a_ref, b_ref, da_ref, db_ref):
        d = lax.dot_general(dh_ref[...].astype(MXU_DTYPE), wd_ref[...], _DIMS["nt"], preferred_element_type=F32)
        av, bv = a_ref[...], b_ref[...]
        s = _sigmoid(av)
        da_ref[...] = (d * bv * (s * (1.0 + av * (1.0 - s)))).astype(da_ref.dtype)
        db_ref[...] = (d * (av * s)).astype(db_ref.dtype)

    o_spec = pl.BlockSpec((bm, bn), lambda i, j: (i, j))
    return pl.pallas_call(
        kern, name="ffn_mid_bwd",
        out_shape=[jax.ShapeDtypeStruct((T, F), MXU_DTYPE)] * 2,
        grid=(T // bm, F // bn),
        in_specs=[pl.BlockSpec((bm, D), lambda i, j: (i, 0)), pl.BlockSpec((bn, D), lambda i, j: (j, 0)),
                  o_spec, o_spec],
        out_specs=[o_spec, o_spec],
        compiler_params=_params(("parallel", "parallel")),
    )(dh, w_down, a, b)


def _loss_head(h, target, g):
    T = h.shape[0]
    nb = T // BLOCK

    def kern(h_ref, t_ref, g_ref, dh_ref, dg_ref, loss_ref, acc):
        i = pl.program_id(0)

        @pl.when(i == 0)
        def _():
            dh_ref[...] = jnp.zeros_like(dh_ref)
            dg_ref[...] = jnp.zeros_like(dg_ref)
            acc[...] = jnp.zeros_like(acc)

        @pl.when(i > 0)
        def _():
            x = h_ref[...]
            gv = g_ref[...]
            e = x * _rms_r(x, D_MODEL) * gv - t_ref[...]
            acc[...] += _colsum(e * e)
            dx, dg = _rms_bwd(x, gv, e * (1.0 / D_MODEL), D_MODEL)
            dh_ref[...] = dx
            dg_ref[...] += _colsum(dg)

        @pl.when(i == nb - 1)
        def _():
            tot = jnp.sum(acc[...], axis=-1, keepdims=True) * (0.5 / D_MODEL)
            loss_ref[...] = jnp.broadcast_to(tot, loss_ref.shape)

    return pl.pallas_call(
        kern, name="loss_head",
        out_shape=[jax.ShapeDtypeStruct((T, D_MODEL), F32), jax.ShapeDtypeStruct((1, D_MODEL), F32),
                   jax.ShapeDtypeStruct((1, LANES), F32)],
        grid=(nb,),
        in_specs=[pl.BlockSpec((BLOCK, D_MODEL), lambda i: (i, 0)),
                  pl.BlockSpec((BLOCK, D_MODEL), lambda i: (jnp.maximum(i - 1, 0), 0)),
                  pl.BlockSpec((1, D_MODEL), lambda i: (0, 0))],
        out_specs=[pl.BlockSpec((BLOCK, D_MODEL), lambda i: (i, 0)),
                   pl.BlockSpec((1, D_MODEL), lambda i: (0, 0)),
                   pl.BlockSpec((1, LANES), lambda i: (0, 0))],
        scratch_shapes=[pltpu.VMEM((1, D_MODEL), F32)],
        compiler_params=_params(("arbitrary",)),
    )(h, target, g)


LOG2E = 1.4426950408889634


def _attn_plan(T, causal):
    tq = _tile(T, (384, 256, 128))
    ck = min(2 * tq, T) if causal else min(tq + WINDOW, T)
    return tq, ck, (-(-T // ck) if causal else 1)


def _chunk(i, c, T, tq, ck, causal):
    if causal:
        return pl.multiple_of(jnp.minimum(c * ck, T - ck), LANES), c * ck
    return pl.multiple_of(jnp.clip(i * tq - WINDOW, 0, T - ck), LANES), 0


def _n_chunks(i, tq, ck, causal):
    return ((i + 1) * tq + ck - 1) // ck if causal else 1


def _mask(s, i, start, first, tq, ck, causal):
    qpos = i * tq + lax.broadcasted_iota(jnp.int32, (tq, 1), 0)
    kpos = start + lax.broadcasted_iota(jnp.int32, (tq, ck), 1)
    low = jnp.maximum(jnp.where(qpos < FRONT, 0, FRONT), first)
    if not causal:
        low = jnp.maximum(low, qpos - (WINDOW - 1))
    return jnp.where(kpos >= low, jnp.where(kpos <= qpos, s, NEG), NEG)


def _chunk_loop(n, body, init, several):
    carry = body(0, init, True)
    if not several:
        return carry
    carry = body(n - 1, carry, True)
    return lax.fori_loop(1, n - 1, lambda c, cr: body(c, cr, False), carry)


def _tile_sweep(nq, tq, ck, causal, q_tiles, pair=False):
    if pair and nq >= 2 and (not causal or ck == 2 * tq):
        if causal:
            q_tiles([0, 1], False)
            lax.fori_loop(1, nq // 2, lambda p, cr: q_tiles([2 * p, 2 * p + 1], True) or cr, 0)
        else:
            lax.fori_loop(0, nq // 2, lambda p, cr: q_tiles([2 * p, 2 * p + 1], False) or cr, 0)
        if nq % 2:
            q_tiles([nq - 1], causal and nq >= 3)
        return
    one = min(nq, ck // tq) if causal else nq
    lax.fori_loop(0, one, lambda i, cr: q_tiles([i], False) or cr, 0)
    if one < nq:
        lax.fori_loop(one, nq, lambda i, cr: q_tiles([i], True) or cr, 0)


BWD_HEADS_PER_STEP = 1


def _head_cols(group, hp):
    q_cols = lambda hh: slice(hh * LANES, (hh + 1) * LANES)
    if group == 1:
        return q_cols, q_cols, hp * LANES
    assert group % hp == 0
    return q_cols, (lambda hh: slice(0, LANES)), LANES


def _whole(T, width, index, single):
    if single:
        return pl.BlockSpec((T, width), index, pipeline_mode=pl.Buffered(1))
    return pl.BlockSpec((T, width), index)


def _attn_fwd(name, q, k, v, sinks, group, causal, scale, hp):
    T = q.shape[0]
    H = q.shape[1] // LANES
    HP = hp
    tq, ck, slots = _attn_plan(T, causal)
    nq = T // tq
    c2 = scale * LOG2E
    q_cols, k_cols, kw = _head_cols(group, HP)

    def kern(sink_ref, q_ref, k_ref, v_ref, o_ref, lse_ref, s_scr):
        sink2 = [sink_ref[pl.program_id(0) * HP + hh] * LOG2E for hh in range(HP)]

        def q_tile(tiles, several):
            i, = tiles
            rows = pl.ds(pl.multiple_of(i * tq, tq), tq)
            qq = [q_ref[rows, q_cols(hh)] for hh in range(HP)]
            n = _n_chunks(i, tq, ck, causal)

            def score(c, m2, masked):
                start, first = _chunk(i, c, T, tq, ck, causal)
                out = []
                for hh in range(HP):
                    s = lax.dot_general(qq[hh], k_ref[pl.ds(start, ck), k_cols(hh)], _DIMS["nt"],
                                        preferred_element_type=F32) * c2
                    if masked:
                        s = _mask(s, i, start, first, tq, ck, causal)
                    s_scr[hh, c] = s
                    out.append(jnp.maximum(m2[hh], jnp.max(s, axis=-1, keepdims=True)))
                return tuple(out)

            m2 = _chunk_loop(n, score, tuple(jnp.full((tq, 1), sk, F32) for sk in sink2), several)

            def weigh(c, acc):
                start, _ = _chunk(i, c, T, tq, ck, causal)
                out = []
                for hh in range(HP):
                    p = jnp.exp2(s_scr[hh, c] - m2[hh])
                    out.append(acc[hh] + jnp.dot(p.astype(MXU_DTYPE), v_ref[pl.ds(start, ck), k_cols(hh)],
                                                 preferred_element_type=F32))
                return tuple(out)

            acc = lax.fori_loop(0, n, weigh, tuple(jnp.zeros((tq, LANES), F32) for _ in range(HP)))
            lane = _lane((tq, LANES))
            for hh in range(HP):
                l = acc[hh][:, 64:65] + jnp.exp2(sink2[hh] - m2[hh])
                o_ref[rows, q_cols(hh)] = jnp.where(lane < 64, acc[hh] / l, 0.0)
                lse_ref[rows, q_cols(hh)] = jnp.broadcast_to(m2[hh] * (1.0 / LOG2E) + jnp.log(l), (tq, LANES))

        _tile_sweep(nq, tq, ck, causal, q_tile)

    q_spec = _whole(T, HP * LANES, lambda g: (0, g), True)
    kv_spec = _whole(T, kw, (lambda g: (0, g)) if group == 1 else (lambda g: (0, (g * HP) // group)), True)
    return pl.pallas_call(
        kern, name=name,
        out_shape=[jax.ShapeDtypeStruct((T, H * LANES), F32)] * 2,
        grid=(H // HP,),
        in_specs=[pl.BlockSpec(memory_space=pltpu.SMEM), q_spec, kv_spec, kv_spec],
        out_specs=[q_spec, q_spec],
        scratch_shapes=[pltpu.VMEM((HP, slots, tq, ck), F32)],
        compiler_params=_params(("parallel",)),
    )(sinks, q, k, v)


def _attn_bwd(name, q, k, v, do, lse, delta, group, causal, scale):
    T = q.shape[0]
    H = q.shape[1] // LANES
    HP = BWD_HEADS_PER_STEP
    tq, ck, _ = _attn_plan(T, causal)
    nq = T // tq
    c2 = scale * LOG2E
    q_cols, k_cols, kw = _head_cols(group, HP)

    def kern(q_ref, k_ref, v_ref, do_ref, lse_ref, dl_ref, dq_ref, dk_ref, dv_ref, dk_acc, dv_acc):
        dk_acc[...] = jnp.zeros_like(dk_acc)
        dv_acc[...] = jnp.zeros_like(dv_acc)

        def q_tiles(tiles, several):
            rows = [pl.ds(pl.multiple_of(i * tq, tq), tq) for i in tiles]
            chains = [(t, hh) for t in range(len(tiles)) for hh in range(HP)]
            qq = [q_ref[rows[t], q_cols(hh)] for t, hh in chains]
            dd = [do_ref[rows[t], q_cols(hh)] for t, hh in chains]
            lse2 = [lse_ref[rows[t], q_cols(hh)][:, 0:1] * LOG2E for t, hh in chains]
            dl_c = [dl_ref[rows[t], q_cols(hh)][:, 0:1] for t, hh in chains]

            def chunk(c, dq, masked):
                out = []
                for n_ch, (t, hh) in enumerate(chains):
                    start, first = _chunk(tiles[t], c, T, tq, ck, causal)
                    keys = pl.ds(start, ck)
                    kk, vv = k_ref[keys, k_cols(hh)], v_ref[keys, k_cols(hh)]
                    s = lax.dot_general(qq[n_ch], kk, _DIMS["nt"], preferred_element_type=F32)
                    if masked:
                        s = _mask(s, tiles[t], start, first, tq, ck, causal)
                    p = jnp.exp2(s * c2 - lse2[n_ch])
                    dv_acc[keys, q_cols(hh)] += lax.dot_general(p.astype(MXU_DTYPE), dd[n_ch], _DIMS["tn"],
                                                                preferred_element_type=F32)
                    dp = lax.dot_general(dd[n_ch], vv, _DIMS["nt"], preferred_element_type=F32)
                    ds = (p * (dp - dl_c[n_ch])).astype(MXU_DTYPE)
                    dk_acc[keys, q_cols(hh)] += lax.dot_general(ds, qq[n_ch], _DIMS["tn"],
                                                                preferred_element_type=F32) * scale
                    out.append(dq[n_ch] + jnp.dot(ds, kk, preferred_element_type=F32))
                return tuple(out)

            dq = _chunk_loop(_n_chunks(tiles[0], tq, ck, causal), chunk,
                             tuple(jnp.zeros((tq, LANES), F32) for _ in chains), several)
            for n_ch, (t, hh) in enumerate(chains):
                dq_ref[rows[t], q_cols(hh)] = (dq[n_ch] * scale).astype(dq_ref.dtype)

        _tile_sweep(nq, tq, ck, causal, q_tiles, pair=True)
        dk_ref[...] = dk_acc[...].astype(dk_ref.dtype)
        dv_ref[...] = dv_acc[...].astype(dv_ref.dtype)

    q_spec = _whole(T, HP * LANES, lambda g: (0, g), False)
    kv_spec = _whole(T, kw, (lambda g: (0, g)) if group == 1 else (lambda g: (0, (g * HP) // group)), False)
    return pl.pallas_call(
        kern, name=name,
        out_shape=[jax.ShapeDtypeStruct((T, H * LANES), MXU_DTYPE)] * 3,
        grid=(H // HP,),
        in_specs=[q_spec, kv_spec, kv_spec, q_spec, q_spec, q_spec],
        out_specs=[q_spec, q_spec, q_spec],
        scratch_shapes=[pltpu.VMEM((T, HP * LANES), F32)] * 2,
        compiler_params=_params(("parallel",)),
    )(q, k, v, do, lse, delta)


def _ew(name, fn, ins, out_dtypes):
    shape = ins[0].shape
    flat = [a.reshape(-1, shape[-1]) for a in ins]
    R, C = flat[0].shape
    tr = _tile(R, (512, 256, 128, 64, 32, 16, 8))
    n_in = len(ins)

    def kern(*refs):
        res = fn(*[r[...] for r in refs[:n_in]])
        for o_ref, val in zip(refs[n_in:], res):
            o_ref[...] = val.astype(o_ref.dtype)

    spec = pl.BlockSpec((tr, C), lambda i: (i, 0))
    outs = pl.pallas_call(
        kern, name=name,
        out_shape=[jax.ShapeDtypeStruct((R, C), dt) for dt in out_dtypes],
        grid=(R // tr,), in_specs=[spec] * n_in, out_specs=[spec] * len(out_dtypes),
        compiler_params=_params(("parallel",)),
    )(*flat)
    return [o.reshape(shape) for o in outs]


def _adamw(name, w, g, m, v):
    c1 = 1.0 - ADAM_B1 ** ADAM_STEP
    c2 = 1.0 - ADAM_B2 ** ADAM_STEP

    def fn(wv, gv, mv, vv):
        mn = ADAM_B1 * mv + (1.0 - ADAM_B1) * gv
        vn = ADAM_B2 * vv + (1.0 - ADAM_B2) * (gv * gv)
        delta = -ADAM_LR * ((mn / c1) / (jnp.sqrt(vn / c2) + ADAM_EPS) + ADAM_WD * wv)
        return delta, mn, vn

    return _ew(name, fn, [w, g, m, v], [F32, F32, F32])


_ANY = pl.BlockSpec(memory_space=pl.ANY)


def _where_am_i():
    x, y, c = lax.axis_index("x"), lax.axis_index("y"), lax.axis_index("c")
    chips = [(1 - x, y), (x, 1 - y), (1 - x, 1 - y)]
    return x, y, c, chips


def _gather_weights(shards, meta):
    arrs = list(shards) + [meta]
    n = len(arrs)
    per = [a.shape[0] // 2 for a in arrs]

    def body(*refs):
        ins, outs = refs[:n], refs[n:2 * n]
        send1, recv1, send2, recv2 = refs[2 * n:]
        x, y, c, chips = _where_am_i()
        me = 2 * x + y

        def half(ref, k, cc):
            return ref.at[pl.ds(per[k] * cc, per[k])]

        first = []
        for k in range(n):
            for j, (cx, cy) in enumerate(chips):
                first.append(pltpu.make_async_remote_copy(
                    src_ref=half(ins[k], k, c), dst_ref=half(outs[k].at[me], k, c),
                    send_sem=send1.at[k, j], recv_sem=recv1.at[k, j],
                    device_id=(cx, cy, c), device_id_type=MESH))
        for cp in first:
            cp.start()
        passed = []
        for k in range(n):
            for j, (cx, cy) in enumerate(chips):
                landed = half(outs[k].at[2 * cx + cy], k, c)
                pltpu.make_async_remote_copy(
                    src_ref=landed, dst_ref=landed, send_sem=send1.at[k, j], recv_sem=recv1.at[k, j],
                    device_id=(cx, cy, c), device_id_type=MESH).wait_recv()
                fwd = pltpu.make_async_remote_copy(
                    src_ref=landed, dst_ref=landed, send_sem=send2.at[k, j], recv_sem=recv2.at[k, j],
                    device_id=(x, y, 1 - c), device_id_type=MESH)
                fwd.start()
                passed.append(fwd)
        for k in range(n):
            for j, (cx, cy) in enumerate(chips):
                other = half(outs[k].at[2 * cx + cy], k, 1 - c)
                pltpu.make_async_remote_copy(
                    src_ref=other, dst_ref=other, send_sem=send2.at[k, j], recv_sem=recv2.at[k, j],
                    device_id=(x, y, 1 - c), device_id_type=MESH).wait_recv()
        for cp in first + passed:
            cp.wait_send()

    return pl.pallas_call(
        body, name="gather_weights",
        out_shape=[jax.ShapeDtypeStruct((4,) + a.shape, a.dtype) for a in arrs],
        in_specs=[_ANY] * n, out_specs=[_ANY] * n,
        scratch_shapes=[pltpu.SemaphoreType.DMA((n, 3))] * 4,
    )(*arrs)


_HBM = pl.BlockSpec(memory_space=pltpu.HBM)
_SEM = pl.BlockSpec(memory_space=pltpu.SEMAPHORE)
_FLOWS = pltpu.SideEffectType.DATAFLOW_SIDE_EFFECTING


def _in_hbm(a):
    return pltpu.with_memory_space_constraint(a, pltpu.HBM)


def _gather_start(shards):
    n = len(shards)
    nl = shards[0].shape[0]
    lands = [lax.empty((4,) + sh.shape[1:], sh.dtype) for _ in range(nl) for sh in shards]

    def body(*refs):
        srcs, land = refs[:n], refs[n:n + n * nl]
        send, recv, token = refs[n + n * nl:n + n * nl + 3]
        x, y, c, chips = _where_am_i()
        me = 2 * x + y
        for l in range(nl):
            for k in range(n):
                for cx, cy in chips:
                    pltpu.make_async_remote_copy(
                        src_ref=srcs[k].at[l], dst_ref=land[l * n + k].at[me],
                        send_sem=send.at[l], recv_sem=recv.at[l],
                        device_id=(cx, cy, c), device_id_type=MESH).start()
        token[...] = jnp.zeros_like(token)

    ins = [_in_hbm(a) for a in list(shards) + lands]
    res = pl.pallas_call(
        body, name="gather_start",
        out_shape=(pltpu.SemaphoreType.DMA((nl,)), pltpu.SemaphoreType.DMA((nl,)),
                   jax.ShapeDtypeStruct((8, LANES), F32)) + tuple(pltpu.HBM(a.shape, a.dtype) for a in ins),
        in_specs=[_HBM] * len(ins),
        out_specs=(_SEM, _SEM, pl.BlockSpec(memory_space=pltpu.VMEM)) + (_HBM,) * len(ins),
        input_output_aliases={i: 3 + i for i in range(len(ins))},
        compiler_params=pltpu.CompilerParams(has_side_effects=_FLOWS),
    )(*ins)
    send, recv, token = res[:3]
    thru = res[3:3 + n]
    lands = res[3 + n:]
    return send, recv, token, list(thru), [list(lands[l * n:(l + 1) * n]) for l in range(nl)]


def _gather_wait(name, send, recv, l, thru, land, after):
    n = len(thru)

    def body(*refs):
        srcs, lands = refs[:n], refs[n:2 * n]
        send_sem, recv_sem = refs[2 * n:2 * n + 2]
        x, y, c, chips = _where_am_i()
        for k in range(n):
            for cx, cy in chips:
                copy = pltpu.make_async_remote_copy(
                    src_ref=srcs[k].at[l], dst_ref=lands[k].at[2 * cx + cy],
                    send_sem=send_sem.at[l], recv_sem=recv_sem.at[l],
                    device_id=(cx, cy, c), device_id_type=MESH)
                copy.wait_send()
                copy.wait_recv()

    ins = list(thru) + list(land)
    res = pl.pallas_call(
        body, name=name,
        out_shape=tuple(pltpu.HBM(a.shape, a.dtype) for a in ins),
        in_specs=[_HBM] * len(ins) + [_SEM, _SEM, _ANY],
        out_specs=(_HBM,) * len(ins),
        input_output_aliases={i: i for i in range(len(ins))},
        compiler_params=pltpu.CompilerParams(has_side_effects=_FLOWS),
    )(*ins, send, recv, after)
    return list(res[:n]), list(res[n:])


def _row_chunks(r):
    n = 4 if r % 64 == 0 else 1
    return [(q * (r // n), r // n) for q in range(n)]


def _pair_exchange(grads):
    n = len(grads)

    def body(*refs):
        ins, got = refs[:n], refs[n:2 * n]
        send, recv = refs[2 * n:]
        x, y, c, _ = _where_am_i()
        sib = (x, y, 1 - c)
        for k in range(n):
            for ch in range(4):
                for l in range(2):
                    pltpu.make_async_remote_copy(
                        src_ref=ins[k].at[ch, 2 * (1 - c) + l], dst_ref=got[k].at[ch, l],
                        send_sem=send.at[k], recv_sem=recv.at[k], device_id=sib, device_id_type=MESH).start()
        for k in range(n):
            pltpu.make_async_remote_copy(
                src_ref=got[k], dst_ref=got[k], send_sem=send.at[k], recv_sem=recv.at[k],
                device_id=sib, device_id_type=MESH).wait()

    return pl.pallas_call(
        body, name="reduce_pair",
        out_shape=[jax.ShapeDtypeStruct((4, 2) + g.shape[2:], g.dtype) for g in grads],
        in_specs=[_ANY] * n, out_specs=[_ANY] * n,
        scratch_shapes=[pltpu.SemaphoreType.DMA((n,))] * 2,
    )(*grads)


def _pair_add(full, got, c):
    _, _, r, cols = full.shape
    tr = _tile(r, (512, 256, 352, 128))

    def kern(c_ref, a_ref, b_ref, o32_ref, o16_ref):
        tot = a_ref[...] + b_ref[...].astype(F32)
        o32_ref[...] = tot
        o16_ref[...] = tot.astype(o16_ref.dtype)

    blk = (None, None, tr, cols)
    mine = pl.BlockSpec(blk, lambda ch, l, i, cr: (ch, 2 * cr[0] + l, i, 0))
    same = pl.BlockSpec(blk, lambda ch, l, i, cr: (ch, l, i, 0))
    return pl.pallas_call(
        kern, name="pair_add",
        out_shape=[jax.ShapeDtypeStruct(got.shape, F32), jax.ShapeDtypeStruct(got.shape, got.dtype)],
        grid_spec=pltpu.PrefetchScalarGridSpec(
            num_scalar_prefetch=1, grid=(4, 2, r // tr), in_specs=[mine, same], out_specs=[same, same]),
        compiler_params=_params(("parallel", "parallel", "parallel")),
    )(c.reshape(1), full, got)


def _chip_scatter(parts):
    n = len(parts)

    def body(*refs):
        ins, outs = refs[:n], refs[n:2 * n]
        send, recv = refs[2 * n:]
        x, y, c, chips = _where_am_i()
        me = 2 * x + y
        for k in range(n):
            for j, (cx, cy) in enumerate(chips):
                for l in range(2):
                    pltpu.make_async_remote_copy(
                        src_ref=ins[k].at[2 * cx + cy, l], dst_ref=outs[k].at[me, l],
                        send_sem=send.at[k, j], recv_sem=recv.at[k, j],
                        device_id=(cx, cy, c), device_id_type=MESH).start()
        for k in range(n):
            for j, (cx, cy) in enumerate(chips):
                slot = outs[k].at[2 * cx + cy]
                pltpu.make_async_remote_copy(
                    src_ref=slot, dst_ref=slot, send_sem=send.at[k, j], recv_sem=recv.at[k, j],
                    device_id=(cx, cy, c), device_id_type=MESH).wait()

    return pl.pallas_call(
        body, name="reduce_chips",
        out_shape=[jax.ShapeDtypeStruct(p.shape, p.dtype) for p in parts],
        in_specs=[_ANY] * n, out_specs=[_ANY] * n,
        scratch_shapes=[pltpu.SemaphoreType.DMA((n, 3))] * 2,
    )(*parts)


def _chip_add(landed, mine, me, c):
    _, _, r, cols = landed.shape
    tr = _tile(r, (512, 256, 352, 128))

    def kern(me_ref, c_ref, land_ref, own_ref, o_ref):
        own = own_ref[...]
        tot = None
        for j in range(4):
            term = jnp.where(me_ref[0] == j, own, land_ref[j].astype(F32))
            tot = term if tot is None else tot + term
        o_ref[...] = tot

    return pl.pallas_call(
        kern, name="chip_add",
        out_shape=jax.ShapeDtypeStruct((4, r, cols), F32),
        grid_spec=pltpu.PrefetchScalarGridSpec(
            num_scalar_prefetch=2, grid=(2, r // tr),
            in_specs=[pl.BlockSpec((4, None, tr, cols), lambda l, i, mr, cr: (0, l, i, 0)),
                      pl.BlockSpec((None, None, tr, cols), lambda l, i, mr, cr: (mr[0], l, i, 0))],
            out_specs=pl.BlockSpec((None, tr, cols), lambda l, i, mr, cr: (2 * cr[0] + l, i, 0))),
        compiler_params=_params(("parallel", "parallel")),
    )(me.reshape(1), c.reshape(1), landed, mine)


def _pair_join(sums):
    n = len(sums)

    def body(*refs):
        bufs = refs[n:2 * n]
        send, recv = refs[2 * n:]
        x, y, c, _ = _where_am_i()
        sib = (x, y, 1 - c)
        for k in range(n):
            for l in range(2):
                for r0, rn in _row_chunks(bufs[k].shape[1]):
                    piece = bufs[k].at[2 * c + l, pl.ds(r0, rn)]
                    pltpu.make_async_remote_copy(
                        src_ref=piece, dst_ref=piece, send_sem=send.at[k], recv_sem=recv.at[k],
                        device_id=sib, device_id_type=MESH).start()
        for k in range(n):
            theirs = bufs[k].at[pl.ds(2 * (1 - c), 2)]
            pltpu.make_async_remote_copy(
                src_ref=theirs, dst_ref=theirs, send_sem=send.at[k], recv_sem=recv.at[k],
                device_id=sib, device_id_type=MESH).wait()

    return pl.pallas_call(
        body, name="reduce_join",
        out_shape=[jax.ShapeDtypeStruct(s.shape, s.dtype) for s in sums],
        in_specs=[_ANY] * n, out_specs=[_ANY] * n,
        input_output_aliases={k: k for k in range(n)},
        scratch_shapes=[pltpu.SemaphoreType.DMA((n,))] * 2,
    )(*sums)


def _allreduce_small(buf):
    R = buf.shape[0]

    def body(in_ref, out_ref, land, send, recv):
        x, y, c, _ = _where_am_i()
        me = 4 * x + 2 * y + c
        land[me] = in_ref[...]
        cps = []
        for k in range(1, 8):
            px, py, pc = x ^ (k >> 2), y ^ ((k >> 1) & 1), c ^ (k & 1)
            cps.append(pltpu.make_async_remote_copy(
                src_ref=in_ref, dst_ref=land.at[me], send_sem=send.at[k - 1], recv_sem=recv.at[k - 1],
                device_id=(px, py, pc), device_id_type=MESH))
        for cp in cps:
            cp.start()
        for k in range(1, 8):
            px, py, pc = x ^ (k >> 2), y ^ ((k >> 1) & 1), c ^ (k & 1)
            slot = land.at[4 * px + 2 * py + pc]
            pltpu.make_async_remote_copy(
                src_ref=slot, dst_ref=slot, send_sem=send.at[k - 1], recv_sem=recv.at[k - 1],
                device_id=(px, py, pc), device_id_type=MESH).wait_recv()
        for cp in cps:
            cp.wait_send()
        tot = land[0]
        for d in range(1, 8):
            tot = tot + land[d]
        out_ref[...] = tot

    vm = pl.BlockSpec(memory_space=pltpu.VMEM)
    return pl.pallas_call(
        body, name="allreduce_small",
        out_shape=jax.ShapeDtypeStruct(buf.shape, F32),
        in_specs=[vm], out_specs=vm,
        scratch_shapes=[pltpu.VMEM((8, R, LANES), F32), pltpu.SemaphoreType.DMA((7,)),
                        pltpu.SemaphoreType.DMA((7,))],
    )(buf)


def _rope_tables(T):
    pos = (jnp.arange(T) - FRONT).astype(F32)
    lane = jnp.arange(LANES)
    inv_a = ROPE_THETA ** (-(2 * ((lane % 64) % 32)).astype(F32) / SWA_HEAD_DIM)
    ang_a = pos[:, None] * inv_a[None, :]
    cosa, sina = jnp.cos(ang_a), jnp.sin(ang_a)
    inv_m = ROPE_THETA ** (-(2 * ((lane - KR_LANE) % 16)).astype(F32) / MLA_ROPE_DIM)
    ang_m = pos[:, None] * inv_m[None, :]
    on = ((lane >= KR_LANE) & (lane < KR_LANE + MLA_ROPE_DIM))[None, :]
    cosm = jnp.where(on, jnp.cos(ang_m), 1.0)
    sinm = jnp.where(on, jnp.sin(ang_m), 0.0)
    return cosa, sina, cosm, sinm


def _cols_from_chips(g):
    return jnp.concatenate(g, axis=-1)


def _rows_from_chips(g):
    return jnp.concatenate(g, axis=-2)


def _cols_to_chips(w):
    L, r, c4 = w.shape
    return jnp.moveaxis(w.reshape(L, r, 4, c4 // 4), 2, 0)


def _rows_to_chips(w):
    L, r4, c = w.shape
    return jnp.moveaxis(w.reshape(L, 4, r4 // 4, c), 1, 0)


def _layer_layouts(w_in, w_qup, w_kvup, w_o, w_gate, w_up, w_down):
    zpad = lambda n: jnp.zeros((D_MODEL, n), w_in.dtype)
    w_in = jnp.concatenate([w_in[:, :C_KR], zpad(KR_LANE), w_in[:, C_KR:IN_W],
                            zpad(LANES - KR_LANE - MLA_ROPE_DIM)], axis=-1)
    w_qup = w_qup.reshape(MLA_Q_RANK, MLA_HEADS, MLA_QK_DIM)
    w_qup = jnp.pad(w_qup, ((0, 0), (0, 0), (0, LANES - MLA_QK_DIM))).reshape(MLA_Q_RANK, SLOT_W)
    return w_in, w_qup, w_kvup, w_o, w_gate, w_up, w_down


def _local_step(x2, target, meta_full, layer_weights, p):
    T = BLOCK + x2.shape[0]
    L = DEPTH
    attn_norm, q_norm, kv_norm, sinks = p["attn_norm"], p["q_norm"], p["kv_norm"], p["sinks"]
    out_norm_swa, out_norm_mla, ffn_norm, final_norm = (
        p["out_norm_swa"], p["out_norm_mla"], p["ffn_norm"], p["final_norm"])

    cosa, sina, cosm, sinm = _rope_tables(T)
    no_sink = jnp.full((MLA_HEADS,), NEG, F32)
    scale_a, scale_b = SWA_HEAD_DIM ** -0.5, MLA_QK_DIM ** -0.5
    row = lambda v: v.reshape(1, -1)

    h = jnp.concatenate([jnp.zeros((FRONT, D_MODEL), F32), meta_full, x2], axis=0)
    saved = []
    weights = []
    for l in range(L):
        weights.append(_layer_layouts(*layer_weights(l, h)))
        W_in, W_qup, W_kvup, W_o, W_gate, W_up, W_down = weights[l]
        u, proj = _norm_proj("in_proj", h, row(attn_norm[l]), [W_in], False)
        qa, ka, va, qn, cn, kr = _prep1(proj, row(q_norm[l]), row(kv_norm[l]), cosa, sina, cosm, sinm)
        qb = _mm("q_up", qn, W_qup, "nn")
        kvb = _mm("kv_up", cn, W_kvup, "nn")
        qs, ks, vs = _prep2(qb, kvb, kr, cosm, sinm)
        oa, lse_a = _attn_fwd("swa_fwd", qa, ka, va, sinks[l], 4, False, scale_a, 4)
        ob, lse_b = _attn_fwd("mla_fwd", qs, ks, vs, no_sink, 1, True, scale_b, 2)
        mix = _merge_fwd(oa, ob, row(out_norm_swa[l]), row(out_norm_mla[l]))
        h1 = _mm("o_proj", mix, W_o, "nn", res=h)
        u2, a, b, hm = _norm_proj("ffn_in", h1, row(ffn_norm[l]), [W_gate, W_up], True)
        h2 = _mm("down_proj", hm, W_down, "nn", res=h1)
        saved.append((h, u, proj, qa, ka, va, qn, cn, qs, ks, vs, oa, lse_a, ob, lse_b, mix, h1, u2, a, b, hm))
        h = h2

    dh, d_final, loss_row = _loss_head(h, target, row(final_norm))

    gw = {k: [None] * L for k in ("in", "qup", "kvup", "o", "gate", "up", "down")}
    gs = {k: [None] * L for k in ("attn", "qn", "kvn", "sink", "ga", "gb", "ffn")}
    for l in reversed(range(L)):
        (h0, u, proj, qa, ka, va, qn, cn, qs, ks, vs, oa, lse_a, ob, lse_b, mix, h1, u2, a, b, hm) = saved[l]
        W_in, W_qup, W_kvup, W_o, W_gate, W_up, W_down = weights[l]
        gw["down"][l] = _mm("down_dw", hm, dh, "tn")
        da, db = _ffn_mid_bwd(dh, W_down, a, b)
        gw["gate"][l] = _mm("gate_dw", u2, da, "tn")
        gw["up"][l] = _mm("up_dw", u2, db, "tn")
        du2 = _mm("gate_dx", da, W_gate, "nt")
        du2 = _mm("up_dx", db, W_up, "nt", res=du2)
        dh1, gs["ffn"][l] = _rmsnorm_bwd("ffn_norm_bwd", h1, row(ffn_norm[l]), du2, dh)
        gw["o"][l] = _mm("o_dw", mix, dh1, "tn")
        dmix = _mm("o_dx", dh1, W_o, "nt")
        sink_slots = jnp.repeat(sinks[l], LANES).reshape(1, SLOT_W)
        doa, dla, dob, dlb, gs["ga"][l], gs["gb"][l], dsink = _merge_bwd(
            dmix, oa, ob, lse_a, row(out_norm_swa[l]), row(out_norm_mla[l]), sink_slots)
        gs["sink"][l] = dsink.reshape(SWA_HEADS, LANES)[:, 0]
        dqs, dks, dvs = _attn_bwd("mla_bwd", qs, ks, vs, dob, lse_b, dlb, 1, True, scale_b)
        dqa, dka, dva = _attn_bwd("swa_bwd", qa, ka, va, doa, lse_a, dla, 4, False, scale_a)
        dqb, dkvb, dkr = _prep2_bwd(dqs, dks, dvs, cosm, sinm)
        gw["qup"][l] = _mm("q_up_dw", qn, dqb, "tn")
        gw["kvup"][l] = _mm("kv_up_dw", cn, dkvb, "tn")
        dqn = _mm("q_up_dx", dqb, W_qup, "nt")
        dcn = _mm("kv_up_dx", dkvb, W_kvup, "nt")
        dproj, gs["qn"][l], gs["kvn"][l] = _prep1_bwd(
            proj, dqa, dka, dva, dqn, dcn, dkr, cosa, sina, row(q_norm[l]), row(kv_norm[l]))
        gw["in"][l] = _mm("in_dw", u, dproj, "tn")
        du = _mm("in_dx", dproj, W_in, "nt")
        dh, gs["attn"][l] = _rmsnorm_bwd("attn_norm_bwd", h0, row(attn_norm[l]), du, dh1)

    st = lambda k: jnp.stack(gw[k])
    d_in = st("in")
    d_in = jnp.concatenate([d_in[..., :C_KR], d_in[..., C_KR + KR_LANE:C_KR + KR_LANE + MLA_ROPE_DIM]], axis=-1)
    d_qup = st("qup").reshape(L, MLA_Q_RANK, MLA_HEADS, LANES)[..., :MLA_QK_DIM].reshape(L, MLA_Q_RANK, -1)
    d_nat = [d_in, d_qup, st("kvup"), st("o"), st("gate"), st("up"), st("down")]
    return loss_row, dh, d_nat, gs, d_final


def kernel(x, meta_tokens, attn_norm, w_in, q_norm, w_q_up, kv_norm, w_kv_up, sinks, out_norm_swa, out_norm_mla, w_o, ffn_norm, w_gate, w_up, w_down, final_norm, loss_target, m_meta_tokens, m_attn_norm, m_w_in, m_q_norm, m_w_q_up, m_kv_norm, m_w_kv_up, m_sinks, m_out_norm_swa, m_out_norm_mla, m_w_o, m_ffn_norm, m_w_gate, m_w_up, m_w_down, m_final_norm, v_meta_tokens, v_attn_norm, v_w_in, v_q_norm, v_w_q_up, v_kv_norm, v_w_kv_up, v_sinks, v_out_norm_swa, v_out_norm_mla, v_w_o, v_ffn_norm, v_w_gate, v_w_up, v_w_down, v_final_norm):
    assert x.shape[0] == 1 and x.shape[1] % BLOCK == 0
    big = [w_in, w_q_up, w_kv_up, w_o, w_gate, w_up, w_down]

    c_idx = lax.axis_index("c").astype(jnp.int32)
    chip = (2 * lax.axis_index("x") + lax.axis_index("y")).astype(jnp.int32)
    w16 = [w.astype(BF16) for w in big]
    join = [_cols_from_chips, _cols_from_chips, _cols_from_chips, _rows_from_chips, _cols_from_chips,
            _cols_from_chips, _rows_from_chips]

    def whole(own, landed):
        return [f([jnp.where(chip == j, o, g[j]) for j in range(4)]) for f, o, g in zip(join, own, landed)]

    first = _gather_weights([w[0] for w in w16], meta_tokens)
    meta_full = jnp.concatenate([jnp.where(chip == j, meta_tokens, first[-1][j]) for j in range(4)], axis=-1)
    send, recv, token, thru, lands = _gather_start([w[1:] for w in w16])
    state = {"thru": thru}

    def layer_weights(l, h):
        if l == 0:
            return whole([w[0] for w in w16], first[:-1])
        state["thru"], landed = _gather_wait("gather_wait%d" % l, send, recv, l - 1, state["thru"], lands[l - 1], h)
        return whole([w[l] for w in w16], landed)

    small_p = dict(attn_norm=attn_norm, q_norm=q_norm, kv_norm=kv_norm, sinks=sinks, out_norm_swa=out_norm_swa,
                   out_norm_mla=out_norm_mla, ffn_norm=ffn_norm, final_norm=final_norm)
    loss_row, dh, d_nat, gs, d_final = _local_step(x[0] + token[0, 0], loss_target[0], meta_full, layer_weights, small_p)
    grad_x = dh[BLOCK:][None]

    split = [_cols_to_chips, _cols_to_chips, _cols_to_chips, _rows_to_chips, _cols_to_chips, _cols_to_chips,
             _rows_to_chips]
    full = [f(d) for f, d in zip(split, d_nat)]

    got = _pair_exchange([f.astype(BF16) for f in full])
    sums = [_pair_add(f, g, c_idx) for f, g in zip(full, got)]
    landed = _chip_scatter([s16 for _, s16 in sums])
    g_big = _pair_join([_chip_add(t, s32, chip, c_idx) for t, (s32, _) in zip(landed, sums)])

    small = [jnp.stack(gs["attn"]).reshape(-1), jnp.stack(gs["qn"]).reshape(-1), jnp.stack(gs["kvn"]).reshape(-1),
             jnp.stack(gs["sink"]).reshape(-1), jnp.stack(gs["ga"]).reshape(-1), jnp.stack(gs["gb"]).reshape(-1),
             jnp.stack(gs["ffn"]).reshape(-1), d_final.reshape(-1)]
    sizes = [s.shape[0] for s in small]
    flat = jnp.concatenate(small + [dh[FRONT:BLOCK].reshape(-1), loss_row[0, :1]])
    n_flat = flat.shape[0]
    rows_needed = -(-n_flat // (8 * LANES)) * 8
    flat = jnp.pad(flat, (0, rows_needed * LANES - n_flat)).reshape(rows_needed, LANES)
    tot = _allreduce_small(flat).reshape(-1)
    n_small = sum(sizes)
    loss = tot[n_small + N_META * D_MODEL]
    g_meta_full = tot[n_small:n_small + N_META * D_MODEL].reshape(N_META, D_MODEL)
    g_meta_mine = lax.dynamic_slice_in_dim(g_meta_full, chip * (D_MODEL // 4), D_MODEL // 4, axis=1)

    small_w = [attn_norm, q_norm, kv_norm, sinks, out_norm_swa, out_norm_mla, ffn_norm, final_norm]
    small_m = [m_attn_norm, m_q_norm, m_kv_norm, m_sinks, m_out_norm_swa, m_out_norm_mla, m_ffn_norm, m_final_norm]
    small_v = [v_attn_norm, v_q_norm, v_kv_norm, v_sinks, v_out_norm_swa, v_out_norm_mla, v_ffn_norm, v_final_norm]
    n_rows = -(-n_small // (8 * LANES)) * 8

    def pack(arrs):
        f = jnp.concatenate([a.reshape(-1) for a in arrs])
        return jnp.pad(f, (0, n_rows * LANES - n_small), constant_values=1.0).reshape(n_rows, LANES)

    g_small_pack = jnp.pad(tot[:n_small], (0, n_rows * LANES - n_small)).reshape(n_rows, LANES)
    upd_small = _adamw("adam_small", pack(small_w), g_small_pack, pack(small_m), pack(small_v))

    def unpack(p):
        f = p.reshape(-1)
        out, off = [], 0
        for a, n in zip(small_w, sizes):
            out.append(f[off:off + n].reshape(a.shape))
            off += n
        return out

    g_small = unpack(g_small_pack)
    d_small, m_small, v_small = [unpack(p) for p in upd_small]
    d_meta, nm_meta, nv_meta = _adamw("adam_meta", meta_tokens, g_meta_mine, m_meta_tokens, v_meta_tokens)

    big_m = [m_w_in, m_w_q_up, m_w_kv_up, m_w_o, m_w_gate, m_w_up, m_w_down]
    big_v = [v_w_in, v_w_q_up, v_w_kv_up, v_w_o, v_w_gate, v_w_up, v_w_down]
    upd_big = [_adamw("adam_big", w, g, m, v) for w, g, m, v in zip(big, g_big, big_m, big_v)]

    names = ["meta_tokens", "attn_norm", "w_in", "q_norm", "w_q_up", "kv_norm", "w_kv_up", "sinks",
             "out_norm_swa", "out_norm_mla", "w_o", "ffn_norm", "w_gate", "w_up", "w_down", "final_norm"]
    small_idx = {"attn_norm": 0, "q_norm": 1, "kv_norm": 2, "sinks": 3, "out_norm_swa": 4,
                 "out_norm_mla": 5, "ffn_norm": 6, "final_norm": 7}
    big_idx = {"w_in": 0, "w_q_up": 1, "w_kv_up": 2, "w_o": 3, "w_gate": 4, "w_up": 5, "w_down": 6}
    grads, deltas, new_m, new_v = [], [], [], []
    for nme in names:
        if nme == "meta_tokens":
            quad = (g_meta_mine, d_meta, nm_meta, nv_meta)
        elif nme in small_idx:
            i = small_idx[nme]
            quad = (g_small[i], d_small[i], m_small[i], v_small[i])
        else:
            i = big_idx[nme]
            quad = (g_big[i], *upd_big[i])
        grads.append(quad[0]); deltas.append(quad[1]); new_m.append(quad[2]); new_v.append(quad[3])
    return (loss, grad_x, *grads, *deltas, *new_m, *new_v)
```

```python
import jax
import jax.numpy as jnp
from jax import lax
from jax.experimental import pallas as pl
from jax.experimental.pallas import tpu as pltpu

F32 = jnp.float32
BF16 = jnp.bfloat16
MXU_DTYPE = BF16

D_MODEL = 1024
DEPTH = 4
N_META = 16
BLOCK = 128
WINDOW = 128
ROPE_THETA = 10000.0
EPS = 1e-6
NEG = -1e30
SWA_HEADS = 8
SWA_KV_HEADS = 2
SWA_HEAD_DIM = 64
MLA_HEADS = 8
MLA_Q_RANK = 256
MLA_KV_RANK = 128
MLA_NOPE_DIM = 64
MLA_ROPE_DIM = 32
MLA_V_DIM = 64
MLA_QK_DIM = MLA_NOPE_DIM + MLA_ROPE_DIM
D_FF = 2816
FRONT = (-N_META) % BLOCK
LANES = 128
SLOT_W = 8 * LANES
C_QA, C_KA, C_VA, C_QL, C_KL, C_KR, IN_WP = 0, 512, 640, 768, 1024, 1152, 1280
KR_LANE = 64
IN_W = 1184

ADAM_LR, ADAM_B1, ADAM_B2, ADAM_EPS, ADAM_WD, ADAM_STEP = 0.001, 0.9, 0.999, 1e-08, 0.01, 10

VMEM_LIMIT = 48 * 1024 * 1024
MESH = pl.DeviceIdType.MESH


def _tile(n, prefs):
    for t in prefs:
        if n % t == 0:
            return t
    return n


def _params(sem):
    return pltpu.CompilerParams(dimension_semantics=sem, vmem_limit_bytes=VMEM_LIMIT)


_DIMS = {"nn": (((1,), (0,)), ((), ())), "nt": (((1,), (1,)), ((), ())), "tn": (((0,), (0,)), ((), ()))}


def _mm(name, a, b, mode, out_dtype=F32, res=None):
    if mode == "nn":
        (M, K), (_, N) = a.shape, b.shape
    elif mode == "nt":
        (M, K), (N, _) = a.shape, b.shape
    else:
        (K, M), (_, N) = a.shape, b.shape
    lane_tiles = (1408, 1024, 640, 768, 512, 384, 256, 128)
    row_tiles = (1056, 528, 512, 384, 256, 128)
    bm = _tile(M, lane_tiles if mode == "tn" else row_tiles)
    bn = _tile(N, lane_tiles)
    bk = _tile(K, row_tiles if mode == "tn" else lane_tiles)
    nk = K // bk
    if mode == "tn":
        a_spec = pl.BlockSpec((bk, bm), lambda i, j, k: (k, i))
    else:
        a_spec = pl.BlockSpec((bm, bk), lambda i, j, k: (i, k))
    if mode == "nt":
        b_spec = pl.BlockSpec((bn, bk), lambda i, j, k: (j, k))
    else:
        b_spec = pl.BlockSpec((bk, bn), lambda i, j, k: (k, j))
    o_spec = pl.BlockSpec((bm, bn), lambda i, j, k: (i, j))
    in_specs = [a_spec, b_spec]
    args = [a, b]
    if res is not None:
        in_specs.append(o_spec)
        args.append(res)
    dims = _DIMS[mode]

    def kern(a_ref, b_ref, *rest):
        if res is not None:
            r_ref, o_ref = rest[0], rest[1]
            scr = rest[2:]
        else:
            r_ref, o_ref = None, rest[0]
            scr = rest[1:]
        p = lax.dot_general(a_ref[...].astype(MXU_DTYPE), b_ref[...].astype(MXU_DTYPE), dims,
                            preferred_element_type=F32)

        def finish(val):
            if r_ref is not None:
                val = val + r_ref[...]
            o_ref[...] = val.astype(o_ref.dtype)

        if nk == 1:
            finish(p)
        else:
            acc = scr[0]
            k = pl.program_id(2)

            @pl.when(k == 0)
            def _():
                acc[...] = p

            @pl.when(k > 0)
            def _():
                acc[...] += p

            @pl.when(k == nk - 1)
            def _():
                finish(acc[...])

    return pl.pallas_call(
        kern, name=name,
        out_shape=jax.ShapeDtypeStruct((M, N), out_dtype),
        grid=(M // bm, N // bn, nk),
        in_specs=in_specs, out_specs=o_spec,
        scratch_shapes=[pltpu.VMEM((bm, bn), F32)] if nk > 1 else [],
        compiler_params=_params(("parallel", "parallel", "arbitrary")),
    )(*args)


def _rowmap(name, body, rows, vecs, outs, accs=(), tr_prefs=(384, 256, 128)):
    R = rows[0].shape[0]
    tr = _tile(R, tr_prefs)
    n_r, n_v, n_o, n_a = len(rows), len(vecs), len(outs), len(accs)

    def kern(*refs):
        ins = [r[...] for r in refs[:n_r + n_v]]
        o_refs = refs[n_r + n_v:n_r + n_v + n_o]
        a_refs = refs[n_r + n_v + n_o:]
        res = body(*ins)
        for o_ref, val in zip(o_refs, res[:n_o]):
            o_ref[...] = val.astype(o_ref.dtype)
        if n_a:
            first = pl.program_id(0) == 0

            @pl.when(first)
            def _():
                for a_ref, val in zip(a_refs, res[n_o:]):
                    a_ref[...] = val

            @pl.when(jnp.logical_not(first))
            def _():
                for a_ref, val in zip(a_refs, res[n_o:]):
                    a_ref[...] += val

    in_specs = [pl.BlockSpec((tr, r.shape[1]), lambda i: (i, 0)) for r in rows]
    in_specs += [pl.BlockSpec((1, v.shape[1]), lambda i: (0, 0)) for v in vecs]
    out_specs = [pl.BlockSpec((tr, c), lambda i: (i, 0)) for c, _ in outs]
    out_specs += [pl.BlockSpec((1, c), lambda i: (0, 0)) for c in accs]
    out_shape = [jax.ShapeDtypeStruct((R, c), dt) for c, dt in outs]
    out_shape += [jax.ShapeDtypeStruct((1, c), F32) for c in accs]
    return pl.pallas_call(
        kern, name=name, out_shape=out_shape, grid=(R // tr,),
        in_specs=in_specs, out_specs=out_specs,
        compiler_params=_params(("arbitrary",) if n_a else ("parallel",)),
    )(*rows, *vecs)


def _lane(shape):
    return lax.broadcasted_iota(jnp.int32, shape, 1)


def _rot_swa(x):
    lane = _lane(x.shape)
    return jnp.where((lane & 63) < 32, -pltpu.roll(x, 96, 1), pltpu.roll(x, 32, 1))


def _rot_mla(x):
    lane = _lane(x.shape)
    lo = jnp.where(lane >= KR_LANE, -pltpu.roll(x, 112, 1), 0.0)
    hi = jnp.where(lane < KR_LANE + MLA_ROPE_DIM, pltpu.roll(x, 16, 1), 0.0)
    return jnp.where(lane < KR_LANE + 16, lo, hi)


def _rope(x, cos, sin, rot):
    return x * cos + rot(x) * sin


def _rope_t(g, cos, sin, rot):
    return g * cos - rot(g * sin)


def _low(x):
    return jnp.where(_lane(x.shape) < 64, x, 0.0)


def _value_slot(x):
    lane = _lane(x.shape)
    return jnp.where(lane < 64, x, jnp.where(lane == 64, 1.0, 0.0))


def _blk(x, j):
    return x[:, j * LANES:(j + 1) * LANES]


def _rms_r(x, width):
    return lax.rsqrt(jnp.sum(x * x, axis=-1, keepdims=True) * (1.0 / width) + EPS)


def _rms_bwd(x, g, dy, width):
    r = _rms_r(x, width)
    gdy = dy * g
    dot = jnp.sum(gdy * x, axis=-1, keepdims=True)
    dx = r * gdy - x * (r * r * r * (1.0 / width) * dot)
    return dx, dy * x * r


def _colsum(x):
    return jnp.sum(x, axis=0, keepdims=True)


def _rmsnorm_bwd(name, x, g, dy, dres):
    def body(xv, dyv, dr, gv):
        dx, dg = _rms_bwd(xv, gv, dyv, D_MODEL)
        return dx + dr, _colsum(dg)
    return _rowmap(name, body, [x, dy, dres], [g], [(D_MODEL, F32)], [D_MODEL])


def _prep1(proj, qn_g, kv_g, cosa, sina, cosm, sinm):
    def body(p, ca, sa, cm, sm, gq, gk):
        qa = []
        for j in range(4):
            xr = _rope(_blk(p, j), ca, sa, _rot_swa)
            qa += [_low(xr), _low(pltpu.roll(xr, 64, 1))]
        kr_ = _rope(_blk(p, C_KA // LANES), ca, sa, _rot_swa)
        ka = [_low(kr_), _low(pltpu.roll(kr_, 64, 1))]
        vv = _blk(p, C_VA // LANES)
        va = [_value_slot(vv), _value_slot(pltpu.roll(vv, 64, 1))]
        ql = p[:, C_QL:C_QL + MLA_Q_RANK]
        qn = ql * _rms_r(ql, MLA_Q_RANK) * gq
        kl = p[:, C_KL:C_KL + MLA_KV_RANK]
        cn = kl * _rms_r(kl, MLA_KV_RANK) * gk
        kr = _rope(_blk(p, C_KR // LANES), cm, sm, _rot_mla)
        return (jnp.concatenate(qa, 1), jnp.concatenate(ka, 1), jnp.concatenate(va, 1), qn, cn, kr)
    return _rowmap("prep1", body, [proj, cosa, sina, cosm, sinm], [qn_g, kv_g],
                   [(SLOT_W, BF16), (2 * LANES, BF16), (2 * LANES, BF16),
                    (MLA_Q_RANK, BF16), (MLA_KV_RANK, BF16), (LANES, F32)])


def _prep2(qb, kvb, kr, cosm, sinm):
    def body(q, kv, krv, cm, sm):
        qs, ks, vs = [], [], []
        for h in range(MLA_HEADS):
            qs.append(_rope(_blk(q, h), cm, sm, _rot_mla))
            kvh = _blk(kv, h)
            ks.append(_low(kvh) + krv)
            vs.append(_value_slot(pltpu.roll(kvh, 64, 1)))
        return jnp.concatenate(qs, 1), jnp.concatenate(ks, 1), jnp.concatenate(vs, 1)
    return _rowmap("prep2", body, [qb, kvb, kr, cosm, sinm], [],
                   [(SLOT_W, BF16), (SLOT_W, BF16), (SLOT_W, BF16)])


def _compact(slots):
    return jnp.concatenate(
        [_blk(slots, 2 * j) + pltpu.roll(_blk(slots, 2 * j + 1), 64, 1) for j in range(4)], 1)


def _expand(nat):
    out = []
    for j in range(4):
        b = _blk(nat, j)
        out += [_low(b), _low(pltpu.roll(b, 64, 1))]
    return jnp.concatenate(out, 1)


def _merge_fwd(oa, ob, ga, gb):
    def body(a, b, gav, gbv):
        xa, xb = _compact(a), _compact(b)
        return (jnp.concatenate([xa * _rms_r(xa, 512) * gav, xb * _rms_r(xb, 512) * gbv], 1),)
    return _rowmap("merge_fwd", body, [oa, ob], [ga, gb], [(D_MODEL, BF16)])[0]


def _merge_bwd(dmix, oa, ob, lse_a, ga, gb, sink_slots):
    def body(dm, a, b, lse, gav, gbv, sk):
        outs = []
        accs = []
        for o, g, lo in ((a, gav, 0), (b, gbv, 512)):
            x = _compact(o)
            dx, dg = _rms_bwd(x, g, dm[:, lo:lo + 512], 512)
            do = _expand(dx)
            delta = jnp.concatenate(
                [jnp.broadcast_to(jnp.sum(_blk(do, h) * _blk(o, h), axis=-1, keepdims=True),
                                  (do.shape[0], LANES)) for h in range(8)], 1)
            outs += [do, delta]
            accs.append(_colsum(dg))
        dsink = _colsum(-jnp.exp(sk - lse) * outs[1])
        return (*outs, *accs, dsink)
    return _rowmap("merge_bwd", body, [dmix, oa, ob, lse_a], [ga, gb, sink_slots],
                   [(SLOT_W, BF16), (SLOT_W, F32), (SLOT_W, BF16), (SLOT_W, F32)],
                   [512, 512, SLOT_W])


def _prep2_bwd(dq, dk, dv, cosm, sinm):
    def body(dqv, dkv, dvv, cm, sm):
        dqv, dkv, dvv = dqv.astype(F32), dkv.astype(F32), dvv.astype(F32)
        dqb, dkvb = [], []
        krsum = jnp.zeros((dqv.shape[0], LANES), F32)
        for h in range(MLA_HEADS):
            dqb.append(_rope_t(_blk(dqv, h), cm, sm, _rot_mla))
            dkh = _blk(dkv, h)
            dkvb.append(_low(dkh) + pltpu.roll(_blk(dvv, h), 64, 1))
            krsum = krsum + dkh
        lane = _lane(krsum.shape)
        dkr = jnp.where((lane >= KR_LANE) & (lane < KR_LANE + MLA_ROPE_DIM),
                        _rope_t(krsum, cm, sm, _rot_mla), 0.0)
        return jnp.concatenate(dqb, 1), jnp.concatenate(dkvb, 1), dkr
    return _rowmap("prep2_bwd", body, [dq, dk, dv, cosm, sinm], [],
                   [(SLOT_W, BF16), (SLOT_W, BF16), (LANES, F32)])


def _prep1_bwd(proj, dqa, dka, dva, dqn, dcn, dkr, cosa, sina, qn_g, kv_g):
    def body(p, dq, dk, dv, dqnv, dcnv, dkrv, ca, sa, gq, gk):
        dq, dk, dv = dq.astype(F32), dk.astype(F32), dv.astype(F32)
        cols = []
        for j in range(4):
            nat = _blk(dq, 2 * j) + pltpu.roll(_blk(dq, 2 * j + 1), 64, 1)
            cols.append(_rope_t(nat, ca, sa, _rot_swa))
        grp = lambda d, g: sum(_blk(d, 4 * g + i) for i in range(4))
        cols.append(_rope_t(grp(dk, 0) + pltpu.roll(grp(dk, 1), 64, 1), ca, sa, _rot_swa))
        cols.append(grp(dv, 0) + pltpu.roll(grp(dv, 1), 64, 1))
        dql, dgq = _rms_bwd(p[:, C_QL:C_QL + MLA_Q_RANK], gq, dqnv, MLA_Q_RANK)
        dkl, dgk = _rms_bwd(p[:, C_KL:C_KL + MLA_KV_RANK], gk, dcnv, MLA_KV_RANK)
        cols += [dql, dkl, dkrv]
        return jnp.concatenate(cols, 1), _colsum(dgq), _colsum(dgk)
    return _rowmap("prep1_bwd", body, [proj, dqa, dka, dva, dqn, dcn, dkr, cosa, sina], [qn_g, kv_g],
                   [(IN_WP, BF16)], [MLA_Q_RANK, MLA_KV_RANK], tr_prefs=(192, 128))


def _sigmoid(x):
    return 1.0 / (1.0 + jnp.exp(-x))


def _ffn_tiles(T, F):
    return _tile(T, (528, 512, 384, 256, 128)), _tile(F, (1408, 1024, 640, 512, 256, 128))


def _norm_proj(name, h, g, weights, swiglu):
    (T, D), F = h.shape, weights[0].shape[1]
    bm, bn = _ffn_tiles(T, F)
    nw = len(weights)

    def kern(h_ref, g_ref, *rest):
        w_refs, u_ref, o_refs, u_scr = rest[:nw], rest[nw], rest[nw + 1:-1], rest[-1]

        @pl.when(pl.program_id(1) == 0)
        def _():
            x = h_ref[...]
            u = (x * _rms_r(x, D) * g_ref[...]).astype(u_scr.dtype)
            u_scr[...] = u
            u_ref[...] = u

        uv = u_scr[...]
        prods = [jnp.dot(uv, w[...], preferred_element_type=F32) for w in w_refs]
        for o_ref, p in zip(o_refs, prods):
            o_ref[...] = p
        if swiglu:
            a, b = prods
            o_refs[nw][...] = (a * _sigmoid(a) * b).astype(o_refs[nw].dtype)

    w_spec = pl.BlockSpec((D, bn), lambda i, j: (0, j))
    row_spec = pl.BlockSpec((bm, D), lambda i, j: (i, 0))
    o_spec = pl.BlockSpec((bm, bn), lambda i, j: (i, j))
    n_out = nw + (1 if swiglu else 0)
    return pl.pallas_call(
        kern, name=name,
        out_shape=[jax.ShapeDtypeStruct((T, D), MXU_DTYPE)] + [jax.ShapeDtypeStruct((T, F), F32)] * nw
        + ([jax.ShapeDtypeStruct((T, F), MXU_DTYPE)] if swiglu else []),
        grid=(T // bm, F // bn),
        in_specs=[row_spec, pl.BlockSpec((1, D), lambda i, j: (0, 0))] + [w_spec] * nw,
        out_specs=[row_spec] + [o_spec] * n_out,
        scratch_shapes=[pltpu.VMEM((bm, D), MXU_DTYPE)],
        compiler_params=_params(("parallel", "arbitrary")),
    )(h, g, *weights)


def _ffn_mid_bwd(dh, w_down, a, b):
    (T, D), F = dh.shape, w_down.shape[0]
    bm, bn = _ffn_tiles(T, F)

    def kern(dh_ref, wd_ref, a_ref, b_ref, da_ref, db_ref):
        d = lax.dot_general(dh_ref[...].astype(MXU_DTYPE), wd_ref[...], _DIMS["nt"], preferred_element_type=F32)
        av, bv = a_ref[...], b_ref[...]
        s = _sigmoid(av)
        da_ref[...] = (d * bv * (s * (1.0 + av * (1.0 - s)))).astype(da_ref.dtype)
        db_ref[...] = (d * (av * s)).astype(db_ref.dtype)

    o_spec = pl.BlockSpec((bm, bn), lambda i, j: (i, j))
    return pl.pallas_call(
        kern, name="ffn_mid_bwd",
        out_shape=[jax.ShapeDtypeStruct((T, F), MXU_DTYPE)] * 2,
        grid=(T // bm, F // bn),
        in_specs=[pl.BlockSpec((bm, D), lambda i, j: (i, 0)), pl.BlockSpec((bn, D), lambda i, j: (j, 0)),
                  o_spec, o_spec],
        out_specs=[o_spec, o_spec],
        compiler_params=_params(("parallel", "parallel")),
    )(dh, w_down, a, b)


def _loss_head(h, target, g):
    T = h.shape[0]
    nb = T // BLOCK

    def kern(h_ref, t_ref, g_ref, dh_ref, dg_ref, loss_ref, acc):
        i = pl.program_id(0)

        @pl.when(i == 0)
        def _():
            dh_ref[...] = jnp.zeros_like(dh_ref)
            dg_ref[...] = jnp.zeros_like(dg_ref)
            acc[...] = jnp.zeros_like(acc)

        @pl.when(i > 0)
        def _():
            x = h_ref[...]
            gv = g_ref[...]
            e = x * _rms_r(x, D_MODEL) * gv - t_ref[...]
            acc[...] += _colsum(e * e)
            dx, dg = _rms_bwd(x, gv, e * (1.0 / D_MODEL), D_MODEL)
            dh_ref[...] = dx
            dg_ref[...] += _colsum(dg)

        @pl.when(i == nb - 1)
        def _():
            tot = jnp.sum(acc[...], axis=-1, keepdims=True) * (0.5 / D_MODEL)
            loss_ref[...] = jnp.broadcast_to(tot, loss_ref.shape)

    return pl.pallas_call(
        kern, name="loss_head",
        out_shape=[jax.ShapeDtypeStruct((T, D_MODEL), F32), jax.ShapeDtypeStruct((1, D_MODEL), F32),
                   jax.ShapeDtypeStruct((1, LANES), F32)],
        grid=(nb,),
        in_specs=[pl.BlockSpec((BLOCK, D_MODEL), lambda i: (i, 0)),
                  pl.BlockSpec((BLOCK, D_MODEL), lambda i: (jnp.maximum(i - 1, 0), 0)),
                  pl.BlockSpec((1, D_MODEL), lambda i: (0, 0))],
        out_specs=[pl.BlockSpec((BLOCK, D_MODEL), lambda i: (i, 0)),
                   pl.BlockSpec((1, D_MODEL), lambda i: (0, 0)),
                   pl.BlockSpec((1, LANES), lambda i: (0, 0))],
        scratch_shapes=[pltpu.VMEM((1, D_MODEL), F32)],
        compiler_params=_params(("arbitrary",)),
    )(h, target, g)


LOG2E = 1.4426950408889634


def _attn_plan(T, causal):
    tq = _tile(T, (384, 256, 128))
    ck = min(2 * tq, T) if causal else min(tq + WINDOW, T)
    return tq, ck, (-(-T // ck) if causal else 1)


def _chunk(i, c, T, tq, ck, causal):
    if causal:
        return pl.multiple_of(jnp.minimum(c * ck, T - ck), LANES), c * ck
    return pl.multiple_of(jnp.clip(i * tq - WINDOW, 0, T - ck), LANES), 0


def _n_chunks(i, tq, ck, causal):
    return ((i + 1) * tq + ck - 1) // ck if causal else 1


def _mask(s, i, start, first, tq, ck, causal):
    qpos = i * tq + lax.broadcasted_iota(jnp.int32, (tq, 1), 0)
    kpos = start + lax.broadcasted_iota(jnp.int32, (tq, ck), 1)
    low = jnp.maximum(jnp.where(qpos < FRONT, 0, FRONT), first)
    if not causal:
        low = jnp.maximum(low, qpos - (WINDOW - 1))
    return jnp.where(kpos >= low, jnp.where(kpos <= qpos, s, NEG), NEG)


def _chunk_loop(n, body, init, several):
    carry = body(0, init, True)
    if not several:
        return carry
    carry = body(n - 1, carry, True)
    return lax.fori_loop(1, n - 1, lambda c, cr: body(c, cr, False), carry)


def _tile_sweep(nq, tq, ck, causal, q_tiles, pair=False):
    if pair and nq >= 2 and (not causal or ck == 2 * tq):
        if causal:
            q_tiles([0, 1], False)
            lax.fori_loop(1, nq // 2, lambda p, cr: q_tiles([2 * p, 2 * p + 1], True) or cr, 0)
        else:
            lax.fori_loop(0, nq // 2, lambda p, cr: q_tiles([2 * p, 2 * p + 1], False) or cr, 0)
        if nq % 2:
            q_tiles([nq - 1], causal and nq >= 3)
        return
    one = min(nq, ck // tq) if causal else nq
    lax.fori_loop(0, one, lambda i, cr: q_tiles([i], False) or cr, 0)
    if one < nq:
        lax.fori_loop(one, nq, lambda i, cr: q_tiles([i], True) or cr, 0)


BWD_HEADS_PER_STEP = 1


def _head_cols(group, hp):
    q_cols = lambda hh: slice(hh * LANES, (hh + 1) * LANES)
    if group == 1:
        return q_cols, q_cols, hp * LANES
    assert group % hp == 0
    return q_cols, (lambda hh: slice(0, LANES)), LANES


def _whole(T, width, index, single):
    if single:
        return pl.BlockSpec((T, width), index, pipeline_mode=pl.Buffered(1))
    return pl.BlockSpec((T, width), index)


def _attn_fwd(name, q, k, v, sinks, group, causal, scale, hp):
    T = q.shape[0]
    H = q.shape[1] // LANES
    HP = hp
    tq, ck, slots = _attn_plan(T, causal)
    nq = T // tq
    c2 = scale * LOG2E
    q_cols, k_cols, kw = _head_cols(group, HP)

    def kern(sink_ref, q_ref, k_ref, v_ref, o_ref, lse_ref, s_scr):
        sink2 = [sink_ref[pl.program_id(0) * HP + hh] * LOG2E for hh in range(HP)]

        def q_tiles(tiles, several):
            rows = [pl.ds(pl.multiple_of(i * tq, tq), tq) for i in tiles]
            chains = [(t, hh) for t in range(len(tiles)) for hh in range(HP)]
            qq = [q_ref[rows[t], q_cols(hh)] for t, hh in chains]
            n = _n_chunks(tiles[0], tq, ck, causal)

            def score(c, m2, masked):
                out = []
                for n_ch, (t, hh) in enumerate(chains):
                    start, first = _chunk(tiles[t], c, T, tq, ck, causal)
                    s = lax.dot_general(qq[n_ch], k_ref[pl.ds(start, ck), k_cols(hh)], _DIMS["nt"],
                                        preferred_element_type=F32) * c2
                    if masked:
                        s = _mask(s, tiles[t], start, first, tq, ck, causal)
                    s_scr[n_ch, c] = s
                    out.append(jnp.maximum(m2[n_ch], jnp.max(s, axis=-1, keepdims=True)))
                return tuple(out)

            m2 = _chunk_loop(n, score, tuple(jnp.full((tq, 1), sink2[hh], F32) for _, hh in chains), several)

            def weigh(c, acc):
                out = []
                for n_ch, (t, hh) in enumerate(chains):
                    start, _ = _chunk(tiles[t], c, T, tq, ck, causal)
                    p = jnp.exp2(s_scr[n_ch, c] - m2[n_ch])
                    out.append(acc[n_ch] + jnp.dot(p.astype(MXU_DTYPE), v_ref[pl.ds(start, ck), k_cols(hh)],
                                                   preferred_element_type=F32))
                return tuple(out)

            acc = lax.fori_loop(0, n, weigh, tuple(jnp.zeros((tq, LANES), F32) for _ in chains))
            lane = _lane((tq, LANES))
            for n_ch, (t, hh) in enumerate(chains):
                l = acc[n_ch][:, 64:65] + jnp.exp2(sink2[hh] - m2[n_ch])
                o_ref[rows[t], q_cols(hh)] = jnp.where(lane < 64, acc[n_ch] / l, 0.0)
                lse_ref[rows[t], q_cols(hh)] = jnp.broadcast_to(m2[n_ch] * (1.0 / LOG2E) + jnp.log(l), (tq, LANES))

        _tile_sweep(nq, tq, ck, causal, q_tiles, pair=True)

    q_spec = _whole(T, HP * LANES, lambda g: (0, g), HP > 1)
    kv_spec = _whole(T, kw, (lambda g: (0, g)) if group == 1 else (lambda g: (0, (g * HP) // group)), HP > 1)
    return pl.pallas_call(
        kern, name=name,
        out_shape=[jax.ShapeDtypeStruct((T, H * LANES), F32)] * 2,
        grid=(H // HP,),
        in_specs=[pl.BlockSpec(memory_space=pltpu.SMEM), q_spec, kv_spec, kv_spec],
        out_specs=[q_spec, q_spec],
        scratch_shapes=[pltpu.VMEM((2 * HP, slots, tq, ck), F32)],
        compiler_params=_params(("parallel",)),
    )(sinks, q, k, v)


def _attn_bwd(name, q, k, v, do, lse, delta, group, causal, scale):
    T = q.shape[0]
    H = q.shape[1] // LANES
    HP = BWD_HEADS_PER_STEP
    tq, ck, _ = _attn_plan(T, causal)
    nq = T // tq
    c2 = scale * LOG2E
    q_cols, k_cols, kw = _head_cols(group, HP)

    def kern(q_ref, k_ref, v_ref, do_ref, lse_ref, dl_ref, dq_ref, dk_ref, dv_ref, dk_acc, dv_acc):
        dk_acc[...] = jnp.zeros_like(dk_acc)
        dv_acc[...] = jnp.zeros_like(dv_acc)

        def q_tiles(tiles, several):
            rows = [pl.ds(pl.multiple_of(i * tq, tq), tq) for i in tiles]
            chains = [(t, hh) for t in range(len(tiles)) for hh in range(HP)]
            qq = [q_ref[rows[t], q_cols(hh)] for t, hh in chains]
            dd = [do_ref[rows[t], q_cols(hh)] for t, hh in chains]
            lse2 = [lse_ref[rows[t], q_cols(hh)][:, 0:1] * LOG2E for t, hh in chains]
            dl_c = [dl_ref[rows[t], q_cols(hh)][:, 0:1] for t, hh in chains]

            def chunk(c, dq, masked):
                out = []
                for n_ch, (t, hh) in enumerate(chains):
                    start, first = _chunk(tiles[t], c, T, tq, ck, causal)
                    keys = pl.ds(start, ck)
                    kk, vv = k_ref[keys, k_cols(hh)], v_ref[keys, k_cols(hh)]
                    s = lax.dot_general(qq[n_ch], kk, _DIMS["nt"], preferred_element_type=F32)
                    if masked:
                        s = _mask(s, tiles[t], start, first, tq, ck, causal)
                    p = jnp.exp2(s * c2 - lse2[n_ch])
                    dv_acc[keys, q_cols(hh)] += lax.dot_general(p.astype(MXU_DTYPE), dd[n_ch], _DIMS["tn"],
                                                                preferred_element_type=F32)
                    dp = lax.dot_general(dd[n_ch], vv, _DIMS["nt"], preferred_element_type=F32)
                    ds = (p * (dp - dl_c[n_ch])).astype(MXU_DTYPE)
                    dk_acc[keys, q_cols(hh)] += lax.dot_general(ds, qq[n_ch], _DIMS["tn"],
                                                                preferred_element_type=F32) * scale
                    out.append(dq[n_ch] + jnp.dot(ds, kk, preferred_element_type=F32))
                return tuple(out)

            dq = _chunk_loop(_n_chunks(tiles[0], tq, ck, causal), chunk,
                             tuple(jnp.zeros((tq, LANES), F32) for _ in chains), several)
            for n_ch, (t, hh) in enumerate(chains):
                dq_ref[rows[t], q_cols(hh)] = (dq[n_ch] * scale).astype(dq_ref.dtype)

        _tile_sweep(nq, tq, ck, causal, q_tiles, pair=True)
        dk_ref[...] = dk_acc[...].astype(dk_ref.dtype)
        dv_ref[...] = dv_acc[...].astype(dv_ref.dtype)

    q_spec = _whole(T, HP * LANES, lambda g: (0, g), False)
    kv_spec = _whole(T, kw, (lambda g: (0, g)) if group == 1 else (lambda g: (0, (g * HP) // group)), False)
    return pl.pallas_call(
        kern, name=name,
        out_shape=[jax.ShapeDtypeStruct((T, H * LANES), MXU_DTYPE)] * 3,
        grid=(H // HP,),
        in_specs=[q_spec, kv_spec, kv_spec, q_spec, q_spec, q_spec],
        out_specs=[q_spec, q_spec, q_spec],
        scratch_shapes=[pltpu.VMEM((T, HP * LANES), F32)] * 2,
        compiler_params=_params(("parallel",)),
    )(q, k, v, do, lse, delta)


def _ew(name, fn, ins, out_dtypes):
    shape = ins[0].shape
    flat = [a.reshape(-1, shape[-1]) for a in ins]
    R, C = flat[0].shape
    tr = _tile(R, (512, 256, 128, 64, 32, 16, 8))
    n_in = len(ins)

    def kern(*refs):
        res = fn(*[r[...] for r in refs[:n_in]])
        for o_ref, val in zip(refs[n_in:], res):
            o_ref[...] = val.astype(o_ref.dtype)

    spec = pl.BlockSpec((tr, C), lambda i: (i, 0))
    outs = pl.pallas_call(
        kern, name=name,
        out_shape=[jax.ShapeDtypeStruct((R, C), dt) for dt in out_dtypes],
        grid=(R // tr,), in_specs=[spec] * n_in, out_specs=[spec] * len(out_dtypes),
        compiler_params=_params(("parallel",)),
    )(*flat)
    return [o.reshape(shape) for o in outs]


def _adamw(name, w, g, m, v):
    c1 = 1.0 - ADAM_B1 ** ADAM_STEP
    c2 = 1.0 - ADAM_B2 ** ADAM_STEP

    def fn(wv, gv, mv, vv):
        mn = ADAM_B1 * mv + (1.0 - ADAM_B1) * gv
        vn = ADAM_B2 * vv + (1.0 - ADAM_B2) * (gv * gv)
        delta = -ADAM_LR * ((mn / c1) / (jnp.sqrt(vn / c2) + ADAM_EPS) + ADAM_WD * wv)
        return delta, mn, vn

    return _ew(name, fn, [w, g, m, v], [F32, F32, F32])


_ANY = pl.BlockSpec(memory_space=pl.ANY)


def _where_am_i():
    x, y, c = lax.axis_index("x"), lax.axis_index("y"), lax.axis_index("c")
    chips = [(1 - x, y), (x, 1 - y), (1 - x, 1 - y)]
    return x, y, c, chips


def _gather_weights(shards, meta):
    arrs = list(shards) + [meta]
    n = len(arrs)
    per = [a.shape[0] // 2 for a in arrs]

    def body(*refs):
        ins, outs = refs[:n], refs[n:2 * n]
        send1, recv1, send2, recv2 = refs[2 * n:]
        x, y, c, chips = _where_am_i()
        me = 2 * x + y

        def half(ref, k, cc):
            return ref.at[pl.ds(per[k] * cc, per[k])]

        first = []
        for k in range(n):
            for j, (cx, cy) in enumerate(chips):
                first.append(pltpu.make_async_remote_copy(
                    src_ref=half(ins[k], k, c), dst_ref=half(outs[k].at[me], k, c),
                    send_sem=send1.at[k, j], recv_sem=recv1.at[k, j],
                    device_id=(cx, cy, c), device_id_type=MESH))
        for cp in first:
            cp.start()
        passed = []
        for k in range(n):
            for j, (cx, cy) in enumerate(chips):
                landed = half(outs[k].at[2 * cx + cy], k, c)
                pltpu.make_async_remote_copy(
                    src_ref=landed, dst_ref=landed, send_sem=send1.at[k, j], recv_sem=recv1.at[k, j],
                    device_id=(cx, cy, c), device_id_type=MESH).wait_recv()
                fwd = pltpu.make_async_remote_copy(
                    src_ref=landed, dst_ref=landed, send_sem=send2.at[k, j], recv_sem=recv2.at[k, j],
                    device_id=(x, y, 1 - c), device_id_type=MESH)
                fwd.start()
                passed.append(fwd)
        for k in range(n):
            for j, (cx, cy) in enumerate(chips):
                other = half(outs[k].at[2 * cx + cy], k, 1 - c)
                pltpu.make_async_remote_copy(
                    src_ref=other, dst_ref=other, send_sem=send2.at[k, j], recv_sem=recv2.at[k, j],
                    device_id=(x, y, 1 - c), device_id_type=MESH).wait_recv()
        for cp in first + passed:
            cp.wait_send()

    return pl.pallas_call(
        body, name="gather_weights",
        out_shape=[jax.ShapeDtypeStruct((4,) + a.shape, a.dtype) for a in arrs],
        in_specs=[_ANY] * n, out_specs=[_ANY] * n,
        scratch_shapes=[pltpu.SemaphoreType.DMA((n, 3))] * 4,
    )(*arrs)


_HBM = pl.BlockSpec(memory_space=pltpu.HBM)
_SEM = pl.BlockSpec(memory_space=pltpu.SEMAPHORE)
_FLOWS = pltpu.SideEffectType.DATAFLOW_SIDE_EFFECTING


def _in_hbm(a):
    return pltpu.with_memory_space_constraint(a, pltpu.HBM)


def _gather_start(shards):
    n = len(shards)
    nl = shards[0].shape[0]
    lands = [lax.empty((4,) + sh.shape[1:], sh.dtype) for _ in range(nl) for sh in shards]

    def body(*refs):
        srcs, land = refs[:n], refs[n:n + n * nl]
        send, recv, token = refs[n + n * nl:n + n * nl + 3]
        x, y, c, chips = _where_am_i()
        me = 2 * x + y
        for l in range(nl):
            for k in range(n):
                for cx, cy in chips:
                    pltpu.make_async_remote_copy(
                        src_ref=srcs[k].at[l], dst_ref=land[l * n + k].at[me],
                        send_sem=send.at[l], recv_sem=recv.at[l],
                        device_id=(cx, cy, c), device_id_type=MESH).start()
        token[...] = jnp.zeros_like(token)

    ins = [_in_hbm(a) for a in list(shards) + lands]
    res = pl.pallas_call(
        body, name="gather_start",
        out_shape=(pltpu.SemaphoreType.DMA((nl,)), pltpu.SemaphoreType.DMA((nl,)),
                   jax.ShapeDtypeStruct((8, LANES), F32)) + tuple(pltpu.HBM(a.shape, a.dtype) for a in ins),
        in_specs=[_HBM] * len(ins),
        out_specs=(_SEM, _SEM, pl.BlockSpec(memory_space=pltpu.VMEM)) + (_HBM,) * len(ins),
        input_output_aliases={i: 3 + i for i in range(len(ins))},
        compiler_params=pltpu.CompilerParams(has_side_effects=_FLOWS),
    )(*ins)
    send, recv, token = res[:3]
    thru = res[3:3 + n]
    lands = res[3 + n:]
    return send, recv, token, list(thru), [list(lands[l * n:(l + 1) * n]) for l in range(nl)]


def _gather_wait(name, send, recv, l, thru, land, after):
    n = len(thru)

    def body(*refs):
        srcs, lands = refs[:n], refs[n:2 * n]
        send_sem, recv_sem = refs[2 * n:2 * n + 2]
        x, y, c, chips = _where_am_i()
        for k in range(n):
            for cx, cy in chips:
                copy = pltpu.make_async_remote_copy(
                    src_ref=srcs[k].at[l], dst_ref=lands[k].at[2 * cx + cy],
                    send_sem=send_sem.at[l], recv_sem=recv_sem.at[l],
                    device_id=(cx, cy, c), device_id_type=MESH)
                copy.wait_send()
                copy.wait_recv()

    ins = list(thru) + list(land)
    res = pl.pallas_call(
        body, name=name,
        out_shape=tuple(pltpu.HBM(a.shape, a.dtype) for a in ins),
        in_specs=[_HBM] * len(ins) + [_SEM, _SEM, _ANY],
        out_specs=(_HBM,) * len(ins),
        input_output_aliases={i: i for i in range(len(ins))},
        compiler_params=pltpu.CompilerParams(has_side_effects=_FLOWS),
    )(*ins, send, recv, after)
    return list(res[:n]), list(res[n:])


def _row_chunks(r):
    n = 4 if r % 64 == 0 else 1
    return [(q * (r // n), r // n) for q in range(n)]


def _pair_exchange(grads):
    n = len(grads)

    def body(*refs):
        ins, got = refs[:n], refs[n:2 * n]
        send, recv = refs[2 * n:]
        x, y, c, _ = _where_am_i()
        sib = (x, y, 1 - c)
        for k in range(n):
            for ch in range(4):
                for l in range(2):
                    pltpu.make_async_remote_copy(
                        src_ref=ins[k].at[ch, 2 * (1 - c) + l], dst_ref=got[k].at[ch, l],
                        send_sem=send.at[k], recv_sem=recv.at[k], device_id=sib, device_id_type=MESH).start()
        for k in range(n):
            pltpu.make_async_remote_copy(
                src_ref=got[k], dst_ref=got[k], send_sem=send.at[k], recv_sem=recv.at[k],
                device_id=sib, device_id_type=MESH).wait()

    return pl.pallas_call(
        body, name="reduce_pair",
        out_shape=[jax.ShapeDtypeStruct((4, 2) + g.shape[2:], g.dtype) for g in grads],
        in_specs=[_ANY] * n, out_specs=[_ANY] * n,
        scratch_shapes=[pltpu.SemaphoreType.DMA((n,))] * 2,
    )(*grads)


def _pair_add(full, got, c):
    _, _, r, cols = full.shape
    tr = _tile(r, (512, 256, 352, 128))

    def kern(c_ref, a_ref, b_ref, o32_ref, o16_ref):
        tot = a_ref[...] + b_ref[...].astype(F32)
        o32_ref[...] = tot
        o16_ref[...] = tot.astype(o16_ref.dtype)

    blk = (None, None, tr, cols)
    mine = pl.BlockSpec(blk, lambda ch, l, i, cr: (ch, 2 * cr[0] + l, i, 0))
    same = pl.BlockSpec(blk, lambda ch, l, i, cr: (ch, l, i, 0))
    return pl.pallas_call(
        kern, name="pair_add",
        out_shape=[jax.ShapeDtypeStruct(got.shape, F32), jax.ShapeDtypeStruct(got.shape, got.dtype)],
        grid_spec=pltpu.PrefetchScalarGridSpec(
            num_scalar_prefetch=1, grid=(4, 2, r // tr), in_specs=[mine, same], out_specs=[same, same]),
        compiler_params=_params(("parallel", "parallel", "parallel")),
    )(c.reshape(1), full, got)


def _chip_scatter(parts):
    n = len(parts)

    def body(*refs):
        ins, outs = refs[:n], refs[n:2 * n]
        send, recv = refs[2 * n:]
        x, y, c, chips = _where_am_i()
        me = 2 * x + y
        for k in range(n):
            for j, (cx, cy) in enumerate(chips):
                for l in range(2):
                    pltpu.make_async_remote_copy(
                        src_ref=ins[k].at[2 * cx + cy, l], dst_ref=outs[k].at[me, l],
                        send_sem=send.at[k, j], recv_sem=recv.at[k, j],
                        device_id=(cx, cy, c), device_id_type=MESH).start()
        for k in range(n):
            for j, (cx, cy) in enumerate(chips):
                slot = outs[k].at[2 * cx + cy]
                pltpu.make_async_remote_copy(
                    src_ref=slot, dst_ref=slot, send_sem=send.at[k, j], recv_sem=recv.at[k, j],
                    device_id=(cx, cy, c), device_id_type=MESH).wait()

    return pl.pallas_call(
        body, name="reduce_chips",
        out_shape=[jax.ShapeDtypeStruct(p.shape, p.dtype) for p in parts],
        in_specs=[_ANY] * n, out_specs=[_ANY] * n,
        scratch_shapes=[pltpu.SemaphoreType.DMA((n, 3))] * 2,
    )(*parts)


def _chip_add(landed, mine, me, c):
    _, _, r, cols = landed.shape
    tr = _tile(r, (512, 256, 352, 128))

    def kern(me_ref, c_ref, land_ref, own_ref, o_ref):
        own = own_ref[...]
        tot = None
        for j in range(4):
            term = jnp.where(me_ref[0] == j, own, land_ref[j].astype(F32))
            tot = term if tot is None else tot + term
        o_ref[...] = tot

    return pl.pallas_call(
        kern, name="chip_add",
        out_shape=jax.ShapeDtypeStruct((4, r, cols), F32),
        grid_spec=pltpu.PrefetchScalarGridSpec(
            num_scalar_prefetch=2, grid=(2, r // tr),
            in_specs=[pl.BlockSpec((4, None, tr, cols), lambda l, i, mr, cr: (0, l, i, 0)),
                      pl.BlockSpec((None, None, tr, cols), lambda l, i, mr, cr: (mr[0], l, i, 0))],
            out_specs=pl.BlockSpec((None, tr, cols), lambda l, i, mr, cr: (2 * cr[0] + l, i, 0))),
        compiler_params=_params(("parallel", "parallel")),
    )(me.reshape(1), c.reshape(1), landed, mine)


def _pair_join(sums):
    n = len(sums)

    def body(*refs):
        bufs = refs[n:2 * n]
        send, recv = refs[2 * n:]
        x, y, c, _ = _where_am_i()
        sib = (x, y, 1 - c)
        for k in range(n):
            for l in range(2):
                for r0, rn in _row_chunks(bufs[k].shape[1]):
                    piece = bufs[k].at[2 * c + l, pl.ds(r0, rn)]
                    pltpu.make_async_remote_copy(
                        src_ref=piece, dst_ref=piece, send_sem=send.at[k], recv_sem=recv.at[k],
                        device_id=sib, device_id_type=MESH).start()
        for k in range(n):
            theirs = bufs[k].at[pl.ds(2 * (1 - c), 2)]
            pltpu.make_async_remote_copy(
                src_ref=theirs, dst_ref=theirs, send_sem=send.at[k], recv_sem=recv.at[k],
                device_id=sib, device_id_type=MESH).wait()

    return pl.pallas_call(
        body, name="reduce_join",
        out_shape=[jax.ShapeDtypeStruct(s.shape, s.dtype) for s in sums],
        in_specs=[_ANY] * n, out_specs=[_ANY] * n,
        input_output_aliases={k: k for k in range(n)},
        scratch_shapes=[pltpu.SemaphoreType.DMA((n,))] * 2,
    )(*sums)


def _allreduce_small(buf):
    R = buf.shape[0]

    def body(in_ref, out_ref, land, send, recv):
        x, y, c, _ = _where_am_i()
        me = 4 * x + 2 * y + c
        land[me] = in_ref[...]
        cps = []
        for k in range(1, 8):
            px, py, pc = x ^ (k >> 2), y ^ ((k >> 1) & 1), c ^ (k & 1)
            cps.append(pltpu.make_async_remote_copy(
                src_ref=in_ref, dst_ref=land.at[me], send_sem=send.at[k - 1], recv_sem=recv.at[k - 1],
                device_id=(px, py, pc), device_id_type=MESH))
        for cp in cps:
            cp.start()
        for k in range(1, 8):
            px, py, pc = x ^ (k >> 2), y ^ ((k >> 1) & 1), c ^ (k & 1)
            slot = land.at[4 * px + 2 * py + pc]
            pltpu.make_async_remote_copy(
                src_ref=slot, dst_ref=slot, send_sem=send.at[k - 1], recv_sem=recv.at[k - 1],
                device_id=(px, py, pc), device_id_type=MESH).wait_recv()
        for cp in cps:
            cp.wait_send()
        tot = land[0]
        for d in range(1, 8):
            tot = tot + land[d]
        out_ref[...] = tot

    vm = pl.BlockSpec(memory_space=pltpu.VMEM)
    return pl.pallas_call(
        body, name="allreduce_small",
        out_shape=jax.ShapeDtypeStruct(buf.shape, F32),
        in_specs=[vm], out_specs=vm,
        scratch_shapes=[pltpu.VMEM((8, R, LANES), F32), pltpu.SemaphoreType.DMA((7,)),
                        pltpu.SemaphoreType.DMA((7,))],
    )(buf)


def _rope_tables(T):
    pos = (jnp.arange(T) - FRONT).astype(F32)
    lane = jnp.arange(LANES)
    inv_a = ROPE_THETA ** (-(2 * ((lane % 64) % 32)).astype(F32) / SWA_HEAD_DIM)
    ang_a = pos[:, None] * inv_a[None, :]
    cosa, sina = jnp.cos(ang_a), jnp.sin(ang_a)
    inv_m = ROPE_THETA ** (-(2 * ((lane - KR_LANE) % 16)).astype(F32) / MLA_ROPE_DIM)
    ang_m = pos[:, None] * inv_m[None, :]
    on = ((lane >= KR_LANE) & (lane < KR_LANE + MLA_ROPE_DIM))[None, :]
    cosm = jnp.where(on, jnp.cos(ang_m), 1.0)
    sinm = jnp.where(on, jnp.sin(ang_m), 0.0)
    return cosa, sina, cosm, sinm


def _cols_from_chips(g):
    return jnp.concatenate(g, axis=-1)


def _rows_from_chips(g):
    return jnp.concatenate(g, axis=-2)


def _cols_to_chips(w):
    L, r, c4 = w.shape
    return jnp.moveaxis(w.reshape(L, r, 4, c4 // 4), 2, 0)


def _rows_to_chips(w):
    L, r4, c = w.shape
    return jnp.moveaxis(w.reshape(L, 4, r4 // 4, c), 1, 0)


def _layer_layouts(w_in, w_qup, w_kvup, w_o, w_gate, w_up, w_down):
    zpad = lambda n: jnp.zeros((D_MODEL, n), w_in.dtype)
    w_in = jnp.concatenate([w_in[:, :C_KR], zpad(KR_LANE), w_in[:, C_KR:IN_W],
                            zpad(LANES - KR_LANE - MLA_ROPE_DIM)], axis=-1)
    w_qup = w_qup.reshape(MLA_Q_RANK, MLA_HEADS, MLA_QK_DIM)
    w_qup = jnp.pad(w_qup, ((0, 0), (0, 0), (0, LANES - MLA_QK_DIM))).reshape(MLA_Q_RANK, SLOT_W)
    return w_in, w_qup, w_kvup, w_o, w_gate, w_up, w_down


def _local_step(x2, target, meta_full, layer_weights, p):
    T = BLOCK + x2.shape[0]
    L = DEPTH
    attn_norm, q_norm, kv_norm, sinks = p["attn_norm"], p["q_norm"], p["kv_norm"], p["sinks"]
    out_norm_swa, out_norm_mla, ffn_norm, final_norm = (
        p["out_norm_swa"], p["out_norm_mla"], p["ffn_norm"], p["final_norm"])

    cosa, sina, cosm, sinm = _rope_tables(T)
    no_sink = jnp.full((MLA_HEADS,), NEG, F32)
    scale_a, scale_b = SWA_HEAD_DIM ** -0.5, MLA_QK_DIM ** -0.5
    row = lambda v: v.reshape(1, -1)

    h = jnp.concatenate([jnp.zeros((FRONT, D_MODEL), F32), meta_full, x2], axis=0)
    saved = []
    weights = []
    for l in range(L):
        weights.append(_layer_layouts(*layer_weights(l, h)))
        W_in, W_qup, W_kvup, W_o, W_gate, W_up, W_down = weights[l]
        u, proj = _norm_proj("in_proj", h, row(attn_norm[l]), [W_in], False)
        qa, ka, va, qn, cn, kr = _prep1(proj, row(q_norm[l]), row(kv_norm[l]), cosa, sina, cosm, sinm)
        qb = _mm("q_up", qn, W_qup, "nn")
        kvb = _mm("kv_up", cn, W_kvup, "nn")
        qs, ks, vs = _prep2(qb, kvb, kr, cosm, sinm)
        oa, lse_a = _attn_fwd("swa_fwd", qa, ka, va, sinks[l], 4, False, scale_a, 4)
        ob, lse_b = _attn_fwd("mla_fwd", qs, ks, vs, no_sink, 1, True, scale_b, 1)
        mix = _merge_fwd(oa, ob, row(out_norm_swa[l]), row(out_norm_mla[l]))
        h1 = _mm("o_proj", mix, W_o, "nn", res=h)
        u2, a, b, hm = _norm_proj("ffn_in", h1, row(ffn_norm[l]), [W_gate, W_up], True)
        h2 = _mm("down_proj", hm, W_down, "nn", res=h1)
        saved.append((h, u, proj, qa, ka, va, qn, cn, qs, ks, vs, oa, lse_a, ob, lse_b, mix, h1, u2, a, b, hm))
        h = h2

    dh, d_final, loss_row = _loss_head(h, target, row(final_norm))

    gw = {k: [None] * L for k in ("in", "qup", "kvup", "o", "gate", "up", "down")}
    gs = {k: [None] * L for k in ("attn", "qn", "kvn", "sink", "ga", "gb", "ffn")}
    for l in reversed(range(L)):
        (h0, u, proj, qa, ka, va, qn, cn, qs, ks, vs, oa, lse_a, ob, lse_b, mix, h1, u2, a, b, hm) = saved[l]
        W_in, W_qup, W_kvup, W_o, W_gate, W_up, W_down = weights[l]
        gw["down"][l] = _mm("down_dw", hm, dh, "tn")
        da, db = _ffn_mid_bwd(dh, W_down, a, b)
        gw["gate"][l] = _mm("gate_dw", u2, da, "tn")
        gw["up"][l] = _mm("up_dw", u2, db, "tn")
        du2 = _mm("gate_dx", da, W_gate, "nt")
        du2 = _mm("up_dx", db, W_up, "nt", res=du2)
        dh1, gs["ffn"][l] = _rmsnorm_bwd("ffn_norm_bwd", h1, row(ffn_norm[l]), du2, dh)
        gw["o"][l] = _mm("o_dw", mix, dh1, "tn")
        dmix = _mm("o_dx", dh1, W_o, "nt")
        sink_slots = jnp.repeat(sinks[l], LANES).reshape(1, SLOT_W)
        doa, dla, dob, dlb, gs["ga"][l], gs["gb"][l], dsink = _merge_bwd(
            dmix, oa, ob, lse_a, row(out_norm_swa[l]), row(out_norm_mla[l]), sink_slots)
        gs["sink"][l] = dsink.reshape(SWA_HEADS, LANES)[:, 0]
        dqs, dks, dvs = _attn_bwd("mla_bwd", qs, ks, vs, dob, lse_b, dlb, 1, True, scale_b)
        dqa, dka, dva = _attn_bwd("swa_bwd", qa, ka, va, doa, lse_a, dla, 4, False, scale_a)
        dqb, dkvb, dkr = _prep2_bwd(dqs, dks, dvs, cosm, sinm)
        gw["qup"][l] = _mm("q_up_dw", qn, dqb, "tn")
        gw["kvup"][l] = _mm("kv_up_dw", cn, dkvb, "tn")
        dqn = _mm("q_up_dx", dqb, W_qup, "nt")
        dcn = _mm("kv_up_dx", dkvb, W_kvup, "nt")
        dproj, gs["qn"][l], gs["kvn"][l] = _prep1_bwd(
            proj, dqa, dka, dva, dqn, dcn, dkr, cosa, sina, row(q_norm[l]), row(kv_norm[l]))
        gw["in"][l] = _mm("in_dw", u, dproj, "tn")
        du = _mm("in_dx", dproj, W_in, "nt")
        dh, gs["attn"][l] = _rmsnorm_bwd("attn_norm_bwd", h0, row(attn_norm[l]), du, dh1)

    st = lambda k: jnp.stack(gw[k])
    d_in = st("in")
    d_in = jnp.concatenate([d_in[..., :C_KR], d_in[..., C_KR + KR_LANE:C_KR + KR_LANE + MLA_ROPE_DIM]], axis=-1)
    d_qup = st("qup").reshape(L, MLA_Q_RANK, MLA_HEADS, LANES)[..., :MLA_QK_DIM].reshape(L, MLA_Q_RANK, -1)
    d_nat = [d_in, d_qup, st("kvup"), st("o"), st("gate"), st("up"), st("down")]
    return loss_row, dh, d_nat, gs, d_final


def kernel(x, meta_tokens, attn_norm, w_in, q_norm, w_q_up, kv_norm, w_kv_up, sinks, out_norm_swa, out_norm_mla, w_o, ffn_norm, w_gate, w_up, w_down, final_norm, loss_target, m_meta_tokens, m_attn_norm, m_w_in, m_q_norm, m_w_q_up, m_kv_norm, m_w_kv_up, m_sinks, m_out_norm_swa, m_out_norm_mla, m_w_o, m_ffn_norm, m_w_gate, m_w_up, m_w_down, m_final_norm, v_meta_tokens, v_attn_norm, v_w_in, v_q_norm, v_w_q_up, v_kv_norm, v_w_kv_up, v_sinks, v_out_norm_swa, v_out_norm_mla, v_w_o, v_ffn_norm, v_w_gate, v_w_up, v_w_down, v_final_norm):
    assert x.shape[0] == 1 and x.shape[1] % BLOCK == 0
    big = [w_in, w_q_up, w_kv_up, w_o, w_gate, w_up, w_down]

    c_idx = lax.axis_index("c").astype(jnp.int32)
    chip = (2 * lax.axis_index("x") + lax.axis_index("y")).astype(jnp.int32)
    w16 = [w.astype(BF16) for w in big]
    join = [_cols_from_chips, _cols_from_chips, _cols_from_chips, _rows_from_chips, _cols_from_chips,
            _cols_from_chips, _rows_from_chips]

    def whole(own, landed):
        return [f([jnp.where(chip == j, o, g[j]) for j in range(4)]) for f, o, g in zip(join, own, landed)]

    first = _gather_weights([w[0] for w in w16], meta_tokens)
    meta_full = jnp.concatenate([jnp.where(chip == j, meta_tokens, first[-1][j]) for j in range(4)], axis=-1)
    send, recv, token, thru, lands = _gather_start([w[1:] for w in w16])
    state = {"thru": thru}

    def layer_weights(l, h):
        if l == 0:
            return whole([w[0] for w in w16], first[:-1])
        state["thru"], landed = _gather_wait("gather_wait%d" % l, send, recv, l - 1, state["thru"], lands[l - 1], h)
        return whole([w[l] for w in w16], landed)

    small_p = dict(attn_norm=attn_norm, q_norm=q_norm, kv_norm=kv_norm, sinks=sinks, out_norm_swa=out_norm_swa,
                   out_norm_mla=out_norm_mla, ffn_norm=ffn_norm, final_norm=final_norm)
    loss_row, dh, d_nat, gs, d_final = _local_step(x[0] + token[0, 0], loss_target[0], meta_full, layer_weights, small_p)
    grad_x = dh[BLOCK:][None]

    split = [_cols_to_chips, _cols_to_chips, _cols_to_chips, _rows_to_chips, _cols_to_chips, _cols_to_chips,
             _rows_to_chips]
    full = [f(d) for f, d in zip(split, d_nat)]

    got = _pair_exchange([f.astype(BF16) for f in full])
    sums = [_pair_add(f, g, c_idx) for f, g in zip(full, got)]
    landed = _chip_scatter([s16 for _, s16 in sums])
    g_big = _pair_join([_chip_add(t, s32, chip, c_idx) for t, (s32, _) in zip(landed, sums)])

    small = [jnp.stack(gs["attn"]).reshape(-1), jnp.stack(gs["qn"]).reshape(-1), jnp.stack(gs["kvn"]).reshape(-1),
             jnp.stack(gs["sink"]).reshape(-1), jnp.stack(gs["ga"]).reshape(-1), jnp.stack(gs["gb"]).reshape(-1),
             jnp.stack(gs["ffn"]).reshape(-1), d_final.reshape(-1)]
    sizes = [s.shape[0] for s in small]
    flat = jnp.concatenate(small + [dh[FRONT:BLOCK].reshape(-1), loss_row[0, :1]])
    n_flat = flat.shape[0]
    rows_needed = -(-n_flat // (8 * LANES)) * 8
    flat = jnp.pad(flat, (0, rows_needed * LANES - n_flat)).reshape(rows_needed, LANES)
    tot = _allreduce_small(flat).reshape(-1)
    n_small = sum(sizes)
    loss = tot[n_small + N_META * D_MODEL]
    g_meta_full = tot[n_small:n_small + N_META * D_MODEL].reshape(N_META, D_MODEL)
    g_meta_mine = lax.dynamic_slice_in_dim(g_meta_full, chip * (D_MODEL // 4), D_MODEL // 4, axis=1)

    small_w = [attn_norm, q_norm, kv_norm, sinks, out_norm_swa, out_norm_mla, ffn_norm, final_norm]
    small_m = [m_attn_norm, m_q_norm, m_kv_norm, m_sinks, m_out_norm_swa, m_out_norm_mla, m_ffn_norm, m_final_norm]
    small_v = [v_attn_norm, v_q_norm, v_kv_norm, v_sinks, v_out_norm_swa, v_out_norm_mla, v_ffn_norm, v_final_norm]
    n_rows = -(-n_small // (8 * LANES)) * 8

    def pack(arrs):
        f = jnp.concatenate([a.reshape(-1) for a in arrs])
        return jnp.pad(f, (0, n_rows * LANES - n_small), constant_values=1.0).reshape(n_rows, LANES)

    g_small_pack = jnp.pad(tot[:n_small], (0, n_rows * LANES - n_small)).reshape(n_rows, LANES)
    upd_small = _adamw("adam_small", pack(small_w), g_small_pack, pack(small_m), pack(small_v))

    def unpack(p):
        f = p.reshape(-1)
        out, off = [], 0
        for a, n in zip(small_w, sizes):
            out.append(f[off:off + n].reshape(a.shape))
            off += n
        return out

    g_small = unpack(g_small_pack)
    d_small, m_small, v_small = [unpack(p) for p in upd_small]
    d_meta, nm_meta, nv_meta = _adamw("adam_meta", meta_tokens, g_meta_mine, m_meta_tokens, v_meta_tokens)

    big_m = [m_w_in, m_w_q_up, m_w_kv_up, m_w_o, m_w_gate, m_w_up, m_w_down]
    big_v = [v_w_in, v_w_q_up, v_w_kv_up, v_w_o, v_w_gate, v_w_up, v_w_down]
    upd_big = [_adamw("adam_big", w, g, m, v) for w, g, m, v in zip(big, g_big, big_m, big_v)]

    names = ["meta_tokens", "attn_norm", "w_in", "q_norm", "w_q_up", "kv_norm", "w_kv_up", "sinks",
             "out_norm_swa", "out_norm_mla", "w_o", "ffn_norm", "w_gate", "w_up", "w_down", "final_norm"]
    small_idx = {"attn_norm": 0, "q_norm": 1, "kv_norm": 2, "sinks": 3, "out_norm_swa": 4,
                 "out_norm_mla": 5, "ffn_norm": 6, "final_norm": 7}
    big_idx = {"w_in": 0, "w_q_up": 1, "w_kv_up": 2, "w_o": 3, "w_gate": 4, "w_up": 5, "w_down": 6}
    grads, deltas, new_m, new_v = [], [], [], []
    for nme in names:
        if nme == "meta_tokens":
            quad = (g_meta_mine, d_meta, nm_meta, nv_meta)
        elif nme in small_idx:
            i = small_idx[nme]
            quad = (g_small[i], d_small[i], m_small[i], v_small[i])
        else:
            i = big_idx[nme]
            quad = (g_big[i], *upd_big[i])
        grads.append(quad[0]); deltas.append(quad[1]); new_m.append(quad[2]); new_v.append(quad[3])
    return (loss, grad_x, *grads, *deltas, *new_m, *new_v)
```

```python
import jax
import jax.numpy as jnp
from jax import lax
from jax.experimental import pallas as pl
from jax.experimental.pallas import tpu as pltpu

F32 = jnp.float32
BF16 = jnp.bfloat16
MXU_DTYPE = BF16

D_MODEL = 1024
DEPTH = 4
N_META = 16
BLOCK = 128
WINDOW = 128
ROPE_THETA = 10000.0
EPS = 1e-6
NEG = -1e30
SWA_HEADS = 8
SWA_KV_HEADS = 2
SWA_HEAD_DIM = 64
MLA_HEADS = 8
MLA_Q_RANK = 256
MLA_KV_RANK = 128
MLA_NOPE_DIM = 64
MLA_ROPE_DIM = 32
MLA_V_DIM = 64
MLA_QK_DIM = MLA_NOPE_DIM + MLA_ROPE_DIM
D_FF = 2816
FRONT = (-N_META) % BLOCK
LANES = 128
SLOT_W = 8 * LANES
C_QA, C_KA, C_VA, C_QL, C_KL, C_KR, IN_WP = 0, 512, 640, 768, 1024, 1152, 1280
KR_LANE = 64
IN_W = 1184

ADAM_LR, ADAM_B1, ADAM_B2, ADAM_EPS, ADAM_WD, ADAM_STEP = 0.001, 0.9, 0.999, 1e-08, 0.01, 10

VMEM_LIMIT = 48 * 1024 * 1024
MESH = pl.DeviceIdType.MESH


def _tile(n, prefs):
    for t in prefs:
        if n % t == 0:
            return t
    return n


def _params(sem):
    return pltpu.CompilerParams(dimension_semantics=sem, vmem_limit_bytes=VMEM_LIMIT)


_DIMS = {"nn": (((1,), (0,)), ((), ())), "nt": (((1,), (1,)), ((), ())), "tn": (((0,), (0,)), ((), ()))}


def _mm(name, a, b, mode, out_dtype=F32, res=None):
    if mode == "nn":
        (M, K), (_, N) = a.shape, b.shape
    elif mode == "nt":
        (M, K), (N, _) = a.shape, b.shape
    else:
        (K, M), (_, N) = a.shape, b.shape
    lane_tiles = (1408, 1024, 640, 768, 512, 384, 256, 128)
    row_tiles = (1056, 528, 512, 384, 256, 128)
    bm = _tile(M, lane_tiles if mode == "tn" else row_tiles)
    bn = _tile(N, lane_tiles)
    bk = _tile(K, row_tiles if mode == "tn" else lane_tiles)
    nk = K // bk
    if mode == "tn":
        a_spec = pl.BlockSpec((bk, bm), lambda i, j, k: (k, i))
    else:
        a_spec = pl.BlockSpec((bm, bk), lambda i, j, k: (i, k))
    if mode == "nt":
        b_spec = pl.BlockSpec((bn, bk), lambda i, j, k: (j, k))
    else:
        b_spec = pl.BlockSpec((bk, bn), lambda i, j, k: (k, j))
    o_spec = pl.BlockSpec((bm, bn), lambda i, j, k: (i, j))
    in_specs = [a_spec, b_spec]
    args = [a, b]
    if res is not None:
        in_specs.append(o_spec)
        args.append(res)
    dims = _DIMS[mode]

    def kern(a_ref, b_ref, *rest):
        if res is not None:
            r_ref, o_ref = rest[0], rest[1]
            scr = rest[2:]
        else:
            r_ref, o_ref = None, rest[0]
            scr = rest[1:]
        p = lax.dot_general(a_ref[...].astype(MXU_DTYPE), b_ref[...].astype(MXU_DTYPE), dims,
                            preferred_element_type=F32)

        def finish(val):
            if r_ref is not None:
                val = val + r_ref[...]
            o_ref[...] = val.astype(o_ref.dtype)

        if nk == 1:
            finish(p)
        else:
            acc = scr[0]
            k = pl.program_id(2)

            @pl.when(k == 0)
            def _():
                acc[...] = p

            @pl.when(k > 0)
            def _():
                acc[...] += p

            @pl.when(k == nk - 1)
            def _():
                finish(acc[...])

    return pl.pallas_call(
        kern, name=name,
        out_shape=jax.ShapeDtypeStruct((M, N), out_dtype),
        grid=(M // bm, N // bn, nk),
        in_specs=in_specs, out_specs=o_spec,
        scratch_shapes=[pltpu.VMEM((bm, bn), F32)] if nk > 1 else [],
        compiler_params=_params(("parallel", "parallel", "arbitrary")),
    )(*args)


def _rowmap(name, body, rows, vecs, outs, accs=(), tr_prefs=(384, 256, 128)):
    R = rows[0].shape[0]
    tr = _tile(R, tr_prefs)
    n_r, n_v, n_o, n_a = len(rows), len(vecs), len(outs), len(accs)

    def kern(*refs):
        ins = [r[...] for r in refs[:n_r + n_v]]
        o_refs = refs[n_r + n_v:n_r + n_v + n_o]
        a_refs = refs[n_r + n_v + n_o:]
        res = body(*ins)
        for o_ref, val in zip(o_refs, res[:n_o]):
            o_ref[...] = val.astype(o_ref.dtype)
        if n_a:
            first = pl.program_id(0) == 0

            @pl.when(first)
            def _():
                for a_ref, val in zip(a_refs, res[n_o:]):
                    a_ref[...] = val

            @pl.when(jnp.logical_not(first))
            def _():
                for a_ref, val in zip(a_refs, res[n_o:]):
                    a_ref[...] += val

    in_specs = [pl.BlockSpec((tr, r.shape[1]), lambda i: (i, 0)) for r in rows]
    in_specs += [pl.BlockSpec((1, v.shape[1]), lambda i: (0, 0)) for v in vecs]
    out_specs = [pl.BlockSpec((tr, c), lambda i: (i, 0)) for c, _ in outs]
    out_specs += [pl.BlockSpec((1, c), lambda i: (0, 0)) for c in accs]
    out_shape = [jax.ShapeDtypeStruct((R, c), dt) for c, dt in outs]
    out_shape += [jax.ShapeDtypeStruct((1, c), F32) for c in accs]
    return pl.pallas_call(
        kern, name=name, out_shape=out_shape, grid=(R // tr,),
        in_specs=in_specs, out_specs=out_specs,
        compiler_params=_params(("arbitrary",) if n_a else ("parallel",)),
    )(*rows, *vecs)


def _lane(shape):
    return lax.broadcasted_iota(jnp.int32, shape, 1)


def _rot_swa(x):
    lane = _lane(x.shape)
    return jnp.where((lane & 63) < 32, -pltpu.roll(x, 96, 1), pltpu.roll(x, 32, 1))


def _rot_mla(x):
    lane = _lane(x.shape)
    lo = jnp.where(lane >= KR_LANE, -pltpu.roll(x, 112, 1), 0.0)
    hi = jnp.where(lane < KR_LANE + MLA_ROPE_DIM, pltpu.roll(x, 16, 1), 0.0)
    return jnp.where(lane < KR_LANE + 16, lo, hi)


def _rope(x, cos, sin, rot):
    return x * cos + rot(x) * sin


def _rope_t(g, cos, sin, rot):
    return g * cos - rot(g * sin)


def _low(x):
    return jnp.where(_lane(x.shape) < 64, x, 0.0)


def _value_slot(x):
    lane = _lane(x.shape)
    return jnp.where(lane < 64, x, jnp.where(lane == 64, 1.0, 0.0))


def _blk(x, j):
    return x[:, j * LANES:(j + 1) * LANES]


def _rms_r(x, width):
    return lax.rsqrt(jnp.sum(x * x, axis=-1, keepdims=True) * (1.0 / width) + EPS)


def _rms_bwd(x, g, dy, width):
    r = _rms_r(x, width)
    gdy = dy * g
    dot = jnp.sum(gdy * x, axis=-1, keepdims=True)
    dx = r * gdy - x * (r * r * r * (1.0 / width) * dot)
    return dx, dy * x * r


def _colsum(x):
    return jnp.sum(x, axis=0, keepdims=True)


def _rmsnorm_bwd(name, x, g, dy, dres):
    def body(xv, dyv, dr, gv):
        dx, dg = _rms_bwd(xv, gv, dyv, D_MODEL)
        return dx + dr, _colsum(dg)
    return _rowmap(name, body, [x, dy, dres], [g], [(D_MODEL, F32)], [D_MODEL])


def _prep1(proj, qn_g, kv_g, cosa, sina, cosm, sinm):
    def body(p, ca, sa, cm, sm, gq, gk):
        qa = []
        for j in range(4):
            xr = _rope(_blk(p, j), ca, sa, _rot_swa)
            qa += [_low(xr), _low(pltpu.roll(xr, 64, 1))]
        kr_ = _rope(_blk(p, C_KA // LANES), ca, sa, _rot_swa)
        ka = [_low(kr_), _low(pltpu.roll(kr_, 64, 1))]
        vv = _blk(p, C_VA // LANES)
        va = [_value_slot(vv), _value_slot(pltpu.roll(vv, 64, 1))]
        ql = p[:, C_QL:C_QL + MLA_Q_RANK]
        qn = ql * _rms_r(ql, MLA_Q_RANK) * gq
        kl = p[:, C_KL:C_KL + MLA_KV_RANK]
        cn = kl * _rms_r(kl, MLA_KV_RANK) * gk
        kr = _rope(_blk(p, C_KR // LANES), cm, sm, _rot_mla)
        return (jnp.concatenate(qa, 1), jnp.concatenate(ka, 1), jnp.concatenate(va, 1), qn, cn, kr)
    return _rowmap("prep1", body, [proj, cosa, sina, cosm, sinm], [qn_g, kv_g],
                   [(SLOT_W, BF16), (2 * LANES, BF16), (2 * LANES, BF16),
                    (MLA_Q_RANK, BF16), (MLA_KV_RANK, BF16), (LANES, F32)])


def _prep2(qb, kvb, kr, cosm, sinm):
    def body(q, kv, krv, cm, sm):
        qs, ks, vs = [], [], []
        for h in range(MLA_HEADS):
            qs.append(_rope(_blk(q, h), cm, sm, _rot_mla))
            kvh = _blk(kv, h)
            ks.append(_low(kvh) + krv)
            vs.append(_value_slot(pltpu.roll(kvh, 64, 1)))
        return jnp.concatenate(qs, 1), jnp.concatenate(ks, 1), jnp.concatenate(vs, 1)
    return _rowmap("prep2", body, [qb, kvb, kr, cosm, sinm], [],
                   [(SLOT_W, BF16), (SLOT_W, BF16), (SLOT_W, BF16)])


def _compact(slots):
    return jnp.concatenate(
        [_blk(slots, 2 * j) + pltpu.roll(_blk(slots, 2 * j + 1), 64, 1) for j in range(4)], 1)


def _expand(nat):
    out = []
    for j in range(4):
        b = _blk(nat, j)
        out += [_low(b), _low(pltpu.roll(b, 64, 1))]
    return jnp.concatenate(out, 1)


def _merge_fwd(oa, ob, ga, gb):
    def body(a, b, gav, gbv):
        xa, xb = _compact(a), _compact(b)
        return (jnp.concatenate([xa * _rms_r(xa, 512) * gav, xb * _rms_r(xb, 512) * gbv], 1),)
    return _rowmap("merge_fwd", body, [oa, ob], [ga, gb], [(D_MODEL, BF16)])[0]


def _merge_bwd(dmix, oa, ob, lse_a, ga, gb, sink_slots):
    def body(dm, a, b, lse, gav, gbv, sk):
        outs = []
        accs = []
        for o, g, lo in ((a, gav, 0), (b, gbv, 512)):
            x = _compact(o)
            dx, dg = _rms_bwd(x, g, dm[:, lo:lo + 512], 512)
            do = _expand(dx)
            delta = jnp.concatenate(
                [jnp.broadcast_to(jnp.sum(_blk(do, h) * _blk(o, h), axis=-1, keepdims=True),
                                  (do.shape[0], LANES)) for h in range(8)], 1)
            outs += [do, delta]
            accs.append(_colsum(dg))
        dsink = _colsum(-jnp.exp(sk - lse) * outs[1])
        return (*outs, *accs, dsink)
    return _rowmap("merge_bwd", body, [dmix, oa, ob, lse_a], [ga, gb, sink_slots],
                   [(SLOT_W, BF16), (SLOT_W, F32), (SLOT_W, BF16), (SLOT_W, F32)],
                   [512, 512, SLOT_W])


def _prep2_bwd(dq, dk, dv, cosm, sinm):
    def body(dqv, dkv, dvv, cm, sm):
        dqv, dkv, dvv = dqv.astype(F32), dkv.astype(F32), dvv.astype(F32)
        dqb, dkvb = [], []
        krsum = jnp.zeros((dqv.shape[0], LANES), F32)
        for h in range(MLA_HEADS):
            dqb.append(_rope_t(_blk(dqv, h), cm, sm, _rot_mla))
            dkh = _blk(dkv, h)
            dkvb.append(_low(dkh) + pltpu.roll(_blk(dvv, h), 64, 1))
            krsum = krsum + dkh
        lane = _lane(krsum.shape)
        dkr = jnp.where((lane >= KR_LANE) & (lane < KR_LANE + MLA_ROPE_DIM),
                        _rope_t(krsum, cm, sm, _rot_mla), 0.0)
        return jnp.concatenate(dqb, 1), jnp.concatenate(dkvb, 1), dkr
    return _rowmap("prep2_bwd", body, [dq, dk, dv, cosm, sinm], [],
                   [(SLOT_W, BF16), (SLOT_W, BF16), (LANES, F32)])


def _prep1_bwd(proj, dqa, dka, dva, dqn, dcn, dkr, cosa, sina, qn_g, kv_g):
    def body(p, dq, dk, dv, dqnv, dcnv, dkrv, ca, sa, gq, gk):
        dq, dk, dv = dq.astype(F32), dk.astype(F32), dv.astype(F32)
        cols = []
        for j in range(4):
            nat = _blk(dq, 2 * j) + pltpu.roll(_blk(dq, 2 * j + 1), 64, 1)
            cols.append(_rope_t(nat, ca, sa, _rot_swa))
        grp = lambda d, g: sum(_blk(d, 4 * g + i) for i in range(4))
        cols.append(_rope_t(grp(dk, 0) + pltpu.roll(grp(dk, 1), 64, 1), ca, sa, _rot_swa))
        cols.append(grp(dv, 0) + pltpu.roll(grp(dv, 1), 64, 1))
        dql, dgq = _rms_bwd(p[:, C_QL:C_QL + MLA_Q_RANK], gq, dqnv, MLA_Q_RANK)
        dkl, dgk = _rms_bwd(p[:, C_KL:C_KL + MLA_KV_RANK], gk, dcnv, MLA_KV_RANK)
        cols += [dql, dkl, dkrv]
        return jnp.concatenate(cols, 1), _colsum(dgq), _colsum(dgk)
    return _rowmap("prep1_bwd", body, [proj, dqa, dka, dva, dqn, dcn, dkr, cosa, sina], [qn_g, kv_g],
                   [(IN_WP, BF16)], [MLA_Q_RANK, MLA_KV_RANK], tr_prefs=(192, 128))


def _sigmoid(x):
    return 1.0 / (1.0 + jnp.exp(-x))


def _ffn_tiles(T, F):
    return _tile(T, (528, 512, 384, 256, 128)), _tile(F, (1408, 1024, 640, 512, 256, 128))


def _norm_proj(name, h, g, weights, swiglu):
    (T, D), F = h.shape, weights[0].shape[1]
    bm, bn = _ffn_tiles(T, F)
    nw = len(weights)

    def kern(h_ref, g_ref, *rest):
        w_refs, u_ref, o_refs, u_scr = rest[:nw], rest[nw], rest[nw + 1:-1], rest[-1]

        @pl.when(pl.program_id(1) == 0)
        def _():
            x = h_ref[...]
            u = (x * _rms_r(x, D) * g_ref[...]).astype(u_scr.dtype)
            u_scr[...] = u
            u_ref[...] = u

        uv = u_scr[...]
        prods = [jnp.dot(uv, w[...], preferred_element_type=F32) for w in w_refs]
        for o_ref, p in zip(o_refs, prods):
            o_ref[...] = p
        if swiglu:
            a, b = prods
            o_refs[nw][...] = (a * _sigmoid(a) * b).astype(o_refs[nw].dtype)

    w_spec = pl.BlockSpec((D, bn), lambda i, j: (0, j))
    row_spec = pl.BlockSpec((bm, D), lambda i, j: (i, 0))
    o_spec = pl.BlockSpec((bm, bn), lambda i, j: (i, j))
    n_out = nw + (1 if swiglu else 0)
    return pl.pallas_call(
        kern, name=name,
        out_shape=[jax.ShapeDtypeStruct((T, D), MXU_DTYPE)] + [jax.ShapeDtypeStruct((T, F), F32)] * nw
        + ([jax.ShapeDtypeStruct((T, F), MXU_DTYPE)] if swiglu else []),
        grid=(T // bm, F // bn),
        in_specs=[row_spec, pl.BlockSpec((1, D), lambda i, j: (0, 0))] + [w_spec] * nw,
        out_specs=[row_spec] + [o_spec] * n_out,
        scratch_shapes=[pltpu.VMEM((bm, D), MXU_DTYPE)],
        compiler_params=_params(("parallel", "arbitrary")),
    )(h, g, *weights)


def _ffn_mid_bwd(dh, w_down, a, b):
    (T, D), F = dh.shape, w_down.shape[0]
    bm, bn = _ffn_tiles(T, F)

    def kern(dh_ref, wd_ref, a_ref, b_ref, da_ref, db_ref):
        d = lax.dot_general(dh_ref[...].astype(MXU_DTYPE), wd_ref[...], _DIMS["nt"], preferred_element_type=F32)
        av, bv = a_ref[...], b_ref[...]
        s = _sigmoid(av)
        da_ref[...] = (d * bv * (s * (1.0 + av * (1.0 - s)))).astype(da_ref.dtype)
        db_ref[...] = (d * (av * s)).astype(db_ref.dtype)

    o_spec = pl.BlockSpec((bm, bn), lambda i, j: (i, j))
    return pl.pallas_call(
        kern, name="ffn_mid_bwd",
        out_shape=[jax.ShapeDtypeStruct((T, F), MXU_DTYPE)] * 2,
        grid=(T // bm, F // bn),
        in_specs=[pl.BlockSpec((bm, D), lambda i, j: (i, 0)), pl.BlockSpec((bn, D), lambda i, j: (j, 0)),
                  o_spec, o_spec],
        out_specs=[o_spec, o_spec],
        compiler_params=_params(("parallel", "parallel")),
    )(dh, w_down, a, b)


def _loss_head(h, target, g):
    T = h.shape[0]
    nb = T // BLOCK

    def kern(h_ref, t_ref, g_ref, dh_ref, dg_ref, loss_ref, acc):
        i = pl.program_id(0)

        @pl.when(i == 0)
        def _():
            dh_ref[...] = jnp.zeros_like(dh_ref)
            dg_ref[...] = jnp.zeros_like(dg_ref)
            acc[...] = jnp.zeros_like(acc)

        @pl.when(i > 0)
        def _():
            x = h_ref[...]
            gv = g_ref[...]
            e = x * _rms_r(x, D_MODEL) * gv - t_ref[...]
            acc[...] += _colsum(e * e)
            dx, dg = _rms_bwd(x, gv, e * (1.0 / D_MODEL), D_MODEL)
            dh_ref[...] = dx
            dg_ref[...] += _colsum(dg)

        @pl.when(i == nb - 1)
        def _():
            tot = jnp.sum(acc[...], axis=-1, keepdims=True) * (0.5 / D_MODEL)
            loss_ref[...] = jnp.broadcast_to(tot, loss_ref.shape)

    return pl.pallas_call(
        kern, name="loss_head",
        out_shape=[jax.ShapeDtypeStruct((T, D_MODEL), F32), jax.ShapeDtypeStruct((1, D_MODEL), F32),
                   jax.ShapeDtypeStruct((1, LANES), F32)],
        grid=(nb,),
        in_specs=[pl.BlockSpec((BLOCK, D_MODEL), lambda i: (i, 0)),
                  pl.BlockSpec((BLOCK, D_MODEL), lambda i: (jnp.maximum(i - 1, 0), 0)),
                  pl.BlockSpec((1, D_MODEL), lambda i: (0, 0))],
        out_specs=[pl.BlockSpec((BLOCK, D_MODEL), lambda i: (i, 0)),
                   pl.BlockSpec((1, D_MODEL), lambda i: (0, 0)),
                   pl.BlockSpec((1, LANES), lambda i: (0, 0))],
        scratch_shapes=[pltpu.VMEM((1, D_MODEL), F32)],
        compiler_params=_params(("arbitrary",)),
    )(h, target, g)


LOG2E = 1.4426950408889634


def _attn_plan(T, causal):
    tq = _tile(T, (384, 256, 128))
    ck = min(2 * tq, T) if causal else min(tq + WINDOW, T)
    return tq, ck, (-(-T // ck) if causal else 1)


def _chunk(i, c, T, tq, ck, causal):
    if causal:
        return pl.multiple_of(jnp.minimum(c * ck, T - ck), LANES), c * ck
    return pl.multiple_of(jnp.clip(i * tq - WINDOW, 0, T - ck), LANES), 0


def _n_chunks(i, tq, ck, causal):
    return ((i + 1) * tq + ck - 1) // ck if causal else 1


def _mask(s, i, start, first, tq, ck, causal):
    qpos = i * tq + lax.broadcasted_iota(jnp.int32, (tq, 1), 0)
    kpos = start + lax.broadcasted_iota(jnp.int32, (tq, ck), 1)
    low = jnp.maximum(jnp.where(qpos < FRONT, 0, FRONT), first)
    if not causal:
        low = jnp.maximum(low, qpos - (WINDOW - 1))
    return jnp.where(kpos >= low, jnp.where(kpos <= qpos, s, NEG), NEG)


def _chunk_loop(n, body, init, several):
    carry = body(0, init, True)
    if not several:
        return carry
    carry = body(n - 1, carry, True)
    return lax.fori_loop(1, n - 1, lambda c, cr: body(c, cr, False), carry)


def _tile_sweep(nq, tq, ck, causal, q_tiles, pair=False):
    if pair and nq >= 2 and (not causal or ck == 2 * tq):
        if causal:
            q_tiles([0, 1], False)
            lax.fori_loop(1, nq // 2, lambda p, cr: q_tiles([2 * p, 2 * p + 1], True) or cr, 0)
        else:
            lax.fori_loop(0, nq // 2, lambda p, cr: q_tiles([2 * p, 2 * p + 1], False) or cr, 0)
        if nq % 2:
            q_tiles([nq - 1], causal and nq >= 3)
        return
    one = min(nq, ck // tq) if causal else nq
    lax.fori_loop(0, one, lambda i, cr: q_tiles([i], False) or cr, 0)
    if one < nq:
        lax.fori_loop(one, nq, lambda i, cr: q_tiles([i], True) or cr, 0)


BWD_HEADS_PER_STEP = 1


def _head_cols(group, hp):
    q_cols = lambda hh: slice(hh * LANES, (hh + 1) * LANES)
    if group == 1:
        return q_cols, q_cols, hp * LANES
    assert group % hp == 0
    return q_cols, (lambda hh: slice(0, LANES)), LANES


def _whole(T, width, index, single):
    if single:
        return pl.BlockSpec((T, width), index, pipeline_mode=pl.Buffered(1))
    return pl.BlockSpec((T, width), index)


def _attn_fwd(name, q, k, v, sinks, group, causal, scale, hp):
    T = q.shape[0]
    H = q.shape[1] // LANES
    HP = hp
    tq, ck, slots = _attn_plan(T, causal)
    nq = T // tq
    c2 = scale * LOG2E
    q_cols, k_cols, kw = _head_cols(group, HP)

    def kern(sink_ref, q_ref, k_ref, v_ref, o_ref, lse_ref, s_scr):
        sink2 = [sink_ref[pl.program_id(0) * HP + hh] * LOG2E for hh in range(HP)]

        def q_tiles(tiles, several):
            rows = [pl.ds(pl.multiple_of(i * tq, tq), tq) for i in tiles]
            chains = [(t, hh) for t in range(len(tiles)) for hh in range(HP)]
            qq = [q_ref[rows[t], q_cols(hh)] for t, hh in chains]
            n = _n_chunks(tiles[0], tq, ck, causal)

            def score(c, m2, masked):
                out = []
                for n_ch, (t, hh) in enumerate(chains):
                    start, first = _chunk(tiles[t], c, T, tq, ck, causal)
                    s = lax.dot_general(qq[n_ch], k_ref[pl.ds(start, ck), k_cols(hh)], _DIMS["nt"],
                                        preferred_element_type=F32) * c2
                    if masked:
                        s = _mask(s, tiles[t], start, first, tq, ck, causal)
                    s_scr[n_ch, c] = s
                    out.append(jnp.maximum(m2[n_ch], jnp.max(s, axis=-1, keepdims=True)))
                return tuple(out)

            m2 = _chunk_loop(n, score, tuple(jnp.full((tq, 1), sink2[hh], F32) for _, hh in chains), several)

            def weigh(c, acc):
                out = []
                for n_ch, (t, hh) in enumerate(chains):
                    start, _ = _chunk(tiles[t], c, T, tq, ck, causal)
                    p = jnp.exp2(s_scr[n_ch, c] - m2[n_ch])
                    out.append(acc[n_ch] + jnp.dot(p.astype(MXU_DTYPE), v_ref[pl.ds(start, ck), k_cols(hh)],
                                                   preferred_element_type=F32))
                return tuple(out)

            acc = lax.fori_loop(0, n, weigh, tuple(jnp.zeros((tq, LANES), F32) for _ in chains))
            lane = _lane((tq, LANES))
            for n_ch, (t, hh) in enumerate(chains):
                l = acc[n_ch][:, 64:65] + jnp.exp2(sink2[hh] - m2[n_ch])
                o_ref[rows[t], q_cols(hh)] = jnp.where(lane < 64, acc[n_ch] / l, 0.0)
                lse_ref[rows[t], q_cols(hh)] = jnp.broadcast_to(m2[n_ch] * (1.0 / LOG2E) + jnp.log(l), (tq, LANES))

        _tile_sweep(nq, tq, ck, causal, q_tiles, pair=True)

    q_spec = _whole(T, HP * LANES, lambda g: (0, g), HP > 1)
    kv_spec = _whole(T, kw, (lambda g: (0, g)) if group == 1 else (lambda g: (0, (g * HP) // group)), HP > 1)
    return pl.pallas_call(
        kern, name=name,
        out_shape=[jax.ShapeDtypeStruct((T, H * LANES), F32)] * 2,
        grid=(H // HP,),
        in_specs=[pl.BlockSpec(memory_space=pltpu.SMEM), q_spec, kv_spec, kv_spec],
        out_specs=[q_spec, q_spec],
        scratch_shapes=[pltpu.VMEM((2 * HP, slots, tq, ck), F32)],
        compiler_params=_params(("parallel",)),
    )(sinks, q, k, v)


def _attn_bwd(name, q, k, v, do, lse, delta, group, causal, scale):
    T = q.shape[0]
    H = q.shape[1] // LANES
    HP = BWD_HEADS_PER_STEP
    tq, ck, _ = _attn_plan(T, causal)
    nq = T // tq
    c2 = scale * LOG2E
    q_cols, k_cols, kw = _head_cols(group, HP)

    def kern(q_ref, k_ref, v_ref, do_ref, lse_ref, dl_ref, dq_ref, dk_ref, dv_ref, dk_acc, dv_acc):
        dk_acc[...] = jnp.zeros_like(dk_acc)
        dv_acc[...] = jnp.zeros_like(dv_acc)

        def q_tiles(tiles, several):
            rows = [pl.ds(pl.multiple_of(i * tq, tq), tq) for i in tiles]
            chains = [(t, hh) for t in range(len(tiles)) for hh in range(HP)]
            qq = [q_ref[rows[t], q_cols(hh)] for t, hh in chains]
            dd = [do_ref[rows[t], q_cols(hh)] for t, hh in chains]
            lse2 = [lse_ref[rows[t], q_cols(hh)][:, 0:1] * LOG2E for t, hh in chains]
            dl_c = [dl_ref[rows[t], q_cols(hh)][:, 0:1] for t, hh in chains]

            def chunk(c, dq, masked):
                out = []
                for n_ch, (t, hh) in enumerate(chains):
                    start, first = _chunk(tiles[t], c, T, tq, ck, causal)
                    keys = pl.ds(start, ck)
                    kk, vv = k_ref[keys, k_cols(hh)], v_ref[keys, k_cols(hh)]
                    s = lax.dot_general(qq[n_ch], kk, _DIMS["nt"], preferred_element_type=F32)
                    if masked:
                        s = _mask(s, tiles[t], start, first, tq, ck, causal)
                    p = jnp.exp2(s * c2 - lse2[n_ch])
                    dv_acc[keys, q_cols(hh)] += lax.dot_general(p.astype(MXU_DTYPE), dd[n_ch], _DIMS["tn"],
                                                                preferred_element_type=F32)
                    dp = lax.dot_general(dd[n_ch], vv, _DIMS["nt"], preferred_element_type=F32)
                    ds = (p * (dp - dl_c[n_ch])).astype(MXU_DTYPE)
                    dk_acc[keys, q_cols(hh)] += lax.dot_general(ds, qq[n_ch], _DIMS["tn"],
                                                                preferred_element_type=F32) * scale
                    out.append(dq[n_ch] + jnp.dot(ds, kk, preferred_element_type=F32))
                return tuple(out)

            dq = _chunk_loop(_n_chunks(tiles[0], tq, ck, causal), chunk,
                             tuple(jnp.zeros((tq, LANES), F32) for _ in chains), several)
            for n_ch, (t, hh) in enumerate(chains):
                dq_ref[rows[t], q_cols(hh)] = (dq[n_ch] * scale).astype(dq_ref.dtype)

        _tile_sweep(nq, tq, ck, causal, q_tiles, pair=True)
        dk_ref[...] = dk_acc[...].astype(dk_ref.dtype)
        dv_ref[...] = dv_acc[...].astype(dv_ref.dtype)

    q_spec = _whole(T, HP * LANES, lambda g: (0, g), False)
    kv_spec = _whole(T, kw, (lambda g: (0, g)) if group == 1 else (lambda g: (0, (g * HP) // group)), False)
    return pl.pallas_call(
        kern, name=name,
        out_shape=[jax.ShapeDtypeStruct((T, H * LANES), MXU_DTYPE)] * 3,
        grid=(H // HP,),
        in_specs=[q_spec, kv_spec, kv_spec, q_spec, q_spec, q_spec],
        out_specs=[q_spec, q_spec, q_spec],
        scratch_shapes=[pltpu.VMEM((T, HP * LANES), F32)] * 2,
        compiler_params=_params(("parallel",)),
    )(q, k, v, do, lse, delta)


def _ew(name, fn, ins, out_dtypes):
    shape = ins[0].shape
    flat = [a.reshape(-1, shape[-1]) for a in ins]
    R, C = flat[0].shape
    tr = _tile(R, (512, 256, 128, 64, 32, 16, 8))
    n_in = len(ins)

    def kern(*refs):
        res = fn(*[r[...] for r in refs[:n_in]])
        for o_ref, val in zip(refs[n_in:], res):
            o_ref[...] = val.astype(o_ref.dtype)

    spec = pl.BlockSpec((tr, C), lambda i: (i, 0))
    outs = pl.pallas_call(
        kern, name=name,
        out_shape=[jax.ShapeDtypeStruct((R, C), dt) for dt in out_dtypes],
        grid=(R // tr,), in_specs=[spec] * n_in, out_specs=[spec] * len(out_dtypes),
        compiler_params=_params(("parallel",)),
    )(*flat)
    return [o.reshape(shape) for o in outs]


def _adamw(name, w, g, m, v):
    c1 = 1.0 - ADAM_B1 ** ADAM_STEP
    c2 = 1.0 - ADAM_B2 ** ADAM_STEP

    def fn(wv, gv, mv, vv):
        mn = ADAM_B1 * mv + (1.0 - ADAM_B1) * gv
        vn = ADAM_B2 * vv + (1.0 - ADAM_B2) * (gv * gv)
        delta = -ADAM_LR * ((mn / c1) / (jnp.sqrt(vn / c2) + ADAM_EPS) + ADAM_WD * wv)
        return delta, mn, vn

    return _ew(name, fn, [w, g, m, v], [F32, F32, F32])


_ANY = pl.BlockSpec(memory_space=pl.ANY)


def _where_am_i():
    x, y, c = lax.axis_index("x"), lax.axis_index("y"), lax.axis_index("c")
    chips = [(1 - x, y), (x, 1 - y), (1 - x, 1 - y)]
    return x, y, c, chips


def _gather_weights(shards, meta):
    arrs = list(shards) + [meta]
    n = len(arrs)
    per = [a.shape[0] // 2 for a in arrs]

    def body(*refs):
        ins, outs = refs[:n], refs[n:2 * n]
        send1, recv1, send2, recv2 = refs[2 * n:]
        x, y, c, chips = _where_am_i()
        me = 2 * x + y

        def half(ref, k, cc):
            return ref.at[pl.ds(per[k] * cc, per[k])]

        first = []
        for k in range(n):
            for j, (cx, cy) in enumerate(chips):
                first.append(pltpu.make_async_remote_copy(
                    src_ref=half(ins[k], k, c), dst_ref=half(outs[k].at[me], k, c),
                    send_sem=send1.at[k, j], recv_sem=recv1.at[k, j],
                    device_id=(cx, cy, c), device_id_type=MESH))
        for cp in first:
            cp.start()
        passed = []
        for k in range(n):
            for j, (cx, cy) in enumerate(chips):
                landed = half(outs[k].at[2 * cx + cy], k, c)
                pltpu.make_async_remote_copy(
                    src_ref=landed, dst_ref=landed, send_sem=send1.at[k, j], recv_sem=recv1.at[k, j],
                    device_id=(cx, cy, c), device_id_type=MESH).wait_recv()
                fwd = pltpu.make_async_remote_copy(
                    src_ref=landed, dst_ref=landed, send_sem=send2.at[k, j], recv_sem=recv2.at[k, j],
                    device_id=(x, y, 1 - c), device_id_type=MESH)
                fwd.start()
                passed.append(fwd)
        for k in range(n):
            for j, (cx, cy) in enumerate(chips):
                other = half(outs[k].at[2 * cx + cy], k, 1 - c)
                pltpu.make_async_remote_copy(
                    src_ref=other, dst_ref=other, send_sem=send2.at[k, j], recv_sem=recv2.at[k, j],
                    device_id=(x, y, 1 - c), device_id_type=MESH).wait_recv()
        for cp in first + passed:
            cp.wait_send()

    return pl.pallas_call(
        body, name="gather_weights",
        out_shape=[jax.ShapeDtypeStruct((4,) + a.shape, a.dtype) for a in arrs],
        in_specs=[_ANY] * n, out_specs=[_ANY] * n,
        scratch_shapes=[pltpu.SemaphoreType.DMA((n, 3))] * 4,
    )(*arrs)


_HBM = pl.BlockSpec(memory_space=pltpu.HBM)
_SEM = pl.BlockSpec(memory_space=pltpu.SEMAPHORE)
_FLOWS = pltpu.SideEffectType.DATAFLOW_SIDE_EFFECTING


def _in_hbm(a):
    return pltpu.with_memory_space_constraint(a, pltpu.HBM)


def _gather_start(shards):
    n = len(shards)
    nl = shards[0].shape[0]
    lands = [lax.empty((4,) + sh.shape[1:], sh.dtype) for _ in range(nl) for sh in shards]

    def body(*refs):
        srcs, land = refs[:n], refs[n:n + n * nl]
        send, recv, token = refs[n + n * nl:n + n * nl + 3]
        x, y, c, chips = _where_am_i()
        me = 2 * x + y
        for l in range(nl):
            for k in range(n):
                for cx, cy in chips:
                    pltpu.make_async_remote_copy(
                        src_ref=srcs[k].at[l], dst_ref=land[l * n + k].at[me],
                        send_sem=send.at[l], recv_sem=recv.at[l],
                        device_id=(cx, cy, c), device_id_type=MESH).start()
        token[...] = jnp.zeros_like(token)

    ins = [_in_hbm(a) for a in list(shards) + lands]
    res = pl.pallas_call(
        body, name="gather_start",
        out_shape=(pltpu.SemaphoreType.DMA((nl,)), pltpu.SemaphoreType.DMA((nl,)),
                   jax.ShapeDtypeStruct((8, LANES), F32)) + tuple(pltpu.HBM(a.shape, a.dtype) for a in ins),
        in_specs=[_HBM] * len(ins),
        out_specs=(_SEM, _SEM, pl.BlockSpec(memory_space=pltpu.VMEM)) + (_HBM,) * len(ins),
        input_output_aliases={i: 3 + i for i in range(len(ins))},
        compiler_params=pltpu.CompilerParams(has_side_effects=_FLOWS),
    )(*ins)
    send, recv, token = res[:3]
    thru = res[3:3 + n]
    lands = res[3 + n:]
    return send, recv, token, list(thru), [list(lands[l * n:(l + 1) * n]) for l in range(nl)]


def _gather_wait(name, send, recv, l, thru, land, after):
    n = len(thru)

    def body(*refs):
        srcs, lands = refs[:n], refs[n:2 * n]
        send_sem, recv_sem = refs[2 * n:2 * n + 2]
        x, y, c, chips = _where_am_i()
        for k in range(n):
            for cx, cy in chips:
                copy = pltpu.make_async_remote_copy(
                    src_ref=srcs[k].at[l], dst_ref=lands[k].at[2 * cx + cy],
                    send_sem=send_sem.at[l], recv_sem=recv_sem.at[l],
                    device_id=(cx, cy, c), device_id_type=MESH)
                copy.wait_send()
                copy.wait_recv()

    ins = list(thru) + list(land)
    res = pl.pallas_call(
        body, name=name,
        out_shape=tuple(pltpu.HBM(a.shape, a.dtype) for a in ins),
        in_specs=[_HBM] * len(ins) + [_SEM, _SEM, _ANY],
        out_specs=(_HBM,) * len(ins),
        input_output_aliases={i: i for i in range(len(ins))},
        compiler_params=pltpu.CompilerParams(has_side_effects=_FLOWS),
    )(*ins, send, recv, after)
    return list(res[:n]), list(res[n:])


def _pair_exchange(grads):
    n = len(grads)

    def body(*refs):
        ins, got = refs[:n], refs[n:2 * n]
        send, recv = refs[2 * n:]
        x, y, c, _ = _where_am_i()
        sib = (x, y, 1 - c)
        for k in range(n):
            h = ins[k].shape[2] // 2
            for ch in range(4):
                for l in range(ins[k].shape[1]):
                    pltpu.make_async_remote_copy(
                        src_ref=ins[k].at[ch, l, pl.ds((1 - c) * h, h)], dst_ref=got[k].at[ch, l],
                        send_sem=send.at[k], recv_sem=recv.at[k], device_id=sib, device_id_type=MESH).start()
        for k in range(n):
            pltpu.make_async_remote_copy(
                src_ref=got[k], dst_ref=got[k], send_sem=send.at[k], recv_sem=recv.at[k],
                device_id=sib, device_id_type=MESH).wait()

    return pl.pallas_call(
        body, name="reduce_pair",
        out_shape=[jax.ShapeDtypeStruct(g.shape[:2] + (g.shape[2] // 2, g.shape[3]), g.dtype) for g in grads],
        in_specs=[_ANY] * n, out_specs=[_ANY] * n,
        scratch_shapes=[pltpu.SemaphoreType.DMA((n,))] * 2,
    )(*grads)


def _half_tiling(r):
    h = r // 2
    tr = _tile(h, (512, 256, 352, 176, 128, 64, 32))
    return h, tr, h // tr


def _pair_add(full, got, c):
    _, ls, r, cols = full.shape
    h, tr, nh = _half_tiling(r)

    def kern(c_ref, a_ref, b_ref, o32_ref, o16_ref):
        tot = a_ref[...] + b_ref[...].astype(F32)
        o32_ref[...] = tot
        o16_ref[...] = tot.astype(o16_ref.dtype)

    blk = (None, None, tr, cols)
    mine = pl.BlockSpec(blk, lambda ch, l, i, cr: (ch, l, cr[0] * nh + i, 0))
    same = pl.BlockSpec(blk, lambda ch, l, i, cr: (ch, l, i, 0))
    return pl.pallas_call(
        kern, name="pair_add",
        out_shape=[jax.ShapeDtypeStruct(got.shape, F32), jax.ShapeDtypeStruct(got.shape, got.dtype)],
        grid_spec=pltpu.PrefetchScalarGridSpec(
            num_scalar_prefetch=1, grid=(4, ls, nh), in_specs=[mine, same], out_specs=[same, same]),
        compiler_params=_params(("parallel", "parallel", "parallel")),
    )(c.reshape(1), full, got)


def _scatter_start(name, parts):
    n = len(parts)
    lands = [lax.empty(p.shape, p.dtype) for p in parts]

    def body(*refs):
        srcs, land = refs[:n], refs[n:2 * n]
        send, recv, token = refs[2 * n:2 * n + 3]
        x, y, c, chips = _where_am_i()
        me = 2 * x + y
        for k in range(n):
            for cx, cy in chips:
                pltpu.make_async_remote_copy(
                    src_ref=srcs[k].at[2 * cx + cy], dst_ref=land[k].at[me], send_sem=send, recv_sem=recv,
                    device_id=(cx, cy, c), device_id_type=MESH).start()
        token[...] = jnp.zeros_like(token)

    ins = [_in_hbm(a) for a in list(parts) + lands]
    res = pl.pallas_call(
        body, name=name,
        out_shape=(pltpu.SemaphoreType.DMA(()), pltpu.SemaphoreType.DMA(()),
                   jax.ShapeDtypeStruct((8, LANES), F32)) + tuple(pltpu.HBM(a.shape, a.dtype) for a in ins),
        in_specs=[_HBM] * len(ins),
        out_specs=(_SEM, _SEM, pl.BlockSpec(memory_space=pltpu.VMEM)) + (_HBM,) * len(ins),
        input_output_aliases={i: 3 + i for i in range(len(ins))},
        compiler_params=pltpu.CompilerParams(has_side_effects=_FLOWS),
    )(*ins)
    return res[0], res[1], res[2], list(res[3:3 + n]), list(res[3 + n:])


def _scatter_wait(name, send, recv, thru, lands, after):
    n = len(thru)

    def body(*refs):
        srcs, land = refs[:n], refs[n:2 * n]
        send_sem, recv_sem = refs[2 * n:2 * n + 2]
        x, y, c, chips = _where_am_i()
        for k in range(n):
            for cx, cy in chips:
                copy = pltpu.make_async_remote_copy(
                    src_ref=srcs[k].at[2 * cx + cy], dst_ref=land[k].at[2 * cx + cy],
                    send_sem=send_sem, recv_sem=recv_sem, device_id=(cx, cy, c), device_id_type=MESH)
                copy.wait_send()
                copy.wait_recv()

    ins = list(thru) + list(lands)
    res = pl.pallas_call(
        body, name=name,
        out_shape=tuple(pltpu.HBM(a.shape, a.dtype) for a in ins),
        in_specs=[_HBM] * len(ins) + [_SEM, _SEM, _ANY],
        out_specs=(_HBM,) * len(ins),
        input_output_aliases={i: i for i in range(len(ins))},
        compiler_params=pltpu.CompilerParams(has_side_effects=_FLOWS),
    )(*ins, send, recv, after)
    return list(res[n:])


def _chip_add(landed, mine, me, c, into, l0):
    _, ls, h, cols = landed.shape
    _, tr, nh = _half_tiling(2 * h)

    def kern(me_ref, c_ref, land_ref, own_ref, into_ref, o_ref):
        own = own_ref[...]
        tot = None
        for j in range(4):
            term = jnp.where(me_ref[0] == j, own, land_ref[j].astype(F32))
            tot = term if tot is None else tot + term
        o_ref[...] = tot

    return pl.pallas_call(
        kern, name="chip_add",
        out_shape=jax.ShapeDtypeStruct(into.shape, F32),
        grid_spec=pltpu.PrefetchScalarGridSpec(
            num_scalar_prefetch=2, grid=(ls, nh),
            in_specs=[pl.BlockSpec((4, None, tr, cols), lambda l, i, mr, cr: (0, l, i, 0)),
                      pl.BlockSpec((None, None, tr, cols), lambda l, i, mr, cr: (mr[0], l, i, 0)),
                      pl.BlockSpec(memory_space=pl.ANY)],
            out_specs=pl.BlockSpec((None, tr, cols), lambda l, i, mr, cr: (l0 + l, cr[0] * nh + i, 0))),
        input_output_aliases={4: 0},
        compiler_params=_params(("parallel", "parallel")),
    )(me.reshape(1), c.reshape(1), landed, mine, into)


def _pair_join(sums):
    n = len(sums)

    def body(*refs):
        bufs = refs[n:2 * n]
        send, recv = refs[2 * n:]
        x, y, c, _ = _where_am_i()
        sib = (x, y, 1 - c)
        for k in range(n):
            h = bufs[k].shape[1] // 2
            for l in range(bufs[k].shape[0]):
                piece = bufs[k].at[l, pl.ds(c * h, h)]
                pltpu.make_async_remote_copy(
                    src_ref=piece, dst_ref=piece, send_sem=send.at[k], recv_sem=recv.at[k],
                    device_id=sib, device_id_type=MESH).start()
        for k in range(n):
            h = bufs[k].shape[1] // 2
            theirs = bufs[k].at[:, pl.ds((1 - c) * h, h)]
            pltpu.make_async_remote_copy(
                src_ref=theirs, dst_ref=theirs, send_sem=send.at[k], recv_sem=recv.at[k],
                device_id=sib, device_id_type=MESH).wait()

    return pl.pallas_call(
        body, name="reduce_join",
        out_shape=[jax.ShapeDtypeStruct(s.shape, s.dtype) for s in sums],
        in_specs=[_ANY] * n, out_specs=[_ANY] * n,
        input_output_aliases={k: k for k in range(n)},
        scratch_shapes=[pltpu.SemaphoreType.DMA((n,))] * 2,
    )(*sums)


def _allreduce_small(buf):
    R = buf.shape[0]

    def body(in_ref, out_ref, land, send, recv):
        x, y, c, _ = _where_am_i()
        me = 4 * x + 2 * y + c
        land[me] = in_ref[...]
        cps = []
        for k in range(1, 8):
            px, py, pc = x ^ (k >> 2), y ^ ((k >> 1) & 1), c ^ (k & 1)
            cps.append(pltpu.make_async_remote_copy(
                src_ref=in_ref, dst_ref=land.at[me], send_sem=send.at[k - 1], recv_sem=recv.at[k - 1],
                device_id=(px, py, pc), device_id_type=MESH))
        for cp in cps:
            cp.start()
        for k in range(1, 8):
            px, py, pc = x ^ (k >> 2), y ^ ((k >> 1) & 1), c ^ (k & 1)
            slot = land.at[4 * px + 2 * py + pc]
            pltpu.make_async_remote_copy(
                src_ref=slot, dst_ref=slot, send_sem=send.at[k - 1], recv_sem=recv.at[k - 1],
                device_id=(px, py, pc), device_id_type=MESH).wait_recv()
        for cp in cps:
            cp.wait_send()
        tot = land[0]
        for d in range(1, 8):
            tot = tot + land[d]
        out_ref[...] = tot

    vm = pl.BlockSpec(memory_space=pltpu.VMEM)
    return pl.pallas_call(
        body, name="allreduce_small",
        out_shape=jax.ShapeDtypeStruct(buf.shape, F32),
        in_specs=[vm], out_specs=vm,
        scratch_shapes=[pltpu.VMEM((8, R, LANES), F32), pltpu.SemaphoreType.DMA((7,)),
                        pltpu.SemaphoreType.DMA((7,))],
    )(buf)


def _rope_tables(T):
    pos = (jnp.arange(T) - FRONT).astype(F32)
    lane = jnp.arange(LANES)
    inv_a = ROPE_THETA ** (-(2 * ((lane % 64) % 32)).astype(F32) / SWA_HEAD_DIM)
    ang_a = pos[:, None] * inv_a[None, :]
    cosa, sina = jnp.cos(ang_a), jnp.sin(ang_a)
    inv_m = ROPE_THETA ** (-(2 * ((lane - KR_LANE) % 16)).astype(F32) / MLA_ROPE_DIM)
    ang_m = pos[:, None] * inv_m[None, :]
    on = ((lane >= KR_LANE) & (lane < KR_LANE + MLA_ROPE_DIM))[None, :]
    cosm = jnp.where(on, jnp.cos(ang_m), 1.0)
    sinm = jnp.where(on, jnp.sin(ang_m), 0.0)
    return cosa, sina, cosm, sinm


def _cols_from_chips(g):
    return jnp.concatenate(g, axis=-1)


def _rows_from_chips(g):
    return jnp.concatenate(g, axis=-2)


def _cols_to_chips(w):
    L, r, c4 = w.shape
    return jnp.moveaxis(w.reshape(L, r, 4, c4 // 4), 2, 0)


def _rows_to_chips(w):
    L, r4, c = w.shape
    return jnp.moveaxis(w.reshape(L, 4, r4 // 4, c), 1, 0)


def _layer_layouts(w_in, w_qup, w_kvup, w_o, w_gate, w_up, w_down):
    zpad = lambda n: jnp.zeros((D_MODEL, n), w_in.dtype)
    w_in = jnp.concatenate([w_in[:, :C_KR], zpad(KR_LANE), w_in[:, C_KR:IN_W],
                            zpad(LANES - KR_LANE - MLA_ROPE_DIM)], axis=-1)
    w_qup = w_qup.reshape(MLA_Q_RANK, MLA_HEADS, MLA_QK_DIM)
    w_qup = jnp.pad(w_qup, ((0, 0), (0, 0), (0, LANES - MLA_QK_DIM))).reshape(MLA_Q_RANK, SLOT_W)
    return w_in, w_qup, w_kvup, w_o, w_gate, w_up, w_down


def _local_step(x2, target, meta_full, layer_weights, p, early):
    T = BLOCK + x2.shape[0]
    L = DEPTH
    attn_norm, q_norm, kv_norm, sinks = p["attn_norm"], p["q_norm"], p["kv_norm"], p["sinks"]
    out_norm_swa, out_norm_mla, ffn_norm, final_norm = (
        p["out_norm_swa"], p["out_norm_mla"], p["ffn_norm"], p["final_norm"])

    cosa, sina, cosm, sinm = _rope_tables(T)
    no_sink = jnp.full((MLA_HEADS,), NEG, F32)
    scale_a, scale_b = SWA_HEAD_DIM ** -0.5, MLA_QK_DIM ** -0.5
    row = lambda v: v.reshape(1, -1)

    h = jnp.concatenate([jnp.zeros((FRONT, D_MODEL), F32), meta_full, x2], axis=0)
    saved = []
    weights = []
    for l in range(L):
        weights.append(_layer_layouts(*layer_weights(l, h)))
        W_in, W_qup, W_kvup, W_o, W_gate, W_up, W_down = weights[l]
        u, proj = _norm_proj("in_proj", h, row(attn_norm[l]), [W_in], False)
        qa, ka, va, qn, cn, kr = _prep1(proj, row(q_norm[l]), row(kv_norm[l]), cosa, sina, cosm, sinm)
        qb = _mm("q_up", qn, W_qup, "nn")
        kvb = _mm("kv_up", cn, W_kvup, "nn")
        qs, ks, vs = _prep2(qb, kvb, kr, cosm, sinm)
        oa, lse_a = _attn_fwd("swa_fwd", qa, ka, va, sinks[l], 4, False, scale_a, 4)
        ob, lse_b = _attn_fwd("mla_fwd", qs, ks, vs, no_sink, 1, True, scale_b, 1)
        mix = _merge_fwd(oa, ob, row(out_norm_swa[l]), row(out_norm_mla[l]))
        h1 = _mm("o_proj", mix, W_o, "nn", res=h)
        u2, a, b, hm = _norm_proj("ffn_in", h1, row(ffn_norm[l]), [W_gate, W_up], True)
        h2 = _mm("down_proj", hm, W_down, "nn", res=h1)
        saved.append((h, u, proj, qa, ka, va, qn, cn, qs, ks, vs, oa, lse_a, ob, lse_b, mix, h1, u2, a, b, hm))
        h = h2

    dh, d_final, loss_row = _loss_head(h, target, row(final_norm))

    gw = {k: [None] * L for k in ("in", "qup", "kvup", "o", "gate", "up", "down")}
    gs = {k: [None] * L for k in ("attn", "qn", "kvn", "sink", "ga", "gb", "ffn")}

    def natural_grads(layers):
        st = lambda k: jnp.stack(gw[k][layers])
        d_in = st("in")
        d_in = jnp.concatenate([d_in[..., :C_KR], d_in[..., C_KR + KR_LANE:C_KR + KR_LANE + MLA_ROPE_DIM]], axis=-1)
        d_qup = st("qup")
        d_qup = d_qup.reshape(d_qup.shape[0], MLA_Q_RANK, MLA_HEADS, LANES)[..., :MLA_QK_DIM]
        return [d_in, d_qup.reshape(d_qup.shape[0], MLA_Q_RANK, -1), st("kvup"), st("o"), st("gate"), st("up"),
                st("down")]

    for l in reversed(range(L)):
        (h0, u, proj, qa, ka, va, qn, cn, qs, ks, vs, oa, lse_a, ob, lse_b, mix, h1, u2, a, b, hm) = saved[l]
        W_in, W_qup, W_kvup, W_o, W_gate, W_up, W_down = weights[l]
        if l == 0:
            dh = dh + early(natural_grads(slice(1, L)))[0, 0]
        gw["down"][l] = _mm("down_dw", hm, dh, "tn")
        da, db = _ffn_mid_bwd(dh, W_down, a, b)
        gw["gate"][l] = _mm("gate_dw", u2, da, "tn")
        gw["up"][l] = _mm("up_dw", u2, db, "tn")
        du2 = _mm("gate_dx", da, W_gate, "nt")
        du2 = _mm("up_dx", db, W_up, "nt", res=du2)
        dh1, gs["ffn"][l] = _rmsnorm_bwd("ffn_norm_bwd", h1, row(ffn_norm[l]), du2, dh)
        gw["o"][l] = _mm("o_dw", mix, dh1, "tn")
        dmix = _mm("o_dx", dh1, W_o, "nt")
        sink_slots = jnp.repeat(sinks[l], LANES).reshape(1, SLOT_W)
        doa, dla, dob, dlb, gs["ga"][l], gs["gb"][l], dsink = _merge_bwd(
            dmix, oa, ob, lse_a, row(out_norm_swa[l]), row(out_norm_mla[l]), sink_slots)
        gs["sink"][l] = dsink.reshape(SWA_HEADS, LANES)[:, 0]
        dqs, dks, dvs = _attn_bwd("mla_bwd", qs, ks, vs, dob, lse_b, dlb, 1, True, scale_b)
        dqa, dka, dva = _attn_bwd("swa_bwd", qa, ka, va, doa, lse_a, dla, 4, False, scale_a)
        dqb, dkvb, dkr = _prep2_bwd(dqs, dks, dvs, cosm, sinm)
        gw["qup"][l] = _mm("q_up_dw", qn, dqb, "tn")
        gw["kvup"][l] = _mm("kv_up_dw", cn, dkvb, "tn")
        dqn = _mm("q_up_dx", dqb, W_qup, "nt")
        dcn = _mm("kv_up_dx", dkvb, W_kvup, "nt")
        dproj, gs["qn"][l], gs["kvn"][l] = _prep1_bwd(
            proj, dqa, dka, dva, dqn, dcn, dkr, cosa, sina, row(q_norm[l]), row(kv_norm[l]))
        gw["in"][l] = _mm("in_dw", u, dproj, "tn")
        du = _mm("in_dx", dproj, W_in, "nt")
        dh, gs["attn"][l] = _rmsnorm_bwd("attn_norm_bwd", h0, row(attn_norm[l]), du, dh1)

    return loss_row, dh, natural_grads(slice(0, 1)), gs, d_final


def kernel(x, meta_tokens, attn_norm, w_in, q_norm, w_q_up, kv_norm, w_kv_up, sinks, out_norm_swa, out_norm_mla, w_o, ffn_norm, w_gate, w_up, w_down, final_norm, loss_target, m_meta_tokens, m_attn_norm, m_w_in, m_q_norm, m_w_q_up, m_kv_norm, m_w_kv_up, m_sinks, m_out_norm_swa, m_out_norm_mla, m_w_o, m_ffn_norm, m_w_gate, m_w_up, m_w_down, m_final_norm, v_meta_tokens, v_attn_norm, v_w_in, v_q_norm, v_w_q_up, v_kv_norm, v_w_kv_up, v_sinks, v_out_norm_swa, v_out_norm_mla, v_w_o, v_ffn_norm, v_w_gate, v_w_up, v_w_down, v_final_norm):
    assert x.shape[0] == 1 and x.shape[1] % BLOCK == 0
    big = [w_in, w_q_up, w_kv_up, w_o, w_gate, w_up, w_down]

    c_idx = lax.axis_index("c").astype(jnp.int32)
    chip = (2 * lax.axis_index("x") + lax.axis_index("y")).astype(jnp.int32)
    w16 = [w.astype(BF16) for w in big]
    join = [_cols_from_chips, _cols_from_chips, _cols_from_chips, _rows_from_chips, _cols_from_chips,
            _cols_from_chips, _rows_from_chips]

    def whole(own, landed):
        return [f([jnp.where(chip == j, o, g[j]) for j in range(4)]) for f, o, g in zip(join, own, landed)]

    first = _gather_weights([w[0] for w in w16], meta_tokens)
    meta_full = jnp.concatenate([jnp.where(chip == j, meta_tokens, first[-1][j]) for j in range(4)], axis=-1)
    send, recv, token, thru, lands = _gather_start([w[1:] for w in w16])
    state = {"thru": thru}

    def layer_weights(l, h):
        if l == 0:
            return whole([w[0] for w in w16], first[:-1])
        state["thru"], landed = _gather_wait("gather_wait%d" % l, send, recv, l - 1, state["thru"], lands[l - 1], h)
        return whole([w[l] for w in w16], landed)

    small_p = dict(attn_norm=attn_norm, q_norm=q_norm, kv_norm=kv_norm, sinks=sinks, out_norm_swa=out_norm_swa,
                   out_norm_mla=out_norm_mla, ffn_norm=ffn_norm, final_norm=final_norm)
    split = [_cols_to_chips, _cols_to_chips, _cols_to_chips, _rows_to_chips, _cols_to_chips, _cols_to_chips,
             _rows_to_chips]
    rounds = []

    def reduce_start(name, grads):
        full = [f(d) for f, d in zip(split, grads)]
        got = _pair_exchange([f.astype(BF16) for f in full])
        sums = [_pair_add(f, g, c_idx) for f, g in zip(full, got)]
        send, recv, tok, thru, lands = _scatter_start(name, [s16 for _, s16 in sums])
        rounds.append(([s32 for s32, _ in sums], send, recv, thru, lands))
        return tok

    loss_row, dh, d_first, gs, d_final = _local_step(
        x[0] + token[0, 0], loss_target[0], meta_full, layer_weights, small_p,
        lambda grads: reduce_start("scatter_start_late", grads))
    grad_x = dh[BLOCK:][None]
    reduce_start("scatter_start_first", d_first)

    g_big = [lax.empty(w.shape, F32) for w in big]
    for (s32, send, recv, thru, lands), name, l0 in zip(rounds, ("scatter_wait_late", "scatter_wait_first"), (1, 0)):
        landed = _scatter_wait(name, send, recv, thru, lands, dh)
        g_big = [_chip_add(t, m32, chip, c_idx, acc, l0) for t, m32, acc in zip(landed, s32, g_big)]
    g_big = _pair_join(g_big)

    small = [jnp.stack(gs["attn"]).reshape(-1), jnp.stack(gs["qn"]).reshape(-1), jnp.stack(gs["kvn"]).reshape(-1),
             jnp.stack(gs["sink"]).reshape(-1), jnp.stack(gs["ga"]).reshape(-1), jnp.stack(gs["gb"]).reshape(-1),
             jnp.stack(gs["ffn"]).reshape(-1), d_final.reshape(-1)]
    sizes = [s.shape[0] for s in small]
    flat = jnp.concatenate(small + [dh[FRONT:BLOCK].reshape(-1), loss_row[0, :1]])
    n_flat = flat.shape[0]
    rows_needed = -(-n_flat // (8 * LANES)) * 8
    flat = jnp.pad(flat, (0, rows_needed * LANES - n_flat)).reshape(rows_needed, LANES)
    tot = _allreduce_small(flat).reshape(-1)
    n_small = sum(sizes)
    loss = tot[n_small + N_META * D_MODEL]
    g_meta_full = tot[n_small:n_small + N_META * D_MODEL].reshape(N_META, D_MODEL)
    g_meta_mine = lax.dynamic_slice_in_dim(g_meta_full, chip * (D_MODEL // 4), D_MODEL // 4, axis=1)

    small_w = [attn_norm, q_norm, kv_norm, sinks, out_norm_swa, out_norm_mla, ffn_norm, final_norm]
    small_m = [m_attn_norm, m_q_norm, m_kv_norm, m_sinks, m_out_norm_swa, m_out_norm_mla, m_ffn_norm, m_final_norm]
    small_v = [v_attn_norm, v_q_norm, v_kv_norm, v_sinks, v_out_norm_swa, v_out_norm_mla, v_ffn_norm, v_final_norm]
    n_rows = -(-n_small // (8 * LANES)) * 8

    def pack(arrs):
        f = jnp.concatenate([a.reshape(-1) for a in arrs])
        return jnp.pad(f, (0, n_rows * LANES - n_small), constant_values=1.0).reshape(n_rows, LANES)

    g_small_pack = jnp.pad(tot[:n_small], (0, n_rows * LANES - n_small)).reshape(n_rows, LANES)
    upd_small = _adamw("adam_small", pack(small_w), g_small_pack, pack(small_m), pack(small_v))

    def unpack(p):
        f = p.reshape(-1)
        out, off = [], 0
        for a, n in zip(small_w, sizes):
            out.append(f[off:off + n].reshape(a.shape))
            off += n
        return out

    g_small = unpack(g_small_pack)
    d_small, m_small, v_small = [unpack(p) for p in upd_small]
    d_meta, nm_meta, nv_meta = _adamw("adam_meta", meta_tokens, g_meta_mine, m_meta_tokens, v_meta_tokens)

    big_m = [m_w_in, m_w_q_up, m_w_kv_up, m_w_o, m_w_gate, m_w_up, m_w_down]
    big_v = [v_w_in, v_w_q_up, v_w_kv_up, v_w_o, v_w_gate, v_w_up, v_w_down]
    upd_big = [_adamw("adam_big", w, g, m, v) for w, g, m, v in zip(big, g_big, big_m, big_v)]

    names = ["meta_tokens", "attn_norm", "w_in", "q_norm", "w_q_up", "kv_norm", "w_kv_up", "sinks",
             "out_norm_swa", "out_norm_mla", "w_o", "ffn_norm", "w_gate", "w_up", "w_down", "final_norm"]
    small_idx = {"attn_norm": 0, "q_norm": 1, "kv_norm": 2, "sinks": 3, "out_norm_swa": 4,
                 "out_norm_mla": 5, "ffn_norm": 6, "final_norm": 7}
    big_idx = {"w_in": 0, "w_q_up": 1, "w_kv_up": 2, "w_o": 3, "w_gate": 4, "w_up": 5, "w_down": 6}
    grads, deltas, new_m, new_v = [], [], [], []
    for nme in names:
        if nme == "meta_tokens":
            quad = (g_meta_mine, d_meta, nm_meta, nv_meta)
        elif nme in small_idx:
            i = small_idx[nme]
            quad = (g_small[i], d_small[i], m_small[i], v_small[i])
        else:
            i = big_idx[nme]
            quad = (g_big[i], *upd_big[i])
        grads.append(quad[0]); deltas.append(quad[1]); new_m.append(quad[2]); new_v.append(quad[3])
    return (loss, grad_x, *grads, *deltas, *new_m, *new_v)
```

```python
import jax
import jax.numpy as jnp
from jax import lax
from jax.experimental import pallas as pl
from jax.experimental.pallas import tpu as pltpu

F32 = jnp.float32
BF16 = jnp.bfloat16
MXU_DTYPE = BF16

D_MODEL = 1024
DEPTH = 4
N_META = 16
BLOCK = 128
WINDOW = 128
ROPE_THETA = 10000.0
EPS = 1e-6
NEG = -1e30
SWA_HEADS = 8
SWA_KV_HEADS = 2
SWA_HEAD_DIM = 64
MLA_HEADS = 8
MLA_Q_RANK = 256
MLA_KV_RANK = 128
MLA_NOPE_DIM = 64
MLA_ROPE_DIM = 32
MLA_V_DIM = 64
MLA_QK_DIM = MLA_NOPE_DIM + MLA_ROPE_DIM
D_FF = 2816
FRONT = (-N_META) % BLOCK
LANES = 128
SLOT_W = 8 * LANES
C_QA, C_KA, C_VA, C_QL, C_KL, C_KR, IN_WP = 0, 512, 640, 768, 1024, 1152, 1280
KR_LANE = 64
IN_W = 1184

ADAM_LR, ADAM_B1, ADAM_B2, ADAM_EPS, ADAM_WD, ADAM_STEP = 0.001, 0.9, 0.999, 1e-08, 0.01, 10

VMEM_LIMIT = 48 * 1024 * 1024
MESH = pl.DeviceIdType.MESH


def _tile(n, prefs):
    for t in prefs:
        if n % t == 0:
            return t
    return n


def _params(sem):
    return pltpu.CompilerParams(dimension_semantics=sem, vmem_limit_bytes=VMEM_LIMIT)


_DIMS = {"nn": (((1,), (0,)), ((), ())), "nt": (((1,), (1,)), ((), ())), "tn": (((0,), (0,)), ((), ()))}


def _mm(name, a, b, mode, out_dtype=F32, res=None):
    if mode == "nn":
        (M, K), (_, N) = a.shape, b.shape
    elif mode == "nt":
        (M, K), (N, _) = a.shape, b.shape
    else:
        (K, M), (_, N) = a.shape, b.shape
    lane_tiles = (1408, 1024, 640, 768, 512, 384, 256, 128)
    row_tiles = (1056, 528, 512, 384, 256, 128)
    bm = _tile(M, lane_tiles if mode == "tn" else row_tiles)
    bn = _tile(N, lane_tiles)
    bk = _tile(K, row_tiles if mode == "tn" else lane_tiles)
    nk = K // bk
    if mode == "tn":
        a_spec = pl.BlockSpec((bk, bm), lambda i, j, k: (k, i))
    else:
        a_spec = pl.BlockSpec((bm, bk), lambda i, j, k: (i, k))
    if mode == "nt":
        b_spec = pl.BlockSpec((bn, bk), lambda i, j, k: (j, k))
    else:
        b_spec = pl.BlockSpec((bk, bn), lambda i, j, k: (k, j))
    o_spec = pl.BlockSpec((bm, bn), lambda i, j, k: (i, j))
    in_specs = [a_spec, b_spec]
    args = [a, b]
    if res is not None:
        in_specs.append(o_spec)
        args.append(res)
    dims = _DIMS[mode]

    def kern(a_ref, b_ref, *rest):
        if res is not None:
            r_ref, o_ref = rest[0], rest[1]
            scr = rest[2:]
        else:
            r_ref, o_ref = None, rest[0]
            scr = rest[1:]
        p = lax.dot_general(a_ref[...].astype(MXU_DTYPE), b_ref[...].astype(MXU_DTYPE), dims,
                            preferred_element_type=F32)

        def finish(val):
            if r_ref is not None:
                val = val + r_ref[...]
            o_ref[...] = val.astype(o_ref.dtype)

        if nk == 1:
            finish(p)
        else:
            acc = scr[0]
            k = pl.program_id(2)

            @pl.when(k == 0)
            def _():
                acc[...] = p

            @pl.when(k > 0)
            def _():
                acc[...] += p

            @pl.when(k == nk - 1)
            def _():
                finish(acc[...])

    return pl.pallas_call(
        kern, name=name,
        out_shape=jax.ShapeDtypeStruct((M, N), out_dtype),
        grid=(M // bm, N // bn, nk),
        in_specs=in_specs, out_specs=o_spec,
        scratch_shapes=[pltpu.VMEM((bm, bn), F32)] if nk > 1 else [],
        compiler_params=_params(("parallel", "parallel", "arbitrary")),
    )(*args)


def _rowmap(name, body, rows, vecs, outs, accs=(), tr_prefs=(384, 256, 128)):
    R = rows[0].shape[0]
    tr = _tile(R, tr_prefs)
    n_r, n_v, n_o, n_a = len(rows), len(vecs), len(outs), len(accs)

    def kern(*refs):
        ins = [r[...] for r in refs[:n_r + n_v]]
        o_refs = refs[n_r + n_v:n_r + n_v + n_o]
        a_refs = refs[n_r + n_v + n_o:]
        res = body(*ins)
        for o_ref, val in zip(o_refs, res[:n_o]):
            o_ref[...] = val.astype(o_ref.dtype)
        if n_a:
            first = pl.program_id(0) == 0

            @pl.when(first)
            def _():
                for a_ref, val in zip(a_refs, res[n_o:]):
                    a_ref[...] = val

            @pl.when(jnp.logical_not(first))
            def _():
                for a_ref, val in zip(a_refs, res[n_o:]):
                    a_ref[...] += val

    in_specs = [pl.BlockSpec((tr, r.shape[1]), lambda i: (i, 0)) for r in rows]
    in_specs += [pl.BlockSpec((1, v.shape[1]), lambda i: (0, 0)) for v in vecs]
    out_specs = [pl.BlockSpec((tr, c), lambda i: (i, 0)) for c, _ in outs]
    out_specs += [pl.BlockSpec((1, c), lambda i: (0, 0)) for c in accs]
    out_shape = [jax.ShapeDtypeStruct((R, c), dt) for c, dt in outs]
    out_shape += [jax.ShapeDtypeStruct((1, c), F32) for c in accs]
    return pl.pallas_call(
        kern, name=name, out_shape=out_shape, grid=(R // tr,),
        in_specs=in_specs, out_specs=out_specs,
        compiler_params=_params(("arbitrary",) if n_a else ("parallel",)),
    )(*rows, *vecs)


def _lane(shape):
    return lax.broadcasted_iota(jnp.int32, shape, 1)


def _rot_swa(x):
    lane = _lane(x.shape)
    return jnp.where((lane & 63) < 32, -pltpu.roll(x, 96, 1), pltpu.roll(x, 32, 1))


def _rot_mla(x):
    lane = _lane(x.shape)
    lo = jnp.where(lane >= KR_LANE, -pltpu.roll(x, 112, 1), 0.0)
    hi = jnp.where(lane < KR_LANE + MLA_ROPE_DIM, pltpu.roll(x, 16, 1), 0.0)
    return jnp.where(lane < KR_LANE + 16, lo, hi)


def _rope(x, cos, sin, rot):
    return x * cos + rot(x) * sin


def _rope_t(g, cos, sin, rot):
    return g * cos - rot(g * sin)


def _low(x):
    return jnp.where(_lane(x.shape) < 64, x, 0.0)


def _value_slot(x):
    lane = _lane(x.shape)
    return jnp.where(lane < 64, x, jnp.where(lane == 64, 1.0, 0.0))


def _blk(x, j):
    return x[:, j * LANES:(j + 1) * LANES]


def _rms_r(x, width):
    return lax.rsqrt(jnp.sum(x * x, axis=-1, keepdims=True) * (1.0 / width) + EPS)


def _rms_bwd(x, g, dy, width):
    r = _rms_r(x, width)
    gdy = dy * g
    dot = jnp.sum(gdy * x, axis=-1, keepdims=True)
    dx = r * gdy - x * (r * r * r * (1.0 / width) * dot)
    return dx, dy * x * r


def _colsum(x):
    return jnp.sum(x, axis=0, keepdims=True)


def _rmsnorm_bwd(name, x, g, dy, dres):
    def body(xv, dyv, dr, gv):
        dx, dg = _rms_bwd(xv, gv, dyv, D_MODEL)
        return dx + dr, _colsum(dg)
    return _rowmap(name, body, [x, dy, dres], [g], [(D_MODEL, F32)], [D_MODEL])


def _prep1(proj, qn_g, kv_g, cosa, sina, cosm, sinm):
    def body(p, ca, sa, cm, sm, gq, gk):
        qa = []
        for j in range(4):
            xr = _rope(_blk(p, j), ca, sa, _rot_swa)
            qa += [_low(xr), _low(pltpu.roll(xr, 64, 1))]
        kr_ = _rope(_blk(p, C_KA // LANES), ca, sa, _rot_swa)
        ka = [_low(kr_), _low(pltpu.roll(kr_, 64, 1))]
        vv = _blk(p, C_VA // LANES)
        va = [_value_slot(vv), _value_slot(pltpu.roll(vv, 64, 1))]
        ql = p[:, C_QL:C_QL + MLA_Q_RANK]
        qn = ql * _rms_r(ql, MLA_Q_RANK) * gq
        kl = p[:, C_KL:C_KL + MLA_KV_RANK]
        cn = kl * _rms_r(kl, MLA_KV_RANK) * gk
        kr = _rope(_blk(p, C_KR // LANES), cm, sm, _rot_mla)
        return (jnp.concatenate(qa, 1), jnp.concatenate(ka, 1), jnp.concatenate(va, 1), qn, cn, kr)
    return _rowmap("prep1", body, [proj, cosa, sina, cosm, sinm], [qn_g, kv_g],
                   [(SLOT_W, BF16), (2 * LANES, BF16), (2 * LANES, BF16),
                    (MLA_Q_RANK, BF16), (MLA_KV_RANK, BF16), (LANES, F32)])


def _prep2(qb, kvb, kr, cosm, sinm):
    def body(q, kv, krv, cm, sm):
        qs, ks, vs = [], [], []
        for h in range(MLA_HEADS):
            qs.append(_rope(_blk(q, h), cm, sm, _rot_mla))
            kvh = _blk(kv, h)
            ks.append(_low(kvh) + krv)
            vs.append(_value_slot(pltpu.roll(kvh, 64, 1)))
        return jnp.concatenate(qs, 1), jnp.concatenate(ks, 1), jnp.concatenate(vs, 1)
    return _rowmap("prep2", body, [qb, kvb, kr, cosm, sinm], [],
                   [(SLOT_W, BF16), (SLOT_W, BF16), (SLOT_W, BF16)])


def _compact(slots):
    return jnp.concatenate(
        [_blk(slots, 2 * j) + pltpu.roll(_blk(slots, 2 * j + 1), 64, 1) for j in range(4)], 1)


def _expand(nat):
    out = []
    for j in range(4):
        b = _blk(nat, j)
        out += [_low(b), _low(pltpu.roll(b, 64, 1))]
    return jnp.concatenate(out, 1)


def _merge_fwd(oa, ob, ga, gb):
    def body(a, b, gav, gbv):
        xa, xb = _compact(a), _compact(b)
        return (jnp.concatenate([xa * _rms_r(xa, 512) * gav, xb * _rms_r(xb, 512) * gbv], 1),)
    return _rowmap("merge_fwd", body, [oa, ob], [ga, gb], [(D_MODEL, BF16)])[0]


def _merge_bwd(dmix, oa, ob, lse_a, ga, gb, sink_slots):
    def body(dm, a, b, lse, gav, gbv, sk):
        outs = []
        accs = []
        for o, g, lo in ((a, gav, 0), (b, gbv, 512)):
            x = _compact(o)
            dx, dg = _rms_bwd(x, g, dm[:, lo:lo + 512], 512)
            do = _expand(dx)
            delta = jnp.concatenate(
                [jnp.broadcast_to(jnp.sum(_blk(do, h) * _blk(o, h), axis=-1, keepdims=True),
                                  (do.shape[0], LANES)) for h in range(8)], 1)
            outs += [do, delta]
            accs.append(_colsum(dg))
        dsink = _colsum(-jnp.exp(sk - lse) * outs[1])
        return (*outs, *accs, dsink)
    return _rowmap("merge_bwd", body, [dmix, oa, ob, lse_a], [ga, gb, sink_slots],
                   [(SLOT_W, BF16), (SLOT_W, F32), (SLOT_W, BF16), (SLOT_W, F32)],
                   [512, 512, SLOT_W])


def _prep2_bwd(dq, dk, dv, cosm, sinm):
    def body(dqv, dkv, dvv, cm, sm):
        dqv, dkv, dvv = dqv.astype(F32), dkv.astype(F32), dvv.astype(F32)
        dqb, dkvb = [], []
        krsum = jnp.zeros((dqv.shape[0], LANES), F32)
        for h in range(MLA_HEADS):
            dqb.append(_rope_t(_blk(dqv, h), cm, sm, _rot_mla))
            dkh = _blk(dkv, h)
            dkvb.append(_low(dkh) + pltpu.roll(_blk(dvv, h), 64, 1))
            krsum = krsum + dkh
        lane = _lane(krsum.shape)
        dkr = jnp.where((lane >= KR_LANE) & (lane < KR_LANE + MLA_ROPE_DIM),
                        _rope_t(krsum, cm, sm, _rot_mla), 0.0)
        return jnp.concatenate(dqb, 1), jnp.concatenate(dkvb, 1), dkr
    return _rowmap("prep2_bwd", body, [dq, dk, dv, cosm, sinm], [],
                   [(SLOT_W, BF16), (SLOT_W, BF16), (LANES, F32)])


def _prep1_bwd(proj, dqa, dka, dva, dqn, dcn, dkr, cosa, sina, qn_g, kv_g):
    def body(p, dq, dk, dv, dqnv, dcnv, dkrv, ca, sa, gq, gk):
        dq, dk, dv = dq.astype(F32), dk.astype(F32), dv.astype(F32)
        cols = []
        for j in range(4):
            nat = _blk(dq, 2 * j) + pltpu.roll(_blk(dq, 2 * j + 1), 64, 1)
            cols.append(_rope_t(nat, ca, sa, _rot_swa))
        grp = lambda d, g: sum(_blk(d, 4 * g + i) for i in range(4))
        cols.append(_rope_t(grp(dk, 0) + pltpu.roll(grp(dk, 1), 64, 1), ca, sa, _rot_swa))
        cols.append(grp(dv, 0) + pltpu.roll(grp(dv, 1), 64, 1))
        dql, dgq = _rms_bwd(p[:, C_QL:C_QL + MLA_Q_RANK], gq, dqnv, MLA_Q_RANK)
        dkl, dgk = _rms_bwd(p[:, C_KL:C_KL + MLA_KV_RANK], gk, dcnv, MLA_KV_RANK)
        cols += [dql, dkl, dkrv]
        return jnp.concatenate(cols, 1), _colsum(dgq), _colsum(dgk)
    return _rowmap("prep1_bwd", body, [proj, dqa, dka, dva, dqn, dcn, dkr, cosa, sina], [qn_g, kv_g],
                   [(IN_WP, BF16)], [MLA_Q_RANK, MLA_KV_RANK], tr_prefs=(192, 128))


def _sigmoid(x):
    return 1.0 / (1.0 + jnp.exp(-x))


def _ffn_tiles(T, F):
    return _tile(T, (528, 512, 384, 256, 128)), _tile(F, (1408, 1024, 640, 512, 256, 128))


def _norm_proj(name, h, g, weights, swiglu):
    (T, D), F = h.shape, weights[0].shape[1]
    bm, bn = _ffn_tiles(T, F)
    nw = len(weights)

    def kern(h_ref, g_ref, *rest):
        w_refs, u_ref, o_refs, u_scr = rest[:nw], rest[nw], rest[nw + 1:-1], rest[-1]

        @pl.when(pl.program_id(1) == 0)
        def _():
            x = h_ref[...]
            u = (x * _rms_r(x, D) * g_ref[...]).astype(u_scr.dtype)
            u_scr[...] = u
            u_ref[...] = u

        uv = u_scr[...]
        prods = [jnp.dot(uv, w[...], preferred_element_type=F32) for w in w_refs]
        for o_ref, p in zip(o_refs, prods):
            o_ref[...] = p
        if swiglu:
            a, b = prods
            o_refs[nw][...] = (a * _sigmoid(a) * b).astype(o_refs[nw].dtype)

    w_spec = pl.BlockSpec((D, bn), lambda i, j: (0, j))
    row_spec = pl.BlockSpec((bm, D), lambda i, j: (i, 0))
    o_spec = pl.BlockSpec((bm, bn), lambda i, j: (i, j))
    n_out = nw + (1 if swiglu else 0)
    return pl.pallas_call(
        kern, name=name,
        out_shape=[jax.ShapeDtypeStruct((T, D), MXU_DTYPE)] + [jax.ShapeDtypeStruct((T, F), F32)] * nw
        + ([jax.ShapeDtypeStruct((T, F), MXU_DTYPE)] if swiglu else []),
        grid=(T // bm, F // bn),
        in_specs=[row_spec, pl.BlockSpec((1, D), lambda i, j: (0, 0))] + [w_spec] * nw,
        out_specs=[row_spec] + [o_spec] * n_out,
        scratch_shapes=[pltpu.VMEM((bm, D), MXU_DTYPE)],
        compiler_params=_params(("parallel", "arbitrary")),
    )(h, g, *weights)


def _ffn_mid_bwd(dh, w_down, a, b):
    (T, D), F = dh.shape, w_down.shape[0]
    bm, bn = _ffn_tiles(T, F)

    def kern(dh_ref, wd_ref, a_ref, b_ref, da_ref, db_ref):
        d = lax.dot_general(dh_ref[...].astype(MXU_DTYPE), wd_ref[...], _DIMS["nt"], preferred_element_type=F32)
        av, bv = a_ref[...], b_ref[...]
        s = _sigmoid(av)
        da_ref[...] = (d * bv * (s * (1.0 + av * (1.0 - s)))).astype(da_ref.dtype)
        db_ref[...] = (d * (av * s)).astype(db_ref.dtype)

    o_spec = pl.BlockSpec((bm, bn), lambda i, j: (i, j))
    return pl.pallas_call(
        kern, name="ffn_mid_bwd",
        out_shape=[jax.ShapeDtypeStruct((T, F), MXU_DTYPE)] * 2,
        grid=(T // bm, F // bn),
        in_specs=[pl.BlockSpec((bm, D), lambda i, j: (i, 0)), pl.BlockSpec((bn, D), lambda i, j: (j, 0)),
                  o_spec, o_spec],
        out_specs=[o_spec, o_spec],
        compiler_params=_params(("parallel", "parallel")),
    )(dh, w_down, a, b)


def _loss_head(h, target, g):
    T = h.shape[0]
    nb = T // BLOCK

    def kern(h_ref, t_ref, g_ref, dh_ref, dg_ref, loss_ref, acc):
        i = pl.program_id(0)

        @pl.when(i == 0)
        def _():
            dh_ref[...] = jnp.zeros_like(dh_ref)
            dg_ref[...] = jnp.zeros_like(dg_ref)
            acc[...] = jnp.zeros_like(acc)

        @pl.when(i > 0)
        def _():
            x = h_ref[...]
            gv = g_ref[...]
            e = x * _rms_r(x, D_MODEL) * gv - t_ref[...]
            acc[...] += _colsum(e * e)
            dx, dg = _rms_bwd(x, gv, e * (1.0 / D_MODEL), D_MODEL)
            dh_ref[...] = dx
            dg_ref[...] += _colsum(dg)

        @pl.when(i == nb - 1)
        def _():
            tot = jnp.sum(acc[...], axis=-1, keepdims=True) * (0.5 / D_MODEL)
            loss_ref[...] = jnp.broadcast_to(tot, loss_ref.shape)

    return pl.pallas_call(
        kern, name="loss_head",
        out_shape=[jax.ShapeDtypeStruct((T, D_MODEL), F32), jax.ShapeDtypeStruct((1, D_MODEL), F32),
                   jax.ShapeDtypeStruct((1, LANES), F32)],
        grid=(nb,),
        in_specs=[pl.BlockSpec((BLOCK, D_MODEL), lambda i: (i, 0)),
                  pl.BlockSpec((BLOCK, D_MODEL), lambda i: (jnp.maximum(i - 1, 0), 0)),
                  pl.BlockSpec((1, D_MODEL), lambda i: (0, 0))],
        out_specs=[pl.BlockSpec((BLOCK, D_MODEL), lambda i: (i, 0)),
                   pl.BlockSpec((1, D_MODEL), lambda i: (0, 0)),
                   pl.BlockSpec((1, LANES), lambda i: (0, 0))],
        scratch_shapes=[pltpu.VMEM((1, D_MODEL), F32)],
        compiler_params=_params(("arbitrary",)),
    )(h, target, g)


LOG2E = 1.4426950408889634


def _attn_plan(T, causal):
    tq = _tile(T, (384, 256, 128))
    ck = min(2 * tq, T) if causal else min(tq + WINDOW, T)
    return tq, ck, (-(-T // ck) if causal else 1)


def _chunk(i, c, T, tq, ck, causal):
    if causal:
        return pl.multiple_of(jnp.minimum(c * ck, T - ck), LANES), c * ck
    return pl.multiple_of(jnp.clip(i * tq - WINDOW, 0, T - ck), LANES), 0


def _n_chunks(i, tq, ck, causal):
    return ((i + 1) * tq + ck - 1) // ck if causal else 1


def _mask(s, i, start, first, tq, ck, causal):
    qpos = i * tq + lax.broadcasted_iota(jnp.int32, (tq, 1), 0)
    kpos = start + lax.broadcasted_iota(jnp.int32, (tq, ck), 1)
    low = jnp.maximum(jnp.where(qpos < FRONT, 0, FRONT), first)
    if not causal:
        low = jnp.maximum(low, qpos - (WINDOW - 1))
    return jnp.where(kpos >= low, jnp.where(kpos <= qpos, s, NEG), NEG)


def _chunk_loop(n, body, init, several):
    carry = body(0, init, True)
    if not several:
        return carry
    carry = body(n - 1, carry, True)
    return lax.fori_loop(1, n - 1, lambda c, cr: body(c, cr, False), carry)


def _tile_sweep(nq, tq, ck, causal, q_tiles, pair=False):
    if pair and nq >= 2 and (not causal or ck == 2 * tq):
        if causal:
            q_tiles([0, 1], False)
            lax.fori_loop(1, nq // 2, lambda p, cr: q_tiles([2 * p, 2 * p + 1], True) or cr, 0)
        else:
            lax.fori_loop(0, nq // 2, lambda p, cr: q_tiles([2 * p, 2 * p + 1], False) or cr, 0)
        if nq % 2:
            q_tiles([nq - 1], causal and nq >= 3)
        return
    one = min(nq, ck // tq) if causal else nq
    lax.fori_loop(0, one, lambda i, cr: q_tiles([i], False) or cr, 0)
    if one < nq:
        lax.fori_loop(one, nq, lambda i, cr: q_tiles([i], True) or cr, 0)


BWD_HEADS_PER_STEP = 1


def _head_cols(group, hp):
    q_cols = lambda hh: slice(hh * LANES, (hh + 1) * LANES)
    if group == 1:
        return q_cols, q_cols, hp * LANES
    assert group % hp == 0
    return q_cols, (lambda hh: slice(0, LANES)), LANES


def _whole(T, width, index, single):
    if single:
        return pl.BlockSpec((T, width), index, pipeline_mode=pl.Buffered(1))
    return pl.BlockSpec((T, width), index)


def _attn_fwd(name, q, k, v, sinks, group, causal, scale, hp):
    T = q.shape[0]
    H = q.shape[1] // LANES
    HP = hp
    tq, ck, slots = _attn_plan(T, causal)
    nq = T // tq
    c2 = scale * LOG2E
    q_cols, k_cols, kw = _head_cols(group, HP)

    def kern(sink_ref, q_ref, k_ref, v_ref, o_ref, lse_ref, s_scr):
        sink2 = [sink_ref[pl.program_id(0) * HP + hh] * LOG2E for hh in range(HP)]

        def q_tiles(tiles, several):
            rows = [pl.ds(pl.multiple_of(i * tq, tq), tq) for i in tiles]
            chains = [(t, hh) for t in range(len(tiles)) for hh in range(HP)]
            qq = [q_ref[rows[t], q_cols(hh)] for t, hh in chains]
            n = _n_chunks(tiles[0], tq, ck, causal)

            def score(c, m2, masked):
                out = []
                for n_ch, (t, hh) in enumerate(chains):
                    start, first = _chunk(tiles[t], c, T, tq, ck, causal)
                    s = lax.dot_general(qq[n_ch], k_ref[pl.ds(start, ck), k_cols(hh)], _DIMS["nt"],
                                        preferred_element_type=F32) * c2
                    if masked:
                        s = _mask(s, tiles[t], start, first, tq, ck, causal)
                    s_scr[n_ch, c] = s
                    out.append(jnp.maximum(m2[n_ch], jnp.max(s, axis=-1, keepdims=True)))
                return tuple(out)

            m2 = _chunk_loop(n, score, tuple(jnp.full((tq, 1), sink2[hh], F32) for _, hh in chains), several)

            def weigh(c, acc):
                out = []
                for n_ch, (t, hh) in enumerate(chains):
                    start, _ = _chunk(tiles[t], c, T, tq, ck, causal)
                    p = jnp.exp2(s_scr[n_ch, c] - m2[n_ch])
                    out.append(acc[n_ch] + jnp.dot(p.astype(MXU_DTYPE), v_ref[pl.ds(start, ck), k_cols(hh)],
                                                   preferred_element_type=F32))
                return tuple(out)

            acc = lax.fori_loop(0, n, weigh, tuple(jnp.zeros((tq, LANES), F32) for _ in chains))
            lane = _lane((tq, LANES))
            for n_ch, (t, hh) in enumerate(chains):
                l = acc[n_ch][:, 64:65] + jnp.exp2(sink2[hh] - m2[n_ch])
                o_ref[rows[t], q_cols(hh)] = jnp.where(lane < 64, acc[n_ch] / l, 0.0)
                lse_ref[rows[t], q_cols(hh)] = jnp.broadcast_to(m2[n_ch] * (1.0 / LOG2E) + jnp.log(l), (tq, LANES))

        _tile_sweep(nq, tq, ck, causal, q_tiles, pair=True)

    q_spec = _whole(T, HP * LANES, lambda g: (0, g), HP > 1)
    kv_spec = _whole(T, kw, (lambda g: (0, g)) if group == 1 else (lambda g: (0, (g * HP) // group)), HP > 1)
    return pl.pallas_call(
        kern, name=name,
        out_shape=[jax.ShapeDtypeStruct((T, H * LANES), F32)] * 2,
        grid=(H // HP,),
        in_specs=[pl.BlockSpec(memory_space=pltpu.SMEM), q_spec, kv_spec, kv_spec],
        out_specs=[q_spec, q_spec],
        scratch_shapes=[pltpu.VMEM((2 * HP, slots, tq, ck), F32)],
        compiler_params=_params(("parallel",)),
    )(sinks, q, k, v)


def _attn_bwd(name, q, k, v, do, lse, delta, group, causal, scale):
    T = q.shape[0]
    H = q.shape[1] // LANES
    HP = BWD_HEADS_PER_STEP
    tq, ck, _ = _attn_plan(T, causal)
    nq = T // tq
    c2 = scale * LOG2E
    q_cols, k_cols, kw = _head_cols(group, HP)

    def kern(q_ref, k_ref, v_ref, do_ref, lse_ref, dl_ref, dq_ref, dk_ref, dv_ref, dk_acc, dv_acc):
        dk_acc[...] = jnp.zeros_like(dk_acc)
        dv_acc[...] = jnp.zeros_like(dv_acc)

        def q_tiles(tiles, several):
            rows = [pl.ds(pl.multiple_of(i * tq, tq), tq) for i in tiles]
            chains = [(t, hh) for t in range(len(tiles)) for hh in range(HP)]
            qq = [q_ref[rows[t], q_cols(hh)] for t, hh in chains]
            dd = [do_ref[rows[t], q_cols(hh)] for t, hh in chains]
            lse2 = [lse_ref[rows[t], q_cols(hh)][:, 0:1] * LOG2E for t, hh in chains]
            dl_c = [dl_ref[rows[t], q_cols(hh)][:, 0:1] for t, hh in chains]

            def chunk(c, dq, masked):
                out = []
                for n_ch, (t, hh) in enumerate(chains):
                    start, first = _chunk(tiles[t], c, T, tq, ck, causal)
                    keys = pl.ds(start, ck)
                    kk, vv = k_ref[keys, k_cols(hh)], v_ref[keys, k_cols(hh)]
                    s = lax.dot_general(qq[n_ch], kk, _DIMS["nt"], preferred_element_type=F32)
                    if masked:
                        s = _mask(s, tiles[t], start, first, tq, ck, causal)
                    p = jnp.exp2(s * c2 - lse2[n_ch])
                    dv_acc[keys, q_cols(hh)] += lax.dot_general(p.astype(MXU_DTYPE), dd[n_ch], _DIMS["tn"],
                                                                preferred_element_type=F32)
                    dp = lax.dot_general(dd[n_ch], vv, _DIMS["nt"], preferred_element_type=F32)
                    ds = (p * (dp - dl_c[n_ch])).astype(MXU_DTYPE)
                    dk_acc[keys, q_cols(hh)] += lax.dot_general(ds, qq[n_ch], _DIMS["tn"],
                                                                preferred_element_type=F32) * scale
                    out.append(dq[n_ch] + jnp.dot(ds, kk, preferred_element_type=F32))
                return tuple(out)

            dq = _chunk_loop(_n_chunks(tiles[0], tq, ck, causal), chunk,
                             tuple(jnp.zeros((tq, LANES), F32) for _ in chains), several)
            for n_ch, (t, hh) in enumerate(chains):
                dq_ref[rows[t], q_cols(hh)] = (dq[n_ch] * scale).astype(dq_ref.dtype)

        _tile_sweep(nq, tq, ck, causal, q_tiles, pair=True)
        dk_ref[...] = dk_acc[...].astype(dk_ref.dtype)
        dv_ref[...] = dv_acc[...].astype(dv_ref.dtype)

    q_spec = _whole(T, HP * LANES, lambda g: (0, g), False)
    kv_spec = _whole(T, kw, (lambda g: (0, g)) if group == 1 else (lambda g: (0, (g * HP) // group)), False)
    return pl.pallas_call(
        kern, name=name,
        out_shape=[jax.ShapeDtypeStruct((T, H * LANES), MXU_DTYPE)] * 3,
        grid=(H // HP,),
        in_specs=[q_spec, kv_spec, kv_spec, q_spec, q_spec, q_spec],
        out_specs=[q_spec, q_spec, q_spec],
        scratch_shapes=[pltpu.VMEM((T, HP * LANES), F32)] * 2,
        compiler_params=_params(("parallel",)),
    )(q, k, v, do, lse, delta)


def _ew(name, fn, ins, out_dtypes):
    shape = ins[0].shape
    flat = [a.reshape(-1, shape[-1]) for a in ins]
    R, C = flat[0].shape
    tr = _tile(R, (512, 256, 128, 64, 32, 16, 8))
    n_in = len(ins)

    def kern(*refs):
        res = fn(*[r[...] for r in refs[:n_in]])
        for o_ref, val in zip(refs[n_in:], res):
            o_ref[...] = val.astype(o_ref.dtype)

    spec = pl.BlockSpec((tr, C), lambda i: (i, 0))
    outs = pl.pallas_call(
        kern, name=name,
        out_shape=[jax.ShapeDtypeStruct((R, C), dt) for dt in out_dtypes],
        grid=(R // tr,), in_specs=[spec] * n_in, out_specs=[spec] * len(out_dtypes),
        compiler_params=_params(("parallel",)),
    )(*flat)
    return [o.reshape(shape) for o in outs]


def _adamw(name, w, g, m, v):
    c1 = 1.0 - ADAM_B1 ** ADAM_STEP
    c2 = 1.0 - ADAM_B2 ** ADAM_STEP

    def fn(wv, gv, mv, vv):
        mn = ADAM_B1 * mv + (1.0 - ADAM_B1) * gv
        vn = ADAM_B2 * vv + (1.0 - ADAM_B2) * (gv * gv)
        delta = -ADAM_LR * ((mn / c1) / (jnp.sqrt(vn / c2) + ADAM_EPS) + ADAM_WD * wv)
        return delta, mn, vn

    return _ew(name, fn, [w, g, m, v], [F32, F32, F32])


_ANY = pl.BlockSpec(memory_space=pl.ANY)


def _where_am_i():
    x, y, c = lax.axis_index("x"), lax.axis_index("y"), lax.axis_index("c")
    chips = [(1 - x, y), (x, 1 - y), (1 - x, 1 - y)]
    return x, y, c, chips


def _gather_weights(shards, meta):
    arrs = list(shards) + [meta]
    n = len(arrs)
    per = [a.shape[0] // 2 for a in arrs]

    def body(*refs):
        ins, outs = refs[:n], refs[n:2 * n]
        send1, recv1, send2, recv2 = refs[2 * n:]
        x, y, c, chips = _where_am_i()
        me = 2 * x + y

        def half(ref, k, cc):
            return ref.at[pl.ds(per[k] * cc, per[k])]

        first = []
        for k in range(n):
            for j, (cx, cy) in enumerate(chips):
                first.append(pltpu.make_async_remote_copy(
                    src_ref=half(ins[k], k, c), dst_ref=half(outs[k].at[me], k, c),
                    send_sem=send1.at[k, j], recv_sem=recv1.at[k, j],
                    device_id=(cx, cy, c), device_id_type=MESH))
        for cp in first:
            cp.start()
        passed = []
        for k in range(n):
            for j, (cx, cy) in enumerate(chips):
                landed = half(outs[k].at[2 * cx + cy], k, c)
                pltpu.make_async_remote_copy(
                    src_ref=landed, dst_ref=landed, send_sem=send1.at[k, j], recv_sem=recv1.at[k, j],
                    device_id=(cx, cy, c), device_id_type=MESH).wait_recv()
                fwd = pltpu.make_async_remote_copy(
                    src_ref=landed, dst_ref=landed, send_sem=send2.at[k, j], recv_sem=recv2.at[k, j],
                    device_id=(x, y, 1 - c), device_id_type=MESH)
                fwd.start()
                passed.append(fwd)
        for k in range(n):
            for j, (cx, cy) in enumerate(chips):
                other = half(outs[k].at[2 * cx + cy], k, 1 - c)
                pltpu.make_async_remote_copy(
                    src_ref=other, dst_ref=other, send_sem=send2.at[k, j], recv_sem=recv2.at[k, j],
                    device_id=(x, y, 1 - c), device_id_type=MESH).wait_recv()
        for cp in first + passed:
            cp.wait_send()

    return pl.pallas_call(
        body, name="gather_weights",
        out_shape=[jax.ShapeDtypeStruct((4,) + a.shape, a.dtype) for a in arrs],
        in_specs=[_ANY] * n, out_specs=[_ANY] * n,
        scratch_shapes=[pltpu.SemaphoreType.DMA((n, 3))] * 4,
    )(*arrs)


_HBM = pl.BlockSpec(memory_space=pltpu.HBM)
_SEM = pl.BlockSpec(memory_space=pltpu.SEMAPHORE)
_FLOWS = pltpu.SideEffectType.DATAFLOW_SIDE_EFFECTING


def _in_hbm(a):
    return pltpu.with_memory_space_constraint(a, pltpu.HBM)


def _gather_start(shards):
    n = len(shards)
    nl = shards[0].shape[0]
    lands = [lax.empty((4,) + sh.shape[1:], sh.dtype) for _ in range(nl) for sh in shards]

    def body(*refs):
        srcs, land = refs[:n], refs[n:n + n * nl]
        send, recv, token = refs[n + n * nl:n + n * nl + 3]
        x, y, c, chips = _where_am_i()
        me = 2 * x + y
        for l in range(nl):
            for k in range(n):
                for cx, cy in chips:
                    pltpu.make_async_remote_copy(
                        src_ref=srcs[k].at[l], dst_ref=land[l * n + k].at[me],
                        send_sem=send.at[l], recv_sem=recv.at[l],
                        device_id=(cx, cy, c), device_id_type=MESH).start()
        token[...] = jnp.zeros_like(token)

    ins = [_in_hbm(a) for a in list(shards) + lands]
    res = pl.pallas_call(
        body, name="gather_start",
        out_shape=(pltpu.SemaphoreType.DMA((nl,)), pltpu.SemaphoreType.DMA((nl,)),
                   jax.ShapeDtypeStruct((8, LANES), F32)) + tuple(pltpu.HBM(a.shape, a.dtype) for a in ins),
        in_specs=[_HBM] * len(ins),
        out_specs=(_SEM, _SEM, pl.BlockSpec(memory_space=pltpu.VMEM)) + (_HBM,) * len(ins),
        input_output_aliases={i: 3 + i for i in range(len(ins))},
        compiler_params=pltpu.CompilerParams(has_side_effects=_FLOWS),
    )(*ins)
    send, recv, token = res[:3]
    thru = res[3:3 + n]
    lands = res[3 + n:]
    return send, recv, token, list(thru), [list(lands[l * n:(l + 1) * n]) for l in range(nl)]


def _gather_wait(name, send, recv, l, thru, land, after):
    n = len(thru)

    def body(*refs):
        srcs, lands = refs[:n], refs[n:2 * n]
        send_sem, recv_sem = refs[2 * n:2 * n + 2]
        x, y, c, chips = _where_am_i()
        for k in range(n):
            for cx, cy in chips:
                copy = pltpu.make_async_remote_copy(
                    src_ref=srcs[k].at[l], dst_ref=lands[k].at[2 * cx + cy],
                    send_sem=send_sem.at[l], recv_sem=recv_sem.at[l],
                    device_id=(cx, cy, c), device_id_type=MESH)
                copy.wait_send()
                copy.wait_recv()

    ins = list(thru) + list(land)
    res = pl.pallas_call(
        body, name=name,
        out_shape=tuple(pltpu.HBM(a.shape, a.dtype) for a in ins),
        in_specs=[_HBM] * len(ins) + [_SEM, _SEM, _ANY],
        out_specs=(_HBM,) * len(ins),
        input_output_aliases={i: i for i in range(len(ins))},
        compiler_params=pltpu.CompilerParams(has_side_effects=_FLOWS),
    )(*ins, send, recv, after)
    return list(res[:n]), list(res[n:])


def _pair_exchange(grads):
    n = len(grads)

    def body(*refs):
        ins, got = refs[:n], refs[n:2 * n]
        send, recv = refs[2 * n:]
        x, y, c, _ = _where_am_i()
        sib = (x, y, 1 - c)
        for k in range(n):
            h = ins[k].shape[2] // 2
            for ch in range(4):
                for l in range(ins[k].shape[1]):
                    pltpu.make_async_remote_copy(
                        src_ref=ins[k].at[ch, l, pl.ds((1 - c) * h, h)], dst_ref=got[k].at[ch, l],
                        send_sem=send.at[k], recv_sem=recv.at[k], device_id=sib, device_id_type=MESH).start()
        for k in range(n):
            pltpu.make_async_remote_copy(
                src_ref=got[k], dst_ref=got[k], send_sem=send.at[k], recv_sem=recv.at[k],
                device_id=sib, device_id_type=MESH).wait()

    return pl.pallas_call(
        body, name="reduce_pair",
        out_shape=[jax.ShapeDtypeStruct(g.shape[:2] + (g.shape[2] // 2, g.shape[3]), g.dtype) for g in grads],
        in_specs=[_ANY] * n, out_specs=[_ANY] * n,
        scratch_shapes=[pltpu.SemaphoreType.DMA((n,))] * 2,
    )(*grads)


def _half_tiling(r):
    h = r // 2
    tr = _tile(h, (512, 256, 352, 176, 128, 64, 32))
    return h, tr, h // tr


def _pair_add(full, got, c):
    _, ls, r, cols = full.shape
    h, tr, nh = _half_tiling(r)

    def kern(c_ref, a_ref, b_ref, o32_ref, o16_ref):
        tot = a_ref[...] + b_ref[...].astype(F32)
        o32_ref[...] = tot
        o16_ref[...] = tot.astype(o16_ref.dtype)

    blk = (None, None, tr, cols)
    mine = pl.BlockSpec(blk, lambda ch, l, i, cr: (ch, l, cr[0] * nh + i, 0))
    same = pl.BlockSpec(blk, lambda ch, l, i, cr: (ch, l, i, 0))
    return pl.pallas_call(
        kern, name="pair_add",
        out_shape=[jax.ShapeDtypeStruct(got.shape, F32), jax.ShapeDtypeStruct(got.shape, got.dtype)],
        grid_spec=pltpu.PrefetchScalarGridSpec(
            num_scalar_prefetch=1, grid=(4, ls, nh), in_specs=[mine, same], out_specs=[same, same]),
        compiler_params=_params(("parallel", "parallel", "parallel")),
    )(c.reshape(1), full, got)


def _scatter_start(name, parts):
    n = len(parts)
    lands = [lax.empty(p.shape, p.dtype) for p in parts]

    def body(*refs):
        srcs, land = refs[:n], refs[n:2 * n]
        send, recv, token = refs[2 * n:2 * n + 3]
        x, y, c, chips = _where_am_i()
        me = 2 * x + y
        for k in range(n):
            for cx, cy in chips:
                pltpu.make_async_remote_copy(
                    src_ref=srcs[k].at[2 * cx + cy], dst_ref=land[k].at[me], send_sem=send, recv_sem=recv,
                    device_id=(cx, cy, c), device_id_type=MESH).start()
        token[...] = jnp.zeros_like(token)

    ins = [_in_hbm(a) for a in list(parts) + lands]
    res = pl.pallas_call(
        body, name=name,
        out_shape=(pltpu.SemaphoreType.DMA(()), pltpu.SemaphoreType.DMA(()),
                   jax.ShapeDtypeStruct((8, LANES), F32)) + tuple(pltpu.HBM(a.shape, a.dtype) for a in ins),
        in_specs=[_HBM] * len(ins),
        out_specs=(_SEM, _SEM, pl.BlockSpec(memory_space=pltpu.VMEM)) + (_HBM,) * len(ins),
        input_output_aliases={i: 3 + i for i in range(len(ins))},
        compiler_params=pltpu.CompilerParams(has_side_effects=_FLOWS),
    )(*ins)
    return res[0], res[1], res[2], list(res[3:3 + n]), list(res[3 + n:])


def _scatter_wait(name, send, recv, thru, lands, after):
    n = len(thru)

    def body(*refs):
        srcs, land = refs[:n], refs[n:2 * n]
        send_sem, recv_sem = refs[2 * n:2 * n + 2]
        x, y, c, chips = _where_am_i()
        for k in range(n):
            for cx, cy in chips:
                copy = pltpu.make_async_remote_copy(
                    src_ref=srcs[k].at[2 * cx + cy], dst_ref=land[k].at[2 * cx + cy],
                    send_sem=send_sem, recv_sem=recv_sem, device_id=(cx, cy, c), device_id_type=MESH)
                copy.wait_send()
                copy.wait_recv()

    ins = list(thru) + list(lands)
    res = pl.pallas_call(
        body, name=name,
        out_shape=tuple(pltpu.HBM(a.shape, a.dtype) for a in ins),
        in_specs=[_HBM] * len(ins) + [_SEM, _SEM, _ANY],
        out_specs=(_HBM,) * len(ins),
        input_output_aliases={i: i for i in range(len(ins))},
        compiler_params=pltpu.CompilerParams(has_side_effects=_FLOWS),
    )(*ins, send, recv, after)
    return list(res[n:])


def _chip_add(landed, mine, me, c, into, l0):
    _, ls, h, cols = landed.shape
    _, tr, nh = _half_tiling(2 * h)

    def kern(me_ref, c_ref, land_ref, own_ref, into_ref, o_ref):
        own = own_ref[...]
        tot = None
        for j in range(4):
            term = jnp.where(me_ref[0] == j, own, land_ref[j].astype(F32))
            tot = term if tot is None else tot + term
        o_ref[...] = tot

    return pl.pallas_call(
        kern, name="chip_add",
        out_shape=jax.ShapeDtypeStruct(into.shape, F32),
        grid_spec=pltpu.PrefetchScalarGridSpec(
            num_scalar_prefetch=2, grid=(ls, nh),
            in_specs=[pl.BlockSpec((4, None, tr, cols), lambda l, i, mr, cr: (0, l, i, 0)),
                      pl.BlockSpec((None, None, tr, cols), lambda l, i, mr, cr: (mr[0], l, i, 0)),
                      pl.BlockSpec(memory_space=pl.ANY)],
            out_specs=pl.BlockSpec((None, tr, cols), lambda l, i, mr, cr: (l0 + l, cr[0] * nh + i, 0))),
        input_output_aliases={4: 0},
        compiler_params=_params(("parallel", "parallel")),
    )(me.reshape(1), c.reshape(1), landed, mine, into)


def _pair_join(sums):
    n = len(sums)

    def body(*refs):
        bufs = refs[n:2 * n]
        send, recv = refs[2 * n:]
        x, y, c, _ = _where_am_i()
        sib = (x, y, 1 - c)
        for k in range(n):
            h = bufs[k].shape[1] // 2
            for l in range(bufs[k].shape[0]):
                piece = bufs[k].at[l, pl.ds(c * h, h)]
                pltpu.make_async_remote_copy(
                    src_ref=piece, dst_ref=piece, send_sem=send.at[k], recv_sem=recv.at[k],
                    device_id=sib, device_id_type=MESH).start()
        for k in range(n):
            h = bufs[k].shape[1] // 2
            theirs = bufs[k].at[:, pl.ds((1 - c) * h, h)]
            pltpu.make_async_remote_copy(
                src_ref=theirs, dst_ref=theirs, send_sem=send.at[k], recv_sem=recv.at[k],
                device_id=sib, device_id_type=MESH).wait()

    return pl.pallas_call(
        body, name="reduce_join",
        out_shape=[jax.ShapeDtypeStruct(s.shape, s.dtype) for s in sums],
        in_specs=[_ANY] * n, out_specs=[_ANY] * n,
        input_output_aliases={k: k for k in range(n)},
        scratch_shapes=[pltpu.SemaphoreType.DMA((n,))] * 2,
    )(*sums)


def _allreduce_small(buf):
    R = buf.shape[0]

    def body(in_ref, out_ref, land, send, recv):
        x, y, c, _ = _where_am_i()
        me = 4 * x + 2 * y + c
        land[me] = in_ref[...]
        cps = []
        for k in range(1, 8):
            px, py, pc = x ^ (k >> 2), y ^ ((k >> 1) & 1), c ^ (k & 1)
            cps.append(pltpu.make_async_remote_copy(
                src_ref=in_ref, dst_ref=land.at[me], send_sem=send.at[k - 1], recv_sem=recv.at[k - 1],
                device_id=(px, py, pc), device_id_type=MESH))
        for cp in cps:
            cp.start()
        for k in range(1, 8):
            px, py, pc = x ^ (k >> 2), y ^ ((k >> 1) & 1), c ^ (k & 1)
            slot = land.at[4 * px + 2 * py + pc]
            pltpu.make_async_remote_copy(
                src_ref=slot, dst_ref=slot, send_sem=send.at[k - 1], recv_sem=recv.at[k - 1],
                device_id=(px, py, pc), device_id_type=MESH).wait_recv()
        for cp in cps:
            cp.wait_send()
        tot = land[0]
        for d in range(1, 8):
            tot = tot + land[d]
        out_ref[...] = tot

    vm = pl.BlockSpec(memory_space=pltpu.VMEM)
    return pl.pallas_call(
        body, name="allreduce_small",
        out_shape=jax.ShapeDtypeStruct(buf.shape, F32),
        in_specs=[vm], out_specs=vm,
        scratch_shapes=[pltpu.VMEM((8, R, LANES), F32), pltpu.SemaphoreType.DMA((7,)),
                        pltpu.SemaphoreType.DMA((7,))],
    )(buf)


def _rope_tables(T):
    pos = (jnp.arange(T) - FRONT).astype(F32)
    lane = jnp.arange(LANES)
    inv_a = ROPE_THETA ** (-(2 * ((lane % 64) % 32)).astype(F32) / SWA_HEAD_DIM)
    ang_a = pos[:, None] * inv_a[None, :]
    cosa, sina = jnp.cos(ang_a), jnp.sin(ang_a)
    inv_m = ROPE_THETA ** (-(2 * ((lane - KR_LANE) % 16)).astype(F32) / MLA_ROPE_DIM)
    ang_m = pos[:, None] * inv_m[None, :]
    on = ((lane >= KR_LANE) & (lane < KR_LANE + MLA_ROPE_DIM))[None, :]
    cosm = jnp.where(on, jnp.cos(ang_m), 1.0)
    sinm = jnp.where(on, jnp.sin(ang_m), 0.0)
    return cosa, sina, cosm, sinm


def _cols_from_chips(g):
    return jnp.concatenate(g, axis=-1)


def _rows_from_chips(g):
    return jnp.concatenate(g, axis=-2)


def _cols_to_chips(w):
    L, r, c4 = w.shape
    return jnp.moveaxis(w.reshape(L, r, 4, c4 // 4), 2, 0)


def _rows_to_chips(w):
    L, r4, c = w.shape
    return jnp.moveaxis(w.reshape(L, 4, r4 // 4, c), 1, 0)


def _layer_layouts(w_in, w_qup, w_kvup, w_o, w_gate, w_up, w_down):
    zpad = lambda n: jnp.zeros((D_MODEL, n), w_in.dtype)
    w_in = jnp.concatenate([w_in[:, :C_KR], zpad(KR_LANE), w_in[:, C_KR:IN_W],
                            zpad(LANES - KR_LANE - MLA_ROPE_DIM)], axis=-1)
    w_qup = w_qup.reshape(MLA_Q_RANK, MLA_HEADS, MLA_QK_DIM)
    w_qup = jnp.pad(w_qup, ((0, 0), (0, 0), (0, LANES - MLA_QK_DIM))).reshape(MLA_Q_RANK, SLOT_W)
    return w_in, w_qup, w_kvup, w_o, w_gate, w_up, w_down


def _local_step(x2, target, meta_full, layer_weights, p, early):
    T = BLOCK + x2.shape[0]
    L = DEPTH
    attn_norm, q_norm, kv_norm, sinks = p["attn_norm"], p["q_norm"], p["kv_norm"], p["sinks"]
    out_norm_swa, out_norm_mla, ffn_norm, final_norm = (
        p["out_norm_swa"], p["out_norm_mla"], p["ffn_norm"], p["final_norm"])

    cosa, sina, cosm, sinm = _rope_tables(T)
    no_sink = jnp.full((MLA_HEADS,), NEG, F32)
    scale_a, scale_b = SWA_HEAD_DIM ** -0.5, MLA_QK_DIM ** -0.5
    row = lambda v: v.reshape(1, -1)

    h = jnp.concatenate([jnp.zeros((FRONT, D_MODEL), F32), meta_full, x2], axis=0)
    saved = []
    weights = []
    for l in range(L):
        weights.append(_layer_layouts(*layer_weights(l, h)))
        W_in, W_qup, W_kvup, W_o, W_gate, W_up, W_down = weights[l]
        u, proj = _norm_proj("in_proj", h, row(attn_norm[l]), [W_in], False)
        qa, ka, va, qn, cn, kr = _prep1(proj, row(q_norm[l]), row(kv_norm[l]), cosa, sina, cosm, sinm)
        qb = _mm("q_up", qn, W_qup, "nn")
        kvb = _mm("kv_up", cn, W_kvup, "nn")
        qs, ks, vs = _prep2(qb, kvb, kr, cosm, sinm)
        oa, lse_a = _attn_fwd("swa_fwd", qa, ka, va, sinks[l], 4, False, scale_a, 4)
        ob, lse_b = _attn_fwd("mla_fwd", qs, ks, vs, no_sink, 1, True, scale_b, 1)
        mix = _merge_fwd(oa, ob, row(out_norm_swa[l]), row(out_norm_mla[l]))
        h1 = _mm("o_proj", mix, W_o, "nn", res=h)
        u2, a, b, hm = _norm_proj("ffn_in", h1, row(ffn_norm[l]), [W_gate, W_up], True)
        h2 = _mm("down_proj", hm, W_down, "nn", res=h1)
        saved.append((h, u, proj, qa, ka, va, qn, cn, qs, ks, vs, oa, lse_a, ob, lse_b, mix, h1, u2, a, b, hm))
        h = h2

    dh, d_final, loss_row = _loss_head(h, target, row(final_norm))

    gw = {k: [None] * L for k in ("in", "qup", "kvup", "o", "gate", "up", "down")}
    gs = {k: [None] * L for k in ("attn", "qn", "kvn", "sink", "ga", "gb", "ffn")}

    def natural_grads(layers):
        st = lambda k: jnp.stack(gw[k][layers]) if gw[k][layers][0] is not None else None
        d_in, d_qup = st("in"), st("qup")
        if d_in is not None:
            d_in = jnp.concatenate([d_in[..., :C_KR], d_in[..., C_KR + KR_LANE:C_KR + KR_LANE + MLA_ROPE_DIM]],
                                   axis=-1)
            d_qup = d_qup.reshape(d_qup.shape[0], MLA_Q_RANK, MLA_HEADS, LANES)[..., :MLA_QK_DIM]
            d_qup = d_qup.reshape(d_qup.shape[0], MLA_Q_RANK, -1)
        return [d_in, d_qup, st("kvup"), st("o"), st("gate"), st("up"), st("down")]

    for l in reversed(range(L)):
        (h0, u, proj, qa, ka, va, qn, cn, qs, ks, vs, oa, lse_a, ob, lse_b, mix, h1, u2, a, b, hm) = saved[l]
        W_in, W_qup, W_kvup, W_o, W_gate, W_up, W_down = weights[l]
        if l == 0:
            dh = dh + early("late", natural_grads(slice(1, L)))[0, 0]
        gw["down"][l] = _mm("down_dw", hm, dh, "tn")
        da, db = _ffn_mid_bwd(dh, W_down, a, b)
        gw["gate"][l] = _mm("gate_dw", u2, da, "tn")
        gw["up"][l] = _mm("up_dw", u2, db, "tn")
        du2 = _mm("gate_dx", da, W_gate, "nt")
        du2 = _mm("up_dx", db, W_up, "nt", res=du2)
        gain = row(ffn_norm[l])
        if l == 0:
            gain = gain + early("ffn", natural_grads(slice(0, 1))[4:])[0:1, 0:1]
        dh1, gs["ffn"][l] = _rmsnorm_bwd("ffn_norm_bwd", h1, gain, du2, dh)
        gw["o"][l] = _mm("o_dw", mix, dh1, "tn")
        dmix = _mm("o_dx", dh1, W_o, "nt")
        sink_slots = jnp.repeat(sinks[l], LANES).reshape(1, SLOT_W)
        doa, dla, dob, dlb, gs["ga"][l], gs["gb"][l], dsink = _merge_bwd(
            dmix, oa, ob, lse_a, row(out_norm_swa[l]), row(out_norm_mla[l]), sink_slots)
        gs["sink"][l] = dsink.reshape(SWA_HEADS, LANES)[:, 0]
        dqs, dks, dvs = _attn_bwd("mla_bwd", qs, ks, vs, dob, lse_b, dlb, 1, True, scale_b)
        dqa, dka, dva = _attn_bwd("swa_bwd", qa, ka, va, doa, lse_a, dla, 4, False, scale_a)
        dqb, dkvb, dkr = _prep2_bwd(dqs, dks, dvs, cosm, sinm)
        gw["qup"][l] = _mm("q_up_dw", qn, dqb, "tn")
        gw["kvup"][l] = _mm("kv_up_dw", cn, dkvb, "tn")
        dqn = _mm("q_up_dx", dqb, W_qup, "nt")
        dcn = _mm("kv_up_dx", dkvb, W_kvup, "nt")
        dproj, gs["qn"][l], gs["kvn"][l] = _prep1_bwd(
            proj, dqa, dka, dva, dqn, dcn, dkr, cosa, sina, row(q_norm[l]), row(kv_norm[l]))
        gw["in"][l] = _mm("in_dw", u, dproj, "tn")
        du = _mm("in_dx", dproj, W_in, "nt")
        dh, gs["attn"][l] = _rmsnorm_bwd("attn_norm_bwd", h0, row(attn_norm[l]), du, dh1)

    return loss_row, dh, natural_grads(slice(0, 1))[:4], gs, d_final


def kernel(x, meta_tokens, attn_norm, w_in, q_norm, w_q_up, kv_norm, w_kv_up, sinks, out_norm_swa, out_norm_mla, w_o, ffn_norm, w_gate, w_up, w_down, final_norm, loss_target, m_meta_tokens, m_attn_norm, m_w_in, m_q_norm, m_w_q_up, m_kv_norm, m_w_kv_up, m_sinks, m_out_norm_swa, m_out_norm_mla, m_w_o, m_ffn_norm, m_w_gate, m_w_up, m_w_down, m_final_norm, v_meta_tokens, v_attn_norm, v_w_in, v_q_norm, v_w_q_up, v_kv_norm, v_w_kv_up, v_sinks, v_out_norm_swa, v_out_norm_mla, v_w_o, v_ffn_norm, v_w_gate, v_w_up, v_w_down, v_final_norm):
    assert x.shape[0] == 1 and x.shape[1] % BLOCK == 0
    big = [w_in, w_q_up, w_kv_up, w_o, w_gate, w_up, w_down]

    c_idx = lax.axis_index("c").astype(jnp.int32)
    chip = (2 * lax.axis_index("x") + lax.axis_index("y")).astype(jnp.int32)
    w16 = [w.astype(BF16) for w in big]
    join = [_cols_from_chips, _cols_from_chips, _cols_from_chips, _rows_from_chips, _cols_from_chips,
            _cols_from_chips, _rows_from_chips]

    def whole(own, landed):
        return [f([jnp.where(chip == j, o, g[j]) for j in range(4)]) for f, o, g in zip(join, own, landed)]

    first = _gather_weights([w[0] for w in w16], meta_tokens)
    meta_full = jnp.concatenate([jnp.where(chip == j, meta_tokens, first[-1][j]) for j in range(4)], axis=-1)
    send, recv, token, thru, lands = _gather_start([w[1:] for w in w16])
    state = {"thru": thru}

    def layer_weights(l, h):
        if l == 0:
            return whole([w[0] for w in w16], first[:-1])
        state["thru"], landed = _gather_wait("gather_wait%d" % l, send, recv, l - 1, state["thru"], lands[l - 1], h)
        return whole([w[l] for w in w16], landed)

    small_p = dict(attn_norm=attn_norm, q_norm=q_norm, kv_norm=kv_norm, sinks=sinks, out_norm_swa=out_norm_swa,
                   out_norm_mla=out_norm_mla, ffn_norm=ffn_norm, final_norm=final_norm)
    split = [_cols_to_chips, _cols_to_chips, _cols_to_chips, _rows_to_chips, _cols_to_chips, _cols_to_chips,
             _rows_to_chips]
    rounds = []

    def reduce_start(name, ks, l0, grads):
        full = [split[k](d) for k, d in zip(ks, grads)]
        got = _pair_exchange([f.astype(BF16) for f in full])
        sums = [_pair_add(f, g, c_idx) for f, g in zip(full, got)]
        send, recv, tok, thru, lands = _scatter_start("scatter_start_" + name, [s16 for _, s16 in sums])
        rounds.append((name, ks, l0, [s32 for s32, _ in sums], send, recv, thru, lands))
        return tok

    starts = {"late": lambda g: reduce_start("late", list(range(7)), 1, g),
              "ffn": lambda g: reduce_start("ffn", [4, 5, 6], 0, g)}
    loss_row, dh, d_first, gs, d_final = _local_step(
        x[0] + token[0, 0], loss_target[0], meta_full, layer_weights, small_p, lambda which, g: starts[which](g))
    grad_x = dh[BLOCK:][None]
    reduce_start("first", [0, 1, 2, 3], 0, d_first)

    g_big = [lax.empty(w.shape, F32) for w in big]
    for name, ks, l0, s32, send, recv, thru, lands in rounds:
        landed = _scatter_wait("scatter_wait_" + name, send, recv, thru, lands, dh)
        for k, t, m32 in zip(ks, landed, s32):
            g_big[k] = _chip_add(t, m32, chip, c_idx, g_big[k], l0)
    g_big = _pair_join(g_big)

    small = [jnp.stack(gs["attn"]).reshape(-1), jnp.stack(gs["qn"]).reshape(-1), jnp.stack(gs["kvn"]).reshape(-1),
             jnp.stack(gs["sink"]).reshape(-1), jnp.stack(gs["ga"]).reshape(-1), jnp.stack(gs["gb"]).reshape(-1),
             jnp.stack(gs["ffn"]).reshape(-1), d_final.reshape(-1)]
    sizes = [s.shape[0] for s in small]
    flat = jnp.concatenate(small + [dh[FRONT:BLOCK].reshape(-1), loss_row[0, :1]])
    n_flat = flat.shape[0]
    rows_needed = -(-n_flat // (8 * LANES)) * 8
    flat = jnp.pad(flat, (0, rows_needed * LANES - n_flat)).reshape(rows_needed, LANES)
    tot = _allreduce_small(flat).reshape(-1)
    n_small = sum(sizes)
    loss = tot[n_small + N_META * D_MODEL]
    g_meta_full = tot[n_small:n_small + N_META * D_MODEL].reshape(N_META, D_MODEL)
    g_meta_mine = lax.dynamic_slice_in_dim(g_meta_full, chip * (D_MODEL // 4), D_MODEL // 4, axis=1)

    small_w = [attn_norm, q_norm, kv_norm, sinks, out_norm_swa, out_norm_mla, ffn_norm, final_norm]
    small_m = [m_attn_norm, m_q_norm, m_kv_norm, m_sinks, m_out_norm_swa, m_out_norm_mla, m_ffn_norm, m_final_norm]
    small_v = [v_attn_norm, v_q_norm, v_kv_norm, v_sinks, v_out_norm_swa, v_out_norm_mla, v_ffn_norm, v_final_norm]
    n_rows = -(-n_small // (8 * LANES)) * 8

    def pack(arrs):
        f = jnp.concatenate([a.reshape(-1) for a in arrs])
        return jnp.pad(f, (0, n_rows * LANES - n_small), constant_values=1.0).reshape(n_rows, LANES)

    g_small_pack = jnp.pad(tot[:n_small], (0, n_rows * LANES - n_small)).reshape(n_rows, LANES)
    upd_small = _adamw("adam_small", pack(small_w), g_small_pack, pack(small_m), pack(small_v))

    def unpack(p):
        f = p.reshape(-1)
        out, off = [], 0
        for a, n in zip(small_w, sizes):
            out.append(f[off:off + n].reshape(a.shape))
            off += n
        return out

    g_small = unpack(g_small_pack)
    d_small, m_small, v_small = [unpack(p) for p in upd_small]
    d_meta, nm_meta, nv_meta = _adamw("adam_meta", meta_tokens, g_meta_mine, m_meta_tokens, v_meta_tokens)

    big_m = [m_w_in, m_w_q_up, m_w_kv_up, m_w_o, m_w_gate, m_w_up, m_w_down]
    big_v = [v_w_in, v_w_q_up, v_w_kv_up, v_w_o, v_w_gate, v_w_up, v_w_down]
    upd_big = [_adamw("adam_big", w, g, m, v) for w, g, m, v in zip(big, g_big, big_m, big_v)]

    names = ["meta_tokens", "attn_norm", "w_in", "q_norm", "w_q_up", "kv_norm", "w_kv_up", "sinks",
             "out_norm_swa", "out_norm_mla", "w_o", "ffn_norm", "w_gate", "w_up", "w_down", "final_norm"]
    small_idx = {"attn_norm": 0, "q_norm": 1, "kv_norm": 2, "sinks": 3, "out_norm_swa": 4,
                 "out_norm_mla": 5, "ffn_norm": 6, "final_norm": 7}
    big_idx = {"w_in": 0, "w_q_up": 1, "w_kv_up": 2, "w_o": 3, "w_gate": 4, "w_up": 5, "w_down": 6}
    grads, deltas, new_m, new_v = [], [], [], []
    for nme in names:
        if nme == "meta_tokens":
            quad = (g_meta_mine, d_meta, nm_meta, nv_meta)
        elif nme in small_idx:
            i = small_idx[nme]
            quad = (g_small[i], d_small[i], m_small[i], v_small[i])
        else:
            i = big_idx[nme]
            quad = (g_big[i], *upd_big[i])
        grads.append(quad[0]); deltas.append(quad[1]); new_m.append(quad[2]); new_v.append(quad[3])
    return (loss, grad_x, *grads, *deltas, *new_m, *new_v)
```

```python
import jax
import jax.numpy as jnp
from jax import lax
from jax.experimental import pallas as pl
from jax.experimental.pallas import tpu as pltpu

F32 = jnp.float32
BF16 = jnp.bfloat16
MXU_DTYPE = BF16

D_MODEL = 1024
DEPTH = 4
N_META = 16
BLOCK = 128
WINDOW = 128
ROPE_THETA = 10000.0
EPS = 1e-6
NEG = -1e30
SWA_HEADS = 8
SWA_KV_HEADS = 2
SWA_HEAD_DIM = 64
MLA_HEADS = 8
MLA_Q_RANK = 256
MLA_KV_RANK = 128
MLA_NOPE_DIM = 64
MLA_ROPE_DIM = 32
MLA_V_DIM = 64
MLA_QK_DIM = MLA_NOPE_DIM + MLA_ROPE_DIM
D_FF = 2816
FRONT = (-N_META) % BLOCK
LANES = 128
SLOT_W = 8 * LANES
C_QA, C_KA, C_VA, C_QL, C_KL, C_KR, IN_WP = 0, 512, 640, 768, 1024, 1152, 1280
KR_LANE = 64
IN_W = 1184

ADAM_LR, ADAM_B1, ADAM_B2, ADAM_EPS, ADAM_WD, ADAM_STEP = 0.001, 0.9, 0.999, 1e-08, 0.01, 10

VMEM_LIMIT = 48 * 1024 * 1024
MESH = pl.DeviceIdType.MESH


def _tile(n, prefs):
    for t in prefs:
        if n % t == 0:
            return t
    return n


def _params(sem):
    return pltpu.CompilerParams(dimension_semantics=sem, vmem_limit_bytes=VMEM_LIMIT)


_DIMS = {"nn": (((1,), (0,)), ((), ())), "nt": (((1,), (1,)), ((), ())), "tn": (((0,), (0,)), ((), ()))}


def _mm(name, a, b, mode, out_dtype=F32, res=None):
    if mode == "nn":
        (M, K), (_, N) = a.shape, b.shape
    elif mode == "nt":
        (M, K), (N, _) = a.shape, b.shape
    else:
        (K, M), (_, N) = a.shape, b.shape
    lane_tiles = (1408, 1024, 640, 768, 512, 384, 256, 128)
    row_tiles = (1056, 528, 512, 384, 256, 128)
    bm = _tile(M, lane_tiles if mode == "tn" else row_tiles)
    bn = _tile(N, lane_tiles)
    bk = _tile(K, row_tiles if mode == "tn" else lane_tiles)
    nk = K // bk
    if mode == "tn":
        a_spec = pl.BlockSpec((bk, bm), lambda i, j, k: (k, i))
    else:
        a_spec = pl.BlockSpec((bm, bk), lambda i, j, k: (i, k))
    if mode == "nt":
        b_spec = pl.BlockSpec((bn, bk), lambda i, j, k: (j, k))
    else:
        b_spec = pl.BlockSpec((bk, bn), lambda i, j, k: (k, j))
    o_spec = pl.BlockSpec((bm, bn), lambda i, j, k: (i, j))
    in_specs = [a_spec, b_spec]
    args = [a, b]
    if res is not None:
        in_specs.append(o_spec)
        args.append(res)
    dims = _DIMS[mode]

    def kern(a_ref, b_ref, *rest):
        if res is not None:
            r_ref, o_ref = rest[0], rest[1]
            scr = rest[2:]
        else:
            r_ref, o_ref = None, rest[0]
            scr = rest[1:]
        p = lax.dot_general(a_ref[...].astype(MXU_DTYPE), b_ref[...].astype(MXU_DTYPE), dims,
                            preferred_element_type=F32)

        def finish(val):
            if r_ref is not None:
                val = val + r_ref[...]
            o_ref[...] = val.astype(o_ref.dtype)

        if nk == 1:
            finish(p)
        else:
            acc = scr[0]
            k = pl.program_id(2)

            @pl.when(k == 0)
            def _():
                acc[...] = p

            @pl.when(k > 0)
            def _():
                acc[...] += p

            @pl.when(k == nk - 1)
            def _():
                finish(acc[...])

    return pl.pallas_call(
        kern, name=name,
        out_shape=jax.ShapeDtypeStruct((M, N), out_dtype),
        grid=(M // bm, N // bn, nk),
        in_specs=in_specs, out_specs=o_spec,
        scratch_shapes=[pltpu.VMEM((bm, bn), F32)] if nk > 1 else [],
        compiler_params=_params(("parallel", "parallel", "arbitrary")),
    )(*args)


def _rowmap(name, body, rows, vecs, outs, accs=(), tr_prefs=(384, 256, 128)):
    R = rows[0].shape[0]
    tr = _tile(R, tr_prefs)
    n_r, n_v, n_o, n_a = len(rows), len(vecs), len(outs), len(accs)

    def kern(*refs):
        ins = [r[...] for r in refs[:n_r + n_v]]
        o_refs = refs[n_r + n_v:n_r + n_v + n_o]
        a_refs = refs[n_r + n_v + n_o:]
        res = body(*ins)
        for o_ref, val in zip(o_refs, res[:n_o]):
            o_ref[...] = val.astype(o_ref.dtype)
        if n_a:
            first = pl.program_id(0) == 0

            @pl.when(first)
            def _():
                for a_ref, val in zip(a_refs, res[n_o:]):
                    a_ref[...] = val

            @pl.when(jnp.logical_not(first))
            def _():
                for a_ref, val in zip(a_refs, res[n_o:]):
                    a_ref[...] += val

    in_specs = [pl.BlockSpec((tr, r.shape[1]), lambda i: (i, 0)) for r in rows]
    in_specs += [pl.BlockSpec((1, v.shape[1]), lambda i: (0, 0)) for v in vecs]
    out_specs = [pl.BlockSpec((tr, c), lambda i: (i, 0)) for c, _ in outs]
    out_specs += [pl.BlockSpec((1, c), lambda i: (0, 0)) for c in accs]
    out_shape = [jax.ShapeDtypeStruct((R, c), dt) for c, dt in outs]
    out_shape += [jax.ShapeDtypeStruct((1, c), F32) for c in accs]
    return pl.pallas_call(
        kern, name=name, out_shape=out_shape, grid=(R // tr,),
        in_specs=in_specs, out_specs=out_specs,
        compiler_params=_params(("arbitrary",) if n_a else ("parallel",)),
    )(*rows, *vecs)


def _lane(shape):
    return lax.broadcasted_iota(jnp.int32, shape, 1)


def _rot_swa(x):
    lane = _lane(x.shape)
    return jnp.where((lane & 63) < 32, -pltpu.roll(x, 96, 1), pltpu.roll(x, 32, 1))


def _rot_mla(x):
    lane = _lane(x.shape)
    lo = jnp.where(lane >= KR_LANE, -pltpu.roll(x, 112, 1), 0.0)
    hi = jnp.where(lane < KR_LANE + MLA_ROPE_DIM, pltpu.roll(x, 16, 1), 0.0)
    return jnp.where(lane < KR_LANE + 16, lo, hi)


def _rope(x, cos, sin, rot):
    return x * cos + rot(x) * sin


def _rope_t(g, cos, sin, rot):
    return g * cos - rot(g * sin)


def _low(x):
    return jnp.where(_lane(x.shape) < 64, x, 0.0)


def _value_slot(x):
    lane = _lane(x.shape)
    return jnp.where(lane < 64, x, jnp.where(lane == 64, 1.0, 0.0))


def _blk(x, j):
    return x[:, j * LANES:(j + 1) * LANES]


def _rms_r(x, width):
    return lax.rsqrt(jnp.sum(x * x, axis=-1, keepdims=True) * (1.0 / width) + EPS)


def _rms_bwd(x, g, dy, width):
    r = _rms_r(x, width)
    gdy = dy * g
    dot = jnp.sum(gdy * x, axis=-1, keepdims=True)
    dx = r * gdy - x * (r * r * r * (1.0 / width) * dot)
    return dx, dy * x * r


def _colsum(x):
    return jnp.sum(x, axis=0, keepdims=True)


def _rmsnorm_bwd(name, x, g, dy, dres):
    def body(xv, dyv, dr, gv):
        dx, dg = _rms_bwd(xv, gv, dyv, D_MODEL)
        return dx + dr, _colsum(dg)
    return _rowmap(name, body, [x, dy, dres], [g], [(D_MODEL, F32)], [D_MODEL])


def _prep1(proj, qn_g, kv_g, cosa, sina, cosm, sinm):
    def body(p, ca, sa, cm, sm, gq, gk):
        qa = []
        for j in range(4):
            xr = _rope(_blk(p, j), ca, sa, _rot_swa)
            qa += [_low(xr), _low(pltpu.roll(xr, 64, 1))]
        kr_ = _rope(_blk(p, C_KA // LANES), ca, sa, _rot_swa)
        ka = [_low(kr_), _low(pltpu.roll(kr_, 64, 1))]
        vv = _blk(p, C_VA // LANES)
        va = [_value_slot(vv), _value_slot(pltpu.roll(vv, 64, 1))]
        ql = p[:, C_QL:C_QL + MLA_Q_RANK]
        qn = ql * _rms_r(ql, MLA_Q_RANK) * gq
        kl = p[:, C_KL:C_KL + MLA_KV_RANK]
        cn = kl * _rms_r(kl, MLA_KV_RANK) * gk
        kr = _rope(_blk(p, C_KR // LANES), cm, sm, _rot_mla)
        return (jnp.concatenate(qa, 1), jnp.concatenate(ka, 1), jnp.concatenate(va, 1), qn, cn, kr)
    return _rowmap("prep1", body, [proj, cosa, sina, cosm, sinm], [qn_g, kv_g],
                   [(SLOT_W, BF16), (2 * LANES, BF16), (2 * LANES, BF16),
                    (MLA_Q_RANK, BF16), (MLA_KV_RANK, BF16), (LANES, F32)])


def _prep2(qb, kvb, kr, cosm, sinm):
    def body(q, kv, krv, cm, sm):
        qs, ks, vs = [], [], []
        for h in range(MLA_HEADS):
            qs.append(_rope(_blk(q, h), cm, sm, _rot_mla))
            kvh = _blk(kv, h)
            ks.append(_low(kvh) + krv)
            vs.append(_value_slot(pltpu.roll(kvh, 64, 1)))
        return jnp.concatenate(qs, 1), jnp.concatenate(ks, 1), jnp.concatenate(vs, 1)
    return _rowmap("prep2", body, [qb, kvb, kr, cosm, sinm], [],
                   [(SLOT_W, BF16), (SLOT_W, BF16), (SLOT_W, BF16)])


def _compact(slots):
    return jnp.concatenate(
        [_blk(slots, 2 * j) + pltpu.roll(_blk(slots, 2 * j + 1), 64, 1) for j in range(4)], 1)


def _expand(nat):
    out = []
    for j in range(4):
        b = _blk(nat, j)
        out += [_low(b), _low(pltpu.roll(b, 64, 1))]
    return jnp.concatenate(out, 1)


def _merge_fwd(oa, ob, ga, gb):
    def body(a, b, gav, gbv):
        xa, xb = _compact(a), _compact(b)
        return (jnp.concatenate([xa * _rms_r(xa, 512) * gav, xb * _rms_r(xb, 512) * gbv], 1),)
    return _rowmap("merge_fwd", body, [oa, ob], [ga, gb], [(D_MODEL, BF16)])[0]


def _merge_bwd(dmix, oa, ob, lse_a, ga, gb, sink_slots):
    def body(dm, a, b, lse, gav, gbv, sk):
        outs = []
        accs = []
        for o, g, lo in ((a, gav, 0), (b, gbv, 512)):
            x = _compact(o)
            dx, dg = _rms_bwd(x, g, dm[:, lo:lo + 512], 512)
            do = _expand(dx)
            delta = jnp.concatenate(
                [jnp.broadcast_to(jnp.sum(_blk(do, h) * _blk(o, h), axis=-1, keepdims=True),
                                  (do.shape[0], LANES)) for h in range(8)], 1)
            outs += [do, delta]
            accs.append(_colsum(dg))
        dsink = _colsum(-jnp.exp(sk - lse) * outs[1])
        return (*outs, *accs, dsink)
    return _rowmap("merge_bwd", body, [dmix, oa, ob, lse_a], [ga, gb, sink_slots],
                   [(SLOT_W, BF16), (SLOT_W, F32), (SLOT_W, BF16), (SLOT_W, F32)],
                   [512, 512, SLOT_W])


def _prep2_bwd(dq, dk, dv, cosm, sinm):
    def body(dqv, dkv, dvv, cm, sm):
        dqv, dkv, dvv = dqv.astype(F32), dkv.astype(F32), dvv.astype(F32)
        dqb, dkvb = [], []
        krsum = jnp.zeros((dqv.shape[0], LANES), F32)
        for h in range(MLA_HEADS):
            dqb.append(_rope_t(_blk(dqv, h), cm, sm, _rot_mla))
            dkh = _blk(dkv, h)
            dkvb.append(_low(dkh) + pltpu.roll(_blk(dvv, h), 64, 1))
            krsum = krsum + dkh
        lane = _lane(krsum.shape)
        dkr = jnp.where((lane >= KR_LANE) & (lane < KR_LANE + MLA_ROPE_DIM),
                        _rope_t(krsum, cm, sm, _rot_mla), 0.0)
        return jnp.concatenate(dqb, 1), jnp.concatenate(dkvb, 1), dkr
    return _rowmap("prep2_bwd", body, [dq, dk, dv, cosm, sinm], [],
                   [(SLOT_W, BF16), (SLOT_W, BF16), (LANES, F32)])


def _prep1_bwd(proj, dqa, dka, dva, dqn, dcn, dkr, cosa, sina, qn_g, kv_g):
    def body(p, dq, dk, dv, dqnv, dcnv, dkrv, ca, sa, gq, gk):
        dq, dk, dv = dq.astype(F32), dk.astype(F32), dv.astype(F32)
        cols = []
        for j in range(4):
            nat = _blk(dq, 2 * j) + pltpu.roll(_blk(dq, 2 * j + 1), 64, 1)
            cols.append(_rope_t(nat, ca, sa, _rot_swa))
        grp = lambda d, g: sum(_blk(d, 4 * g + i) for i in range(4))
        cols.append(_rope_t(grp(dk, 0) + pltpu.roll(grp(dk, 1), 64, 1), ca, sa, _rot_swa))
        cols.append(grp(dv, 0) + pltpu.roll(grp(dv, 1), 64, 1))
        dql, dgq = _rms_bwd(p[:, C_QL:C_QL + MLA_Q_RANK], gq, dqnv, MLA_Q_RANK)
        dkl, dgk = _rms_bwd(p[:, C_KL:C_KL + MLA_KV_RANK], gk, dcnv, MLA_KV_RANK)
        cols += [dql, dkl, dkrv]
        return jnp.concatenate(cols, 1), _colsum(dgq), _colsum(dgk)
    return _rowmap("prep1_bwd", body, [proj, dqa, dka, dva, dqn, dcn, dkr, cosa, sina], [qn_g, kv_g],
                   [(IN_WP, BF16)], [MLA_Q_RANK, MLA_KV_RANK], tr_prefs=(192, 128))


def _sigmoid(x):
    return 1.0 / (1.0 + jnp.exp(-x))


def _ffn_tiles(T, F):
    return _tile(T, (528, 512, 384, 256, 128)), _tile(F, (1408, 1024, 640, 512, 256, 128))


def _norm_proj(name, h, g, weights, swiglu):
    (T, D), F = h.shape, weights[0].shape[1]
    bm, bn = _ffn_tiles(T, F)
    nw = len(weights)

    def kern(h_ref, g_ref, *rest):
        w_refs, u_ref, o_refs, u_scr = rest[:nw], rest[nw], rest[nw + 1:-1], rest[-1]

        @pl.when(pl.program_id(1) == 0)
        def _():
            x = h_ref[...]
            u = (x * _rms_r(x, D) * g_ref[...]).astype(u_scr.dtype)
            u_scr[...] = u
            u_ref[...] = u

        uv = u_scr[...]
        prods = [jnp.dot(uv, w[...], preferred_element_type=F32) for w in w_refs]
        for o_ref, p in zip(o_refs, prods):
            o_ref[...] = p
        if swiglu:
            a, b = prods
            o_refs[nw][...] = (a * _sigmoid(a) * b).astype(o_refs[nw].dtype)

    w_spec = pl.BlockSpec((D, bn), lambda i, j: (0, j))
    row_spec = pl.BlockSpec((bm, D), lambda i, j: (i, 0))
    o_spec = pl.BlockSpec((bm, bn), lambda i, j: (i, j))
    n_out = nw + (1 if swiglu else 0)
    return pl.pallas_call(
        kern, name=name,
        out_shape=[jax.ShapeDtypeStruct((T, D), MXU_DTYPE)] + [jax.ShapeDtypeStruct((T, F), F32)] * nw
        + ([jax.ShapeDtypeStruct((T, F), MXU_DTYPE)] if swiglu else []),
        grid=(T // bm, F // bn),
        in_specs=[row_spec, pl.BlockSpec((1, D), lambda i, j: (0, 0))] + [w_spec] * nw,
        out_specs=[row_spec] + [o_spec] * n_out,
        scratch_shapes=[pltpu.VMEM((bm, D), MXU_DTYPE)],
        compiler_params=_params(("parallel", "arbitrary")),
    )(h, g, *weights)


def _ffn_mid_bwd(dh, w_down, a, b):
    (T, D), F = dh.shape, w_down.shape[0]
    bm, bn = _ffn_tiles(T, F)

    def kern(dh_ref, wd_ref, a_ref, b_ref, da_ref, db_ref):
        d = lax.dot_general(dh_ref[...].astype(MXU_DTYPE), wd_ref[...], _DIMS["nt"], preferred_element_type=F32)
        av, bv = a_ref[...], b_ref[...]
        s = _sigmoid(av)
        da_ref[...] = (d * bv * (s * (1.0 + av * (1.0 - s)))).astype(da_ref.dtype)
        db_ref[...] = (d * (av * s)).astype(db_ref.dtype)

    o_spec = pl.BlockSpec((bm, bn), lambda i, j: (i, j))
    return pl.pallas_call(
        kern, name="ffn_mid_bwd",
        out_shape=[jax.ShapeDtypeStruct((T, F), MXU_DTYPE)] * 2,
        grid=(T // bm, F // bn),
        in_specs=[pl.BlockSpec((bm, D), lambda i, j: (i, 0)), pl.BlockSpec((bn, D), lambda i, j: (j, 0)),
                  o_spec, o_spec],
        out_specs=[o_spec, o_spec],
        compiler_params=_params(("parallel", "parallel")),
    )(dh, w_down, a, b)


def _loss_head(h, target, g):
    T = h.shape[0]
    nb = T // BLOCK

    def kern(h_ref, t_ref, g_ref, dh_ref, dg_ref, loss_ref, acc):
        i = pl.program_id(0)

        @pl.when(i == 0)
        def _():
            dh_ref[...] = jnp.zeros_like(dh_ref)
            dg_ref[...] = jnp.zeros_like(dg_ref)
            acc[...] = jnp.zeros_like(acc)

        @pl.when(i > 0)
        def _():
            x = h_ref[...]
            gv = g_ref[...]
            e = x * _rms_r(x, D_MODEL) * gv - t_ref[...]
            acc[...] += _colsum(e * e)
            dx, dg = _rms_bwd(x, gv, e * (1.0 / D_MODEL), D_MODEL)
            dh_ref[...] = dx
            dg_ref[...] += _colsum(dg)

        @pl.when(i == nb - 1)
        def _():
            tot = jnp.sum(acc[...], axis=-1, keepdims=True) * (0.5 / D_MODEL)
            loss_ref[...] = jnp.broadcast_to(tot, loss_ref.shape)

    return pl.pallas_call(
        kern, name="loss_head",
        out_shape=[jax.ShapeDtypeStruct((T, D_MODEL), F32), jax.ShapeDtypeStruct((1, D_MODEL), F32),
                   jax.ShapeDtypeStruct((1, LANES), F32)],
        grid=(nb,),
        in_specs=[pl.BlockSpec((BLOCK, D_MODEL), lambda i: (i, 0)),
                  pl.BlockSpec((BLOCK, D_MODEL), lambda i: (jnp.maximum(i - 1, 0), 0)),
                  pl.BlockSpec((1, D_MODEL), lambda i: (0, 0))],
        out_specs=[pl.BlockSpec((BLOCK, D_MODEL), lambda i: (i, 0)),
                   pl.BlockSpec((1, D_MODEL), lambda i: (0, 0)),
                   pl.BlockSpec((1, LANES), lambda i: (0, 0))],
        scratch_shapes=[pltpu.VMEM((1, D_MODEL), F32)],
        compiler_params=_params(("arbitrary",)),
    )(h, target, g)


LOG2E = 1.4426950408889634


def _attn_plan(T, causal):
    tq = _tile(T, (384, 256, 128))
    ck = min(2 * tq, T) if causal else min(tq + WINDOW, T)
    return tq, ck, (-(-T // ck) if causal else 1)


def _chunk(i, c, T, tq, ck, causal):
    if causal:
        return pl.multiple_of(jnp.minimum(c * ck, T - ck), LANES), c * ck
    return pl.multiple_of(jnp.clip(i * tq - WINDOW, 0, T - ck), LANES), 0


def _n_chunks(i, tq, ck, causal):
    return ((i + 1) * tq + ck - 1) // ck if causal else 1


def _mask(s, i, start, first, tq, ck, causal):
    qpos = i * tq + lax.broadcasted_iota(jnp.int32, (tq, 1), 0)
    kpos = start + lax.broadcasted_iota(jnp.int32, (tq, ck), 1)
    low = jnp.maximum(jnp.where(qpos < FRONT, 0, FRONT), first)
    if not causal:
        low = jnp.maximum(low, qpos - (WINDOW - 1))
    return jnp.where(kpos >= low, jnp.where(kpos <= qpos, s, NEG), NEG)


def _chunk_loop(n, body, init, several):
    carry = body(0, init, True)
    if not several:
        return carry
    carry = body(n - 1, carry, True)
    return lax.fori_loop(1, n - 1, lambda c, cr: body(c, cr, False), carry)


def _tile_sweep(nq, tq, ck, causal, q_tiles, pair=False):
    if pair and nq >= 2 and (not causal or ck == 2 * tq):
        if causal:
            q_tiles([0, 1], False)
            lax.fori_loop(1, nq // 2, lambda p, cr: q_tiles([2 * p, 2 * p + 1], True) or cr, 0)
        else:
            lax.fori_loop(0, nq // 2, lambda p, cr: q_tiles([2 * p, 2 * p + 1], False) or cr, 0)
        if nq % 2:
            q_tiles([nq - 1], causal and nq >= 3)
        return
    one = min(nq, ck // tq) if causal else nq
    lax.fori_loop(0, one, lambda i, cr: q_tiles([i], False) or cr, 0)
    if one < nq:
        lax.fori_loop(one, nq, lambda i, cr: q_tiles([i], True) or cr, 0)


BWD_HEADS_PER_STEP = 1


def _head_cols(group, hp):
    q_cols = lambda hh: slice(hh * LANES, (hh + 1) * LANES)
    if group == 1:
        return q_cols, q_cols, hp * LANES
    assert group % hp == 0
    return q_cols, (lambda hh: slice(0, LANES)), LANES


def _whole(T, width, index, single):
    if single:
        return pl.BlockSpec((T, width), index, pipeline_mode=pl.Buffered(1))
    return pl.BlockSpec((T, width), index)


def _attn_fwd(name, q, k, v, sinks, group, causal, scale, hp):
    T = q.shape[0]
    H = q.shape[1] // LANES
    HP = hp
    tq, ck, slots = _attn_plan(T, causal)
    nq = T // tq
    c2 = scale * LOG2E
    q_cols, k_cols, kw = _head_cols(group, HP)

    def kern(sink_ref, q_ref, k_ref, v_ref, o_ref, lse_ref, s_scr):
        sink2 = [sink_ref[pl.program_id(0) * HP + hh] * LOG2E for hh in range(HP)]

        def q_tiles(tiles, several):
            rows = [pl.ds(pl.multiple_of(i * tq, tq), tq) for i in tiles]
            chains = [(t, hh) for t in range(len(tiles)) for hh in range(HP)]
            qq = [q_ref[rows[t], q_cols(hh)] for t, hh in chains]
            n = _n_chunks(tiles[0], tq, ck, causal)

            def score(c, m2, masked):
                out = []
                for n_ch, (t, hh) in enumerate(chains):
                    start, first = _chunk(tiles[t], c, T, tq, ck, causal)
                    s = lax.dot_general(qq[n_ch], k_ref[pl.ds(start, ck), k_cols(hh)], _DIMS["nt"],
                                        preferred_element_type=F32) * c2
                    if masked:
                        s = _mask(s, tiles[t], start, first, tq, ck, causal)
                    s_scr[n_ch, c] = s
                    out.append(jnp.maximum(m2[n_ch], jnp.max(s, axis=-1, keepdims=True)))
                return tuple(out)

            m2 = _chunk_loop(n, score, tuple(jnp.full((tq, 1), sink2[hh], F32) for _, hh in chains), several)

            def weigh(c, acc):
                out = []
                for n_ch, (t, hh) in enumerate(chains):
                    start, _ = _chunk(tiles[t], c, T, tq, ck, causal)
                    p = jnp.exp2(s_scr[n_ch, c] - m2[n_ch])
                    out.append(acc[n_ch] + jnp.dot(p.astype(MXU_DTYPE), v_ref[pl.ds(start, ck), k_cols(hh)],
                                                   preferred_element_type=F32))
                return tuple(out)

            acc = lax.fori_loop(0, n, weigh, tuple(jnp.zeros((tq, LANES), F32) for _ in chains))
            lane = _lane((tq, LANES))
            for n_ch, (t, hh) in enumerate(chains):
                l = acc[n_ch][:, 64:65] + jnp.exp2(sink2[hh] - m2[n_ch])
                o_ref[rows[t], q_cols(hh)] = jnp.where(lane < 64, acc[n_ch] / l, 0.0)
                lse_ref[rows[t], q_cols(hh)] = jnp.broadcast_to(m2[n_ch] * (1.0 / LOG2E) + jnp.log(l), (tq, LANES))

        _tile_sweep(nq, tq, ck, causal, q_tiles, pair=True)

    q_spec = _whole(T, HP * LANES, lambda g: (0, g), HP > 1)
    kv_spec = _whole(T, kw, (lambda g: (0, g)) if group == 1 else (lambda g: (0, (g * HP) // group)), HP > 1)
    return pl.pallas_call(
        kern, name=name,
        out_shape=[jax.ShapeDtypeStruct((T, H * LANES), F32)] * 2,
        grid=(H // HP,),
        in_specs=[pl.BlockSpec(memory_space=pltpu.SMEM), q_spec, kv_spec, kv_spec],
        out_specs=[q_spec, q_spec],
        scratch_shapes=[pltpu.VMEM((2 * HP, slots, tq, ck), F32)],
        compiler_params=_params(("parallel",)),
    )(sinks, q, k, v)


def _attn_bwd(name, q, k, v, do, lse, delta, group, causal, scale):
    T = q.shape[0]
    H = q.shape[1] // LANES
    HP = BWD_HEADS_PER_STEP
    tq, ck, _ = _attn_plan(T, causal)
    nq = T // tq
    c2 = scale * LOG2E
    q_cols, k_cols, kw = _head_cols(group, HP)

    def kern(q_ref, k_ref, v_ref, do_ref, lse_ref, dl_ref, dq_ref, dk_ref, dv_ref, dk_acc, dv_acc):
        dk_acc[...] = jnp.zeros_like(dk_acc)
        dv_acc[...] = jnp.zeros_like(dv_acc)

        def q_tiles(tiles, several):
            rows = [pl.ds(pl.multiple_of(i * tq, tq), tq) for i in tiles]
            chains = [(t, hh) for t in range(len(tiles)) for hh in range(HP)]
            qq = [q_ref[rows[t], q_cols(hh)] for t, hh in chains]
            dd = [do_ref[rows[t], q_cols(hh)] for t, hh in chains]
            lse2 = [lse_ref[rows[t], q_cols(hh)][:, 0:1] * LOG2E for t, hh in chains]
            dl_c = [dl_ref[rows[t], q_cols(hh)][:, 0:1] for t, hh in chains]

            def chunk(c, dq, masked):
                out = []
                for n_ch, (t, hh) in enumerate(chains):
                    start, first = _chunk(tiles[t], c, T, tq, ck, causal)
                    keys = pl.ds(start, ck)
                    kk, vv = k_ref[keys, k_cols(hh)], v_ref[keys, k_cols(hh)]
                    s = lax.dot_general(qq[n_ch], kk, _DIMS["nt"], preferred_element_type=F32)
                    if masked:
                        s = _mask(s, tiles[t], start, first, tq, ck, causal)
                    p = jnp.exp2(s * c2 - lse2[n_ch])
                    dv_acc[keys, q_cols(hh)] += lax.dot_general(p.astype(MXU_DTYPE), dd[n_ch], _DIMS["tn"],
                                                                preferred_element_type=F32)
                    dp = lax.dot_general(dd[n_ch], vv, _DIMS["nt"], preferred_element_type=F32)
                    ds = (p * (dp - dl_c[n_ch])).astype(MXU_DTYPE)
                    dk_acc[keys, q_cols(hh)] += lax.dot_general(ds, qq[n_ch], _DIMS["tn"],
                                                                preferred_element_type=F32) * scale
                    out.append(dq[n_ch] + jnp.dot(ds, kk, preferred_element_type=F32))
                return tuple(out)

            dq = _chunk_loop(_n_chunks(tiles[0], tq, ck, causal), chunk,
                             tuple(jnp.zeros((tq, LANES), F32) for _ in chains), several)
            for n_ch, (t, hh) in enumerate(chains):
                dq_ref[rows[t], q_cols(hh)] = (dq[n_ch] * scale).astype(dq_ref.dtype)

        _tile_sweep(nq, tq, ck, causal, q_tiles, pair=True)
        dk_ref[...] = dk_acc[...].astype(dk_ref.dtype)
        dv_ref[...] = dv_acc[...].astype(dv_ref.dtype)

    q_spec = _whole(T, HP * LANES, lambda g: (0, g), False)
    kv_spec = _whole(T, kw, (lambda g: (0, g)) if group == 1 else (lambda g: (0, (g * HP) // group)), False)
    return pl.pallas_call(
        kern, name=name,
        out_shape=[jax.ShapeDtypeStruct((T, H * LANES), MXU_DTYPE)] * 3,
        grid=(H // HP,),
        in_specs=[q_spec, kv_spec, kv_spec, q_spec, q_spec, q_spec],
        out_specs=[q_spec, q_spec, q_spec],
        scratch_shapes=[pltpu.VMEM((T, HP * LANES), F32)] * 2,
        compiler_params=_params(("parallel",)),
    )(q, k, v, do, lse, delta)


def _ew(name, fn, ins, out_dtypes):
    shape = ins[0].shape
    flat = [a.reshape(-1, shape[-1]) for a in ins]
    R, C = flat[0].shape
    tr = _tile(R, (512, 256, 128, 64, 32, 16, 8))
    n_in = len(ins)

    def kern(*refs):
        res = fn(*[r[...] for r in refs[:n_in]])
        for o_ref, val in zip(refs[n_in:], res):
            o_ref[...] = val.astype(o_ref.dtype)

    spec = pl.BlockSpec((tr, C), lambda i: (i, 0))
    outs = pl.pallas_call(
        kern, name=name,
        out_shape=[jax.ShapeDtypeStruct((R, C), dt) for dt in out_dtypes],
        grid=(R // tr,), in_specs=[spec] * n_in, out_specs=[spec] * len(out_dtypes),
        compiler_params=_params(("parallel",)),
    )(*flat)
    return [o.reshape(shape) for o in outs]


def _adamw(name, w, g, m, v):
    c1 = 1.0 - ADAM_B1 ** ADAM_STEP
    c2 = 1.0 - ADAM_B2 ** ADAM_STEP

    def fn(wv, gv, mv, vv):
        mn = ADAM_B1 * mv + (1.0 - ADAM_B1) * gv
        vn = ADAM_B2 * vv + (1.0 - ADAM_B2) * (gv * gv)
        delta = -ADAM_LR * ((mn / c1) / (jnp.sqrt(vn / c2) + ADAM_EPS) + ADAM_WD * wv)
        return delta, mn, vn

    return _ew(name, fn, [w, g, m, v], [F32, F32, F32])


_ANY = pl.BlockSpec(memory_space=pl.ANY)


def _where_am_i():
    x, y, c = lax.axis_index("x"), lax.axis_index("y"), lax.axis_index("c")
    chips = [(1 - x, y), (x, 1 - y), (1 - x, 1 - y)]
    return x, y, c, chips


def _gather_weights(shards, meta):
    arrs = list(shards) + [meta]
    n = len(arrs)
    per = [a.shape[0] // 2 for a in arrs]

    def body(*refs):
        ins, outs = refs[:n], refs[n:2 * n]
        send1, recv1, send2, recv2 = refs[2 * n:]
        x, y, c, chips = _where_am_i()
        me = 2 * x + y

        def half(ref, k, cc):
            return ref.at[pl.ds(per[k] * cc, per[k])]

        first = []
        for k in range(n):
            for j, (cx, cy) in enumerate(chips):
                first.append(pltpu.make_async_remote_copy(
                    src_ref=half(ins[k], k, c), dst_ref=half(outs[k].at[me], k, c),
                    send_sem=send1.at[k, j], recv_sem=recv1.at[k, j],
                    device_id=(cx, cy, c), device_id_type=MESH))
        for cp in first:
            cp.start()
        passed = []
        for k in range(n):
            for j, (cx, cy) in enumerate(chips):
                landed = half(outs[k].at[2 * cx + cy], k, c)
                pltpu.make_async_remote_copy(
                    src_ref=landed, dst_ref=landed, send_sem=send1.at[k, j], recv_sem=recv1.at[k, j],
                    device_id=(cx, cy, c), device_id_type=MESH).wait_recv()
                fwd = pltpu.make_async_remote_copy(
                    src_ref=landed, dst_ref=landed, send_sem=send2.at[k, j], recv_sem=recv2.at[k, j],
                    device_id=(x, y, 1 - c), device_id_type=MESH)
                fwd.start()
                passed.append(fwd)
        for k in range(n):
            for j, (cx, cy) in enumerate(chips):
                other = half(outs[k].at[2 * cx + cy], k, 1 - c)
                pltpu.make_async_remote_copy(
                    src_ref=other, dst_ref=other, send_sem=send2.at[k, j], recv_sem=recv2.at[k, j],
                    device_id=(x, y, 1 - c), device_id_type=MESH).wait_recv()
        for cp in first + passed:
            cp.wait_send()

    return pl.pallas_call(
        body, name="gather_weights",
        out_shape=[jax.ShapeDtypeStruct((4,) + a.shape, a.dtype) for a in arrs],
        in_specs=[_ANY] * n, out_specs=[_ANY] * n,
        scratch_shapes=[pltpu.SemaphoreType.DMA((n, 3))] * 4,
    )(*arrs)


_HBM = pl.BlockSpec(memory_space=pltpu.HBM)
_SEM = pl.BlockSpec(memory_space=pltpu.SEMAPHORE)
_FLOWS = pltpu.SideEffectType.DATAFLOW_SIDE_EFFECTING


def _in_hbm(a):
    return pltpu.with_memory_space_constraint(a, pltpu.HBM)


def _gather_start(shards):
    n = len(shards)
    nl = shards[0].shape[0]
    lands = [lax.empty((4,) + sh.shape[1:], sh.dtype) for _ in range(nl) for sh in shards]

    def body(*refs):
        srcs, land = refs[:n], refs[n:n + n * nl]
        send, recv, token = refs[n + n * nl:n + n * nl + 3]
        x, y, c, chips = _where_am_i()
        me = 2 * x + y
        for l in range(nl):
            for k in range(n):
                for cx, cy in chips:
                    pltpu.make_async_remote_copy(
                        src_ref=srcs[k].at[l], dst_ref=land[l * n + k].at[me],
                        send_sem=send.at[l], recv_sem=recv.at[l],
                        device_id=(cx, cy, c), device_id_type=MESH).start()
        token[...] = jnp.zeros_like(token)

    ins = [_in_hbm(a) for a in list(shards) + lands]
    res = pl.pallas_call(
        body, name="gather_start",
        out_shape=(pltpu.SemaphoreType.DMA((nl,)), pltpu.SemaphoreType.DMA((nl,)),
                   jax.ShapeDtypeStruct((8, LANES), F32)) + tuple(pltpu.HBM(a.shape, a.dtype) for a in ins),
        in_specs=[_HBM] * len(ins),
        out_specs=(_SEM, _SEM, pl.BlockSpec(memory_space=pltpu.VMEM)) + (_HBM,) * len(ins),
        input_output_aliases={i: 3 + i for i in range(len(ins))},
        compiler_params=pltpu.CompilerParams(has_side_effects=_FLOWS),
    )(*ins)
    send, recv, token = res[:3]
    thru = res[3:3 + n]
    lands = res[3 + n:]
    return send, recv, token, list(thru), [list(lands[l * n:(l + 1) * n]) for l in range(nl)]


def _gather_wait(name, send, recv, l, thru, land, after):
    n = len(thru)

    def body(*refs):
        srcs, lands = refs[:n], refs[n:2 * n]
        send_sem, recv_sem = refs[2 * n:2 * n + 2]
        x, y, c, chips = _where_am_i()
        for k in range(n):
            for cx, cy in chips:
                copy = pltpu.make_async_remote_copy(
                    src_ref=srcs[k].at[l], dst_ref=lands[k].at[2 * cx + cy],
                    send_sem=send_sem.at[l], recv_sem=recv_sem.at[l],
                    device_id=(cx, cy, c), device_id_type=MESH)
                copy.wait_send()
                copy.wait_recv()

    ins = list(thru) + list(land)
    res = pl.pallas_call(
        body, name=name,
        out_shape=tuple(pltpu.HBM(a.shape, a.dtype) for a in ins),
        in_specs=[_HBM] * len(ins) + [_SEM, _SEM, _ANY],
        out_specs=(_HBM,) * len(ins),
        input_output_aliases={i: i for i in range(len(ins))},
        compiler_params=pltpu.CompilerParams(has_side_effects=_FLOWS),
    )(*ins, send, recv, after)
    return list(res[:n]), list(res[n:])


def _pair_exchange(grads):
    n = len(grads)

    def body(*refs):
        ins, got = refs[:n], refs[n:2 * n]
        send, recv = refs[2 * n:]
        x, y, c, _ = _where_am_i()
        sib = (x, y, 1 - c)
        for k in range(n):
            h = ins[k].shape[2] // 2
            for ch in range(4):
                for l in range(ins[k].shape[1]):
                    pltpu.make_async_remote_copy(
                        src_ref=ins[k].at[ch, l, pl.ds((1 - c) * h, h)], dst_ref=got[k].at[ch, l],
                        send_sem=send.at[k], recv_sem=recv.at[k], device_id=sib, device_id_type=MESH).start()
        for k in range(n):
            pltpu.make_async_remote_copy(
                src_ref=got[k], dst_ref=got[k], send_sem=send.at[k], recv_sem=recv.at[k],
                device_id=sib, device_id_type=MESH).wait()

    return pl.pallas_call(
        body, name="reduce_pair",
        out_shape=[jax.ShapeDtypeStruct(g.shape[:2] + (g.shape[2] // 2, g.shape[3]), g.dtype) for g in grads],
        in_specs=[_ANY] * n, out_specs=[_ANY] * n,
        scratch_shapes=[pltpu.SemaphoreType.DMA((n,))] * 2,
    )(*grads)


def _half_tiling(r):
    h = r // 2
    tr = _tile(h, (512, 256, 352, 176, 128, 64, 32))
    return h, tr, h // tr


def _pair_add(full, got, c):
    _, ls, r, cols = full.shape
    h, tr, nh = _half_tiling(r)

    def kern(c_ref, a_ref, b_ref, o32_ref, o16_ref):
        tot = a_ref[...] + b_ref[...].astype(F32)
        o32_ref[...] = tot
        o16_ref[...] = tot.astype(o16_ref.dtype)

    blk = (None, None, tr, cols)
    mine = pl.BlockSpec(blk, lambda ch, l, i, cr: (ch, l, cr[0] * nh + i, 0))
    same = pl.BlockSpec(blk, lambda ch, l, i, cr: (ch, l, i, 0))
    return pl.pallas_call(
        kern, name="pair_add",
        out_shape=[jax.ShapeDtypeStruct(got.shape, F32), jax.ShapeDtypeStruct(got.shape, got.dtype)],
        grid_spec=pltpu.PrefetchScalarGridSpec(
            num_scalar_prefetch=1, grid=(4, ls, nh), in_specs=[mine, same], out_specs=[same, same]),
        compiler_params=_params(("parallel", "parallel", "parallel")),
    )(c.reshape(1), full, got)


def _scatter_start(name, parts):
    n = len(parts)
    lands = [lax.empty(p.shape, p.dtype) for p in parts]

    def body(*refs):
        srcs, land = refs[:n], refs[n:2 * n]
        send, recv, token = refs[2 * n:2 * n + 3]
        x, y, c, chips = _where_am_i()
        me = 2 * x + y
        for k in range(n):
            for cx, cy in chips:
                pltpu.make_async_remote_copy(
                    src_ref=srcs[k].at[2 * cx + cy], dst_ref=land[k].at[me], send_sem=send, recv_sem=recv,
                    device_id=(cx, cy, c), device_id_type=MESH).start()
        token[...] = jnp.zeros_like(token)

    ins = [_in_hbm(a) for a in list(parts) + lands]
    res = pl.pallas_call(
        body, name=name,
        out_shape=(pltpu.SemaphoreType.DMA(()), pltpu.SemaphoreType.DMA(()),
                   jax.ShapeDtypeStruct((8, LANES), F32)) + tuple(pltpu.HBM(a.shape, a.dtype) for a in ins),
        in_specs=[_HBM] * len(ins),
        out_specs=(_SEM, _SEM, pl.BlockSpec(memory_space=pltpu.VMEM)) + (_HBM,) * len(ins),
        input_output_aliases={i: 3 + i for i in range(len(ins))},
        compiler_params=pltpu.CompilerParams(has_side_effects=_FLOWS),
    )(*ins)
    return res[0], res[1], res[2], list(res[3:3 + n]), list(res[3 + n:])


def _scatter_wait(name, send, recv, thru, lands, after):
    n = len(thru)

    def body(*refs):
        srcs, land = refs[:n], refs[n:2 * n]
        send_sem, recv_sem = refs[2 * n:2 * n + 2]
        x, y, c, chips = _where_am_i()
        for k in range(n):
            for cx, cy in chips:
                copy = pltpu.make_async_remote_copy(
                    src_ref=srcs[k].at[2 * cx + cy], dst_ref=land[k].at[2 * cx + cy],
                    send_sem=send_sem, recv_sem=recv_sem, device_id=(cx, cy, c), device_id_type=MESH)
                copy.wait_send()
                copy.wait_recv()

    ins = list(thru) + list(lands)
    res = pl.pallas_call(
        body, name=name,
        out_shape=tuple(pltpu.HBM(a.shape, a.dtype) for a in ins),
        in_specs=[_HBM] * len(ins) + [_SEM, _SEM, _ANY],
        out_specs=(_HBM,) * len(ins),
        input_output_aliases={i: i for i in range(len(ins))},
        compiler_params=pltpu.CompilerParams(has_side_effects=_FLOWS),
    )(*ins, send, recv, after)
    return list(res[n:])


def _chip_add(landed, mine, me, c, into, l0):
    _, ls, h, cols = landed.shape
    _, tr, nh = _half_tiling(2 * h)

    def kern(me_ref, c_ref, land_ref, own_ref, into_ref, o_ref):
        own = own_ref[...]
        tot = None
        for j in range(4):
            term = jnp.where(me_ref[0] == j, own, land_ref[j].astype(F32))
            tot = term if tot is None else tot + term
        o_ref[...] = tot

    return pl.pallas_call(
        kern, name="chip_add",
        out_shape=jax.ShapeDtypeStruct(into.shape, F32),
        grid_spec=pltpu.PrefetchScalarGridSpec(
            num_scalar_prefetch=2, grid=(ls, nh),
            in_specs=[pl.BlockSpec((4, None, tr, cols), lambda l, i, mr, cr: (0, l, i, 0)),
                      pl.BlockSpec((None, None, tr, cols), lambda l, i, mr, cr: (mr[0], l, i, 0)),
                      pl.BlockSpec(memory_space=pl.ANY)],
            out_specs=pl.BlockSpec((None, tr, cols), lambda l, i, mr, cr: (l0 + l, cr[0] * nh + i, 0))),
        input_output_aliases={4: 0},
        compiler_params=_params(("parallel", "parallel")),
    )(me.reshape(1), c.reshape(1), landed, mine, into)


def _pair_join(sums):
    n = len(sums)

    def body(*refs):
        bufs = refs[n:2 * n]
        send, recv = refs[2 * n:]
        x, y, c, _ = _where_am_i()
        sib = (x, y, 1 - c)
        for k in range(n):
            h = bufs[k].shape[1] // 2
            for l in range(bufs[k].shape[0]):
                piece = bufs[k].at[l, pl.ds(c * h, h)]
                pltpu.make_async_remote_copy(
                    src_ref=piece, dst_ref=piece, send_sem=send.at[k], recv_sem=recv.at[k],
                    device_id=sib, device_id_type=MESH).start()
        for k in range(n):
            h = bufs[k].shape[1] // 2
            theirs = bufs[k].at[:, pl.ds((1 - c) * h, h)]
            pltpu.make_async_remote_copy(
                src_ref=theirs, dst_ref=theirs, send_sem=send.at[k], recv_sem=recv.at[k],
                device_id=sib, device_id_type=MESH).wait()

    return pl.pallas_call(
        body, name="reduce_join",
        out_shape=[jax.ShapeDtypeStruct(s.shape, s.dtype) for s in sums],
        in_specs=[_ANY] * n, out_specs=[_ANY] * n,
        input_output_aliases={k: k for k in range(n)},
        scratch_shapes=[pltpu.SemaphoreType.DMA((n,))] * 2,
    )(*sums)


def _allreduce_small(buf):
    R = buf.shape[0]

    def body(in_ref, out_ref, land, send, recv):
        x, y, c, _ = _where_am_i()
        me = 4 * x + 2 * y + c
        land[me] = in_ref[...]
        cps = []
        for k in range(1, 8):
            px, py, pc = x ^ (k >> 2), y ^ ((k >> 1) & 1), c ^ (k & 1)
            cps.append(pltpu.make_async_remote_copy(
                src_ref=in_ref, dst_ref=land.at[me], send_sem=send.at[k - 1], recv_sem=recv.at[k - 1],
                device_id=(px, py, pc), device_id_type=MESH))
        for cp in cps:
            cp.start()
        for k in range(1, 8):
            px, py, pc = x ^ (k >> 2), y ^ ((k >> 1) & 1), c ^ (k & 1)
            slot = land.at[4 * px + 2 * py + pc]
            pltpu.make_async_remote_copy(
                src_ref=slot, dst_ref=slot, send_sem=send.at[k - 1], recv_sem=recv.at[k - 1],
                device_id=(px, py, pc), device_id_type=MESH).wait_recv()
        for cp in cps:
            cp.wait_send()
        tot = land[0]
        for d in range(1, 8):
            tot = tot + land[d]
        out_ref[...] = tot

    vm = pl.BlockSpec(memory_space=pltpu.VMEM)
    return pl.pallas_call(
        body, name="allreduce_small",
        out_shape=jax.ShapeDtypeStruct(buf.shape, F32),
        in_specs=[vm], out_specs=vm,
        scratch_shapes=[pltpu.VMEM((8, R, LANES), F32), pltpu.SemaphoreType.DMA((7,)),
                        pltpu.SemaphoreType.DMA((7,))],
    )(buf)


def _rope_tables(T):
    pos = (jnp.arange(T) - FRONT).astype(F32)
    lane = jnp.arange(LANES)
    inv_a = ROPE_THETA ** (-(2 * ((lane % 64) % 32)).astype(F32) / SWA_HEAD_DIM)
    ang_a = pos[:, None] * inv_a[None, :]
    cosa, sina = jnp.cos(ang_a), jnp.sin(ang_a)
    inv_m = ROPE_THETA ** (-(2 * ((lane - KR_LANE) % 16)).astype(F32) / MLA_ROPE_DIM)
    ang_m = pos[:, None] * inv_m[None, :]
    on = ((lane >= KR_LANE) & (lane < KR_LANE + MLA_ROPE_DIM))[None, :]
    cosm = jnp.where(on, jnp.cos(ang_m), 1.0)
    sinm = jnp.where(on, jnp.sin(ang_m), 0.0)
    return cosa, sina, cosm, sinm


def _cols_from_chips(g):
    return jnp.concatenate(g, axis=-1)


def _rows_from_chips(g):
    return jnp.concatenate(g, axis=-2)


def _cols_to_chips(w):
    c = w.shape[-1] // 4
    return jnp.stack([w[..., j * c:(j + 1) * c] for j in range(4)])


def _rows_to_chips(w):
    L, r4, c = w.shape
    return jnp.moveaxis(w.reshape(L, 4, r4 // 4, c), 1, 0)


def _layer_layouts(w_in, w_qup, w_kvup, w_o, w_gate, w_up, w_down):
    zpad = lambda n: jnp.zeros((D_MODEL, n), w_in.dtype)
    w_in = jnp.concatenate([w_in[:, :C_KR], zpad(KR_LANE), w_in[:, C_KR:IN_W],
                            zpad(LANES - KR_LANE - MLA_ROPE_DIM)], axis=-1)
    w_qup = w_qup.reshape(MLA_Q_RANK, MLA_HEADS, MLA_QK_DIM)
    w_qup = jnp.pad(w_qup, ((0, 0), (0, 0), (0, LANES - MLA_QK_DIM))).reshape(MLA_Q_RANK, SLOT_W)
    return w_in, w_qup, w_kvup, w_o, w_gate, w_up, w_down


def _local_step(x2, target, meta_full, layer_weights, p, early):
    T = BLOCK + x2.shape[0]
    L = DEPTH
    attn_norm, q_norm, kv_norm, sinks = p["attn_norm"], p["q_norm"], p["kv_norm"], p["sinks"]
    out_norm_swa, out_norm_mla, ffn_norm, final_norm = (
        p["out_norm_swa"], p["out_norm_mla"], p["ffn_norm"], p["final_norm"])

    cosa, sina, cosm, sinm = _rope_tables(T)
    no_sink = jnp.full((MLA_HEADS,), NEG, F32)
    scale_a, scale_b = SWA_HEAD_DIM ** -0.5, MLA_QK_DIM ** -0.5
    row = lambda v: v.reshape(1, -1)

    h = jnp.concatenate([jnp.zeros((FRONT, D_MODEL), F32), meta_full, x2], axis=0)
    saved = []
    weights = []
    for l in range(L):
        weights.append(_layer_layouts(*layer_weights(l, h)))
        W_in, W_qup, W_kvup, W_o, W_gate, W_up, W_down = weights[l]
        u, proj = _norm_proj("in_proj", h, row(attn_norm[l]), [W_in], False)
        qa, ka, va, qn, cn, kr = _prep1(proj, row(q_norm[l]), row(kv_norm[l]), cosa, sina, cosm, sinm)
        qb = _mm("q_up", qn, W_qup, "nn")
        kvb = _mm("kv_up", cn, W_kvup, "nn")
        qs, ks, vs = _prep2(qb, kvb, kr, cosm, sinm)
        oa, lse_a = _attn_fwd("swa_fwd", qa, ka, va, sinks[l], 4, False, scale_a, 4)
        ob, lse_b = _attn_fwd("mla_fwd", qs, ks, vs, no_sink, 1, True, scale_b, 1)
        mix = _merge_fwd(oa, ob, row(out_norm_swa[l]), row(out_norm_mla[l]))
        h1 = _mm("o_proj", mix, W_o, "nn", res=h)
        u2, a, b, hm = _norm_proj("ffn_in", h1, row(ffn_norm[l]), [W_gate, W_up], True)
        h2 = _mm("down_proj", hm, W_down, "nn", res=h1)
        saved.append((h, u, proj, qa, ka, va, qn, cn, qs, ks, vs, oa, lse_a, ob, lse_b, mix, h1, u2, a, b, hm))
        h = h2

    dh, d_final, loss_row = _loss_head(h, target, row(final_norm))

    gw = {k: [None] * L for k in ("in", "qup", "kvup", "o", "gate", "up", "down")}
    gs = {k: [None] * L for k in ("attn", "qn", "kvn", "sink", "ga", "gb", "ffn")}

    def natural_grads(layers):
        st = lambda k: jnp.stack(gw[k][layers]) if gw[k][layers][0] is not None else None
        d_in, d_qup = st("in"), st("qup")
        if d_in is not None:
            d_in = jnp.concatenate([d_in[..., :C_KR], d_in[..., C_KR + KR_LANE:C_KR + KR_LANE + MLA_ROPE_DIM]],
                                   axis=-1)
            d_qup = d_qup.reshape(d_qup.shape[0], MLA_Q_RANK, MLA_HEADS, LANES)[..., :MLA_QK_DIM]
            d_qup = d_qup.reshape(d_qup.shape[0], MLA_Q_RANK, -1)
        return [d_in, d_qup, st("kvup"), st("o"), st("gate"), st("up"), st("down")]

    for l in reversed(range(L)):
        (h0, u, proj, qa, ka, va, qn, cn, qs, ks, vs, oa, lse_a, ob, lse_b, mix, h1, u2, a, b, hm) = saved[l]
        W_in, W_qup, W_kvup, W_o, W_gate, W_up, W_down = weights[l]
        if l == 0:
            dh = dh + early("late", natural_grads(slice(1, L)))[0, 0]
        gw["down"][l] = _mm("down_dw", hm, dh, "tn")
        da, db = _ffn_mid_bwd(dh, W_down, a, b)
        gw["gate"][l] = _mm("gate_dw", u2, da, "tn")
        gw["up"][l] = _mm("up_dw", u2, db, "tn")
        du2 = _mm("gate_dx", da, W_gate, "nt")
        du2 = _mm("up_dx", db, W_up, "nt", res=du2)
        gain = row(ffn_norm[l])
        if l == 0:
            gain = gain + early("ffn", natural_grads(slice(0, 1))[4:])[0:1, 0:1]
        dh1, gs["ffn"][l] = _rmsnorm_bwd("ffn_norm_bwd", h1, gain, du2, dh)
        gw["o"][l] = _mm("o_dw", mix, dh1, "tn")
        dmix = _mm("o_dx", dh1, W_o, "nt")
        sink_slots = jnp.repeat(sinks[l], LANES).reshape(1, SLOT_W)
        doa, dla, dob, dlb, gs["ga"][l], gs["gb"][l], dsink = _merge_bwd(
            dmix, oa, ob, lse_a, row(out_norm_swa[l]), row(out_norm_mla[l]), sink_slots)
        gs["sink"][l] = dsink.reshape(SWA_HEADS, LANES)[:, 0]
        dqs, dks, dvs = _attn_bwd("mla_bwd", qs, ks, vs, dob, lse_b, dlb, 1, True, scale_b)
        dqa, dka, dva = _attn_bwd("swa_bwd", qa, ka, va, doa, lse_a, dla, 4, False, scale_a)
        dqb, dkvb, dkr = _prep2_bwd(dqs, dks, dvs, cosm, sinm)
        gw["qup"][l] = _mm("q_up_dw", qn, dqb, "tn")
        gw["kvup"][l] = _mm("kv_up_dw", cn, dkvb, "tn")
        dqn = _mm("q_up_dx", dqb, W_qup, "nt")
        dcn = _mm("kv_up_dx", dkvb, W_kvup, "nt")
        dproj, gs["qn"][l], gs["kvn"][l] = _prep1_bwd(
            proj, dqa, dka, dva, dqn, dcn, dkr, cosa, sina, row(q_norm[l]), row(kv_norm[l]))
        gw["in"][l] = _mm("in_dw", u, dproj, "tn")
        du = _mm("in_dx", dproj, W_in, "nt")
        dh, gs["attn"][l] = _rmsnorm_bwd("attn_norm_bwd", h0, row(attn_norm[l]), du, dh1)

    return loss_row, dh, natural_grads(slice(0, 1))[:4], gs, d_final


def kernel(x, meta_tokens, attn_norm, w_in, q_norm, w_q_up, kv_norm, w_kv_up, sinks, out_norm_swa, out_norm_mla, w_o, ffn_norm, w_gate, w_up, w_down, final_norm, loss_target, m_meta_tokens, m_attn_norm, m_w_in, m_q_norm, m_w_q_up, m_kv_norm, m_w_kv_up, m_sinks, m_out_norm_swa, m_out_norm_mla, m_w_o, m_ffn_norm, m_w_gate, m_w_up, m_w_down, m_final_norm, v_meta_tokens, v_attn_norm, v_w_in, v_q_norm, v_w_q_up, v_kv_norm, v_w_kv_up, v_sinks, v_out_norm_swa, v_out_norm_mla, v_w_o, v_ffn_norm, v_w_gate, v_w_up, v_w_down, v_final_norm):
    assert x.shape[0] == 1 and x.shape[1] % BLOCK == 0
    big = [w_in, w_q_up, w_kv_up, w_o, w_gate, w_up, w_down]

    c_idx = lax.axis_index("c").astype(jnp.int32)
    chip = (2 * lax.axis_index("x") + lax.axis_index("y")).astype(jnp.int32)
    w16 = [w.astype(BF16) for w in big]
    join = [_cols_from_chips, _cols_from_chips, _cols_from_chips, _rows_from_chips, _cols_from_chips,
            _cols_from_chips, _rows_from_chips]

    def whole(own, landed):
        return [f([jnp.where(chip == j, o, g[j]) for j in range(4)]) for f, o, g in zip(join, own, landed)]

    first = _gather_weights([w[0] for w in w16], meta_tokens)
    meta_full = jnp.concatenate([jnp.where(chip == j, meta_tokens, first[-1][j]) for j in range(4)], axis=-1)
    send, recv, token, thru, lands = _gather_start([w[1:] for w in w16])
    state = {"thru": thru}

    def layer_weights(l, h):
        if l == 0:
            return whole([w[0] for w in w16], first[:-1])
        state["thru"], landed = _gather_wait("gather_wait%d" % l, send, recv, l - 1, state["thru"], lands[l - 1], h)
        return whole([w[l] for w in w16], landed)

    small_p = dict(attn_norm=attn_norm, q_norm=q_norm, kv_norm=kv_norm, sinks=sinks, out_norm_swa=out_norm_swa,
                   out_norm_mla=out_norm_mla, ffn_norm=ffn_norm, final_norm=final_norm)
    split = [_cols_to_chips, _cols_to_chips, _cols_to_chips, _rows_to_chips, _cols_to_chips, _cols_to_chips,
             _rows_to_chips]
    rounds = []

    def reduce_start(name, ks, l0, grads):
        full = [split[k](d) for k, d in zip(ks, grads)]
        got = _pair_exchange([f.astype(BF16) for f in full])
        sums = [_pair_add(f, g, c_idx) for f, g in zip(full, got)]
        send, recv, tok, thru, lands = _scatter_start("scatter_start_" + name, [s16 for _, s16 in sums])
        rounds.append((name, ks, l0, [s32 for s32, _ in sums], send, recv, thru, lands))
        return tok

    starts = {"late": lambda g: reduce_start("late", list(range(7)), 1, g),
              "ffn": lambda g: reduce_start("ffn", [4, 5, 6], 0, g)}
    loss_row, dh, d_first, gs, d_final = _local_step(
        x[0] + token[0, 0], loss_target[0], meta_full, layer_weights, small_p, lambda which, g: starts[which](g))
    grad_x = dh[BLOCK:][None]
    reduce_start("first", [0, 1, 2, 3], 0, d_first)

    g_big = [lax.empty(w.shape, F32) for w in big]
    for name, ks, l0, s32, send, recv, thru, lands in rounds:
        landed = _scatter_wait("scatter_wait_" + name, send, recv, thru, lands, dh)
        for k, t, m32 in zip(ks, landed, s32):
            g_big[k] = _chip_add(t, m32, chip, c_idx, g_big[k], l0)
    g_big = _pair_join(g_big)

    small = [jnp.stack(gs["attn"]).reshape(-1), jnp.stack(gs["qn"]).reshape(-1), jnp.stack(gs["kvn"]).reshape(-1),
             jnp.stack(gs["sink"]).reshape(-1), jnp.stack(gs["ga"]).reshape(-1), jnp.stack(gs["gb"]).reshape(-1),
             jnp.stack(gs["ffn"]).reshape(-1), d_final.reshape(-1)]
    sizes = [s.shape[0] for s in small]
    flat = jnp.concatenate(small + [dh[FRONT:BLOCK].reshape(-1), loss_row[0, :1]])
    n_flat = flat.shape[0]
    rows_needed = -(-n_flat // (8 * LANES)) * 8
    flat = jnp.pad(flat, (0, rows_needed * LANES - n_flat)).reshape(rows_needed, LANES)
    tot = _allreduce_small(flat).reshape(-1)
    n_small = sum(sizes)
    loss = tot[n_small + N_META * D_MODEL]
    g_meta_full = tot[n_small:n_small + N_META * D_MODEL].reshape(N_META, D_MODEL)
    g_meta_mine = lax.dynamic_slice_in_dim(g_meta_full, chip * (D_MODEL // 4), D_MODEL // 4, axis=1)

    small_w = [attn_norm, q_norm, kv_norm, sinks, out_norm_swa, out_norm_mla, ffn_norm, final_norm]
    small_m = [m_attn_norm, m_q_norm, m_kv_norm, m_sinks, m_out_norm_swa, m_out_norm_mla, m_ffn_norm, m_final_norm]
    small_v = [v_attn_norm, v_q_norm, v_kv_norm, v_sinks, v_out_norm_swa, v_out_norm_mla, v_ffn_norm, v_final_norm]
    n_rows = -(-n_small // (8 * LANES)) * 8

    def pack(arrs):
        f = jnp.concatenate([a.reshape(-1) for a in arrs])
        return jnp.pad(f, (0, n_rows * LANES - n_small), constant_values=1.0).reshape(n_rows, LANES)

    g_small_pack = jnp.pad(tot[:n_small], (0, n_rows * LANES - n_small)).reshape(n_rows, LANES)
    upd_small = _adamw("adam_small", pack(small_w), g_small_pack, pack(small_m), pack(small_v))

    def unpack(p):
        f = p.reshape(-1)
        out, off = [], 0
        for a, n in zip(small_w, sizes):
            out.append(f[off:off + n].reshape(a.shape))
            off += n
        return out

    g_small = unpack(g_small_pack)
    d_small, m_small, v_small = [unpack(p) for p in upd_small]
    d_meta, nm_meta, nv_meta = _adamw("adam_meta", meta_tokens, g_meta_mine, m_meta_tokens, v_meta_tokens)

    big_m = [m_w_in, m_w_q_up, m_w_kv_up, m_w_o, m_w_gate, m_w_up, m_w_down]
    big_v = [v_w_in, v_w_q_up, v_w_kv_up, v_w_o, v_w_gate, v_w_up, v_w_down]
    upd_big = [_adamw("adam_big", w, g, m, v) for w, g, m, v in zip(big, g_big, big_m, big_v)]

    names = ["meta_tokens", "attn_norm", "w_in", "q_norm", "w_q_up", "kv_norm", "w_kv_up", "sinks",
             "out_norm_swa", "out_norm_mla", "w_o", "ffn_norm", "w_gate", "w_up", "w_down", "final_norm"]
    small_idx = {"attn_norm": 0, "q_norm": 1, "kv_norm": 2, "sinks": 3, "out_norm_swa": 4,
                 "out_norm_mla": 5, "ffn_norm": 6, "final_norm": 7}
    big_idx = {"w_in": 0, "w_q_up": 1, "w_kv_up": 2, "w_o": 3, "w_gate": 4, "w_up": 5, "w_down": 6}
    grads, deltas, new_m, new_v = [], [], [], []
    for nme in names:
        if nme == "meta_tokens":
            quad = (g_meta_mine, d_meta, nm_meta, nv_meta)
        elif nme in small_idx:
            i = small_idx[nme]
            quad = (g_small[i], d_small[i], m_small[i], v_small[i])
        else:
            i = big_idx[nme]
            quad = (g_big[i], *upd_big[i])
        grads.append(quad[0]); deltas.append(quad[1]); new_m.append(quad[2]); new_v.append(quad[3])
    return (loss, grad_x, *grads, *deltas, *new_m, *new_v)
```

```python
import jax
import jax.numpy as jnp
from jax import lax
from jax.experimental import pallas as pl
from jax.experimental.pallas import tpu as pltpu

F32 = jnp.float32
BF16 = jnp.bfloat16
MXU_DTYPE = BF16

D_MODEL = 1024
DEPTH = 4
N_META = 16
BLOCK = 128
WINDOW = 128
ROPE_THETA = 10000.0
EPS = 1e-6
NEG = -1e30
SWA_HEADS = 8
SWA_KV_HEADS = 2
SWA_HEAD_DIM = 64
MLA_HEADS = 8
MLA_Q_RANK = 256
MLA_KV_RANK = 128
MLA_NOPE_DIM = 64
MLA_ROPE_DIM = 32
MLA_V_DIM = 64
MLA_QK_DIM = MLA_NOPE_DIM + MLA_ROPE_DIM
D_FF = 2816
FRONT = (-N_META) % BLOCK
LANES = 128
SLOT_W = 8 * LANES
C_QA, C_KA, C_VA, C_QL, C_KL, C_KR, IN_WP = 0, 512, 640, 768, 1024, 1152, 1280
KR_LANE = 64
IN_W = 1184

ADAM_LR, ADAM_B1, ADAM_B2, ADAM_EPS, ADAM_WD, ADAM_STEP = 0.001, 0.9, 0.999, 1e-08, 0.01, 10

VMEM_LIMIT = 48 * 1024 * 1024
MESH = pl.DeviceIdType.MESH


def _tile(n, prefs):
    for t in prefs:
        if n % t == 0:
            return t
    return n


def _params(sem):
    return pltpu.CompilerParams(dimension_semantics=sem, vmem_limit_bytes=VMEM_LIMIT)


_DIMS = {"nn": (((1,), (0,)), ((), ())), "nt": (((1,), (1,)), ((), ())), "tn": (((0,), (0,)), ((), ()))}


def _mm(name, a, b, mode, out_dtype=F32, res=None):
    if mode == "nn":
        (M, K), (_, N) = a.shape, b.shape
    elif mode == "nt":
        (M, K), (N, _) = a.shape, b.shape
    else:
        (K, M), (_, N) = a.shape, b.shape
    lane_tiles = (1408, 1024, 640, 768, 512, 384, 256, 128)
    row_tiles = (1056, 528, 512, 384, 256, 128)
    bm = _tile(M, lane_tiles if mode == "tn" else row_tiles)
    bn = _tile(N, lane_tiles)
    bk = _tile(K, row_tiles if mode == "tn" else lane_tiles)
    nk = K // bk
    if mode == "tn":
        a_spec = pl.BlockSpec((bk, bm), lambda i, j, k: (k, i))
    else:
        a_spec = pl.BlockSpec((bm, bk), lambda i, j, k: (i, k))
    if mode == "nt":
        b_spec = pl.BlockSpec((bn, bk), lambda i, j, k: (j, k))
    else:
        b_spec = pl.BlockSpec((bk, bn), lambda i, j, k: (k, j))
    o_spec = pl.BlockSpec((bm, bn), lambda i, j, k: (i, j))
    in_specs = [a_spec, b_spec]
    args = [a, b]
    if res is not None:
        in_specs.append(o_spec)
        args.append(res)
    dims = _DIMS[mode]

    def kern(a_ref, b_ref, *rest):
        if res is not None:
            r_ref, o_ref = rest[0], rest[1]
            scr = rest[2:]
        else:
            r_ref, o_ref = None, rest[0]
            scr = rest[1:]
        p = lax.dot_general(a_ref[...].astype(MXU_DTYPE), b_ref[...].astype(MXU_DTYPE), dims,
                            preferred_element_type=F32)

        def finish(val):
            if r_ref is not None:
                val = val + r_ref[...]
            o_ref[...] = val.astype(o_ref.dtype)

        if nk == 1:
            finish(p)
        else:
            acc = scr[0]
            k = pl.program_id(2)

            @pl.when(k == 0)
            def _():
                acc[...] = p

            @pl.when(k > 0)
            def _():
                acc[...] += p

            @pl.when(k == nk - 1)
            def _():
                finish(acc[...])

    return pl.pallas_call(
        kern, name=name,
        out_shape=jax.ShapeDtypeStruct((M, N), out_dtype),
        grid=(M // bm, N // bn, nk),
        in_specs=in_specs, out_specs=o_spec,
        scratch_shapes=[pltpu.VMEM((bm, bn), F32)] if nk > 1 else [],
        compiler_params=_params(("parallel", "parallel", "arbitrary")),
    )(*args)


def _rowmap(name, body, rows, vecs, outs, accs=(), tr_prefs=(384, 256, 128)):
    R = rows[0].shape[0]
    tr = _tile(R, tr_prefs)
    n_r, n_v, n_o, n_a = len(rows), len(vecs), len(outs), len(accs)

    def kern(*refs):
        ins = [r[...] for r in refs[:n_r + n_v]]
        o_refs = refs[n_r + n_v:n_r + n_v + n_o]
        a_refs = refs[n_r + n_v + n_o:]
        res = body(*ins)
        for o_ref, val in zip(o_refs, res[:n_o]):
            o_ref[...] = val.astype(o_ref.dtype)
        if n_a:
            first = pl.program_id(0) == 0

            @pl.when(first)
            def _():
                for a_ref, val in zip(a_refs, res[n_o:]):
                    a_ref[...] = val

            @pl.when(jnp.logical_not(first))
            def _():
                for a_ref, val in zip(a_refs, res[n_o:]):
                    a_ref[...] += val

    in_specs = [pl.BlockSpec((tr, r.shape[1]), lambda i: (i, 0)) for r in rows]
    in_specs += [pl.BlockSpec((1, v.shape[1]), lambda i: (0, 0)) for v in vecs]
    out_specs = [pl.BlockSpec((tr, c), lambda i: (i, 0)) for c, _ in outs]
    out_specs += [pl.BlockSpec((1, c), lambda i: (0, 0)) for c in accs]
    out_shape = [jax.ShapeDtypeStruct((R, c), dt) for c, dt in outs]
    out_shape += [jax.ShapeDtypeStruct((1, c), F32) for c in accs]
    return pl.pallas_call(
        kern, name=name, out_shape=out_shape, grid=(R // tr,),
        in_specs=in_specs, out_specs=out_specs,
        compiler_params=_params(("arbitrary",) if n_a else ("parallel",)),
    )(*rows, *vecs)


def _lane(shape):
    return lax.broadcasted_iota(jnp.int32, shape, 1)


def _rot_swa(x):
    lane = _lane(x.shape)
    return jnp.where((lane & 63) < 32, -pltpu.roll(x, 96, 1), pltpu.roll(x, 32, 1))


def _rot_mla(x):
    lane = _lane(x.shape)
    lo = jnp.where(lane >= KR_LANE, -pltpu.roll(x, 112, 1), 0.0)
    hi = jnp.where(lane < KR_LANE + MLA_ROPE_DIM, pltpu.roll(x, 16, 1), 0.0)
    return jnp.where(lane < KR_LANE + 16, lo, hi)


def _rope(x, cos, sin, rot):
    return x * cos + rot(x) * sin


def _rope_t(g, cos, sin, rot):
    return g * cos - rot(g * sin)


def _low(x):
    return jnp.where(_lane(x.shape) < 64, x, 0.0)


def _value_slot(x):
    lane = _lane(x.shape)
    return jnp.where(lane < 64, x, jnp.where(lane == 64, 1.0, 0.0))


def _blk(x, j):
    return x[:, j * LANES:(j + 1) * LANES]


def _rms_r(x, width):
    return lax.rsqrt(jnp.sum(x * x, axis=-1, keepdims=True) * (1.0 / width) + EPS)


def _rms_bwd(x, g, dy, width):
    r = _rms_r(x, width)
    gdy = dy * g
    dot = jnp.sum(gdy * x, axis=-1, keepdims=True)
    dx = r * gdy - x * (r * r * r * (1.0 / width) * dot)
    return dx, dy * x * r


def _colsum(x):
    return jnp.sum(x, axis=0, keepdims=True)


def _rmsnorm_bwd(name, x, g, dy, dres):
    def body(xv, dyv, dr, gv):
        dx, dg = _rms_bwd(xv, gv, dyv, D_MODEL)
        return dx + dr, _colsum(dg)
    return _rowmap(name, body, [x, dy, dres], [g], [(D_MODEL, F32)], [D_MODEL])


def _prep1(proj, qn_g, kv_g, cosa, sina, cosm, sinm):
    def body(p, ca, sa, cm, sm, gq, gk):
        qa = []
        for j in range(4):
            xr = _rope(_blk(p, j), ca, sa, _rot_swa)
            qa += [_low(xr), _low(pltpu.roll(xr, 64, 1))]
        kr_ = _rope(_blk(p, C_KA // LANES), ca, sa, _rot_swa)
        ka = [_low(kr_), _low(pltpu.roll(kr_, 64, 1))]
        vv = _blk(p, C_VA // LANES)
        va = [_value_slot(vv), _value_slot(pltpu.roll(vv, 64, 1))]
        ql = p[:, C_QL:C_QL + MLA_Q_RANK]
        qn = ql * _rms_r(ql, MLA_Q_RANK) * gq
        kl = p[:, C_KL:C_KL + MLA_KV_RANK]
        cn = kl * _rms_r(kl, MLA_KV_RANK) * gk
        kr = _rope(_blk(p, C_KR // LANES), cm, sm, _rot_mla)
        return (jnp.concatenate(qa, 1), jnp.concatenate(ka, 1), jnp.concatenate(va, 1), qn, cn, kr)
    return _rowmap("prep1", body, [proj, cosa, sina, cosm, sinm], [qn_g, kv_g],
                   [(SLOT_W, BF16), (2 * LANES, BF16), (2 * LANES, BF16),
                    (MLA_Q_RANK, BF16), (MLA_KV_RANK, BF16), (LANES, F32)])


def _prep2(qb, kvb, kr, cosm, sinm):
    def body(q, kv, krv, cm, sm):
        qs, ks, vs = [], [], []
        for h in range(MLA_HEADS):
            qs.append(_rope(_blk(q, h), cm, sm, _rot_mla))
            kvh = _blk(kv, h)
            ks.append(_low(kvh) + krv)
            vs.append(_value_slot(pltpu.roll(kvh, 64, 1)))
        return jnp.concatenate(qs, 1), jnp.concatenate(ks, 1), jnp.concatenate(vs, 1)
    return _rowmap("prep2", body, [qb, kvb, kr, cosm, sinm], [],
                   [(SLOT_W, BF16), (SLOT_W, BF16), (SLOT_W, BF16)])


def _compact(slots):
    return jnp.concatenate(
        [_blk(slots, 2 * j) + pltpu.roll(_blk(slots, 2 * j + 1), 64, 1) for j in range(4)], 1)


def _expand(nat):
    out = []
    for j in range(4):
        b = _blk(nat, j)
        out += [_low(b), _low(pltpu.roll(b, 64, 1))]
    return jnp.concatenate(out, 1)


def _merge_fwd(oa, ob, ga, gb):
    def body(a, b, gav, gbv):
        xa, xb = _compact(a), _compact(b)
        return (jnp.concatenate([xa * _rms_r(xa, 512) * gav, xb * _rms_r(xb, 512) * gbv], 1),)
    return _rowmap("merge_fwd", body, [oa, ob], [ga, gb], [(D_MODEL, BF16)])[0]


def _merge_bwd(dmix, oa, ob, lse_a, ga, gb, sink_slots):
    def body(dm, a, b, lse, gav, gbv, sk):
        outs = []
        accs = []
        for o, g, lo in ((a, gav, 0), (b, gbv, 512)):
            x = _compact(o)
            dx, dg = _rms_bwd(x, g, dm[:, lo:lo + 512], 512)
            do = _expand(dx)
            delta = jnp.concatenate(
                [jnp.broadcast_to(jnp.sum(_blk(do, h) * _blk(o, h), axis=-1, keepdims=True),
                                  (do.shape[0], LANES)) for h in range(8)], 1)
            outs += [do, delta]
            accs.append(_colsum(dg))
        dsink = _colsum(-jnp.exp(sk - lse) * outs[1])
        return (*outs, *accs, dsink)
    return _rowmap("merge_bwd", body, [dmix, oa, ob, lse_a], [ga, gb, sink_slots],
                   [(SLOT_W, BF16), (SLOT_W, F32), (SLOT_W, BF16), (SLOT_W, F32)],
                   [512, 512, SLOT_W])


def _prep2_bwd(dq, dk, dv, cosm, sinm):
    def body(dqv, dkv, dvv, cm, sm):
        dqv, dkv, dvv = dqv.astype(F32), dkv.astype(F32), dvv.astype(F32)
        dqb, dkvb = [], []
        krsum = jnp.zeros((dqv.shape[0], LANES), F32)
        for h in range(MLA_HEADS):
            dqb.append(_rope_t(_blk(dqv, h), cm, sm, _rot_mla))
            dkh = _blk(dkv, h)
            dkvb.append(_low(dkh) + pltpu.roll(_blk(dvv, h), 64, 1))
            krsum = krsum + dkh
        lane = _lane(krsum.shape)
        dkr = jnp.where((lane >= KR_LANE) & (lane < KR_LANE + MLA_ROPE_DIM),
                        _rope_t(krsum, cm, sm, _rot_mla), 0.0)
        return jnp.concatenate(dqb, 1), jnp.concatenate(dkvb, 1), dkr
    return _rowmap("prep2_bwd", body, [dq, dk, dv, cosm, sinm], [],
                   [(SLOT_W, BF16), (SLOT_W, BF16), (LANES, F32)])


def _prep1_bwd(proj, dqa, dka, dva, dqn, dcn, dkr, cosa, sina, qn_g, kv_g):
    def body(p, dq, dk, dv, dqnv, dcnv, dkrv, ca, sa, gq, gk):
        dq, dk, dv = dq.astype(F32), dk.astype(F32), dv.astype(F32)
        cols = []
        for j in range(4):
            nat = _blk(dq, 2 * j) + pltpu.roll(_blk(dq, 2 * j + 1), 64, 1)
            cols.append(_rope_t(nat, ca, sa, _rot_swa))
        grp = lambda d, g: sum(_blk(d, 4 * g + i) for i in range(4))
        cols.append(_rope_t(grp(dk, 0) + pltpu.roll(grp(dk, 1), 64, 1), ca, sa, _rot_swa))
        cols.append(grp(dv, 0) + pltpu.roll(grp(dv, 1), 64, 1))
        dql, dgq = _rms_bwd(p[:, C_QL:C_QL + MLA_Q_RANK], gq, dqnv, MLA_Q_RANK)
        dkl, dgk = _rms_bwd(p[:, C_KL:C_KL + MLA_KV_RANK], gk, dcnv, MLA_KV_RANK)
        cols += [dql, dkl, dkrv]
        return jnp.concatenate(cols, 1), _colsum(dgq), _colsum(dgk)
    return _rowmap("prep1_bwd", body, [proj, dqa, dka, dva, dqn, dcn, dkr, cosa, sina], [qn_g, kv_g],
                   [(IN_WP, BF16)], [MLA_Q_RANK, MLA_KV_RANK], tr_prefs=(192, 128))


def _sigmoid(x):
    return 1.0 / (1.0 + jnp.exp(-x))


def _ffn_tiles(T, F):
    return _tile(T, (528, 512, 384, 256, 128)), _tile(F, (1408, 1024, 640, 512, 256, 128))


def _norm_proj(name, h, g, weights, swiglu):
    (T, D), F = h.shape, weights[0].shape[1]
    bm, bn = _ffn_tiles(T, F)
    nw = len(weights)

    def kern(h_ref, g_ref, *rest):
        w_refs, u_ref, o_refs, u_scr = rest[:nw], rest[nw], rest[nw + 1:-1], rest[-1]

        @pl.when(pl.program_id(1) == 0)
        def _():
            x = h_ref[...]
            u = (x * _rms_r(x, D) * g_ref[...]).astype(u_scr.dtype)
            u_scr[...] = u
            u_ref[...] = u

        uv = u_scr[...]
        prods = [jnp.dot(uv, w[...], preferred_element_type=F32) for w in w_refs]
        for o_ref, p in zip(o_refs, prods):
            o_ref[...] = p
        if swiglu:
            a, b = prods
            o_refs[nw][...] = (a * _sigmoid(a) * b).astype(o_refs[nw].dtype)

    w_spec = pl.BlockSpec((D, bn), lambda i, j: (0, j))
    row_spec = pl.BlockSpec((bm, D), lambda i, j: (i, 0))
    o_spec = pl.BlockSpec((bm, bn), lambda i, j: (i, j))
    n_out = nw + (1 if swiglu else 0)
    return pl.pallas_call(
        kern, name=name,
        out_shape=[jax.ShapeDtypeStruct((T, D), MXU_DTYPE)] + [jax.ShapeDtypeStruct((T, F), F32)] * nw
        + ([jax.ShapeDtypeStruct((T, F), MXU_DTYPE)] if swiglu else []),
        grid=(T // bm, F // bn),
        in_specs=[row_spec, pl.BlockSpec((1, D), lambda i, j: (0, 0))] + [w_spec] * nw,
        out_specs=[row_spec] + [o_spec] * n_out,
        scratch_shapes=[pltpu.VMEM((bm, D), MXU_DTYPE)],
        compiler_params=_params(("parallel", "arbitrary")),
    )(h, g, *weights)


def _ffn_mid_bwd(dh, w_down, a, b):
    (T, D), F = dh.shape, w_down.shape[0]
    bm, bn = _ffn_tiles(T, F)

    def kern(dh_ref, wd_ref, a_ref, b_ref, da_ref, db_ref):
        d = lax.dot_general(dh_ref[...].astype(MXU_DTYPE), wd_ref[...], _DIMS["nt"], preferred_element_type=F32)
        av, bv = a_ref[...], b_ref[...]
        s = _sigmoid(av)
        da_ref[...] = (d * bv * (s * (1.0 + av * (1.0 - s)))).astype(da_ref.dtype)
        db_ref[...] = (d * (av * s)).astype(db_ref.dtype)

    o_spec = pl.BlockSpec((bm, bn), lambda i, j: (i, j))
    return pl.pallas_call(
        kern, name="ffn_mid_bwd",
        out_shape=[jax.ShapeDtypeStruct((T, F), MXU_DTYPE)] * 2,
        grid=(T // bm, F // bn),
        in_specs=[pl.BlockSpec((bm, D), lambda i, j: (i, 0)), pl.BlockSpec((bn, D), lambda i, j: (j, 0)),
                  o_spec, o_spec],
        out_specs=[o_spec, o_spec],
        compiler_params=_params(("parallel", "parallel")),
    )(dh, w_down, a, b)


def _loss_head(h, target, g):
    T = h.shape[0]
    nb = T // BLOCK

    def kern(h_ref, t_ref, g_ref, dh_ref, dg_ref, loss_ref, acc):
        i = pl.program_id(0)

        @pl.when(i == 0)
        def _():
            dh_ref[...] = jnp.zeros_like(dh_ref)
            dg_ref[...] = jnp.zeros_like(dg_ref)
            acc[...] = jnp.zeros_like(acc)

        @pl.when(i > 0)
        def _():
            x = h_ref[...]
            gv = g_ref[...]
            e = x * _rms_r(x, D_MODEL) * gv - t_ref[...]
            acc[...] += _colsum(e * e)
            dx, dg = _rms_bwd(x, gv, e * (1.0 / D_MODEL), D_MODEL)
            dh_ref[...] = dx
            dg_ref[...] += _colsum(dg)

        @pl.when(i == nb - 1)
        def _():
            tot = jnp.sum(acc[...], axis=-1, keepdims=True) * (0.5 / D_MODEL)
            loss_ref[...] = jnp.broadcast_to(tot, loss_ref.shape)

    return pl.pallas_call(
        kern, name="loss_head",
        out_shape=[jax.ShapeDtypeStruct((T, D_MODEL), F32), jax.ShapeDtypeStruct((1, D_MODEL), F32),
                   jax.ShapeDtypeStruct((1, LANES), F32)],
        grid=(nb,),
        in_specs=[pl.BlockSpec((BLOCK, D_MODEL), lambda i: (i, 0)),
                  pl.BlockSpec((BLOCK, D_MODEL), lambda i: (jnp.maximum(i - 1, 0), 0)),
                  pl.BlockSpec((1, D_MODEL), lambda i: (0, 0))],
        out_specs=[pl.BlockSpec((BLOCK, D_MODEL), lambda i: (i, 0)),
                   pl.BlockSpec((1, D_MODEL), lambda i: (0, 0)),
                   pl.BlockSpec((1, LANES), lambda i: (0, 0))],
        scratch_shapes=[pltpu.VMEM((1, D_MODEL), F32)],
        compiler_params=_params(("arbitrary",)),
    )(h, target, g)


LOG2E = 1.4426950408889634


def _attn_plan(T, causal):
    tq = _tile(T, (384, 256, 128))
    ck = min(2 * tq, T) if causal else min(tq + WINDOW, T)
    return tq, ck, (-(-T // ck) if causal else 1)


def _chunk(i, c, T, tq, ck, causal):
    if causal:
        return pl.multiple_of(jnp.minimum(c * ck, T - ck), LANES), c * ck
    return pl.multiple_of(jnp.clip(i * tq - WINDOW, 0, T - ck), LANES), 0


def _n_chunks(i, tq, ck, causal):
    return ((i + 1) * tq + ck - 1) // ck if causal else 1


def _mask(s, i, start, first, tq, ck, causal):
    qpos = i * tq + lax.broadcasted_iota(jnp.int32, (tq, 1), 0)
    kpos = start + lax.broadcasted_iota(jnp.int32, (tq, ck), 1)
    low = jnp.maximum(jnp.where(qpos < FRONT, 0, FRONT), first)
    if not causal:
        low = jnp.maximum(low, qpos - (WINDOW - 1))
    return jnp.where(kpos >= low, jnp.where(kpos <= qpos, s, NEG), NEG)


def _chunk_loop(n, body, init, several):
    carry = body(0, init, True)
    if not several:
        return carry
    carry = body(n - 1, carry, True)
    return lax.fori_loop(1, n - 1, lambda c, cr: body(c, cr, False), carry)


def _tile_sweep(nq, tq, ck, causal, q_tiles, pair=False):
    if pair and nq >= 2 and (not causal or ck == 2 * tq):
        if causal:
            q_tiles([0, 1], False)
            lax.fori_loop(1, nq // 2, lambda p, cr: q_tiles([2 * p, 2 * p + 1], True) or cr, 0)
        else:
            lax.fori_loop(0, nq // 2, lambda p, cr: q_tiles([2 * p, 2 * p + 1], False) or cr, 0)
        if nq % 2:
            q_tiles([nq - 1], causal and nq >= 3)
        return
    one = min(nq, ck // tq) if causal else nq
    lax.fori_loop(0, one, lambda i, cr: q_tiles([i], False) or cr, 0)
    if one < nq:
        lax.fori_loop(one, nq, lambda i, cr: q_tiles([i], True) or cr, 0)


BWD_HEADS_PER_STEP = 1


def _head_cols(group, hp):
    q_cols = lambda hh: slice(hh * LANES, (hh + 1) * LANES)
    if group == 1:
        return q_cols, q_cols, hp * LANES
    assert group % hp == 0
    return q_cols, (lambda hh: slice(0, LANES)), LANES


def _whole(T, width, index, single):
    if single:
        return pl.BlockSpec((T, width), index, pipeline_mode=pl.Buffered(1))
    return pl.BlockSpec((T, width), index)


def _attn_fwd(name, q, k, v, sinks, group, causal, scale, hp):
    T = q.shape[0]
    H = q.shape[1] // LANES
    HP = hp
    tq, ck, slots = _attn_plan(T, causal)
    nq = T // tq
    c2 = scale * LOG2E
    q_cols, k_cols, kw = _head_cols(group, HP)

    def kern(sink_ref, q_ref, k_ref, v_ref, o_ref, lse_ref, s_scr):
        sink2 = [sink_ref[pl.program_id(0) * HP + hh] * LOG2E for hh in range(HP)]

        def q_tiles(tiles, several):
            rows = [pl.ds(pl.multiple_of(i * tq, tq), tq) for i in tiles]
            chains = [(t, hh) for t in range(len(tiles)) for hh in range(HP)]
            qq = [q_ref[rows[t], q_cols(hh)] for t, hh in chains]
            n = _n_chunks(tiles[0], tq, ck, causal)

            def score(c, m2, masked):
                out = []
                for n_ch, (t, hh) in enumerate(chains):
                    start, first = _chunk(tiles[t], c, T, tq, ck, causal)
                    s = lax.dot_general(qq[n_ch], k_ref[pl.ds(start, ck), k_cols(hh)], _DIMS["nt"],
                                        preferred_element_type=F32) * c2
                    if masked:
                        s = _mask(s, tiles[t], start, first, tq, ck, causal)
                    s_scr[n_ch, c] = s
                    out.append(jnp.maximum(m2[n_ch], jnp.max(s, axis=-1, keepdims=True)))
                return tuple(out)

            m2 = _chunk_loop(n, score, tuple(jnp.full((tq, 1), sink2[hh], F32) for _, hh in chains), several)

            def weigh(c, acc):
                out = []
                for n_ch, (t, hh) in enumerate(chains):
                    start, _ = _chunk(tiles[t], c, T, tq, ck, causal)
                    p = jnp.exp2(s_scr[n_ch, c] - m2[n_ch])
                    out.append(acc[n_ch] + jnp.dot(p.astype(MXU_DTYPE), v_ref[pl.ds(start, ck), k_cols(hh)],
                                                   preferred_element_type=F32))
                return tuple(out)

            acc = lax.fori_loop(0, n, weigh, tuple(jnp.zeros((tq, LANES), F32) for _ in chains))
            lane = _lane((tq, LANES))
            for n_ch, (t, hh) in enumerate(chains):
                l = acc[n_ch][:, 64:65] + jnp.exp2(sink2[hh] - m2[n_ch])
                o_ref[rows[t], q_cols(hh)] = jnp.where(lane < 64, acc[n_ch] / l, 0.0)
                lse_ref[rows[t], q_cols(hh)] = jnp.broadcast_to(m2[n_ch] * (1.0 / LOG2E) + jnp.log(l), (tq, LANES))

        _tile_sweep(nq, tq, ck, causal, q_tiles, pair=True)

    q_spec = _whole(T, HP * LANES, lambda g: (0, g), HP > 1)
    kv_spec = _whole(T, kw, (lambda g: (0, g)) if group == 1 else (lambda g: (0, (g * HP) // group)), HP > 1)
    return pl.pallas_call(
        kern, name=name,
        out_shape=[jax.ShapeDtypeStruct((T, H * LANES), F32)] * 2,
        grid=(H // HP,),
        in_specs=[pl.BlockSpec(memory_space=pltpu.SMEM), q_spec, kv_spec, kv_spec],
        out_specs=[q_spec, q_spec],
        scratch_shapes=[pltpu.VMEM((2 * HP, slots, tq, ck), F32)],
        compiler_params=_params(("parallel",)),
    )(sinks, q, k, v)


def _attn_bwd(name, q, k, v, do, lse, delta, group, causal, scale):
    T = q.shape[0]
    H = q.shape[1] // LANES
    HP = BWD_HEADS_PER_STEP
    tq, ck, _ = _attn_plan(T, causal)
    nq = T // tq
    c2 = scale * LOG2E
    q_cols, k_cols, kw = _head_cols(group, HP)

    def kern(q_ref, k_ref, v_ref, do_ref, lse_ref, dl_ref, dq_ref, dk_ref, dv_ref, dk_acc, dv_acc):
        dk_acc[...] = jnp.zeros_like(dk_acc)
        dv_acc[...] = jnp.zeros_like(dv_acc)

        def q_tiles(tiles, several):
            rows = [pl.ds(pl.multiple_of(i * tq, tq), tq) for i in tiles]
            chains = [(t, hh) for t in range(len(tiles)) for hh in range(HP)]
            qq = [q_ref[rows[t], q_cols(hh)] for t, hh in chains]
            dd = [do_ref[rows[t], q_cols(hh)] for t, hh in chains]
            lse2 = [lse_ref[rows[t], q_cols(hh)][:, 0:1] * LOG2E for t, hh in chains]
            dl_c = [dl_ref[rows[t], q_cols(hh)][:, 0:1] for t, hh in chains]

            def chunk(c, dq, masked):
                out = []
                for n_ch, (t, hh) in enumerate(chains):
                    start, first = _chunk(tiles[t], c, T, tq, ck, causal)
                    keys = pl.ds(start, ck)
                    kk, vv = k_ref[keys, k_cols(hh)], v_ref[keys, k_cols(hh)]
                    s = lax.dot_general(qq[n_ch], kk, _DIMS["nt"], preferred_element_type=F32)
                    if masked:
                        s = _mask(s, tiles[t], start, first, tq, ck, causal)
                    p = jnp.exp2(s * c2 - lse2[n_ch])
                    dv_acc[keys, q_cols(hh)] += lax.dot_general(p.astype(MXU_DTYPE), dd[n_ch], _DIMS["tn"],
                                                                preferred_element_type=F32)
                    dp = lax.dot_general(dd[n_ch], vv, _DIMS["nt"], preferred_element_type=F32)
                    ds = (p * (dp - dl_c[n_ch])).astype(MXU_DTYPE)
                    dk_acc[keys, q_cols(hh)] += lax.dot_general(ds, qq[n_ch], _DIMS["tn"],
                                                                preferred_element_type=F32) * scale
                    out.append(dq[n_ch] + jnp.dot(ds, kk, preferred_element_type=F32))
                return tuple(out)

            dq = _chunk_loop(_n_chunks(tiles[0], tq, ck, causal), chunk,
                             tuple(jnp.zeros((tq, LANES), F32) for _ in chains), several)
            for n_ch, (t, hh) in enumerate(chains):
                dq_ref[rows[t], q_cols(hh)] = (dq[n_ch] * scale).astype(dq_ref.dtype)

        _tile_sweep(nq, tq, ck, causal, q_tiles, pair=True)
        dk_ref[...] = dk_acc[...].astype(dk_ref.dtype)
        dv_ref[...] = dv_acc[...].astype(dv_ref.dtype)

    q_spec = _whole(T, HP * LANES, lambda g: (0, g), False)
    kv_spec = _whole(T, kw, (lambda g: (0, g)) if group == 1 else (lambda g: (0, (g * HP) // group)), False)
    return pl.pallas_call(
        kern, name=name,
        out_shape=[jax.ShapeDtypeStruct((T, H * LANES), MXU_DTYPE)] * 3,
        grid=(H // HP,),
        in_specs=[q_spec, kv_spec, kv_spec, q_spec, q_spec, q_spec],
        out_specs=[q_spec, q_spec, q_spec],
        scratch_shapes=[pltpu.VMEM((T, HP * LANES), F32)] * 2,
        compiler_params=_params(("parallel",)),
    )(q, k, v, do, lse, delta)


def _ew(name, fn, ins, out_dtypes):
    shape = ins[0].shape
    flat = [a.reshape(-1, shape[-1]) for a in ins]
    R, C = flat[0].shape
    tr = _tile(R, (512, 256, 128, 64, 32, 16, 8))
    n_in = len(ins)

    def kern(*refs):
        res = fn(*[r[...] for r in refs[:n_in]])
        for o_ref, val in zip(refs[n_in:], res):
            o_ref[...] = val.astype(o_ref.dtype)

    spec = pl.BlockSpec((tr, C), lambda i: (i, 0))
    outs = pl.pallas_call(
        kern, name=name,
        out_shape=[jax.ShapeDtypeStruct((R, C), dt) for dt in out_dtypes],
        grid=(R // tr,), in_specs=[spec] * n_in, out_specs=[spec] * len(out_dtypes),
        compiler_params=_params(("parallel",)),
    )(*flat)
    return [o.reshape(shape) for o in outs]


def _adamw(name, w, g, m, v):
    c1 = 1.0 - ADAM_B1 ** ADAM_STEP
    c2 = 1.0 - ADAM_B2 ** ADAM_STEP

    def fn(wv, gv, mv, vv):
        mn = ADAM_B1 * mv + (1.0 - ADAM_B1) * gv
        vn = ADAM_B2 * vv + (1.0 - ADAM_B2) * (gv * gv)
        delta = -ADAM_LR * ((mn / c1) / (jnp.sqrt(vn / c2) + ADAM_EPS) + ADAM_WD * wv)
        return delta, mn, vn

    return _ew(name, fn, [w, g, m, v], [F32, F32, F32])


_ANY = pl.BlockSpec(memory_space=pl.ANY)


def _where_am_i():
    x, y, c = lax.axis_index("x"), lax.axis_index("y"), lax.axis_index("c")
    chips = [(1 - x, y), (x, 1 - y), (1 - x, 1 - y)]
    return x, y, c, chips


def _gather_weights(shards, meta):
    arrs = list(shards) + [meta]
    n = len(arrs)
    per = [a.shape[0] // 2 for a in arrs]

    def body(*refs):
        ins, outs = refs[:n], refs[n:2 * n]
        send1, recv1, send2, recv2 = refs[2 * n:]
        x, y, c, chips = _where_am_i()
        me = 2 * x + y

        def half(ref, k, cc):
            return ref.at[pl.ds(per[k] * cc, per[k])]

        first = []
        for k in range(n):
            for j, (cx, cy) in enumerate(chips):
                first.append(pltpu.make_async_remote_copy(
                    src_ref=half(ins[k], k, c), dst_ref=half(outs[k].at[me], k, c),
                    send_sem=send1.at[k, j], recv_sem=recv1.at[k, j],
                    device_id=(cx, cy, c), device_id_type=MESH))
        for cp in first:
            cp.start()
        passed = []
        for k in range(n):
            for j, (cx, cy) in enumerate(chips):
                landed = half(outs[k].at[2 * cx + cy], k, c)
                pltpu.make_async_remote_copy(
                    src_ref=landed, dst_ref=landed, send_sem=send1.at[k, j], recv_sem=recv1.at[k, j],
                    device_id=(cx, cy, c), device_id_type=MESH).wait_recv()
                fwd = pltpu.make_async_remote_copy(
                    src_ref=landed, dst_ref=landed, send_sem=send2.at[k, j], recv_sem=recv2.at[k, j],
                    device_id=(x, y, 1 - c), device_id_type=MESH)
                fwd.start()
                passed.append(fwd)
        for k in range(n):
            for j, (cx, cy) in enumerate(chips):
                other = half(outs[k].at[2 * cx + cy], k, 1 - c)
                pltpu.make_async_remote_copy(
                    src_ref=other, dst_ref=other, send_sem=send2.at[k, j], recv_sem=recv2.at[k, j],
                    device_id=(x, y, 1 - c), device_id_type=MESH).wait_recv()
        for cp in first + passed:
            cp.wait_send()

    return pl.pallas_call(
        body, name="gather_weights",
        out_shape=[jax.ShapeDtypeStruct((4,) + a.shape, a.dtype) for a in arrs],
        in_specs=[_ANY] * n, out_specs=[_ANY] * n,
        scratch_shapes=[pltpu.SemaphoreType.DMA((n, 3))] * 4,
    )(*arrs)


_HBM = pl.BlockSpec(memory_space=pltpu.HBM)
_SEM = pl.BlockSpec(memory_space=pltpu.SEMAPHORE)
_FLOWS = pltpu.SideEffectType.DATAFLOW_SIDE_EFFECTING


def _in_hbm(a):
    return pltpu.with_memory_space_constraint(a, pltpu.HBM)


def _gather_start(name, shards):
    n = len(shards)
    nl = shards[0].shape[0]
    lands = [lax.empty((4,) + sh.shape[1:], sh.dtype) for _ in range(nl) for sh in shards]

    def body(*refs):
        srcs, land = refs[:n], refs[n:n + n * nl]
        send, recv, token = refs[n + n * nl:n + n * nl + 3]
        x, y, c, chips = _where_am_i()
        me = 2 * x + y
        for l in range(nl):
            for k in range(n):
                for cx, cy in chips:
                    pltpu.make_async_remote_copy(
                        src_ref=srcs[k].at[l], dst_ref=land[l * n + k].at[me],
                        send_sem=send.at[l], recv_sem=recv.at[l],
                        device_id=(cx, cy, c), device_id_type=MESH).start()
        token[...] = jnp.zeros_like(token)

    ins = [_in_hbm(a) for a in list(shards) + lands]
    res = pl.pallas_call(
        body, name=name,
        out_shape=(pltpu.SemaphoreType.DMA((nl,)), pltpu.SemaphoreType.DMA((nl,)),
                   jax.ShapeDtypeStruct((8, LANES), F32)) + tuple(pltpu.HBM(a.shape, a.dtype) for a in ins),
        in_specs=[_HBM] * len(ins),
        out_specs=(_SEM, _SEM, pl.BlockSpec(memory_space=pltpu.VMEM)) + (_HBM,) * len(ins),
        input_output_aliases={i: 3 + i for i in range(len(ins))},
        compiler_params=pltpu.CompilerParams(has_side_effects=_FLOWS),
    )(*ins)
    send, recv, token = res[:3]
    thru = res[3:3 + n]
    lands = res[3 + n:]
    return send, recv, token, list(thru), [list(lands[l * n:(l + 1) * n]) for l in range(nl)]


def _gather_wait(name, send, recv, l, thru, land, after):
    n = len(thru)

    def body(*refs):
        srcs, lands = refs[:n], refs[n:2 * n]
        send_sem, recv_sem = refs[2 * n:2 * n + 2]
        x, y, c, chips = _where_am_i()
        for k in range(n):
            for cx, cy in chips:
                copy = pltpu.make_async_remote_copy(
                    src_ref=srcs[k].at[l], dst_ref=lands[k].at[2 * cx + cy],
                    send_sem=send_sem.at[l], recv_sem=recv_sem.at[l],
                    device_id=(cx, cy, c), device_id_type=MESH)
                copy.wait_send()
                copy.wait_recv()

    ins = list(thru) + list(land)
    res = pl.pallas_call(
        body, name=name,
        out_shape=tuple(pltpu.HBM(a.shape, a.dtype) for a in ins),
        in_specs=[_HBM] * len(ins) + [_SEM, _SEM, _ANY],
        out_specs=(_HBM,) * len(ins),
        input_output_aliases={i: i for i in range(len(ins))},
        compiler_params=pltpu.CompilerParams(has_side_effects=_FLOWS),
    )(*ins, send, recv, after)
    return list(res[:n]), list(res[n:])


def _pair_exchange(grads):
    n = len(grads)

    def body(*refs):
        ins, got = refs[:n], refs[n:2 * n]
        send, recv = refs[2 * n:]
        x, y, c, _ = _where_am_i()
        sib = (x, y, 1 - c)
        for k in range(n):
            h = ins[k].shape[2] // 2
            for ch in range(4):
                for l in range(ins[k].shape[1]):
                    pltpu.make_async_remote_copy(
                        src_ref=ins[k].at[ch, l, pl.ds((1 - c) * h, h)], dst_ref=got[k].at[ch, l],
                        send_sem=send.at[k], recv_sem=recv.at[k], device_id=sib, device_id_type=MESH).start()
        for k in range(n):
            pltpu.make_async_remote_copy(
                src_ref=got[k], dst_ref=got[k], send_sem=send.at[k], recv_sem=recv.at[k],
                device_id=sib, device_id_type=MESH).wait()

    return pl.pallas_call(
        body, name="reduce_pair",
        out_shape=[jax.ShapeDtypeStruct(g.shape[:2] + (g.shape[2] // 2, g.shape[3]), g.dtype) for g in grads],
        in_specs=[_ANY] * n, out_specs=[_ANY] * n,
        scratch_shapes=[pltpu.SemaphoreType.DMA((n,))] * 2,
    )(*grads)


def _half_tiling(r):
    h = r // 2
    tr = _tile(h, (512, 256, 352, 176, 128, 64, 32))
    return h, tr, h // tr


def _pair_add(full, got, c):
    _, ls, r, cols = full.shape
    h, tr, nh = _half_tiling(r)

    def kern(c_ref, a_ref, b_ref, o32_ref, o16_ref):
        tot = a_ref[...] + b_ref[...].astype(F32)
        o32_ref[...] = tot
        o16_ref[...] = tot.astype(o16_ref.dtype)

    blk = (None, None, tr, cols)
    mine = pl.BlockSpec(blk, lambda ch, l, i, cr: (ch, l, cr[0] * nh + i, 0))
    same = pl.BlockSpec(blk, lambda ch, l, i, cr: (ch, l, i, 0))
    return pl.pallas_call(
        kern, name="pair_add",
        out_shape=[jax.ShapeDtypeStruct(got.shape, F32), jax.ShapeDtypeStruct(got.shape, got.dtype)],
        grid_spec=pltpu.PrefetchScalarGridSpec(
            num_scalar_prefetch=1, grid=(4, ls, nh), in_specs=[mine, same], out_specs=[same, same]),
        compiler_params=_params(("parallel", "parallel", "parallel")),
    )(c.reshape(1), full, got)


def _scatter_start(name, parts):
    n = len(parts)
    lands = [lax.empty(p.shape, p.dtype) for p in parts]

    def body(*refs):
        srcs, land = refs[:n], refs[n:2 * n]
        send, recv, token = refs[2 * n:2 * n + 3]
        x, y, c, chips = _where_am_i()
        me = 2 * x + y
        for k in range(n):
            for cx, cy in chips:
                pltpu.make_async_remote_copy(
                    src_ref=srcs[k].at[2 * cx + cy], dst_ref=land[k].at[me], send_sem=send, recv_sem=recv,
                    device_id=(cx, cy, c), device_id_type=MESH).start()
        token[...] = jnp.zeros_like(token)

    ins = [_in_hbm(a) for a in list(parts) + lands]
    res = pl.pallas_call(
        body, name=name,
        out_shape=(pltpu.SemaphoreType.DMA(()), pltpu.SemaphoreType.DMA(()),
                   jax.ShapeDtypeStruct((8, LANES), F32)) + tuple(pltpu.HBM(a.shape, a.dtype) for a in ins),
        in_specs=[_HBM] * len(ins),
        out_specs=(_SEM, _SEM, pl.BlockSpec(memory_space=pltpu.VMEM)) + (_HBM,) * len(ins),
        input_output_aliases={i: 3 + i for i in range(len(ins))},
        compiler_params=pltpu.CompilerParams(has_side_effects=_FLOWS),
    )(*ins)
    return res[0], res[1], res[2], list(res[3:3 + n]), list(res[3 + n:])


def _scatter_wait(name, send, recv, thru, lands, after):
    n = len(thru)

    def body(*refs):
        srcs, land = refs[:n], refs[n:2 * n]
        send_sem, recv_sem = refs[2 * n:2 * n + 2]
        x, y, c, chips = _where_am_i()
        for k in range(n):
            for cx, cy in chips:
                copy = pltpu.make_async_remote_copy(
                    src_ref=srcs[k].at[2 * cx + cy], dst_ref=land[k].at[2 * cx + cy],
                    send_sem=send_sem, recv_sem=recv_sem, device_id=(cx, cy, c), device_id_type=MESH)
                copy.wait_send()
                copy.wait_recv()

    ins = list(thru) + list(lands)
    res = pl.pallas_call(
        body, name=name,
        out_shape=tuple(pltpu.HBM(a.shape, a.dtype) for a in ins),
        in_specs=[_HBM] * len(ins) + [_SEM, _SEM, _ANY],
        out_specs=(_HBM,) * len(ins),
        input_output_aliases={i: i for i in range(len(ins))},
        compiler_params=pltpu.CompilerParams(has_side_effects=_FLOWS),
    )(*ins, send, recv, after)
    return list(res[n:])


def _chip_add(landed, mine, me, c, into, l0):
    _, ls, h, cols = landed.shape
    _, tr, nh = _half_tiling(2 * h)

    def kern(me_ref, c_ref, land_ref, own_ref, into_ref, o_ref):
        own = own_ref[...]
        tot = None
        for j in range(4):
            term = jnp.where(me_ref[0] == j, own, land_ref[j].astype(F32))
            tot = term if tot is None else tot + term
        o_ref[...] = tot

    return pl.pallas_call(
        kern, name="chip_add",
        out_shape=jax.ShapeDtypeStruct(into.shape, F32),
        grid_spec=pltpu.PrefetchScalarGridSpec(
            num_scalar_prefetch=2, grid=(ls, nh),
            in_specs=[pl.BlockSpec((4, None, tr, cols), lambda l, i, mr, cr: (0, l, i, 0)),
                      pl.BlockSpec((None, None, tr, cols), lambda l, i, mr, cr: (mr[0], l, i, 0)),
                      pl.BlockSpec(memory_space=pl.ANY)],
            out_specs=pl.BlockSpec((None, tr, cols), lambda l, i, mr, cr: (l0 + l, cr[0] * nh + i, 0))),
        input_output_aliases={4: 0},
        compiler_params=_params(("parallel", "parallel")),
    )(me.reshape(1), c.reshape(1), landed, mine, into)


def _pair_join(sums):
    n = len(sums)

    def body(*refs):
        bufs = refs[n:2 * n]
        send, recv = refs[2 * n:]
        x, y, c, _ = _where_am_i()
        sib = (x, y, 1 - c)
        for k in range(n):
            h = bufs[k].shape[1] // 2
            for l in range(bufs[k].shape[0]):
                piece = bufs[k].at[l, pl.ds(c * h, h)]
                pltpu.make_async_remote_copy(
                    src_ref=piece, dst_ref=piece, send_sem=send.at[k], recv_sem=recv.at[k],
                    device_id=sib, device_id_type=MESH).start()
        for k in range(n):
            h = bufs[k].shape[1] // 2
            theirs = bufs[k].at[:, pl.ds((1 - c) * h, h)]
            pltpu.make_async_remote_copy(
                src_ref=theirs, dst_ref=theirs, send_sem=send.at[k], recv_sem=recv.at[k],
                device_id=sib, device_id_type=MESH).wait()

    return pl.pallas_call(
        body, name="reduce_join",
        out_shape=[jax.ShapeDtypeStruct(s.shape, s.dtype) for s in sums],
        in_specs=[_ANY] * n, out_specs=[_ANY] * n,
        input_output_aliases={k: k for k in range(n)},
        scratch_shapes=[pltpu.SemaphoreType.DMA((n,))] * 2,
    )(*sums)


def _allreduce_small(buf):
    R = buf.shape[0]

    def body(in_ref, out_ref, land, send, recv):
        x, y, c, _ = _where_am_i()
        me = 4 * x + 2 * y + c
        land[me] = in_ref[...]
        cps = []
        for k in range(1, 8):
            px, py, pc = x ^ (k >> 2), y ^ ((k >> 1) & 1), c ^ (k & 1)
            cps.append(pltpu.make_async_remote_copy(
                src_ref=in_ref, dst_ref=land.at[me], send_sem=send.at[k - 1], recv_sem=recv.at[k - 1],
                device_id=(px, py, pc), device_id_type=MESH))
        for cp in cps:
            cp.start()
        for k in range(1, 8):
            px, py, pc = x ^ (k >> 2), y ^ ((k >> 1) & 1), c ^ (k & 1)
            slot = land.at[4 * px + 2 * py + pc]
            pltpu.make_async_remote_copy(
                src_ref=slot, dst_ref=slot, send_sem=send.at[k - 1], recv_sem=recv.at[k - 1],
                device_id=(px, py, pc), device_id_type=MESH).wait_recv()
        for cp in cps:
            cp.wait_send()
        tot = land[0]
        for d in range(1, 8):
            tot = tot + land[d]
        out_ref[...] = tot

    vm = pl.BlockSpec(memory_space=pltpu.VMEM)
    return pl.pallas_call(
        body, name="allreduce_small",
        out_shape=jax.ShapeDtypeStruct(buf.shape, F32),
        in_specs=[vm], out_specs=vm,
        scratch_shapes=[pltpu.VMEM((8, R, LANES), F32), pltpu.SemaphoreType.DMA((7,)),
                        pltpu.SemaphoreType.DMA((7,))],
    )(buf)


def _rope_tables(T):
    pos = (jnp.arange(T) - FRONT).astype(F32)
    lane = jnp.arange(LANES)
    inv_a = ROPE_THETA ** (-(2 * ((lane % 64) % 32)).astype(F32) / SWA_HEAD_DIM)
    ang_a = pos[:, None] * inv_a[None, :]
    cosa, sina = jnp.cos(ang_a), jnp.sin(ang_a)
    inv_m = ROPE_THETA ** (-(2 * ((lane - KR_LANE) % 16)).astype(F32) / MLA_ROPE_DIM)
    ang_m = pos[:, None] * inv_m[None, :]
    on = ((lane >= KR_LANE) & (lane < KR_LANE + MLA_ROPE_DIM))[None, :]
    cosm = jnp.where(on, jnp.cos(ang_m), 1.0)
    sinm = jnp.where(on, jnp.sin(ang_m), 0.0)
    return cosa, sina, cosm, sinm


def _cols_from_chips(g):
    return jnp.concatenate(g, axis=-1)


def _rows_from_chips(g):
    return jnp.concatenate(g, axis=-2)


def _cols_to_chips(w):
    c = w.shape[-1] // 4
    return jnp.stack([w[..., j * c:(j + 1) * c] for j in range(4)])


def _rows_to_chips(w):
    L, r4, c = w.shape
    return jnp.moveaxis(w.reshape(L, 4, r4 // 4, c), 1, 0)


def _layer_layouts(w_in, w_qup, w_kvup, w_o, w_gate, w_up, w_down):
    zpad = lambda n: jnp.zeros((D_MODEL, n), w_in.dtype)
    w_in = jnp.concatenate([w_in[:, :C_KR], zpad(KR_LANE), w_in[:, C_KR:IN_W],
                            zpad(LANES - KR_LANE - MLA_ROPE_DIM)], axis=-1)
    w_qup = w_qup.reshape(MLA_Q_RANK, MLA_HEADS, MLA_QK_DIM)
    w_qup = jnp.pad(w_qup, ((0, 0), (0, 0), (0, LANES - MLA_QK_DIM))).reshape(MLA_Q_RANK, SLOT_W)
    return w_in, w_qup, w_kvup, w_o, w_gate, w_up, w_down


def _local_step(x2, target, meta_full, layer_weights, p, early):
    T = BLOCK + x2.shape[0]
    L = DEPTH
    attn_norm, q_norm, kv_norm, sinks = p["attn_norm"], p["q_norm"], p["kv_norm"], p["sinks"]
    out_norm_swa, out_norm_mla, ffn_norm, final_norm = (
        p["out_norm_swa"], p["out_norm_mla"], p["ffn_norm"], p["final_norm"])

    cosa, sina, cosm, sinm = _rope_tables(T)
    no_sink = jnp.full((MLA_HEADS,), NEG, F32)
    scale_a, scale_b = SWA_HEAD_DIM ** -0.5, MLA_QK_DIM ** -0.5
    row = lambda v: v.reshape(1, -1)

    h = jnp.concatenate([jnp.zeros((FRONT, D_MODEL), F32), meta_full, x2], axis=0)
    saved = []
    weights = []
    for l in range(L):
        weights.append(_layer_layouts(*layer_weights(l, h)))
        W_in, W_qup, W_kvup, W_o, W_gate, W_up, W_down = weights[l]
        u, proj = _norm_proj("in_proj", h, row(attn_norm[l]), [W_in], False)
        qa, ka, va, qn, cn, kr = _prep1(proj, row(q_norm[l]), row(kv_norm[l]), cosa, sina, cosm, sinm)
        qb = _mm("q_up", qn, W_qup, "nn")
        kvb = _mm("kv_up", cn, W_kvup, "nn")
        qs, ks, vs = _prep2(qb, kvb, kr, cosm, sinm)
        oa, lse_a = _attn_fwd("swa_fwd", qa, ka, va, sinks[l], 4, False, scale_a, 4)
        ob, lse_b = _attn_fwd("mla_fwd", qs, ks, vs, no_sink, 1, True, scale_b, 1)
        mix = _merge_fwd(oa, ob, row(out_norm_swa[l]), row(out_norm_mla[l]))
        h1 = _mm("o_proj", mix, W_o, "nn", res=h)
        if callable(W_gate):
            W_gate, W_up, W_down = W_gate(h1), W_up(h1), W_down(h1)
            weights[l] = (W_in, W_qup, W_kvup, W_o, W_gate, W_up, W_down)
        u2, a, b, hm = _norm_proj("ffn_in", h1, row(ffn_norm[l]), [W_gate, W_up], True)
        h2 = _mm("down_proj", hm, W_down, "nn", res=h1)
        saved.append((h, u, proj, qa, ka, va, qn, cn, qs, ks, vs, oa, lse_a, ob, lse_b, mix, h1, u2, a, b, hm))
        h = h2

    dh, d_final, loss_row = _loss_head(h, target, row(final_norm))

    gw = {k: [None] * L for k in ("in", "qup", "kvup", "o", "gate", "up", "down")}
    gs = {k: [None] * L for k in ("attn", "qn", "kvn", "sink", "ga", "gb", "ffn")}

    def natural_grads(layers):
        st = lambda k: jnp.stack(gw[k][layers]) if gw[k][layers][0] is not None else None
        d_in, d_qup = st("in"), st("qup")
        if d_in is not None:
            d_in = jnp.concatenate([d_in[..., :C_KR], d_in[..., C_KR + KR_LANE:C_KR + KR_LANE + MLA_ROPE_DIM]],
                                   axis=-1)
            d_qup = d_qup.reshape(d_qup.shape[0], MLA_Q_RANK, MLA_HEADS, LANES)[..., :MLA_QK_DIM]
            d_qup = d_qup.reshape(d_qup.shape[0], MLA_Q_RANK, -1)
        return [d_in, d_qup, st("kvup"), st("o"), st("gate"), st("up"), st("down")]

    for l in reversed(range(L)):
        (h0, u, proj, qa, ka, va, qn, cn, qs, ks, vs, oa, lse_a, ob, lse_b, mix, h1, u2, a, b, hm) = saved[l]
        W_in, W_qup, W_kvup, W_o, W_gate, W_up, W_down = weights[l]
        if l == 0:
            dh = dh + early("late", natural_grads(slice(1, L)))[0, 0]
        gw["down"][l] = _mm("down_dw", hm, dh, "tn")
        da, db = _ffn_mid_bwd(dh, W_down, a, b)
        gw["gate"][l] = _mm("gate_dw", u2, da, "tn")
        gw["up"][l] = _mm("up_dw", u2, db, "tn")
        du2 = _mm("gate_dx", da, W_gate, "nt")
        du2 = _mm("up_dx", db, W_up, "nt", res=du2)
        gain = row(ffn_norm[l])
        if l == 0:
            gain = gain + early("ffn", natural_grads(slice(0, 1))[4:])[0:1, 0:1]
        dh1, gs["ffn"][l] = _rmsnorm_bwd("ffn_norm_bwd", h1, gain, du2, dh)
        gw["o"][l] = _mm("o_dw", mix, dh1, "tn")
        dmix = _mm("o_dx", dh1, W_o, "nt")
        sink_slots = jnp.repeat(sinks[l], LANES).reshape(1, SLOT_W)
        doa, dla, dob, dlb, gs["ga"][l], gs["gb"][l], dsink = _merge_bwd(
            dmix, oa, ob, lse_a, row(out_norm_swa[l]), row(out_norm_mla[l]), sink_slots)
        gs["sink"][l] = dsink.reshape(SWA_HEADS, LANES)[:, 0]
        dqs, dks, dvs = _attn_bwd("mla_bwd", qs, ks, vs, dob, lse_b, dlb, 1, True, scale_b)
        dqa, dka, dva = _attn_bwd("swa_bwd", qa, ka, va, doa, lse_a, dla, 4, False, scale_a)
        dqb, dkvb, dkr = _prep2_bwd(dqs, dks, dvs, cosm, sinm)
        gw["qup"][l] = _mm("q_up_dw", qn, dqb, "tn")
        gw["kvup"][l] = _mm("kv_up_dw", cn, dkvb, "tn")
        dqn = _mm("q_up_dx", dqb, W_qup, "nt")
        dcn = _mm("kv_up_dx", dkvb, W_kvup, "nt")
        dproj, gs["qn"][l], gs["kvn"][l] = _prep1_bwd(
            proj, dqa, dka, dva, dqn, dcn, dkr, cosa, sina, row(q_norm[l]), row(kv_norm[l]))
        gw["in"][l] = _mm("in_dw", u, dproj, "tn")
        du = _mm("in_dx", dproj, W_in, "nt")
        dh, gs["attn"][l] = _rmsnorm_bwd("attn_norm_bwd", h0, row(attn_norm[l]), du, dh1)

    return loss_row, dh, natural_grads(slice(0, 1))[:4], gs, d_final


def kernel(x, meta_tokens, attn_norm, w_in, q_norm, w_q_up, kv_norm, w_kv_up, sinks, out_norm_swa, out_norm_mla, w_o, ffn_norm, w_gate, w_up, w_down, final_norm, loss_target, m_meta_tokens, m_attn_norm, m_w_in, m_q_norm, m_w_q_up, m_kv_norm, m_w_kv_up, m_sinks, m_out_norm_swa, m_out_norm_mla, m_w_o, m_ffn_norm, m_w_gate, m_w_up, m_w_down, m_final_norm, v_meta_tokens, v_attn_norm, v_w_in, v_q_norm, v_w_q_up, v_kv_norm, v_w_kv_up, v_sinks, v_out_norm_swa, v_out_norm_mla, v_w_o, v_ffn_norm, v_w_gate, v_w_up, v_w_down, v_final_norm):
    assert x.shape[0] == 1 and x.shape[1] % BLOCK == 0
    big = [w_in, w_q_up, w_kv_up, w_o, w_gate, w_up, w_down]

    c_idx = lax.axis_index("c").astype(jnp.int32)
    chip = (2 * lax.axis_index("x") + lax.axis_index("y")).astype(jnp.int32)
    w16 = [w.astype(BF16) for w in big]
    join = [_cols_from_chips, _cols_from_chips, _cols_from_chips, _rows_from_chips, _cols_from_chips,
            _cols_from_chips, _rows_from_chips]

    def whole(ks, own, landed):
        return [join[k]([jnp.where(chip == j, o, g[j]) for j in range(4)]) for k, o, g in zip(ks, own, landed)]

    head, ffn = [0, 1, 2, 3], [4, 5, 6]
    first = _gather_weights([w16[k][0] for k in head], meta_tokens)
    meta_full = jnp.concatenate([jnp.where(chip == j, meta_tokens, first[-1][j]) for j in range(4)], axis=-1)
    send0, recv0, token0, thru0, lands0 = _gather_start("gather_start_ffn", [w16[k][0:1] for k in ffn])
    send, recv, token, thru, lands = _gather_start("gather_start", [w[1:] for w in w16])
    state = {"thru": thru}

    def ffn_first(h1):
        if "ffn" not in state:
            _, landed = _gather_wait("gather_wait_ffn", send0, recv0, 0, thru0, lands0[0], h1)
            state["ffn"] = whole(ffn, [w16[k][0] for k in ffn], landed)
        return state["ffn"]

    def layer_weights(l, h):
        if l == 0:
            return whole(head, [w16[k][0] for k in head], first[:-1]) + [
                (lambda h1, i=i: ffn_first(h1)[i]) for i in range(3)]
        state["thru"], landed = _gather_wait("gather_wait%d" % l, send, recv, l - 1, state["thru"], lands[l - 1], h)
        return whole(head + ffn, [w[l] for w in w16], landed)

    small_p = dict(attn_norm=attn_norm, q_norm=q_norm, kv_norm=kv_norm, sinks=sinks, out_norm_swa=out_norm_swa,
                   out_norm_mla=out_norm_mla, ffn_norm=ffn_norm, final_norm=final_norm)
    split = [_cols_to_chips, _cols_to_chips, _cols_to_chips, _rows_to_chips, _cols_to_chips, _cols_to_chips,
             _rows_to_chips]
    rounds = []

    def reduce_start(name, ks, l0, grads):
        full = [split[k](d) for k, d in zip(ks, grads)]
        got = _pair_exchange([f.astype(BF16) for f in full])
        sums = [_pair_add(f, g, c_idx) for f, g in zip(full, got)]
        send, recv, tok, thru, lands = _scatter_start("scatter_start_" + name, [s16 for _, s16 in sums])
        rounds.append((name, ks, l0, [s32 for s32, _ in sums], send, recv, thru, lands))
        return tok

    starts = {"late": lambda g: reduce_start("late", list(range(7)), 1, g),
              "ffn": lambda g: reduce_start("ffn", [4, 5, 6], 0, g)}
    loss_row, dh, d_first, gs, d_final = _local_step(
        x[0] + (token[0, 0] + token0[0, 0]), loss_target[0], meta_full, layer_weights, small_p,
        lambda which, g: starts[which](g))
    grad_x = dh[BLOCK:][None]
    reduce_start("first", [0, 1, 2, 3], 0, d_first)

    g_big = [lax.empty(w.shape, F32) for w in big]
    for name, ks, l0, s32, send, recv, thru, lands in rounds:
        landed = _scatter_wait("scatter_wait_" + name, send, recv, thru, lands, dh)
        for k, t, m32 in zip(ks, landed, s32):
            g_big[k] = _chip_add(t, m32, chip, c_idx, g_big[k], l0)
    g_big = _pair_join(g_big)

    small = [jnp.stack(gs["attn"]).reshape(-1), jnp.stack(gs["qn"]).reshape(-1), jnp.stack(gs["kvn"]).reshape(-1),
             jnp.stack(gs["sink"]).reshape(-1), jnp.stack(gs["ga"]).reshape(-1), jnp.stack(gs["gb"]).reshape(-1),
             jnp.stack(gs["ffn"]).reshape(-1), d_final.reshape(-1)]
    sizes = [s.shape[0] for s in small]
    flat = jnp.concatenate(small + [dh[FRONT:BLOCK].reshape(-1), loss_row[0, :1]])
    n_flat = flat.shape[0]
    rows_needed = -(-n_flat // (8 * LANES)) * 8
    flat = jnp.pad(flat, (0, rows_needed * LANES - n_flat)).reshape(rows_needed, LANES)
    tot = _allreduce_small(flat).reshape(-1)
    n_small = sum(sizes)
    loss = tot[n_small + N_META * D_MODEL]
    g_meta_full = tot[n_small:n_small + N_META * D_MODEL].reshape(N_META, D_MODEL)
    g_meta_mine = lax.dynamic_slice_in_dim(g_meta_full, chip * (D_MODEL // 4), D_MODEL // 4, axis=1)

    small_w = [attn_norm, q_norm, kv_norm, sinks, out_norm_swa, out_norm_mla, ffn_norm, final_norm]
    small_m = [m_attn_norm, m_q_norm, m_kv_norm, m_sinks, m_out_norm_swa, m_out_norm_mla, m_ffn_norm, m_final_norm]
    small_v = [v_attn_norm, v_q_norm, v_kv_norm, v_sinks, v_out_norm_swa, v_out_norm_mla, v_ffn_norm, v_final_norm]
    n_rows = -(-n_small // (8 * LANES)) * 8

    def pack(arrs):
        f = jnp.concatenate([a.reshape(-1) for a in arrs])
        return jnp.pad(f, (0, n_rows * LANES - n_small), constant_values=1.0).reshape(n_rows, LANES)

    g_small_pack = jnp.pad(tot[:n_small], (0, n_rows * LANES - n_small)).reshape(n_rows, LANES)
    upd_small = _adamw("adam_small", pack(small_w), g_small_pack, pack(small_m), pack(small_v))

    def unpack(p):
        f = p.reshape(-1)
        out, off = [], 0
        for a, n in zip(small_w, sizes):
            out.append(f[off:off + n].reshape(a.shape))
            off += n
        return out

    g_small = unpack(g_small_pack)
    d_small, m_small, v_small = [unpack(p) for p in upd_small]
    d_meta, nm_meta, nv_meta = _adamw("adam_meta", meta_tokens, g_meta_mine, m_meta_tokens, v_meta_tokens)

    big_m = [m_w_in, m_w_q_up, m_w_kv_up, m_w_o, m_w_gate, m_w_up, m_w_down]
    big_v = [v_w_in, v_w_q_up, v_w_kv_up, v_w_o, v_w_gate, v_w_up, v_w_down]
    upd_big = [_adamw("adam_big", w, g, m, v) for w, g, m, v in zip(big, g_big, big_m, big_v)]

    names = ["meta_tokens", "attn_norm", "w_in", "q_norm", "w_q_up", "kv_norm", "w_kv_up", "sinks",
             "out_norm_swa", "out_norm_mla", "w_o", "ffn_norm", "w_gate", "w_up", "w_down", "final_norm"]
    small_idx = {"attn_norm": 0, "q_norm": 1, "kv_norm": 2, "sinks": 3, "out_norm_swa": 4,
                 "out_norm_mla": 5, "ffn_norm": 6, "final_norm": 7}
    big_idx = {"w_in": 0, "w_q_up": 1, "w_kv_up": 2, "w_o": 3, "w_gate": 4, "w_up": 5, "w_down": 6}
    grads, deltas, new_m, new_v = [], [], [], []
    for nme in names:
        if nme == "meta_tokens":
            quad = (g_meta_mine, d_meta, nm_meta, nv_meta)
        elif nme in small_idx:
            i = small_idx[nme]
            quad = (g_small[i], d_small[i], m_small[i], v_small[i])
        else:
            i = big_idx[nme]
            quad = (g_big[i], *upd_big[i])
        grads.append(quad[0]); deltas.append(quad[1]); new_m.append(quad[2]); new_v.append(quad[3])
    return (loss, grad_x, *grads, *deltas, *new_m, *new_v)
```

```python
import jax
import jax.numpy as jnp
from jax import lax
from jax.experimental import pallas as pl
from jax.experimental.pallas import tpu as pltpu

F32 = jnp.float32
BF16 = jnp.bfloat16
MXU_DTYPE = BF16

D_MODEL = 1024
DEPTH = 4
N_META = 16
BLOCK = 128
WINDOW = 128
ROPE_THETA = 10000.0
EPS = 1e-6
NEG = -1e30
SWA_HEADS = 8
SWA_KV_HEADS = 2
SWA_HEAD_DIM = 64
MLA_HEADS = 8
MLA_Q_RANK = 256
MLA_KV_RANK = 128
MLA_NOPE_DIM = 64
MLA_ROPE_DIM = 32
MLA_V_DIM = 64
MLA_QK_DIM = MLA_NOPE_DIM + MLA_ROPE_DIM
D_FF = 2816
FRONT = (-N_META) % BLOCK
LANES = 128
SLOT_W = 8 * LANES
C_QA, C_KA, C_VA, C_QL, C_KL, C_KR, IN_WP = 0, 512, 640, 768, 1024, 1152, 1280
KR_LANE = 64
IN_W = 1184

ADAM_LR, ADAM_B1, ADAM_B2, ADAM_EPS, ADAM_WD, ADAM_STEP = 0.001, 0.9, 0.999, 1e-08, 0.01, 10

VMEM_LIMIT = 48 * 1024 * 1024
MESH = pl.DeviceIdType.MESH


def _tile(n, prefs):
    for t in prefs:
        if n % t == 0:
            return t
    return n


def _params(sem):
    return pltpu.CompilerParams(dimension_semantics=sem, vmem_limit_bytes=VMEM_LIMIT)


_DIMS = {"nn": (((1,), (0,)), ((), ())), "nt": (((1,), (1,)), ((), ())), "tn": (((0,), (0,)), ((), ()))}


def _mm(name, a, b, mode, out_dtype=F32, res=None):
    if mode == "nn":
        (M, K), (_, N) = a.shape, b.shape
    elif mode == "nt":
        (M, K), (N, _) = a.shape, b.shape
    else:
        (K, M), (_, N) = a.shape, b.shape
    lane_tiles = (1408, 1024, 640, 768, 512, 384, 256, 128)
    row_tiles = (1056, 528, 512, 384, 256, 128)
    bm = _tile(M, lane_tiles if mode == "tn" else row_tiles)
    bn = _tile(N, lane_tiles)
    bk = _tile(K, row_tiles if mode == "tn" else lane_tiles)
    nk = K // bk
    if mode == "tn":
        a_spec = pl.BlockSpec((bk, bm), lambda i, j, k: (k, i))
    else:
        a_spec = pl.BlockSpec((bm, bk), lambda i, j, k: (i, k))
    if mode == "nt":
        b_spec = pl.BlockSpec((bn, bk), lambda i, j, k: (j, k))
    else:
        b_spec = pl.BlockSpec((bk, bn), lambda i, j, k: (k, j))
    o_spec = pl.BlockSpec((bm, bn), lambda i, j, k: (i, j))
    in_specs = [a_spec, b_spec]
    args = [a, b]
    if res is not None:
        in_specs.append(o_spec)
        args.append(res)
    dims = _DIMS[mode]

    def kern(a_ref, b_ref, *rest):
        if res is not None:
            r_ref, o_ref = rest[0], rest[1]
            scr = rest[2:]
        else:
            r_ref, o_ref = None, rest[0]
            scr = rest[1:]
        p = lax.dot_general(a_ref[...].astype(MXU_DTYPE), b_ref[...].astype(MXU_DTYPE), dims,
                            preferred_element_type=F32)

        def finish(val):
            if r_ref is not None:
                val = val + r_ref[...]
            o_ref[...] = val.astype(o_ref.dtype)

        if nk == 1:
            finish(p)
        else:
            acc = scr[0]
            k = pl.program_id(2)

            @pl.when(k == 0)
            def _():
                acc[...] = p

            @pl.when(k > 0)
            def _():
                acc[...] += p

            @pl.when(k == nk - 1)
            def _():
                finish(acc[...])

    return pl.pallas_call(
        kern, name=name,
        out_shape=jax.ShapeDtypeStruct((M, N), out_dtype),
        grid=(M // bm, N // bn, nk),
        in_specs=in_specs, out_specs=o_spec,
        scratch_shapes=[pltpu.VMEM((bm, bn), F32)] if nk > 1 else [],
        compiler_params=_params(("parallel", "parallel", "arbitrary")),
    )(*args)


def _rowmap(name, body, rows, vecs, outs, accs=(), tr_prefs=(384, 256, 128)):
    R = rows[0].shape[0]
    tr = _tile(R, tr_prefs)
    n_r, n_v, n_o, n_a = len(rows), len(vecs), len(outs), len(accs)

    def kern(*refs):
        ins = [r[...] for r in refs[:n_r + n_v]]
        o_refs = refs[n_r + n_v:n_r + n_v + n_o]
        a_refs = refs[n_r + n_v + n_o:]
        res = body(*ins)
        for o_ref, val in zip(o_refs, res[:n_o]):
            o_ref[...] = val.astype(o_ref.dtype)
        if n_a:
            first = pl.program_id(0) == 0

            @pl.when(first)
            def _():
                for a_ref, val in zip(a_refs, res[n_o:]):
                    a_ref[...] = val

            @pl.when(jnp.logical_not(first))
            def _():
                for a_ref, val in zip(a_refs, res[n_o:]):
                    a_ref[...] += val

    in_specs = [pl.BlockSpec((tr, r.shape[1]), lambda i: (i, 0)) for r in rows]
    in_specs += [pl.BlockSpec((1, v.shape[1]), lambda i: (0, 0)) for v in vecs]
    out_specs = [pl.BlockSpec((tr, c), lambda i: (i, 0)) for c, _ in outs]
    out_specs += [pl.BlockSpec((1, c), lambda i: (0, 0)) for c in accs]
    out_shape = [jax.ShapeDtypeStruct((R, c), dt) for c, dt in outs]
    out_shape += [jax.ShapeDtypeStruct((1, c), F32) for c in accs]
    return pl.pallas_call(
        kern, name=name, out_shape=out_shape, grid=(R // tr,),
        in_specs=in_specs, out_specs=out_specs,
        compiler_params=_params(("arbitrary",) if n_a else ("parallel",)),
    )(*rows, *vecs)


def _lane(shape):
    return lax.broadcasted_iota(jnp.int32, shape, 1)


def _rot_swa(x):
    lane = _lane(x.shape)
    return jnp.where((lane & 63) < 32, -pltpu.roll(x, 96, 1), pltpu.roll(x, 32, 1))


def _rot_mla(x):
    lane = _lane(x.shape)
    lo = jnp.where(lane >= KR_LANE, -pltpu.roll(x, 112, 1), 0.0)
    hi = jnp.where(lane < KR_LANE + MLA_ROPE_DIM, pltpu.roll(x, 16, 1), 0.0)
    return jnp.where(lane < KR_LANE + 16, lo, hi)


def _rope(x, cos, sin, rot):
    return x * cos + rot(x) * sin


def _rope_t(g, cos, sin, rot):
    return g * cos - rot(g * sin)


def _low(x):
    return jnp.where(_lane(x.shape) < 64, x, 0.0)


def _value_slot(x):
    lane = _lane(x.shape)
    return jnp.where(lane < 64, x, jnp.where(lane == 64, 1.0, 0.0))


def _blk(x, j):
    return x[:, j * LANES:(j + 1) * LANES]


def _rms_r(x, width):
    return lax.rsqrt(jnp.sum(x * x, axis=-1, keepdims=True) * (1.0 / width) + EPS)


def _rms_bwd(x, g, dy, width):
    r = _rms_r(x, width)
    gdy = dy * g
    dot = jnp.sum(gdy * x, axis=-1, keepdims=True)
    dx = r * gdy - x * (r * r * r * (1.0 / width) * dot)
    return dx, dy * x * r


def _colsum(x):
    return jnp.sum(x, axis=0, keepdims=True)


def _rmsnorm_bwd(name, x, g, dy, dres):
    def body(xv, dyv, dr, gv):
        dx, dg = _rms_bwd(xv, gv, dyv, D_MODEL)
        return dx + dr, _colsum(dg)
    return _rowmap(name, body, [x, dy, dres], [g], [(D_MODEL, F32)], [D_MODEL])


def _prep1(proj, qn_g, kv_g, cosa, sina, cosm, sinm):
    def body(p, ca, sa, cm, sm, gq, gk):
        qa = []
        for j in range(4):
            xr = _rope(_blk(p, j), ca, sa, _rot_swa)
            qa += [_low(xr), _low(pltpu.roll(xr, 64, 1))]
        kr_ = _rope(_blk(p, C_KA // LANES), ca, sa, _rot_swa)
        ka = [_low(kr_), _low(pltpu.roll(kr_, 64, 1))]
        vv = _blk(p, C_VA // LANES)
        va = [_value_slot(vv), _value_slot(pltpu.roll(vv, 64, 1))]
        ql = p[:, C_QL:C_QL + MLA_Q_RANK]
        qn = ql * _rms_r(ql, MLA_Q_RANK) * gq
        kl = p[:, C_KL:C_KL + MLA_KV_RANK]
        cn = kl * _rms_r(kl, MLA_KV_RANK) * gk
        kr = _rope(_blk(p, C_KR // LANES), cm, sm, _rot_mla)
        return (jnp.concatenate(qa, 1), jnp.concatenate(ka, 1), jnp.concatenate(va, 1), qn, cn, kr)
    return _rowmap("prep1", body, [proj, cosa, sina, cosm, sinm], [qn_g, kv_g],
                   [(SLOT_W, BF16), (2 * LANES, BF16), (2 * LANES, BF16),
                    (MLA_Q_RANK, BF16), (MLA_KV_RANK, BF16), (LANES, F32)])


def _prep2(qb, kvb, kr, cosm, sinm):
    def body(q, kv, krv, cm, sm):
        qs, ks, vs = [], [], []
        for h in range(MLA_HEADS):
            qs.append(_rope(_blk(q, h), cm, sm, _rot_mla))
            kvh = _blk(kv, h)
            ks.append(_low(kvh) + krv)
            vs.append(_value_slot(pltpu.roll(kvh, 64, 1)))
        return jnp.concatenate(qs, 1), jnp.concatenate(ks, 1), jnp.concatenate(vs, 1)
    return _rowmap("prep2", body, [qb, kvb, kr, cosm, sinm], [],
                   [(SLOT_W, BF16), (SLOT_W, BF16), (SLOT_W, BF16)])


def _compact(slots):
    return jnp.concatenate(
        [_blk(slots, 2 * j) + pltpu.roll(_blk(slots, 2 * j + 1), 64, 1) for j in range(4)], 1)


def _expand(nat):
    out = []
    for j in range(4):
        b = _blk(nat, j)
        out += [_low(b), _low(pltpu.roll(b, 64, 1))]
    return jnp.concatenate(out, 1)


def _merge_fwd(oa, ob, ga, gb):
    def body(a, b, gav, gbv):
        xa, xb = _compact(a), _compact(b)
        return (jnp.concatenate([xa * _rms_r(xa, 512) * gav, xb * _rms_r(xb, 512) * gbv], 1),)
    return _rowmap("merge_fwd", body, [oa, ob], [ga, gb], [(D_MODEL, BF16)])[0]


def _merge_bwd(dmix, oa, ob, lse_a, ga, gb, sink_slots):
    def body(dm, a, b, lse, gav, gbv, sk):
        outs = []
        accs = []
        for o, g, lo in ((a, gav, 0), (b, gbv, 512)):
            x = _compact(o)
            dx, dg = _rms_bwd(x, g, dm[:, lo:lo + 512], 512)
            do = _expand(dx)
            delta = jnp.concatenate(
                [jnp.broadcast_to(jnp.sum(_blk(do, h) * _blk(o, h), axis=-1, keepdims=True),
                                  (do.shape[0], LANES)) for h in range(8)], 1)
            outs += [do, delta]
            accs.append(_colsum(dg))
        dsink = _colsum(-jnp.exp(sk - lse) * outs[1])
        return (*outs, *accs, dsink)
    return _rowmap("merge_bwd", body, [dmix, oa, ob, lse_a], [ga, gb, sink_slots],
                   [(SLOT_W, BF16), (SLOT_W, F32), (SLOT_W, BF16), (SLOT_W, F32)],
                   [512, 512, SLOT_W])


def _prep2_bwd(dq, dk, dv, cosm, sinm):
    def body(dqv, dkv, dvv, cm, sm):
        dqv, dkv, dvv = dqv.astype(F32), dkv.astype(F32), dvv.astype(F32)
        dqb, dkvb = [], []
        krsum = jnp.zeros((dqv.shape[0], LANES), F32)
        for h in range(MLA_HEADS):
            dqb.append(_rope_t(_blk(dqv, h), cm, sm, _rot_mla))
            dkh = _blk(dkv, h)
            dkvb.append(_low(dkh) + pltpu.roll(_blk(dvv, h), 64, 1))
            krsum = krsum + dkh
        lane = _lane(krsum.shape)
        dkr = jnp.where((lane >= KR_LANE) & (lane < KR_LANE + MLA_ROPE_DIM),
                        _rope_t(krsum, cm, sm, _rot_mla), 0.0)
        return jnp.concatenate(dqb, 1), jnp.concatenate(dkvb, 1), dkr
    return _rowmap("prep2_bwd", body, [dq, dk, dv, cosm, sinm], [],
                   [(SLOT_W, BF16), (SLOT_W, BF16), (LANES, F32)])


def _prep1_bwd(proj, dqa, dka, dva, dqn, dcn, dkr, cosa, sina, qn_g, kv_g):
    def body(p, dq, dk, dv, dqnv, dcnv, dkrv, ca, sa, gq, gk):
        dq, dk, dv = dq.astype(F32), dk.astype(F32), dv.astype(F32)
        cols = []
        for j in range(4):
            nat = _blk(dq, 2 * j) + pltpu.roll(_blk(dq, 2 * j + 1), 64, 1)
            cols.append(_rope_t(nat, ca, sa, _rot_swa))
        grp = lambda d, g: sum(_blk(d, 4 * g + i) for i in range(4))
        cols.append(_rope_t(grp(dk, 0) + pltpu.roll(grp(dk, 1), 64, 1), ca, sa, _rot_swa))
        cols.append(grp(dv, 0) + pltpu.roll(grp(dv, 1), 64, 1))
        dql, dgq = _rms_bwd(p[:, C_QL:C_QL + MLA_Q_RANK], gq, dqnv, MLA_Q_RANK)
        dkl, dgk = _rms_bwd(p[:, C_KL:C_KL + MLA_KV_RANK], gk, dcnv, MLA_KV_RANK)
        cols += [dql, dkl, dkrv]
        return jnp.concatenate(cols, 1), _colsum(dgq), _colsum(dgk)
    return _rowmap("prep1_bwd", body, [proj, dqa, dka, dva, dqn, dcn, dkr, cosa, sina], [qn_g, kv_g],
                   [(IN_WP, BF16)], [MLA_Q_RANK, MLA_KV_RANK], tr_prefs=(192, 128))


def _sigmoid(x):
    return 1.0 / (1.0 + jnp.exp(-x))


def _ffn_tiles(T, F):
    return _tile(T, (528, 512, 384, 256, 128)), _tile(F, (1408, 1024, 640, 512, 256, 128))


def _norm_proj(name, h, g, weights, swiglu):
    (T, D), F = h.shape, weights[0].shape[1]
    bm, bn = _ffn_tiles(T, F)
    nw = len(weights)

    def kern(h_ref, g_ref, *rest):
        w_refs, u_ref, o_refs, u_scr = rest[:nw], rest[nw], rest[nw + 1:-1], rest[-1]

        @pl.when(pl.program_id(1) == 0)
        def _():
            x = h_ref[...]
            u = (x * _rms_r(x, D) * g_ref[...]).astype(u_scr.dtype)
            u_scr[...] = u
            u_ref[...] = u

        uv = u_scr[...]
        prods = [jnp.dot(uv, w[...], preferred_element_type=F32) for w in w_refs]
        for o_ref, p in zip(o_refs, prods):
            o_ref[...] = p
        if swiglu:
            a, b = prods
            o_refs[nw][...] = (a * _sigmoid(a) * b).astype(o_refs[nw].dtype)

    w_spec = pl.BlockSpec((D, bn), lambda i, j: (0, j))
    row_spec = pl.BlockSpec((bm, D), lambda i, j: (i, 0))
    o_spec = pl.BlockSpec((bm, bn), lambda i, j: (i, j))
    n_out = nw + (1 if swiglu else 0)
    return pl.pallas_call(
        kern, name=name,
        out_shape=[jax.ShapeDtypeStruct((T, D), MXU_DTYPE)] + [jax.ShapeDtypeStruct((T, F), F32)] * nw
        + ([jax.ShapeDtypeStruct((T, F), MXU_DTYPE)] if swiglu else []),
        grid=(T // bm, F // bn),
        in_specs=[row_spec, pl.BlockSpec((1, D), lambda i, j: (0, 0))] + [w_spec] * nw,
        out_specs=[row_spec] + [o_spec] * n_out,
        scratch_shapes=[pltpu.VMEM((bm, D), MXU_DTYPE)],
        compiler_params=_params(("parallel", "arbitrary")),
    )(h, g, *weights)


def _ffn_mid_bwd(dh, w_down, a, b):
    (T, D), F = dh.shape, w_down.shape[0]
    bm, bn = _ffn_tiles(T, F)

    def kern(dh_ref, wd_ref, a_ref, b_ref, da_ref, db_ref):
        d = lax.dot_general(dh_ref[...].astype(MXU_DTYPE), wd_ref[...], _DIMS["nt"], preferred_element_type=F32)
        av, bv = a_ref[...], b_ref[...]
        s = _sigmoid(av)
        da_ref[...] = (d * bv * (s * (1.0 + av * (1.0 - s)))).astype(da_ref.dtype)
        db_ref[...] = (d * (av * s)).astype(db_ref.dtype)

    o_spec = pl.BlockSpec((bm, bn), lambda i, j: (i, j))
    return pl.pallas_call(
        kern, name="ffn_mid_bwd",
        out_shape=[jax.ShapeDtypeStruct((T, F), MXU_DTYPE)] * 2,
        grid=(T // bm, F // bn),
        in_specs=[pl.BlockSpec((bm, D), lambda i, j: (i, 0)), pl.BlockSpec((bn, D), lambda i, j: (j, 0)),
                  o_spec, o_spec],
        out_specs=[o_spec, o_spec],
        compiler_params=_params(("parallel", "parallel")),
    )(dh, w_down, a, b)


def _loss_head(h, target, g):
    T = h.shape[0]
    nb = T // BLOCK

    def kern(h_ref, t_ref, g_ref, dh_ref, dg_ref, loss_ref, acc):
        i = pl.program_id(0)

        @pl.when(i == 0)
        def _():
            dh_ref[...] = jnp.zeros_like(dh_ref)
            dg_ref[...] = jnp.zeros_like(dg_ref)
            acc[...] = jnp.zeros_like(acc)

        @pl.when(i > 0)
        def _():
            x = h_ref[...]
            gv = g_ref[...]
            e = x * _rms_r(x, D_MODEL) * gv - t_ref[...]
            acc[...] += _colsum(e * e)
            dx, dg = _rms_bwd(x, gv, e * (1.0 / D_MODEL), D_MODEL)
            dh_ref[...] = dx
            dg_ref[...] += _colsum(dg)

        @pl.when(i == nb - 1)
        def _():
            tot = jnp.sum(acc[...], axis=-1, keepdims=True) * (0.5 / D_MODEL)
            loss_ref[...] = jnp.broadcast_to(tot, loss_ref.shape)

    return pl.pallas_call(
        kern, name="loss_head",
        out_shape=[jax.ShapeDtypeStruct((T, D_MODEL), F32), jax.ShapeDtypeStruct((1, D_MODEL), F32),
                   jax.ShapeDtypeStruct((1, LANES), F32)],
        grid=(nb,),
        in_specs=[pl.BlockSpec((BLOCK, D_MODEL), lambda i: (i, 0)),
                  pl.BlockSpec((BLOCK, D_MODEL), lambda i: (jnp.maximum(i - 1, 0), 0)),
                  pl.BlockSpec((1, D_MODEL), lambda i: (0, 0))],
        out_specs=[pl.BlockSpec((BLOCK, D_MODEL), lambda i: (i, 0)),
                   pl.BlockSpec((1, D_MODEL), lambda i: (0, 0)),
                   pl.BlockSpec((1, LANES), lambda i: (0, 0))],
        scratch_shapes=[pltpu.VMEM((1, D_MODEL), F32)],
        compiler_params=_params(("arbitrary",)),
    )(h, target, g)


LOG2E = 1.4426950408889634


def _attn_plan(T, causal):
    tq = _tile(T, (384, 256, 128))
    ck = min(2 * tq, T) if causal else min(tq + WINDOW, T)
    return tq, ck, (-(-T // ck) if causal else 1)


def _chunk(i, c, T, tq, ck, causal):
    if causal:
        return pl.multiple_of(jnp.minimum(c * ck, T - ck), LANES), c * ck
    return pl.multiple_of(jnp.clip(i * tq - WINDOW, 0, T - ck), LANES), 0


def _n_chunks(i, tq, ck, causal):
    return ((i + 1) * tq + ck - 1) // ck if causal else 1


def _mask(s, i, start, first, tq, ck, causal):
    qpos = i * tq + lax.broadcasted_iota(jnp.int32, (tq, 1), 0)
    kpos = start + lax.broadcasted_iota(jnp.int32, (tq, ck), 1)
    low = jnp.maximum(jnp.where(qpos < FRONT, 0, FRONT), first)
    if not causal:
        low = jnp.maximum(low, qpos - (WINDOW - 1))
    return jnp.where(kpos >= low, jnp.where(kpos <= qpos, s, NEG), NEG)


def _chunk_loop(n, body, init, several):
    carry = body(0, init, True)
    if not several:
        return carry
    carry = body(n - 1, carry, True)
    return lax.fori_loop(1, n - 1, lambda c, cr: body(c, cr, False), carry)


def _tile_sweep(nq, tq, ck, causal, q_tiles, pair=False):
    if pair and nq >= 2 and (not causal or ck == 2 * tq):
        if causal:
            q_tiles([0, 1], False)
            lax.fori_loop(1, nq // 2, lambda p, cr: q_tiles([2 * p, 2 * p + 1], True) or cr, 0)
        else:
            lax.fori_loop(0, nq // 2, lambda p, cr: q_tiles([2 * p, 2 * p + 1], False) or cr, 0)
        if nq % 2:
            q_tiles([nq - 1], causal and nq >= 3)
        return
    one = min(nq, ck // tq) if causal else nq
    lax.fori_loop(0, one, lambda i, cr: q_tiles([i], False) or cr, 0)
    if one < nq:
        lax.fori_loop(one, nq, lambda i, cr: q_tiles([i], True) or cr, 0)


BWD_HEADS_PER_STEP = 1


def _head_cols(group, hp):
    q_cols = lambda hh: slice(hh * LANES, (hh + 1) * LANES)
    if group == 1:
        return q_cols, q_cols, hp * LANES
    assert group % hp == 0
    return q_cols, (lambda hh: slice(0, LANES)), LANES


def _whole(T, width, index, single):
    if single:
        return pl.BlockSpec((T, width), index, pipeline_mode=pl.Buffered(1))
    return pl.BlockSpec((T, width), index)


def _attn_fwd(name, q, k, v, sinks, group, causal, scale, hp):
    T = q.shape[0]
    H = q.shape[1] // LANES
    HP = hp
    tq, ck, slots = _attn_plan(T, causal)
    nq = T // tq
    c2 = scale * LOG2E
    q_cols, k_cols, kw = _head_cols(group, HP)

    def kern(sink_ref, q_ref, k_ref, v_ref, o_ref, lse_ref, s_scr):
        sink2 = [sink_ref[pl.program_id(0) * HP + hh] * LOG2E for hh in range(HP)]

        def q_tiles(tiles, several):
            rows = [pl.ds(pl.multiple_of(i * tq, tq), tq) for i in tiles]
            chains = [(t, hh) for t in range(len(tiles)) for hh in range(HP)]
            qq = [q_ref[rows[t], q_cols(hh)] for t, hh in chains]
            n = _n_chunks(tiles[0], tq, ck, causal)

            def score(c, m2, masked):
                out = []
                for n_ch, (t, hh) in enumerate(chains):
                    start, first = _chunk(tiles[t], c, T, tq, ck, causal)
                    s = lax.dot_general(qq[n_ch], k_ref[pl.ds(start, ck), k_cols(hh)], _DIMS["nt"],
                                        preferred_element_type=F32) * c2
                    if masked:
                        s = _mask(s, tiles[t], start, first, tq, ck, causal)
                    s_scr[n_ch, c] = s
                    out.append(jnp.maximum(m2[n_ch], jnp.max(s, axis=-1, keepdims=True)))
                return tuple(out)

            m2 = _chunk_loop(n, score, tuple(jnp.full((tq, 1), sink2[hh], F32) for _, hh in chains), several)

            def weigh(c, acc):
                out = []
                for n_ch, (t, hh) in enumerate(chains):
                    start, _ = _chunk(tiles[t], c, T, tq, ck, causal)
                    p = jnp.exp2(s_scr[n_ch, c] - m2[n_ch])
                    out.append(acc[n_ch] + jnp.dot(p.astype(MXU_DTYPE), v_ref[pl.ds(start, ck), k_cols(hh)],
                                                   preferred_element_type=F32))
                return tuple(out)

            acc = lax.fori_loop(0, n, weigh, tuple(jnp.zeros((tq, LANES), F32) for _ in chains))
            lane = _lane((tq, LANES))
            for n_ch, (t, hh) in enumerate(chains):
                l = acc[n_ch][:, 64:65] + jnp.exp2(sink2[hh] - m2[n_ch])
                o_ref[rows[t], q_cols(hh)] = jnp.where(lane < 64, acc[n_ch] / l, 0.0)
                lse_ref[rows[t], q_cols(hh)] = jnp.broadcast_to(m2[n_ch] * (1.0 / LOG2E) + jnp.log(l), (tq, LANES))

        _tile_sweep(nq, tq, ck, causal, q_tiles, pair=True)

    q_spec = _whole(T, HP * LANES, lambda g: (0, g), HP > 1)
    kv_spec = _whole(T, kw, (lambda g: (0, g)) if group == 1 else (lambda g: (0, (g * HP) // group)), HP > 1)
    return pl.pallas_call(
        kern, name=name,
        out_shape=[jax.ShapeDtypeStruct((T, H * LANES), F32)] * 2,
        grid=(H // HP,),
        in_specs=[pl.BlockSpec(memory_space=pltpu.SMEM), q_spec, kv_spec, kv_spec],
        out_specs=[q_spec, q_spec],
        scratch_shapes=[pltpu.VMEM((2 * HP, slots, tq, ck), F32)],
        compiler_params=_params(("parallel",)),
    )(sinks, q, k, v)


def _attn_bwd(name, q, k, v, do, lse, delta, group, causal, scale):
    T = q.shape[0]
    H = q.shape[1] // LANES
    HP = BWD_HEADS_PER_STEP
    tq, ck, _ = _attn_plan(T, causal)
    nq = T // tq
    c2 = scale * LOG2E
    q_cols, k_cols, kw = _head_cols(group, HP)

    def kern(q_ref, k_ref, v_ref, do_ref, lse_ref, dl_ref, dq_ref, dk_ref, dv_ref, dk_acc, dv_acc):
        dk_acc[...] = jnp.zeros_like(dk_acc)
        dv_acc[...] = jnp.zeros_like(dv_acc)

        def q_tiles(tiles, several):
            rows = [pl.ds(pl.multiple_of(i * tq, tq), tq) for i in tiles]
            chains = [(t, hh) for t in range(len(tiles)) for hh in range(HP)]
            qq = [q_ref[rows[t], q_cols(hh)] for t, hh in chains]
            dd = [do_ref[rows[t], q_cols(hh)] for t, hh in chains]
            lse2 = [lse_ref[rows[t], q_cols(hh)][:, 0:1] * LOG2E for t, hh in chains]
            dl_c = [dl_ref[rows[t], q_cols(hh)][:, 0:1] for t, hh in chains]

            def chunk(c, dq, masked):
                out = []
                for n_ch, (t, hh) in enumerate(chains):
                    start, first = _chunk(tiles[t], c, T, tq, ck, causal)
                    keys = pl.ds(start, ck)
                    kk, vv = k_ref[keys, k_cols(hh)], v_ref[keys, k_cols(hh)]
                    s = lax.dot_general(qq[n_ch], kk, _DIMS["nt"], preferred_element_type=F32)
                    if masked:
                        s = _mask(s, tiles[t], start, first, tq, ck, causal)
                    p = jnp.exp2(s * c2 - lse2[n_ch])
                    dv_acc[keys, q_cols(hh)] += lax.dot_general(p.astype(MXU_DTYPE), dd[n_ch], _DIMS["tn"],
                                                                preferred_element_type=F32)
                    dp = lax.dot_general(dd[n_ch], vv, _DIMS["nt"], preferred_element_type=F32)
                    ds = (p * (dp - dl_c[n_ch])).astype(MXU_DTYPE)
                    dk_acc[keys, q_cols(hh)] += lax.dot_general(ds, qq[n_ch], _DIMS["tn"],
                                                                preferred_element_type=F32) * scale
                    out.append(dq[n_ch] + jnp.dot(ds, kk, preferred_element_type=F32))
                return tuple(out)

            dq = _chunk_loop(_n_chunks(tiles[0], tq, ck, causal), chunk,
                             tuple(jnp.zeros((tq, LANES), F32) for _ in chains), several)
            for n_ch, (t, hh) in enumerate(chains):
                dq_ref[rows[t], q_cols(hh)] = (dq[n_ch] * scale).astype(dq_ref.dtype)

        _tile_sweep(nq, tq, ck, causal, q_tiles, pair=True)
        dk_ref[...] = dk_acc[...].astype(dk_ref.dtype)
        dv_ref[...] = dv_acc[...].astype(dv_ref.dtype)

    q_spec = _whole(T, HP * LANES, lambda g: (0, g), False)
    kv_spec = _whole(T, kw, (lambda g: (0, g)) if group == 1 else (lambda g: (0, (g * HP) // group)), False)
    return pl.pallas_call(
        kern, name=name,
        out_shape=[jax.ShapeDtypeStruct((T, H * LANES), MXU_DTYPE)] * 3,
        grid=(H // HP,),
        in_specs=[q_spec, kv_spec, kv_spec, q_spec, q_spec, q_spec],
        out_specs=[q_spec, q_spec, q_spec],
        scratch_shapes=[pltpu.VMEM((T, HP * LANES), F32)] * 2,
        compiler_params=_params(("parallel",)),
    )(q, k, v, do, lse, delta)


def _ew(name, fn, ins, out_dtypes):
    shape = ins[0].shape
    flat = [a.reshape(-1, shape[-1]) for a in ins]
    R, C = flat[0].shape
    tr = _tile(R, (512, 256, 128, 64, 32, 16, 8))
    n_in = len(ins)

    def kern(*refs):
        res = fn(*[r[...] for r in refs[:n_in]])
        for o_ref, val in zip(refs[n_in:], res):
            o_ref[...] = val.astype(o_ref.dtype)

    spec = pl.BlockSpec((tr, C), lambda i: (i, 0))
    outs = pl.pallas_call(
        kern, name=name,
        out_shape=[jax.ShapeDtypeStruct((R, C), dt) for dt in out_dtypes],
        grid=(R // tr,), in_specs=[spec] * n_in, out_specs=[spec] * len(out_dtypes),
        compiler_params=_params(("parallel",)),
    )(*flat)
    return [o.reshape(shape) for o in outs]


def _adamw(name, w, g, m, v):
    c1 = 1.0 - ADAM_B1 ** ADAM_STEP
    c2 = 1.0 - ADAM_B2 ** ADAM_STEP

    def fn(wv, gv, mv, vv):
        mn = ADAM_B1 * mv + (1.0 - ADAM_B1) * gv
        vn = ADAM_B2 * vv + (1.0 - ADAM_B2) * (gv * gv)
        delta = -ADAM_LR * ((mn / c1) / (jnp.sqrt(vn / c2) + ADAM_EPS) + ADAM_WD * wv)
        return delta, mn, vn

    return _ew(name, fn, [w, g, m, v], [F32, F32, F32])


_ANY = pl.BlockSpec(memory_space=pl.ANY)


def _where_am_i():
    x, y, c = lax.axis_index("x"), lax.axis_index("y"), lax.axis_index("c")
    chips = [(1 - x, y), (x, 1 - y), (1 - x, 1 - y)]
    return x, y, c, chips


def _gather_weights(shards, meta):
    arrs = list(shards) + [meta]
    n = len(arrs)
    per = [a.shape[0] // 2 for a in arrs]

    def body(*refs):
        ins, outs = refs[:n], refs[n:2 * n]
        send1, recv1, send2, recv2 = refs[2 * n:]
        x, y, c, chips = _where_am_i()
        me = 2 * x + y

        def half(ref, k, cc):
            return ref.at[pl.ds(per[k] * cc, per[k])]

        first = []
        for k in range(n):
            for j, (cx, cy) in enumerate(chips):
                first.append(pltpu.make_async_remote_copy(
                    src_ref=half(ins[k], k, c), dst_ref=half(outs[k].at[me], k, c),
                    send_sem=send1.at[k, j], recv_sem=recv1.at[k, j],
                    device_id=(cx, cy, c), device_id_type=MESH))
        for cp in first:
            cp.start()
        passed = []
        for k in range(n):
            for j, (cx, cy) in enumerate(chips):
                landed = half(outs[k].at[2 * cx + cy], k, c)
                pltpu.make_async_remote_copy(
                    src_ref=landed, dst_ref=landed, send_sem=send1.at[k, j], recv_sem=recv1.at[k, j],
                    device_id=(cx, cy, c), device_id_type=MESH).wait_recv()
                fwd = pltpu.make_async_remote_copy(
                    src_ref=landed, dst_ref=landed, send_sem=send2.at[k, j], recv_sem=recv2.at[k, j],
                    device_id=(x, y, 1 - c), device_id_type=MESH)
                fwd.start()
                passed.append(fwd)
        for k in range(n):
            for j, (cx, cy) in enumerate(chips):
                other = half(outs[k].at[2 * cx + cy], k, 1 - c)
                pltpu.make_async_remote_copy(
                    src_ref=other, dst_ref=other, send_sem=send2.at[k, j], recv_sem=recv2.at[k, j],
                    device_id=(x, y, 1 - c), device_id_type=MESH).wait_recv()
        for cp in first + passed:
            cp.wait_send()

    return pl.pallas_call(
        body, name="gather_weights",
        out_shape=[jax.ShapeDtypeStruct((4,) + a.shape, a.dtype) for a in arrs],
        in_specs=[_ANY] * n, out_specs=[_ANY] * n,
        scratch_shapes=[pltpu.SemaphoreType.DMA((n, 3))] * 4,
    )(*arrs)


_HBM = pl.BlockSpec(memory_space=pltpu.HBM)
_SEM = pl.BlockSpec(memory_space=pltpu.SEMAPHORE)
_FLOWS = pltpu.SideEffectType.DATAFLOW_SIDE_EFFECTING


def _in_hbm(a):
    return pltpu.with_memory_space_constraint(a, pltpu.HBM)


def _gather_start(name, shards):
    n = len(shards)
    nl = shards[0].shape[0]
    lands = [lax.empty((4,) + sh.shape[1:], sh.dtype) for _ in range(nl) for sh in shards]

    def body(*refs):
        srcs, land = refs[:n], refs[n:n + n * nl]
        send, recv, token = refs[n + n * nl:n + n * nl + 3]
        x, y, c, chips = _where_am_i()
        me = 2 * x + y
        for l in range(nl):
            for k in range(n):
                for cx, cy in chips:
                    pltpu.make_async_remote_copy(
                        src_ref=srcs[k].at[l], dst_ref=land[l * n + k].at[me],
                        send_sem=send.at[l], recv_sem=recv.at[l],
                        device_id=(cx, cy, c), device_id_type=MESH).start()
        token[...] = jnp.zeros_like(token)

    ins = [_in_hbm(a) for a in list(shards) + lands]
    res = pl.pallas_call(
        body, name=name,
        out_shape=(pltpu.SemaphoreType.DMA((nl,)), pltpu.SemaphoreType.DMA((nl,)),
                   jax.ShapeDtypeStruct((8, LANES), F32)) + tuple(pltpu.HBM(a.shape, a.dtype) for a in ins),
        in_specs=[_HBM] * len(ins),
        out_specs=(_SEM, _SEM, pl.BlockSpec(memory_space=pltpu.VMEM)) + (_HBM,) * len(ins),
        input_output_aliases={i: 3 + i for i in range(len(ins))},
        compiler_params=pltpu.CompilerParams(has_side_effects=_FLOWS),
    )(*ins)
    send, recv, token = res[:3]
    thru = res[3:3 + n]
    lands = res[3 + n:]
    return send, recv, token, list(thru), [list(lands[l * n:(l + 1) * n]) for l in range(nl)]


def _gather_wait(name, send, recv, l, thru, land, after):
    n = len(thru)

    def body(*refs):
        srcs, lands = refs[:n], refs[n:2 * n]
        send_sem, recv_sem = refs[2 * n:2 * n + 2]
        x, y, c, chips = _where_am_i()
        for k in range(n):
            for cx, cy in chips:
                copy = pltpu.make_async_remote_copy(
                    src_ref=srcs[k].at[l], dst_ref=lands[k].at[2 * cx + cy],
                    send_sem=send_sem.at[l], recv_sem=recv_sem.at[l],
                    device_id=(cx, cy, c), device_id_type=MESH)
                copy.wait_send()
                copy.wait_recv()

    ins = list(thru) + list(land)
    res = pl.pallas_call(
        body, name=name,
        out_shape=tuple(pltpu.HBM(a.shape, a.dtype) for a in ins),
        in_specs=[_HBM] * len(ins) + [_SEM, _SEM, _ANY],
        out_specs=(_HBM,) * len(ins),
        input_output_aliases={i: i for i in range(len(ins))},
        compiler_params=pltpu.CompilerParams(has_side_effects=_FLOWS),
    )(*ins, send, recv, after)
    return list(res[:n]), list(res[n:])


def _pair_exchange(grads):
    n = len(grads)

    def body(*refs):
        ins, got = refs[:n], refs[n:2 * n]
        send, recv = refs[2 * n:]
        x, y, c, _ = _where_am_i()
        sib = (x, y, 1 - c)
        for k in range(n):
            h = ins[k].shape[2] // 2
            for ch in range(4):
                for l in range(ins[k].shape[1]):
                    pltpu.make_async_remote_copy(
                        src_ref=ins[k].at[ch, l, pl.ds((1 - c) * h, h)], dst_ref=got[k].at[ch, l],
                        send_sem=send.at[k], recv_sem=recv.at[k], device_id=sib, device_id_type=MESH).start()
        for k in range(n):
            pltpu.make_async_remote_copy(
                src_ref=got[k], dst_ref=got[k], send_sem=send.at[k], recv_sem=recv.at[k],
                device_id=sib, device_id_type=MESH).wait()

    return pl.pallas_call(
        body, name="reduce_pair",
        out_shape=[jax.ShapeDtypeStruct(g.shape[:2] + (g.shape[2] // 2, g.shape[3]), g.dtype) for g in grads],
        in_specs=[_ANY] * n, out_specs=[_ANY] * n,
        scratch_shapes=[pltpu.SemaphoreType.DMA((n,))] * 2,
    )(*grads)


def _half_tiling(r):
    h = r // 2
    tr = _tile(h, (512, 256, 352, 176, 128, 64, 32))
    return h, tr, h // tr


def _pair_add(full, got, c):
    _, ls, r, cols = full.shape
    h, tr, nh = _half_tiling(r)

    def kern(c_ref, a_ref, b_ref, o32_ref, o16_ref):
        tot = a_ref[...] + b_ref[...].astype(F32)
        o32_ref[...] = tot
        o16_ref[...] = tot.astype(o16_ref.dtype)

    blk = (None, None, tr, cols)
    mine = pl.BlockSpec(blk, lambda ch, l, i, cr: (ch, l, cr[0] * nh + i, 0))
    same = pl.BlockSpec(blk, lambda ch, l, i, cr: (ch, l, i, 0))
    return pl.pallas_call(
        kern, name="pair_add",
        out_shape=[jax.ShapeDtypeStruct(got.shape, F32), jax.ShapeDtypeStruct(got.shape, got.dtype)],
        grid_spec=pltpu.PrefetchScalarGridSpec(
            num_scalar_prefetch=1, grid=(4, ls, nh), in_specs=[mine, same], out_specs=[same, same]),
        compiler_params=_params(("parallel", "parallel", "parallel")),
    )(c.reshape(1), full, got)


def _scatter_start(name, parts):
    n = len(parts)
    lands = [lax.empty(p.shape, p.dtype) for p in parts]

    def body(*refs):
        srcs, land = refs[:n], refs[n:2 * n]
        send, recv, token = refs[2 * n:2 * n + 3]
        x, y, c, chips = _where_am_i()
        me = 2 * x + y
        for k in range(n):
            for cx, cy in chips:
                pltpu.make_async_remote_copy(
                    src_ref=srcs[k].at[2 * cx + cy], dst_ref=land[k].at[me], send_sem=send, recv_sem=recv,
                    device_id=(cx, cy, c), device_id_type=MESH).start()
        token[...] = jnp.zeros_like(token)

    ins = [_in_hbm(a) for a in list(parts) + lands]
    res = pl.pallas_call(
        body, name=name,
        out_shape=(pltpu.SemaphoreType.DMA(()), pltpu.SemaphoreType.DMA(()),
                   jax.ShapeDtypeStruct((8, LANES), F32)) + tuple(pltpu.HBM(a.shape, a.dtype) for a in ins),
        in_specs=[_HBM] * len(ins),
        out_specs=(_SEM, _SEM, pl.BlockSpec(memory_space=pltpu.VMEM)) + (_HBM,) * len(ins),
        input_output_aliases={i: 3 + i for i in range(len(ins))},
        compiler_params=pltpu.CompilerParams(has_side_effects=_FLOWS),
    )(*ins)
    return res[0], res[1], res[2], list(res[3:3 + n]), list(res[3 + n:])


def _scatter_wait(name, send, recv, thru, lands, after):
    n = len(thru)

    def body(*refs):
        srcs, land = refs[:n], refs[n:2 * n]
        send_sem, recv_sem = refs[2 * n:2 * n + 2]
        x, y, c, chips = _where_am_i()
        for k in range(n):
            for cx, cy in chips:
                copy = pltpu.make_async_remote_copy(
                    src_ref=srcs[k].at[2 * cx + cy], dst_ref=land[k].at[2 * cx + cy],
                    send_sem=send_sem, recv_sem=recv_sem, device_id=(cx, cy, c), device_id_type=MESH)
                copy.wait_send()
                copy.wait_recv()

    ins = list(thru) + list(lands)
    res = pl.pallas_call(
        body, name=name,
        out_shape=tuple(pltpu.HBM(a.shape, a.dtype) for a in ins),
        in_specs=[_HBM] * len(ins) + [_SEM, _SEM, _ANY],
        out_specs=(_HBM,) * len(ins),
        input_output_aliases={i: i for i in range(len(ins))},
        compiler_params=pltpu.CompilerParams(has_side_effects=_FLOWS),
    )(*ins, send, recv, after)
    return list(res[n:])


def _chip_add(landed, mine, me, c, into, l0):
    _, ls, h, cols = landed.shape
    _, tr, nh = _half_tiling(2 * h)

    def kern(me_ref, c_ref, land_ref, own_ref, into_ref, o_ref):
        own = own_ref[...]
        tot = None
        for j in range(4):
            term = jnp.where(me_ref[0] == j, own, land_ref[j].astype(F32))
            tot = term if tot is None else tot + term
        o_ref[...] = tot

    return pl.pallas_call(
        kern, name="chip_add",
        out_shape=jax.ShapeDtypeStruct(into.shape, F32),
        grid_spec=pltpu.PrefetchScalarGridSpec(
            num_scalar_prefetch=2, grid=(ls, nh),
            in_specs=[pl.BlockSpec((4, None, tr, cols), lambda l, i, mr, cr: (0, l, i, 0)),
                      pl.BlockSpec((None, None, tr, cols), lambda l, i, mr, cr: (mr[0], l, i, 0)),
                      pl.BlockSpec(memory_space=pl.ANY)],
            out_specs=pl.BlockSpec((None, tr, cols), lambda l, i, mr, cr: (l0 + l, cr[0] * nh + i, 0))),
        input_output_aliases={4: 0},
        compiler_params=_params(("parallel", "parallel")),
    )(me.reshape(1), c.reshape(1), landed, mine, into)


def _pair_join(sums):
    n = len(sums)

    def body(*refs):
        bufs = refs[n:2 * n]
        send, recv = refs[2 * n:]
        x, y, c, _ = _where_am_i()
        sib = (x, y, 1 - c)
        for k in range(n):
            h = bufs[k].shape[1] // 2
            for l in range(bufs[k].shape[0]):
                piece = bufs[k].at[l, pl.ds(c * h, h)]
                pltpu.make_async_remote_copy(
                    src_ref=piece, dst_ref=piece, send_sem=send.at[k], recv_sem=recv.at[k],
                    device_id=sib, device_id_type=MESH).start()
        for k in range(n):
            h = bufs[k].shape[1] // 2
            theirs = bufs[k].at[:, pl.ds((1 - c) * h, h)]
            pltpu.make_async_remote_copy(
                src_ref=theirs, dst_ref=theirs, send_sem=send.at[k], recv_sem=recv.at[k],
                device_id=sib, device_id_type=MESH).wait()

    return pl.pallas_call(
        body, name="reduce_join",
        out_shape=[jax.ShapeDtypeStruct(s.shape, s.dtype) for s in sums],
        in_specs=[_ANY] * n, out_specs=[_ANY] * n,
        input_output_aliases={k: k for k in range(n)},
        scratch_shapes=[pltpu.SemaphoreType.DMA((n,))] * 2,
    )(*sums)


def _allreduce_small(buf):
    R = buf.shape[0]

    def body(in_ref, out_ref, land, send, recv):
        x, y, c, _ = _where_am_i()
        me = 4 * x + 2 * y + c
        land[me] = in_ref[...]
        cps = []
        for k in range(1, 8):
            px, py, pc = x ^ (k >> 2), y ^ ((k >> 1) & 1), c ^ (k & 1)
            cps.append(pltpu.make_async_remote_copy(
                src_ref=in_ref, dst_ref=land.at[me], send_sem=send.at[k - 1], recv_sem=recv.at[k - 1],
                device_id=(px, py, pc), device_id_type=MESH))
        for cp in cps:
            cp.start()
        for k in range(1, 8):
            px, py, pc = x ^ (k >> 2), y ^ ((k >> 1) & 1), c ^ (k & 1)
            slot = land.at[4 * px + 2 * py + pc]
            pltpu.make_async_remote_copy(
                src_ref=slot, dst_ref=slot, send_sem=send.at[k - 1], recv_sem=recv.at[k - 1],
                device_id=(px, py, pc), device_id_type=MESH).wait_recv()
        for cp in cps:
            cp.wait_send()
        tot = land[0]
        for d in range(1, 8):
            tot = tot + land[d]
        out_ref[...] = tot

    vm = pl.BlockSpec(memory_space=pltpu.VMEM)
    return pl.pallas_call(
        body, name="allreduce_small",
        out_shape=jax.ShapeDtypeStruct(buf.shape, F32),
        in_specs=[vm], out_specs=vm,
        scratch_shapes=[pltpu.VMEM((8, R, LANES), F32), pltpu.SemaphoreType.DMA((7,)),
                        pltpu.SemaphoreType.DMA((7,))],
    )(buf)


def _rope_tables(T):
    pos = (jnp.arange(T) - FRONT).astype(F32)
    lane = jnp.arange(LANES)
    inv_a = ROPE_THETA ** (-(2 * ((lane % 64) % 32)).astype(F32) / SWA_HEAD_DIM)
    ang_a = pos[:, None] * inv_a[None, :]
    cosa, sina = jnp.cos(ang_a), jnp.sin(ang_a)
    inv_m = ROPE_THETA ** (-(2 * ((lane - KR_LANE) % 16)).astype(F32) / MLA_ROPE_DIM)
    ang_m = pos[:, None] * inv_m[None, :]
    on = ((lane >= KR_LANE) & (lane < KR_LANE + MLA_ROPE_DIM))[None, :]
    cosm = jnp.where(on, jnp.cos(ang_m), 1.0)
    sinm = jnp.where(on, jnp.sin(ang_m), 0.0)
    return cosa, sina, cosm, sinm


def _cols_from_chips(g):
    return jnp.concatenate(g, axis=-1)


def _rows_from_chips(g):
    return jnp.concatenate(g, axis=-2)


def _cols_to_chips(w):
    c = w.shape[-1] // 4
    return jnp.stack([w[..., j * c:(j + 1) * c] for j in range(4)])


def _rows_to_chips(w):
    L, r4, c = w.shape
    return jnp.moveaxis(w.reshape(L, 4, r4 // 4, c), 1, 0)


def _layer_layouts(w_in, w_qup, w_kvup, w_o, w_gate, w_up, w_down):
    zpad = lambda n: jnp.zeros((D_MODEL, n), w_in.dtype)
    w_in = jnp.concatenate([w_in[:, :C_KR], zpad(KR_LANE), w_in[:, C_KR:IN_W],
                            zpad(LANES - KR_LANE - MLA_ROPE_DIM)], axis=-1)
    w_qup = w_qup.reshape(MLA_Q_RANK, MLA_HEADS, MLA_QK_DIM)
    w_qup = jnp.pad(w_qup, ((0, 0), (0, 0), (0, LANES - MLA_QK_DIM))).reshape(MLA_Q_RANK, SLOT_W)
    return w_in, w_qup, w_kvup, w_o, w_gate, w_up, w_down


def _local_step(x2, target, meta_full, layer_weights, p, early):
    T = BLOCK + x2.shape[0]
    L = DEPTH
    attn_norm, q_norm, kv_norm, sinks = p["attn_norm"], p["q_norm"], p["kv_norm"], p["sinks"]
    out_norm_swa, out_norm_mla, ffn_norm, final_norm = (
        p["out_norm_swa"], p["out_norm_mla"], p["ffn_norm"], p["final_norm"])

    cosa, sina, cosm, sinm = _rope_tables(T)
    no_sink = jnp.full((MLA_HEADS,), NEG, F32)
    scale_a, scale_b = SWA_HEAD_DIM ** -0.5, MLA_QK_DIM ** -0.5
    row = lambda v: v.reshape(1, -1)

    h = jnp.concatenate([jnp.zeros((FRONT, D_MODEL), F32), meta_full, x2], axis=0)
    saved = []
    weights = []
    for l in range(L):
        weights.append(_layer_layouts(*layer_weights(l, h)))
        W_in, W_qup, W_kvup, W_o, W_gate, W_up, W_down = weights[l]
        u, proj = _norm_proj("in_proj", h, row(attn_norm[l]), [W_in], False)
        qa, ka, va, qn, cn, kr = _prep1(proj, row(q_norm[l]), row(kv_norm[l]), cosa, sina, cosm, sinm)
        qb = _mm("q_up", qn, W_qup, "nn")
        kvb = _mm("kv_up", cn, W_kvup, "nn")
        qs, ks, vs = _prep2(qb, kvb, kr, cosm, sinm)
        oa, lse_a = _attn_fwd("swa_fwd", qa, ka, va, sinks[l], 4, False, scale_a, 4)
        ob, lse_b = _attn_fwd("mla_fwd", qs, ks, vs, no_sink, 1, True, scale_b, 1)
        mix = _merge_fwd(oa, ob, row(out_norm_swa[l]), row(out_norm_mla[l]))
        h1 = _mm("o_proj", mix, W_o, "nn", res=h)
        if callable(W_gate):
            W_gate, W_up, W_down = W_gate(h1), W_up(h1), W_down(h1)
            weights[l] = (W_in, W_qup, W_kvup, W_o, W_gate, W_up, W_down)
        u2, a, b, hm = _norm_proj("ffn_in", h1, row(ffn_norm[l]), [W_gate, W_up], True)
        h2 = _mm("down_proj", hm, W_down, "nn", res=h1)
        saved.append((h, u, proj, qa, ka, va, qn, cn, qs, ks, vs, oa, lse_a, ob, lse_b, mix, h1, u2, a, b, hm))
        h = h2

    dh, d_final, loss_row = _loss_head(h, target, row(final_norm))

    gw = {k: [None] * L for k in ("in", "qup", "kvup", "o", "gate", "up", "down")}
    gs = {k: [None] * L for k in ("attn", "qn", "kvn", "sink", "ga", "gb", "ffn")}

    def natural_grads(layers):
        st = lambda k: jnp.stack(gw[k][layers]) if gw[k][layers][0] is not None else None
        d_in, d_qup = st("in"), st("qup")
        if d_in is not None:
            d_in = jnp.concatenate([d_in[..., :C_KR], d_in[..., C_KR + KR_LANE:C_KR + KR_LANE + MLA_ROPE_DIM]],
                                   axis=-1)
            d_qup = d_qup.reshape(d_qup.shape[0], MLA_Q_RANK, MLA_HEADS, LANES)[..., :MLA_QK_DIM]
            d_qup = d_qup.reshape(d_qup.shape[0], MLA_Q_RANK, -1)
        return [d_in, d_qup, st("kvup"), st("o"), st("gate"), st("up"), st("down")]

    for l in reversed(range(L)):
        (h0, u, proj, qa, ka, va, qn, cn, qs, ks, vs, oa, lse_a, ob, lse_b, mix, h1, u2, a, b, hm) = saved[l]
        W_in, W_qup, W_kvup, W_o, W_gate, W_up, W_down = weights[l]
        if l == 0:
            dh = dh + early("late", natural_grads(slice(1, L)))[0, 0]
        gw["down"][l] = _mm("down_dw", hm, dh, "tn")
        da, db = _ffn_mid_bwd(dh, W_down, a, b)
        gw["gate"][l] = _mm("gate_dw", u2, da, "tn")
        gw["up"][l] = _mm("up_dw", u2, db, "tn")
        du2 = _mm("gate_dx", da, W_gate, "nt")
        du2 = _mm("up_dx", db, W_up, "nt", res=du2)
        gain = row(ffn_norm[l])
        if l == 0:
            gain = gain + early("ffn", natural_grads(slice(0, 1))[4:])[0:1, 0:1]
        dh1, gs["ffn"][l] = _rmsnorm_bwd("ffn_norm_bwd", h1, gain, du2, dh)
        gw["o"][l] = _mm("o_dw", mix, dh1, "tn")
        dmix = _mm("o_dx", dh1, W_o, "nt")
        sink_slots = jnp.repeat(sinks[l], LANES).reshape(1, SLOT_W)
        doa, dla, dob, dlb, gs["ga"][l], gs["gb"][l], dsink = _merge_bwd(
            dmix, oa, ob, lse_a, row(out_norm_swa[l]), row(out_norm_mla[l]), sink_slots)
        gs["sink"][l] = dsink.reshape(SWA_HEADS, LANES)[:, 0]
        dqs, dks, dvs = _attn_bwd("mla_bwd", qs, ks, vs, dob, lse_b, dlb, 1, True, scale_b)
        dqa, dka, dva = _attn_bwd("swa_bwd", qa, ka, va, doa, lse_a, dla, 4, False, scale_a)
        dqb, dkvb, dkr = _prep2_bwd(dqs, dks, dvs, cosm, sinm)
        gw["qup"][l] = _mm("q_up_dw", qn, dqb, "tn")
        gw["kvup"][l] = _mm("kv_up_dw", cn, dkvb, "tn")
        dqn = _mm("q_up_dx", dqb, W_qup, "nt")
        dcn = _mm("kv_up_dx", dkvb, W_kvup, "nt")
        dproj, gs["qn"][l], gs["kvn"][l] = _prep1_bwd(
            proj, dqa, dka, dva, dqn, dcn, dkr, cosa, sina, row(q_norm[l]), row(kv_norm[l]))
        gw["in"][l] = _mm("in_dw", u, dproj, "tn")
        du = _mm("in_dx", dproj, W_in, "nt")
        dh, gs["attn"][l] = _rmsnorm_bwd("attn_norm_bwd", h0, row(attn_norm[l]), du, dh1)

    return loss_row, dh, natural_grads(slice(0, 1))[:4], gs, d_final


def kernel(x, meta_tokens, attn_norm, w_in, q_norm, w_q_up, kv_norm, w_kv_up, sinks, out_norm_swa, out_norm_mla, w_o, ffn_norm, w_gate, w_up, w_down, final_norm, loss_target, m_meta_tokens, m_attn_norm, m_w_in, m_q_norm, m_w_q_up, m_kv_norm, m_w_kv_up, m_sinks, m_out_norm_swa, m_out_norm_mla, m_w_o, m_ffn_norm, m_w_gate, m_w_up, m_w_down, m_final_norm, v_meta_tokens, v_attn_norm, v_w_in, v_q_norm, v_w_q_up, v_kv_norm, v_w_kv_up, v_sinks, v_out_norm_swa, v_out_norm_mla, v_w_o, v_ffn_norm, v_w_gate, v_w_up, v_w_down, v_final_norm):
    assert x.shape[0] == 1 and x.shape[1] % BLOCK == 0
    big = [w_in, w_q_up, w_kv_up, w_o, w_gate, w_up, w_down]

    c_idx = lax.axis_index("c").astype(jnp.int32)
    chip = (2 * lax.axis_index("x") + lax.axis_index("y")).astype(jnp.int32)
    w16 = [w.astype(BF16) for w in big]
    join = [_cols_from_chips, _cols_from_chips, _cols_from_chips, _rows_from_chips, _cols_from_chips,
            _cols_from_chips, _rows_from_chips]

    def whole(ks, own, landed):
        return [join[k]([jnp.where(chip == j, o, g[j]) for j in range(4)]) for k, o, g in zip(ks, own, landed)]

    head, ffn = [0, 1, 2, 3], [4, 5, 6]
    first = _gather_weights([w16[k][0] for k in head], meta_tokens)
    meta_full = jnp.concatenate([jnp.where(chip == j, meta_tokens, first[-1][j]) for j in range(4)], axis=-1)
    send0, recv0, token, thru0, lands0 = _gather_start("gather_start_ffn", [w16[k][0:1] for k in ffn])
    state = {}

    def ffn_first(h1):
        if "ffn" not in state:
            _, landed = _gather_wait("gather_wait_ffn", send0, recv0, 0, thru0, lands0[0], h1)
            state["ffn"] = whole(ffn, [w16[k][0] for k in ffn], landed)
            later = [w[1:] for w in w16]
            later[2] = later[2] + (state["ffn"][0][0, 0] * 0).astype(BF16)
            state["late"] = _gather_start("gather_start", later)
            state["thru"] = state["late"][3]
            state["ffn"][2] = state["ffn"][2] + state["late"][2][0, 0].astype(BF16)
        return state["ffn"]

    def layer_weights(l, h):
        if l == 0:
            return whole(head, [w16[k][0] for k in head], first[:-1]) + [
                (lambda h1, i=i: ffn_first(h1)[i]) for i in range(3)]
        send, recv, _, _, lands = state["late"]
        state["thru"], landed = _gather_wait("gather_wait%d" % l, send, recv, l - 1, state["thru"], lands[l - 1], h)
        return whole(head + ffn, [w[l] for w in w16], landed)

    small_p = dict(attn_norm=attn_norm, q_norm=q_norm, kv_norm=kv_norm, sinks=sinks, out_norm_swa=out_norm_swa,
                   out_norm_mla=out_norm_mla, ffn_norm=ffn_norm, final_norm=final_norm)
    split = [_cols_to_chips, _cols_to_chips, _cols_to_chips, _rows_to_chips, _cols_to_chips, _cols_to_chips,
             _rows_to_chips]
    rounds = []

    def reduce_start(name, ks, l0, grads):
        full = [split[k](d) for k, d in zip(ks, grads)]
        got = _pair_exchange([f.astype(BF16) for f in full])
        sums = [_pair_add(f, g, c_idx) for f, g in zip(full, got)]
        send, recv, tok, thru, lands = _scatter_start("scatter_start_" + name, [s16 for _, s16 in sums])
        rounds.append((name, ks, l0, [s32 for s32, _ in sums], send, recv, thru, lands))
        return tok

    starts = {"late": lambda g: reduce_start("late", list(range(7)), 1, g),
              "ffn": lambda g: reduce_start("ffn", [4, 5, 6], 0, g)}
    loss_row, dh, d_first, gs, d_final = _local_step(
        x[0] + token[0, 0], loss_target[0], meta_full, layer_weights, small_p,
        lambda which, g: starts[which](g))
    grad_x = dh[BLOCK:][None]
    reduce_start("first", [0, 1, 2, 3], 0, d_first)

    g_big = [lax.empty(w.shape, F32) for w in big]
    for name, ks, l0, s32, send, recv, thru, lands in rounds:
        landed = _scatter_wait("scatter_wait_" + name, send, recv, thru, lands, dh)
        for k, t, m32 in zip(ks, landed, s32):
            g_big[k] = _chip_add(t, m32, chip, c_idx, g_big[k], l0)
    g_big = _pair_join(g_big)

    small = [jnp.stack(gs["attn"]).reshape(-1), jnp.stack(gs["qn"]).reshape(-1), jnp.stack(gs["kvn"]).reshape(-1),
             jnp.stack(gs["sink"]).reshape(-1), jnp.stack(gs["ga"]).reshape(-1), jnp.stack(gs["gb"]).reshape(-1),
             jnp.stack(gs["ffn"]).reshape(-1), d_final.reshape(-1)]
    sizes = [s.shape[0] for s in small]
    flat = jnp.concatenate(small + [dh[FRONT:BLOCK].reshape(-1), loss_row[0, :1]])
    n_flat = flat.shape[0]
    rows_needed = -(-n_flat // (8 * LANES)) * 8
    flat = jnp.pad(flat, (0, rows_needed * LANES - n_flat)).reshape(rows_needed, LANES)
    tot = _allreduce_small(flat).reshape(-1)
    n_small = sum(sizes)
    loss = tot[n_small + N_META * D_MODEL]
    g_meta_full = tot[n_small:n_small + N_META * D_MODEL].reshape(N_META, D_MODEL)
    g_meta_mine = lax.dynamic_slice_in_dim(g_meta_full, chip * (D_MODEL // 4), D_MODEL // 4, axis=1)

    small_w = [attn_norm, q_norm, kv_norm, sinks, out_norm_swa, out_norm_mla, ffn_norm, final_norm]
    small_m = [m_attn_norm, m_q_norm, m_kv_norm, m_sinks, m_out_norm_swa, m_out_norm_mla, m_ffn_norm, m_final_norm]
    small_v = [v_attn_norm, v_q_norm, v_kv_norm, v_sinks, v_out_norm_swa, v_out_norm_mla, v_ffn_norm, v_final_norm]
    n_rows = -(-n_small // (8 * LANES)) * 8

    def pack(arrs):
        f = jnp.concatenate([a.reshape(-1) for a in arrs])
        return jnp.pad(f, (0, n_rows * LANES - n_small), constant_values=1.0).reshape(n_rows, LANES)

    g_small_pack = jnp.pad(tot[:n_small], (0, n_rows * LANES - n_small)).reshape(n_rows, LANES)
    upd_small = _adamw("adam_small", pack(small_w), g_small_pack, pack(small_m), pack(small_v))

    def unpack(p):
        f = p.reshape(-1)
        out, off = [], 0
        for a, n in zip(small_w, sizes):
            out.append(f[off:off + n].reshape(a.shape))
            off += n
        return out

    g_small = unpack(g_small_pack)
    d_small, m_small, v_small = [unpack(p) for p in upd_small]
    d_meta, nm_meta, nv_meta = _adamw("adam_meta", meta_tokens, g_meta_mine, m_meta_tokens, v_meta_tokens)

    big_m = [m_w_in, m_w_q_up, m_w_kv_up, m_w_o, m_w_gate, m_w_up, m_w_down]
    big_v = [v_w_in, v_w_q_up, v_w_kv_up, v_w_o, v_w_gate, v_w_up, v_w_down]
    upd_big = [_adamw("adam_big", w, g, m, v) for w, g, m, v in zip(big, g_big, big_m, big_v)]

    names = ["meta_tokens", "attn_norm", "w_in", "q_norm", "w_q_up", "kv_norm", "w_kv_up", "sinks",
             "out_norm_swa", "out_norm_mla", "w_o", "ffn_norm", "w_gate", "w_up", "w_down", "final_norm"]
    small_idx = {"attn_norm": 0, "q_norm": 1, "kv_norm": 2, "sinks": 3, "out_norm_swa": 4,
                 "out_norm_mla": 5, "ffn_norm": 6, "final_norm": 7}
    big_idx = {"w_in": 0, "w_q_up": 1, "w_kv_up": 2, "w_o": 3, "w_gate": 4, "w_up": 5, "w_down": 6}
    grads, deltas, new_m, new_v = [], [], [], []
    for nme in names:
        if nme == "meta_tokens":
            quad = (g_meta_mine, d_meta, nm_meta, nv_meta)
        elif nme in small_idx:
            i = small_idx[nme]
            quad = (g_small[i], d_small[i], m_small[i], v_small[i])
        else:
            i = big_idx[nme]
            quad = (g_big[i], *upd_big[i])
        grads.append(quad[0]); deltas.append(quad[1]); new_m.append(quad[2]); new_v.append(quad[3])
    return (loss, grad_x, *grads, *deltas, *new_m, *new_v)
```

```python
import jax
import jax.numpy as jnp
from jax import lax
from jax.experimental import pallas as pl
from jax.experimental.pallas import tpu as pltpu

F32 = jnp.float32
BF16 = jnp.bfloat16
MXU_DTYPE = BF16

D_MODEL = 1024
DEPTH = 4
N_META = 16
BLOCK = 128
WINDOW = 128
ROPE_THETA = 10000.0
EPS = 1e-6
NEG = -1e30
SWA_HEADS = 8
SWA_KV_HEADS = 2
SWA_HEAD_DIM = 64
MLA_HEADS = 8
MLA_Q_RANK = 256
MLA_KV_RANK = 128
MLA_NOPE_DIM = 64
MLA_ROPE_DIM = 32
MLA_V_DIM = 64
MLA_QK_DIM = MLA_NOPE_DIM + MLA_ROPE_DIM
D_FF = 2816
FRONT = (-N_META) % BLOCK
LANES = 128
SLOT_W = 8 * LANES
C_QA, C_KA, C_VA, C_QL, C_KL, C_KR, IN_WP = 0, 512, 640, 768, 1024, 1152, 1280
KR_LANE = 64
IN_W = 1184

ADAM_LR, ADAM_B1, ADAM_B2, ADAM_EPS, ADAM_WD, ADAM_STEP = 0.001, 0.9, 0.999, 1e-08, 0.01, 10

VMEM_LIMIT = 48 * 1024 * 1024
MESH = pl.DeviceIdType.MESH


def _tile(n, prefs):
    for t in prefs:
        if n % t == 0:
            return t
    return n


def _params(sem):
    return pltpu.CompilerParams(dimension_semantics=sem, vmem_limit_bytes=VMEM_LIMIT)


_DIMS = {"nn": (((1,), (0,)), ((), ())), "nt": (((1,), (1,)), ((), ())), "tn": (((0,), (0,)), ((), ()))}


def _mm(name, a, b, mode, out_dtype=F32, res=None):
    if mode == "nn":
        (M, K), (_, N) = a.shape, b.shape
    elif mode == "nt":
        (M, K), (N, _) = a.shape, b.shape
    else:
        (K, M), (_, N) = a.shape, b.shape
    lane_tiles = (1408, 1024, 640, 768, 512, 384, 256, 128)
    row_tiles = (1056, 528, 512, 384, 256, 128)
    bm = _tile(M, lane_tiles if mode == "tn" else row_tiles)
    bn = _tile(N, lane_tiles)
    bk = _tile(K, row_tiles if mode == "tn" else lane_tiles)
    nk = K // bk
    if mode == "tn":
        a_spec = pl.BlockSpec((bk, bm), lambda i, j, k: (k, i))
    else:
        a_spec = pl.BlockSpec((bm, bk), lambda i, j, k: (i, k))
    if mode == "nt":
        b_spec = pl.BlockSpec((bn, bk), lambda i, j, k: (j, k))
    else:
        b_spec = pl.BlockSpec((bk, bn), lambda i, j, k: (k, j))
    o_spec = pl.BlockSpec((bm, bn), lambda i, j, k: (i, j))
    in_specs = [a_spec, b_spec]
    args = [a, b]
    if res is not None:
        in_specs.append(o_spec)
        args.append(res)
    dims = _DIMS[mode]

    def kern(a_ref, b_ref, *rest):
        if res is not None:
            r_ref, o_ref = rest[0], rest[1]
            scr = rest[2:]
        else:
            r_ref, o_ref = None, rest[0]
            scr = rest[1:]
        p = lax.dot_general(a_ref[...].astype(MXU_DTYPE), b_ref[...].astype(MXU_DTYPE), dims,
                            preferred_element_type=F32)

        def finish(val):
            if r_ref is not None:
                val = val + r_ref[...]
            o_ref[...] = val.astype(o_ref.dtype)

        if nk == 1:
            finish(p)
        else:
            acc = scr[0]
            k = pl.program_id(2)

            @pl.when(k == 0)
            def _():
                acc[...] = p

            @pl.when(k > 0)
            def _():
                acc[...] += p

            @pl.when(k == nk - 1)
            def _():
                finish(acc[...])

    return pl.pallas_call(
        kern, name=name,
        out_shape=jax.ShapeDtypeStruct((M, N), out_dtype),
        grid=(M // bm, N // bn, nk),
        in_specs=in_specs, out_specs=o_spec,
        scratch_shapes=[pltpu.VMEM((bm, bn), F32)] if nk > 1 else [],
        compiler_params=_params(("parallel", "parallel", "arbitrary")),
    )(*args)


def _rowmap(name, body, rows, vecs, outs, accs=(), tr_prefs=(384, 256, 128)):
    R = rows[0].shape[0]
    tr = _tile(R, tr_prefs)
    n_r, n_v, n_o, n_a = len(rows), len(vecs), len(outs), len(accs)

    def kern(*refs):
        ins = [r[...] for r in refs[:n_r + n_v]]
        o_refs = refs[n_r + n_v:n_r + n_v + n_o]
        a_refs = refs[n_r + n_v + n_o:]
        res = body(*ins)
        for o_ref, val in zip(o_refs, res[:n_o]):
            o_ref[...] = val.astype(o_ref.dtype)
        if n_a:
            first = pl.program_id(0) == 0

            @pl.when(first)
            def _():
                for a_ref, val in zip(a_refs, res[n_o:]):
                    a_ref[...] = val

            @pl.when(jnp.logical_not(first))
            def _():
                for a_ref, val in zip(a_refs, res[n_o:]):
                    a_ref[...] += val

    in_specs = [pl.BlockSpec((tr, r.shape[1]), lambda i: (i, 0)) for r in rows]
    in_specs += [pl.BlockSpec((1, v.shape[1]), lambda i: (0, 0)) for v in vecs]
    out_specs = [pl.BlockSpec((tr, c), lambda i: (i, 0)) for c, _ in outs]
    out_specs += [pl.BlockSpec((1, c), lambda i: (0, 0)) for c in accs]
    out_shape = [jax.ShapeDtypeStruct((R, c), dt) for c, dt in outs]
    out_shape += [jax.ShapeDtypeStruct((1, c), F32) for c in accs]
    return pl.pallas_call(
        kern, name=name, out_shape=out_shape, grid=(R // tr,),
        in_specs=in_specs, out_specs=out_specs,
        compiler_params=_params(("arbitrary",) if n_a else ("parallel",)),
    )(*rows, *vecs)


def _lane(shape):
    return lax.broadcasted_iota(jnp.int32, shape, 1)


def _rot_swa(x):
    lane = _lane(x.shape)
    return jnp.where((lane & 63) < 32, -pltpu.roll(x, 96, 1), pltpu.roll(x, 32, 1))


def _rot_mla(x):
    lane = _lane(x.shape)
    lo = jnp.where(lane >= KR_LANE, -pltpu.roll(x, 112, 1), 0.0)
    hi = jnp.where(lane < KR_LANE + MLA_ROPE_DIM, pltpu.roll(x, 16, 1), 0.0)
    return jnp.where(lane < KR_LANE + 16, lo, hi)


def _rope(x, cos, sin, rot):
    return x * cos + rot(x) * sin


def _rope_t(g, cos, sin, rot):
    return g * cos - rot(g * sin)


def _low(x):
    return jnp.where(_lane(x.shape) < 64, x, 0.0)


def _value_slot(x):
    lane = _lane(x.shape)
    return jnp.where(lane < 64, x, jnp.where(lane == 64, 1.0, 0.0))


def _blk(x, j):
    return x[:, j * LANES:(j + 1) * LANES]


def _rms_r(x, width):
    return lax.rsqrt(jnp.sum(x * x, axis=-1, keepdims=True) * (1.0 / width) + EPS)


def _rms_bwd(x, g, dy, width):
    r = _rms_r(x, width)
    gdy = dy * g
    dot = jnp.sum(gdy * x, axis=-1, keepdims=True)
    dx = r * gdy - x * (r * r * r * (1.0 / width) * dot)
    return dx, dy * x * r


def _colsum(x):
    return jnp.sum(x, axis=0, keepdims=True)


def _rmsnorm_bwd(name, x, g, dy, dres):
    def body(xv, dyv, dr, gv):
        dx, dg = _rms_bwd(xv, gv, dyv, D_MODEL)
        return dx + dr, _colsum(dg)
    return _rowmap(name, body, [x, dy, dres], [g], [(D_MODEL, F32)], [D_MODEL])


def _prep1(proj, qn_g, kv_g, cosa, sina, cosm, sinm):
    def body(p, ca, sa, cm, sm, gq, gk):
        qa = []
        for j in range(4):
            xr = _rope(_blk(p, j), ca, sa, _rot_swa)
            qa += [_low(xr), _low(pltpu.roll(xr, 64, 1))]
        kr_ = _rope(_blk(p, C_KA // LANES), ca, sa, _rot_swa)
        ka = [_low(kr_), _low(pltpu.roll(kr_, 64, 1))]
        vv = _blk(p, C_VA // LANES)
        va = [_value_slot(vv), _value_slot(pltpu.roll(vv, 64, 1))]
        ql = p[:, C_QL:C_QL + MLA_Q_RANK]
        qn = ql * _rms_r(ql, MLA_Q_RANK) * gq
        kl = p[:, C_KL:C_KL + MLA_KV_RANK]
        cn = kl * _rms_r(kl, MLA_KV_RANK) * gk
        kr = _rope(_blk(p, C_KR // LANES), cm, sm, _rot_mla)
        return (jnp.concatenate(qa, 1), jnp.concatenate(ka, 1), jnp.concatenate(va, 1), qn, cn, kr)
    return _rowmap("prep1", body, [proj, cosa, sina, cosm, sinm], [qn_g, kv_g],
                   [(SLOT_W, BF16), (2 * LANES, BF16), (2 * LANES, BF16),
                    (MLA_Q_RANK, BF16), (MLA_KV_RANK, BF16), (LANES, F32)])


def _prep2(qb, kvb, kr, cosm, sinm):
    def body(q, kv, krv, cm, sm):
        qs, ks, vs = [], [], []
        for h in range(MLA_HEADS):
            qs.append(_rope(_blk(q, h), cm, sm, _rot_mla))
            kvh = _blk(kv, h)
            ks.append(_low(kvh) + krv)
            vs.append(_value_slot(pltpu.roll(kvh, 64, 1)))
        return jnp.concatenate(qs, 1), jnp.concatenate(ks, 1), jnp.concatenate(vs, 1)
    return _rowmap("prep2", body, [qb, kvb, kr, cosm, sinm], [],
                   [(SLOT_W, BF16), (SLOT_W, BF16), (SLOT_W, BF16)])


def _compact(slots):
    return jnp.concatenate(
        [_blk(slots, 2 * j) + pltpu.roll(_blk(slots, 2 * j + 1), 64, 1) for j in range(4)], 1)


def _expand(nat):
    out = []
    for j in range(4):
        b = _blk(nat, j)
        out += [_low(b), _low(pltpu.roll(b, 64, 1))]
    return jnp.concatenate(out, 1)


def _merge_fwd(oa, ob, ga, gb):
    def body(a, b, gav, gbv):
        xa, xb = _compact(a), _compact(b)
        return (jnp.concatenate([xa * _rms_r(xa, 512) * gav, xb * _rms_r(xb, 512) * gbv], 1),)
    return _rowmap("merge_fwd", body, [oa, ob], [ga, gb], [(D_MODEL, BF16)])[0]


def _merge_bwd(dmix, oa, ob, lse_a, ga, gb, sink_slots):
    def body(dm, a, b, lse, gav, gbv, sk):
        outs = []
        accs = []
        for o, g, lo in ((a, gav, 0), (b, gbv, 512)):
            x = _compact(o)
            dx, dg = _rms_bwd(x, g, dm[:, lo:lo + 512], 512)
            do = _expand(dx)
            delta = jnp.concatenate(
                [jnp.broadcast_to(jnp.sum(_blk(do, h) * _blk(o, h), axis=-1, keepdims=True),
                                  (do.shape[0], LANES)) for h in range(8)], 1)
            outs += [do, delta]
            accs.append(_colsum(dg))
        dsink = _colsum(-jnp.exp(sk - lse) * outs[1])
        return (*outs, *accs, dsink)
    return _rowmap("merge_bwd", body, [dmix, oa, ob, lse_a], [ga, gb, sink_slots],
                   [(SLOT_W, BF16), (SLOT_W, F32), (SLOT_W, BF16), (SLOT_W, F32)],
                   [512, 512, SLOT_W])


def _prep2_bwd(dq, dk, dv, cosm, sinm):
    def body(dqv, dkv, dvv, cm, sm):
        dqv, dkv, dvv = dqv.astype(F32), dkv.astype(F32), dvv.astype(F32)
        dqb, dkvb = [], []
        krsum = jnp.zeros((dqv.shape[0], LANES), F32)
        for h in range(MLA_HEADS):
            dqb.append(_rope_t(_blk(dqv, h), cm, sm, _rot_mla))
            dkh = _blk(dkv, h)
            dkvb.append(_low(dkh) + pltpu.roll(_blk(dvv, h), 64, 1))
            krsum = krsum + dkh
        lane = _lane(krsum.shape)
        dkr = jnp.where((lane >= KR_LANE) & (lane < KR_LANE + MLA_ROPE_DIM),
                        _rope_t(krsum, cm, sm, _rot_mla), 0.0)
        return jnp.concatenate(dqb, 1), jnp.concatenate(dkvb, 1), dkr
    return _rowmap("prep2_bwd", body, [dq, dk, dv, cosm, sinm], [],
                   [(SLOT_W, BF16), (SLOT_W, BF16), (LANES, F32)])


def _prep1_bwd(proj, dqa, dka, dva, dqn, dcn, dkr, cosa, sina, qn_g, kv_g):
    def body(p, dq, dk, dv, dqnv, dcnv, dkrv, ca, sa, gq, gk):
        dq, dk, dv = dq.astype(F32), dk.astype(F32), dv.astype(F32)
        cols = []
        for j in range(4):
            nat = _blk(dq, 2 * j) + pltpu.roll(_blk(dq, 2 * j + 1), 64, 1)
            cols.append(_rope_t(nat, ca, sa, _rot_swa))
        grp = lambda d, g: sum(_blk(d, 4 * g + i) for i in range(4))
        cols.append(_rope_t(grp(dk, 0) + pltpu.roll(grp(dk, 1), 64, 1), ca, sa, _rot_swa))
        cols.append(grp(dv, 0) + pltpu.roll(grp(dv, 1), 64, 1))
        dql, dgq = _rms_bwd(p[:, C_QL:C_QL + MLA_Q_RANK], gq, dqnv, MLA_Q_RANK)
        dkl, dgk = _rms_bwd(p[:, C_KL:C_KL + MLA_KV_RANK], gk, dcnv, MLA_KV_RANK)
        cols += [dql, dkl, dkrv]
        return jnp.concatenate(cols, 1), _colsum(dgq), _colsum(dgk)
    return _rowmap("prep1_bwd", body, [proj, dqa, dka, dva, dqn, dcn, dkr, cosa, sina], [qn_g, kv_g],
                   [(IN_WP, BF16)], [MLA_Q_RANK, MLA_KV_RANK], tr_prefs=(192, 128))


def _sigmoid(x):
    return 1.0 / (1.0 + jnp.exp(-x))


def _ffn_tiles(T, F):
    return _tile(T, (528, 512, 384, 256, 128)), _tile(F, (1408, 1024, 640, 512, 256, 128))


def _norm_proj(name, h, g, weights, swiglu):
    (T, D), F = h.shape, weights[0].shape[1]
    bm, bn = _ffn_tiles(T, F)
    nw = len(weights)

    def kern(h_ref, g_ref, *rest):
        w_refs, u_ref, o_refs, u_scr = rest[:nw], rest[nw], rest[nw + 1:-1], rest[-1]

        @pl.when(pl.program_id(1) == 0)
        def _():
            x = h_ref[...]
            u = (x * _rms_r(x, D) * g_ref[...]).astype(u_scr.dtype)
            u_scr[...] = u
            u_ref[...] = u

        uv = u_scr[...]
        prods = [jnp.dot(uv, w[...], preferred_element_type=F32) for w in w_refs]
        for o_ref, p in zip(o_refs, prods):
            o_ref[...] = p
        if swiglu:
            a, b = prods
            o_refs[nw][...] = (a * _sigmoid(a) * b).astype(o_refs[nw].dtype)

    w_spec = pl.BlockSpec((D, bn), lambda i, j: (0, j))
    row_spec = pl.BlockSpec((bm, D), lambda i, j: (i, 0))
    o_spec = pl.BlockSpec((bm, bn), lambda i, j: (i, j))
    n_out = nw + (1 if swiglu else 0)
    return pl.pallas_call(
        kern, name=name,
        out_shape=[jax.ShapeDtypeStruct((T, D), MXU_DTYPE)] + [jax.ShapeDtypeStruct((T, F), F32)] * nw
        + ([jax.ShapeDtypeStruct((T, F), MXU_DTYPE)] if swiglu else []),
        grid=(T // bm, F // bn),
        in_specs=[row_spec, pl.BlockSpec((1, D), lambda i, j: (0, 0))] + [w_spec] * nw,
        out_specs=[row_spec] + [o_spec] * n_out,
        scratch_shapes=[pltpu.VMEM((bm, D), MXU_DTYPE)],
        compiler_params=_params(("parallel", "arbitrary")),
    )(h, g, *weights)


def _ffn_mid_bwd(dh, w_down, a, b):
    (T, D), F = dh.shape, w_down.shape[0]
    bm, bn = _ffn_tiles(T, F)

    def kern(dh_ref, wd_ref, a_ref, b_ref, da_ref, db_ref):
        d = lax.dot_general(dh_ref[...].astype(MXU_DTYPE), wd_ref[...], _DIMS["nt"], preferred_element_type=F32)
        av, bv = a_ref[...], b_ref[...]
        s = _sigmoid(av)
        da_ref[...] = (d * bv * (s * (1.0 + av * (1.0 - s)))).astype(da_ref.dtype)
        db_ref[...] = (d * (av * s)).astype(db_ref.dtype)

    o_spec = pl.BlockSpec((bm, bn), lambda i, j: (i, j))
    return pl.pallas_call(
        kern, name="ffn_mid_bwd",
        out_shape=[jax.ShapeDtypeStruct((T, F), MXU_DTYPE)] * 2,
        grid=(T // bm, F // bn),
        in_specs=[pl.BlockSpec((bm, D), lambda i, j: (i, 0)), pl.BlockSpec((bn, D), lambda i, j: (j, 0)),
                  o_spec, o_spec],
        out_specs=[o_spec, o_spec],
        compiler_params=_params(("parallel", "parallel")),
    )(dh, w_down, a, b)


def _loss_head(h, target, g):
    T = h.shape[0]
    nb = T // BLOCK

    def kern(h_ref, t_ref, g_ref, dh_ref, dg_ref, loss_ref, acc):
        i = pl.program_id(0)

        @pl.when(i == 0)
        def _():
            dh_ref[...] = jnp.zeros_like(dh_ref)
            dg_ref[...] = jnp.zeros_like(dg_ref)
            acc[...] = jnp.zeros_like(acc)

        @pl.when(i > 0)
        def _():
            x = h_ref[...]
            gv = g_ref[...]
            e = x * _rms_r(x, D_MODEL) * gv - t_ref[...]
            acc[...] += _colsum(e * e)
            dx, dg = _rms_bwd(x, gv, e * (1.0 / D_MODEL), D_MODEL)
            dh_ref[...] = dx
            dg_ref[...] += _colsum(dg)

        @pl.when(i == nb - 1)
        def _():
            tot = jnp.sum(acc[...], axis=-1, keepdims=True) * (0.5 / D_MODEL)
            loss_ref[...] = jnp.broadcast_to(tot, loss_ref.shape)

    return pl.pallas_call(
        kern, name="loss_head",
        out_shape=[jax.ShapeDtypeStruct((T, D_MODEL), F32), jax.ShapeDtypeStruct((1, D_MODEL), F32),
                   jax.ShapeDtypeStruct((1, LANES), F32)],
        grid=(nb,),
        in_specs=[pl.BlockSpec((BLOCK, D_MODEL), lambda i: (i, 0)),
                  pl.BlockSpec((BLOCK, D_MODEL), lambda i: (jnp.maximum(i - 1, 0), 0)),
                  pl.BlockSpec((1, D_MODEL), lambda i: (0, 0))],
        out_specs=[pl.BlockSpec((BLOCK, D_MODEL), lambda i: (i, 0)),
                   pl.BlockSpec((1, D_MODEL), lambda i: (0, 0)),
                   pl.BlockSpec((1, LANES), lambda i: (0, 0))],
        scratch_shapes=[pltpu.VMEM((1, D_MODEL), F32)],
        compiler_params=_params(("arbitrary",)),
    )(h, target, g)


LOG2E = 1.4426950408889634


def _attn_plan(T, causal):
    tq = _tile(T, (384, 256, 128))
    ck = min(2 * tq, T) if causal else min(tq + WINDOW, T)
    return tq, ck, (-(-T // ck) if causal else 1)


def _chunk(i, c, T, tq, ck, causal):
    if causal:
        return pl.multiple_of(jnp.minimum(c * ck, T - ck), LANES), c * ck
    return pl.multiple_of(jnp.clip(i * tq - WINDOW, 0, T - ck), LANES), 0


def _n_chunks(i, tq, ck, causal):
    return ((i + 1) * tq + ck - 1) // ck if causal else 1


def _mask(s, i, start, first, tq, ck, causal):
    qpos = i * tq + lax.broadcasted_iota(jnp.int32, (tq, 1), 0)
    kpos = start + lax.broadcasted_iota(jnp.int32, (tq, ck), 1)
    low = jnp.maximum(jnp.where(qpos < FRONT, 0, FRONT), first)
    if not causal:
        low = jnp.maximum(low, qpos - (WINDOW - 1))
    return jnp.where(kpos >= low, jnp.where(kpos <= qpos, s, NEG), NEG)


def _chunk_loop(n, body, init, several):
    carry = body(0, init, True)
    if not several:
        return carry
    carry = body(n - 1, carry, True)
    return lax.fori_loop(1, n - 1, lambda c, cr: body(c, cr, False), carry)


def _tile_sweep(nq, tq, ck, causal, q_tiles, pair=False):
    if pair and nq >= 2 and (not causal or ck == 2 * tq):
        if causal:
            q_tiles([0, 1], False)
            lax.fori_loop(1, nq // 2, lambda p, cr: q_tiles([2 * p, 2 * p + 1], True) or cr, 0)
        else:
            lax.fori_loop(0, nq // 2, lambda p, cr: q_tiles([2 * p, 2 * p + 1], False) or cr, 0)
        if nq % 2:
            q_tiles([nq - 1], causal and nq >= 3)
        return
    one = min(nq, ck // tq) if causal else nq
    lax.fori_loop(0, one, lambda i, cr: q_tiles([i], False) or cr, 0)
    if one < nq:
        lax.fori_loop(one, nq, lambda i, cr: q_tiles([i], True) or cr, 0)


BWD_HEADS_PER_STEP = 1


def _head_cols(group, hp):
    q_cols = lambda hh: slice(hh * LANES, (hh + 1) * LANES)
    if group == 1:
        return q_cols, q_cols, hp * LANES
    assert group % hp == 0
    return q_cols, (lambda hh: slice(0, LANES)), LANES


def _whole(T, width, index, single):
    if single:
        return pl.BlockSpec((T, width), index, pipeline_mode=pl.Buffered(1))
    return pl.BlockSpec((T, width), index)


def _attn_fwd(name, q, k, v, sinks, group, causal, scale, hp):
    T = q.shape[0]
    H = q.shape[1] // LANES
    HP = hp
    tq, ck, slots = _attn_plan(T, causal)
    nq = T // tq
    c2 = scale * LOG2E
    q_cols, k_cols, kw = _head_cols(group, HP)

    def kern(sink_ref, q_ref, k_ref, v_ref, o_ref, lse_ref, s_scr):
        sink2 = [sink_ref[pl.program_id(0) * HP + hh] * LOG2E for hh in range(HP)]

        def q_tiles(tiles, several):
            rows = [pl.ds(pl.multiple_of(i * tq, tq), tq) for i in tiles]
            chains = [(t, hh) for t in range(len(tiles)) for hh in range(HP)]
            qq = [q_ref[rows[t], q_cols(hh)] for t, hh in chains]
            n = _n_chunks(tiles[0], tq, ck, causal)

            def score(c, m2, masked):
                out = []
                for n_ch, (t, hh) in enumerate(chains):
                    start, first = _chunk(tiles[t], c, T, tq, ck, causal)
                    s = lax.dot_general(qq[n_ch], k_ref[pl.ds(start, ck), k_cols(hh)], _DIMS["nt"],
                                        preferred_element_type=F32) * c2
                    if masked:
                        s = _mask(s, tiles[t], start, first, tq, ck, causal)
                    s_scr[n_ch, c] = s
                    out.append(jnp.maximum(m2[n_ch], jnp.max(s, axis=-1, keepdims=True)))
                return tuple(out)

            m2 = _chunk_loop(n, score, tuple(jnp.full((tq, 1), sink2[hh], F32) for _, hh in chains), several)

            def weigh(c, acc):
                out = []
                for n_ch, (t, hh) in enumerate(chains):
                    start, _ = _chunk(tiles[t], c, T, tq, ck, causal)
                    p = jnp.exp2(s_scr[n_ch, c] - m2[n_ch])
                    out.append(acc[n_ch] + jnp.dot(p.astype(MXU_DTYPE), v_ref[pl.ds(start, ck), k_cols(hh)],
                                                   preferred_element_type=F32))
                return tuple(out)

            acc = lax.fori_loop(0, n, weigh, tuple(jnp.zeros((tq, LANES), F32) for _ in chains))
            lane = _lane((tq, LANES))
            for n_ch, (t, hh) in enumerate(chains):
                l = acc[n_ch][:, 64:65] + jnp.exp2(sink2[hh] - m2[n_ch])
                o_ref[rows[t], q_cols(hh)] = jnp.where(lane < 64, acc[n_ch] / l, 0.0)
                lse_ref[rows[t], q_cols(hh)] = jnp.broadcast_to(m2[n_ch] * (1.0 / LOG2E) + jnp.log(l), (tq, LANES))

        _tile_sweep(nq, tq, ck, causal, q_tiles, pair=True)

    q_spec = _whole(T, HP * LANES, lambda g: (0, g), HP > 1)
    kv_spec = _whole(T, kw, (lambda g: (0, g)) if group == 1 else (lambda g: (0, (g * HP) // group)), HP > 1)
    return pl.pallas_call(
        kern, name=name,
        out_shape=[jax.ShapeDtypeStruct((T, H * LANES), F32)] * 2,
        grid=(H // HP,),
        in_specs=[pl.BlockSpec(memory_space=pltpu.SMEM), q_spec, kv_spec, kv_spec],
        out_specs=[q_spec, q_spec],
        scratch_shapes=[pltpu.VMEM((2 * HP, slots, tq, ck), F32)],
        compiler_params=_params(("parallel",)),
    )(sinks, q, k, v)


def _attn_bwd(name, q, k, v, do, lse, delta, group, causal, scale):
    T = q.shape[0]
    H = q.shape[1] // LANES
    HP = BWD_HEADS_PER_STEP
    tq, ck, _ = _attn_plan(T, causal)
    nq = T // tq
    c2 = scale * LOG2E
    q_cols, k_cols, kw = _head_cols(group, HP)

    def kern(q_ref, k_ref, v_ref, do_ref, lse_ref, dl_ref, dq_ref, dk_ref, dv_ref, dk_acc, dv_acc):
        dk_acc[...] = jnp.zeros_like(dk_acc)
        dv_acc[...] = jnp.zeros_like(dv_acc)

        def q_tiles(tiles, several):
            rows = [pl.ds(pl.multiple_of(i * tq, tq), tq) for i in tiles]
            chains = [(t, hh) for t in range(len(tiles)) for hh in range(HP)]
            qq = [q_ref[rows[t], q_cols(hh)] for t, hh in chains]
            dd = [do_ref[rows[t], q_cols(hh)] for t, hh in chains]
            lse2 = [lse_ref[rows[t], q_cols(hh)][:, 0:1] * LOG2E for t, hh in chains]
            dl_c = [dl_ref[rows[t], q_cols(hh)][:, 0:1] for t, hh in chains]

            def chunk(c, dq, masked):
                out = []
                for n_ch, (t, hh) in enumerate(chains):
                    start, first = _chunk(tiles[t], c, T, tq, ck, causal)
                    keys = pl.ds(start, ck)
                    kk, vv = k_ref[keys, k_cols(hh)], v_ref[keys, k_cols(hh)]
                    s = lax.dot_general(qq[n_ch], kk, _DIMS["nt"], preferred_element_type=F32)
                    if masked:
                        s = _mask(s, tiles[t], start, first, tq, ck, causal)
                    p = jnp.exp2(s * c2 - lse2[n_ch])
                    dv_acc[keys, q_cols(hh)] += lax.dot_general(p.astype(MXU_DTYPE), dd[n_ch], _DIMS["tn"],
                                                                preferred_element_type=F32)
                    dp = lax.dot_general(dd[n_ch], vv, _DIMS["nt"], preferred_element_type=F32)
                    ds = (p * (dp - dl_c[n_ch])).astype(MXU_DTYPE)
                    dk_acc[keys, q_cols(hh)] += lax.dot_general(ds, qq[n_ch], _DIMS["tn"],
                                                                preferred_element_type=F32) * scale
                    out.append(dq[n_ch] + jnp.dot(ds, kk, preferred_element_type=F32))
                return tuple(out)

            dq = _chunk_loop(_n_chunks(tiles[0], tq, ck, causal), chunk,
                             tuple(jnp.zeros((tq, LANES), F32) for _ in chains), several)
            for n_ch, (t, hh) in enumerate(chains):
                dq_ref[rows[t], q_cols(hh)] = (dq[n_ch] * scale).astype(dq_ref.dtype)

        _tile_sweep(nq, tq, ck, causal, q_tiles, pair=True)
        dk_ref[...] = dk_acc[...].astype(dk_ref.dtype)
        dv_ref[...] = dv_acc[...].astype(dv_ref.dtype)

    q_spec = _whole(T, HP * LANES, lambda g: (0, g), False)
    kv_spec = _whole(T, kw, (lambda g: (0, g)) if group == 1 else (lambda g: (0, (g * HP) // group)), False)
    return pl.pallas_call(
        kern, name=name,
        out_shape=[jax.ShapeDtypeStruct((T, H * LANES), MXU_DTYPE)] * 3,
        grid=(H // HP,),
        in_specs=[q_spec, kv_spec, kv_spec, q_spec, q_spec, q_spec],
        out_specs=[q_spec, q_spec, q_spec],
        scratch_shapes=[pltpu.VMEM((T, HP * LANES), F32)] * 2,
        compiler_params=_params(("parallel",)),
    )(q, k, v, do, lse, delta)


def _ew(name, fn, ins, out_dtypes):
    shape = ins[0].shape
    flat = [a.reshape(-1, shape[-1]) for a in ins]
    R, C = flat[0].shape
    tr = _tile(R, (512, 256, 128, 64, 32, 16, 8))
    n_in = len(ins)

    def kern(*refs):
        res = fn(*[r[...] for r in refs[:n_in]])
        for o_ref, val in zip(refs[n_in:], res):
            o_ref[...] = val.astype(o_ref.dtype)

    spec = pl.BlockSpec((tr, C), lambda i: (i, 0))
    outs = pl.pallas_call(
        kern, name=name,
        out_shape=[jax.ShapeDtypeStruct((R, C), dt) for dt in out_dtypes],
        grid=(R // tr,), in_specs=[spec] * n_in, out_specs=[spec] * len(out_dtypes),
        compiler_params=_params(("parallel",)),
    )(*flat)
    return [o.reshape(shape) for o in outs]


def _adamw(name, w, g, m, v):
    c1 = 1.0 - ADAM_B1 ** ADAM_STEP
    c2 = 1.0 - ADAM_B2 ** ADAM_STEP

    def fn(wv, gv, mv, vv):
        mn = ADAM_B1 * mv + (1.0 - ADAM_B1) * gv
        vn = ADAM_B2 * vv + (1.0 - ADAM_B2) * (gv * gv)
        delta = -ADAM_LR * ((mn / c1) / (jnp.sqrt(vn / c2) + ADAM_EPS) + ADAM_WD * wv)
        return delta, mn, vn

    return _ew(name, fn, [w, g, m, v], [F32, F32, F32])


_ANY = pl.BlockSpec(memory_space=pl.ANY)


def _where_am_i():
    x, y, c = lax.axis_index("x"), lax.axis_index("y"), lax.axis_index("c")
    chips = [(1 - x, y), (x, 1 - y), (1 - x, 1 - y)]
    return x, y, c, chips


def _gather_weights(shards, meta):
    arrs = list(shards) + [meta]
    n = len(arrs)
    per = [a.shape[0] // 2 for a in arrs]

    def body(*refs):
        ins, outs = refs[:n], refs[n:2 * n]
        send1, recv1, send2, recv2 = refs[2 * n:]
        x, y, c, chips = _where_am_i()
        me = 2 * x + y

        def half(ref, k, cc):
            return ref.at[pl.ds(per[k] * cc, per[k])]

        first = []
        for k in range(n):
            for j, (cx, cy) in enumerate(chips):
                first.append(pltpu.make_async_remote_copy(
                    src_ref=half(ins[k], k, c), dst_ref=half(outs[k].at[me], k, c),
                    send_sem=send1.at[k, j], recv_sem=recv1.at[k, j],
                    device_id=(cx, cy, c), device_id_type=MESH))
        for cp in first:
            cp.start()
        passed = []
        for k in range(n):
            for j, (cx, cy) in enumerate(chips):
                landed = half(outs[k].at[2 * cx + cy], k, c)
                pltpu.make_async_remote_copy(
                    src_ref=landed, dst_ref=landed, send_sem=send1.at[k, j], recv_sem=recv1.at[k, j],
                    device_id=(cx, cy, c), device_id_type=MESH).wait_recv()
                fwd = pltpu.make_async_remote_copy(
                    src_ref=landed, dst_ref=landed, send_sem=send2.at[k, j], recv_sem=recv2.at[k, j],
                    device_id=(x, y, 1 - c), device_id_type=MESH)
                fwd.start()
                passed.append(fwd)
        for k in range(n):
            for j, (cx, cy) in enumerate(chips):
                other = half(outs[k].at[2 * cx + cy], k, 1 - c)
                pltpu.make_async_remote_copy(
                    src_ref=other, dst_ref=other, send_sem=send2.at[k, j], recv_sem=recv2.at[k, j],
                    device_id=(x, y, 1 - c), device_id_type=MESH).wait_recv()
        for cp in first + passed:
            cp.wait_send()

    return pl.pallas_call(
        body, name="gather_weights",
        out_shape=[jax.ShapeDtypeStruct((4,) + a.shape, a.dtype) for a in arrs],
        in_specs=[_ANY] * n, out_specs=[_ANY] * n,
        scratch_shapes=[pltpu.SemaphoreType.DMA((n, 3))] * 4,
    )(*arrs)


_HBM = pl.BlockSpec(memory_space=pltpu.HBM)
_SEM = pl.BlockSpec(memory_space=pltpu.SEMAPHORE)
_FLOWS = pltpu.SideEffectType.DATAFLOW_SIDE_EFFECTING


def _in_hbm(a):
    return pltpu.with_memory_space_constraint(a, pltpu.HBM)


def _gather_start(shards, extra):
    n, ne = len(shards), len(extra)
    nl = shards[0].shape[0]
    lands = [lax.empty((4,) + sh.shape[1:], sh.dtype) for _ in range(nl) for sh in shards]
    lands += [lax.empty((4,) + e.shape, e.dtype) for e in extra]

    def body(*refs):
        srcs, ex = refs[:n], refs[n:n + ne]
        land, land_ex = refs[n + ne:n + ne + n * nl], refs[n + ne + n * nl:n + 2 * ne + n * nl]
        send, recv, token = refs[n + 2 * ne + n * nl:n + 2 * ne + n * nl + 3]
        x, y, c, chips = _where_am_i()
        me = 2 * x + y
        for k in range(ne):
            for cx, cy in chips:
                pltpu.make_async_remote_copy(
                    src_ref=ex[k], dst_ref=land_ex[k].at[me], send_sem=send.at[nl], recv_sem=recv.at[nl],
                    device_id=(cx, cy, c), device_id_type=MESH).start()
        for l in range(nl):
            for k in range(n):
                for cx, cy in chips:
                    pltpu.make_async_remote_copy(
                        src_ref=srcs[k].at[l], dst_ref=land[l * n + k].at[me],
                        send_sem=send.at[l], recv_sem=recv.at[l],
                        device_id=(cx, cy, c), device_id_type=MESH).start()
        token[...] = jnp.zeros_like(token)

    ins = [_in_hbm(a) for a in list(shards) + list(extra) + lands]
    res = pl.pallas_call(
        body, name="gather_start",
        out_shape=(pltpu.SemaphoreType.DMA((nl + 1,)), pltpu.SemaphoreType.DMA((nl + 1,)),
                   jax.ShapeDtypeStruct((8, LANES), F32)) + tuple(pltpu.HBM(a.shape, a.dtype) for a in ins),
        in_specs=[_HBM] * len(ins),
        out_specs=(_SEM, _SEM, pl.BlockSpec(memory_space=pltpu.VMEM)) + (_HBM,) * len(ins),
        input_output_aliases={i: 3 + i for i in range(len(ins))},
        compiler_params=pltpu.CompilerParams(has_side_effects=_FLOWS),
    )(*ins)
    send, recv, token = res[:3]
    thru, thru_ex = res[3:3 + n], res[3 + n:3 + n + ne]
    lands = res[3 + n + ne:]
    return (send, recv, token, list(thru), [list(lands[l * n:(l + 1) * n]) for l in range(nl)],
            list(thru_ex), list(lands[nl * n:]))


def _gather_wait(name, send, recv, l, thru, land, after, whole_sem=None):
    n = len(thru)
    sem = l if whole_sem is None else whole_sem

    def body(*refs):
        srcs, lands = refs[:n], refs[n:2 * n]
        send_sem, recv_sem = refs[2 * n:2 * n + 2]
        x, y, c, chips = _where_am_i()
        for k in range(n):
            for cx, cy in chips:
                copy = pltpu.make_async_remote_copy(
                    src_ref=srcs[k].at[l] if whole_sem is None else srcs[k], dst_ref=lands[k].at[2 * cx + cy],
                    send_sem=send_sem.at[sem], recv_sem=recv_sem.at[sem],
                    device_id=(cx, cy, c), device_id_type=MESH)
                copy.wait_send()
                copy.wait_recv()

    ins = list(thru) + list(land)
    res = pl.pallas_call(
        body, name=name,
        out_shape=tuple(pltpu.HBM(a.shape, a.dtype) for a in ins),
        in_specs=[_HBM] * len(ins) + [_SEM, _SEM, _ANY],
        out_specs=(_HBM,) * len(ins),
        input_output_aliases={i: i for i in range(len(ins))},
        compiler_params=pltpu.CompilerParams(has_side_effects=_FLOWS),
    )(*ins, send, recv, after)
    return list(res[:n]), list(res[n:])


def _pair_exchange(grads):
    n = len(grads)

    def body(*refs):
        ins, got = refs[:n], refs[n:2 * n]
        send, recv = refs[2 * n:]
        x, y, c, _ = _where_am_i()
        sib = (x, y, 1 - c)
        for k in range(n):
            h = ins[k].shape[2] // 2
            for ch in range(4):
                for l in range(ins[k].shape[1]):
                    pltpu.make_async_remote_copy(
                        src_ref=ins[k].at[ch, l, pl.ds((1 - c) * h, h)], dst_ref=got[k].at[ch, l],
                        send_sem=send.at[k], recv_sem=recv.at[k], device_id=sib, device_id_type=MESH).start()
        for k in range(n):
            pltpu.make_async_remote_copy(
                src_ref=got[k], dst_ref=got[k], send_sem=send.at[k], recv_sem=recv.at[k],
                device_id=sib, device_id_type=MESH).wait()

    return pl.pallas_call(
        body, name="reduce_pair",
        out_shape=[jax.ShapeDtypeStruct(g.shape[:2] + (g.shape[2] // 2, g.shape[3]), g.dtype) for g in grads],
        in_specs=[_ANY] * n, out_specs=[_ANY] * n,
        scratch_shapes=[pltpu.SemaphoreType.DMA((n,))] * 2,
    )(*grads)


def _half_tiling(r):
    h = r // 2
    tr = _tile(h, (512, 256, 352, 176, 128, 64, 32))
    return h, tr, h // tr


def _pair_add(full, got, c):
    _, ls, r, cols = full.shape
    h, tr, nh = _half_tiling(r)

    def kern(c_ref, a_ref, b_ref, o32_ref, o16_ref):
        tot = a_ref[...] + b_ref[...].astype(F32)
        o32_ref[...] = tot
        o16_ref[...] = tot.astype(o16_ref.dtype)

    blk = (None, None, tr, cols)
    mine = pl.BlockSpec(blk, lambda ch, l, i, cr: (ch, l, cr[0] * nh + i, 0))
    same = pl.BlockSpec(blk, lambda ch, l, i, cr: (ch, l, i, 0))
    return pl.pallas_call(
        kern, name="pair_add",
        out_shape=[jax.ShapeDtypeStruct(got.shape, F32), jax.ShapeDtypeStruct(got.shape, got.dtype)],
        grid_spec=pltpu.PrefetchScalarGridSpec(
            num_scalar_prefetch=1, grid=(4, ls, nh), in_specs=[mine, same], out_specs=[same, same]),
        compiler_params=_params(("parallel", "parallel", "parallel")),
    )(c.reshape(1), full, got)


def _scatter_start(name, parts):
    n = len(parts)
    lands = [lax.empty(p.shape, p.dtype) for p in parts]

    def body(*refs):
        srcs, land = refs[:n], refs[n:2 * n]
        send, recv, token = refs[2 * n:2 * n + 3]
        x, y, c, chips = _where_am_i()
        me = 2 * x + y
        for k in range(n):
            for cx, cy in chips:
                pltpu.make_async_remote_copy(
                    src_ref=srcs[k].at[2 * cx + cy], dst_ref=land[k].at[me], send_sem=send, recv_sem=recv,
                    device_id=(cx, cy, c), device_id_type=MESH).start()
        token[...] = jnp.zeros_like(token)

    ins = [_in_hbm(a) for a in list(parts) + lands]
    res = pl.pallas_call(
        body, name=name,
        out_shape=(pltpu.SemaphoreType.DMA(()), pltpu.SemaphoreType.DMA(()),
                   jax.ShapeDtypeStruct((8, LANES), F32)) + tuple(pltpu.HBM(a.shape, a.dtype) for a in ins),
        in_specs=[_HBM] * len(ins),
        out_specs=(_SEM, _SEM, pl.BlockSpec(memory_space=pltpu.VMEM)) + (_HBM,) * len(ins),
        input_output_aliases={i: 3 + i for i in range(len(ins))},
        compiler_params=pltpu.CompilerParams(has_side_effects=_FLOWS),
    )(*ins)
    return res[0], res[1], res[2], list(res[3:3 + n]), list(res[3 + n:])


def _scatter_wait(name, send, recv, thru, lands, after):
    n = len(thru)

    def body(*refs):
        srcs, land = refs[:n], refs[n:2 * n]
        send_sem, recv_sem = refs[2 * n:2 * n + 2]
        x, y, c, chips = _where_am_i()
        for k in range(n):
            for cx, cy in chips:
                copy = pltpu.make_async_remote_copy(
                    src_ref=srcs[k].at[2 * cx + cy], dst_ref=land[k].at[2 * cx + cy],
                    send_sem=send_sem, recv_sem=recv_sem, device_id=(cx, cy, c), device_id_type=MESH)
                copy.wait_send()
                copy.wait_recv()

    ins = list(thru) + list(lands)
    res = pl.pallas_call(
        body, name=name,
        out_shape=tuple(pltpu.HBM(a.shape, a.dtype) for a in ins),
        in_specs=[_HBM] * len(ins) + [_SEM, _SEM, _ANY],
        out_specs=(_HBM,) * len(ins),
        input_output_aliases={i: i for i in range(len(ins))},
        compiler_params=pltpu.CompilerParams(has_side_effects=_FLOWS),
    )(*ins, send, recv, after)
    return list(res[n:])


def _chip_add(landed, mine, me, c, into, l0):
    _, ls, h, cols = landed.shape
    _, tr, nh = _half_tiling(2 * h)

    def kern(me_ref, c_ref, land_ref, own_ref, into_ref, o_ref):
        own = own_ref[...]
        tot = None
        for j in range(4):
            term = jnp.where(me_ref[0] == j, own, land_ref[j].astype(F32))
            tot = term if tot is None else tot + term
        o_ref[...] = tot

    return pl.pallas_call(
        kern, name="chip_add",
        out_shape=jax.ShapeDtypeStruct(into.shape, F32),
        grid_spec=pltpu.PrefetchScalarGridSpec(
            num_scalar_prefetch=2, grid=(ls, nh),
            in_specs=[pl.BlockSpec((4, None, tr, cols), lambda l, i, mr, cr: (0, l, i, 0)),
                      pl.BlockSpec((None, None, tr, cols), lambda l, i, mr, cr: (mr[0], l, i, 0)),
                      pl.BlockSpec(memory_space=pl.ANY)],
            out_specs=pl.BlockSpec((None, tr, cols), lambda l, i, mr, cr: (l0 + l, cr[0] * nh + i, 0))),
        input_output_aliases={4: 0},
        compiler_params=_params(("parallel", "parallel")),
    )(me.reshape(1), c.reshape(1), landed, mine, into)


def _pair_join(sums):
    n = len(sums)

    def body(*refs):
        bufs = refs[n:2 * n]
        send, recv = refs[2 * n:]
        x, y, c, _ = _where_am_i()
        sib = (x, y, 1 - c)
        for k in range(n):
            h = bufs[k].shape[1] // 2
            for l in range(bufs[k].shape[0]):
                piece = bufs[k].at[l, pl.ds(c * h, h)]
                pltpu.make_async_remote_copy(
                    src_ref=piece, dst_ref=piece, send_sem=send.at[k], recv_sem=recv.at[k],
                    device_id=sib, device_id_type=MESH).start()
        for k in range(n):
            h = bufs[k].shape[1] // 2
            theirs = bufs[k].at[:, pl.ds((1 - c) * h, h)]
            pltpu.make_async_remote_copy(
                src_ref=theirs, dst_ref=theirs, send_sem=send.at[k], recv_sem=recv.at[k],
                device_id=sib, device_id_type=MESH).wait()

    return pl.pallas_call(
        body, name="reduce_join",
        out_shape=[jax.ShapeDtypeStruct(s.shape, s.dtype) for s in sums],
        in_specs=[_ANY] * n, out_specs=[_ANY] * n,
        input_output_aliases={k: k for k in range(n)},
        scratch_shapes=[pltpu.SemaphoreType.DMA((n,))] * 2,
    )(*sums)


def _allreduce_small(buf):
    R = buf.shape[0]

    def body(in_ref, out_ref, land, send, recv):
        x, y, c, _ = _where_am_i()
        me = 4 * x + 2 * y + c
        land[me] = in_ref[...]
        cps = []
        for k in range(1, 8):
            px, py, pc = x ^ (k >> 2), y ^ ((k >> 1) & 1), c ^ (k & 1)
            cps.append(pltpu.make_async_remote_copy(
                src_ref=in_ref, dst_ref=land.at[me], send_sem=send.at[k - 1], recv_sem=recv.at[k - 1],
                device_id=(px, py, pc), device_id_type=MESH))
        for cp in cps:
            cp.start()
        for k in range(1, 8):
            px, py, pc = x ^ (k >> 2), y ^ ((k >> 1) & 1), c ^ (k & 1)
            slot = land.at[4 * px + 2 * py + pc]
            pltpu.make_async_remote_copy(
                src_ref=slot, dst_ref=slot, send_sem=send.at[k - 1], recv_sem=recv.at[k - 1],
                device_id=(px, py, pc), device_id_type=MESH).wait_recv()
        for cp in cps:
            cp.wait_send()
        tot = land[0]
        for d in range(1, 8):
            tot = tot + land[d]
        out_ref[...] = tot

    vm = pl.BlockSpec(memory_space=pltpu.VMEM)
    return pl.pallas_call(
        body, name="allreduce_small",
        out_shape=jax.ShapeDtypeStruct(buf.shape, F32),
        in_specs=[vm], out_specs=vm,
        scratch_shapes=[pltpu.VMEM((8, R, LANES), F32), pltpu.SemaphoreType.DMA((7,)),
                        pltpu.SemaphoreType.DMA((7,))],
    )(buf)


def _rope_tables(T):
    pos = (jnp.arange(T) - FRONT).astype(F32)
    lane = jnp.arange(LANES)
    inv_a = ROPE_THETA ** (-(2 * ((lane % 64) % 32)).astype(F32) / SWA_HEAD_DIM)
    ang_a = pos[:, None] * inv_a[None, :]
    cosa, sina = jnp.cos(ang_a), jnp.sin(ang_a)
    inv_m = ROPE_THETA ** (-(2 * ((lane - KR_LANE) % 16)).astype(F32) / MLA_ROPE_DIM)
    ang_m = pos[:, None] * inv_m[None, :]
    on = ((lane >= KR_LANE) & (lane < KR_LANE + MLA_ROPE_DIM))[None, :]
    cosm = jnp.where(on, jnp.cos(ang_m), 1.0)
    sinm = jnp.where(on, jnp.sin(ang_m), 0.0)
    return cosa, sina, cosm, sinm


def _cols_from_chips(g):
    return jnp.concatenate(g, axis=-1)


def _rows_from_chips(g):
    return jnp.concatenate(g, axis=-2)


def _cols_to_chips(w):
    c = w.shape[-1] // 4
    return jnp.stack([w[..., j * c:(j + 1) * c] for j in range(4)])


def _rows_to_chips(w):
    L, r4, c = w.shape
    return jnp.moveaxis(w.reshape(L, 4, r4 // 4, c), 1, 0)


def _layer_layouts(w_in, w_qup, w_kvup, w_o, w_gate, w_up, w_down):
    zpad = lambda n: jnp.zeros((D_MODEL, n), w_in.dtype)
    w_in = jnp.concatenate([w_in[:, :C_KR], zpad(KR_LANE), w_in[:, C_KR:IN_W],
                            zpad(LANES - KR_LANE - MLA_ROPE_DIM)], axis=-1)
    w_qup = w_qup.reshape(MLA_Q_RANK, MLA_HEADS, MLA_QK_DIM)
    w_qup = jnp.pad(w_qup, ((0, 0), (0, 0), (0, LANES - MLA_QK_DIM))).reshape(MLA_Q_RANK, SLOT_W)
    return w_in, w_qup, w_kvup, w_o, w_gate, w_up, w_down


def _local_step(x2, target, meta_full, layer_weights, p, early):
    T = BLOCK + x2.shape[0]
    L = DEPTH
    attn_norm, q_norm, kv_norm, sinks = p["attn_norm"], p["q_norm"], p["kv_norm"], p["sinks"]
    out_norm_swa, out_norm_mla, ffn_norm, final_norm = (
        p["out_norm_swa"], p["out_norm_mla"], p["ffn_norm"], p["final_norm"])

    cosa, sina, cosm, sinm = _rope_tables(T)
    no_sink = jnp.full((MLA_HEADS,), NEG, F32)
    scale_a, scale_b = SWA_HEAD_DIM ** -0.5, MLA_QK_DIM ** -0.5
    row = lambda v: v.reshape(1, -1)

    h = jnp.concatenate([jnp.zeros((FRONT, D_MODEL), F32), meta_full, x2], axis=0)
    saved = []
    weights = []
    for l in range(L):
        weights.append(_layer_layouts(*layer_weights(l, h)))
        W_in, W_qup, W_kvup, W_o, W_gate, W_up, W_down = weights[l]
        u, proj = _norm_proj("in_proj", h, row(attn_norm[l]), [W_in], False)
        qa, ka, va, qn, cn, kr = _prep1(proj, row(q_norm[l]), row(kv_norm[l]), cosa, sina, cosm, sinm)
        qb = _mm("q_up", qn, W_qup, "nn")
        kvb = _mm("kv_up", cn, W_kvup, "nn")
        qs, ks, vs = _prep2(qb, kvb, kr, cosm, sinm)
        oa, lse_a = _attn_fwd("swa_fwd", qa, ka, va, sinks[l], 4, False, scale_a, 4)
        ob, lse_b = _attn_fwd("mla_fwd", qs, ks, vs, no_sink, 1, True, scale_b, 1)
        mix = _merge_fwd(oa, ob, row(out_norm_swa[l]), row(out_norm_mla[l]))
        h1 = _mm("o_proj", mix, W_o, "nn", res=h)
        if callable(W_gate):
            W_gate, W_up, W_down = W_gate(h1), W_up(h1), W_down(h1)
            weights[l] = (W_in, W_qup, W_kvup, W_o, W_gate, W_up, W_down)
        u2, a, b, hm = _norm_proj("ffn_in", h1, row(ffn_norm[l]), [W_gate, W_up], True)
        h2 = _mm("down_proj", hm, W_down, "nn", res=h1)
        saved.append((h, u, proj, qa, ka, va, qn, cn, qs, ks, vs, oa, lse_a, ob, lse_b, mix, h1, u2, a, b, hm))
        h = h2

    dh, d_final, loss_row = _loss_head(h, target, row(final_norm))

    gw = {k: [None] * L for k in ("in", "qup", "kvup", "o", "gate", "up", "down")}
    gs = {k: [None] * L for k in ("attn", "qn", "kvn", "sink", "ga", "gb", "ffn")}

    def natural_grads(layers):
        st = lambda k: jnp.stack(gw[k][layers]) if gw[k][layers][0] is not None else None
        d_in, d_qup = st("in"), st("qup")
        if d_in is not None:
            d_in = jnp.concatenate([d_in[..., :C_KR], d_in[..., C_KR + KR_LANE:C_KR + KR_LANE + MLA_ROPE_DIM]],
                                   axis=-1)
            d_qup = d_qup.reshape(d_qup.shape[0], MLA_Q_RANK, MLA_HEADS, LANES)[..., :MLA_QK_DIM]
            d_qup = d_qup.reshape(d_qup.shape[0], MLA_Q_RANK, -1)
        return [d_in, d_qup, st("kvup"), st("o"), st("gate"), st("up"), st("down")]

    for l in reversed(range(L)):
        (h0, u, proj, qa, ka, va, qn, cn, qs, ks, vs, oa, lse_a, ob, lse_b, mix, h1, u2, a, b, hm) = saved[l]
        W_in, W_qup, W_kvup, W_o, W_gate, W_up, W_down = weights[l]
        if l == 0:
            dh = dh + early("late", natural_grads(slice(1, L)))[0, 0]
        gw["down"][l] = _mm("down_dw", hm, dh, "tn")
        da, db = _ffn_mid_bwd(dh, W_down, a, b)
        gw["gate"][l] = _mm("gate_dw", u2, da, "tn")
        gw["up"][l] = _mm("up_dw", u2, db, "tn")
        du2 = _mm("gate_dx", da, W_gate, "nt")
        du2 = _mm("up_dx", db, W_up, "nt", res=du2)
        gain = row(ffn_norm[l])
        if l == 0:
            gain = gain + early("ffn", natural_grads(slice(0, 1))[4:])[0:1, 0:1]
        dh1, gs["ffn"][l] = _rmsnorm_bwd("ffn_norm_bwd", h1, gain, du2, dh)
        gw["o"][l] = _mm("o_dw", mix, dh1, "tn")
        dmix = _mm("o_dx", dh1, W_o, "nt")
        sink_slots = jnp.repeat(sinks[l], LANES).reshape(1, SLOT_W)
        doa, dla, dob, dlb, gs["ga"][l], gs["gb"][l], dsink = _merge_bwd(
            dmix, oa, ob, lse_a, row(out_norm_swa[l]), row(out_norm_mla[l]), sink_slots)
        gs["sink"][l] = dsink.reshape(SWA_HEADS, LANES)[:, 0]
        dqs, dks, dvs = _attn_bwd("mla_bwd", qs, ks, vs, dob, lse_b, dlb, 1, True, scale_b)
        dqa, dka, dva = _attn_bwd("swa_bwd", qa, ka, va, doa, lse_a, dla, 4, False, scale_a)
        dqb, dkvb, dkr = _prep2_bwd(dqs, dks, dvs, cosm, sinm)
        gw["qup"][l] = _mm("q_up_dw", qn, dqb, "tn")
        gw["kvup"][l] = _mm("kv_up_dw", cn, dkvb, "tn")
        dqn = _mm("q_up_dx", dqb, W_qup, "nt")
        dcn = _mm("kv_up_dx", dkvb, W_kvup, "nt")
        dproj, gs["qn"][l], gs["kvn"][l] = _prep1_bwd(
            proj, dqa, dka, dva, dqn, dcn, dkr, cosa, sina, row(q_norm[l]), row(kv_norm[l]))
        gw["in"][l] = _mm("in_dw", u, dproj, "tn")
        du = _mm("in_dx", dproj, W_in, "nt")
        dh, gs["attn"][l] = _rmsnorm_bwd("attn_norm_bwd", h0, row(attn_norm[l]), du, dh1)

    return loss_row, dh, natural_grads(slice(0, 1))[:4], gs, d_final


def kernel(x, meta_tokens, attn_norm, w_in, q_norm, w_q_up, kv_norm, w_kv_up, sinks, out_norm_swa, out_norm_mla, w_o, ffn_norm, w_gate, w_up, w_down, final_norm, loss_target, m_meta_tokens, m_attn_norm, m_w_in, m_q_norm, m_w_q_up, m_kv_norm, m_w_kv_up, m_sinks, m_out_norm_swa, m_out_norm_mla, m_w_o, m_ffn_norm, m_w_gate, m_w_up, m_w_down, m_final_norm, v_meta_tokens, v_attn_norm, v_w_in, v_q_norm, v_w_q_up, v_kv_norm, v_w_kv_up, v_sinks, v_out_norm_swa, v_out_norm_mla, v_w_o, v_ffn_norm, v_w_gate, v_w_up, v_w_down, v_final_norm):
    assert x.shape[0] == 1 and x.shape[1] % BLOCK == 0
    big = [w_in, w_q_up, w_kv_up, w_o, w_gate, w_up, w_down]

    c_idx = lax.axis_index("c").astype(jnp.int32)
    chip = (2 * lax.axis_index("x") + lax.axis_index("y")).astype(jnp.int32)
    w16 = [w.astype(BF16) for w in big]
    join = [_cols_from_chips, _cols_from_chips, _cols_from_chips, _rows_from_chips, _cols_from_chips,
            _cols_from_chips, _rows_from_chips]

    def whole(ks, own, landed):
        return [join[k]([jnp.where(chip == j, o, g[j]) for j in range(4)]) for k, o, g in zip(ks, own, landed)]

    head, ffn = [0, 1, 2, 3], [4, 5, 6]
    first = _gather_weights([w16[k][0] for k in head], meta_tokens)
    meta_full = jnp.concatenate([jnp.where(chip == j, meta_tokens, first[-1][j]) for j in range(4)], axis=-1)
    send, recv, token, thru, lands, thru0, lands0 = _gather_start([w[1:] for w in w16], [w16[k][0] for k in ffn])
    state = {"thru": thru}

    def ffn_first(h1):
        if "ffn" not in state:
            _, landed = _gather_wait("gather_wait_ffn", send, recv, None, thru0, lands0, h1, whole_sem=DEPTH - 1)
            state["ffn"] = whole(ffn, [w16[k][0] for k in ffn], landed)
        return state["ffn"]

    def layer_weights(l, h):
        if l == 0:
            return whole(head, [w16[k][0] for k in head], first[:-1]) + [
                (lambda h1, i=i: ffn_first(h1)[i]) for i in range(3)]
        state["thru"], landed = _gather_wait("gather_wait%d" % l, send, recv, l - 1, state["thru"], lands[l - 1], h)
        return whole(head + ffn, [w[l] for w in w16], landed)

    small_p = dict(attn_norm=attn_norm, q_norm=q_norm, kv_norm=kv_norm, sinks=sinks, out_norm_swa=out_norm_swa,
                   out_norm_mla=out_norm_mla, ffn_norm=ffn_norm, final_norm=final_norm)
    split = [_cols_to_chips, _cols_to_chips, _cols_to_chips, _rows_to_chips, _cols_to_chips, _cols_to_chips,
             _rows_to_chips]
    rounds = []

    def reduce_start(name, ks, l0, grads):
        full = [split[k](d) for k, d in zip(ks, grads)]
        got = _pair_exchange([f.astype(BF16) for f in full])
        sums = [_pair_add(f, g, c_idx) for f, g in zip(full, got)]
        send, recv, tok, thru, lands = _scatter_start("scatter_start_" + name, [s16 for _, s16 in sums])
        rounds.append((name, ks, l0, [s32 for s32, _ in sums], send, recv, thru, lands))
        return tok

    starts = {"late": lambda g: reduce_start("late", list(range(7)), 1, g),
              "ffn": lambda g: reduce_start("ffn", [4, 5, 6], 0, g)}
    loss_row, dh, d_first, gs, d_final = _local_step(
        x[0] + token[0, 0], loss_target[0], meta_full, layer_weights, small_p, lambda which, g: starts[which](g))
    grad_x = dh[BLOCK:][None]
    reduce_start("first", [0, 1, 2, 3], 0, d_first)

    g_big = [lax.empty(w.shape, F32) for w in big]
    for name, ks, l0, s32, send, recv, thru, lands in rounds:
        landed = _scatter_wait("scatter_wait_" + name, send, recv, thru, lands, dh)
        for k, t, m32 in zip(ks, landed, s32):
            g_big[k] = _chip_add(t, m32, chip, c_idx, g_big[k], l0)
    g_big = _pair_join(g_big)

    small = [jnp.stack(gs["attn"]).reshape(-1), jnp.stack(gs["qn"]).reshape(-1), jnp.stack(gs["kvn"]).reshape(-1),
             jnp.stack(gs["sink"]).reshape(-1), jnp.stack(gs["ga"]).reshape(-1), jnp.stack(gs["gb"]).reshape(-1),
             jnp.stack(gs["ffn"]).reshape(-1), d_final.reshape(-1)]
    sizes = [s.shape[0] for s in small]
    flat = jnp.concatenate(small + [dh[FRONT:BLOCK].reshape(-1), loss_row[0, :1]])
    n_flat = flat.shape[0]
    rows_needed = -(-n_flat // (8 * LANES)) * 8
    flat = jnp.pad(flat, (0, rows_needed * LANES - n_flat)).reshape(rows_needed, LANES)
    tot = _allreduce_small(flat).reshape(-1)
    n_small = sum(sizes)
    loss = tot[n_small + N_META * D_MODEL]
    g_meta_full = tot[n_small:n_small + N_META * D_MODEL].reshape(N_META, D_MODEL)
    g_meta_mine = lax.dynamic_slice_in_dim(g_meta_full, chip * (D_MODEL // 4), D_MODEL // 4, axis=1)

    small_w = [attn_norm, q_norm, kv_norm, sinks, out_norm_swa, out_norm_mla, ffn_norm, final_norm]
    small_m = [m_attn_norm, m_q_norm, m_kv_norm, m_sinks, m_out_norm_swa, m_out_norm_mla, m_ffn_norm, m_final_norm]
    small_v = [v_attn_norm, v_q_norm, v_kv_norm, v_sinks, v_out_norm_swa, v_out_norm_mla, v_ffn_norm, v_final_norm]
    n_rows = -(-n_small // (8 * LANES)) * 8

    def pack(arrs):
        f = jnp.concatenate([a.reshape(-1) for a in arrs])
        return jnp.pad(f, (0, n_rows * LANES - n_small), constant_values=1.0).reshape(n_rows, LANES)

    g_small_pack = jnp.pad(tot[:n_small], (0, n_rows * LANES - n_small)).reshape(n_rows, LANES)
    upd_small = _adamw("adam_small", pack(small_w), g_small_pack, pack(small_m), pack(small_v))

    def unpack(p):
        f = p.reshape(-1)
        out, off = [], 0
        for a, n in zip(small_w, sizes):
            out.append(f[off:off + n].reshape(a.shape))
            off += n
        return out

    g_small = unpack(g_small_pack)
    d_small, m_small, v_small = [unpack(p) for p in upd_small]
    d_meta, nm_meta, nv_meta = _adamw("adam_meta", meta_tokens, g_meta_mine, m_meta_tokens, v_meta_tokens)

    big_m = [m_w_in, m_w_q_up, m_w_kv_up, m_w_o, m_w_gate, m_w_up, m_w_down]
    big_v = [v_w_in, v_w_q_up, v_w_kv_up, v_w_o, v_w_gate, v_w_up, v_w_down]
    upd_big = [_adamw("adam_big", w, g, m, v) for w, g, m, v in zip(big, g_big, big_m, big_v)]

    names = ["meta_tokens", "attn_norm", "w_in", "q_norm", "w_q_up", "kv_norm", "w_kv_up", "sinks",
             "out_norm_swa", "out_norm_mla", "w_o", "ffn_norm", "w_gate", "w_up", "w_down", "final_norm"]
    small_idx = {"attn_norm": 0, "q_norm": 1, "kv_norm": 2, "sinks": 3, "out_norm_swa": 4,
                 "out_norm_mla": 5, "ffn_norm": 6, "final_norm": 7}
    big_idx = {"w_in": 0, "w_q_up": 1, "w_kv_up": 2, "w_o": 3, "w_gate": 4, "w_up": 5, "w_down": 6}
    grads, deltas, new_m, new_v = [], [], [], []
    for nme in names:
        if nme == "meta_tokens":
            quad = (g_meta_mine, d_meta, nm_meta, nv_meta)
        elif nme in small_idx:
            i = small_idx[nme]
            quad = (g_small[i], d_small[i], m_small[i], v_small[i])
        else:
            i = big_idx[nme]
            quad = (g_big[i], *upd_big[i])
        grads.append(quad[0]); deltas.append(quad[1]); new_m.append(quad[2]); new_v.append(quad[3])
    return (loss, grad_x, *grads, *deltas, *new_m, *new_v)
```
